```python
import math
import jax, jax.numpy as jnp
from jax import lax
import numpy as np

D_MODEL = 2048
BATCH = 8
SEQ = 2048
DEPTH = 2

N_MIXERS = 2
N_POOL_LAYERS = (DEPTH + 1) // 2
N_SSM_LAYERS = DEPTH // 2

ALPHA = (2.0 * DEPTH) ** 0.25
BETA = (8.0 * DEPTH) ** -0.25
LN_EPS = 1e-5

POOL_WINDOWS = (2, 4, 8, 16)
N_POOL_GROUPS = len(POOL_WINDOWS)
POOL_GROUP_DIM = D_MODEL // N_POOL_GROUPS

SSM_EXPAND = 2
D_INNER = SSM_EXPAND * D_MODEL
SSM_HEAD_DIM = 64
SSM_HEADS = D_INNER // SSM_HEAD_DIM
SSM_GROUPS = 8
HEADS_PER_GROUP = SSM_HEADS // SSM_GROUPS
D_STATE = 128
CONV_WIDTH = 4
CHUNK = 128
CONV_DIM = D_INNER + 2 * SSM_GROUPS * D_STATE
D_IN_PROJ = D_INNER + CONV_DIM + SSM_HEADS
RMS_EPS = 1e-5

D_FF = 4 * D_MODEL

PLE_DIM = 256

kernel_name = "pool_ssd_interleaved_deepnorm_hybrid"


def layer_norm(x, g, b):
    xf = x.astype(jnp.float32)
    mu = jnp.mean(xf, axis=-1, keepdims=True)
    var = jnp.mean(jnp.square(xf - mu), axis=-1, keepdims=True)
    y = (xf - mu) * lax.rsqrt(var + LN_EPS) * g.astype(jnp.float32) + b.astype(jnp.float32)
    return y.astype(x.dtype)


def rms_norm(x, g):
    xf = x.astype(jnp.float32)
    y = xf * lax.rsqrt(jnp.mean(jnp.square(xf), axis=-1, keepdims=True) + RMS_EPS)
    return y * g.astype(jnp.float32)


def pool_mixer(x, w, scale):
    bsz, seq, _ = x.shape
    xf = x.astype(jnp.float32)
    cs = jnp.cumsum(xf, axis=1)
    pos = jnp.arange(seq)
    outs = []
    for g, win in enumerate(POOL_WINDOWS):
        sl = slice(g * POOL_GROUP_DIM, (g + 1) * POOL_GROUP_DIM)
        c = cs[..., sl]
        c_prev = jnp.pad(c, ((0, 0), (win, 0), (0, 0)))[:, :seq]
        cnt = jnp.minimum(pos + 1, win).astype(jnp.float32)[:, None]
        outs.append((c - c_prev) / cnt - xf[..., sl])
    pooled = jnp.stack(outs, axis=2).astype(x.dtype)
    y = jnp.einsum('bsgc,gcd->bsgd', pooled, w).reshape(bsz, seq, D_MODEL)
    return y * scale


def causal_depthwise_conv(u, w, b):
    seq = u.shape[1]
    up = jnp.pad(u, ((0, 0), (CONV_WIDTH - 1, 0), (0, 0)))
    out = b
    for k in range(CONV_WIDTH):
        out = out + up[:, k:k + seq] * w[k]
    return out


def ssd_mixer(x, in_w, conv_w, conv_b, dt_bias, a_log, d_skip, norm_w, out_w):
    bsz, seq, _ = x.shape
    nc = seq // CHUNK
    zxbcdt = x @ in_w
    z = zxbcdt[..., :D_INNER]
    xbc = zxbcdt[..., D_INNER:D_INNER + CONV_DIM]
    dt = zxbcdt[..., D_INNER + CONV_DIM:]
    xbc = jax.nn.silu(causal_depthwise_conv(xbc, conv_w, conv_b))
    xs = xbc[..., :D_INNER]
    bm = xbc[..., D_INNER:D_INNER + SSM_GROUPS * D_STATE]
    cm = xbc[..., D_INNER + SSM_GROUPS * D_STATE:]

    dt = jax.nn.softplus(dt.astype(jnp.float32) + dt_bias.astype(jnp.float32))
    a = -jnp.exp(a_log.astype(jnp.float32)).reshape(SSM_GROUPS, HEADS_PER_GROUP)

    xs = xs.astype(jnp.float32).reshape(bsz, nc, CHUNK, SSM_GROUPS, HEADS_PER_GROUP, SSM_HEAD_DIM)
    bm = bm.astype(jnp.float32).reshape(bsz, nc, CHUNK, SSM_GROUPS, D_STATE)
    cm = cm.astype(jnp.float32).reshape(bsz, nc, CHUNK, SSM_GROUPS, D_STATE)
    dt = dt.reshape(bsz, nc, CHUNK, SSM_GROUPS, HEADS_PER_GROUP)

    da = jnp.transpose(dt * a, (0, 3, 4, 1, 2))
    a_cs = jnp.cumsum(da, axis=-1)
    xdt = xs * dt[..., None]

    causal = jnp.tril(jnp.ones((CHUNK, CHUNK), dtype=bool))
    seg = a_cs[..., :, None] - a_cs[..., None, :]
    lmat = jnp.exp(jnp.where(causal, seg, -jnp.inf))
    cb = jnp.einsum('bclgn,bcsgn->bgcls', cm, bm)
    mmat = cb[:, :, None] * lmat
    y_diag = jnp.einsum('bghcls,bcsghp->bclghp', mmat, xdt)

    decay_states = jnp.exp(a_cs[..., -1:] - a_cs)
    xdt_dec = xdt * jnp.transpose(decay_states, (0, 3, 4, 1, 2))[..., None]
    states = jnp.einsum('bclgn,bclghp->bcghpn', bm, xdt_dec)
    chunk_decay = jnp.exp(a_cs[..., -1])

    def step(h, inp):
        s, d = inp
        return d[..., None, None] * h + s, h

    h0 = jnp.zeros((bsz, SSM_GROUPS, HEADS_PER_GROUP, SSM_HEAD_DIM, D_STATE), jnp.float32)
    _, prev = lax.scan(step, h0, (jnp.moveaxis(states, 1, 0), jnp.moveaxis(chunk_decay, 3, 0)))
    prev = jnp.moveaxis(prev, 0, 1)

    state_decay = jnp.transpose(jnp.exp(a_cs), (0, 3, 4, 1, 2))
    y_off = jnp.einsum('bclgn,bcghpn->bclghp', cm, prev) * state_decay[..., None]

    dsk = d_skip.astype(jnp.float32).reshape(SSM_GROUPS, HEADS_PER_GROUP)[..., None]
    y = (y_diag + y_off + xs * dsk).reshape(bsz, seq, D_INNER)
    y = rms_norm(y * jax.nn.silu(z.astype(jnp.float32)), norm_w).astype(x.dtype)
    return y @ out_w


def sq_relu_mlp(x, w1, w2):
    h = jax.nn.relu(x @ w1)
    return (h * h) @ w2


def _fwd_setup_inputs(seed: int = 0) -> dict:
    key = jax.random.key(seed)
    ks = iter(jax.random.split(key, 32))
    f32 = jnp.float32

    def nrm(shape, scale):
        return jax.random.normal(next(ks), shape, f32) * scale

    x = nrm((BATCH, SEQ, D_MODEL), 1.0)
    p = nrm((DEPTH, BATCH, SEQ, PLE_DIM), 1.0)

    pool_w = nrm((N_POOL_LAYERS, N_POOL_GROUPS, POOL_GROUP_DIM, POOL_GROUP_DIM), BETA * POOL_GROUP_DIM ** -0.5)
    pool_scale = 1.0 + nrm((N_POOL_LAYERS, D_MODEL), 0.1)

    ssm_in_w = nrm((N_SSM_LAYERS, D_MODEL, D_IN_PROJ), D_MODEL ** -0.5)
    ssm_conv_w = nrm((N_SSM_LAYERS, CONV_WIDTH, CONV_DIM), CONV_WIDTH ** -0.5)
    ssm_conv_b = nrm((N_SSM_LAYERS, CONV_DIM), 0.02)
    dt0 = jnp.exp(jax.random.uniform(next(ks), (N_SSM_LAYERS, SSM_HEADS), f32,
                                     math.log(1e-3), math.log(1e-1)))
    ssm_dt_bias = dt0 + jnp.log(-jnp.expm1(-dt0))
    ssm_a_log = jnp.log(jax.random.uniform(next(ks), (N_SSM_LAYERS, SSM_HEADS), f32, 1.0, 16.0))
    ssm_d = 1.0 + nrm((N_SSM_LAYERS, SSM_HEADS), 0.1)
    ssm_norm_w = 1.0 + nrm((N_SSM_LAYERS, D_INNER), 0.1)
    ssm_out_w = nrm((N_SSM_LAYERS, D_INNER, D_MODEL), BETA * D_INNER ** -0.5)

    mlp_w1 = nrm((DEPTH, D_MODEL, D_FF), D_MODEL ** -0.5)
    mlp_w2 = nrm((DEPTH, D_FF, D_MODEL), BETA * D_FF ** -0.5)

    ln_g = 1.0 + nrm((DEPTH, 2, D_MODEL), 0.1)
    ln_b = nrm((DEPTH, 2, D_MODEL), 0.02)

    ple_w = nrm((DEPTH, PLE_DIM, D_MODEL), PLE_DIM ** -0.5)
    ple_gate_w = nrm((DEPTH, D_MODEL, D_MODEL), D_MODEL ** -0.5)

    return {"x": x, "p": p,
            "pool_w": pool_w, "pool_scale": pool_scale,
            "ssm_in_w": ssm_in_w, "ssm_conv_w": ssm_conv_w, "ssm_conv_b": ssm_conv_b,
            "ssm_dt_bias": ssm_dt_bias, "ssm_a_log": ssm_a_log, "ssm_d": ssm_d,
            "ssm_norm_w": ssm_norm_w, "ssm_out_w": ssm_out_w,
            "mlp_w1": mlp_w1, "mlp_w2": mlp_w2,
            "ln_g": ln_g, "ln_b": ln_b,
            "ple_w": ple_w, "ple_gate_w": ple_gate_w}


def _fwd_reference(x, p, pool_w, pool_scale, ssm_in_w, ssm_conv_w, ssm_conv_b,
              ssm_dt_bias, ssm_a_log, ssm_d, ssm_norm_w, ssm_out_w,
              mlp_w1, mlp_w2, ln_g, ln_b, ple_w, ple_gate_w):
    for i in range(DEPTH):
        j = i // N_MIXERS
        if i % N_MIXERS == 0:
            h = pool_mixer(x, pool_w[j], pool_scale[j])
        else:
            h = ssd_mixer(x, ssm_in_w[j], ssm_conv_w[j], ssm_conv_b[j], ssm_dt_bias[j],
                          ssm_a_log[j], ssm_d[j], ssm_norm_w[j], ssm_out_w[j])
        x = layer_norm(ALPHA * x + h, ln_g[i, 0], ln_b[i, 0])
        h = sq_relu_mlp(x, mlp_w1[i], mlp_w2[i])
        x = layer_norm(ALPHA * x + h, ln_g[i, 1], ln_b[i, 1])
        gate = jax.nn.sigmoid(x @ ple_gate_w[i])
        x = x + gate * (p[i] @ ple_w[i])
    return x


import jax as _jax
import jax.numpy as _jnp

TWIN_FORMAT = 'train_step'
FWD_PARAMS = ['x', 'p', 'pool_w', 'pool_scale', 'ssm_in_w', 'ssm_conv_w', 'ssm_conv_b', 'ssm_dt_bias', 'ssm_a_log', 'ssm_d', 'ssm_norm_w', 'ssm_out_w', 'mlp_w1', 'mlp_w2', 'ln_g', 'ln_b', 'ple_w', 'ple_gate_w']
TWIN_WEIGHTS = ['pool_w', 'pool_scale', 'ssm_in_w', 'ssm_conv_w', 'ssm_conv_b', 'ssm_dt_bias', 'ssm_a_log', 'ssm_d', 'ssm_norm_w', 'ssm_out_w', 'mlp_w1', 'mlp_w2', 'ln_g', 'ln_b', 'ple_w', 'ple_gate_w']
TWIN_DIFF_INPUT = 'x'
TWIN_INPUTS = ['x', 'p', 'pool_w', 'pool_scale', 'ssm_in_w', 'ssm_conv_w', 'ssm_conv_b', 'ssm_dt_bias', 'ssm_a_log', 'ssm_d', 'ssm_norm_w', 'ssm_out_w', 'mlp_w1', 'mlp_w2', 'ln_g', 'ln_b', 'ple_w', 'ple_gate_w', 'loss_target', 'm_pool_w', 'm_pool_scale', 'm_ssm_in_w', 'm_ssm_conv_w', 'm_ssm_conv_b', 'm_ssm_dt_bias', 'm_ssm_a_log', 'm_ssm_d', 'm_ssm_norm_w', 'm_ssm_out_w', 'm_mlp_w1', 'm_mlp_w2', 'm_ln_g', 'm_ln_b', 'm_ple_w', 'm_ple_gate_w', 'v_pool_w', 'v_pool_scale', 'v_ssm_in_w', 'v_ssm_conv_w', 'v_ssm_conv_b', 'v_ssm_dt_bias', 'v_ssm_a_log', 'v_ssm_d', 'v_ssm_norm_w', 'v_ssm_out_w', 'v_mlp_w1', 'v_mlp_w2', 'v_ln_g', 'v_ln_b', 'v_ple_w', 'v_ple_gate_w']
TWIN_OUTPUTS = ['loss', 'grad_x', 'grad_pool_w', 'grad_pool_scale', 'grad_ssm_in_w', 'grad_ssm_conv_w', 'grad_ssm_conv_b', 'grad_ssm_dt_bias', 'grad_ssm_a_log', 'grad_ssm_d', 'grad_ssm_norm_w', 'grad_ssm_out_w', 'grad_mlp_w1', 'grad_mlp_w2', 'grad_ln_g', 'grad_ln_b', 'grad_ple_w', 'grad_ple_gate_w', 'delta_pool_w', 'delta_pool_scale', 'delta_ssm_in_w', 'delta_ssm_conv_w', 'delta_ssm_conv_b', 'delta_ssm_dt_bias', 'delta_ssm_a_log', 'delta_ssm_d', 'delta_ssm_norm_w', 'delta_ssm_out_w', 'delta_mlp_w1', 'delta_mlp_w2', 'delta_ln_g', 'delta_ln_b', 'delta_ple_w', 'delta_ple_gate_w', 'new_m_pool_w', 'new_m_pool_scale', 'new_m_ssm_in_w', 'new_m_ssm_conv_w', 'new_m_ssm_conv_b', 'new_m_ssm_dt_bias', 'new_m_ssm_a_log', 'new_m_ssm_d', 'new_m_ssm_norm_w', 'new_m_ssm_out_w', 'new_m_mlp_w1', 'new_m_mlp_w2', 'new_m_ln_g', 'new_m_ln_b', 'new_m_ple_w', 'new_m_ple_gate_w', 'new_v_pool_w', 'new_v_pool_scale', 'new_v_ssm_in_w', 'new_v_ssm_conv_w', 'new_v_ssm_conv_b', 'new_v_ssm_dt_bias', 'new_v_ssm_a_log', 'new_v_ssm_d', 'new_v_ssm_norm_w', 'new_v_ssm_out_w', 'new_v_mlp_w1', 'new_v_mlp_w2', 'new_v_ln_g', 'new_v_ln_b', 'new_v_ple_w', 'new_v_ple_gate_w']
TWIN_LEAF_KINDS = {'loss': 'loss', 'grad_x': 'grad_x', 'grad_pool_w': 'grad_w', 'grad_pool_scale': 'grad_w', 'grad_ssm_in_w': 'grad_w', 'grad_ssm_conv_w': 'grad_w', 'grad_ssm_conv_b': 'grad_w', 'grad_ssm_dt_bias': 'grad_w', 'grad_ssm_a_log': 'grad_w', 'grad_ssm_d': 'grad_w', 'grad_ssm_norm_w': 'grad_w', 'grad_ssm_out_w': 'grad_w', 'grad_mlp_w1': 'grad_w', 'grad_mlp_w2': 'grad_w', 'grad_ln_g': 'grad_w', 'grad_ln_b': 'grad_w', 'grad_ple_w': 'grad_w', 'grad_ple_gate_w': 'grad_w', 'delta_pool_w': 'delta_w', 'delta_pool_scale': 'delta_w', 'delta_ssm_in_w': 'delta_w', 'delta_ssm_conv_w': 'delta_w', 'delta_ssm_conv_b': 'delta_w', 'delta_ssm_dt_bias': 'delta_w', 'delta_ssm_a_log': 'delta_w', 'delta_ssm_d': 'delta_w', 'delta_ssm_norm_w': 'delta_w', 'delta_ssm_out_w': 'delta_w', 'delta_mlp_w1': 'delta_w', 'delta_mlp_w2': 'delta_w', 'delta_ln_g': 'delta_w', 'delta_ln_b': 'delta_w', 'delta_ple_w': 'delta_w', 'delta_ple_gate_w': 'delta_w', 'new_m_pool_w': 'new_m', 'new_m_pool_scale': 'new_m', 'new_m_ssm_in_w': 'new_m', 'new_m_ssm_conv_w': 'new_m', 'new_m_ssm_conv_b': 'new_m', 'new_m_ssm_dt_bias': 'new_m', 'new_m_ssm_a_log': 'new_m', 'new_m_ssm_d': 'new_m', 'new_m_ssm_norm_w': 'new_m', 'new_m_ssm_out_w': 'new_m', 'new_m_mlp_w1': 'new_m', 'new_m_mlp_w2': 'new_m', 'new_m_ln_g': 'new_m', 'new_m_ln_b': 'new_m', 'new_m_ple_w': 'new_m', 'new_m_ple_gate_w': 'new_m', 'new_v_pool_w': 'new_v', 'new_v_pool_scale': 'new_v', 'new_v_ssm_in_w': 'new_v', 'new_v_ssm_conv_w': 'new_v', 'new_v_ssm_conv_b': 'new_v', 'new_v_ssm_dt_bias': 'new_v', 'new_v_ssm_a_log': 'new_v', 'new_v_ssm_d': 'new_v', 'new_v_ssm_norm_w': 'new_v', 'new_v_ssm_out_w': 'new_v', 'new_v_mlp_w1': 'new_v', 'new_v_mlp_w2': 'new_v', 'new_v_ln_g': 'new_v', 'new_v_ln_b': 'new_v', 'new_v_ple_w': 'new_v', 'new_v_ple_gate_w': 'new_v'}


def _forward(args):
    return _fwd_reference(*[args[k] for k in FWD_PARAMS])


def _output_shape():
    out = _jax.eval_shape(lambda: _forward(_fwd_setup_inputs(0)))
    return out.shape, out.dtype

N_MICROBATCH = 1
ADAM_LR = 0.001
ADAM_B1 = 0.9
ADAM_B2 = 0.999
ADAM_EPS = 1e-08
ADAM_WD = 0.01
ADAM_STEP = 10
PER_EXAMPLE_BATCH_AXIS = {'x': 0, 'p': 1, 'loss_target': 0}
SHARED_INPUTS = []
_WEIGHT_DTYPES = {'pool_w': _jnp.float32, 'pool_scale': _jnp.float32, 'ssm_in_w': _jnp.float32, 'ssm_conv_w': _jnp.float32, 'ssm_conv_b': _jnp.float32, 'ssm_dt_bias': _jnp.float32, 'ssm_a_log': _jnp.float32, 'ssm_d': _jnp.float32, 'ssm_norm_w': _jnp.float32, 'ssm_out_w': _jnp.float32, 'mlp_w1': _jnp.float32, 'mlp_w2': _jnp.float32, 'ln_g': _jnp.float32, 'ln_b': _jnp.float32, 'ple_w': _jnp.float32, 'ple_gate_w': _jnp.float32}
MOMENT_SCALE = {'pool_w': 4.795731e-02, 'pool_scale': 7.132638e-02, 'ssm_in_w': 1.838835e-02, 'ssm_conv_w': 2.576104e-02, 'ssm_conv_b': 6.048379e-02, 'ssm_dt_bias': 5.425287e-02, 'ssm_a_log': 2.937551e-01, 'ssm_d': 1.068438e-01, 'ssm_norm_w': 4.587456e-02, 'ssm_out_w': 1.127824e-01, 'mlp_w1': 2.019196e-02, 'mlp_w2': 2.578465e-01, 'ln_g': 4.377219e+00, 'ln_b': 1.211068e+00, 'ple_w': 9.309043e-02, 'ple_gate_w': 1.249492e-01}


def _to_microbatches(a, axis):
    t = _jnp.moveaxis(a, axis, 0)
    t = t.reshape((N_MICROBATCH, t.shape[0] // N_MICROBATCH) + t.shape[1:])
    return _jnp.moveaxis(t, 1, axis + 1)


def setup_inputs(seed: int = 0) -> dict:
    inp = _fwd_setup_inputs(seed)
    key = _jax.random.fold_in(_jax.random.key(seed), 7919)
    shape, _ = _output_shape()
    out = dict(inp)
    out["loss_target"] = _jax.random.normal(_jax.random.fold_in(key, 0), shape, _jnp.float32)
    for i, name in enumerate(TWIN_WEIGHTS):
        w = inp[name].astype(_jnp.float32)
        if MOMENT_SCALE is None:
            s = _jnp.sqrt(_jnp.mean(_jnp.square(w)) + 1e-30)
        else:
            s = MOMENT_SCALE[name]
        km, kv = _jax.random.split(_jax.random.fold_in(key, i + 1))
        out[name] = w
        out["m_" + name] = s * _jax.random.normal(km, w.shape, _jnp.float32)
        out["v_" + name] = (s * s) * _jax.random.uniform(kv, w.shape, _jnp.float32, 0.5, 1.5)
    if N_MICROBATCH > 1:
        for name, axis in PER_EXAMPLE_BATCH_AXIS.items():
            out[name] = _to_microbatches(out[name], axis)
    return {'x': out['x'], 'p': out['p'], 'pool_w': out['pool_w'], 'pool_scale': out['pool_scale'], 'ssm_in_w': out['ssm_in_w'], 'ssm_conv_w': out['ssm_conv_w'], 'ssm_conv_b': out['ssm_conv_b'], 'ssm_dt_bias': out['ssm_dt_bias'], 'ssm_a_log': out['ssm_a_log'], 'ssm_d': out['ssm_d'], 'ssm_norm_w': out['ssm_norm_w'], 'ssm_out_w': out['ssm_out_w'], 'mlp_w1': out['mlp_w1'], 'mlp_w2': out['mlp_w2'], 'ln_g': out['ln_g'], 'ln_b': out['ln_b'], 'ple_w': out['ple_w'], 'ple_gate_w': out['ple_gate_w'], 'loss_target': out['loss_target'], 'm_pool_w': out['m_pool_w'], 'm_pool_scale': out['m_pool_scale'], 'm_ssm_in_w': out['m_ssm_in_w'], 'm_ssm_conv_w': out['m_ssm_conv_w'], 'm_ssm_conv_b': out['m_ssm_conv_b'], 'm_ssm_dt_bias': out['m_ssm_dt_bias'], 'm_ssm_a_log': out['m_ssm_a_log'], 'm_ssm_d': out['m_ssm_d'], 'm_ssm_norm_w': out['m_ssm_norm_w'], 'm_ssm_out_w': out['m_ssm_out_w'], 'm_mlp_w1': out['m_mlp_w1'], 'm_mlp_w2': out['m_mlp_w2'], 'm_ln_g': out['m_ln_g'], 'm_ln_b': out['m_ln_b'], 'm_ple_w': out['m_ple_w'], 'm_ple_gate_w': out['m_ple_gate_w'], 'v_pool_w': out['v_pool_w'], 'v_pool_scale': out['v_pool_scale'], 'v_ssm_in_w': out['v_ssm_in_w'], 'v_ssm_conv_w': out['v_ssm_conv_w'], 'v_ssm_conv_b': out['v_ssm_conv_b'], 'v_ssm_dt_bias': out['v_ssm_dt_bias'], 'v_ssm_a_log': out['v_ssm_a_log'], 'v_ssm_d': out['v_ssm_d'], 'v_ssm_norm_w': out['v_ssm_norm_w'], 'v_ssm_out_w': out['v_ssm_out_w'], 'v_mlp_w1': out['v_mlp_w1'], 'v_mlp_w2': out['v_mlp_w2'], 'v_ln_g': out['v_ln_g'], 'v_ln_b': out['v_ln_b'], 'v_ple_w': out['v_ple_w'], 'v_ple_gate_w': out['v_ple_gate_w']}


def _loss(weights, diff, rest, loss_target):
    with _jax.named_scope("forward"):
        args = {**rest, TWIN_DIFF_INPUT: diff, **{k: w.astype(_WEIGHT_DTYPES[k]) for k, w in weights.items()}}
        y = _forward(args)
    with _jax.named_scope("loss_head"):
        err = _jnp.square(y.astype(_jnp.float32) - loss_target)
        return 0.5 * _jnp.sum(_jnp.mean(err, axis=-1)) if err.ndim else 0.5 * err


def _adamw(w, g, m, v):
    m = ADAM_B1 * m + (1.0 - ADAM_B1) * g
    v = ADAM_B2 * v + (1.0 - ADAM_B2) * _jnp.square(g)
    m_hat = m / (1.0 - ADAM_B1 ** ADAM_STEP)
    v_hat = v / (1.0 - ADAM_B2 ** ADAM_STEP)
    delta = -ADAM_LR * (m_hat / (_jnp.sqrt(v_hat) + ADAM_EPS) + ADAM_WD * w)
    return delta, m, v


def reference(x, p, pool_w, pool_scale, ssm_in_w, ssm_conv_w, ssm_conv_b, ssm_dt_bias, ssm_a_log, ssm_d, ssm_norm_w, ssm_out_w, mlp_w1, mlp_w2, ln_g, ln_b, ple_w, ple_gate_w, loss_target, m_pool_w, m_pool_scale, m_ssm_in_w, m_ssm_conv_w, m_ssm_conv_b, m_ssm_dt_bias, m_ssm_a_log, m_ssm_d, m_ssm_norm_w, m_ssm_out_w, m_mlp_w1, m_mlp_w2, m_ln_g, m_ln_b, m_ple_w, m_ple_gate_w, v_pool_w, v_pool_scale, v_ssm_in_w, v_ssm_conv_w, v_ssm_conv_b, v_ssm_dt_bias, v_ssm_a_log, v_ssm_d, v_ssm_norm_w, v_ssm_out_w, v_mlp_w1, v_mlp_w2, v_ln_g, v_ln_b, v_ple_w, v_ple_gate_w):
    given = dict(x=x, p=p, pool_w=pool_w, pool_scale=pool_scale, ssm_in_w=ssm_in_w, ssm_conv_w=ssm_conv_w, ssm_conv_b=ssm_conv_b, ssm_dt_bias=ssm_dt_bias, ssm_a_log=ssm_a_log, ssm_d=ssm_d, ssm_norm_w=ssm_norm_w, ssm_out_w=ssm_out_w, mlp_w1=mlp_w1, mlp_w2=mlp_w2, ln_g=ln_g, ln_b=ln_b, ple_w=ple_w, ple_gate_w=ple_gate_w, loss_target=loss_target, m_pool_w=m_pool_w, m_pool_scale=m_pool_scale, m_ssm_in_w=m_ssm_in_w, m_ssm_conv_w=m_ssm_conv_w, m_ssm_conv_b=m_ssm_conv_b, m_ssm_dt_bias=m_ssm_dt_bias, m_ssm_a_log=m_ssm_a_log, m_ssm_d=m_ssm_d, m_ssm_norm_w=m_ssm_norm_w, m_ssm_out_w=m_ssm_out_w, m_mlp_w1=m_mlp_w1, m_mlp_w2=m_mlp_w2, m_ln_g=m_ln_g, m_ln_b=m_ln_b, m_ple_w=m_ple_w, m_ple_gate_w=m_ple_gate_w, v_pool_w=v_pool_w, v_pool_scale=v_pool_scale, v_ssm_in_w=v_ssm_in_w, v_ssm_conv_w=v_ssm_conv_w, v_ssm_conv_b=v_ssm_conv_b, v_ssm_dt_bias=v_ssm_dt_bias, v_ssm_a_log=v_ssm_a_log, v_ssm_d=v_ssm_d, v_ssm_norm_w=v_ssm_norm_w, v_ssm_out_w=v_ssm_out_w, v_mlp_w1=v_mlp_w1, v_mlp_w2=v_mlp_w2, v_ln_g=v_ln_g, v_ln_b=v_ln_b, v_ple_w=v_ple_w, v_ple_gate_w=v_ple_gate_w)
    weights = {n: given[n] for n in TWIN_WEIGHTS}
    shared = {n: given[n] for n in SHARED_INPUTS}
    per_example = {n: given[n] for n in ['x', 'p']}
    grad_fn = _jax.value_and_grad(_loss, argnums=(0, 1))

    def one_microbatch(ex, loss_target):
        ex = dict(ex)
        diff = ex.pop(TWIN_DIFF_INPUT)
        return grad_fn(weights, diff, {**shared, **ex}, loss_target)

    if N_MICROBATCH == 1:
        loss, (grad_w, grad_x) = one_microbatch(per_example, given["loss_target"])
    else:
        def body(carry, xs):
            loss_sum, grad_sum = carry
            l_k, (gw_k, gx_k) = one_microbatch(xs[0], xs[1])
            with _jax.named_scope("update"):
                return (loss_sum + l_k, _jax.tree.map(_jnp.add, grad_sum, gw_k)), gx_k

        init = (_jnp.zeros((), _jnp.float32), _jax.tree.map(_jnp.zeros_like, weights))
        (loss, grad_w), grad_x = _jax.lax.scan(body, init, (per_example, given["loss_target"]))
    with _jax.named_scope("update"):
        delta_w, new_m, new_v = {}, {}, {}
        for n in TWIN_WEIGHTS:
            delta_w[n], new_m[n], new_v[n] = _adamw(weights[n], grad_w[n], given["m_" + n], given["v_" + n])
    return (loss, grad_x, *[grad_w[n] for n in TWIN_WEIGHTS], *[delta_w[n] for n in TWIN_WEIGHTS],
            *[new_m[n] for n in TWIN_WEIGHTS], *[new_v[n] for n in TWIN_WEIGHTS])
```

```python
import functools

import jax
import jax.numpy as jnp
from jax import lax
from jax.experimental import pallas as pl
from jax.experimental.pallas import tpu as pltpu

F32 = jnp.float32
BF16 = jnp.bfloat16
SDS = jax.ShapeDtypeStruct
MESH = pl.DeviceIdType.MESH
ANY = pl.BlockSpec(memory_space=pl.ANY)

N_DEV = 8
DEPTH = 2
ALPHA = (2.0 * DEPTH) ** 0.25
LN_EPS = 1e-5
RMS_EPS = 1e-5
POOL_WINDOW_LOG2 = (1, 2, 3, 4)
D_STATE = 128
CHUNK = 128
CONV_WIDTH = 4
ADAM_LR = 0.001
ADAM_B1 = 0.9
ADAM_B2 = 0.999
ADAM_EPS = 1e-08
ADAM_WD = 0.01
ADAM_STEP = 10

V7X_LANES = 128
V7X_VMEM_LIMIT = 48 * 1024 * 1024


def _cp(*sem):
    return pltpu.CompilerParams(dimension_semantics=sem, vmem_limit_bytes=V7X_VMEM_LIMIT)


def _pick(dim, cap):
    if dim <= cap:
        return dim
    best = None
    for t in range(V7X_LANES, cap + 1, V7X_LANES):
        if dim % t == 0:
            best = t
    assert best is not None, (dim, cap)
    return best


def _row_tile(rows, cols, itemsize=4, target=1 << 20):
    t = rows
    while t % 2 == 0 and t // 2 >= 16 and (t // 2) % 16 == 0 and t * cols * itemsize > target:
        t //= 2
    return t


def _slot(s):
    return (s % 2) * 4 + s // 2


def _all_gather(name, shards):
    n = len(shards)

    def body(*refs):
        ins, outs = refs[:n], refs[n:2 * n]
        send_sems, recv_sems, local_sems = refs[2 * n:]
        x, y, c = lax.axis_index("x"), lax.axis_index("y"), lax.axis_index("c")
        me, sibling = (x, y, c), (x, y, 1 - c)
        chips = [(1 - x, y), (x, 1 - y), (1 - x, 1 - y)]

        def copy(a, k, block, to, src=None):
            dst = outs[a].at[4 * block[0] + 2 * block[1] + block[2]]
            return pltpu.make_async_remote_copy(
                src_ref=dst if src is None else src, dst_ref=dst, send_sem=send_sems.at[a, k],
                recv_sem=recv_sems.at[a, k], device_id=to, device_id_type=MESH)

        mine = [pltpu.make_async_copy(ins[a], outs[a].at[4 * x + 2 * y + c], local_sems.at[a]) for a in range(n)]
        for cp in mine:
            cp.start()
        first = []
        for a in range(n):
            first.append(copy(a, 0, me, sibling, src=ins[a]))
            first += [copy(a, 1 + j, me, (*chip, c), src=ins[a]) for j, chip in enumerate(chips)]
        for cp in first:
            cp.start()
        passed = []
        for j, chip in enumerate(chips):
            for a in range(n):
                copy(a, 1 + j, (*chip, c), me).wait_recv()
                fwd = copy(a, 4 + j, (*chip, c), sibling)
                fwd.start()
                passed.append(fwd)
        for a in range(n):
            copy(a, 0, sibling, me).wait_recv()
            for j, chip in enumerate(chips):
                copy(a, 4 + j, (*chip, 1 - c), me).wait_recv()
        for cp in first + passed:
            cp.wait_send()
        for cp in mine:
            cp.wait()

    return pl.pallas_call(
        body, name=name,
        out_shape=[SDS((N_DEV,) + s.shape, s.dtype) for s in shards],
        in_specs=[ANY] * n, out_specs=[ANY] * n,
        scratch_shapes=[pltpu.SemaphoreType.DMA((n, 7)), pltpu.SemaphoreType.DMA((n, 7)),
                        pltpu.SemaphoreType.DMA((n,))],
    )(*shards)


def _rs_sibling_exchange(name, grads):
    n = len(grads)

    def body(*refs):
        ins, outs = refs[:n], refs[n:2 * n]
        send_sems, recv_sems, local_sems = refs[2 * n:]
        x, y, c = lax.axis_index("x"), lax.axis_index("y"), lax.axis_index("c")
        local, remote = [], []
        for a in range(n):
            local.append(pltpu.make_async_copy(ins[a].at[pl.ds(4 * c, 4)], outs[a].at[1], local_sems.at[a]))
            remote.append(pltpu.make_async_remote_copy(
                src_ref=ins[a].at[pl.ds(4 * (1 - c), 4)], dst_ref=outs[a].at[0], send_sem=send_sems.at[a],
                recv_sem=recv_sems.at[a], device_id=(x, y, 1 - c), device_id_type=MESH))
        for cp in local + remote:
            cp.start()
        for cp in remote:
            cp.wait()
        for cp in local:
            cp.wait()

    return pl.pallas_call(
        body, name=name,
        out_shape=[SDS((2, 4) + g.shape[1:], g.dtype) for g in grads],
        in_specs=[ANY] * n, out_specs=[ANY] * n,
        scratch_shapes=[pltpu.SemaphoreType.DMA((n,)), pltpu.SemaphoreType.DMA((n,)), pltpu.SemaphoreType.DMA((n,))],
    )(*grads)


def _rs_chip_exchange(name, sums):
    n = len(sums)

    def body(*refs):
        ins, outs = refs[:n], refs[n:2 * n]
        send_sems, recv_sems, local_sems = refs[2 * n:]
        x, y, c = lax.axis_index("x"), lax.axis_index("y"), lax.axis_index("c")
        chips = [(1 - x, y), (x, 1 - y), (1 - x, 1 - y)]
        local, remote = [], []
        for a in range(n):
            local.append(pltpu.make_async_copy(ins[a].at[2 * x + y], outs[a].at[3], local_sems.at[a]))
            for j, (px, py) in enumerate(chips):
                remote.append(pltpu.make_async_remote_copy(
                    src_ref=ins[a].at[2 * px + py], dst_ref=outs[a].at[j], send_sem=send_sems.at[a, j],
                    recv_sem=recv_sems.at[a, j], device_id=(px, py, c), device_id_type=MESH))
        for cp in local + remote:
            cp.start()
        for cp in remote:
            cp.wait()
        for cp in local:
            cp.wait()

    return pl.pallas_call(
        body, name=name,
        out_shape=[SDS((4,) + s.shape[1:], s.dtype) for s in sums],
        in_specs=[ANY] * n, out_specs=[ANY] * n,
        scratch_shapes=[pltpu.SemaphoreType.DMA((n, 3)), pltpu.SemaphoreType.DMA((n, 3)),
                        pltpu.SemaphoreType.DMA((n,))],
    )(*sums)


def _mm_core(name, a, b, *, grid, a_spec, b_spec, dims, acc_shape, outs, out_spec, epilogue=None, extras=(),
             extra_specs=()):
    nk = grid[2]
    ne, no = len(extras), len(outs)

    def body(a_ref, b_ref, *rest):
        e_refs, o_refs, acc = rest[:ne], rest[ne:ne + no], rest[ne + no]
        k = pl.program_id(2)

        @pl.when(k == 0)
        def _():
            acc[...] = jnp.zeros_like(acc)

        acc[...] += lax.dot_general(a_ref[...].astype(BF16), b_ref[...].astype(BF16), dims,
                                    preferred_element_type=F32)

        @pl.when(k == nk - 1)
        def _():
            r = acc[...]
            vals = epilogue(r, *[e[...] for e in e_refs]) if epilogue is not None else (r,)
            for o, v in zip(o_refs, vals):
                o[...] = v.astype(o.dtype)

    res = pl.pallas_call(
        body, name=name, grid=grid, out_shape=list(outs),
        in_specs=[a_spec, b_spec, *extra_specs], out_specs=[out_spec] * no,
        scratch_shapes=[pltpu.VMEM(acc_shape, F32)],
        compiler_params=_cp("parallel", "parallel", "arbitrary"),
    )(a, b, *extras)
    return res


NN = (((1,), (0,)), ((), ()))
NT = (((1,), (1,)), ((), ()))
TN = (((0,), (0,)), ((), ()))


def _w_dims(w, kind):
    if kind == "col":
        return w.shape[1], N_DEV * w.shape[2], w.shape[1], w.shape[2]
    if kind == "row":
        return N_DEV * w.shape[1], w.shape[2], w.shape[1], w.shape[2]
    return w.shape[0], w.shape[1], w.shape[0], w.shape[1]


def _mm_fwd(name, a, w, kind, out_dtypes, epilogue=None, extras=()):
    m, k_dim = a.shape
    kk, n, ks, ns = _w_dims(w, kind)
    assert kk == k_dim
    tm = _pick(m, 1024)
    if kind == "col":
        tn, tk = _pick(ns, 1024), _pick(kk, 512)
        nb = ns // tn
        b_spec = pl.BlockSpec((None, tk, tn), lambda i, j, k: (j // nb, k, j % nb))
    elif kind == "row":
        tn, tk = _pick(n, 1024), _pick(ks, 512)
        kb = ks // tk
        b_spec = pl.BlockSpec((None, tk, tn), lambda i, j, k: (k // kb, k % kb, j))
    else:
        tn, tk = _pick(n, 1152), _pick(kk, 512)
        b_spec = pl.BlockSpec((tk, tn), lambda i, j, k: (k, j))
    mn_spec = pl.BlockSpec((tm, tn), lambda i, j, k: (i, j))
    return _mm_core(
        name, a, w, grid=(m // tm, n // tn, kk // tk),
        a_spec=pl.BlockSpec((tm, tk), lambda i, j, k: (i, k)), b_spec=b_spec, dims=NN, acc_shape=(tm, tn),
        outs=[SDS((m, n), dt) for dt in out_dtypes], out_spec=mn_spec, epilogue=epilogue, extras=extras,
        extra_specs=[mn_spec] * len(extras))


def _mm_dx(name, dy, w, kind, out_dtypes, epilogue=None, extras=()):
    m, n_dim = dy.shape
    kk, n, ks, ns = _w_dims(w, kind)
    assert n == n_dim
    tm = _pick(m, 1024)
    if kind == "col":
        tn, tk = _pick(kk, 1024), _pick(ns, 512)
        kb = ns // tk
        b_spec = pl.BlockSpec((None, tn, tk), lambda i, j, k: (k // kb, j, k % kb))
    elif kind == "row":
        tn, tk = _pick(ks, 1024), _pick(n, 512)
        nb = ks // tn
        b_spec = pl.BlockSpec((None, tn, tk), lambda i, j, k: (j // nb, j % nb, k))
    else:
        tn, tk = _pick(kk, 1024), _pick(n, 1152)
        b_spec = pl.BlockSpec((tn, tk), lambda i, j, k: (j, k))
    mk_spec = pl.BlockSpec((tm, tn), lambda i, j, k: (i, j))
    return _mm_core(
        name, dy, w, grid=(m // tm, kk // tn, n // tk),
        a_spec=pl.BlockSpec((tm, tk), lambda i, j, k: (i, k)), b_spec=b_spec, dims=NT, acc_shape=(tm, tn),
        outs=[SDS((m, kk), dt) for dt in out_dtypes], out_spec=mk_spec, epilogue=epilogue, extras=extras,
        extra_specs=[mk_spec] * len(extras))


def _mm_dw(name, a, dy, kind):
    m, kk = a.shape
    n = dy.shape[1]
    tk = _pick(m, 512)
    if kind == "col":
        ns = n // N_DEV
        tm, tn = _pick(kk, 1024), _pick(ns, 1024)
        nb = ns // tn
        out = SDS((N_DEV, kk, ns), BF16)
        out_spec = pl.BlockSpec((None, tm, tn), lambda i, j, k: (_slot(j // nb), i, j % nb))
    elif kind == "row":
        ks = kk // N_DEV
        tm, tn = _pick(ks, 1024), _pick(n, 1024)
        mb = ks // tm
        out = SDS((N_DEV, ks, n), BF16)
        out_spec = pl.BlockSpec((None, tm, tn), lambda i, j, k: (_slot(i // mb), i % mb, j))
    else:
        tm, tn = _pick(kk, 1024), _pick(n, 1152)
        out = SDS((kk, n), BF16)
        out_spec = pl.BlockSpec((tm, tn), lambda i, j, k: (i, j))
    return _mm_core(
        name, a, dy, grid=(kk // tm, n // tn, m // tk),
        a_spec=pl.BlockSpec((tk, tm), lambda i, j, k: (k, i)), b_spec=pl.BlockSpec((tk, tn), lambda i, j, k: (k, j)),
        dims=TN, acc_shape=(tm, tn), outs=[out], out_spec=out_spec)[0]


def _rowwise(name, fn, ins, outs, rows, tile):
    arrays, specs = [], []
    for arr, kind in ins:
        arrays.append(arr)
        if kind == "row":
            specs.append(pl.BlockSpec((tile, arr.shape[1]), lambda i: (i, 0)))
        elif kind == "vec":
            specs.append(pl.BlockSpec(arr.shape, lambda i, nd=arr.ndim: (0,) * nd))
        else:
            specs.append(kind)
    out_shapes, out_specs, kinds = [], [], []
    for cols, dt, kind in outs:
        kinds.append(kind)
        if kind == "row":
            out_shapes.append(SDS((rows, cols), dt))
            out_specs.append(pl.BlockSpec((tile, cols), lambda i: (i, 0)))
        else:
            out_shapes.append(SDS((1, cols), F32))
            out_specs.append(pl.BlockSpec((1, cols), lambda i: (0, 0)))
    ni = len(arrays)
    has_acc = "acc" in kinds

    def body(*refs):
        vals = fn(*[r[...] for r in refs[:ni]])
        i = pl.program_id(0)
        for o, v, kind in zip(refs[ni:], vals, kinds):
            if kind == "row":
                o[...] = v.astype(o.dtype)
            else:
                @pl.when(i == 0)
                def _(o=o):
                    o[...] = jnp.zeros_like(o)

                o[...] += v

    return pl.pallas_call(
        body, name=name, grid=(rows // tile,), out_shape=out_shapes, in_specs=specs, out_specs=out_specs,
        compiler_params=_cp("arbitrary" if has_acc else "parallel"),
    )(*arrays)


def _ln_fwd(name, u, g, b):
    d = u.shape[1]

    def fn(u, g, b):
        mu = jnp.mean(u, axis=1, keepdims=True)
        xc = u - mu
        var = jnp.mean(xc * xc, axis=1, keepdims=True)
        return (xc * lax.rsqrt(var + LN_EPS) * g + b,)

    return _rowwise(name, fn, [(u, "row"), (g, "vec"), (b, "vec")], [(d, F32, "row")], u.shape[0], 256)[0]


def _ln_bwd(name, u, dy, g):
    d = u.shape[1]

    def fn(u, dy, g):
        mu = jnp.mean(u, axis=1, keepdims=True)
        xc = u - mu
        var = jnp.mean(xc * xc, axis=1, keepdims=True)
        rstd = lax.rsqrt(var + LN_EPS)
        xhat = xc * rstd
        dxhat = dy * g
        m1 = jnp.mean(dxhat, axis=1, keepdims=True)
        m2 = jnp.mean(dxhat * xhat, axis=1, keepdims=True)
        du = rstd * (dxhat - m1 - xhat * m2)
        return du, jnp.sum(dy * xhat, axis=0, keepdims=True), jnp.sum(dy, axis=0, keepdims=True)

    return _rowwise(name, fn, [(u, "row"), (dy, "row"), (g, "vec")],
                    [(d, F32, "row"), (d, F32, "acc"), (d, F32, "acc")], u.shape[0], 256)


def _loss_bwd(name, y, target):
    d = y.shape[1]

    def fn(y, t):
        e = y - t
        return e * (1.0 / d), jnp.sum(e * e, axis=0, keepdims=True) * (0.5 / d)

    return _rowwise(name, fn, [(y, "row"), (target, "row")], [(d, F32, "row"), (d, F32, "acc")], y.shape[0], 256)


def _ple_bwd(name, dx, e, gate):
    d = dx.shape[1]

    def fn(dx, e, gate):
        return dx * e * gate * (1.0 - gate), dx * gate

    return _rowwise(name, fn, [(dx, "row"), (e, "row"), (gate, "row")], [(d, BF16, "row"), (d, BF16, "row")],
                    dx.shape[0], 256)


def _sigmoid(v):
    return 1.0 / (1.0 + jnp.exp(-v))


def _gated_rms_fwd(name, y, zx, norm_w):
    di = y.shape[1]

    def fn(y, z, w):
        yg = y * (z * _sigmoid(z))
        r = lax.rsqrt(jnp.mean(yg * yg, axis=1, keepdims=True) + RMS_EPS)
        return (yg * r * w,)

    z_spec = pl.BlockSpec((128, di), lambda i: (i, 0))
    return _rowwise(name, fn, [(y, "row"), (zx, z_spec), (norm_w, "vec")], [(di, BF16, "row")], y.shape[0], 128)[0]


def _gated_rms_bwd(name, y, zx, norm_w, dout):
    di = y.shape[1]

    def fn(y, z, w, dout):
        sg = _sigmoid(z)
        sz = z * sg
        yg = y * sz
        r = lax.rsqrt(jnp.mean(yg * yg, axis=1, keepdims=True) + RMS_EPS)
        dn = dout * w
        dyg = r * (dn - yg * (r * r) * jnp.mean(dn * yg, axis=1, keepdims=True))
        dy = dyg * sz
        dz = dyg * y * (sg * (1.0 + z * (1.0 - sg)))
        return dy, dz, jnp.sum(dout * yg * r, axis=0, keepdims=True)

    z_spec = pl.BlockSpec((128, di), lambda i: (i, 0))
    return _rowwise(name, fn, [(y, "row"), (zx, z_spec), (norm_w, "vec"), (dout, "row")],
                    [(di, F32, "row"), (di, BF16, "row"), (di, F32, "acc")], y.shape[0], 128)


def _shift_down(v, j, row):
    return jnp.where(row >= j, pltpu.roll(v, j, 0), 0.0)


def _shift_up(v, j, row):
    t = v.shape[0]
    return jnp.where(row < t - j, pltpu.roll(v, t - j, 0), 0.0)


def _pool_select(parts, g):
    return jnp.where(g == 0, parts[0], jnp.where(g == 1, parts[1], jnp.where(g == 2, parts[2], parts[3])))


def _pool_windows(name, x, transpose, scale_by=None):
    t, d = x.shape
    cg = d // 4
    cw = V7X_LANES
    per = cg // cw

    def body(*refs):
        x_ref, o_ref = refs[0], refs[-1]
        g = pl.program_id(0) // per
        xv = x_ref[...]
        row = lax.broadcasted_iota(jnp.int32, (t, 1), 0)
        cnt = jnp.minimum(row + 1, jnp.left_shift(2, g)).astype(F32)
        s = xv / cnt if transpose else xv
        parts = []
        for lg in POOL_WINDOW_LOG2:
            j = 1 << (lg - 1)
            s = s + (_shift_up(s, j, row) if transpose else _shift_down(s, j, row))
            parts.append(s)
        sel = _pool_select(parts, g)
        if transpose:
            o_ref[...] = ALPHA * refs[1][...] + sel - xv
        else:
            o_ref[...] = (sel / cnt - xv).astype(o_ref.dtype)

    col = pl.BlockSpec((t, cw), lambda j: (0, j))
    ins = [x] if scale_by is None else [x, scale_by]
    return pl.pallas_call(
        body, name=name, grid=(d // cw,), out_shape=SDS((t, d), F32 if transpose else BF16),
        in_specs=[col] * len(ins), out_specs=col, compiler_params=_cp("parallel"),
    )(*ins)


def _pool_mm(name, pooled, w, scale, x):
    t, d = x.shape
    cg = d // 4
    tm = _pick(t, 1024)

    def body(p_ref, w_ref, s_ref, x_ref, u_ref, h_ref):
        h = jnp.dot(p_ref[...], w_ref[...], preferred_element_type=F32)
        h_ref[...] = h
        u_ref[...] = ALPHA * x_ref[...] + h * s_ref[...]

    blk = pl.BlockSpec((tm, cg), lambda g, i: (i, g))
    return pl.pallas_call(
        body, name=name, grid=(4, t // tm), out_shape=[SDS((t, d), F32), SDS((t, d), F32)],
        in_specs=[blk, pl.BlockSpec((None, cg, cg), lambda g, i: (g, 0, 0)), pl.BlockSpec((1, cg), lambda g, i: (0, g)),
                  blk],
        out_specs=[blk, blk], compiler_params=_cp("parallel", "parallel"),
    )(pooled, w, scale, x)


def _pool_bwd_mm(name, du, hraw, w, scale):
    t, d = du.shape
    cg = d // 4
    tm = _pick(t, 1024)

    def body(du_ref, h_ref, w_ref, s_ref, dh_ref, dp_ref, ds_ref):
        @pl.when(pl.program_id(1) == 0)
        def _():
            ds_ref[...] = jnp.zeros_like(ds_ref)

        duv = du_ref[...]
        ds_ref[...] += jnp.sum(duv * h_ref[...], axis=0, keepdims=True)
        dh = (duv * s_ref[...]).astype(BF16)
        dh_ref[...] = dh
        dp_ref[...] = lax.dot_general(dh, w_ref[...], NT, preferred_element_type=F32)

    blk = pl.BlockSpec((tm, cg), lambda g, i: (i, g))
    vec = pl.BlockSpec((1, cg), lambda g, i: (0, g))
    return pl.pallas_call(
        body, name=name, grid=(4, t // tm), out_shape=[SDS((t, d), BF16), SDS((t, d), F32), SDS((1, d), F32)],
        in_specs=[blk, blk, pl.BlockSpec((None, cg, cg), lambda g, i: (g, 0, 0)), vec],
        out_specs=[blk, blk, vec], compiler_params=_cp("parallel", "arbitrary"),
    )(du, hraw, w, scale)


def _pool_dw(name, pooled, dh):
    t, d = pooled.shape
    cg = d // 4
    tk = _pick(t, 512)
    nk = t // tk

    def body(p_ref, dh_ref, o_ref, acc):
        k = pl.program_id(1)

        @pl.when(k == 0)
        def _():
            acc[...] = jnp.zeros_like(acc)

        acc[...] += lax.dot_general(p_ref[...], dh_ref[...], TN, preferred_element_type=F32)

        @pl.when(k == nk - 1)
        def _():
            o_ref[...] = acc[...].astype(o_ref.dtype)

    blk = pl.BlockSpec((tk, cg), lambda g, k: (k, g))
    return pl.pallas_call(
        body, name=name, grid=(4, nk), out_shape=SDS((4, cg, cg), BF16), in_specs=[blk, blk],
        out_specs=pl.BlockSpec((None, cg, cg), lambda g, k: (g, 0, 0)), scratch_shapes=[pltpu.VMEM((cg, cg), F32)],
        compiler_params=_cp("parallel", "arbitrary"),
    )(pooled, dh)


def _conv_pre(u, w_ref, b_ref, row):
    pre = b_ref[...] + _shift_down(u, 3, row) * w_ref[0:1, :]
    pre = pre + _shift_down(u, 2, row) * w_ref[1:2, :]
    pre = pre + _shift_down(u, 1, row) * w_ref[2:3, :]
    return pre + u * w_ref[3:4, :]


def _conv_fwd(name, zx, conv_w, conv_b, di):
    t = zx.shape[0]
    cd = conv_w.shape[1]
    cw = _pick(cd, 256)
    off = di // cw

    def body(u_ref, w_ref, b_ref, o_ref):
        row = lax.broadcasted_iota(jnp.int32, (t, 1), 0)
        pre = _conv_pre(u_ref[...], w_ref, b_ref, row)
        o_ref[...] = pre * _sigmoid(pre)

    return pl.pallas_call(
        body, name=name, grid=(cd // cw,), out_shape=SDS((t, cd), F32),
        in_specs=[pl.BlockSpec((t, cw), lambda j: (0, off + j)), pl.BlockSpec((CONV_WIDTH, cw), lambda j: (0, j)),
                  pl.BlockSpec((1, cw), lambda j: (0, j))],
        out_specs=pl.BlockSpec((t, cw), lambda j: (0, j)), compiler_params=_cp("parallel"),
    )(zx, conv_w, conv_b)


def _conv_bwd(name, zx, conv_w, conv_b, dact, di):
    t = zx.shape[0]
    cd = conv_w.shape[1]
    cw = _pick(cd, 256)
    off = di // cw

    def body(u_ref, w_ref, b_ref, da_ref, du_ref, dw_ref, db_ref):
        row = lax.broadcasted_iota(jnp.int32, (t, 1), 0)
        u = u_ref[...]
        pre = _conv_pre(u, w_ref, b_ref, row)
        sg = _sigmoid(pre)
        dpre = da_ref[...] * (sg * (1.0 + pre * (1.0 - sg)))
        du = dpre * w_ref[3:4, :]
        for j in (1, 2, 3):
            du = du + _shift_up(dpre, j, row) * w_ref[3 - j:4 - j, :]
            dw_ref[3 - j:4 - j, :] = jnp.sum(dpre * _shift_down(u, j, row), axis=0, keepdims=True)
        dw_ref[3:4, :] = jnp.sum(dpre * u, axis=0, keepdims=True)
        db_ref[...] = jnp.sum(dpre, axis=0, keepdims=True)
        du_ref[...] = du.astype(du_ref.dtype)

    wspec = pl.BlockSpec((CONV_WIDTH, cw), lambda j: (0, j))
    bspec = pl.BlockSpec((1, cw), lambda j: (0, j))
    ospec = pl.BlockSpec((t, cw), lambda j: (0, j))
    return pl.pallas_call(
        body, name=name, grid=(cd // cw,), out_shape=[SDS((t, cd), BF16), SDS((CONV_WIDTH, cd), F32), SDS((1, cd), F32)],
        in_specs=[pl.BlockSpec((t, cw), lambda j: (0, off + j)), wspec, bspec, ospec],
        out_specs=[ospec, wspec, bspec], compiler_params=_cp("parallel"),
    )(zx, conv_w, conv_b, dact)


def _softplus(v):
    return jnp.maximum(v, 0.0) + jnp.log(1.0 + jnp.exp(-jnp.abs(v)))


def _dt_fwd(name, zx, bias, a_log, col_block):
    t = zx.shape[0]

    def body(r_ref, b_ref, al_ref, dt_ref, acs_ref):
        row = lax.broadcasted_iota(jnp.int32, (t, 1), 0) % CHUNK
        dt = _softplus(r_ref[...] + b_ref[...])
        s = dt * (-jnp.exp(al_ref[...]))
        j = 1
        while j < CHUNK:
            s = s + jnp.where(row >= j, pltpu.roll(s, j, 0), 0.0)
            j *= 2
        dt_ref[...] = dt
        acs_ref[...] = s

    vec = pl.BlockSpec((1, V7X_LANES), lambda i: (0, 0))
    full = pl.BlockSpec((t, V7X_LANES), lambda i: (0, 0))
    return pl.pallas_call(
        body, name=name, grid=(1,), out_shape=[SDS((t, V7X_LANES), F32)] * 2,
        in_specs=[pl.BlockSpec((t, V7X_LANES), lambda i: (0, col_block)), vec, vec], out_specs=[full, full],
        compiler_params=_cp("arbitrary"),
    )(zx, bias, a_log)


def _dt_bwd(name, zx, bias, a_log, d_acs, d_dt, col_block):
    t = zx.shape[0]

    def body(r_ref, b_ref, al_ref, da_ref, dd_ref, draw_ref, db_ref, dal_ref):
        row = lax.broadcasted_iota(jnp.int32, (t, 1), 0) % CHUNK
        pre = r_ref[...] + b_ref[...]
        dt = _softplus(pre)
        a = -jnp.exp(al_ref[...])
        s = da_ref[...]
        j = 1
        while j < CHUNK:
            s = s + jnp.where(row < CHUNK - j, pltpu.roll(s, t - j, 0), 0.0)
            j *= 2
        ddt = dd_ref[...] + s * a
        dal_ref[...] = jnp.sum(s * dt, axis=0, keepdims=True) * a
        draw = ddt * _sigmoid(pre)
        db_ref[...] = jnp.sum(draw, axis=0, keepdims=True)
        draw_ref[...] = draw.astype(draw_ref.dtype)

    vec = pl.BlockSpec((1, V7X_LANES), lambda i: (0, 0))
    full = pl.BlockSpec((t, V7X_LANES), lambda i: (0, 0))
    return pl.pallas_call(
        body, name=name, grid=(1,), out_shape=[SDS((t, V7X_LANES), BF16), SDS((1, V7X_LANES), F32), SDS((1, V7X_LANES), F32)],
        in_specs=[pl.BlockSpec((t, V7X_LANES), lambda i: (0, col_block)), vec, vec, full, full],
        out_specs=[full, vec, vec], compiler_params=_cp("arbitrary"),
    )(zx, bias, a_log, d_acs, d_dt)


def _ssd_specs(t, di, g_n, hpg, p, rev):
    nc = t // CHUNK
    w = hpg * p
    nb = di // D_STATE

    def cc(c):
        return nc - 1 - c if rev else c

    return dict(
        xs=pl.BlockSpec((CHUNK, w), lambda g, c: (cc(c), g)),
        bm=pl.BlockSpec((CHUNK, D_STATE), lambda g, c: (cc(c), nb + g)),
        cm=pl.BlockSpec((CHUNK, D_STATE), lambda g, c: (cc(c), nb + g_n + g)),
        col=pl.BlockSpec((None, CHUNK, hpg), lambda g, c: (g, cc(c), 0)),
        rowv=pl.BlockSpec((None, hpg, CHUNK), lambda g, c: (g, 0, cc(c))),
        head=pl.BlockSpec((None, 1, hpg), lambda g, c: (g, 0, 0)),
        y=pl.BlockSpec((CHUNK, w), lambda g, c: (cc(c), g)),
        bc=pl.BlockSpec((CHUNK, D_STATE), lambda g, c: (cc(c), g)),
        prev=pl.BlockSpec((None, hpg, D_STATE, p), lambda g, c: (cc(c), g, 0, 0)),
    )


def _ssd_fwd(name, xbc, dt_col, a_col, a_row, d_skip, di, g_n, hpg, p):
    t = xbc.shape[0]
    nc = t // CHUNK
    sp = _ssd_specs(t, di, g_n, hpg, p, False)

    def body(xs_ref, bm_ref, cm_ref, dt_ref, ac_ref, ar_ref, d_ref, y_ref, prev_ref, h_ref):
        @pl.when(pl.program_id(1) == 0)
        def _():
            h_ref[...] = jnp.zeros_like(h_ref)

        bm = bm_ref[...].astype(BF16)
        cm = cm_ref[...].astype(BF16)
        cb = lax.dot_general(cm, bm, NT, preferred_element_type=F32)
        li = lax.broadcasted_iota(jnp.int32, (CHUNK, CHUNK), 0)
        si = lax.broadcasted_iota(jnp.int32, (CHUNK, CHUNK), 1)
        for hh in range(hpg):
            sl = slice(hh * p, (hh + 1) * p)
            x_h = xs_ref[:, sl]
            ac = ac_ref[:, hh:hh + 1]
            ar = ar_ref[hh:hh + 1, :]
            a_last = ar[:, CHUNK - 1:CHUNK]
            lm = jnp.exp(jnp.where(li >= si, ac - ar, -jnp.inf))
            xdt = x_h * dt_ref[:, hh:hh + 1]
            y = jnp.dot((cb * lm).astype(BF16), xdt.astype(BF16), preferred_element_type=F32)
            h_prev = h_ref[hh]
            prev_ref[hh] = h_prev
            y = y + jnp.dot(cm, h_prev.astype(BF16), preferred_element_type=F32) * jnp.exp(ac)
            y_ref[:, sl] = y + x_h * d_ref[:, hh:hh + 1]
            st = lax.dot_general(bm, (xdt * jnp.exp(a_last - ac)).astype(BF16), TN, preferred_element_type=F32)
            h_ref[hh] = jnp.exp(a_last) * h_prev + st

    return pl.pallas_call(
        body, name=name, grid=(g_n, nc),
        out_shape=[SDS((t, di), F32), SDS((nc, g_n * hpg, D_STATE, p), F32)],
        in_specs=[sp["xs"], sp["bm"], sp["cm"], sp["col"], sp["col"], sp["rowv"], sp["head"]],
        out_specs=[sp["y"], sp["prev"]], scratch_shapes=[pltpu.VMEM((hpg, D_STATE, p), F32)],
        compiler_params=_cp("parallel", "arbitrary"),
    )(xbc, xbc, xbc, dt_col, a_col, a_row, d_skip)


def _ssd_bwd(name, xbc, dt_col, a_col, a_row, d_skip, prev, dy, di, g_n, hpg, p):
    t = xbc.shape[0]
    nc = t // CHUNK
    sp = _ssd_specs(t, di, g_n, hpg, p, True)

    def body(xs_ref, bm_ref, cm_ref, dt_ref, ac_ref, ar_ref, d_ref, prev_ref, dy_ref,
             dx_ref, dbm_ref, dcm_ref, ddt_ref, dacs_ref, dd_ref, dh_ref):
        @pl.when(pl.program_id(1) == 0)
        def _():
            dh_ref[...] = jnp.zeros_like(dh_ref)
            dd_ref[...] = jnp.zeros_like(dd_ref)

        bm = bm_ref[...].astype(BF16)
        cm = cm_ref[...].astype(BF16)
        cb = lax.dot_general(cm, bm, NT, preferred_element_type=F32)
        li = lax.broadcasted_iota(jnp.int32, (CHUNK, CHUNK), 0)
        si = lax.broadcasted_iota(jnp.int32, (CHUNK, CHUNK), 1)
        last_row = lax.broadcasted_iota(jnp.int32, (CHUNK, 1), 0) == CHUNK - 1
        d_cb = jnp.zeros((CHUNK, CHUNK), F32)
        d_bm = jnp.zeros((CHUNK, D_STATE), F32)
        d_cm = jnp.zeros((CHUNK, D_STATE), F32)
        for hh in range(hpg):
            sl = slice(hh * p, (hh + 1) * p)
            x_h = xs_ref[:, sl]
            dt_h = dt_ref[:, hh:hh + 1]
            ac = ac_ref[:, hh:hh + 1]
            ar = ar_ref[hh:hh + 1, :]
            a_last = ar[:, CHUNK - 1:CHUNK]
            e_in = jnp.exp(ac)
            e_out = jnp.exp(a_last - ac)
            e_all = jnp.exp(a_last)
            lm = jnp.exp(jnp.where(li >= si, ac - ar, -jnp.inf))
            mm = cb * lm
            xdt = x_h * dt_h
            xdt_b = xdt.astype(BF16)
            dy_h = dy_ref[:, sl]
            dy_b = dy_h.astype(BF16)
            h_prev = h_prev_f = prev_ref[hh]
            h_prev = h_prev.astype(BF16)
            dh_next = dh_ref[hh]
            dh_next_b = dh_next.astype(BF16)
            ch = jnp.dot(cm, h_prev, preferred_element_type=F32)
            dy_e = (dy_h * e_in).astype(BF16)
            d_cm = d_cm + lax.dot_general(dy_e, h_prev, NT, preferred_element_type=F32)
            d_a = jnp.sum(dy_h * ch, axis=1, keepdims=True) * e_in
            dh_ref[hh] = e_all * dh_next + lax.dot_general(cm, dy_e, TN, preferred_element_type=F32)
            d_m = lax.dot_general(dy_b, xdt_b, NT, preferred_element_type=F32)
            d_xdt = lax.dot_general(mm.astype(BF16), dy_b, TN, preferred_element_type=F32)
            wm = d_m * mm
            d_a = d_a + jnp.sum(wm, axis=1, keepdims=True) - jnp.sum(wm.T, axis=1, keepdims=True)
            d_cb = d_cb + d_m * lm
            q = jnp.dot(bm, dh_next_b, preferred_element_type=F32)
            d_xdt = d_xdt + q * e_out
            d_f = jnp.sum(q * xdt, axis=1, keepdims=True) * e_out
            d_last = jnp.sum(d_f, axis=0, keepdims=True) + e_all * jnp.sum(
                jnp.sum(dh_next * h_prev_f, axis=1, keepdims=True), axis=0, keepdims=True)
            d_a = d_a - d_f + jnp.where(last_row, d_last, 0.0)
            d_bm = d_bm + lax.dot_general((xdt * e_out).astype(BF16), dh_next_b, NT, preferred_element_type=F32)
            d_skip = d_ref[:, hh:hh + 1]
            dx_ref[:, sl] = d_xdt * dt_h + dy_h * d_skip
            ddt_ref[:, hh:hh + 1] = jnp.sum(d_xdt * x_h, axis=1, keepdims=True)
            dacs_ref[:, hh:hh + 1] = d_a
            dd_ref[:, hh:hh + 1] += jnp.sum(jnp.sum(dy_h * x_h, axis=1, keepdims=True), axis=0, keepdims=True)
        d_cb_b = d_cb.astype(BF16)
        dcm_ref[...] = d_cm + jnp.dot(d_cb_b, bm, preferred_element_type=F32)
        dbm_ref[...] = d_bm + lax.dot_general(d_cb_b, cm, TN, preferred_element_type=F32)

    gn = g_n * D_STATE
    return pl.pallas_call(
        body, name=name, grid=(g_n, nc),
        out_shape=[SDS((t, di), F32), SDS((t, gn), F32), SDS((t, gn), F32), SDS((g_n, t, hpg), F32),
                   SDS((g_n, t, hpg), F32), SDS((g_n, 1, hpg), F32)],
        in_specs=[sp["xs"], sp["bm"], sp["cm"], sp["col"], sp["col"], sp["rowv"], sp["head"], sp["prev"], sp["y"]],
        out_specs=[sp["y"], sp["bc"], sp["bc"], sp["col"], sp["col"], sp["head"]],
        scratch_shapes=[pltpu.VMEM((hpg, D_STATE, p), F32)],
        compiler_params=_cp("parallel", "arbitrary"),
    )(xbc, xbc, xbc, dt_col, a_col, a_row, d_skip, prev, dy)


def _as3d(a):
    return a.reshape(a.shape[0], -1, a.shape[-1])


def _pair_sum(name, pair):
    shape = pair.shape[1:]
    v = pair.reshape(2, -1, pair.shape[-1])
    rows, cols = v.shape[1:]
    tr = _row_tile(rows, cols, 2)

    def body(a_ref, b_ref, o_ref):
        o_ref[...] = (a_ref[...].astype(F32) + b_ref[...].astype(F32)).astype(o_ref.dtype)

    out = pl.pallas_call(
        body, name=name, grid=(rows // tr,), out_shape=SDS((rows, cols), pair.dtype),
        in_specs=[pl.BlockSpec((None, tr, cols), lambda i: (0, i, 0)), pl.BlockSpec((None, tr, cols), lambda i: (1, i, 0))],
        out_specs=pl.BlockSpec((tr, cols), lambda i: (i, 0)), compiler_params=_cp("parallel"),
    )(v, v)
    return out.reshape(shape)


def _adamw(name, w, m, v, parts, layer, prev=None):
    lyr, rows, cols = w.shape
    n = parts.shape[0]
    tr = _row_tile(rows, cols)
    np_ = 0 if prev is None else 4

    def body(*refs):
        w_ref, m_ref, v_ref = refs[:3]
        p_refs = refs[3:3 + n]
        g_ref, d_ref, nm_ref, nv_ref = refs[3 + n + np_:]
        g = p_refs[0][...].astype(F32)
        for r in p_refs[1:]:
            g = g + r[...].astype(F32)
        nm = ADAM_B1 * m_ref[...] + (1.0 - ADAM_B1) * g
        nv = ADAM_B2 * v_ref[...] + (1.0 - ADAM_B2) * (g * g)
        m_hat = nm / (1.0 - ADAM_B1 ** ADAM_STEP)
        v_hat = nv / (1.0 - ADAM_B2 ** ADAM_STEP)
        g_ref[...] = g
        d_ref[...] = -ADAM_LR * (m_hat / (jnp.sqrt(v_hat) + ADAM_EPS) + ADAM_WD * w_ref[...])
        nm_ref[...] = nm
        nv_ref[...] = nv

    lspec = pl.BlockSpec((None, tr, cols), lambda i: (layer, i, 0))
    pspecs = [pl.BlockSpec((None, tr, cols), lambda i, q=q: (q, i, 0)) for q in range(n)]
    aliases = {} if prev is None else {3 + n + q: q for q in range(4)}
    return pl.pallas_call(
        body, name=name, grid=(rows // tr,), out_shape=[SDS(w.shape, F32)] * 4,
        in_specs=[lspec] * 3 + pspecs + [ANY] * np_, out_specs=[lspec] * 4, input_output_aliases=aliases,
        compiler_params=_cp("parallel"),
    )(w, m, v, *([parts] * n), *(prev or ()))


def _sum8(name, parts):
    rows = parts.shape[1]

    def body(p_ref, o_ref):
        s = p_ref[0]
        for q in range(1, N_DEV):
            s = s + p_ref[q]
        o_ref[...] = s

    return pl.pallas_call(
        body, name=name, grid=(1,), out_shape=SDS((rows, V7X_LANES), F32),
        in_specs=[pl.BlockSpec((N_DEV, rows, V7X_LANES), lambda i: (0, 0, 0))],
        out_specs=pl.BlockSpec((rows, V7X_LANES), lambda i: (0, 0)), compiler_params=_cp("arbitrary"),
    )(parts)


def _pack(vectors, align):
    flat = jnp.concatenate([v.reshape(-1) for v in vectors])
    pad = (-flat.shape[0]) % align
    if pad:
        flat = jnp.concatenate([flat, jnp.zeros((pad,), F32)])
    return flat.reshape(-1, V7X_LANES)


def _unpack(packed, shapes):
    flat = packed.reshape(-1)
    out, o = [], 0
    for s in shapes:
        size = 1
        for dim in s:
            size *= dim
        out.append(flat[o:o + size].reshape(s))
        o += size
    return out


def _relu2_epilogue(acc):
    r = jnp.maximum(acc, 0.0)
    return acc, r * r


def _residual_epilogue(acc, res):
    return (ALPHA * res + acc,)


def _plain_add_epilogue(acc, res):
    return (res + acc,)


def _gate_epilogue(acc, y, e):
    gate = _sigmoid(acc)
    return y + gate * e, gate


def _relu2_bwd_epilogue(acc, pre):
    return (acc * (2.0 * jnp.maximum(pre, 0.0)),)


def _tail_fwd(tag, u_a, wts, lng, lnb, p_l):
    y1 = _ln_fwd(f"ln1_{tag}", u_a, lng[0], lnb[0])
    pre, act = _mm_fwd(f"mlp1_{tag}", y1, wts["w1"], "col", [F32, BF16], _relu2_epilogue)
    (u_b,) = _mm_fwd(f"mlp2_{tag}", act, wts["w2"], "row", [F32], _residual_epilogue, (y1,))
    y2 = _ln_fwd(f"ln2_{tag}", u_b, lng[1], lnb[1])
    (e,) = _mm_fwd(f"ple_{tag}", p_l, wts["plew"], "col", [F32])
    xn, gate = _mm_fwd(f"gate_{tag}", y2, wts["gate"], "row", [F32, F32], _gate_epilogue, (y2, e))
    return xn, (u_a, y1, pre, act, u_b, y2, e, gate)


def _tail_bwd(tag, dxn, saved, wts, lng, p_l):
    u_a, y1, pre, act, u_b, y2, e, gate = saved
    dgpre, de = _ple_bwd(f"ple_bwd_{tag}", dxn, e, gate)
    (dy2,) = _mm_dx(f"gate_dx_{tag}", dgpre, wts["gate"], "row", [F32], _plain_add_epilogue, (dxn,))
    g_gate = _mm_dw(f"gate_dw_{tag}", y2, dgpre, "row")
    g_plew = _mm_dw(f"ple_dw_{tag}", p_l, de, "col")
    du_b, dg2, db2 = _ln_bwd(f"ln2_bwd_{tag}", u_b, dy2, lng[1])
    (dpre,) = _mm_dx(f"mlp2_dx_{tag}", du_b, wts["w2"], "row", [BF16], _relu2_bwd_epilogue, (pre,))
    g_w2 = _mm_dw(f"mlp2_dw_{tag}", act, du_b, "row")
    (dy1,) = _mm_dx(f"mlp1_dx_{tag}", dpre, wts["w1"], "col", [F32], _residual_epilogue, (du_b,))
    g_w1 = _mm_dw(f"mlp1_dw_{tag}", y1, dpre, "col")
    du_a, dg1, db1 = _ln_bwd(f"ln1_bwd_{tag}", u_a, dy1, lng[0])
    return du_a, dict(w1=g_w1, w2=g_w2, plew=g_plew, gate=g_gate), [dg1, dg2], [db1, db2]


def _to_slots(a, axis):
    shape = a.shape
    per = shape[axis] // N_DEV
    v = a.reshape(shape[:axis] + (2, 2, 2, per) + shape[axis + 1:])
    perm = (axis + 2, axis, axis + 1) + tuple(range(axis)) + tuple(range(axis + 3, v.ndim))
    v = v.transpose(perm)
    return v.reshape((N_DEV,) + shape[:axis] + (per,) + shape[axis + 1:])


def _pad_lanes(a):
    return jnp.pad(a, [(0, 0)] * (a.ndim - 1) + [(0, V7X_LANES - a.shape[-1])])


def kernel(x, p, pool_w, pool_scale, ssm_in_w, ssm_conv_w, ssm_conv_b, ssm_dt_bias, ssm_a_log, ssm_d, ssm_norm_w, ssm_out_w, mlp_w1, mlp_w2, ln_g, ln_b, ple_w, ple_gate_w, loss_target, m_pool_w, m_pool_scale, m_ssm_in_w, m_ssm_conv_w, m_ssm_conv_b, m_ssm_dt_bias, m_ssm_a_log, m_ssm_d, m_ssm_norm_w, m_ssm_out_w, m_mlp_w1, m_mlp_w2, m_ln_g, m_ln_b, m_ple_w, m_ple_gate_w, v_pool_w, v_pool_scale, v_ssm_in_w, v_ssm_conv_w, v_ssm_conv_b, v_ssm_dt_bias, v_ssm_a_log, v_ssm_d, v_ssm_norm_w, v_ssm_out_w, v_mlp_w1, v_mlp_w2, v_ln_g, v_ln_b, v_ple_w, v_ple_gate_w):
    t, d = x.shape[1:]
    h_n = ssm_dt_bias.shape[-1]
    di_s, cd_s, dp_s, d_s = ssm_norm_w.shape[-1], ssm_conv_b.shape[-1], ssm_in_w.shape[-1], ln_g.shape[-1]
    di, cd, dp = N_DEV * di_s, N_DEV * cd_s, N_DEV * dp_s
    p_dim = di // h_n
    g_n = (cd - di) // (2 * D_STATE)
    hpg = h_n // g_n
    dpp = di + cd + V7X_LANES
    assert h_n <= V7X_LANES and dp == di + cd + h_n and (di + cd) % V7X_LANES == 0
    dt_block = (di + cd) // V7X_LANES
    cg = d // 4
    me = 4 * lax.axis_index("x") + 2 * lax.axis_index("y") + lax.axis_index("c")

    x0, target = x[0], loss_target[0]
    p_l = [p[0, 0], p[1, 0]]

    small_shapes = [(CONV_WIDTH, cd_s), (1, cd_s), (1, di_s), (2, 2, d_s), (2, 2, d_s)]
    small = _pack([ssm_conv_w[0], ssm_conv_b, ssm_norm_w, ln_g, ln_b], 8 * V7X_LANES)
    big = [pool_w[0], mlp_w1[0], mlp_w2[0], ple_w[0], ple_gate_w[0],
           ssm_in_w[0], ssm_out_w[0], mlp_w1[1], mlp_w2[1], ple_w[1], ple_gate_w[1]]
    gathered = _all_gather("ag_weights", [w.astype(BF16) for w in big] + [small])
    pool_g, w1_0, w2_0, plew_0, gate_0, in_g, out_g, w1_1, w2_1, plew_1, gate_1, small_g = gathered
    wts = [dict(w1=w1_0, w2=w2_0, plew=plew_0, gate=gate_0), dict(w1=w1_1, w2=w2_1, plew=plew_1, gate=gate_1)]
    pool_full = pool_g.transpose(1, 0, 2, 3).reshape(4, cg, cg)
    in_full = jnp.pad(in_g.transpose(1, 0, 2).reshape(d, dp), ((0, 0), (0, dpp - dp)))
    sm = small_g.reshape(N_DEV, -1)
    o = 0
    parts = []
    for shp in small_shapes:
        size = 1
        for s in shp:
            size *= s
        parts.append(sm[:, o:o + size].reshape((N_DEV,) + shp))
        o += size
    conv_w_full = parts[0].transpose(1, 0, 2).reshape(CONV_WIDTH, cd)
    conv_b_full = parts[1].transpose(1, 0, 2).reshape(1, cd)
    norm_w_full = parts[2].transpose(1, 0, 2).reshape(1, di)
    ln_g_full = parts[3].transpose(1, 2, 0, 3).reshape(2, 2, 1, d)
    ln_b_full = parts[4].transpose(1, 2, 0, 3).reshape(2, 2, 1, d)
    bias128, alog128 = _pad_lanes(ssm_dt_bias), _pad_lanes(ssm_a_log)
    d_skip = ssm_d.reshape(g_n, 1, hpg)

    pooled = _pool_windows("pool_fwd", x0, False)
    u0, hraw = _pool_mm("pool_mm", pooled, pool_full, pool_scale, x0)
    x1, saved0 = _tail_fwd("l0", u0, wts[0], ln_g_full[0], ln_b_full[0], p_l[0])

    (zx,) = _mm_fwd("in_proj", x1, in_full, "plain", [F32])
    xbc = _conv_fwd("conv_fwd", zx, conv_w_full, conv_b_full, di)
    dt, acs = _dt_fwd("dt_fwd", zx, bias128, alog128, dt_block)

    def to_col(a):
        return a[:, :h_n].reshape(t, g_n, hpg).transpose(1, 0, 2)

    def to_row(a):
        return a[:, :h_n].reshape(t, g_n, hpg).transpose(1, 2, 0)

    def from_col(a):
        return _pad_lanes(a.transpose(1, 0, 2).reshape(t, h_n))

    dt_col, a_col, a_row = to_col(dt), to_col(acs), to_row(acs)
    y_ssd, prev = _ssd_fwd("ssd_fwd", xbc, dt_col, a_col, a_row, d_skip, di, g_n, hpg, p_dim)
    yn = _gated_rms_fwd("gated_rms_fwd", y_ssd, zx, norm_w_full)
    (u2,) = _mm_fwd("out_proj", yn, out_g, "row", [F32], _residual_epilogue, (x1,))
    x2, saved1 = _tail_fwd("l1", u2, wts[1], ln_g_full[1], ln_b_full[1], p_l[1])

    dx2, loss_cols = _loss_bwd("loss", x2, target)
    du2, gw1, dg_1, db_1 = _tail_bwd("l1", dx2, saved1, wts[1], ln_g_full[1], p_l[1])
    (dyn,) = _mm_dx("out_proj_dx", du2, out_g, "row", [F32])
    g_out = _mm_dw("out_proj_dw", yn, du2, "row")
    dy_ssd, dz, d_norm_w = _gated_rms_bwd("gated_rms_bwd", y_ssd, zx, norm_w_full, dyn)
    dxs, dbm, dcm, ddt_x, dacs, dd = _ssd_bwd("ssd_bwd", xbc, dt_col, a_col, a_row, d_skip, prev, dy_ssd,
                                              di, g_n, hpg, p_dim)
    draw, d_bias, d_alog = _dt_bwd("dt_bwd", zx, bias128, alog128, from_col(dacs), from_col(ddt_x), dt_block)
    dxbc, d_conv_w, d_conv_b = _conv_bwd("conv_bwd", zx, conv_w_full, conv_b_full,
                                         jnp.concatenate([dxs, dbm, dcm], axis=1), di)
    dzx = jnp.concatenate([dz, dxbc, draw], axis=1)
    (dx1,) = _mm_dx("in_proj_dx", dzx, in_full, "plain", [F32], _residual_epilogue, (du2,))
    g_in = _to_slots(_mm_dw("in_proj_dw", x1, dzx, "plain")[:, :dp], 1)

    du0, gw0, dg_0, db_0 = _tail_bwd("l0", dx1, saved0, wts[0], ln_g_full[0], p_l[0])
    dh, dpool, d_scale = _pool_bwd_mm("pool_bwd_mm", du0, hraw, pool_full, pool_scale)
    grad_x = _pool_windows("pool_bwd", dpool, True, du0)
    g_pool = _to_slots(_pool_dw("pool_dw", pooled, dh), 1)

    glist = [g_in, g_out, gw1["w1"], gw1["w2"], gw1["plew"], gw1["gate"],
             g_pool, gw0["w1"], gw0["w2"], gw0["plew"], gw0["gate"]]
    halves = _rs_sibling_exchange("rs_sibling", glist)
    sums = [_pair_sum(f"rs_pair_sum_{a}", hv) for a, hv in enumerate(halves)]
    quads = _rs_chip_exchange("rs_chips", sums)
    q_in, q_out, q_w1_1, q_w2_1, q_plew_1, q_gate_1, q_pool, q_w1_0, q_w2_0, q_plew_0, q_gate_0 = [
        q.reshape(4, -1, q.shape[-1]) for q in quads]

    def update(tag, w, m, v, by_layer):
        res = None
        for layer in sorted(by_layer, reverse=True):
            res = _adamw(f"adamw_{tag}_{layer}", _as3d(w), _as3d(m), _as3d(v), by_layer[layer], layer, res)
        return [r.reshape(w.shape) for r in res]

    upd = {
        "pool_w": update("pool_w", pool_w, m_pool_w, v_pool_w, {0: q_pool}),
        "ssm_in_w": update("ssm_in_w", ssm_in_w, m_ssm_in_w, v_ssm_in_w, {0: q_in}),
        "ssm_out_w": update("ssm_out_w", ssm_out_w, m_ssm_out_w, v_ssm_out_w, {0: q_out}),
        "mlp_w1": update("mlp_w1", mlp_w1, m_mlp_w1, v_mlp_w1, {0: q_w1_0, 1: q_w1_1}),
        "mlp_w2": update("mlp_w2", mlp_w2, m_mlp_w2, v_mlp_w2, {0: q_w2_0, 1: q_w2_1}),
        "ple_w": update("ple_w", ple_w, m_ple_w, v_ple_w, {0: q_plew_0, 1: q_plew_1}),
        "ple_gate_w": update("ple_gate_w", ple_gate_w, m_ple_gate_w, v_ple_gate_w, {0: q_gate_0, 1: q_gate_1}),
    }

    d_ln_g = jnp.stack([jnp.stack(dg_0), jnp.stack(dg_1)]).reshape(2, 2, d)
    d_ln_b = jnp.stack([jnp.stack(db_0), jnp.stack(db_1)]).reshape(2, 2, d)
    partial_shapes = [(CONV_WIDTH, cd), (1, cd), (1, di), (2, 2, d), (2, 2, d), (1, d), (1, h_n), (1, h_n), (1, h_n),
                      (1, d)]
    partial = _pack([d_conv_w, d_conv_b, d_norm_w, d_ln_g, d_ln_b, d_scale, d_bias[:, :h_n], d_alog[:, :h_n],
                     dd.reshape(1, h_n), loss_cols], 8 * V7X_LANES)
    (all_partials,) = _all_gather("ag_small_grads", [partial])
    tot = _unpack(_sum8("sum_small_grads", all_partials), partial_shapes)
    t_conv_w, t_conv_b, t_norm_w, t_ln_g, t_ln_b, t_scale, t_bias, t_alog, t_dd, t_loss = tot
    loss = jnp.sum(t_loss)

    def mine(a, per):
        return lax.dynamic_slice_in_dim(a, me * per, per, axis=a.ndim - 1)

    small_names = ["ssm_conv_w", "ssm_conv_b", "ssm_norm_w", "ln_g", "ln_b", "pool_scale", "ssm_dt_bias", "ssm_a_log",
                   "ssm_d"]
    small_w = [ssm_conv_w, ssm_conv_b, ssm_norm_w, ln_g, ln_b, pool_scale, ssm_dt_bias, ssm_a_log, ssm_d]
    small_m = [m_ssm_conv_w, m_ssm_conv_b, m_ssm_norm_w, m_ln_g, m_ln_b, m_pool_scale, m_ssm_dt_bias, m_ssm_a_log,
               m_ssm_d]
    small_v = [v_ssm_conv_w, v_ssm_conv_b, v_ssm_norm_w, v_ln_g, v_ln_b, v_pool_scale, v_ssm_dt_bias, v_ssm_a_log,
               v_ssm_d]
    small_grads = [mine(t_conv_w, cd_s), mine(t_conv_b, cd_s), mine(t_norm_w, di_s), mine(t_ln_g, d_s),
                   mine(t_ln_b, d_s), t_scale, t_bias, t_alog, t_dd]
    shapes = [w.shape for w in small_w]
    pk = [_pack(group, 8 * V7X_LANES)[None] for group in (small_w, small_m, small_v, small_grads)]
    res = _adamw("adamw_small", pk[0], pk[1], pk[2], pk[3], 0)
    for name, vals in zip(small_names, zip(*[_unpack(r, shapes) for r in res])):
        upd[name] = list(vals)

    order = ["pool_w", "pool_scale", "ssm_in_w", "ssm_conv_w", "ssm_conv_b", "ssm_dt_bias", "ssm_a_log", "ssm_d",
             "ssm_norm_w", "ssm_out_w", "mlp_w1", "mlp_w2", "ln_g", "ln_b", "ple_w", "ple_gate_w"]
    out = [loss, grad_x[None]]
    for k in range(4):
        out += [upd[name][k] for name in order]
    return tuple(out)
```

```python
import jax
import jax.numpy as jnp
from jax import lax
from jax.experimental import pallas as pl
from jax.experimental.pallas import tpu as pltpu

F32 = jnp.float32
BF16 = jnp.bfloat16
SDS = jax.ShapeDtypeStruct
MESH = pl.DeviceIdType.MESH
ANY = pl.BlockSpec(memory_space=pl.ANY)

N_DEV = 8
DEPTH = 2
ALPHA = (2.0 * DEPTH) ** 0.25
LN_EPS = 1e-5
RMS_EPS = 1e-5
POOL_WINDOW_LOG2 = (1, 2, 3, 4)
D_STATE = 128
CHUNK = 128
CONV_WIDTH = 4
ADAM_LR = 0.001
ADAM_B1 = 0.9
ADAM_B2 = 0.999
ADAM_EPS = 1e-08
ADAM_WD = 0.01
ADAM_STEP = 10

V7X_LANES = 128
V7X_VMEM_LIMIT = 48 * 1024 * 1024


def _cp(*sem):
    return pltpu.CompilerParams(dimension_semantics=sem, vmem_limit_bytes=V7X_VMEM_LIMIT)


def _pick(dim, cap):
    if dim <= cap:
        return dim
    best = None
    for t in range(V7X_LANES, cap + 1, V7X_LANES):
        if dim % t == 0:
            best = t
    assert best is not None, (dim, cap)
    return best


def _row_tile(rows, cols, itemsize=4, target=1 << 20):
    t = rows
    while t % 2 == 0 and t // 2 >= 16 and (t // 2) % 16 == 0 and t * cols * itemsize > target:
        t //= 2
    return t


def _slot(s):
    return (s % 2) * 4 + s // 2


def _all_gather(name, shards):
    n = len(shards)

    def body(*refs):
        ins, outs = refs[:n], refs[n:2 * n]
        send_sems, recv_sems, local_sems = refs[2 * n:]
        x, y, c = lax.axis_index("x"), lax.axis_index("y"), lax.axis_index("c")
        me, sibling = (x, y, c), (x, y, 1 - c)
        chips = [(1 - x, y), (x, 1 - y), (1 - x, 1 - y)]

        def copy(a, k, block, to, src=None):
            dst = outs[a].at[4 * block[0] + 2 * block[1] + block[2]]
            return pltpu.make_async_remote_copy(
                src_ref=dst if src is None else src, dst_ref=dst, send_sem=send_sems.at[a, k],
                recv_sem=recv_sems.at[a, k], device_id=to, device_id_type=MESH)

        mine = [pltpu.make_async_copy(ins[a], outs[a].at[4 * x + 2 * y + c], local_sems.at[a]) for a in range(n)]
        for cp in mine:
            cp.start()
        first = []
        for a in range(n):
            first.append(copy(a, 0, me, sibling, src=ins[a]))
            first += [copy(a, 1 + j, me, (*chip, c), src=ins[a]) for j, chip in enumerate(chips)]
        for cp in first:
            cp.start()
        passed = []
        for j, chip in enumerate(chips):
            for a in range(n):
                copy(a, 1 + j, (*chip, c), me).wait_recv()
                fwd = copy(a, 4 + j, (*chip, c), sibling)
                fwd.start()
                passed.append(fwd)
        for a in range(n):
            copy(a, 0, sibling, me).wait_recv()
            for j, chip in enumerate(chips):
                copy(a, 4 + j, (*chip, 1 - c), me).wait_recv()
        for cp in first + passed:
            cp.wait_send()
        for cp in mine:
            cp.wait()

    return pl.pallas_call(
        body, name=name,
        out_shape=[SDS((N_DEV,) + s.shape, s.dtype) for s in shards],
        in_specs=[ANY] * n, out_specs=[ANY] * n,
        scratch_shapes=[pltpu.SemaphoreType.DMA((n, 7)), pltpu.SemaphoreType.DMA((n, 7)),
                        pltpu.SemaphoreType.DMA((n,))],
    )(*shards)


def _rs_sibling_exchange(name, grads):
    n = len(grads)

    def body(*refs):
        ins, outs = refs[:n], refs[n:2 * n]
        send_sems, recv_sems = refs[2 * n:]
        x, y, c = lax.axis_index("x"), lax.axis_index("y"), lax.axis_index("c")
        remote = [pltpu.make_async_remote_copy(
            src_ref=ins[a].at[pl.ds(4 * (1 - c), 4)], dst_ref=outs[a], send_sem=send_sems.at[a],
            recv_sem=recv_sems.at[a], device_id=(x, y, 1 - c), device_id_type=MESH) for a in range(n)]
        for cp in remote:
            cp.start()
        for cp in remote:
            cp.wait()

    return pl.pallas_call(
        body, name=name,
        out_shape=[SDS((4,) + g.shape[1:], g.dtype) for g in grads],
        in_specs=[ANY] * n, out_specs=[ANY] * n,
        scratch_shapes=[pltpu.SemaphoreType.DMA((n,)), pltpu.SemaphoreType.DMA((n,))],
    )(*grads)


def _rs_chip_exchange(name, sums):
    n = len(sums)

    def body(*refs):
        ins, outs = refs[:n], refs[n:2 * n]
        send_sems, recv_sems = refs[2 * n:]
        x, y, c = lax.axis_index("x"), lax.axis_index("y"), lax.axis_index("c")
        chips = [(1 - x, y), (x, 1 - y), (1 - x, 1 - y)]
        remote = []
        for a in range(n):
            for j, (px, py) in enumerate(chips):
                remote.append(pltpu.make_async_remote_copy(
                    src_ref=ins[a].at[2 * px + py], dst_ref=outs[a].at[j], send_sem=send_sems.at[a, j],
                    recv_sem=recv_sems.at[a, j], device_id=(px, py, c), device_id_type=MESH))
        for cp in remote:
            cp.start()
        for cp in remote:
            cp.wait()

    return pl.pallas_call(
        body, name=name,
        out_shape=[SDS((3,) + s.shape[1:], s.dtype) for s in sums],
        in_specs=[ANY] * n, out_specs=[ANY] * n,
        scratch_shapes=[pltpu.SemaphoreType.DMA((n, 3)), pltpu.SemaphoreType.DMA((n, 3))],
    )(*sums)


def _mm_core(name, a, b, *, grid, a_spec, b_spec, dims, acc_shape, outs, out_spec, epilogue=None, extras=(),
             extra_specs=()):
    nk = grid[2]
    ne, no = len(extras), len(outs)

    def body(a_ref, b_ref, *rest):
        e_refs, o_refs, acc = rest[:ne], rest[ne:ne + no], rest[ne + no]
        k = pl.program_id(2)

        @pl.when(k == 0)
        def _():
            acc[...] = jnp.zeros_like(acc)

        acc[...] += lax.dot_general(a_ref[...].astype(BF16), b_ref[...].astype(BF16), dims,
                                    preferred_element_type=F32)

        @pl.when(k == nk - 1)
        def _():
            r = acc[...]
            vals = epilogue(r, *[e[...] for e in e_refs]) if epilogue is not None else (r,)
            for o, v in zip(o_refs, vals):
                o[...] = v.astype(o.dtype)

    res = pl.pallas_call(
        body, name=name, grid=grid, out_shape=list(outs),
        in_specs=[a_spec, b_spec, *extra_specs], out_specs=[out_spec] * no,
        scratch_shapes=[pltpu.VMEM(acc_shape, F32)],
        compiler_params=_cp("parallel", "parallel", "arbitrary"),
    )(a, b, *extras)
    return res


NN = (((1,), (0,)), ((), ()))
NT = (((1,), (1,)), ((), ()))
TN = (((0,), (0,)), ((), ()))


def _w_dims(w, kind):
    if kind == "col":
        return w.shape[1], N_DEV * w.shape[2], w.shape[1], w.shape[2]
    if kind == "row":
        return N_DEV * w.shape[1], w.shape[2], w.shape[1], w.shape[2]
    return w.shape[0], w.shape[1], w.shape[0], w.shape[1]


def _mm_fwd(name, a, w, kind, out_dtypes, epilogue=None, extras=()):
    m, k_dim = a.shape
    kk, n, ks, ns = _w_dims(w, kind)
    assert kk == k_dim
    tm = _pick(m, 1024)
    if kind == "col":
        tn, tk = _pick(ns, 1024), _pick(kk, 512)
        nb = ns // tn
        b_spec = pl.BlockSpec((None, tk, tn), lambda i, j, k: (j // nb, k, j % nb))
    elif kind == "row":
        tn, tk = _pick(n, 1024), _pick(ks, 512)
        kb = ks // tk
        b_spec = pl.BlockSpec((None, tk, tn), lambda i, j, k: (k // kb, k % kb, j))
    else:
        tn, tk = _pick(n, 1152), _pick(kk, 512)
        b_spec = pl.BlockSpec((tk, tn), lambda i, j, k: (k, j))
    mn_spec = pl.BlockSpec((tm, tn), lambda i, j, k: (i, j))
    return _mm_core(
        name, a, w, grid=(m // tm, n // tn, kk // tk),
        a_spec=pl.BlockSpec((tm, tk), lambda i, j, k: (i, k)), b_spec=b_spec, dims=NN, acc_shape=(tm, tn),
        outs=[SDS((m, n), dt) for dt in out_dtypes], out_spec=mn_spec, epilogue=epilogue, extras=extras,
        extra_specs=[mn_spec] * len(extras))


def _mm_dx(name, dy, w, kind, out_dtypes, epilogue=None, extras=()):
    m, n_dim = dy.shape
    kk, n, ks, ns = _w_dims(w, kind)
    assert n == n_dim
    tm = _pick(m, 1024)
    if kind == "col":
        tn, tk = _pick(kk, 1024), _pick(ns, 512)
        kb = ns // tk
        b_spec = pl.BlockSpec((None, tn, tk), lambda i, j, k: (k // kb, j, k % kb))
    elif kind == "row":
        tn, tk = _pick(ks, 1024), _pick(n, 512)
        nb = ks // tn
        b_spec = pl.BlockSpec((None, tn, tk), lambda i, j, k: (j // nb, j % nb, k))
    else:
        tn, tk = _pick(kk, 1024), _pick(n, 1152)
        b_spec = pl.BlockSpec((tn, tk), lambda i, j, k: (j, k))
    mk_spec = pl.BlockSpec((tm, tn), lambda i, j, k: (i, j))
    return _mm_core(
        name, dy, w, grid=(m // tm, kk // tn, n // tk),
        a_spec=pl.BlockSpec((tm, tk), lambda i, j, k: (i, k)), b_spec=b_spec, dims=NT, acc_shape=(tm, tn),
        outs=[SDS((m, kk), dt) for dt in out_dtypes], out_spec=mk_spec, epilogue=epilogue, extras=extras,
        extra_specs=[mk_spec] * len(extras))


def _mm_dw(name, a, dy, kind):
    m, kk = a.shape
    n = dy.shape[1]
    tk = _pick(m, 512)
    if kind == "col":
        ns = n // N_DEV
        tm, tn = _pick(kk, 1024), _pick(ns, 1024)
        nb = ns // tn
        out = SDS((N_DEV, kk, ns), BF16)
        out_spec = pl.BlockSpec((None, tm, tn), lambda i, j, k: (_slot(j // nb), i, j % nb))
    elif kind == "row":
        ks = kk // N_DEV
        tm, tn = _pick(ks, 1024), _pick(n, 1024)
        mb = ks // tm
        out = SDS((N_DEV, ks, n), BF16)
        out_spec = pl.BlockSpec((None, tm, tn), lambda i, j, k: (_slot(i // mb), i % mb, j))
    else:
        tm, tn = _pick(kk, 1024), _pick(n, 1152)
        out = SDS((kk, n), BF16)
        out_spec = pl.BlockSpec((tm, tn), lambda i, j, k: (i, j))
    return _mm_core(
        name, a, dy, grid=(kk // tm, n // tn, m // tk),
        a_spec=pl.BlockSpec((tk, tm), lambda i, j, k: (k, i)), b_spec=pl.BlockSpec((tk, tn), lambda i, j, k: (k, j)),
        dims=TN, acc_shape=(tm, tn), outs=[out], out_spec=out_spec)[0]


def _rowwise(name, fn, ins, outs, rows, tile):
    arrays, specs = [], []
    for arr, kind in ins:
        arrays.append(arr)
        if kind == "row":
            specs.append(pl.BlockSpec((tile, arr.shape[1]), lambda i: (i, 0)))
        elif kind == "vec":
            specs.append(pl.BlockSpec(arr.shape, lambda i, nd=arr.ndim: (0,) * nd))
        else:
            specs.append(kind)
    out_shapes, out_specs, kinds = [], [], []
    for cols, dt, kind in outs:
        kinds.append(kind)
        if kind == "row":
            out_shapes.append(SDS((rows, cols), dt))
            out_specs.append(pl.BlockSpec((tile, cols), lambda i: (i, 0)))
        else:
            out_shapes.append(SDS((1, cols), F32))
            out_specs.append(pl.BlockSpec((1, cols), lambda i: (0, 0)))
    ni = len(arrays)
    has_acc = "acc" in kinds

    def body(*refs):
        vals = fn(*[r[...] for r in refs[:ni]])
        i = pl.program_id(0)
        for o, v, kind in zip(refs[ni:], vals, kinds):
            if kind == "row":
                o[...] = v.astype(o.dtype)
            else:
                @pl.when(i == 0)
                def _(o=o):
                    o[...] = jnp.zeros_like(o)

                o[...] += v

    return pl.pallas_call(
        body, name=name, grid=(rows // tile,), out_shape=out_shapes, in_specs=specs, out_specs=out_specs,
        compiler_params=_cp("arbitrary" if has_acc else "parallel"),
    )(*arrays)


def _ln_fwd(name, u, g, b):
    d = u.shape[1]

    def fn(u, g, b):
        mu = jnp.mean(u, axis=1, keepdims=True)
        xc = u - mu
        var = jnp.mean(xc * xc, axis=1, keepdims=True)
        return (xc * lax.rsqrt(var + LN_EPS) * g + b,)

    return _rowwise(name, fn, [(u, "row"), (g, "vec"), (b, "vec")], [(d, F32, "row")], u.shape[0], 256)[0]


def _ln_bwd(name, u, dy, g):
    d = u.shape[1]

    def fn(u, dy, g):
        mu = jnp.mean(u, axis=1, keepdims=True)
        xc = u - mu
        var = jnp.mean(xc * xc, axis=1, keepdims=True)
        rstd = lax.rsqrt(var + LN_EPS)
        xhat = xc * rstd
        dxhat = dy * g
        m1 = jnp.mean(dxhat, axis=1, keepdims=True)
        m2 = jnp.mean(dxhat * xhat, axis=1, keepdims=True)
        du = rstd * (dxhat - m1 - xhat * m2)
        return du, jnp.sum(dy * xhat, axis=0, keepdims=True), jnp.sum(dy, axis=0, keepdims=True)

    return _rowwise(name, fn, [(u, "row"), (dy, "row"), (g, "vec")],
                    [(d, F32, "row"), (d, F32, "acc"), (d, F32, "acc")], u.shape[0], 256)


def _loss_bwd(name, y, target):
    d = y.shape[1]

    def fn(y, t):
        e = y - t
        return e * (1.0 / d), jnp.sum(e * e, axis=0, keepdims=True) * (0.5 / d)

    return _rowwise(name, fn, [(y, "row"), (target, "row")], [(d, F32, "row"), (d, F32, "acc")], y.shape[0], 256)


def _ple_bwd(name, dx, e, gate):
    d = dx.shape[1]

    def fn(dx, e, gate):
        return dx * e * gate * (1.0 - gate), dx * gate

    return _rowwise(name, fn, [(dx, "row"), (e, "row"), (gate, "row")], [(d, BF16, "row"), (d, BF16, "row")],
                    dx.shape[0], 256)


def _sigmoid(v):
    return 1.0 / (1.0 + jnp.exp(-v))


def _gated_rms_fwd(name, y, zx, norm_w):
    di = y.shape[1]

    def fn(y, z, w):
        yg = y * (z * _sigmoid(z))
        r = lax.rsqrt(jnp.mean(yg * yg, axis=1, keepdims=True) + RMS_EPS)
        return (yg * r * w,)

    z_spec = pl.BlockSpec((128, di), lambda i: (i, 0))
    return _rowwise(name, fn, [(y, "row"), (zx, z_spec), (norm_w, "vec")], [(di, BF16, "row")], y.shape[0], 128)[0]


def _gated_rms_bwd(name, y, zx, norm_w, dout):
    di = y.shape[1]

    def fn(y, z, w, dout):
        sg = _sigmoid(z)
        sz = z * sg
        yg = y * sz
        r = lax.rsqrt(jnp.mean(yg * yg, axis=1, keepdims=True) + RMS_EPS)
        dn = dout * w
        dyg = r * (dn - yg * (r * r) * jnp.mean(dn * yg, axis=1, keepdims=True))
        dy = dyg * sz
        dz = dyg * y * (sg * (1.0 + z * (1.0 - sg)))
        return dy, dz, jnp.sum(dout * yg * r, axis=0, keepdims=True)

    z_spec = pl.BlockSpec((128, di), lambda i: (i, 0))
    return _rowwise(name, fn, [(y, "row"), (zx, z_spec), (norm_w, "vec"), (dout, "row")],
                    [(di, F32, "row"), (di, BF16, "row"), (di, F32, "acc")], y.shape[0], 128)


def _shift_down(v, j, row):
    return jnp.where(row >= j, pltpu.roll(v, j, 0), 0.0)


def _shift_up(v, j, row):
    t = v.shape[0]
    return jnp.where(row < t - j, pltpu.roll(v, t - j, 0), 0.0)


def _pool_select(parts, g):
    return jnp.where(g == 0, parts[0], jnp.where(g == 1, parts[1], jnp.where(g == 2, parts[2], parts[3])))


def _pool_windows(name, x, transpose, scale_by=None):
    t, d = x.shape
    cg = d // 4
    cw = V7X_LANES
    per = cg // cw

    def body(*refs):
        x_ref, o_ref = refs[0], refs[-1]
        g = pl.program_id(0) // per
        xv = x_ref[...]
        row = lax.broadcasted_iota(jnp.int32, (t, 1), 0)
        cnt = jnp.minimum(row + 1, jnp.left_shift(2, g)).astype(F32)
        s = xv / cnt if transpose else xv
        parts = []
        for lg in POOL_WINDOW_LOG2:
            j = 1 << (lg - 1)
            s = s + (_shift_up(s, j, row) if transpose else _shift_down(s, j, row))
            parts.append(s)
        sel = _pool_select(parts, g)
        if transpose:
            o_ref[...] = ALPHA * refs[1][...] + sel - xv
        else:
            o_ref[...] = (sel / cnt - xv).astype(o_ref.dtype)

    col = pl.BlockSpec((t, cw), lambda j: (0, j))
    ins = [x] if scale_by is None else [x, scale_by]
    return pl.pallas_call(
        body, name=name, grid=(d // cw,), out_shape=SDS((t, d), F32 if transpose else BF16),
        in_specs=[col] * len(ins), out_specs=col, compiler_params=_cp("parallel"),
    )(*ins)


def _pool_mm(name, pooled, w, scale, x):
    t, d = x.shape
    cg = d // 4
    tm = _pick(t, 1024)

    def body(p_ref, w_ref, s_ref, x_ref, u_ref, h_ref):
        h = jnp.dot(p_ref[...], w_ref[...], preferred_element_type=F32)
        h_ref[...] = h
        u_ref[...] = ALPHA * x_ref[...] + h * s_ref[...]

    blk = pl.BlockSpec((tm, cg), lambda g, i: (i, g))
    return pl.pallas_call(
        body, name=name, grid=(4, t // tm), out_shape=[SDS((t, d), F32), SDS((t, d), F32)],
        in_specs=[blk, pl.BlockSpec((None, cg, cg), lambda g, i: (g, 0, 0)), pl.BlockSpec((1, cg), lambda g, i: (0, g)),
                  blk],
        out_specs=[blk, blk], compiler_params=_cp("parallel", "parallel"),
    )(pooled, w, scale, x)


def _pool_bwd_mm(name, du, hraw, w, scale):
    t, d = du.shape
    cg = d // 4
    tm = _pick(t, 1024)

    def body(du_ref, h_ref, w_ref, s_ref, dh_ref, dp_ref, ds_ref):
        @pl.when(pl.program_id(1) == 0)
        def _():
            ds_ref[...] = jnp.zeros_like(ds_ref)

        duv = du_ref[...]
        ds_ref[...] += jnp.sum(duv * h_ref[...], axis=0, keepdims=True)
        dh = (duv * s_ref[...]).astype(BF16)
        dh_ref[...] = dh
        dp_ref[...] = lax.dot_general(dh, w_ref[...], NT, preferred_element_type=F32)

    blk = pl.BlockSpec((tm, cg), lambda g, i: (i, g))
    vec = pl.BlockSpec((1, cg), lambda g, i: (0, g))
    return pl.pallas_call(
        body, name=name, grid=(4, t // tm), out_shape=[SDS((t, d), BF16), SDS((t, d), F32), SDS((1, d), F32)],
        in_specs=[blk, blk, pl.BlockSpec((None, cg, cg), lambda g, i: (g, 0, 0)), vec],
        out_specs=[blk, blk, vec], compiler_params=_cp("parallel", "arbitrary"),
    )(du, hraw, w, scale)


def _pool_dw(name, pooled, dh):
    t, d = pooled.shape
    cg = d // 4
    tk = _pick(t, 512)
    nk = t // tk

    def body(p_ref, dh_ref, o_ref, acc):
        k = pl.program_id(1)

        @pl.when(k == 0)
        def _():
            acc[...] = jnp.zeros_like(acc)

        acc[...] += lax.dot_general(p_ref[...], dh_ref[...], TN, preferred_element_type=F32)

        @pl.when(k == nk - 1)
        def _():
            o_ref[...] = acc[...].astype(o_ref.dtype)

    blk = pl.BlockSpec((tk, cg), lambda g, k: (k, g))
    return pl.pallas_call(
        body, name=name, grid=(4, nk), out_shape=SDS((4, cg, cg), BF16), in_specs=[blk, blk],
        out_specs=pl.BlockSpec((None, cg, cg), lambda g, k: (g, 0, 0)), scratch_shapes=[pltpu.VMEM((cg, cg), F32)],
        compiler_params=_cp("parallel", "arbitrary"),
    )(pooled, dh)


def _conv_pre(u, w_ref, b_ref, row):
    pre = b_ref[...] + _shift_down(u, 3, row) * w_ref[0:1, :]
    pre = pre + _shift_down(u, 2, row) * w_ref[1:2, :]
    pre = pre + _shift_down(u, 1, row) * w_ref[2:3, :]
    return pre + u * w_ref[3:4, :]


def _conv_fwd(name, zx, conv_w, conv_b, di):
    t = zx.shape[0]
    cd = conv_w.shape[1]
    cw = _pick(cd, 256)
    off = di // cw

    def body(u_ref, w_ref, b_ref, o_ref):
        row = lax.broadcasted_iota(jnp.int32, (t, 1), 0)
        pre = _conv_pre(u_ref[...], w_ref, b_ref, row)
        o_ref[...] = pre * _sigmoid(pre)

    return pl.pallas_call(
        body, name=name, grid=(cd // cw,), out_shape=SDS((t, cd), F32),
        in_specs=[pl.BlockSpec((t, cw), lambda j: (0, off + j)), pl.BlockSpec((CONV_WIDTH, cw), lambda j: (0, j)),
                  pl.BlockSpec((1, cw), lambda j: (0, j))],
        out_specs=pl.BlockSpec((t, cw), lambda j: (0, j)), compiler_params=_cp("parallel"),
    )(zx, conv_w, conv_b)


def _conv_bwd(name, zx, conv_w, conv_b, dact, di):
    t = zx.shape[0]
    cd = conv_w.shape[1]
    cw = _pick(cd, 256)
    off = di // cw

    def body(u_ref, w_ref, b_ref, da_ref, du_ref, dw_ref, db_ref):
        row = lax.broadcasted_iota(jnp.int32, (t, 1), 0)
        u = u_ref[...]
        pre = _conv_pre(u, w_ref, b_ref, row)
        sg = _sigmoid(pre)
        dpre = da_ref[...] * (sg * (1.0 + pre * (1.0 - sg)))
        du = dpre * w_ref[3:4, :]
        for j in (1, 2, 3):
            du = du + _shift_up(dpre, j, row) * w_ref[3 - j:4 - j, :]
            dw_ref[3 - j:4 - j, :] = jnp.sum(dpre * _shift_down(u, j, row), axis=0, keepdims=True)
        dw_ref[3:4, :] = jnp.sum(dpre * u, axis=0, keepdims=True)
        db_ref[...] = jnp.sum(dpre, axis=0, keepdims=True)
        du_ref[...] = du.astype(du_ref.dtype)

    wspec = pl.BlockSpec((CONV_WIDTH, cw), lambda j: (0, j))
    bspec = pl.BlockSpec((1, cw), lambda j: (0, j))
    ospec = pl.BlockSpec((t, cw), lambda j: (0, j))
    return pl.pallas_call(
        body, name=name, grid=(cd // cw,), out_shape=[SDS((t, cd), BF16), SDS((CONV_WIDTH, cd), F32), SDS((1, cd), F32)],
        in_specs=[pl.BlockSpec((t, cw), lambda j: (0, off + j)), wspec, bspec, ospec],
        out_specs=[ospec, wspec, bspec], compiler_params=_cp("parallel"),
    )(zx, conv_w, conv_b, dact)


def _softplus(v):
    return jnp.maximum(v, 0.0) + jnp.log(1.0 + jnp.exp(-jnp.abs(v)))


def _dt_fwd(name, zx, bias, a_log, col_block):
    t = zx.shape[0]

    def body(r_ref, b_ref, al_ref, dt_ref, acs_ref):
        row = lax.broadcasted_iota(jnp.int32, (t, 1), 0) % CHUNK
        dt = _softplus(r_ref[...] + b_ref[...])
        s = dt * (-jnp.exp(al_ref[...]))
        j = 1
        while j < CHUNK:
            s = s + jnp.where(row >= j, pltpu.roll(s, j, 0), 0.0)
            j *= 2
        dt_ref[...] = dt
        acs_ref[...] = s

    vec = pl.BlockSpec((1, V7X_LANES), lambda i: (0, 0))
    full = pl.BlockSpec((t, V7X_LANES), lambda i: (0, 0))
    return pl.pallas_call(
        body, name=name, grid=(1,), out_shape=[SDS((t, V7X_LANES), F32)] * 2,
        in_specs=[pl.BlockSpec((t, V7X_LANES), lambda i: (0, col_block)), vec, vec], out_specs=[full, full],
        compiler_params=_cp("arbitrary"),
    )(zx, bias, a_log)


def _dt_bwd(name, zx, bias, a_log, d_acs, d_dt, col_block):
    t = zx.shape[0]

    def body(r_ref, b_ref, al_ref, da_ref, dd_ref, draw_ref, db_ref, dal_ref):
        row = lax.broadcasted_iota(jnp.int32, (t, 1), 0) % CHUNK
        pre = r_ref[...] + b_ref[...]
        dt = _softplus(pre)
        a = -jnp.exp(al_ref[...])
        s = da_ref[...]
        j = 1
        while j < CHUNK:
            s = s + jnp.where(row < CHUNK - j, pltpu.roll(s, t - j, 0), 0.0)
            j *= 2
        ddt = dd_ref[...] + s * a
        dal_ref[...] = jnp.sum(s * dt, axis=0, keepdims=True) * a
        draw = ddt * _sigmoid(pre)
        db_ref[...] = jnp.sum(draw, axis=0, keepdims=True)
        draw_ref[...] = draw.astype(draw_ref.dtype)

    vec = pl.BlockSpec((1, V7X_LANES), lambda i: (0, 0))
    full = pl.BlockSpec((t, V7X_LANES), lambda i: (0, 0))
    return pl.pallas_call(
        body, name=name, grid=(1,), out_shape=[SDS((t, V7X_LANES), BF16), SDS((1, V7X_LANES), F32), SDS((1, V7X_LANES), F32)],
        in_specs=[pl.BlockSpec((t, V7X_LANES), lambda i: (0, col_block)), vec, vec, full, full],
        out_specs=[full, vec, vec], compiler_params=_cp("arbitrary"),
    )(zx, bias, a_log, d_acs, d_dt)


def _ssd_specs(t, di, g_n, hpg, p, rev):
    nc = t // CHUNK
    w = hpg * p
    nb = di // D_STATE

    def cc(c):
        return nc - 1 - c if rev else c

    return dict(
        xs=pl.BlockSpec((CHUNK, w), lambda g, c: (cc(c), g)),
        bm=pl.BlockSpec((CHUNK, D_STATE), lambda g, c: (cc(c), nb + g)),
        cm=pl.BlockSpec((CHUNK, D_STATE), lambda g, c: (cc(c), nb + g_n + g)),
        col=pl.BlockSpec((None, CHUNK, hpg), lambda g, c: (g, cc(c), 0)),
        rowv=pl.BlockSpec((None, hpg, CHUNK), lambda g, c: (g, 0, cc(c))),
        head=pl.BlockSpec((None, 1, hpg), lambda g, c: (g, 0, 0)),
        y=pl.BlockSpec((CHUNK, w), lambda g, c: (cc(c), g)),
        bc=pl.BlockSpec((CHUNK, D_STATE), lambda g, c: (cc(c), g)),
        prev=pl.BlockSpec((None, hpg, D_STATE, p), lambda g, c: (cc(c), g, 0, 0)),
    )


def _ssd_fwd(name, xbc, dt_col, a_col, a_row, d_skip, di, g_n, hpg, p):
    t = xbc.shape[0]
    nc = t // CHUNK
    sp = _ssd_specs(t, di, g_n, hpg, p, False)

    def body(xs_ref, bm_ref, cm_ref, dt_ref, ac_ref, ar_ref, d_ref, y_ref, prev_ref, h_ref):
        @pl.when(pl.program_id(1) == 0)
        def _():
            h_ref[...] = jnp.zeros_like(h_ref)

        bm = bm_ref[...].astype(BF16)
        cm = cm_ref[...].astype(BF16)
        cb = lax.dot_general(cm, bm, NT, preferred_element_type=F32)
        li = lax.broadcasted_iota(jnp.int32, (CHUNK, CHUNK), 0)
        si = lax.broadcasted_iota(jnp.int32, (CHUNK, CHUNK), 1)
        for hh in range(hpg):
            sl = slice(hh * p, (hh + 1) * p)
            x_h = xs_ref[:, sl]
            ac = ac_ref[:, hh:hh + 1]
            ar = ar_ref[hh:hh + 1, :]
            a_last = ar[:, CHUNK - 1:CHUNK]
            lm = jnp.exp(jnp.where(li >= si, ac - ar, -jnp.inf))
            xdt = x_h * dt_ref[:, hh:hh + 1]
            y = jnp.dot((cb * lm).astype(BF16), xdt.astype(BF16), preferred_element_type=F32)
            h_prev = h_ref[hh]
            prev_ref[hh] = h_prev
            y = y + jnp.dot(cm, h_prev.astype(BF16), preferred_element_type=F32) * jnp.exp(ac)
            y_ref[:, sl] = y + x_h * d_ref[:, hh:hh + 1]
            st = lax.dot_general(bm, (xdt * jnp.exp(a_last - ac)).astype(BF16), TN, preferred_element_type=F32)
            h_ref[hh] = jnp.exp(a_last) * h_prev + st

    return pl.pallas_call(
        body, name=name, grid=(g_n, nc),
        out_shape=[SDS((t, di), F32), SDS((nc, g_n * hpg, D_STATE, p), F32)],
        in_specs=[sp["xs"], sp["bm"], sp["cm"], sp["col"], sp["col"], sp["rowv"], sp["head"]],
        out_specs=[sp["y"], sp["prev"]], scratch_shapes=[pltpu.VMEM((hpg, D_STATE, p), F32)],
        compiler_params=_cp("parallel", "arbitrary"),
    )(xbc, xbc, xbc, dt_col, a_col, a_row, d_skip)


def _ssd_bwd(name, xbc, dt_col, a_col, a_row, d_skip, prev, dy, di, g_n, hpg, p):
    t = xbc.shape[0]
    nc = t // CHUNK
    sp = _ssd_specs(t, di, g_n, hpg, p, True)

    def body(xs_ref, bm_ref, cm_ref, dt_ref, ac_ref, ar_ref, d_ref, prev_ref, dy_ref,
             dx_ref, dbm_ref, dcm_ref, ddt_ref, dacs_ref, dd_ref, dh_ref):
        @pl.when(pl.program_id(1) == 0)
        def _():
            dh_ref[...] = jnp.zeros_like(dh_ref)
            dd_ref[...] = jnp.zeros_like(dd_ref)

        bm = bm_ref[...].astype(BF16)
        cm = cm_ref[...].astype(BF16)
        cb = lax.dot_general(cm, bm, NT, preferred_element_type=F32)
        li = lax.broadcasted_iota(jnp.int32, (CHUNK, CHUNK), 0)
        si = lax.broadcasted_iota(jnp.int32, (CHUNK, CHUNK), 1)
        last_row = lax.broadcasted_iota(jnp.int32, (CHUNK, 1), 0) == CHUNK - 1
        d_cb = jnp.zeros((CHUNK, CHUNK), F32)
        d_bm = jnp.zeros((CHUNK, D_STATE), F32)
        d_cm = jnp.zeros((CHUNK, D_STATE), F32)
        for hh in range(hpg):
            sl = slice(hh * p, (hh + 1) * p)
            x_h = xs_ref[:, sl]
            dt_h = dt_ref[:, hh:hh + 1]
            ac = ac_ref[:, hh:hh + 1]
            ar = ar_ref[hh:hh + 1, :]
            a_last = ar[:, CHUNK - 1:CHUNK]
            e_in = jnp.exp(ac)
            e_out = jnp.exp(a_last - ac)
            e_all = jnp.exp(a_last)
            lm = jnp.exp(jnp.where(li >= si, ac - ar, -jnp.inf))
            mm = cb * lm
            xdt = x_h * dt_h
            xdt_b = xdt.astype(BF16)
            dy_h = dy_ref[:, sl]
            dy_b = dy_h.astype(BF16)
            h_prev = h_prev_f = prev_ref[hh]
            h_prev = h_prev.astype(BF16)
            dh_next = dh_ref[hh]
            dh_next_b = dh_next.astype(BF16)
            ch = jnp.dot(cm, h_prev, preferred_element_type=F32)
            dy_e = (dy_h * e_in).astype(BF16)
            d_cm = d_cm + lax.dot_general(dy_e, h_prev, NT, preferred_element_type=F32)
            d_a = jnp.sum(dy_h * ch, axis=1, keepdims=True) * e_in
            dh_ref[hh] = e_all * dh_next + lax.dot_general(cm, dy_e, TN, preferred_element_type=F32)
            d_m = lax.dot_general(dy_b, xdt_b, NT, preferred_element_type=F32)
            d_xdt = lax.dot_general(mm.astype(BF16), dy_b, TN, preferred_element_type=F32)
            wm = d_m * mm
            d_a = d_a + jnp.sum(wm, axis=1, keepdims=True) - jnp.sum(wm.T, axis=1, keepdims=True)
            d_cb = d_cb + d_m * lm
            q = jnp.dot(bm, dh_next_b, preferred_element_type=F32)
            d_xdt = d_xdt + q * e_out
            d_f = jnp.sum(q * xdt, axis=1, keepdims=True) * e_out
            d_last = jnp.sum(d_f, axis=0, keepdims=True) + e_all * jnp.sum(
                jnp.sum(dh_next * h_prev_f, axis=1, keepdims=True), axis=0, keepdims=True)
            d_a = d_a - d_f + jnp.where(last_row, d_last, 0.0)
            d_bm = d_bm + lax.dot_general((xdt * e_out).astype(BF16), dh_next_b, NT, preferred_element_type=F32)
            d_skip = d_ref[:, hh:hh + 1]
            dx_ref[:, sl] = d_xdt * dt_h + dy_h * d_skip
            ddt_ref[:, hh:hh + 1] = jnp.sum(d_xdt * x_h, axis=1, keepdims=True)
            dacs_ref[:, hh:hh + 1] = d_a
            dd_ref[:, hh:hh + 1] += jnp.sum(jnp.sum(dy_h * x_h, axis=1, keepdims=True), axis=0, keepdims=True)
        d_cb_b = d_cb.astype(BF16)
        dcm_ref[...] = d_cm + jnp.dot(d_cb_b, bm, preferred_element_type=F32)
        dbm_ref[...] = d_bm + lax.dot_general(d_cb_b, cm, TN, preferred_element_type=F32)

    gn = g_n * D_STATE
    return pl.pallas_call(
        body, name=name, grid=(g_n, nc),
        out_shape=[SDS((t, di), F32), SDS((t, gn), F32), SDS((t, gn), F32), SDS((g_n, t, hpg), F32),
                   SDS((g_n, t, hpg), F32), SDS((g_n, 1, hpg), F32)],
        in_specs=[sp["xs"], sp["bm"], sp["cm"], sp["col"], sp["col"], sp["rowv"], sp["head"], sp["prev"], sp["y"]],
        out_specs=[sp["y"], sp["bc"], sp["bc"], sp["col"], sp["col"], sp["head"]],
        scratch_shapes=[pltpu.VMEM((hpg, D_STATE, p), F32)],
        compiler_params=_cp("parallel", "arbitrary"),
    )(xbc, xbc, xbc, dt_col, a_col, a_row, d_skip, prev, dy)


def _as3d(a):
    return a.reshape(a.shape[0], -1, a.shape[-1])


def _pair_sum(name, own, recv, core):
    shape = recv.shape
    cols = shape[-1]
    own3, recv3 = own.reshape(8, -1, cols), recv.reshape(4, -1, cols)
    rows = recv3.shape[1]
    tr = _row_tile(rows, cols, 2)

    def body(c_ref, a_ref, b_ref, o_ref):
        o_ref[...] = (a_ref[...].astype(F32) + b_ref[...].astype(F32)).astype(o_ref.dtype)

    blk = pl.BlockSpec((None, tr, cols), lambda q, i, c_ref: (q, i, 0))
    out = pl.pallas_call(
        body, name=name, out_shape=SDS(recv3.shape, recv.dtype),
        grid_spec=pltpu.PrefetchScalarGridSpec(
            num_scalar_prefetch=1, grid=(4, rows // tr),
            in_specs=[pl.BlockSpec((None, tr, cols), lambda q, i, c_ref: (4 * c_ref[0] + q, i, 0)), blk], out_specs=blk),
        compiler_params=_cp("parallel", "parallel"),
    )(core, own3, recv3)
    return out.reshape(shape)


def _adamw(name, w, m, v, parts, layer, prev=None, sel=None):
    lyr, rows, cols = w.shape
    n = len(parts)
    tr = _row_tile(rows, cols)
    np_ = 0 if prev is None else 4
    if sel is None:
        sel = jnp.zeros((1,), jnp.int32)

    def body(sel_ref, *refs):
        w_ref, m_ref, v_ref = refs[:3]
        p_refs = refs[3:3 + n]
        g_ref, d_ref, nm_ref, nv_ref = refs[3 + n + np_:]
        g = p_refs[0][...].astype(F32)
        for r in p_refs[1:]:
            g = g + r[...].astype(F32)
        nm = ADAM_B1 * m_ref[...] + (1.0 - ADAM_B1) * g
        nv = ADAM_B2 * v_ref[...] + (1.0 - ADAM_B2) * (g * g)
        m_hat = nm / (1.0 - ADAM_B1 ** ADAM_STEP)
        v_hat = nv / (1.0 - ADAM_B2 ** ADAM_STEP)
        g_ref[...] = g
        d_ref[...] = -ADAM_LR * (m_hat / (jnp.sqrt(v_hat) + ADAM_EPS) + ADAM_WD * w_ref[...])
        nm_ref[...] = nm
        nv_ref[...] = nv

    lspec = pl.BlockSpec((None, tr, cols), lambda i, s: (layer, i, 0))
    pspecs = [pl.BlockSpec((None, tr, cols), (lambda i, s: (s[0], i, 0)) if q is None else (lambda i, s, q=q: (q, i, 0)))
              for _, q in parts]
    aliases = {} if prev is None else {4 + n + q: q for q in range(4)}
    return pl.pallas_call(
        body, name=name, out_shape=[SDS(w.shape, F32)] * 4,
        grid_spec=pltpu.PrefetchScalarGridSpec(
            num_scalar_prefetch=1, grid=(rows // tr,), in_specs=[lspec] * 3 + pspecs + [ANY] * np_,
            out_specs=[lspec] * 4),
        input_output_aliases=aliases, compiler_params=_cp("parallel"),
    )(sel, w, m, v, *[arr for arr, _ in parts], *(prev or ()))


def _sum8(name, parts):
    rows = parts.shape[1]

    def body(p_ref, o_ref):
        s = p_ref[0]
        for q in range(1, N_DEV):
            s = s + p_ref[q]
        o_ref[...] = s

    return pl.pallas_call(
        body, name=name, grid=(1,), out_shape=SDS((rows, V7X_LANES), F32),
        in_specs=[pl.BlockSpec((N_DEV, rows, V7X_LANES), lambda i: (0, 0, 0))],
        out_specs=pl.BlockSpec((rows, V7X_LANES), lambda i: (0, 0)), compiler_params=_cp("arbitrary"),
    )(parts)


def _pack(vectors, align):
    flat = jnp.concatenate([v.reshape(-1) for v in vectors])
    pad = (-flat.shape[0]) % align
    if pad:
        flat = jnp.concatenate([flat, jnp.zeros((pad,), F32)])
    return flat.reshape(-1, V7X_LANES)


def _unpack(packed, shapes):
    flat = packed.reshape(-1)
    out, o = [], 0
    for s in shapes:
        size = 1
        for dim in s:
            size *= dim
        out.append(flat[o:o + size].reshape(s))
        o += size
    return out


def _relu2_epilogue(acc):
    r = jnp.maximum(acc, 0.0)
    return acc, r * r


def _residual_epilogue(acc, res):
    return (ALPHA * res + acc,)


def _plain_add_epilogue(acc, res):
    return (res + acc,)


def _gate_epilogue(acc, y, e):
    gate = _sigmoid(acc)
    return y + gate * e, gate


def _relu2_bwd_epilogue(acc, pre):
    return (acc * (2.0 * jnp.maximum(pre, 0.0)),)


def _tail_fwd(tag, u_a, wts, lng, lnb, p_l):
    y1 = _ln_fwd(f"ln1_{tag}", u_a, lng[0], lnb[0])
    pre, act = _mm_fwd(f"mlp1_{tag}", y1, wts["w1"], "col", [F32, BF16], _relu2_epilogue)
    (u_b,) = _mm_fwd(f"mlp2_{tag}", act, wts["w2"], "row", [F32], _residual_epilogue, (y1,))
    y2 = _ln_fwd(f"ln2_{tag}", u_b, lng[1], lnb[1])
    (e,) = _mm_fwd(f"ple_{tag}", p_l, wts["plew"], "col", [F32])
    xn, gate = _mm_fwd(f"gate_{tag}", y2, wts["gate"], "row", [F32, F32], _gate_epilogue, (y2, e))
    return xn, (u_a, y1, pre, act, u_b, y2, e, gate)


def _tail_bwd(tag, dxn, saved, wts, lng, p_l):
    u_a, y1, pre, act, u_b, y2, e, gate = saved
    dgpre, de = _ple_bwd(f"ple_bwd_{tag}", dxn, e, gate)
    (dy2,) = _mm_dx(f"gate_dx_{tag}", dgpre, wts["gate"], "row", [F32], _plain_add_epilogue, (dxn,))
    g_gate = _mm_dw(f"gate_dw_{tag}", y2, dgpre, "row")
    g_plew = _mm_dw(f"ple_dw_{tag}", p_l, de, "col")
    du_b, dg2, db2 = _ln_bwd(f"ln2_bwd_{tag}", u_b, dy2, lng[1])
    (dpre,) = _mm_dx(f"mlp2_dx_{tag}", du_b, wts["w2"], "row", [BF16], _relu2_bwd_epilogue, (pre,))
    g_w2 = _mm_dw(f"mlp2_dw_{tag}", act, du_b, "row")
    (dy1,) = _mm_dx(f"mlp1_dx_{tag}", dpre, wts["w1"], "col", [F32], _residual_epilogue, (du_b,))
    g_w1 = _mm_dw(f"mlp1_dw_{tag}", y1, dpre, "col")
    du_a, dg1, db1 = _ln_bwd(f"ln1_bwd_{tag}", u_a, dy1, lng[0])
    return du_a, dict(w1=g_w1, w2=g_w2, plew=g_plew, gate=g_gate), [dg1, dg2], [db1, db2]


def _to_slots(a, axis):
    shape = a.shape
    per = shape[axis] // N_DEV
    v = a.reshape(shape[:axis] + (2, 2, 2, per) + shape[axis + 1:])
    perm = (axis + 2, axis, axis + 1) + tuple(range(axis)) + tuple(range(axis + 3, v.ndim))
    v = v.transpose(perm)
    return v.reshape((N_DEV,) + shape[:axis] + (per,) + shape[axis + 1:])


def _pad_lanes(a):
    return jnp.pad(a, [(0, 0)] * (a.ndim - 1) + [(0, V7X_LANES - a.shape[-1])])


def kernel(x, p, pool_w, pool_scale, ssm_in_w, ssm_conv_w, ssm_conv_b, ssm_dt_bias, ssm_a_log, ssm_d, ssm_norm_w, ssm_out_w, mlp_w1, mlp_w2, ln_g, ln_b, ple_w, ple_gate_w, loss_target, m_pool_w, m_pool_scale, m_ssm_in_w, m_ssm_conv_w, m_ssm_conv_b, m_ssm_dt_bias, m_ssm_a_log, m_ssm_d, m_ssm_norm_w, m_ssm_out_w, m_mlp_w1, m_mlp_w2, m_ln_g, m_ln_b, m_ple_w, m_ple_gate_w, v_pool_w, v_pool_scale, v_ssm_in_w, v_ssm_conv_w, v_ssm_conv_b, v_ssm_dt_bias, v_ssm_a_log, v_ssm_d, v_ssm_norm_w, v_ssm_out_w, v_mlp_w1, v_mlp_w2, v_ln_g, v_ln_b, v_ple_w, v_ple_gate_w):
    t, d = x.shape[1:]
    h_n = ssm_dt_bias.shape[-1]
    di_s, cd_s, dp_s, d_s = ssm_norm_w.shape[-1], ssm_conv_b.shape[-1], ssm_in_w.shape[-1], ln_g.shape[-1]
    di, cd, dp = N_DEV * di_s, N_DEV * cd_s, N_DEV * dp_s
    p_dim = di // h_n
    g_n = (cd - di) // (2 * D_STATE)
    hpg = h_n // g_n
    dpp = di + cd + V7X_LANES
    assert h_n <= V7X_LANES and dp == di + cd + h_n and (di + cd) % V7X_LANES == 0
    dt_block = (di + cd) // V7X_LANES
    cg = d // 4
    me = 4 * lax.axis_index("x") + 2 * lax.axis_index("y") + lax.axis_index("c")

    x0, target = x[0], loss_target[0]
    p_l = [p[0, 0], p[1, 0]]

    small_shapes = [(CONV_WIDTH, cd_s), (1, cd_s), (1, di_s), (2, 2, d_s), (2, 2, d_s)]
    small = _pack([ssm_conv_w[0], ssm_conv_b, ssm_norm_w, ln_g, ln_b], 8 * V7X_LANES)
    big = [pool_w[0], mlp_w1[0], mlp_w2[0], ple_w[0], ple_gate_w[0],
           ssm_in_w[0], ssm_out_w[0], mlp_w1[1], mlp_w2[1], ple_w[1], ple_gate_w[1]]
    gathered = _all_gather("ag_weights", [w.astype(BF16) for w in big] + [small])
    pool_g, w1_0, w2_0, plew_0, gate_0, in_g, out_g, w1_1, w2_1, plew_1, gate_1, small_g = gathered
    wts = [dict(w1=w1_0, w2=w2_0, plew=plew_0, gate=gate_0), dict(w1=w1_1, w2=w2_1, plew=plew_1, gate=gate_1)]
    pool_full = pool_g.transpose(1, 0, 2, 3).reshape(4, cg, cg)
    in_full = jnp.pad(in_g.transpose(1, 0, 2).reshape(d, dp), ((0, 0), (0, dpp - dp)))
    sm = small_g.reshape(N_DEV, -1)
    o = 0
    parts = []
    for shp in small_shapes:
        size = 1
        for s in shp:
            size *= s
        parts.append(sm[:, o:o + size].reshape((N_DEV,) + shp))
        o += size
    conv_w_full = parts[0].transpose(1, 0, 2).reshape(CONV_WIDTH, cd)
    conv_b_full = parts[1].transpose(1, 0, 2).reshape(1, cd)
    norm_w_full = parts[2].transpose(1, 0, 2).reshape(1, di)
    ln_g_full = parts[3].transpose(1, 2, 0, 3).reshape(2, 2, 1, d)
    ln_b_full = parts[4].transpose(1, 2, 0, 3).reshape(2, 2, 1, d)
    bias128, alog128 = _pad_lanes(ssm_dt_bias), _pad_lanes(ssm_a_log)
    d_skip = ssm_d.reshape(g_n, 1, hpg)

    pooled = _pool_windows("pool_fwd", x0, False)
    u0, hraw = _pool_mm("pool_mm", pooled, pool_full, pool_scale, x0)
    x1, saved0 = _tail_fwd("l0", u0, wts[0], ln_g_full[0], ln_b_full[0], p_l[0])

    (zx,) = _mm_fwd("in_proj", x1, in_full, "plain", [F32])
    xbc = _conv_fwd("conv_fwd", zx, conv_w_full, conv_b_full, di)
    dt, acs = _dt_fwd("dt_fwd", zx, bias128, alog128, dt_block)

    def to_col(a):
        return a[:, :h_n].reshape(t, g_n, hpg).transpose(1, 0, 2)

    def to_row(a):
        return a[:, :h_n].reshape(t, g_n, hpg).transpose(1, 2, 0)

    def from_col(a):
        return _pad_lanes(a.transpose(1, 0, 2).reshape(t, h_n))

    dt_col, a_col, a_row = to_col(dt), to_col(acs), to_row(acs)
    y_ssd, prev = _ssd_fwd("ssd_fwd", xbc, dt_col, a_col, a_row, d_skip, di, g_n, hpg, p_dim)
    yn = _gated_rms_fwd("gated_rms_fwd", y_ssd, zx, norm_w_full)
    (u2,) = _mm_fwd("out_proj", yn, out_g, "row", [F32], _residual_epilogue, (x1,))
    x2, saved1 = _tail_fwd("l1", u2, wts[1], ln_g_full[1], ln_b_full[1], p_l[1])

    dx2, loss_cols = _loss_bwd("loss", x2, target)
    du2, gw1, dg_1, db_1 = _tail_bwd("l1", dx2, saved1, wts[1], ln_g_full[1], p_l[1])
    (dyn,) = _mm_dx("out_proj_dx", du2, out_g, "row", [F32])
    g_out = _mm_dw("out_proj_dw", yn, du2, "row")
    dy_ssd, dz, d_norm_w = _gated_rms_bwd("gated_rms_bwd", y_ssd, zx, norm_w_full, dyn)
    dxs, dbm, dcm, ddt_x, dacs, dd = _ssd_bwd("ssd_bwd", xbc, dt_col, a_col, a_row, d_skip, prev, dy_ssd,
                                              di, g_n, hpg, p_dim)
    draw, d_bias, d_alog = _dt_bwd("dt_bwd", zx, bias128, alog128, from_col(dacs), from_col(ddt_x), dt_block)
    dxbc, d_conv_w, d_conv_b = _conv_bwd("conv_bwd", zx, conv_w_full, conv_b_full,
                                         jnp.concatenate([dxs, dbm, dcm], axis=1), di)
    dzx = jnp.concatenate([dz, dxbc, draw], axis=1)
    (dx1,) = _mm_dx("in_proj_dx", dzx, in_full, "plain", [F32], _residual_epilogue, (du2,))
    g_in = _to_slots(_mm_dw("in_proj_dw", x1, dzx, "plain")[:, :dp], 1)

    du0, gw0, dg_0, db_0 = _tail_bwd("l0", dx1, saved0, wts[0], ln_g_full[0], p_l[0])
    dh, dpool, d_scale = _pool_bwd_mm("pool_bwd_mm", du0, hraw, pool_full, pool_scale)
    grad_x = _pool_windows("pool_bwd", dpool, True, du0)
    g_pool = _to_slots(_pool_dw("pool_dw", pooled, dh), 1)

    glist = [g_in, g_out, gw1["w1"], gw1["w2"], gw1["plew"], gw1["gate"],
             g_pool, gw0["w1"], gw0["w2"], gw0["plew"], gw0["gate"]]
    core = lax.axis_index("c").astype(jnp.int32).reshape(1)
    chip = (2 * lax.axis_index("x") + lax.axis_index("y")).astype(jnp.int32).reshape(1)
    halves = _rs_sibling_exchange("rs_sibling", glist)
    sums = [_pair_sum(f"rs_pair_sum_{a}", g, hv, core) for a, (g, hv) in enumerate(zip(glist, halves))]
    thirds = _rs_chip_exchange("rs_chips", sums)
    q_in, q_out, q_w1_1, q_w2_1, q_plew_1, q_gate_1, q_pool, q_w1_0, q_w2_0, q_plew_0, q_gate_0 = [
        (s.reshape(4, -1, s.shape[-1]), r.reshape(3, -1, r.shape[-1])) for s, r in zip(sums, thirds)]

    def update(tag, w, m, v, by_layer):
        res = None
        for layer in sorted(by_layer, reverse=True):
            own, recv = by_layer[layer]
            res = _adamw(f"adamw_{tag}_{layer}", _as3d(w), _as3d(m), _as3d(v),
                         [(own, None), (recv, 0), (recv, 1), (recv, 2)], layer, res, chip)
        return [r.reshape(w.shape) for r in res]

    upd = {
        "pool_w": update("pool_w", pool_w, m_pool_w, v_pool_w, {0: q_pool}),
        "ssm_in_w": update("ssm_in_w", ssm_in_w, m_ssm_in_w, v_ssm_in_w, {0: q_in}),
        "ssm_out_w": update("ssm_out_w", ssm_out_w, m_ssm_out_w, v_ssm_out_w, {0: q_out}),
        "mlp_w1": update("mlp_w1", mlp_w1, m_mlp_w1, v_mlp_w1, {0: q_w1_0, 1: q_w1_1}),
        "mlp_w2": update("mlp_w2", mlp_w2, m_mlp_w2, v_mlp_w2, {0: q_w2_0, 1: q_w2_1}),
        "ple_w": update("ple_w", ple_w, m_ple_w, v_ple_w, {0: q_plew_0, 1: q_plew_1}),
        "ple_gate_w": update("ple_gate_w", ple_gate_w, m_ple_gate_w, v_ple_gate_w, {0: q_gate_0, 1: q_gate_1}),
    }

    d_ln_g = jnp.stack([jnp.stack(dg_0), jnp.stack(dg_1)]).reshape(2, 2, d)
    d_ln_b = jnp.stack([jnp.stack(db_0), jnp.stack(db_1)]).reshape(2, 2, d)
    partial_shapes = [(CONV_WIDTH, cd), (1, cd), (1, di), (2, 2, d), (2, 2, d), (1, d), (1, h_n), (1, h_n), (1, h_n),
                      (1, d)]
    partial = _pack([d_conv_w, d_conv_b, d_norm_w, d_ln_g, d_ln_b, d_scale, d_bias[:, :h_n], d_alog[:, :h_n],
                     dd.reshape(1, h_n), loss_cols], 8 * V7X_LANES)
    (all_partials,) = _all_gather("ag_small_grads", [partial])
    tot = _unpack(_sum8("sum_small_grads", all_partials), partial_shapes)
    t_conv_w, t_conv_b, t_norm_w, t_ln_g, t_ln_b, t_scale, t_bias, t_alog, t_dd, t_loss = tot
    loss = jnp.sum(t_loss)

    def mine(a, per):
        return lax.dynamic_slice_in_dim(a, me * per, per, axis=a.ndim - 1)

    small_names = ["ssm_conv_w", "ssm_conv_b", "ssm_norm_w", "ln_g", "ln_b", "pool_scale", "ssm_dt_bias", "ssm_a_log",
                   "ssm_d"]
    small_w = [ssm_conv_w, ssm_conv_b, ssm_norm_w, ln_g, ln_b, pool_scale, ssm_dt_bias, ssm_a_log, ssm_d]
    small_m = [m_ssm_conv_w, m_ssm_conv_b, m_ssm_norm_w, m_ln_g, m_ln_b, m_pool_scale, m_ssm_dt_bias, m_ssm_a_log,
               m_ssm_d]
    small_v = [v_ssm_conv_w, v_ssm_conv_b, v_ssm_norm_w, v_ln_g, v_ln_b, v_pool_scale, v_ssm_dt_bias, v_ssm_a_log,
               v_ssm_d]
    small_grads = [mine(t_conv_w, cd_s), mine(t_conv_b, cd_s), mine(t_norm_w, di_s), mine(t_ln_g, d_s),
                   mine(t_ln_b, d_s), t_scale, t_bias, t_alog, t_dd]
    shapes = [w.shape for w in small_w]
    pk = [_pack(group, 8 * V7X_LANES)[None] for group in (small_w, small_m, small_v, small_grads)]
    res = _adamw("adamw_small", pk[0], pk[1], pk[2], [(pk[3], 0)], 0)
    for name, vals in zip(small_names, zip(*[_unpack(r, shapes) for r in res])):
        upd[name] = list(vals)

    order = ["pool_w", "pool_scale", "ssm_in_w", "ssm_conv_w", "ssm_conv_b", "ssm_dt_bias", "ssm_a_log", "ssm_d",
             "ssm_norm_w", "ssm_out_w", "mlp_w1", "mlp_w2", "ln_g", "ln_b", "ple_w", "ple_gate_w"]
    out = [loss, grad_x[None]]
    for k in range(4):
        out += [upd[name][k] for name in order]
    return tuple(out)
```

```python
import jax
import jax.numpy as jnp
from jax import lax
from jax.experimental import pallas as pl
from jax.experimental.pallas import tpu as pltpu

F32 = jnp.float32
BF16 = jnp.bfloat16
SDS = jax.ShapeDtypeStruct
MESH = pl.DeviceIdType.MESH
ANY = pl.BlockSpec(memory_space=pl.ANY)

N_DEV = 8
DEPTH = 2
ALPHA = (2.0 * DEPTH) ** 0.25
LN_EPS = 1e-5
RMS_EPS = 1e-5
POOL_WINDOW_LOG2 = (1, 2, 3, 4)
D_STATE = 128
CHUNK = 128
CONV_WIDTH = 4
ADAM_LR = 0.001
ADAM_B1 = 0.9
ADAM_B2 = 0.999
ADAM_EPS = 1e-08
ADAM_WD = 0.01
ADAM_STEP = 10

V7X_LANES = 128
V7X_VMEM_LIMIT = 48 * 1024 * 1024


def _cp(*sem):
    return pltpu.CompilerParams(dimension_semantics=sem, vmem_limit_bytes=V7X_VMEM_LIMIT)


def _pick(dim, cap):
    if dim <= cap:
        return dim
    best = None
    for t in range(V7X_LANES, cap + 1, V7X_LANES):
        if dim % t == 0:
            best = t
    assert best is not None, (dim, cap)
    return best


def _row_tile(rows, cols, itemsize=4, target=1 << 20):
    t = rows
    while t % 2 == 0 and t // 2 >= 16 and (t // 2) % 16 == 0 and t * cols * itemsize > target:
        t //= 2
    return t


def _slot(s):
    return (s % 2) * 4 + s // 2


def _all_gather(name, shards):
    n = len(shards)

    def body(*refs):
        ins, outs = refs[:n], refs[n:2 * n]
        send_sems, recv_sems, local_sems = refs[2 * n:]
        x, y, c = lax.axis_index("x"), lax.axis_index("y"), lax.axis_index("c")
        me, sibling = (x, y, c), (x, y, 1 - c)
        chips = [(1 - x, y), (x, 1 - y), (1 - x, 1 - y)]

        def copy(a, k, block, to, src=None):
            dst = outs[a].at[4 * block[0] + 2 * block[1] + block[2]]
            return pltpu.make_async_remote_copy(
                src_ref=dst if src is None else src, dst_ref=dst, send_sem=send_sems.at[a, k],
                recv_sem=recv_sems.at[a, k], device_id=to, device_id_type=MESH)

        mine = [pltpu.make_async_copy(ins[a], outs[a].at[4 * x + 2 * y + c], local_sems.at[a]) for a in range(n)]
        for cp in mine:
            cp.start()
        first = []
        for a in range(n):
            first.append(copy(a, 0, me, sibling, src=ins[a]))
            first += [copy(a, 1 + j, me, (*chip, c), src=ins[a]) for j, chip in enumerate(chips)]
        for cp in first:
            cp.start()
        passed = []
        for j, chip in enumerate(chips):
            for a in range(n):
                copy(a, 1 + j, (*chip, c), me).wait_recv()
                fwd = copy(a, 4 + j, (*chip, c), sibling)
                fwd.start()
                passed.append(fwd)
        for a in range(n):
            copy(a, 0, sibling, me).wait_recv()
            for j, chip in enumerate(chips):
                copy(a, 4 + j, (*chip, 1 - c), me).wait_recv()
        for cp in first + passed:
            cp.wait_send()
        for cp in mine:
            cp.wait()

    return pl.pallas_call(
        body, name=name,
        out_shape=[SDS((N_DEV,) + s.shape, s.dtype) for s in shards],
        in_specs=[ANY] * n, out_specs=[ANY] * n,
        scratch_shapes=[pltpu.SemaphoreType.DMA((n, 7)), pltpu.SemaphoreType.DMA((n, 7)),
                        pltpu.SemaphoreType.DMA((n,))],
    )(*shards)


def _rs_sibling_exchange(name, grads):
    n = len(grads)

    def body(*refs):
        ins, outs = refs[:n], refs[n:2 * n]
        send_sems, recv_sems = refs[2 * n:]
        x, y, c = lax.axis_index("x"), lax.axis_index("y"), lax.axis_index("c")
        remote = [pltpu.make_async_remote_copy(
            src_ref=ins[a].at[pl.ds(4 * (1 - c), 4)], dst_ref=outs[a], send_sem=send_sems.at[a],
            recv_sem=recv_sems.at[a], device_id=(x, y, 1 - c), device_id_type=MESH) for a in range(n)]
        for cp in remote:
            cp.start()
        for cp in remote:
            cp.wait()

    return pl.pallas_call(
        body, name=name,
        out_shape=[SDS((4,) + g.shape[1:], g.dtype) for g in grads],
        in_specs=[ANY] * n, out_specs=[ANY] * n,
        scratch_shapes=[pltpu.SemaphoreType.DMA((n,)), pltpu.SemaphoreType.DMA((n,))],
    )(*grads)


def _rs_chip_exchange(name, sums):
    n = len(sums)

    def body(*refs):
        ins, outs = refs[:n], refs[n:2 * n]
        send_sems, recv_sems = refs[2 * n:]
        x, y, c = lax.axis_index("x"), lax.axis_index("y"), lax.axis_index("c")
        chips = [(1 - x, y), (x, 1 - y), (1 - x, 1 - y)]
        remote = []
        for a in range(n):
            for j, (px, py) in enumerate(chips):
                remote.append(pltpu.make_async_remote_copy(
                    src_ref=ins[a].at[2 * px + py], dst_ref=outs[a].at[j], send_sem=send_sems.at[a, j],
                    recv_sem=recv_sems.at[a, j], device_id=(px, py, c), device_id_type=MESH))
        for cp in remote:
            cp.start()
        for cp in remote:
            cp.wait()

    return pl.pallas_call(
        body, name=name,
        out_shape=[SDS((3,) + s.shape[1:], s.dtype) for s in sums],
        in_specs=[ANY] * n, out_specs=[ANY] * n,
        scratch_shapes=[pltpu.SemaphoreType.DMA((n, 3)), pltpu.SemaphoreType.DMA((n, 3))],
    )(*sums)


HBM_SPEC = pl.BlockSpec(memory_space=pltpu.HBM)
SEM_SPEC = pl.BlockSpec(memory_space=pltpu.SEMAPHORE)
EFFECT = pltpu.SideEffectType.DATAFLOW_SIDE_EFFECTING


def _ag_first_copies(ins, lands, send_sems, recv_sems):
    x, y, c = lax.axis_index("x"), lax.axis_index("y"), lax.axis_index("c")
    targets = [(x, y, 1 - c), (1 - x, y, c), (x, 1 - y, c), (1 - x, 1 - y, c)]
    return [pltpu.make_async_remote_copy(
        src_ref=ins[a], dst_ref=lands[a].at[4 * x + 2 * y + c], send_sem=send_sems.at[4 * a + k],
        recv_sem=recv_sems.at[4 * a + k], device_id=to, device_id_type=MESH)
        for a in range(len(ins)) for k, to in enumerate(targets)]


def _ag_forward_copies(ins, lands, send_sems, recv_sems):
    x, y, c = lax.axis_index("x"), lax.axis_index("y"), lax.axis_index("c")
    cps = []
    for a in range(len(lands)):
        for j, (px, py) in enumerate([(1 - x, y), (x, 1 - y), (1 - x, 1 - y)]):
            blk = lands[a].at[4 * px + 2 * py + c]
            cps.append(pltpu.make_async_remote_copy(
                src_ref=blk, dst_ref=blk, send_sem=send_sems.at[3 * a + j], recv_sem=recv_sems.at[3 * a + j],
                device_id=(x, y, 1 - c), device_id_type=MESH))
    return cps


def _rs_chip_copies(ins, lands, send_sems, recv_sems):
    x, y, c = lax.axis_index("x"), lax.axis_index("y"), lax.axis_index("c")
    cps = []
    for a in range(len(ins)):
        for j, (px, py) in enumerate([(1 - x, y), (x, 1 - y), (1 - x, 1 - y)]):
            cps.append(pltpu.make_async_remote_copy(
                src_ref=ins[a].at[2 * px + py], dst_ref=lands[a].at[j], send_sem=send_sems.at[3 * a + j],
                recv_sem=recv_sems.at[3 * a + j], device_id=(px, py, c), device_id_type=MESH))
    return cps


def _async_start(name, build, sem_shape, ins, lands, after=()):
    arrays = [*ins, *lands]
    n_i, n_t, n_a = len(ins), len(arrays), len(after)

    def body(*refs):
        outs = refs[n_t + n_a:]
        for cp in build(refs[:n_i], refs[n_i:n_t], outs[0], outs[1]):
            cp.start()
        outs[-1][...] = jnp.zeros_like(outs[-1])

    res = pl.pallas_call(
        body, name=name,
        out_shape=(pltpu.SemaphoreType.DMA(sem_shape), pltpu.SemaphoreType.DMA(sem_shape),
                   *[pltpu.HBM(a.shape, a.dtype) for a in arrays], SDS((8, V7X_LANES), F32)),
        in_specs=[HBM_SPEC] * n_t + [ANY] * n_a,
        out_specs=(SEM_SPEC, SEM_SPEC, *[HBM_SPEC] * n_t, pl.BlockSpec(memory_space=pltpu.VMEM)),
        input_output_aliases={i: 2 + i for i in range(n_t)},
        compiler_params=pltpu.CompilerParams(has_side_effects=EFFECT),
    )(*[pltpu.with_memory_space_constraint(a, pltpu.HBM) for a in arrays], *after)
    return res[0], res[1], list(res[2:2 + n_t]), res[-1]


def _async_wait(name, build, handle, n_i, after=()):
    send_sems, recv_sems, arrays, _ = handle
    n_t, n_a = len(arrays), len(after)

    def body(*refs):
        for cp in build(refs[:n_i], refs[n_i:n_t], refs[n_t], refs[n_t + 1]):
            cp.wait_send()
            cp.wait_recv()

    res = pl.pallas_call(
        body, name=name, out_shape=tuple(pltpu.HBM(a.shape, a.dtype) for a in arrays),
        in_specs=[HBM_SPEC] * n_t + [SEM_SPEC, SEM_SPEC] + [ANY] * n_a, out_specs=tuple([HBM_SPEC] * n_t),
        input_output_aliases={i: i for i in range(n_t)},
        compiler_params=pltpu.CompilerParams(has_side_effects=EFFECT),
    )(*arrays, send_sems, recv_sems, *after)
    return list(res[:n_i]), list(res[n_i:])


def _token_add(a, *tokens):
    for tok in tokens:
        a = a + tok[0:1, 0:1].reshape((1,) * a.ndim)
    return a


def _mm_core(name, a, b, *, grid, a_spec, b_spec, dims, acc_shape, outs, out_spec, epilogue=None, extras=(),
             extra_specs=()):
    nk = grid[2]
    ne, no = len(extras), len(outs)

    def body(a_ref, b_ref, *rest):
        e_refs, o_refs, acc = rest[:ne], rest[ne:ne + no], rest[ne + no]
        k = pl.program_id(2)

        @pl.when(k == 0)
        def _():
            acc[...] = jnp.zeros_like(acc)

        acc[...] += lax.dot_general(a_ref[...].astype(BF16), b_ref[...].astype(BF16), dims,
                                    preferred_element_type=F32)

        @pl.when(k == nk - 1)
        def _():
            r = acc[...]
            vals = epilogue(r, *[e[...] for e in e_refs]) if epilogue is not None else (r,)
            for o, v in zip(o_refs, vals):
                o[...] = v.astype(o.dtype)

    res = pl.pallas_call(
        body, name=name, grid=grid, out_shape=list(outs),
        in_specs=[a_spec, b_spec, *extra_specs], out_specs=[out_spec] * no,
        scratch_shapes=[pltpu.VMEM(acc_shape, F32)],
        compiler_params=_cp("parallel", "parallel", "arbitrary"),
    )(a, b, *extras)
    return res


NN = (((1,), (0,)), ((), ()))
NT = (((1,), (1,)), ((), ()))
TN = (((0,), (0,)), ((), ()))


def _w_dims(w, kind):
    if kind == "col":
        return w.shape[1], N_DEV * w.shape[2], w.shape[1], w.shape[2]
    if kind == "row":
        return N_DEV * w.shape[1], w.shape[2], w.shape[1], w.shape[2]
    return w.shape[0], w.shape[1], w.shape[0], w.shape[1]


def _mm_fwd(name, a, w, kind, out_dtypes, epilogue=None, extras=()):
    m, k_dim = a.shape
    kk, n, ks, ns = _w_dims(w, kind)
    assert kk == k_dim
    tm = _pick(m, 1024)
    if kind == "col":
        tn, tk = _pick(ns, 1024), _pick(kk, 512)
        nb = ns // tn
        b_spec = pl.BlockSpec((None, tk, tn), lambda i, j, k: (j // nb, k, j % nb))
    elif kind == "row":
        tn, tk = _pick(n, 1024), _pick(ks, 512)
        kb = ks // tk
        b_spec = pl.BlockSpec((None, tk, tn), lambda i, j, k: (k // kb, k % kb, j))
    else:
        tn, tk = _pick(n, 1152), _pick(kk, 512)
        b_spec = pl.BlockSpec((tk, tn), lambda i, j, k: (k, j))
    mn_spec = pl.BlockSpec((tm, tn), lambda i, j, k: (i, j))
    return _mm_core(
        name, a, w, grid=(m // tm, n // tn, kk // tk),
        a_spec=pl.BlockSpec((tm, tk), lambda i, j, k: (i, k)), b_spec=b_spec, dims=NN, acc_shape=(tm, tn),
        outs=[SDS((m, n), dt) for dt in out_dtypes], out_spec=mn_spec, epilogue=epilogue, extras=extras,
        extra_specs=[mn_spec] * len(extras))


def _mm_dx(name, dy, w, kind, out_dtypes, epilogue=None, extras=()):
    m, n_dim = dy.shape
    kk, n, ks, ns = _w_dims(w, kind)
    assert n == n_dim
    tm = _pick(m, 1024)
    if kind == "col":
        tn, tk = _pick(kk, 1024), _pick(ns, 512)
        kb = ns // tk
        b_spec = pl.BlockSpec((None, tn, tk), lambda i, j, k: (k // kb, j, k % kb))
    elif kind == "row":
        tn, tk = _pick(ks, 1024), _pick(n, 512)
        nb = ks // tn
        b_spec = pl.BlockSpec((None, tn, tk), lambda i, j, k: (j // nb, j % nb, k))
    else:
        tn, tk = _pick(kk, 1024), _pick(n, 1152)
        b_spec = pl.BlockSpec((tn, tk), lambda i, j, k: (j, k))
    mk_spec = pl.BlockSpec((tm, tn), lambda i, j, k: (i, j))
    return _mm_core(
        name, dy, w, grid=(m // tm, kk // tn, n // tk),
        a_spec=pl.BlockSpec((tm, tk), lambda i, j, k: (i, k)), b_spec=b_spec, dims=NT, acc_shape=(tm, tn),
        outs=[SDS((m, kk), dt) for dt in out_dtypes], out_spec=mk_spec, epilogue=epilogue, extras=extras,
        extra_specs=[mk_spec] * len(extras))


def _mm_dw(name, a, dy, kind):
    m, kk = a.shape
    n = dy.shape[1]
    tk = _pick(m, 512)
    if kind == "col":
        ns = n // N_DEV
        tm, tn = _pick(kk, 1024), _pick(ns, 1024)
        nb = ns // tn
        out = SDS((N_DEV, kk, ns), BF16)
        out_spec = pl.BlockSpec((None, tm, tn), lambda i, j, k: (_slot(j // nb), i, j % nb))
    elif kind == "row":
        ks = kk // N_DEV
        tm, tn = _pick(ks, 1024), _pick(n, 1024)
        mb = ks // tm
        out = SDS((N_DEV, ks, n), BF16)
        out_spec = pl.BlockSpec((None, tm, tn), lambda i, j, k: (_slot(i // mb), i % mb, j))
    else:
        tm, tn = _pick(kk, 1024), _pick(n, 1152)
        out = SDS((kk, n), BF16)
        out_spec = pl.BlockSpec((tm, tn), lambda i, j, k: (i, j))
    return _mm_core(
        name, a, dy, grid=(kk // tm, n // tn, m // tk),
        a_spec=pl.BlockSpec((tk, tm), lambda i, j, k: (k, i)), b_spec=pl.BlockSpec((tk, tn), lambda i, j, k: (k, j)),
        dims=TN, acc_shape=(tm, tn), outs=[out], out_spec=out_spec)[0]


def _rowwise(name, fn, ins, outs, rows, tile):
    arrays, specs = [], []
    for arr, kind in ins:
        arrays.append(arr)
        if kind == "row":
            specs.append(pl.BlockSpec((tile, arr.shape[1]), lambda i: (i, 0)))
        elif kind == "vec":
            specs.append(pl.BlockSpec(arr.shape, lambda i, nd=arr.ndim: (0,) * nd))
        else:
            specs.append(kind)
    out_shapes, out_specs, kinds = [], [], []
    for cols, dt, kind in outs:
        kinds.append(kind)
        if kind == "row":
            out_shapes.append(SDS((rows, cols), dt))
            out_specs.append(pl.BlockSpec((tile, cols), lambda i: (i, 0)))
        else:
            out_shapes.append(SDS((1, cols), F32))
            out_specs.append(pl.BlockSpec((1, cols), lambda i: (0, 0)))
    ni = len(arrays)
    has_acc = "acc" in kinds

    def body(*refs):
        vals = fn(*[r[...] for r in refs[:ni]])
        i = pl.program_id(0)
        for o, v, kind in zip(refs[ni:], vals, kinds):
            if kind == "row":
                o[...] = v.astype(o.dtype)
            else:
                @pl.when(i == 0)
                def _(o=o):
                    o[...] = jnp.zeros_like(o)

                o[...] += v

    return pl.pallas_call(
        body, name=name, grid=(rows // tile,), out_shape=out_shapes, in_specs=specs, out_specs=out_specs,
        compiler_params=_cp("arbitrary" if has_acc else "parallel"),
    )(*arrays)


def _ln_fwd(name, u, g, b):
    d = u.shape[1]

    def fn(u, g, b):
        mu = jnp.mean(u, axis=1, keepdims=True)
        xc = u - mu
        var = jnp.mean(xc * xc, axis=1, keepdims=True)
        return (xc * lax.rsqrt(var + LN_EPS) * g + b,)

    return _rowwise(name, fn, [(u, "row"), (g, "vec"), (b, "vec")], [(d, F32, "row")], u.shape[0], 256)[0]


def _ln_bwd(name, u, dy, g):
    d = u.shape[1]

    def fn(u, dy, g):
        mu = jnp.mean(u, axis=1, keepdims=True)
        xc = u - mu
        var = jnp.mean(xc * xc, axis=1, keepdims=True)
        rstd = lax.rsqrt(var + LN_EPS)
        xhat = xc * rstd
        dxhat = dy * g
        m1 = jnp.mean(dxhat, axis=1, keepdims=True)
        m2 = jnp.mean(dxhat * xhat, axis=1, keepdims=True)
        du = rstd * (dxhat - m1 - xhat * m2)
        return du, jnp.sum(dy * xhat, axis=0, keepdims=True), jnp.sum(dy, axis=0, keepdims=True)

    return _rowwise(name, fn, [(u, "row"), (dy, "row"), (g, "vec")],
                    [(d, F32, "row"), (d, F32, "acc"), (d, F32, "acc")], u.shape[0], 256)


def _loss_bwd(name, y, target):
    d = y.shape[1]

    def fn(y, t):
        e = y - t
        return e * (1.0 / d), jnp.sum(e * e, axis=0, keepdims=True) * (0.5 / d)

    return _rowwise(name, fn, [(y, "row"), (target, "row")], [(d, F32, "row"), (d, F32, "acc")], y.shape[0], 256)


def _ple_bwd(name, dx, e, gate):
    d = dx.shape[1]

    def fn(dx, e, gate):
        return dx * e * gate * (1.0 - gate), dx * gate

    return _rowwise(name, fn, [(dx, "row"), (e, "row"), (gate, "row")], [(d, BF16, "row"), (d, BF16, "row")],
                    dx.shape[0], 256)


def _sigmoid(v):
    return 1.0 / (1.0 + jnp.exp(-v))


def _gated_rms_fwd(name, y, zx, norm_w):
    di = y.shape[1]

    def fn(y, z, w):
        yg = y * (z * _sigmoid(z))
        r = lax.rsqrt(jnp.mean(yg * yg, axis=1, keepdims=True) + RMS_EPS)
        return (yg * r * w,)

    z_spec = pl.BlockSpec((128, di), lambda i: (i, 0))
    return _rowwise(name, fn, [(y, "row"), (zx, z_spec), (norm_w, "vec")], [(di, BF16, "row")], y.shape[0], 128)[0]


def _gated_rms_bwd(name, y, zx, norm_w, dout):
    di = y.shape[1]

    def fn(y, z, w, dout):
        sg = _sigmoid(z)
        sz = z * sg
        yg = y * sz
        r = lax.rsqrt(jnp.mean(yg * yg, axis=1, keepdims=True) + RMS_EPS)
        dn = dout * w
        dyg = r * (dn - yg * (r * r) * jnp.mean(dn * yg, axis=1, keepdims=True))
        dy = dyg * sz
        dz = dyg * y * (sg * (1.0 + z * (1.0 - sg)))
        return dy, dz, jnp.sum(dout * yg * r, axis=0, keepdims=True)

    z_spec = pl.BlockSpec((128, di), lambda i: (i, 0))
    return _rowwise(name, fn, [(y, "row"), (zx, z_spec), (norm_w, "vec"), (dout, "row")],
                    [(di, F32, "row"), (di, BF16, "row"), (di, F32, "acc")], y.shape[0], 128)


def _shift_down(v, j, row):
    return jnp.where(row >= j, pltpu.roll(v, j, 0), 0.0)


def _shift_up(v, j, row):
    t = v.shape[0]
    return jnp.where(row < t - j, pltpu.roll(v, t - j, 0), 0.0)


def _pool_select(parts, g):
    return jnp.where(g == 0, parts[0], jnp.where(g == 1, parts[1], jnp.where(g == 2, parts[2], parts[3])))


def _pool_windows(name, x, transpose, scale_by=None):
    t, d = x.shape
    cg = d // 4
    cw = V7X_LANES
    per = cg // cw

    def body(*refs):
        x_ref, o_ref = refs[0], refs[-1]
        g = pl.program_id(0) // per
        xv = x_ref[...]
        row = lax.broadcasted_iota(jnp.int32, (t, 1), 0)
        cnt = jnp.minimum(row + 1, jnp.left_shift(2, g)).astype(F32)
        s = xv / cnt if transpose else xv
        parts = []
        for lg in POOL_WINDOW_LOG2:
            j = 1 << (lg - 1)
            s = s + (_shift_up(s, j, row) if transpose else _shift_down(s, j, row))
            parts.append(s)
        sel = _pool_select(parts, g)
        if transpose:
            o_ref[...] = ALPHA * refs[1][...] + sel - xv
        else:
            o_ref[...] = (sel / cnt - xv).astype(o_ref.dtype)

    col = pl.BlockSpec((t, cw), lambda j: (0, j))
    ins = [x] if scale_by is None else [x, scale_by]
    return pl.pallas_call(
        body, name=name, grid=(d // cw,), out_shape=SDS((t, d), F32 if transpose else BF16),
        in_specs=[col] * len(ins), out_specs=col, compiler_params=_cp("parallel"),
    )(*ins)


def _pool_mm(name, pooled, w, scale, x):
    t, d = x.shape
    cg = d // 4
    tm = _pick(t, 1024)

    def body(p_ref, w_ref, s_ref, x_ref, u_ref, h_ref):
        h = jnp.dot(p_ref[...], w_ref[...], preferred_element_type=F32)
        h_ref[...] = h
        u_ref[...] = ALPHA * x_ref[...] + h * s_ref[...]

    blk = pl.BlockSpec((tm, cg), lambda g, i: (i, g))
    return pl.pallas_call(
        body, name=name, grid=(4, t // tm), out_shape=[SDS((t, d), F32), SDS((t, d), F32)],
        in_specs=[blk, pl.BlockSpec((None, cg, cg), lambda g, i: (g, 0, 0)), pl.BlockSpec((1, cg), lambda g, i: (0, g)),
                  blk],
        out_specs=[blk, blk], compiler_params=_cp("parallel", "parallel"),
    )(pooled, w, scale, x)


def _pool_bwd_mm(name, du, hraw, w, scale):
    t, d = du.shape
    cg = d // 4
    tm = _pick(t, 1024)

    def body(du_ref, h_ref, w_ref, s_ref, dh_ref, dp_ref, ds_ref):
        @pl.when(pl.program_id(1) == 0)
        def _():
            ds_ref[...] = jnp.zeros_like(ds_ref)

        duv = du_ref[...]
        ds_ref[...] += jnp.sum(duv * h_ref[...], axis=0, keepdims=True)
        dh = (duv * s_ref[...]).astype(BF16)
        dh_ref[...] = dh
        dp_ref[...] = lax.dot_general(dh, w_ref[...], NT, preferred_element_type=F32)

    blk = pl.BlockSpec((tm, cg), lambda g, i: (i, g))
    vec = pl.BlockSpec((1, cg), lambda g, i: (0, g))
    return pl.pallas_call(
        body, name=name, grid=(4, t // tm), out_shape=[SDS((t, d), BF16), SDS((t, d), F32), SDS((1, d), F32)],
        in_specs=[blk, blk, pl.BlockSpec((None, cg, cg), lambda g, i: (g, 0, 0)), vec],
        out_specs=[blk, blk, vec], compiler_params=_cp("parallel", "arbitrary"),
    )(du, hraw, w, scale)


def _pool_dw(name, pooled, dh):
    t, d = pooled.shape
    cg = d // 4
    tk = _pick(t, 512)
    nk = t // tk

    def body(p_ref, dh_ref, o_ref, acc):
        k = pl.program_id(1)

        @pl.when(k == 0)
        def _():
            acc[...] = jnp.zeros_like(acc)

        acc[...] += lax.dot_general(p_ref[...], dh_ref[...], TN, preferred_element_type=F32)

        @pl.when(k == nk - 1)
        def _():
            o_ref[...] = acc[...].astype(o_ref.dtype)

    blk = pl.BlockSpec((tk, cg), lambda g, k: (k, g))
    return pl.pallas_call(
        body, name=name, grid=(4, nk), out_shape=SDS((4, cg, cg), BF16), in_specs=[blk, blk],
        out_specs=pl.BlockSpec((None, cg, cg), lambda g, k: (g, 0, 0)), scratch_shapes=[pltpu.VMEM((cg, cg), F32)],
        compiler_params=_cp("parallel", "arbitrary"),
    )(pooled, dh)


def _conv_pre(u, w_ref, b_ref, row):
    pre = b_ref[...] + _shift_down(u, 3, row) * w_ref[0:1, :]
    pre = pre + _shift_down(u, 2, row) * w_ref[1:2, :]
    pre = pre + _shift_down(u, 1, row) * w_ref[2:3, :]
    return pre + u * w_ref[3:4, :]


def _conv_fwd(name, zx, conv_w, conv_b, di):
    t = zx.shape[0]
    cd = conv_w.shape[1]
    cw = _pick(cd, 256)
    off = di // cw

    def body(u_ref, w_ref, b_ref, o_ref):
        row = lax.broadcasted_iota(jnp.int32, (t, 1), 0)
        pre = _conv_pre(u_ref[...], w_ref, b_ref, row)
        o_ref[...] = pre * _sigmoid(pre)

    return pl.pallas_call(
        body, name=name, grid=(cd // cw,), out_shape=SDS((t, cd), F32),
        in_specs=[pl.BlockSpec((t, cw), lambda j: (0, off + j)), pl.BlockSpec((CONV_WIDTH, cw), lambda j: (0, j)),
                  pl.BlockSpec((1, cw), lambda j: (0, j))],
        out_specs=pl.BlockSpec((t, cw), lambda j: (0, j)), compiler_params=_cp("parallel"),
    )(zx, conv_w, conv_b)


def _conv_bwd(name, zx, conv_w, conv_b, dact, di):
    t = zx.shape[0]
    cd = conv_w.shape[1]
    cw = _pick(cd, 256)
    off = di // cw

    def body(u_ref, w_ref, b_ref, da_ref, du_ref, dw_ref, db_ref):
        row = lax.broadcasted_iota(jnp.int32, (t, 1), 0)
        u = u_ref[...]
        pre = _conv_pre(u, w_ref, b_ref, row)
        sg = _sigmoid(pre)
        dpre = da_ref[...] * (sg * (1.0 + pre * (1.0 - sg)))
        du = dpre * w_ref[3:4, :]
        for j in (1, 2, 3):
            du = du + _shift_up(dpre, j, row) * w_ref[3 - j:4 - j, :]
            dw_ref[3 - j:4 - j, :] = jnp.sum(dpre * _shift_down(u, j, row), axis=0, keepdims=True)
        dw_ref[3:4, :] = jnp.sum(dpre * u, axis=0, keepdims=True)
        db_ref[...] = jnp.sum(dpre, axis=0, keepdims=True)
        du_ref[...] = du.astype(du_ref.dtype)

    wspec = pl.BlockSpec((CONV_WIDTH, cw), lambda j: (0, j))
    bspec = pl.BlockSpec((1, cw), lambda j: (0, j))
    ospec = pl.BlockSpec((t, cw), lambda j: (0, j))
    return pl.pallas_call(
        body, name=name, grid=(cd // cw,), out_shape=[SDS((t, cd), BF16), SDS((CONV_WIDTH, cd), F32), SDS((1, cd), F32)],
        in_specs=[pl.BlockSpec((t, cw), lambda j: (0, off + j)), wspec, bspec, ospec],
        out_specs=[ospec, wspec, bspec], compiler_params=_cp("parallel"),
    )(zx, conv_w, conv_b, dact)


def _softplus(v):
    return jnp.maximum(v, 0.0) + jnp.log(1.0 + jnp.exp(-jnp.abs(v)))


def _dt_fwd(name, zx, bias, a_log, col_block):
    t = zx.shape[0]

    def body(r_ref, b_ref, al_ref, dt_ref, acs_ref):
        row = lax.broadcasted_iota(jnp.int32, (t, 1), 0) % CHUNK
        dt = _softplus(r_ref[...] + b_ref[...])
        s = dt * (-jnp.exp(al_ref[...]))
        j = 1
        while j < CHUNK:
            s = s + jnp.where(row >= j, pltpu.roll(s, j, 0), 0.0)
            j *= 2
        dt_ref[...] = dt
        acs_ref[...] = s

    vec = pl.BlockSpec((1, V7X_LANES), lambda i: (0, 0))
    full = pl.BlockSpec((t, V7X_LANES), lambda i: (0, 0))
    return pl.pallas_call(
        body, name=name, grid=(1,), out_shape=[SDS((t, V7X_LANES), F32)] * 2,
        in_specs=[pl.BlockSpec((t, V7X_LANES), lambda i: (0, col_block)), vec, vec], out_specs=[full, full],
        compiler_params=_cp("arbitrary"),
    )(zx, bias, a_log)


def _dt_bwd(name, zx, bias, a_log, d_acs, d_dt, col_block):
    t = zx.shape[0]

    def body(r_ref, b_ref, al_ref, da_ref, dd_ref, draw_ref, db_ref, dal_ref):
        row = lax.broadcasted_iota(jnp.int32, (t, 1), 0) % CHUNK
        pre = r_ref[...] + b_ref[...]
        dt = _softplus(pre)
        a = -jnp.exp(al_ref[...])
        s = da_ref[...]
        j = 1
        while j < CHUNK:
            s = s + jnp.where(row < CHUNK - j, pltpu.roll(s, t - j, 0), 0.0)
            j *= 2
        ddt = dd_ref[...] + s * a
        dal_ref[...] = jnp.sum(s * dt, axis=0, keepdims=True) * a
        draw = ddt * _sigmoid(pre)
        db_ref[...] = jnp.sum(draw, axis=0, keepdims=True)
        draw_ref[...] = draw.astype(draw_ref.dtype)

    vec = pl.BlockSpec((1, V7X_LANES), lambda i: (0, 0))
    full = pl.BlockSpec((t, V7X_LANES), lambda i: (0, 0))
    return pl.pallas_call(
        body, name=name, grid=(1,), out_shape=[SDS((t, V7X_LANES), BF16), SDS((1, V7X_LANES), F32), SDS((1, V7X_LANES), F32)],
        in_specs=[pl.BlockSpec((t, V7X_LANES), lambda i: (0, col_block)), vec, vec, full, full],
        out_specs=[full, vec, vec], compiler_params=_cp("arbitrary"),
    )(zx, bias, a_log, d_acs, d_dt)


def _ssd_specs(t, di, g_n, hpg, p, rev):
    nc = t // CHUNK
    w = hpg * p
    nb = di // D_STATE

    def cc(c):
        return nc - 1 - c if rev else c

    return dict(
        xs=pl.BlockSpec((CHUNK, w), lambda g, c: (cc(c), g)),
        bm=pl.BlockSpec((CHUNK, D_STATE), lambda g, c: (cc(c), nb + g)),
        cm=pl.BlockSpec((CHUNK, D_STATE), lambda g, c: (cc(c), nb + g_n + g)),
        col=pl.BlockSpec((None, CHUNK, hpg), lambda g, c: (g, cc(c), 0)),
        rowv=pl.BlockSpec((None, hpg, CHUNK), lambda g, c: (g, 0, cc(c))),
        head=pl.BlockSpec((None, 1, hpg), lambda g, c: (g, 0, 0)),
        y=pl.BlockSpec((CHUNK, w), lambda g, c: (cc(c), g)),
        bc=pl.BlockSpec((CHUNK, D_STATE), lambda g, c: (cc(c), g)),
        prev=pl.BlockSpec((None, hpg, D_STATE, p), lambda g, c: (cc(c), g, 0, 0)),
    )


def _ssd_fwd(name, xbc, dt_col, a_col, a_row, d_skip, di, g_n, hpg, p):
    t = xbc.shape[0]
    nc = t // CHUNK
    sp = _ssd_specs(t, di, g_n, hpg, p, False)

    def body(xs_ref, bm_ref, cm_ref, dt_ref, ac_ref, ar_ref, d_ref, y_ref, prev_ref, h_ref):
        @pl.when(pl.program_id(1) == 0)
        def _():
            h_ref[...] = jnp.zeros_like(h_ref)

        bm = bm_ref[...].astype(BF16)
        cm = cm_ref[...].astype(BF16)
        cb = lax.dot_general(cm, bm, NT, preferred_element_type=F32)
        li = lax.broadcasted_iota(jnp.int32, (CHUNK, CHUNK), 0)
        si = lax.broadcasted_iota(jnp.int32, (CHUNK, CHUNK), 1)
        for hh in range(hpg):
            sl = slice(hh * p, (hh + 1) * p)
            x_h = xs_ref[:, sl]
            ac = ac_ref[:, hh:hh + 1]
            ar = ar_ref[hh:hh + 1, :]
            a_last = ar[:, CHUNK - 1:CHUNK]
            lm = jnp.exp(jnp.where(li >= si, ac - ar, -jnp.inf))
            xdt = x_h * dt_ref[:, hh:hh + 1]
            y = jnp.dot((cb * lm).astype(BF16), xdt.astype(BF16), preferred_element_type=F32)
            h_prev = h_ref[hh]
            prev_ref[hh] = h_prev
            y = y + jnp.dot(cm, h_prev.astype(BF16), preferred_element_type=F32) * jnp.exp(ac)
            y_ref[:, sl] = y + x_h * d_ref[:, hh:hh + 1]
            st = lax.dot_general(bm, (xdt * jnp.exp(a_last - ac)).astype(BF16), TN, preferred_element_type=F32)
            h_ref[hh] = jnp.exp(a_last) * h_prev + st

    return pl.pallas_call(
        body, name=name, grid=(g_n, nc),
        out_shape=[SDS((t, di), F32), SDS((nc, g_n * hpg, D_STATE, p), F32)],
        in_specs=[sp["xs"], sp["bm"], sp["cm"], sp["col"], sp["col"], sp["rowv"], sp["head"]],
        out_specs=[sp["y"], sp["prev"]], scratch_shapes=[pltpu.VMEM((hpg, D_STATE, p), F32)],
        compiler_params=_cp("parallel", "arbitrary"),
    )(xbc, xbc, xbc, dt_col, a_col, a_row, d_skip)


def _ssd_bwd(name, xbc, dt_col, a_col, a_row, d_skip, prev, dy, di, g_n, hpg, p):
    t = xbc.shape[0]
    nc = t // CHUNK
    sp = _ssd_specs(t, di, g_n, hpg, p, True)

    def body(xs_ref, bm_ref, cm_ref, dt_ref, ac_ref, ar_ref, d_ref, prev_ref, dy_ref,
             dx_ref, dbm_ref, dcm_ref, ddt_ref, dacs_ref, dd_ref, dh_ref):
        @pl.when(pl.program_id(1) == 0)
        def _():
            dh_ref[...] = jnp.zeros_like(dh_ref)
            dd_ref[...] = jnp.zeros_like(dd_ref)

        bm = bm_ref[...].astype(BF16)
        cm = cm_ref[...].astype(BF16)
        cb = lax.dot_general(cm, bm, NT, preferred_element_type=F32)
        li = lax.broadcasted_iota(jnp.int32, (CHUNK, CHUNK), 0)
        si = lax.broadcasted_iota(jnp.int32, (CHUNK, CHUNK), 1)
        last_row = lax.broadcasted_iota(jnp.int32, (CHUNK, 1), 0) == CHUNK - 1
        d_cb = jnp.zeros((CHUNK, CHUNK), F32)
        d_bm = jnp.zeros((CHUNK, D_STATE), F32)
        d_cm = jnp.zeros((CHUNK, D_STATE), F32)
        for hh in range(hpg):
            sl = slice(hh * p, (hh + 1) * p)
            x_h = xs_ref[:, sl]
            dt_h = dt_ref[:, hh:hh + 1]
            ac = ac_ref[:, hh:hh + 1]
            ar = ar_ref[hh:hh + 1, :]
            a_last = ar[:, CHUNK - 1:CHUNK]
            e_in = jnp.exp(ac)
            e_out = jnp.exp(a_last - ac)
            e_all = jnp.exp(a_last)
            lm = jnp.exp(jnp.where(li >= si, ac - ar, -jnp.inf))
            mm = cb * lm
            xdt = x_h * dt_h
            xdt_b = xdt.astype(BF16)
            dy_h = dy_ref[:, sl]
            dy_b = dy_h.astype(BF16)
            h_prev = h_prev_f = prev_ref[hh]
            h_prev = h_prev.astype(BF16)
            dh_next = dh_ref[hh]
            dh_next_b = dh_next.astype(BF16)
            ch = jnp.dot(cm, h_prev, preferred_element_type=F32)
            dy_e = (dy_h * e_in).astype(BF16)
            d_cm = d_cm + lax.dot_general(dy_e, h_prev, NT, preferred_element_type=F32)
            d_a = jnp.sum(dy_h * ch, axis=1, keepdims=True) * e_in
            dh_ref[hh] = e_all * dh_next + lax.dot_general(cm, dy_e, TN, preferred_element_type=F32)
            d_m = lax.dot_general(dy_b, xdt_b, NT, preferred_element_type=F32)
            d_xdt = lax.dot_general(mm.astype(BF16), dy_b, TN, preferred_element_type=F32)
            wm = d_m * mm
            d_a = d_a + jnp.sum(wm, axis=1, keepdims=True) - jnp.sum(wm.T, axis=1, keepdims=True)
            d_cb = d_cb + d_m * lm
            q = jnp.dot(bm, dh_next_b, preferred_element_type=F32)
            d_xdt = d_xdt + q * e_out
            d_f = jnp.sum(q * xdt, axis=1, keepdims=True) * e_out
            d_last = jnp.sum(d_f, axis=0, keepdims=True) + e_all * jnp.sum(
                jnp.sum(dh_next * h_prev_f, axis=1, keepdims=True), axis=0, keepdims=True)
            d_a = d_a - d_f + jnp.where(last_row, d_last, 0.0)
            d_bm = d_bm + lax.dot_general((xdt * e_out).astype(BF16), dh_next_b, NT, preferred_element_type=F32)
            d_skip = d_ref[:, hh:hh + 1]
            dx_ref[:, sl] = d_xdt * dt_h + dy_h * d_skip
            ddt_ref[:, hh:hh + 1] = jnp.sum(d_xdt * x_h, axis=1, keepdims=True)
            dacs_ref[:, hh:hh + 1] = d_a
            dd_ref[:, hh:hh + 1] += jnp.sum(jnp.sum(dy_h * x_h, axis=1, keepdims=True), axis=0, keepdims=True)
        d_cb_b = d_cb.astype(BF16)
        dcm_ref[...] = d_cm + jnp.dot(d_cb_b, bm, preferred_element_type=F32)
        dbm_ref[...] = d_bm + lax.dot_general(d_cb_b, cm, TN, preferred_element_type=F32)

    gn = g_n * D_STATE
    return pl.pallas_call(
        body, name=name, grid=(g_n, nc),
        out_shape=[SDS((t, di), F32), SDS((t, gn), F32), SDS((t, gn), F32), SDS((g_n, t, hpg), F32),
                   SDS((g_n, t, hpg), F32), SDS((g_n, 1, hpg), F32)],
        in_specs=[sp["xs"], sp["bm"], sp["cm"], sp["col"], sp["col"], sp["rowv"], sp["head"], sp["prev"], sp["y"]],
        out_specs=[sp["y"], sp["bc"], sp["bc"], sp["col"], sp["col"], sp["head"]],
        scratch_shapes=[pltpu.VMEM((hpg, D_STATE, p), F32)],
        compiler_params=_cp("parallel", "arbitrary"),
    )(xbc, xbc, xbc, dt_col, a_col, a_row, d_skip, prev, dy)


def _as3d(a):
    return a.reshape(a.shape[0], -1, a.shape[-1])


def _pair_sum(name, own, recv, core):
    shape = recv.shape
    cols = shape[-1]
    own3, recv3 = own.reshape(8, -1, cols), recv.reshape(4, -1, cols)
    rows = recv3.shape[1]
    tr = _row_tile(rows, cols, 2)

    def body(c_ref, a_ref, b_ref, o_ref):
        o_ref[...] = (a_ref[...].astype(F32) + b_ref[...].astype(F32)).astype(o_ref.dtype)

    blk = pl.BlockSpec((None, tr, cols), lambda q, i, c_ref: (q, i, 0))
    out = pl.pallas_call(
        body, name=name, out_shape=SDS(recv3.shape, recv.dtype),
        grid_spec=pltpu.PrefetchScalarGridSpec(
            num_scalar_prefetch=1, grid=(4, rows // tr),
            in_specs=[pl.BlockSpec((None, tr, cols), lambda q, i, c_ref: (4 * c_ref[0] + q, i, 0)), blk], out_specs=blk),
        compiler_params=_cp("parallel", "parallel"),
    )(core, own3, recv3)
    return out.reshape(shape)


def _adamw(name, w, m, v, parts, layer, prev=None, sel=None):
    lyr, rows, cols = w.shape
    n = len(parts)
    tr = _row_tile(rows, cols)
    np_ = 0 if prev is None else 4
    if sel is None:
        sel = jnp.zeros((1,), jnp.int32)

    def body(sel_ref, *refs):
        w_ref, m_ref, v_ref = refs[:3]
        p_refs = refs[3:3 + n]
        g_ref, d_ref, nm_ref, nv_ref = refs[3 + n + np_:]
        g = p_refs[0][...].astype(F32)
        for r in p_refs[1:]:
            g = g + r[...].astype(F32)
        nm = ADAM_B1 * m_ref[...] + (1.0 - ADAM_B1) * g
        nv = ADAM_B2 * v_ref[...] + (1.0 - ADAM_B2) * (g * g)
        m_hat = nm / (1.0 - ADAM_B1 ** ADAM_STEP)
        v_hat = nv / (1.0 - ADAM_B2 ** ADAM_STEP)
        g_ref[...] = g
        d_ref[...] = -ADAM_LR * (m_hat / (jnp.sqrt(v_hat) + ADAM_EPS) + ADAM_WD * w_ref[...])
        nm_ref[...] = nm
        nv_ref[...] = nv

    lspec = pl.BlockSpec((None, tr, cols), lambda i, s: (layer, i, 0))
    pspecs = [pl.BlockSpec((None, tr, cols), (lambda i, s: (s[0], i, 0)) if q is None else (lambda i, s, q=q: (q, i, 0)))
              for _, q in parts]
    aliases = {} if prev is None else {4 + n + q: q for q in range(4)}
    return pl.pallas_call(
        body, name=name, out_shape=[SDS(w.shape, F32)] * 4,
        grid_spec=pltpu.PrefetchScalarGridSpec(
            num_scalar_prefetch=1, grid=(rows // tr,), in_specs=[lspec] * 3 + pspecs + [ANY] * np_,
            out_specs=[lspec] * 4),
        input_output_aliases=aliases, compiler_params=_cp("parallel"),
    )(sel, w, m, v, *[arr for arr, _ in parts], *(prev or ()))


def _sum8(name, parts):
    rows = parts.shape[1]

    def body(p_ref, o_ref):
        s = p_ref[0]
        for q in range(1, N_DEV):
            s = s + p_ref[q]
        o_ref[...] = s

    return pl.pallas_call(
        body, name=name, grid=(1,), out_shape=SDS((rows, V7X_LANES), F32),
        in_specs=[pl.BlockSpec((N_DEV, rows, V7X_LANES), lambda i: (0, 0, 0))],
        out_specs=pl.BlockSpec((rows, V7X_LANES), lambda i: (0, 0)), compiler_params=_cp("arbitrary"),
    )(parts)


def _pack(vectors, align):
    flat = jnp.concatenate([v.reshape(-1) for v in vectors])
    pad = (-flat.shape[0]) % align
    if pad:
        flat = jnp.concatenate([flat, jnp.zeros((pad,), F32)])
    return flat.reshape(-1, V7X_LANES)


def _unpack(packed, shapes):
    flat = packed.reshape(-1)
    out, o = [], 0
    for s in shapes:
        size = 1
        for dim in s:
            size *= dim
        out.append(flat[o:o + size].reshape(s))
        o += size
    return out


def _relu2_epilogue(acc):
    r = jnp.maximum(acc, 0.0)
    return acc, r * r


def _residual_epilogue(acc, res):
    return (ALPHA * res + acc,)


def _plain_add_epilogue(acc, res):
    return (res + acc,)


def _gate_epilogue(acc, y, e):
    gate = _sigmoid(acc)
    return y + gate * e, gate


def _relu2_bwd_epilogue(acc, pre):
    return (acc * (2.0 * jnp.maximum(pre, 0.0)),)


def _tail_fwd(tag, u_a, wts, lng, lnb, p_l):
    y1 = _ln_fwd(f"ln1_{tag}", u_a, lng[0], lnb[0])
    pre, act = _mm_fwd(f"mlp1_{tag}", y1, wts["w1"], "col", [F32, BF16], _relu2_epilogue)
    (u_b,) = _mm_fwd(f"mlp2_{tag}", act, wts["w2"], "row", [F32], _residual_epilogue, (y1,))
    y2 = _ln_fwd(f"ln2_{tag}", u_b, lng[1], lnb[1])
    (e,) = _mm_fwd(f"ple_{tag}", p_l, wts["plew"], "col", [F32])
    xn, gate = _mm_fwd(f"gate_{tag}", y2, wts["gate"], "row", [F32, F32], _gate_epilogue, (y2, e))
    return xn, (u_a, y1, pre, act, u_b, y2, e, gate)


def _tail_bwd(tag, dxn, saved, wts, lng, p_l):
    u_a, y1, pre, act, u_b, y2, e, gate = saved
    dgpre, de = _ple_bwd(f"ple_bwd_{tag}", dxn, e, gate)
    (dy2,) = _mm_dx(f"gate_dx_{tag}", dgpre, wts["gate"], "row", [F32], _plain_add_epilogue, (dxn,))
    g_gate = _mm_dw(f"gate_dw_{tag}", y2, dgpre, "row")
    g_plew = _mm_dw(f"ple_dw_{tag}", p_l, de, "col")
    du_b, dg2, db2 = _ln_bwd(f"ln2_bwd_{tag}", u_b, dy2, lng[1])
    (dpre,) = _mm_dx(f"mlp2_dx_{tag}", du_b, wts["w2"], "row", [BF16], _relu2_bwd_epilogue, (pre,))
    g_w2 = _mm_dw(f"mlp2_dw_{tag}", act, du_b, "row")
    (dy1,) = _mm_dx(f"mlp1_dx_{tag}", dpre, wts["w1"], "col", [F32], _residual_epilogue, (du_b,))
    g_w1 = _mm_dw(f"mlp1_dw_{tag}", y1, dpre, "col")
    du_a, dg1, db1 = _ln_bwd(f"ln1_bwd_{tag}", u_a, dy1, lng[0])
    return du_a, dict(w1=g_w1, w2=g_w2, plew=g_plew, gate=g_gate), [dg1, dg2], [db1, db2]


def _to_slots(a, axis):
    shape = a.shape
    per = shape[axis] // N_DEV
    v = a.reshape(shape[:axis] + (2, 2, 2, per) + shape[axis + 1:])
    perm = (axis + 2, axis, axis + 1) + tuple(range(axis)) + tuple(range(axis + 3, v.ndim))
    v = v.transpose(perm)
    return v.reshape((N_DEV,) + shape[:axis] + (per,) + shape[axis + 1:])


def _pad_lanes(a):
    return jnp.pad(a, [(0, 0)] * (a.ndim - 1) + [(0, V7X_LANES - a.shape[-1])])


def kernel(x, p, pool_w, pool_scale, ssm_in_w, ssm_conv_w, ssm_conv_b, ssm_dt_bias, ssm_a_log, ssm_d, ssm_norm_w, ssm_out_w, mlp_w1, mlp_w2, ln_g, ln_b, ple_w, ple_gate_w, loss_target, m_pool_w, m_pool_scale, m_ssm_in_w, m_ssm_conv_w, m_ssm_conv_b, m_ssm_dt_bias, m_ssm_a_log, m_ssm_d, m_ssm_norm_w, m_ssm_out_w, m_mlp_w1, m_mlp_w2, m_ln_g, m_ln_b, m_ple_w, m_ple_gate_w, v_pool_w, v_pool_scale, v_ssm_in_w, v_ssm_conv_w, v_ssm_conv_b, v_ssm_dt_bias, v_ssm_a_log, v_ssm_d, v_ssm_norm_w, v_ssm_out_w, v_mlp_w1, v_mlp_w2, v_ln_g, v_ln_b, v_ple_w, v_ple_gate_w):
    t, d = x.shape[1:]
    h_n = ssm_dt_bias.shape[-1]
    di_s, cd_s, dp_s, d_s = ssm_norm_w.shape[-1], ssm_conv_b.shape[-1], ssm_in_w.shape[-1], ln_g.shape[-1]
    di, cd, dp = N_DEV * di_s, N_DEV * cd_s, N_DEV * dp_s
    p_dim = di // h_n
    g_n = (cd - di) // (2 * D_STATE)
    hpg = h_n // g_n
    dpp = di + cd + V7X_LANES
    assert h_n <= V7X_LANES and dp == di + cd + h_n and (di + cd) % V7X_LANES == 0
    dt_block = (di + cd) // V7X_LANES
    cg = d // 4
    me = 4 * lax.axis_index("x") + 2 * lax.axis_index("y") + lax.axis_index("c")

    x0, target = x[0], loss_target[0]
    p_l = [p[0, 0], p[1, 0]]

    small_shapes = [(CONV_WIDTH, cd_s), (1, cd_s), (1, di_s), (2, 2, d_s), (2, 2, d_s)]
    small = _pack([ssm_conv_w[0], ssm_conv_b, ssm_norm_w, ln_g, ln_b], 8 * V7X_LANES)
    first = [w.astype(BF16) for w in (pool_w[0], mlp_w1[0], mlp_w2[0], ple_w[0], ple_gate_w[0])]
    own_ssm = [w.astype(BF16) for w in (ssm_in_w[0], ssm_out_w[0])]
    own_mlp = [w.astype(BF16) for w in (mlp_w1[1], mlp_w2[1], ple_w[1], ple_gate_w[1])]
    pool_g, w1_0, w2_0, plew_0, gate_0, small_g = _all_gather("ag_layer0", first + [small])

    def gather_start(tag, own, after):
        lands = [lax.empty((N_DEV,) + w.shape, w.dtype) for w in own]
        return _async_start(f"ag_{tag}_start", _ag_first_copies, (4 * len(own),), own, lands, after)

    def gather_finish(tag, handle, after):
        n = len(handle[2]) // 2
        own, lands = _async_wait(f"ag_{tag}_wait", _ag_first_copies, handle, n, after)
        fwd = _async_start(f"ag_{tag}_forward_start", _ag_forward_copies, (3 * n,), [], lands)
        _, lands = _async_wait(f"ag_{tag}_forward_wait", _ag_forward_copies, fwd, 0)
        return [lax.dynamic_update_slice_in_dim(g, w[None], me, 0) for g, w in zip(lands, own)]

    ag_ssm = gather_start("ssm", own_ssm, (pool_g,))
    ag_mlp = gather_start("mlp1", own_mlp, (ag_ssm[3],))
    pool_scale_fwd = _token_add(pool_scale, ag_ssm[3], ag_mlp[3])
    pool_full = pool_g.transpose(1, 0, 2, 3).reshape(4, cg, cg)
    sm = small_g.reshape(N_DEV, -1)
    o = 0
    parts = []
    for shp in small_shapes:
        size = 1
        for s in shp:
            size *= s
        parts.append(sm[:, o:o + size].reshape((N_DEV,) + shp))
        o += size
    conv_w_full = parts[0].transpose(1, 0, 2).reshape(CONV_WIDTH, cd)
    conv_b_full = parts[1].transpose(1, 0, 2).reshape(1, cd)
    norm_w_full = parts[2].transpose(1, 0, 2).reshape(1, di)
    ln_g_full = parts[3].transpose(1, 2, 0, 3).reshape(2, 2, 1, d)
    ln_b_full = parts[4].transpose(1, 2, 0, 3).reshape(2, 2, 1, d)
    bias128, alog128 = _pad_lanes(ssm_dt_bias), _pad_lanes(ssm_a_log)
    d_skip = ssm_d.reshape(g_n, 1, hpg)

    pooled = _pool_windows("pool_fwd", x0, False)
    u0, hraw = _pool_mm("pool_mm", pooled, pool_full, pool_scale_fwd, x0)
    wts = [dict(w1=w1_0, w2=w2_0, plew=plew_0, gate=gate_0)]
    x1, saved0 = _tail_fwd("l0", u0, wts[0], ln_g_full[0], ln_b_full[0], p_l[0])

    in_g, out_g = gather_finish("ssm", ag_ssm, (x1,))
    in_full = jnp.pad(in_g.transpose(1, 0, 2).reshape(d, dp), ((0, 0), (0, dpp - dp)))
    (zx,) = _mm_fwd("in_proj", x1, in_full, "plain", [F32])
    xbc = _conv_fwd("conv_fwd", zx, conv_w_full, conv_b_full, di)
    dt, acs = _dt_fwd("dt_fwd", zx, bias128, alog128, dt_block)

    def to_col(a):
        return a[:, :h_n].reshape(t, g_n, hpg).transpose(1, 0, 2)

    def to_row(a):
        return a[:, :h_n].reshape(t, g_n, hpg).transpose(1, 2, 0)

    def from_col(a):
        return _pad_lanes(a.transpose(1, 0, 2).reshape(t, h_n))

    dt_col, a_col, a_row = to_col(dt), to_col(acs), to_row(acs)
    y_ssd, prev = _ssd_fwd("ssd_fwd", xbc, dt_col, a_col, a_row, d_skip, di, g_n, hpg, p_dim)
    yn = _gated_rms_fwd("gated_rms_fwd", y_ssd, zx, norm_w_full)
    (u2,) = _mm_fwd("out_proj", yn, out_g, "row", [F32], _residual_epilogue, (x1,))
    w1_1, w2_1, plew_1, gate_1 = gather_finish("mlp1", ag_mlp, (u2,))
    wts.append(dict(w1=w1_1, w2=w2_1, plew=plew_1, gate=gate_1))
    x2, saved1 = _tail_fwd("l1", u2, wts[1], ln_g_full[1], ln_b_full[1], p_l[1])

    core = lax.axis_index("c").astype(jnp.int32).reshape(1)
    chip = (2 * lax.axis_index("x") + lax.axis_index("y")).astype(jnp.int32).reshape(1)

    def sibling_sums(tag, grads):
        halves = _rs_sibling_exchange(f"rs_{tag}_sibling", grads)
        return [_pair_sum(f"rs_{tag}_pair_sum_{a}", g, hv, core) for a, (g, hv) in enumerate(zip(grads, halves))]

    def scatter_start(tag, grads):
        sums = sibling_sums(tag, grads)
        lands = [lax.empty((3,) + s.shape[1:], s.dtype) for s in sums]
        return _async_start(f"rs_{tag}_start", _rs_chip_copies, (3 * len(sums),), sums, lands)

    def as_parts(sums, thirds):
        return [(s.reshape(4, -1, s.shape[-1]), r.reshape(3, -1, r.shape[-1])) for s, r in zip(sums, thirds)]

    dx2, loss_cols = _loss_bwd("loss", x2, target)
    du2, gw1, dg_1, db_1 = _tail_bwd("l1", dx2, saved1, wts[1], ln_g_full[1], p_l[1])
    rs_mlp = scatter_start("mlp1", [gw1["w1"], gw1["w2"], gw1["plew"], gw1["gate"]])
    (dyn,) = _mm_dx("out_proj_dx", du2, out_g, "row", [F32])
    g_out = _mm_dw("out_proj_dw", yn, du2, "row")
    dy_ssd, dz, d_norm_w = _gated_rms_bwd("gated_rms_bwd", y_ssd, zx, _token_add(norm_w_full, rs_mlp[3]), dyn)
    dxs, dbm, dcm, ddt_x, dacs, dd = _ssd_bwd("ssd_bwd", xbc, dt_col, a_col, a_row, d_skip, prev, dy_ssd,
                                              di, g_n, hpg, p_dim)
    draw, d_bias, d_alog = _dt_bwd("dt_bwd", zx, bias128, alog128, from_col(dacs), from_col(ddt_x), dt_block)
    dxbc, d_conv_w, d_conv_b = _conv_bwd("conv_bwd", zx, conv_w_full, conv_b_full,
                                         jnp.concatenate([dxs, dbm, dcm], axis=1), di)
    dzx = jnp.concatenate([dz, dxbc, draw], axis=1)
    (dx1,) = _mm_dx("in_proj_dx", dzx, in_full, "plain", [F32], _residual_epilogue, (du2,))
    g_in = _to_slots(_mm_dw("in_proj_dw", x1, dzx, "plain")[:, :dp], 1)
    rs_ssm = scatter_start("ssm", [g_in, g_out])

    du0, gw0, dg_0, db_0 = _tail_bwd("l0", dx1, saved0, wts[0], _token_add(ln_g_full[0], rs_ssm[3]), p_l[0])
    dh, dpool, d_scale = _pool_bwd_mm("pool_bwd_mm", du0, hraw, pool_full, pool_scale)
    grad_x = _pool_windows("pool_bwd", dpool, True, du0)
    g_pool = _to_slots(_pool_dw("pool_dw", pooled, dh), 1)

    sums0 = sibling_sums("layer0", [g_pool, gw0["w1"], gw0["w2"], gw0["plew"], gw0["gate"]])
    q_w1_1, q_w2_1, q_plew_1, q_gate_1 = as_parts(*_async_wait("rs_mlp1_wait", _rs_chip_copies, rs_mlp, 4, (grad_x,)))
    q_in, q_out = as_parts(*_async_wait("rs_ssm_wait", _rs_chip_copies, rs_ssm, 2, (grad_x,)))
    q_pool, q_w1_0, q_w2_0, q_plew_0, q_gate_0 = as_parts(sums0, _rs_chip_exchange("rs_layer0_chips", sums0))

    def update(tag, w, m, v, by_layer):
        res = None
        for layer in sorted(by_layer, reverse=True):
            own, recv = by_layer[layer]
            res = _adamw(f"adamw_{tag}_{layer}", _as3d(w), _as3d(m), _as3d(v),
                         [(own, None), (recv, 0), (recv, 1), (recv, 2)], layer, res, chip)
        return [r.reshape(w.shape) for r in res]

    upd = {
        "pool_w": update("pool_w", pool_w, m_pool_w, v_pool_w, {0: q_pool}),
        "ssm_in_w": update("ssm_in_w", ssm_in_w, m_ssm_in_w, v_ssm_in_w, {0: q_in}),
        "ssm_out_w": update("ssm_out_w", ssm_out_w, m_ssm_out_w, v_ssm_out_w, {0: q_out}),
        "mlp_w1": update("mlp_w1", mlp_w1, m_mlp_w1, v_mlp_w1, {0: q_w1_0, 1: q_w1_1}),
        "mlp_w2": update("mlp_w2", mlp_w2, m_mlp_w2, v_mlp_w2, {0: q_w2_0, 1: q_w2_1}),
        "ple_w": update("ple_w", ple_w, m_ple_w, v_ple_w, {0: q_plew_0, 1: q_plew_1}),
        "ple_gate_w": update("ple_gate_w", ple_gate_w, m_ple_gate_w, v_ple_gate_w, {0: q_gate_0, 1: q_gate_1}),
    }

    d_ln_g = jnp.stack([jnp.stack(dg_0), jnp.stack(dg_1)]).reshape(2, 2, d)
    d_ln_b = jnp.stack([jnp.stack(db_0), jnp.stack(db_1)]).reshape(2, 2, d)
    partial_shapes = [(CONV_WIDTH, cd), (1, cd), (1, di), (2, 2, d), (2, 2, d), (1, d), (1, h_n), (1, h_n), (1, h_n),
                      (1, d)]
    partial = _pack([d_conv_w, d_conv_b, d_norm_w, d_ln_g, d_ln_b, d_scale, d_bias[:, :h_n], d_alog[:, :h_n],
                     dd.reshape(1, h_n), loss_cols], 8 * V7X_LANES)
    (all_partials,) = _all_gather("ag_small_grads", [partial])
    tot = _unpack(_sum8("sum_small_grads", all_partials), partial_shapes)
    t_conv_w, t_conv_b, t_norm_w, t_ln_g, t_ln_b, t_scale, t_bias, t_alog, t_dd, t_loss = tot
    loss = jnp.sum(t_loss)

    def mine(a, per):
        return lax.dynamic_slice_in_dim(a, me * per, per, axis=a.ndim - 1)

    small_names = ["ssm_conv_w", "ssm_conv_b", "ssm_norm_w", "ln_g", "ln_b", "pool_scale", "ssm_dt_bias", "ssm_a_log",
                   "ssm_d"]
    small_w = [ssm_conv_w, ssm_conv_b, ssm_norm_w, ln_g, ln_b, pool_scale, ssm_dt_bias, ssm_a_log, ssm_d]
    small_m = [m_ssm_conv_w, m_ssm_conv_b, m_ssm_norm_w, m_ln_g, m_ln_b, m_pool_scale, m_ssm_dt_bias, m_ssm_a_log,
               m_ssm_d]
    small_v = [v_ssm_conv_w, v_ssm_conv_b, v_ssm_norm_w, v_ln_g, v_ln_b, v_pool_scale, v_ssm_dt_bias, v_ssm_a_log,
               v_ssm_d]
    small_grads = [mine(t_conv_w, cd_s), mine(t_conv_b, cd_s), mine(t_norm_w, di_s), mine(t_ln_g, d_s),
                   mine(t_ln_b, d_s), t_scale, t_bias, t_alog, t_dd]
    shapes = [w.shape for w in small_w]
    pk = [_pack(group, 8 * V7X_LANES)[None] for group in (small_w, small_m, small_v, small_grads)]
    res = _adamw("adamw_small", pk[0], pk[1], pk[2], [(pk[3], 0)], 0)
    for name, vals in zip(small_names, zip(*[_unpack(r, shapes) for r in res])):
        upd[name] = list(vals)

    order = ["pool_w", "pool_scale", "ssm_in_w", "ssm_conv_w", "ssm_conv_b", "ssm_dt_bias", "ssm_a_log", "ssm_d",
             "ssm_norm_w", "ssm_out_w", "mlp_w1", "mlp_w2", "ln_g", "ln_b", "ple_w", "ple_gate_w"]
    out = [loss, grad_x[None]]
    for k in range(4):
        out += [upd[name][k] for name in order]
    return tuple(out)
```

```python
import jax
import jax.numpy as jnp
from jax import lax
from jax.experimental import pallas as pl
from jax.experimental.pallas import tpu as pltpu

F32 = jnp.float32
BF16 = jnp.bfloat16
SDS = jax.ShapeDtypeStruct
MESH = pl.DeviceIdType.MESH
ANY = pl.BlockSpec(memory_space=pl.ANY)

N_DEV = 8
DEPTH = 2
ALPHA = (2.0 * DEPTH) ** 0.25
LN_EPS = 1e-5
RMS_EPS = 1e-5
POOL_WINDOW_LOG2 = (1, 2, 3, 4)
D_STATE = 128
CHUNK = 128
CONV_WIDTH = 4
ADAM_LR = 0.001
ADAM_B1 = 0.9
ADAM_B2 = 0.999
ADAM_EPS = 1e-08
ADAM_WD = 0.01
ADAM_STEP = 10

V7X_LANES = 128
V7X_VMEM_LIMIT = 48 * 1024 * 1024


def _cp(*sem):
    return pltpu.CompilerParams(dimension_semantics=sem, vmem_limit_bytes=V7X_VMEM_LIMIT)


def _pick(dim, cap):
    if dim <= cap:
        return dim
    best = None
    for t in range(V7X_LANES, cap + 1, V7X_LANES):
        if dim % t == 0:
            best = t
    assert best is not None, (dim, cap)
    return best


def _row_tile(rows, cols, itemsize=4, target=1 << 20):
    t = rows
    while t % 2 == 0 and t // 2 >= 16 and (t // 2) % 16 == 0 and t * cols * itemsize > target:
        t //= 2
    return t


def _slot(s):
    return (s % 2) * 4 + s // 2


def _all_gather(name, shards):
    n = len(shards)

    def body(*refs):
        ins, outs = refs[:n], refs[n:2 * n]
        send_sems, recv_sems, local_sems = refs[2 * n:]
        x, y, c = lax.axis_index("x"), lax.axis_index("y"), lax.axis_index("c")
        me, sibling = (x, y, c), (x, y, 1 - c)
        chips = [(1 - x, y), (x, 1 - y), (1 - x, 1 - y)]

        def copy(a, k, block, to, src=None):
            dst = outs[a].at[4 * block[0] + 2 * block[1] + block[2]]
            return pltpu.make_async_remote_copy(
                src_ref=dst if src is None else src, dst_ref=dst, send_sem=send_sems.at[a, k],
                recv_sem=recv_sems.at[a, k], device_id=to, device_id_type=MESH)

        mine = [pltpu.make_async_copy(ins[a], outs[a].at[4 * x + 2 * y + c], local_sems.at[a]) for a in range(n)]
        for cp in mine:
            cp.start()
        first = []
        for a in range(n):
            first.append(copy(a, 0, me, sibling, src=ins[a]))
            first += [copy(a, 1 + j, me, (*chip, c), src=ins[a]) for j, chip in enumerate(chips)]
        for cp in first:
            cp.start()
        passed = []
        for j, chip in enumerate(chips):
            for a in range(n):
                copy(a, 1 + j, (*chip, c), me).wait_recv()
                fwd = copy(a, 4 + j, (*chip, c), sibling)
                fwd.start()
                passed.append(fwd)
        for a in range(n):
            copy(a, 0, sibling, me).wait_recv()
            for j, chip in enumerate(chips):
                copy(a, 4 + j, (*chip, 1 - c), me).wait_recv()
        for cp in first + passed:
            cp.wait_send()
        for cp in mine:
            cp.wait()

    return pl.pallas_call(
        body, name=name,
        out_shape=[SDS((N_DEV,) + s.shape, s.dtype) for s in shards],
        in_specs=[ANY] * n, out_specs=[ANY] * n,
        scratch_shapes=[pltpu.SemaphoreType.DMA((n, 7)), pltpu.SemaphoreType.DMA((n, 7)),
                        pltpu.SemaphoreType.DMA((n,))],
    )(*shards)


def _rs_sibling_exchange(name, grads):
    n = len(grads)

    def body(*refs):
        ins, outs = refs[:n], refs[n:2 * n]
        send_sems, recv_sems = refs[2 * n:]
        x, y, c = lax.axis_index("x"), lax.axis_index("y"), lax.axis_index("c")
        remote = [pltpu.make_async_remote_copy(
            src_ref=ins[a].at[pl.ds(4 * (1 - c), 4)], dst_ref=outs[a], send_sem=send_sems.at[a],
            recv_sem=recv_sems.at[a], device_id=(x, y, 1 - c), device_id_type=MESH) for a in range(n)]
        for cp in remote:
            cp.start()
        for cp in remote:
            cp.wait()

    return pl.pallas_call(
        body, name=name,
        out_shape=[SDS((4,) + g.shape[1:], g.dtype) for g in grads],
        in_specs=[ANY] * n, out_specs=[ANY] * n,
        scratch_shapes=[pltpu.SemaphoreType.DMA((n,)), pltpu.SemaphoreType.DMA((n,))],
    )(*grads)


HBM_SPEC = pl.BlockSpec(memory_space=pltpu.HBM)
SEM_SPEC = pl.BlockSpec(memory_space=pltpu.SEMAPHORE)
EFFECT = pltpu.SideEffectType.DATAFLOW_SIDE_EFFECTING


def _ag_first_copies(ins, lands, send_sems, recv_sems):
    x, y, c = lax.axis_index("x"), lax.axis_index("y"), lax.axis_index("c")
    targets = [(x, y, 1 - c), (1 - x, y, c), (x, 1 - y, c), (1 - x, 1 - y, c)]
    return [pltpu.make_async_remote_copy(
        src_ref=ins[a], dst_ref=lands[a].at[4 * x + 2 * y + c], send_sem=send_sems.at[4 * a + k],
        recv_sem=recv_sems.at[4 * a + k], device_id=to, device_id_type=MESH)
        for a in range(len(ins)) for k, to in enumerate(targets)]


def _ag_forward_copies(ins, lands, send_sems, recv_sems):
    x, y, c = lax.axis_index("x"), lax.axis_index("y"), lax.axis_index("c")
    cps = []
    for a in range(len(lands)):
        for j, (px, py) in enumerate([(1 - x, y), (x, 1 - y), (1 - x, 1 - y)]):
            blk = lands[a].at[4 * px + 2 * py + c]
            cps.append(pltpu.make_async_remote_copy(
                src_ref=blk, dst_ref=blk, send_sem=send_sems.at[3 * a + j], recv_sem=recv_sems.at[3 * a + j],
                device_id=(x, y, 1 - c), device_id_type=MESH))
    return cps


def _rs_chip_copies(ins, lands, send_sems, recv_sems):
    x, y, c = lax.axis_index("x"), lax.axis_index("y"), lax.axis_index("c")
    cps = []
    for a in range(len(ins)):
        for j, (px, py) in enumerate([(1 - x, y), (x, 1 - y), (1 - x, 1 - y)]):
            cps.append(pltpu.make_async_remote_copy(
                src_ref=ins[a].at[2 * px + py], dst_ref=lands[a].at[j], send_sem=send_sems.at[3 * a + j],
                recv_sem=recv_sems.at[3 * a + j], device_id=(px, py, c), device_id_type=MESH))
    return cps


def _async_start(name, build, sem_shape, ins, lands, after=()):
    arrays = [*ins, *lands]
    n_i, n_t, n_a = len(ins), len(arrays), len(after)

    def body(*refs):
        outs = refs[n_t + n_a:]
        for cp in build(refs[:n_i], refs[n_i:n_t], outs[0], outs[1]):
            cp.start()
        outs[-1][...] = jnp.zeros_like(outs[-1])

    res = pl.pallas_call(
        body, name=name,
        out_shape=(pltpu.SemaphoreType.DMA(sem_shape), pltpu.SemaphoreType.DMA(sem_shape),
                   *[pltpu.HBM(a.shape, a.dtype) for a in arrays], SDS((8, V7X_LANES), F32)),
        in_specs=[HBM_SPEC] * n_t + [ANY] * n_a,
        out_specs=(SEM_SPEC, SEM_SPEC, *[HBM_SPEC] * n_t, pl.BlockSpec(memory_space=pltpu.VMEM)),
        input_output_aliases={i: 2 + i for i in range(n_t)},
        compiler_params=pltpu.CompilerParams(has_side_effects=EFFECT),
    )(*[pltpu.with_memory_space_constraint(a, pltpu.HBM) for a in arrays], *after)
    return res[0], res[1], list(res[2:2 + n_t]), res[-1]


def _async_wait(name, build, handle, n_i, after=()):
    send_sems, recv_sems, arrays, _ = handle
    n_t, n_a = len(arrays), len(after)

    def body(*refs):
        for cp in build(refs[:n_i], refs[n_i:n_t], refs[n_t], refs[n_t + 1]):
            cp.wait_send()
            cp.wait_recv()

    res = pl.pallas_call(
        body, name=name, out_shape=tuple(pltpu.HBM(a.shape, a.dtype) for a in arrays),
        in_specs=[HBM_SPEC] * n_t + [SEM_SPEC, SEM_SPEC] + [ANY] * n_a, out_specs=tuple([HBM_SPEC] * n_t),
        input_output_aliases={i: i for i in range(n_t)},
        compiler_params=pltpu.CompilerParams(has_side_effects=EFFECT),
    )(*arrays, send_sems, recv_sems, *after)
    return list(res[:n_i]), list(res[n_i:])


def _token_add(a, *tokens):
    for tok in tokens:
        a = a + tok[0:1, 0:1].reshape((1,) * a.ndim)
    return a


def _mm_core(name, a, b, *, grid, a_spec, b_spec, dims, acc_shape, outs, out_spec, epilogue=None, extras=(),
             extra_specs=()):
    nk = grid[2]
    ne, no = len(extras), len(outs)

    def body(a_ref, b_ref, *rest):
        e_refs, o_refs, acc = rest[:ne], rest[ne:ne + no], rest[ne + no]
        k = pl.program_id(2)

        @pl.when(k == 0)
        def _():
            acc[...] = jnp.zeros_like(acc)

        acc[...] += lax.dot_general(a_ref[...].astype(BF16), b_ref[...].astype(BF16), dims,
                                    preferred_element_type=F32)

        @pl.when(k == nk - 1)
        def _():
            r = acc[...]
            vals = epilogue(r, *[e[...] for e in e_refs]) if epilogue is not None else (r,)
            for o, v in zip(o_refs, vals):
                o[...] = v.astype(o.dtype)

    res = pl.pallas_call(
        body, name=name, grid=grid, out_shape=list(outs),
        in_specs=[a_spec, b_spec, *extra_specs], out_specs=[out_spec] * no,
        scratch_shapes=[pltpu.VMEM(acc_shape, F32)],
        compiler_params=_cp("parallel", "parallel", "arbitrary"),
    )(a, b, *extras)
    return res


NN = (((1,), (0,)), ((), ()))
NT = (((1,), (1,)), ((), ()))
TN = (((0,), (0,)), ((), ()))


def _w_dims(w, kind):
    if kind == "col":
        return w.shape[1], N_DEV * w.shape[2], w.shape[1], w.shape[2]
    if kind == "row":
        return N_DEV * w.shape[1], w.shape[2], w.shape[1], w.shape[2]
    return w.shape[0], w.shape[1], w.shape[0], w.shape[1]


def _mm_fwd(name, a, w, kind, out_dtypes, epilogue=None, extras=()):
    m, k_dim = a.shape
    kk, n, ks, ns = _w_dims(w, kind)
    assert kk == k_dim
    tm = _pick(m, 1024)
    if kind == "col":
        tn, tk = _pick(ns, 1024), _pick(kk, 512)
        nb = ns // tn
        b_spec = pl.BlockSpec((None, tk, tn), lambda i, j, k: (j // nb, k, j % nb))
    elif kind == "row":
        tn, tk = _pick(n, 1024), _pick(ks, 512)
        kb = ks // tk
        b_spec = pl.BlockSpec((None, tk, tn), lambda i, j, k: (k // kb, k % kb, j))
    else:
        tn, tk = _pick(n, 1152), _pick(kk, 512)
        b_spec = pl.BlockSpec((tk, tn), lambda i, j, k: (k, j))
    mn_spec = pl.BlockSpec((tm, tn), lambda i, j, k: (i, j))
    return _mm_core(
        name, a, w, grid=(m // tm, n // tn, kk // tk),
        a_spec=pl.BlockSpec((tm, tk), lambda i, j, k: (i, k)), b_spec=b_spec, dims=NN, acc_shape=(tm, tn),
        outs=[SDS((m, n), dt) for dt in out_dtypes], out_spec=mn_spec, epilogue=epilogue, extras=extras,
        extra_specs=[mn_spec] * len(extras))


def _mm_dx(name, dy, w, kind, out_dtypes, epilogue=None, extras=()):
    m, n_dim = dy.shape
    kk, n, ks, ns = _w_dims(w, kind)
    assert n == n_dim
    tm = _pick(m, 1024)
    if kind == "col":
        tn, tk = _pick(kk, 1024), _pick(ns, 512)
        kb = ns // tk
        b_spec = pl.BlockSpec((None, tn, tk), lambda i, j, k: (k // kb, j, k % kb))
    elif kind == "row":
        tn, tk = _pick(ks, 1024), _pick(n, 512)
        nb = ks // tn
        b_spec = pl.BlockSpec((None, tn, tk), lambda i, j, k: (j // nb, j % nb, k))
    else:
        tn, tk = _pick(kk, 1024), _pick(n, 1152)
        b_spec = pl.BlockSpec((tn, tk), lambda i, j, k: (j, k))
    mk_spec = pl.BlockSpec((tm, tn), lambda i, j, k: (i, j))
    return _mm_core(
        name, dy, w, grid=(m // tm, kk // tn, n // tk),
        a_spec=pl.BlockSpec((tm, tk), lambda i, j, k: (i, k)), b_spec=b_spec, dims=NT, acc_shape=(tm, tn),
        outs=[SDS((m, kk), dt) for dt in out_dtypes], out_spec=mk_spec, epilogue=epilogue, extras=extras,
        extra_specs=[mk_spec] * len(extras))


def _mm_dw(name, a, dy, kind):
    m, kk = a.shape
    n = dy.shape[1]
    tk = _pick(m, 512)
    if kind == "col":
        ns = n // N_DEV
        tm, tn = _pick(kk, 1024), _pick(ns, 1024)
        nb = ns // tn
        out = SDS((N_DEV, kk, ns), BF16)
        out_spec = pl.BlockSpec((None, tm, tn), lambda i, j, k: (_slot(j // nb), i, j % nb))
    elif kind == "row":
        ks = kk // N_DEV
        tm, tn = _pick(ks, 1024), _pick(n, 1024)
        mb = ks // tm
        out = SDS((N_DEV, ks, n), BF16)
        out_spec = pl.BlockSpec((None, tm, tn), lambda i, j, k: (_slot(i // mb), i % mb, j))
    else:
        tm, tn = _pick(kk, 1024), _pick(n, 1152)
        out = SDS((kk, n), BF16)
        out_spec = pl.BlockSpec((tm, tn), lambda i, j, k: (i, j))
    return _mm_core(
        name, a, dy, grid=(kk // tm, n // tn, m // tk),
        a_spec=pl.BlockSpec((tk, tm), lambda i, j, k: (k, i)), b_spec=pl.BlockSpec((tk, tn), lambda i, j, k: (k, j)),
        dims=TN, acc_shape=(tm, tn), outs=[out], out_spec=out_spec)[0]


def _rowwise(name, fn, ins, outs, rows, tile):
    arrays, specs = [], []
    for arr, kind in ins:
        arrays.append(arr)
        if kind == "row":
            specs.append(pl.BlockSpec((tile, arr.shape[1]), lambda i: (i, 0)))
        elif kind == "vec":
            specs.append(pl.BlockSpec(arr.shape, lambda i, nd=arr.ndim: (0,) * nd))
        else:
            specs.append(kind)
    out_shapes, out_specs, kinds = [], [], []
    for cols, dt, kind in outs:
        kinds.append(kind)
        if kind == "row":
            out_shapes.append(SDS((rows, cols), dt))
            out_specs.append(pl.BlockSpec((tile, cols), lambda i: (i, 0)))
        else:
            out_shapes.append(SDS((1, cols), F32))
            out_specs.append(pl.BlockSpec((1, cols), lambda i: (0, 0)))
    ni = len(arrays)
    has_acc = "acc" in kinds

    def body(*refs):
        vals = fn(*[r[...] for r in refs[:ni]])
        i = pl.program_id(0)
        for o, v, kind in zip(refs[ni:], vals, kinds):
            if kind == "row":
                o[...] = v.astype(o.dtype)
            else:
                @pl.when(i == 0)
                def _(o=o):
                    o[...] = jnp.zeros_like(o)

                o[...] += v

    return pl.pallas_call(
        body, name=name, grid=(rows // tile,), out_shape=out_shapes, in_specs=specs, out_specs=out_specs,
        compiler_params=_cp("arbitrary" if has_acc else "parallel"),
    )(*arrays)


def _ln_fwd(name, u, g, b):
    d = u.shape[1]

    def fn(u, g, b):
        mu = jnp.mean(u, axis=1, keepdims=True)
        xc = u - mu
        var = jnp.mean(xc * xc, axis=1, keepdims=True)
        return (xc * lax.rsqrt(var + LN_EPS) * g + b,)

    return _rowwise(name, fn, [(u, "row"), (g, "vec"), (b, "vec")], [(d, F32, "row")], u.shape[0], 256)[0]


def _ln_bwd(name, u, dy, g):
    d = u.shape[1]

    def fn(u, dy, g):
        mu = jnp.mean(u, axis=1, keepdims=True)
        xc = u - mu
        var = jnp.mean(xc * xc, axis=1, keepdims=True)
        rstd = lax.rsqrt(var + LN_EPS)
        xhat = xc * rstd
        dxhat = dy * g
        m1 = jnp.mean(dxhat, axis=1, keepdims=True)
        m2 = jnp.mean(dxhat * xhat, axis=1, keepdims=True)
        du = rstd * (dxhat - m1 - xhat * m2)
        return du, jnp.sum(dy * xhat, axis=0, keepdims=True), jnp.sum(dy, axis=0, keepdims=True)

    return _rowwise(name, fn, [(u, "row"), (dy, "row"), (g, "vec")],
                    [(d, F32, "row"), (d, F32, "acc"), (d, F32, "acc")], u.shape[0], 256)


def _loss_bwd(name, y, target):
    d = y.shape[1]

    def fn(y, t):
        e = y - t
        return e * (1.0 / d), jnp.sum(e * e, axis=0, keepdims=True) * (0.5 / d)

    return _rowwise(name, fn, [(y, "row"), (target, "row")], [(d, F32, "row"), (d, F32, "acc")], y.shape[0], 256)


def _ple_bwd(name, dx, e, gate):
    d = dx.shape[1]

    def fn(dx, e, gate):
        return dx * e * gate * (1.0 - gate), dx * gate

    return _rowwise(name, fn, [(dx, "row"), (e, "row"), (gate, "row")], [(d, BF16, "row"), (d, BF16, "row")],
                    dx.shape[0], 256)


def _sigmoid(v):
    return 1.0 / (1.0 + jnp.exp(-v))


def _gated_rms_fwd(name, y, zx, norm_w):
    di = y.shape[1]

    def fn(y, z, w):
        yg = y * (z * _sigmoid(z))
        r = lax.rsqrt(jnp.mean(yg * yg, axis=1, keepdims=True) + RMS_EPS)
        return (yg * r * w,)

    z_spec = pl.BlockSpec((128, di), lambda i: (i, 0))
    return _rowwise(name, fn, [(y, "row"), (zx, z_spec), (norm_w, "vec")], [(di, BF16, "row")], y.shape[0], 128)[0]


def _gated_rms_bwd(name, y, zx, norm_w, dout):
    di = y.shape[1]

    def fn(y, z, w, dout):
        sg = _sigmoid(z)
        sz = z * sg
        yg = y * sz
        r = lax.rsqrt(jnp.mean(yg * yg, axis=1, keepdims=True) + RMS_EPS)
        dn = dout * w
        dyg = r * (dn - yg * (r * r) * jnp.mean(dn * yg, axis=1, keepdims=True))
        dy = dyg * sz
        dz = dyg * y * (sg * (1.0 + z * (1.0 - sg)))
        return dy, dz, jnp.sum(dout * yg * r, axis=0, keepdims=True)

    z_spec = pl.BlockSpec((128, di), lambda i: (i, 0))
    return _rowwise(name, fn, [(y, "row"), (zx, z_spec), (norm_w, "vec"), (dout, "row")],
                    [(di, F32, "row"), (di, BF16, "row"), (di, F32, "acc")], y.shape[0], 128)


def _shift_down(v, j, row):
    return jnp.where(row >= j, pltpu.roll(v, j, 0), 0.0)


def _shift_up(v, j, row):
    t = v.shape[0]
    return jnp.where(row < t - j, pltpu.roll(v, t - j, 0), 0.0)


def _pool_select(parts, g):
    return jnp.where(g == 0, parts[0], jnp.where(g == 1, parts[1], jnp.where(g == 2, parts[2], parts[3])))


def _pool_windows(name, x, transpose, scale_by=None):
    t, d = x.shape
    cg = d // 4
    cw = V7X_LANES
    per = cg // cw

    def body(*refs):
        x_ref, o_ref = refs[0], refs[-1]
        g = pl.program_id(0) // per
        xv = x_ref[...]
        row = lax.broadcasted_iota(jnp.int32, (t, 1), 0)
        cnt = jnp.minimum(row + 1, jnp.left_shift(2, g)).astype(F32)
        s = xv / cnt if transpose else xv
        parts = []
        for lg in POOL_WINDOW_LOG2:
            j = 1 << (lg - 1)
            s = s + (_shift_up(s, j, row) if transpose else _shift_down(s, j, row))
            parts.append(s)
        sel = _pool_select(parts, g)
        if transpose:
            o_ref[...] = ALPHA * refs[1][...] + sel - xv
        else:
            o_ref[...] = (sel / cnt - xv).astype(o_ref.dtype)

    col = pl.BlockSpec((t, cw), lambda j: (0, j))
    ins = [x] if scale_by is None else [x, scale_by]
    return pl.pallas_call(
        body, name=name, grid=(d // cw,), out_shape=SDS((t, d), F32 if transpose else BF16),
        in_specs=[col] * len(ins), out_specs=col, compiler_params=_cp("parallel"),
    )(*ins)


def _pool_mm(name, pooled, w, scale, x):
    t, d = x.shape
    cg = d // 4
    tm = _pick(t, 1024)

    def body(p_ref, w_ref, s_ref, x_ref, u_ref, h_ref):
        h = jnp.dot(p_ref[...], w_ref[...], preferred_element_type=F32)
        h_ref[...] = h
        u_ref[...] = ALPHA * x_ref[...] + h * s_ref[...]

    blk = pl.BlockSpec((tm, cg), lambda g, i: (i, g))
    return pl.pallas_call(
        body, name=name, grid=(4, t // tm), out_shape=[SDS((t, d), F32), SDS((t, d), F32)],
        in_specs=[blk, pl.BlockSpec((None, cg, cg), lambda g, i: (g, 0, 0)), pl.BlockSpec((1, cg), lambda g, i: (0, g)),
                  blk],
        out_specs=[blk, blk], compiler_params=_cp("parallel", "parallel"),
    )(pooled, w, scale, x)


def _pool_bwd_mm(name, du, hraw, w, scale):
    t, d = du.shape
    cg = d // 4
    tm = _pick(t, 1024)

    def body(du_ref, h_ref, w_ref, s_ref, dh_ref, dp_ref, ds_ref):
        @pl.when(pl.program_id(1) == 0)
        def _():
            ds_ref[...] = jnp.zeros_like(ds_ref)

        duv = du_ref[...]
        ds_ref[...] += jnp.sum(duv * h_ref[...], axis=0, keepdims=True)
        dh = (duv * s_ref[...]).astype(BF16)
        dh_ref[...] = dh
        dp_ref[...] = lax.dot_general(dh, w_ref[...], NT, preferred_element_type=F32)

    blk = pl.BlockSpec((tm, cg), lambda g, i: (i, g))
    vec = pl.BlockSpec((1, cg), lambda g, i: (0, g))
    return pl.pallas_call(
        body, name=name, grid=(4, t // tm), out_shape=[SDS((t, d), BF16), SDS((t, d), F32), SDS((1, d), F32)],
        in_specs=[blk, blk, pl.BlockSpec((None, cg, cg), lambda g, i: (g, 0, 0)), vec],
        out_specs=[blk, blk, vec], compiler_params=_cp("parallel", "arbitrary"),
    )(du, hraw, w, scale)


def _pool_dw(name, pooled, dh):
    t, d = pooled.shape
    cg = d // 4
    tk = _pick(t, 512)
    nk = t // tk

    def body(p_ref, dh_ref, o_ref, acc):
        k = pl.program_id(1)

        @pl.when(k == 0)
        def _():
            acc[...] = jnp.zeros_like(acc)

        acc[...] += lax.dot_general(p_ref[...], dh_ref[...], TN, preferred_element_type=F32)

        @pl.when(k == nk - 1)
        def _():
            o_ref[...] = acc[...].astype(o_ref.dtype)

    blk = pl.BlockSpec((tk, cg), lambda g, k: (k, g))
    return pl.pallas_call(
        body, name=name, grid=(4, nk), out_shape=SDS((4, cg, cg), BF16), in_specs=[blk, blk],
        out_specs=pl.BlockSpec((None, cg, cg), lambda g, k: (g, 0, 0)), scratch_shapes=[pltpu.VMEM((cg, cg), F32)],
        compiler_params=_cp("parallel", "arbitrary"),
    )(pooled, dh)


def _conv_pre(u, w_ref, b_ref, row):
    pre = b_ref[...] + _shift_down(u, 3, row) * w_ref[0:1, :]
    pre = pre + _shift_down(u, 2, row) * w_ref[1:2, :]
    pre = pre + _shift_down(u, 1, row) * w_ref[2:3, :]
    return pre + u * w_ref[3:4, :]


def _conv_fwd(name, zx, conv_w, conv_b, di):
    t = zx.shape[0]
    cd = conv_w.shape[1]
    cw = _pick(cd, 256)
    off = di // cw

    def body(u_ref, w_ref, b_ref, o_ref):
        row = lax.broadcasted_iota(jnp.int32, (t, 1), 0)
        pre = _conv_pre(u_ref[...], w_ref, b_ref, row)
        o_ref[...] = pre * _sigmoid(pre)

    return pl.pallas_call(
        body, name=name, grid=(cd // cw,), out_shape=SDS((t, cd), F32),
        in_specs=[pl.BlockSpec((t, cw), lambda j: (0, off + j)), pl.BlockSpec((CONV_WIDTH, cw), lambda j: (0, j)),
                  pl.BlockSpec((1, cw), lambda j: (0, j))],
        out_specs=pl.BlockSpec((t, cw), lambda j: (0, j)), compiler_params=_cp("parallel"),
    )(zx, conv_w, conv_b)


def _conv_bwd(name, zx, conv_w, conv_b, dact, di):
    t = zx.shape[0]
    cd = conv_w.shape[1]
    cw = _pick(cd, 256)
    off = di // cw

    def body(u_ref, w_ref, b_ref, da_ref, du_ref, dw_ref, db_ref):
        row = lax.broadcasted_iota(jnp.int32, (t, 1), 0)
        u = u_ref[...]
        pre = _conv_pre(u, w_ref, b_ref, row)
        sg = _sigmoid(pre)
        dpre = da_ref[...] * (sg * (1.0 + pre * (1.0 - sg)))
        du = dpre * w_ref[3:4, :]
        for j in (1, 2, 3):
            du = du + _shift_up(dpre, j, row) * w_ref[3 - j:4 - j, :]
            dw_ref[3 - j:4 - j, :] = jnp.sum(dpre * _shift_down(u, j, row), axis=0, keepdims=True)
        dw_ref[3:4, :] = jnp.sum(dpre * u, axis=0, keepdims=True)
        db_ref[...] = jnp.sum(dpre, axis=0, keepdims=True)
        du_ref[...] = du.astype(du_ref.dtype)

    wspec = pl.BlockSpec((CONV_WIDTH, cw), lambda j: (0, j))
    bspec = pl.BlockSpec((1, cw), lambda j: (0, j))
    ospec = pl.BlockSpec((t, cw), lambda j: (0, j))
    return pl.pallas_call(
        body, name=name, grid=(cd // cw,), out_shape=[SDS((t, cd), BF16), SDS((CONV_WIDTH, cd), F32), SDS((1, cd), F32)],
        in_specs=[pl.BlockSpec((t, cw), lambda j: (0, off + j)), wspec, bspec, ospec],
        out_specs=[ospec, wspec, bspec], compiler_params=_cp("parallel"),
    )(zx, conv_w, conv_b, dact)


def _softplus(v):
    return jnp.maximum(v, 0.0) + jnp.log(1.0 + jnp.exp(-jnp.abs(v)))


def _dt_fwd(name, zx, bias, a_log, col_block):
    t = zx.shape[0]

    def body(r_ref, b_ref, al_ref, dt_ref, acs_ref, ein_ref, eout_ref):
        row = lax.broadcasted_iota(jnp.int32, (t, 1), 0) % CHUNK
        dt = _softplus(r_ref[...] + b_ref[...])
        da = dt * (-jnp.exp(al_ref[...]))
        s, r = da, da
        j = 1
        while j < CHUNK:
            s = s + jnp.where(row >= j, pltpu.roll(s, j, 0), 0.0)
            r = r + jnp.where(row < CHUNK - j, pltpu.roll(r, t - j, 0), 0.0)
            j *= 2
        dt_ref[...] = dt
        acs_ref[...] = s
        ein_ref[...] = jnp.exp(s)
        eout_ref[...] = jnp.exp(r - da)

    vec = pl.BlockSpec((1, V7X_LANES), lambda i: (0, 0))
    full = pl.BlockSpec((t, V7X_LANES), lambda i: (0, 0))
    return pl.pallas_call(
        body, name=name, grid=(1,), out_shape=[SDS((t, V7X_LANES), F32)] * 4,
        in_specs=[pl.BlockSpec((t, V7X_LANES), lambda i: (0, col_block)), vec, vec], out_specs=[full] * 4,
        compiler_params=_cp("arbitrary"),
    )(zx, bias, a_log)


def _dt_bwd(name, zx, bias, a_log, d_acs, d_dt, col_block):
    t = zx.shape[0]

    def body(r_ref, b_ref, al_ref, da_ref, dd_ref, draw_ref, db_ref, dal_ref):
        row = lax.broadcasted_iota(jnp.int32, (t, 1), 0) % CHUNK
        pre = r_ref[...] + b_ref[...]
        dt = _softplus(pre)
        a = -jnp.exp(al_ref[...])
        s = da_ref[...]
        j = 1
        while j < CHUNK:
            s = s + jnp.where(row < CHUNK - j, pltpu.roll(s, t - j, 0), 0.0)
            j *= 2
        ddt = dd_ref[...] + s * a
        dal_ref[...] = jnp.sum(s * dt, axis=0, keepdims=True) * a
        draw = ddt * _sigmoid(pre)
        db_ref[...] = jnp.sum(draw, axis=0, keepdims=True)
        draw_ref[...] = draw.astype(draw_ref.dtype)

    vec = pl.BlockSpec((1, V7X_LANES), lambda i: (0, 0))
    full = pl.BlockSpec((t, V7X_LANES), lambda i: (0, 0))
    return pl.pallas_call(
        body, name=name, grid=(1,), out_shape=[SDS((t, V7X_LANES), BF16), SDS((1, V7X_LANES), F32), SDS((1, V7X_LANES), F32)],
        in_specs=[pl.BlockSpec((t, V7X_LANES), lambda i: (0, col_block)), vec, vec, full, full],
        out_specs=[full, vec, vec], compiler_params=_cp("arbitrary"),
    )(zx, bias, a_log, d_acs, d_dt)


def _ssd_specs(t, di, g_n, hpg, p, rev):
    nc = t // CHUNK
    w = hpg * p
    nb = di // D_STATE

    def cc(c):
        return nc - 1 - c if rev else c

    return dict(
        xs=pl.BlockSpec((CHUNK, w), lambda g, c: (cc(c), g)),
        bm=pl.BlockSpec((CHUNK, D_STATE), lambda g, c: (cc(c), nb + g)),
        cm=pl.BlockSpec((CHUNK, D_STATE), lambda g, c: (cc(c), nb + g_n + g)),
        col=pl.BlockSpec((None, CHUNK, hpg), lambda g, c: (g, cc(c), 0)),
        rowv=pl.BlockSpec((None, hpg, CHUNK), lambda g, c: (g, 0, cc(c))),
        head=pl.BlockSpec((None, 1, hpg), lambda g, c: (g, 0, 0)),
        lanes=pl.BlockSpec((1, w), lambda g, c: (0, g)),
        bc=pl.BlockSpec((CHUNK, D_STATE), lambda g, c: (cc(c), g)),
        prev=pl.BlockSpec((None, None, D_STATE, w), lambda g, c: (cc(c), g, 0, 0)),
        seg=pl.BlockSpec((w, V7X_LANES), lambda g, c: (0, 0)),
    )


def _decay_masks(cb, ac_ref, ar_ref, heads):
    li = lax.broadcasted_iota(jnp.int32, (CHUNK, CHUNK), 0)
    si = lax.broadcasted_iota(jnp.int32, (CHUNK, CHUNK), 1)
    lms = [jnp.exp(jnp.where(li >= si, ac_ref[:, hh:hh + 1] - ar_ref[hh:hh + 1, :], -jnp.inf)) for hh in heads]
    return lms, [(cb * lm).astype(BF16) for lm in lms]


def _ssd_fwd(name, xbc, dt_x, ein_x, eout_x, a_col, a_row, d_x, di, g_n, hpg, p):
    t = xbc.shape[0]
    nc = t // CHUNK
    w = hpg * p
    assert 2 * p == V7X_LANES and hpg % 2 == 0
    sp = _ssd_specs(t, di, g_n, hpg, p, False)

    def body(xs_ref, bm_ref, cm_ref, dt_ref, ein_ref, eout_ref, ac_ref, ar_ref, d_ref, y_ref, prev_ref, h_ref):
        @pl.when(pl.program_id(1) == 0)
        def _():
            h_ref[...] = jnp.zeros_like(h_ref)

        bm = bm_ref[...].astype(BF16)
        cm = cm_ref[...].astype(BF16)
        cb = lax.dot_general(cm, bm, NT, preferred_element_type=F32)
        first = lax.broadcasted_iota(jnp.int32, (1, V7X_LANES), 1) < p
        xs = xs_ref[...]
        e_in = ein_ref[...]
        xdt = xs * dt_ref[...]
        ys = []
        for pr in range(hpg // 2):
            _, ms = _decay_masks(cb, ac_ref, ar_ref, (2 * pr, 2 * pr + 1))
            xp = xdt[:, pr * V7X_LANES:(pr + 1) * V7X_LANES]
            rhs = jnp.concatenate([jnp.where(first, xp, 0.0), jnp.where(first, 0.0, xp)], axis=0).astype(BF16)
            ys.append(jnp.dot(jnp.concatenate(ms, axis=1), rhs, preferred_element_type=F32))
        h_prev = h_ref[...]
        prev_ref[...] = h_prev
        y = jnp.concatenate(ys, axis=1) + jnp.dot(cm, h_prev.astype(BF16), preferred_element_type=F32) * e_in
        y_ref[...] = y + xs * d_ref[...]
        st = lax.dot_general(bm, (xdt * eout_ref[...]).astype(BF16), TN, preferred_element_type=F32)
        h_ref[...] = e_in[CHUNK - 1:CHUNK, :] * h_prev + st

    return pl.pallas_call(
        body, name=name, grid=(g_n, nc),
        out_shape=[SDS((t, di), F32), SDS((nc, g_n, D_STATE, w), F32)],
        in_specs=[sp["xs"], sp["bm"], sp["cm"], sp["xs"], sp["xs"], sp["xs"], sp["col"], sp["rowv"], sp["lanes"]],
        out_specs=[sp["xs"], sp["prev"]], scratch_shapes=[pltpu.VMEM((D_STATE, w), F32)],
        compiler_params=_cp("parallel", "arbitrary"),
    )(xbc, xbc, xbc, dt_x, ein_x, eout_x, a_col, a_row, d_x)


def _head_sums(v, seg):
    hi = v.astype(BF16)
    lo = (v - hi.astype(F32)).astype(BF16)
    return jnp.dot(hi, seg, preferred_element_type=F32) + jnp.dot(lo, seg, preferred_element_type=F32)


def _head_totals(v, seg):
    part = v[0:8]
    for r in range(8, v.shape[0], 8):
        part = part + v[r:r + 8]
    return jnp.sum(_head_sums(part, seg), axis=0, keepdims=True)


def _ssd_bwd(name, xbc, dt_x, ein_x, eout_x, a_col, a_row, d_x, prev, dy, di, g_n, hpg, p):
    t = xbc.shape[0]
    nc = t // CHUNK
    w = hpg * p
    sp = _ssd_specs(t, di, g_n, hpg, p, True)
    seg = (lax.broadcasted_iota(jnp.int32, (w, V7X_LANES), 0) // p
           == lax.broadcasted_iota(jnp.int32, (w, V7X_LANES), 1)).astype(BF16)

    def body(xs_ref, bm_ref, cm_ref, dt_ref, ein_ref, eout_ref, ac_ref, ar_ref, d_ref, prev_ref, dy_ref,
             seg_ref, dx_ref, dbm_ref, dcm_ref, ddt_ref, dacs_ref, dd_ref, dh_ref):
        @pl.when(pl.program_id(1) == 0)
        def _():
            dh_ref[...] = jnp.zeros_like(dh_ref)
            dd_ref[...] = jnp.zeros_like(dd_ref)

        bm = bm_ref[...].astype(BF16)
        cm = cm_ref[...].astype(BF16)
        cb = lax.dot_general(cm, bm, NT, preferred_element_type=F32)
        first = lax.broadcasted_iota(jnp.int32, (1, V7X_LANES), 1) < p
        last_row = lax.broadcasted_iota(jnp.int32, (CHUNK, 1), 0) == CHUNK - 1
        seg_m = seg_ref[...]
        xs, dy, e_in, e_out, d_skip = xs_ref[...], dy_ref[...], ein_ref[...], eout_ref[...], d_ref[...]
        dt_l = dt_ref[...]
        xdt = xs * dt_l
        h_prev = prev_ref[...]
        h_prev_b = h_prev.astype(BF16)
        dh_next = dh_ref[...]
        dh_next_b = dh_next.astype(BF16)
        dy_e = (dy * e_in).astype(BF16)
        d_cm = lax.dot_general(dy_e, h_prev_b, NT, preferred_element_type=F32)
        dh_ref[...] = e_in[CHUNK - 1:CHUNK, :] * dh_next + lax.dot_general(cm, dy_e, TN, preferred_element_type=F32)
        q = jnp.dot(bm, dh_next_b, preferred_element_type=F32)
        xf = xdt * e_out
        d_bm = lax.dot_general(xf.astype(BF16), dh_next_b, NT, preferred_element_type=F32)
        d_cb = jnp.zeros((CHUNK, CHUNK), F32)
        parts, w_parts = [], []
        for pr in range(hpg // 2):
            lanes = slice(pr * V7X_LANES, (pr + 1) * V7X_LANES)
            lms, ms = _decay_masks(cb, ac_ref, ar_ref, (2 * pr, 2 * pr + 1))
            xp = xdt[:, lanes]
            xp_b = xp.astype(BF16)
            dyp = dy[:, lanes]
            halves = [jnp.where(first, dyp, 0.0).astype(BF16), jnp.where(first, 0.0, dyp).astype(BF16)]
            for lm, half in zip(lms, halves):
                d_cb = d_cb + lax.dot_general(half, xp_b, NT, preferred_element_type=F32) * lm
            dxd = lax.dot_general(jnp.concatenate(ms, axis=0), jnp.concatenate(halves, axis=0), TN,
                                  preferred_element_type=F32)
            stacked = jnp.concatenate([jnp.where(first, xp, 0.0), jnp.where(first, 0.0, xp)], axis=0).astype(BF16)
            y_diag = jnp.dot(jnp.concatenate(ms, axis=1), stacked, preferred_element_type=F32)
            parts.append(dxd)
            w_parts.append(dyp.astype(BF16).astype(F32) * y_diag - xp_b.astype(F32) * dxd)
        d_xdt = jnp.concatenate(parts, axis=1) + q * e_out
        dx_ref[...] = d_xdt * dt_l + dy * d_skip
        ch = jnp.dot(cm, h_prev_b, preferred_element_type=F32)
        qx = q * xf
        s_a = _head_sums(dy * ch * e_in - qx + jnp.concatenate(w_parts, axis=1), seg_m)[:, :hpg]
        d_last = (_head_totals(qx, seg_m)[:, :hpg]
                  + jnp.exp(ac_ref[CHUNK - 1:CHUNK, :]) * _head_totals(dh_next * h_prev, seg_m)[:, :hpg])
        ddt_ref[...] = _head_sums(d_xdt * xs, seg_m)[:, :hpg]
        dacs_ref[...] = s_a + jnp.where(last_row, d_last, 0.0)
        dd_ref[...] += _head_totals(dy * xs, seg_m)[:, :hpg]
        d_cb_b = d_cb.astype(BF16)
        dcm_ref[...] = d_cm + jnp.dot(d_cb_b, bm, preferred_element_type=F32)
        dbm_ref[...] = d_bm + lax.dot_general(d_cb_b, cm, TN, preferred_element_type=F32)

    gn = g_n * D_STATE
    return pl.pallas_call(
        body, name=name, grid=(g_n, nc),
        out_shape=[SDS((t, di), F32), SDS((t, gn), F32), SDS((t, gn), F32), SDS((g_n, t, hpg), F32),
                   SDS((g_n, t, hpg), F32), SDS((g_n, 1, hpg), F32)],
        in_specs=[sp["xs"], sp["bm"], sp["cm"], sp["xs"], sp["xs"], sp["xs"], sp["col"], sp["rowv"],
                  sp["lanes"], sp["prev"], sp["xs"], sp["seg"]],
        out_specs=[sp["xs"], sp["bc"], sp["bc"], sp["col"], sp["col"], sp["head"]],
        scratch_shapes=[pltpu.VMEM((D_STATE, w), F32)],
        compiler_params=_cp("parallel", "arbitrary"),
    )(xbc, xbc, xbc, dt_x, ein_x, eout_x, a_col, a_row, d_x, prev, dy, seg)


def _as3d(a):
    return a.reshape(a.shape[0], -1, a.shape[-1])


def _pair_sum(name, own, recv, core):
    shape = recv.shape
    cols = shape[-1]
    own3, recv3 = own.reshape(8, -1, cols), recv.reshape(4, -1, cols)
    rows = recv3.shape[1]
    tr = _row_tile(rows, cols, 2)

    def body(c_ref, a_ref, b_ref, o_ref):
        o_ref[...] = (a_ref[...].astype(F32) + b_ref[...].astype(F32)).astype(o_ref.dtype)

    blk = pl.BlockSpec((None, tr, cols), lambda q, i, c_ref: (q, i, 0))
    out = pl.pallas_call(
        body, name=name, out_shape=SDS(recv3.shape, recv.dtype),
        grid_spec=pltpu.PrefetchScalarGridSpec(
            num_scalar_prefetch=1, grid=(4, rows // tr),
            in_specs=[pl.BlockSpec((None, tr, cols), lambda q, i, c_ref: (4 * c_ref[0] + q, i, 0)), blk], out_specs=blk),
        compiler_params=_cp("parallel", "parallel"),
    )(core, own3, recv3)
    return out.reshape(shape)


def _adamw(name, w, m, v, parts, layer, prev=None, sel=None):
    lyr, rows, cols = w.shape
    n = len(parts)
    tr = _row_tile(rows, cols)
    np_ = 0 if prev is None else 4
    if sel is None:
        sel = jnp.zeros((1,), jnp.int32)

    def body(sel_ref, *refs):
        w_ref, m_ref, v_ref = refs[:3]
        p_refs = refs[3:3 + n]
        g_ref, d_ref, nm_ref, nv_ref = refs[3 + n + np_:]
        g = p_refs[0][...].astype(F32)
        for r in p_refs[1:]:
            g = g + r[...].astype(F32)
        nm = ADAM_B1 * m_ref[...] + (1.0 - ADAM_B1) * g
        nv = ADAM_B2 * v_ref[...] + (1.0 - ADAM_B2) * (g * g)
        m_hat = nm / (1.0 - ADAM_B1 ** ADAM_STEP)
        v_hat = nv / (1.0 - ADAM_B2 ** ADAM_STEP)
        g_ref[...] = g
        d_ref[...] = -ADAM_LR * (m_hat / (jnp.sqrt(v_hat) + ADAM_EPS) + ADAM_WD * w_ref[...])
        nm_ref[...] = nm
        nv_ref[...] = nv

    lspec = pl.BlockSpec((None, tr, cols), lambda i, s: (layer, i, 0))
    pspecs = [pl.BlockSpec((None, tr, cols), (lambda i, s: (s[0], i, 0)) if q is None else (lambda i, s, q=q: (q, i, 0)))
              for _, q in parts]
    aliases = {} if prev is None else {4 + n + q: q for q in range(4)}
    return pl.pallas_call(
        body, name=name, out_shape=[SDS(w.shape, F32)] * 4,
        grid_spec=pltpu.PrefetchScalarGridSpec(
            num_scalar_prefetch=1, grid=(rows // tr,), in_specs=[lspec] * 3 + pspecs + [ANY] * np_,
            out_specs=[lspec] * 4),
        input_output_aliases=aliases, compiler_params=_cp("parallel"),
    )(sel, w, m, v, *[arr for arr, _ in parts], *(prev or ()))


def _sum8(name, parts):
    rows = parts.shape[1]

    def body(p_ref, o_ref):
        s = p_ref[0]
        for q in range(1, N_DEV):
            s = s + p_ref[q]
        o_ref[...] = s

    return pl.pallas_call(
        body, name=name, grid=(1,), out_shape=SDS((rows, V7X_LANES), F32),
        in_specs=[pl.BlockSpec((N_DEV, rows, V7X_LANES), lambda i: (0, 0, 0))],
        out_specs=pl.BlockSpec((rows, V7X_LANES), lambda i: (0, 0)), compiler_params=_cp("arbitrary"),
    )(parts)


def _pack(vectors, align):
    flat = jnp.concatenate([v.reshape(-1) for v in vectors])
    pad = (-flat.shape[0]) % align
    if pad:
        flat = jnp.concatenate([flat, jnp.zeros((pad,), F32)])
    return flat.reshape(-1, V7X_LANES)


def _unpack(packed, shapes):
    flat = packed.reshape(-1)
    out, o = [], 0
    for s in shapes:
        size = 1
        for dim in s:
            size *= dim
        out.append(flat[o:o + size].reshape(s))
        o += size
    return out


def _relu2_epilogue(acc):
    r = jnp.maximum(acc, 0.0)
    return acc, r * r


def _residual_epilogue(acc, res):
    return (ALPHA * res + acc,)


def _plain_add_epilogue(acc, res):
    return (res + acc,)


def _gate_epilogue(acc, y, e):
    gate = _sigmoid(acc)
    return y + gate * e, gate


def _relu2_bwd_epilogue(acc, pre):
    return (acc * (2.0 * jnp.maximum(pre, 0.0)),)


def _tail_fwd(tag, u_a, wts, lng, lnb, p_l):
    y1 = _ln_fwd(f"ln1_{tag}", u_a, lng[0], lnb[0])
    pre, act = _mm_fwd(f"mlp1_{tag}", y1, wts["w1"], "col", [F32, BF16], _relu2_epilogue)
    (u_b,) = _mm_fwd(f"mlp2_{tag}", act, wts["w2"], "row", [F32], _residual_epilogue, (y1,))
    y2 = _ln_fwd(f"ln2_{tag}", u_b, lng[1], lnb[1])
    (e,) = _mm_fwd(f"ple_{tag}", p_l, wts["plew"], "col", [F32])
    xn, gate = _mm_fwd(f"gate_{tag}", y2, wts["gate"], "row", [F32, F32], _gate_epilogue, (y2, e))
    return xn, (u_a, y1, pre, act, u_b, y2, e, gate)


def _tail_bwd(tag, dxn, saved, wts, lng, p_l):
    u_a, y1, pre, act, u_b, y2, e, gate = saved
    dgpre, de = _ple_bwd(f"ple_bwd_{tag}", dxn, e, gate)
    (dy2,) = _mm_dx(f"gate_dx_{tag}", dgpre, wts["gate"], "row", [F32], _plain_add_epilogue, (dxn,))
    g_gate = _mm_dw(f"gate_dw_{tag}", y2, dgpre, "row")
    g_plew = _mm_dw(f"ple_dw_{tag}", p_l, de, "col")
    du_b, dg2, db2 = _ln_bwd(f"ln2_bwd_{tag}", u_b, dy2, lng[1])
    (dpre,) = _mm_dx(f"mlp2_dx_{tag}", du_b, wts["w2"], "row", [BF16], _relu2_bwd_epilogue, (pre,))
    g_w2 = _mm_dw(f"mlp2_dw_{tag}", act, du_b, "row")
    (dy1,) = _mm_dx(f"mlp1_dx_{tag}", dpre, wts["w1"], "col", [F32], _residual_epilogue, (du_b,))
    g_w1 = _mm_dw(f"mlp1_dw_{tag}", y1, dpre, "col")
    du_a, dg1, db1 = _ln_bwd(f"ln1_bwd_{tag}", u_a, dy1, lng[0])
    return du_a, dict(w1=g_w1, w2=g_w2, plew=g_plew, gate=g_gate), [dg1, dg2], [db1, db2]


def _to_slots(a, axis):
    shape = a.shape
    per = shape[axis] // N_DEV
    v = a.reshape(shape[:axis] + (2, 2, 2, per) + shape[axis + 1:])
    perm = (axis + 2, axis, axis + 1) + tuple(range(axis)) + tuple(range(axis + 3, v.ndim))
    v = v.transpose(perm)
    return v.reshape((N_DEV,) + shape[:axis] + (per,) + shape[axis + 1:])


def _pad_lanes(a):
    return jnp.pad(a, [(0, 0)] * (a.ndim - 1) + [(0, V7X_LANES - a.shape[-1])])


def kernel(x, p, pool_w, pool_scale, ssm_in_w, ssm_conv_w, ssm_conv_b, ssm_dt_bias, ssm_a_log, ssm_d, ssm_norm_w, ssm_out_w, mlp_w1, mlp_w2, ln_g, ln_b, ple_w, ple_gate_w, loss_target, m_pool_w, m_pool_scale, m_ssm_in_w, m_ssm_conv_w, m_ssm_conv_b, m_ssm_dt_bias, m_ssm_a_log, m_ssm_d, m_ssm_norm_w, m_ssm_out_w, m_mlp_w1, m_mlp_w2, m_ln_g, m_ln_b, m_ple_w, m_ple_gate_w, v_pool_w, v_pool_scale, v_ssm_in_w, v_ssm_conv_w, v_ssm_conv_b, v_ssm_dt_bias, v_ssm_a_log, v_ssm_d, v_ssm_norm_w, v_ssm_out_w, v_mlp_w1, v_mlp_w2, v_ln_g, v_ln_b, v_ple_w, v_ple_gate_w):
    t, d = x.shape[1:]
    h_n = ssm_dt_bias.shape[-1]
    di_s, cd_s, dp_s, d_s = ssm_norm_w.shape[-1], ssm_conv_b.shape[-1], ssm_in_w.shape[-1], ln_g.shape[-1]
    di, cd, dp = N_DEV * di_s, N_DEV * cd_s, N_DEV * dp_s
    p_dim = di // h_n
    g_n = (cd - di) // (2 * D_STATE)
    hpg = h_n // g_n
    dpp = di + cd + V7X_LANES
    assert h_n <= V7X_LANES and dp == di + cd + h_n and (di + cd) % V7X_LANES == 0
    dt_block = (di + cd) // V7X_LANES
    cg = d // 4
    me = 4 * lax.axis_index("x") + 2 * lax.axis_index("y") + lax.axis_index("c")

    x0, target = x[0], loss_target[0]
    p_l = [p[0, 0], p[1, 0]]

    small_shapes = [(CONV_WIDTH, cd_s), (1, cd_s), (1, di_s), (2, 2, d_s), (2, 2, d_s)]
    small = _pack([ssm_conv_w[0], ssm_conv_b, ssm_norm_w, ln_g, ln_b], 8 * V7X_LANES)
    first = [w.astype(BF16) for w in (pool_w[0], mlp_w1[0], mlp_w2[0], ple_w[0], ple_gate_w[0])]
    own_ssm = [w.astype(BF16) for w in (ssm_in_w[0], ssm_out_w[0])]
    own_mlp = [w.astype(BF16) for w in (mlp_w1[1], mlp_w2[1], ple_w[1], ple_gate_w[1])]
    pool_g, w1_0, w2_0, plew_0, gate_0, small_g = _all_gather("ag_layer0", first + [small])

    def gather_start(tag, own, after):
        lands = [lax.empty((N_DEV,) + w.shape, w.dtype) for w in own]
        return _async_start(f"ag_{tag}_start", _ag_first_copies, (4 * len(own),), own, lands, after)

    def gather_finish(tag, handle, after):
        n = len(handle[2]) // 2
        own, lands = _async_wait(f"ag_{tag}_wait", _ag_first_copies, handle, n, after)
        fwd = _async_start(f"ag_{tag}_forward_start", _ag_forward_copies, (3 * n,), [], lands)
        _, lands = _async_wait(f"ag_{tag}_forward_wait", _ag_forward_copies, fwd, 0)
        return [lax.dynamic_update_slice_in_dim(g, w[None], me, 0) for g, w in zip(lands, own)]

    ag_ssm = gather_start("ssm", own_ssm, (pool_g,))
    ag_mlp = gather_start("mlp1", own_mlp, (ag_ssm[3],))
    pool_scale_fwd = _token_add(pool_scale, ag_ssm[3], ag_mlp[3])
    pool_full = pool_g.transpose(1, 0, 2, 3).reshape(4, cg, cg)
    sm = small_g.reshape(N_DEV, -1)
    o = 0
    parts = []
    for shp in small_shapes:
        size = 1
        for s in shp:
            size *= s
        parts.append(sm[:, o:o + size].reshape((N_DEV,) + shp))
        o += size
    conv_w_full = parts[0].transpose(1, 0, 2).reshape(CONV_WIDTH, cd)
    conv_b_full = parts[1].transpose(1, 0, 2).reshape(1, cd)
    norm_w_full = parts[2].transpose(1, 0, 2).reshape(1, di)
    ln_g_full = parts[3].transpose(1, 2, 0, 3).reshape(2, 2, 1, d)
    ln_b_full = parts[4].transpose(1, 2, 0, 3).reshape(2, 2, 1, d)
    bias128, alog128 = _pad_lanes(ssm_dt_bias), _pad_lanes(ssm_a_log)

    pooled = _pool_windows("pool_fwd", x0, False)
    u0, hraw = _pool_mm("pool_mm", pooled, pool_full, pool_scale_fwd, x0)
    wts = [dict(w1=w1_0, w2=w2_0, plew=plew_0, gate=gate_0)]
    x1, saved0 = _tail_fwd("l0", u0, wts[0], ln_g_full[0], ln_b_full[0], p_l[0])

    in_g, out_g = gather_finish("ssm", ag_ssm, (x1,))
    in_full = jnp.pad(in_g.transpose(1, 0, 2).reshape(d, dp), ((0, 0), (0, dpp - dp)))
    (zx,) = _mm_fwd("in_proj", x1, in_full, "plain", [F32])
    xbc = _conv_fwd("conv_fwd", zx, conv_w_full, conv_b_full, di)
    dt, acs, e_in, e_out = _dt_fwd("dt_fwd", zx, bias128, alog128, dt_block)

    def per_lane(a):
        return jnp.repeat(a[:, :h_n], p_dim, axis=1)

    dt_x, ein_x, eout_x, d_x = per_lane(dt), per_lane(e_in), per_lane(e_out), per_lane(ssm_d)

    def to_col(a):
        return a[:, :h_n].reshape(t, g_n, hpg).transpose(1, 0, 2)

    def to_row(a):
        return a[:, :h_n].reshape(t, g_n, hpg).transpose(1, 2, 0)

    def from_col(a):
        return _pad_lanes(a.transpose(1, 0, 2).reshape(t, h_n))

    a_col, a_row = to_col(acs), to_row(acs)
    y_ssd, prev = _ssd_fwd("ssd_fwd", xbc, dt_x, ein_x, eout_x, a_col, a_row, d_x, di, g_n, hpg, p_dim)
    yn = _gated_rms_fwd("gated_rms_fwd", y_ssd, zx, norm_w_full)
    (u2,) = _mm_fwd("out_proj", yn, out_g, "row", [F32], _residual_epilogue, (x1,))
    w1_1, w2_1, plew_1, gate_1 = gather_finish("mlp1", ag_mlp, (u2,))
    wts.append(dict(w1=w1_1, w2=w2_1, plew=plew_1, gate=gate_1))
    x2, saved1 = _tail_fwd("l1", u2, wts[1], ln_g_full[1], ln_b_full[1], p_l[1])

    core = lax.axis_index("c").astype(jnp.int32).reshape(1)
    chip = (2 * lax.axis_index("x") + lax.axis_index("y")).astype(jnp.int32).reshape(1)

    def sibling_sums(tag, grads):
        halves = _rs_sibling_exchange(f"rs_{tag}_sibling", grads)
        return [_pair_sum(f"rs_{tag}_pair_sum_{a}", g, hv, core) for a, (g, hv) in enumerate(zip(grads, halves))]

    def scatter_start(tag, grads):
        sums = sibling_sums(tag, grads)
        lands = [lax.empty((3,) + s.shape[1:], s.dtype) for s in sums]
        return _async_start(f"rs_{tag}_start", _rs_chip_copies, (3 * len(sums),), sums, lands)

    def as_parts(sums, thirds):
        return [(s.reshape(4, -1, s.shape[-1]), r.reshape(3, -1, r.shape[-1])) for s, r in zip(sums, thirds)]

    dx2, loss_cols = _loss_bwd("loss", x2, target)
    du2, gw1, dg_1, db_1 = _tail_bwd("l1", dx2, saved1, wts[1], ln_g_full[1], p_l[1])
    rs_mlp = scatter_start("mlp1", [gw1["w1"], gw1["w2"], gw1["plew"], gw1["gate"]])
    (dyn,) = _mm_dx("out_proj_dx", du2, out_g, "row", [F32])
    g_out = _mm_dw("out_proj_dw", yn, du2, "row")
    dy_ssd, dz, d_norm_w = _gated_rms_bwd("gated_rms_bwd", y_ssd, zx, _token_add(norm_w_full, rs_mlp[3]), dyn)
    dxs, dbm, dcm, ddt_x, dacs, dd = _ssd_bwd("ssd_bwd", xbc, dt_x, ein_x, eout_x, a_col, a_row, d_x, prev, dy_ssd,
                                              di, g_n, hpg, p_dim)
    draw, d_bias, d_alog = _dt_bwd("dt_bwd", zx, bias128, alog128, from_col(dacs), from_col(ddt_x), dt_block)
    dxbc, d_conv_w, d_conv_b = _conv_bwd("conv_bwd", zx, conv_w_full, conv_b_full,
                                         jnp.concatenate([dxs, dbm, dcm], axis=1), di)
    dzx = jnp.concatenate([dz, dxbc, draw], axis=1)
    (dx1,) = _mm_dx("in_proj_dx", dzx, in_full, "plain", [F32], _residual_epilogue, (du2,))
    g_in = _to_slots(_mm_dw("in_proj_dw", x1, dzx, "plain")[:, :dp], 1)
    rs_ssm = scatter_start("ssm", [g_in, g_out])

    du0, gw0, dg_0, db_0 = _tail_bwd("l0", dx1, saved0, wts[0], _token_add(ln_g_full[0], rs_ssm[3]), p_l[0])
    dh, dpool, d_scale = _pool_bwd_mm("pool_bwd_mm", du0, hraw, pool_full, pool_scale)
    grad_x = _pool_windows("pool_bwd", dpool, True, du0)
    g_pool = _to_slots(_pool_dw("pool_dw", pooled, dh), 1)

    rs_l0 = scatter_start("layer0", [g_pool, gw0["w1"], gw0["w2"], gw0["plew"], gw0["gate"]])

    d_ln_g = jnp.stack([jnp.stack(dg_0), jnp.stack(dg_1)]).reshape(2, 2, d)
    d_ln_b = jnp.stack([jnp.stack(db_0), jnp.stack(db_1)]).reshape(2, 2, d)
    partial_shapes = [(CONV_WIDTH, cd), (1, cd), (1, di), (2, 2, d), (2, 2, d), (1, d), (1, h_n), (1, h_n), (1, h_n),
                      (1, d)]
    partial = _pack([d_conv_w, d_conv_b, d_norm_w, d_ln_g, d_ln_b, d_scale, d_bias[:, :h_n], d_alog[:, :h_n],
                     dd.reshape(1, h_n), loss_cols], 8 * V7X_LANES)
    (all_partials,) = _all_gather("ag_small_grads", [_token_add(partial, rs_l0[3])])
    tot = _unpack(_sum8("sum_small_grads", all_partials), partial_shapes)
    t_conv_w, t_conv_b, t_norm_w, t_ln_g, t_ln_b, t_scale, t_bias, t_alog, t_dd, t_loss = tot
    loss = jnp.sum(t_loss)

    def mine(a, per):
        return lax.dynamic_slice_in_dim(a, me * per, per, axis=a.ndim - 1)

    small_names = ["ssm_conv_w", "ssm_conv_b", "ssm_norm_w", "ln_g", "ln_b", "pool_scale", "ssm_dt_bias", "ssm_a_log",
                   "ssm_d"]
    small_w = [ssm_conv_w, ssm_conv_b, ssm_norm_w, ln_g, ln_b, pool_scale, ssm_dt_bias, ssm_a_log, ssm_d]
    small_m = [m_ssm_conv_w, m_ssm_conv_b, m_ssm_norm_w, m_ln_g, m_ln_b, m_pool_scale, m_ssm_dt_bias, m_ssm_a_log,
               m_ssm_d]
    small_v = [v_ssm_conv_w, v_ssm_conv_b, v_ssm_norm_w, v_ln_g, v_ln_b, v_pool_scale, v_ssm_dt_bias, v_ssm_a_log,
               v_ssm_d]
    small_grads = [mine(t_conv_w, cd_s), mine(t_conv_b, cd_s), mine(t_norm_w, di_s), mine(t_ln_g, d_s),
                   mine(t_ln_b, d_s), t_scale, t_bias, t_alog, t_dd]
    shapes = [w.shape for w in small_w]
    pk = [_pack(group, 8 * V7X_LANES)[None] for group in (small_w, small_m, small_v, small_grads)]
    res = _adamw("adamw_small", pk[0], pk[1], pk[2], [(pk[3], 0)], 0)
    upd = {}
    for name, vals in zip(small_names, zip(*[_unpack(r, shapes) for r in res])):
        upd[name] = list(vals)

    def update(tag, w, m, v, parts, layer, prev=None):
        own, recv = parts
        return _adamw(f"adamw_{tag}_{layer}", _as3d(w), _as3d(m), _as3d(v),
                      [(own, None), (recv, 0), (recv, 1), (recv, 2)], layer, prev, chip)

    q_w1_1, q_w2_1, q_plew_1, q_gate_1 = as_parts(*_async_wait("rs_mlp1_wait", _rs_chip_copies, rs_mlp, 4, (res[0],)))
    q_in, q_out = as_parts(*_async_wait("rs_ssm_wait", _rs_chip_copies, rs_ssm, 2, (res[0],)))
    r_in = update("ssm_in_w", ssm_in_w, m_ssm_in_w, v_ssm_in_w, q_in, 0)
    r_out = update("ssm_out_w", ssm_out_w, m_ssm_out_w, v_ssm_out_w, q_out, 0)
    r_w1 = update("mlp_w1", mlp_w1, m_mlp_w1, v_mlp_w1, q_w1_1, 1)
    r_w2 = update("mlp_w2", mlp_w2, m_mlp_w2, v_mlp_w2, q_w2_1, 1)
    r_plew = update("ple_w", ple_w, m_ple_w, v_ple_w, q_plew_1, 1)
    r_gate = update("ple_gate_w", ple_gate_w, m_ple_gate_w, v_ple_gate_w, q_gate_1, 1)
    q_pool, q_w1_0, q_w2_0, q_plew_0, q_gate_0 = as_parts(
        *_async_wait("rs_layer0_wait", _rs_chip_copies, rs_l0, 5, (r_gate[0], r_w1[0], r_w2[0], r_in[0])))
    large = {
        "pool_w": (pool_w, update("pool_w", pool_w, m_pool_w, v_pool_w, q_pool, 0)),
        "ssm_in_w": (ssm_in_w, r_in),
        "ssm_out_w": (ssm_out_w, r_out),
        "mlp_w1": (mlp_w1, update("mlp_w1", mlp_w1, m_mlp_w1, v_mlp_w1, q_w1_0, 0, r_w1)),
        "mlp_w2": (mlp_w2, update("mlp_w2", mlp_w2, m_mlp_w2, v_mlp_w2, q_w2_0, 0, r_w2)),
        "ple_w": (ple_w, update("ple_w", ple_w, m_ple_w, v_ple_w, q_plew_0, 0, r_plew)),
        "ple_gate_w": (ple_gate_w, update("ple_gate_w", ple_gate_w, m_ple_gate_w, v_ple_gate_w, q_gate_0, 0, r_gate)),
    }
    for name, (w, rs) in large.items():
        upd[name] = [r.reshape(w.shape) for r in rs]

    order = ["pool_w", "pool_scale", "ssm_in_w", "ssm_conv_w", "ssm_conv_b", "ssm_dt_bias", "ssm_a_log", "ssm_d",
             "ssm_norm_w", "ssm_out_w", "mlp_w1", "mlp_w2", "ln_g", "ln_b", "ple_w", "ple_gate_w"]
    out = [loss, grad_x[None]]
    for k in range(4):
        out += [upd[name][k] for name in order]
    return tuple(out)
```

```python
import jax
import jax.numpy as jnp
from jax import lax
from jax.experimental import pallas as pl
from jax.experimental.pallas import tpu as pltpu

F32 = jnp.float32
BF16 = jnp.bfloat16
SDS = jax.ShapeDtypeStruct
MESH = pl.DeviceIdType.MESH
ANY = pl.BlockSpec(memory_space=pl.ANY)

N_DEV = 8
DEPTH = 2
ALPHA = (2.0 * DEPTH) ** 0.25
LN_EPS = 1e-5
RMS_EPS = 1e-5
POOL_WINDOW_LOG2 = (1, 2, 3, 4)
D_STATE = 128
CHUNK = 128
CONV_WIDTH = 4
ADAM_LR = 0.001
ADAM_B1 = 0.9
ADAM_B2 = 0.999
ADAM_EPS = 1e-08
ADAM_WD = 0.01
ADAM_STEP = 10

V7X_LANES = 128
V7X_VMEM_LIMIT = 48 * 1024 * 1024


def _cp(*sem):
    return pltpu.CompilerParams(dimension_semantics=sem, vmem_limit_bytes=V7X_VMEM_LIMIT)


def _pick(dim, cap):
    if dim <= cap:
        return dim
    best = None
    for t in range(V7X_LANES, cap + 1, V7X_LANES):
        if dim % t == 0:
            best = t
    assert best is not None, (dim, cap)
    return best


def _row_tile(rows, cols, itemsize=4, target=1 << 20):
    t = rows
    while t % 2 == 0 and t // 2 >= 16 and (t // 2) % 16 == 0 and t * cols * itemsize > target:
        t //= 2
    return t


def _slot(s):
    return (s % 2) * 4 + s // 2


def _all_gather(name, shards):
    n = len(shards)

    def body(*refs):
        ins, outs = refs[:n], refs[n:2 * n]
        send_sems, recv_sems, local_sems = refs[2 * n:]
        x, y, c = lax.axis_index("x"), lax.axis_index("y"), lax.axis_index("c")
        me, sibling = (x, y, c), (x, y, 1 - c)
        chips = [(1 - x, y), (x, 1 - y), (1 - x, 1 - y)]

        def copy(a, k, block, to, src=None):
            dst = outs[a].at[4 * block[0] + 2 * block[1] + block[2]]
            return pltpu.make_async_remote_copy(
                src_ref=dst if src is None else src, dst_ref=dst, send_sem=send_sems.at[a, k],
                recv_sem=recv_sems.at[a, k], device_id=to, device_id_type=MESH)

        mine = [pltpu.make_async_copy(ins[a], outs[a].at[4 * x + 2 * y + c], local_sems.at[a]) for a in range(n)]
        for cp in mine:
            cp.start()
        first = []
        for a in range(n):
            first.append(copy(a, 0, me, sibling, src=ins[a]))
            first += [copy(a, 1 + j, me, (*chip, c), src=ins[a]) for j, chip in enumerate(chips)]
        for cp in first:
            cp.start()
        passed = []
        for j, chip in enumerate(chips):
            for a in range(n):
                copy(a, 1 + j, (*chip, c), me).wait_recv()
                fwd = copy(a, 4 + j, (*chip, c), sibling)
                fwd.start()
                passed.append(fwd)
        for a in range(n):
            copy(a, 0, sibling, me).wait_recv()
            for j, chip in enumerate(chips):
                copy(a, 4 + j, (*chip, 1 - c), me).wait_recv()
        for cp in first + passed:
            cp.wait_send()
        for cp in mine:
            cp.wait()

    return pl.pallas_call(
        body, name=name,
        out_shape=[SDS((N_DEV,) + s.shape, s.dtype) for s in shards],
        in_specs=[ANY] * n, out_specs=[ANY] * n,
        scratch_shapes=[pltpu.SemaphoreType.DMA((n, 7)), pltpu.SemaphoreType.DMA((n, 7)),
                        pltpu.SemaphoreType.DMA((n,))],
    )(*shards)


def _rs_sibling_exchange(name, grads):
    n = len(grads)

    def body(*refs):
        ins, outs = refs[:n], refs[n:2 * n]
        send_sems, recv_sems = refs[2 * n:]
        x, y, c = lax.axis_index("x"), lax.axis_index("y"), lax.axis_index("c")
        remote = [pltpu.make_async_remote_copy(
            src_ref=ins[a].at[pl.ds(4 * (1 - c), 4)], dst_ref=outs[a], send_sem=send_sems.at[a],
            recv_sem=recv_sems.at[a], device_id=(x, y, 1 - c), device_id_type=MESH) for a in range(n)]
        for cp in remote:
            cp.start()
        for cp in remote:
            cp.wait()

    return pl.pallas_call(
        body, name=name,
        out_shape=[SDS((4,) + g.shape[1:], g.dtype) for g in grads],
        in_specs=[ANY] * n, out_specs=[ANY] * n,
        scratch_shapes=[pltpu.SemaphoreType.DMA((n,)), pltpu.SemaphoreType.DMA((n,))],
    )(*grads)


HBM_SPEC = pl.BlockSpec(memory_space=pltpu.HBM)
SEM_SPEC = pl.BlockSpec(memory_space=pltpu.SEMAPHORE)
EFFECT = pltpu.SideEffectType.DATAFLOW_SIDE_EFFECTING


def _ag_first_copies(ins, lands, send_sems, recv_sems):
    x, y, c = lax.axis_index("x"), lax.axis_index("y"), lax.axis_index("c")
    targets = [(x, y, 1 - c), (1 - x, y, c), (x, 1 - y, c), (1 - x, 1 - y, c)]
    return [pltpu.make_async_remote_copy(
        src_ref=ins[a], dst_ref=lands[a].at[4 * x + 2 * y + c], send_sem=send_sems.at[4 * a + k],
        recv_sem=recv_sems.at[4 * a + k], device_id=to, device_id_type=MESH)
        for a in range(len(ins)) for k, to in enumerate(targets)]


def _ag_forward_copies(ins, lands, send_sems, recv_sems):
    x, y, c = lax.axis_index("x"), lax.axis_index("y"), lax.axis_index("c")
    cps = []
    for a in range(len(lands)):
        for j, (px, py) in enumerate([(1 - x, y), (x, 1 - y), (1 - x, 1 - y)]):
            blk = lands[a].at[4 * px + 2 * py + c]
            cps.append(pltpu.make_async_remote_copy(
                src_ref=blk, dst_ref=blk, send_sem=send_sems.at[3 * a + j], recv_sem=recv_sems.at[3 * a + j],
                device_id=(x, y, 1 - c), device_id_type=MESH))
    return cps


def _rs_chip_copies(ins, lands, send_sems, recv_sems):
    x, y, c = lax.axis_index("x"), lax.axis_index("y"), lax.axis_index("c")
    cps = []
    for a in range(len(ins)):
        for j, (px, py) in enumerate([(1 - x, y), (x, 1 - y), (1 - x, 1 - y)]):
            cps.append(pltpu.make_async_remote_copy(
                src_ref=ins[a].at[2 * px + py], dst_ref=lands[a].at[j], send_sem=send_sems.at[3 * a + j],
                recv_sem=recv_sems.at[3 * a + j], device_id=(px, py, c), device_id_type=MESH))
    return cps


def _async_start(name, build, sem_shape, ins, lands, after=()):
    arrays = [*ins, *lands]
    n_i, n_t, n_a = len(ins), len(arrays), len(after)

    def body(*refs):
        outs = refs[n_t + n_a:]
        for cp in build(refs[:n_i], refs[n_i:n_t], outs[0], outs[1]):
            cp.start()
        outs[-1][...] = jnp.zeros_like(outs[-1])

    res = pl.pallas_call(
        body, name=name,
        out_shape=(pltpu.SemaphoreType.DMA(sem_shape), pltpu.SemaphoreType.DMA(sem_shape),
                   *[pltpu.HBM(a.shape, a.dtype) for a in arrays], SDS((8, V7X_LANES), F32)),
        in_specs=[HBM_SPEC] * n_t + [ANY] * n_a,
        out_specs=(SEM_SPEC, SEM_SPEC, *[HBM_SPEC] * n_t, pl.BlockSpec(memory_space=pltpu.VMEM)),
        input_output_aliases={i: 2 + i for i in range(n_t)},
        compiler_params=pltpu.CompilerParams(has_side_effects=EFFECT),
    )(*[pltpu.with_memory_space_constraint(a, pltpu.HBM) for a in arrays], *after)
    return res[0], res[1], list(res[2:2 + n_t]), res[-1]


def _async_wait(name, build, handle, n_i, after=()):
    send_sems, recv_sems, arrays, _ = handle
    n_t, n_a = len(arrays), len(after)

    def body(*refs):
        for cp in build(refs[:n_i], refs[n_i:n_t], refs[n_t], refs[n_t + 1]):
            cp.wait_send()
            cp.wait_recv()

    res = pl.pallas_call(
        body, name=name, out_shape=tuple(pltpu.HBM(a.shape, a.dtype) for a in arrays),
        in_specs=[HBM_SPEC] * n_t + [SEM_SPEC, SEM_SPEC] + [ANY] * n_a, out_specs=tuple([HBM_SPEC] * n_t),
        input_output_aliases={i: i for i in range(n_t)},
        compiler_params=pltpu.CompilerParams(has_side_effects=EFFECT),
    )(*arrays, send_sems, recv_sems, *after)
    return list(res[:n_i]), list(res[n_i:])


def _token_add(a, *tokens):
    for tok in tokens:
        a = a + tok[0:1, 0:1].reshape((1,) * a.ndim)
    return a


def _mm_core(name, a, b, *, grid, a_spec, b_spec, dims, acc_shape, outs, out_spec, epilogue=None, extras=(),
             extra_specs=()):
    nk = grid[2]
    ne, no = len(extras), len(outs)

    def body(a_ref, b_ref, *rest):
        e_refs, o_refs, acc = rest[:ne], rest[ne:ne + no], rest[ne + no]
        k = pl.program_id(2)

        @pl.when(k == 0)
        def _():
            acc[...] = jnp.zeros_like(acc)

        acc[...] += lax.dot_general(a_ref[...].astype(BF16), b_ref[...].astype(BF16), dims,
                                    preferred_element_type=F32)

        @pl.when(k == nk - 1)
        def _():
            r = acc[...]
            vals = epilogue(r, *[e[...] for e in e_refs]) if epilogue is not None else (r,)
            for o, v in zip(o_refs, vals):
                o[...] = v.astype(o.dtype)

    res = pl.pallas_call(
        body, name=name, grid=grid, out_shape=list(outs),
        in_specs=[a_spec, b_spec, *extra_specs], out_specs=[out_spec] * no,
        scratch_shapes=[pltpu.VMEM(acc_shape, F32)],
        compiler_params=_cp("parallel", "parallel", "arbitrary"),
    )(a, b, *extras)
    return res


NN = (((1,), (0,)), ((), ()))
NT = (((1,), (1,)), ((), ()))
TN = (((0,), (0,)), ((), ()))


def _w_dims(w, kind):
    if kind == "col":
        return w.shape[1], N_DEV * w.shape[2], w.shape[1], w.shape[2]
    if kind == "row":
        return N_DEV * w.shape[1], w.shape[2], w.shape[1], w.shape[2]
    return w.shape[0], w.shape[1], w.shape[0], w.shape[1]


MM_VMEM_BUDGET = 36 * 1024 * 1024


def _row_block(m, tn, tk, a, out_dtypes, extras):
    for tm in (_pick(m, 2048), _pick(m, 1024), _pick(m, 512)):
        per_out = sum(jnp.dtype(dt).itemsize for dt in out_dtypes) + sum(e.dtype.itemsize for e in extras)
        used = tm * tn * (4 + 2 * per_out) + 2 * (tm * tk * a.dtype.itemsize + tk * tn * 2)
        if used <= MM_VMEM_BUDGET:
            return tm
    return tm


def _mm_fwd(name, a, w, kind, out_dtypes, epilogue=None, extras=()):
    if kind == "row":
        w, kind = w.reshape(-1, w.shape[-1]), "plain"
    m, k_dim = a.shape
    kk, n, ks, ns = _w_dims(w, kind)
    assert kk == k_dim
    if kind == "col":
        tn, tk = _pick(ns, 1024), _pick(kk, 512)
        nb = ns // tn
        b_spec = pl.BlockSpec((None, tk, tn), lambda i, j, k: (j // nb, k, j % nb))
    else:
        tn, tk = _pick(n, 1152), _pick(kk, 512)
        b_spec = pl.BlockSpec((tk, tn), lambda i, j, k: (k, j))
    tm = _row_block(m, tn, tk, a, out_dtypes, extras)
    mn_spec = pl.BlockSpec((tm, tn), lambda i, j, k: (i, j))
    return _mm_core(
        name, a, w, grid=(m // tm, n // tn, kk // tk),
        a_spec=pl.BlockSpec((tm, tk), lambda i, j, k: (i, k)), b_spec=b_spec, dims=NN, acc_shape=(tm, tn),
        outs=[SDS((m, n), dt) for dt in out_dtypes], out_spec=mn_spec, epilogue=epilogue, extras=extras,
        extra_specs=[mn_spec] * len(extras))


def _mm_dx(name, dy, w, kind, out_dtypes, epilogue=None, extras=()):
    if kind == "row":
        w, kind = w.reshape(-1, w.shape[-1]), "plain"
    m, n_dim = dy.shape
    kk, n, ks, ns = _w_dims(w, kind)
    assert n == n_dim
    if kind == "col":
        tn, tk = _pick(kk, 1024), _pick(ns, 512)
        kb = ns // tk
        b_spec = pl.BlockSpec((None, tn, tk), lambda i, j, k: (k // kb, j, k % kb))
    else:
        tn, tk = _pick(kk, 1024), _pick(n, 1152)
        b_spec = pl.BlockSpec((tn, tk), lambda i, j, k: (j, k))
    tm = _row_block(m, tn, tk, dy, out_dtypes, extras)
    mk_spec = pl.BlockSpec((tm, tn), lambda i, j, k: (i, j))
    return _mm_core(
        name, dy, w, grid=(m // tm, kk // tn, n // tk),
        a_spec=pl.BlockSpec((tm, tk), lambda i, j, k: (i, k)), b_spec=b_spec, dims=NT, acc_shape=(tm, tn),
        outs=[SDS((m, kk), dt) for dt in out_dtypes], out_spec=mk_spec, epilogue=epilogue, extras=extras,
        extra_specs=[mk_spec] * len(extras))


def _mm_dw(name, a, dy, kind):
    m, kk = a.shape
    n = dy.shape[1]
    tk = _pick(m, 512)
    if kind == "col":
        ns = n // N_DEV
        tm, tn = _pick(kk, 1024), _pick(ns, 1024)
        nb = ns // tn
        out = SDS((N_DEV, kk, ns), BF16)
        out_spec = pl.BlockSpec((None, tm, tn), lambda i, j, k: (_slot(j // nb), i, j % nb))
    elif kind == "row":
        ks = kk // N_DEV
        tm, tn = _pick(ks, 1024), _pick(n, 1024)
        mb = ks // tm
        out = SDS((N_DEV, ks, n), BF16)
        out_spec = pl.BlockSpec((None, tm, tn), lambda i, j, k: (_slot(i // mb), i % mb, j))
    else:
        tm, tn = _pick(kk, 1024), _pick(n, 1152)
        out = SDS((kk, n), BF16)
        out_spec = pl.BlockSpec((tm, tn), lambda i, j, k: (i, j))
    return _mm_core(
        name, a, dy, grid=(kk // tm, n // tn, m // tk),
        a_spec=pl.BlockSpec((tk, tm), lambda i, j, k: (k, i)), b_spec=pl.BlockSpec((tk, tn), lambda i, j, k: (k, j)),
        dims=TN, acc_shape=(tm, tn), outs=[out], out_spec=out_spec)[0]


def _rowwise(name, fn, ins, outs, rows, tile):
    arrays, specs = [], []
    for arr, kind in ins:
        arrays.append(arr)
        if kind == "row":
            specs.append(pl.BlockSpec((tile, arr.shape[1]), lambda i: (i, 0)))
        elif kind == "vec":
            specs.append(pl.BlockSpec(arr.shape, lambda i, nd=arr.ndim: (0,) * nd))
        else:
            specs.append(kind)
    out_shapes, out_specs, kinds = [], [], []
    for cols, dt, kind in outs:
        kinds.append(kind)
        if kind == "row":
            out_shapes.append(SDS((rows, cols), dt))
            out_specs.append(pl.BlockSpec((tile, cols), lambda i: (i, 0)))
        else:
            out_shapes.append(SDS((1, cols), F32))
            out_specs.append(pl.BlockSpec((1, cols), lambda i: (0, 0)))
    ni = len(arrays)
    has_acc = "acc" in kinds

    def body(*refs):
        vals = fn(*[r[...] for r in refs[:ni]])
        i = pl.program_id(0)
        for o, v, kind in zip(refs[ni:], vals, kinds):
            if kind == "row":
                o[...] = v.astype(o.dtype)
            else:
                @pl.when(i == 0)
                def _(o=o):
                    o[...] = jnp.zeros_like(o)

                o[...] += v

    return pl.pallas_call(
        body, name=name, grid=(rows // tile,), out_shape=out_shapes, in_specs=specs, out_specs=out_specs,
        compiler_params=_cp("arbitrary" if has_acc else "parallel"),
    )(*arrays)


def _ln_fwd(name, u, g, b):
    d = u.shape[1]

    def fn(u, g, b):
        mu = jnp.mean(u, axis=1, keepdims=True)
        xc = u - mu
        var = jnp.mean(xc * xc, axis=1, keepdims=True)
        y = xc * lax.rsqrt(var + LN_EPS) * g + b
        return y, y

    return _rowwise(name, fn, [(u, "row"), (g, "vec"), (b, "vec")], [(d, F32, "row"), (d, BF16, "row")], u.shape[0], 256)


def _ln_bwd(name, u, dy, g):
    d = u.shape[1]

    def fn(u, dy, g):
        mu = jnp.mean(u, axis=1, keepdims=True)
        xc = u - mu
        var = jnp.mean(xc * xc, axis=1, keepdims=True)
        rstd = lax.rsqrt(var + LN_EPS)
        xhat = xc * rstd
        dxhat = dy * g
        m1 = jnp.mean(dxhat, axis=1, keepdims=True)
        m2 = jnp.mean(dxhat * xhat, axis=1, keepdims=True)
        du = rstd * (dxhat - m1 - xhat * m2)
        return du, du, jnp.sum(dy * xhat, axis=0, keepdims=True), jnp.sum(dy, axis=0, keepdims=True)

    return _rowwise(name, fn, [(u, "row"), (dy, "row"), (g, "vec")],
                    [(d, F32, "row"), (d, BF16, "row"), (d, F32, "acc"), (d, F32, "acc")], u.shape[0], 256)


def _loss_bwd(name, y, target):
    d = y.shape[1]

    def fn(y, t):
        e = y - t
        return e * (1.0 / d), jnp.sum(e * e, axis=0, keepdims=True) * (0.5 / d)

    return _rowwise(name, fn, [(y, "row"), (target, "row")], [(d, F32, "row"), (d, F32, "acc")], y.shape[0], 256)


def _ple_bwd(name, dx, e, gate):
    d = dx.shape[1]

    def fn(dx, e, gate):
        return dx * e * gate * (1.0 - gate), dx * gate

    return _rowwise(name, fn, [(dx, "row"), (e, "row"), (gate, "row")], [(d, BF16, "row"), (d, BF16, "row")],
                    dx.shape[0], 256)


def _sigmoid(v):
    return 1.0 / (1.0 + jnp.exp(-v))


def _gated_rms_fwd(name, y, zx, norm_w):
    di = y.shape[1]

    def fn(y, z, w):
        yg = y * (z * _sigmoid(z))
        r = lax.rsqrt(jnp.mean(yg * yg, axis=1, keepdims=True) + RMS_EPS)
        return (yg * r * w,)

    z_spec = pl.BlockSpec((128, di), lambda i: (i, 0))
    return _rowwise(name, fn, [(y, "row"), (zx, z_spec), (norm_w, "vec")], [(di, BF16, "row")], y.shape[0], 128)[0]


def _gated_rms_bwd(name, y, zx, norm_w, dout):
    di = y.shape[1]

    def fn(y, z, w, dout):
        sg = _sigmoid(z)
        sz = z * sg
        yg = y * sz
        r = lax.rsqrt(jnp.mean(yg * yg, axis=1, keepdims=True) + RMS_EPS)
        dn = dout * w
        dyg = r * (dn - yg * (r * r) * jnp.mean(dn * yg, axis=1, keepdims=True))
        dy = dyg * sz
        dz = dyg * y * (sg * (1.0 + z * (1.0 - sg)))
        return dy, dz, jnp.sum(dout * yg * r, axis=0, keepdims=True)

    z_spec = pl.BlockSpec((128, di), lambda i: (i, 0))
    return _rowwise(name, fn, [(y, "row"), (zx, z_spec), (norm_w, "vec"), (dout, "row")],
                    [(di, F32, "row"), (di, BF16, "row"), (di, F32, "acc")], y.shape[0], 128)


def _shift_down(v, j, row):
    return jnp.where(row >= j, pltpu.roll(v, j, 0), 0.0)


def _shift_up(v, j, row):
    t = v.shape[0]
    return jnp.where(row < t - j, pltpu.roll(v, t - j, 0), 0.0)


def _pool_select(parts, g):
    return jnp.where(g == 0, parts[0], jnp.where(g == 1, parts[1], jnp.where(g == 2, parts[2], parts[3])))


def _pool_windows(name, x, transpose, scale_by=None):
    t, d = x.shape
    cg = d // 4
    cw = V7X_LANES
    per = cg // cw

    def body(*refs):
        x_ref, o_ref = refs[0], refs[-1]
        g = pl.program_id(0) // per
        xv = x_ref[...]
        row = lax.broadcasted_iota(jnp.int32, (t, 1), 0)
        cnt = jnp.minimum(row + 1, jnp.left_shift(2, g)).astype(F32)
        s = xv / cnt if transpose else xv
        parts = []
        for lg in POOL_WINDOW_LOG2:
            j = 1 << (lg - 1)
            s = s + (_shift_up(s, j, row) if transpose else _shift_down(s, j, row))
            parts.append(s)
        sel = _pool_select(parts, g)
        if transpose:
            o_ref[...] = ALPHA * refs[1][...] + sel - xv
        else:
            o_ref[...] = (sel / cnt - xv).astype(o_ref.dtype)

    col = pl.BlockSpec((t, cw), lambda j: (0, j))
    ins = [x] if scale_by is None else [x, scale_by]
    return pl.pallas_call(
        body, name=name, grid=(d // cw,), out_shape=SDS((t, d), F32 if transpose else BF16),
        in_specs=[col] * len(ins), out_specs=col, compiler_params=_cp("parallel"),
    )(*ins)


def _pool_mm(name, pooled, w, scale, x):
    t, d = x.shape
    cg = d // 4
    tm = _pick(t, 1024)

    def body(p_ref, w_ref, s_ref, x_ref, u_ref, h_ref):
        h = jnp.dot(p_ref[...], w_ref[...], preferred_element_type=F32)
        h_ref[...] = h
        u_ref[...] = ALPHA * x_ref[...] + h * s_ref[...]

    blk = pl.BlockSpec((tm, cg), lambda g, i: (i, g))
    return pl.pallas_call(
        body, name=name, grid=(4, t // tm), out_shape=[SDS((t, d), F32), SDS((t, d), F32)],
        in_specs=[blk, pl.BlockSpec((None, cg, cg), lambda g, i: (g, 0, 0)), pl.BlockSpec((1, cg), lambda g, i: (0, g)),
                  blk],
        out_specs=[blk, blk], compiler_params=_cp("parallel", "parallel"),
    )(pooled, w, scale, x)


def _pool_bwd_mm(name, du, hraw, w, scale):
    t, d = du.shape
    cg = d // 4
    tm = _pick(t, 1024)

    def body(du_ref, h_ref, w_ref, s_ref, dh_ref, dp_ref, ds_ref):
        @pl.when(pl.program_id(1) == 0)
        def _():
            ds_ref[...] = jnp.zeros_like(ds_ref)

        duv = du_ref[...]
        ds_ref[...] += jnp.sum(duv * h_ref[...], axis=0, keepdims=True)
        dh = (duv * s_ref[...]).astype(BF16)
        dh_ref[...] = dh
        dp_ref[...] = lax.dot_general(dh, w_ref[...], NT, preferred_element_type=F32)

    blk = pl.BlockSpec((tm, cg), lambda g, i: (i, g))
    vec = pl.BlockSpec((1, cg), lambda g, i: (0, g))
    return pl.pallas_call(
        body, name=name, grid=(4, t // tm), out_shape=[SDS((t, d), BF16), SDS((t, d), F32), SDS((1, d), F32)],
        in_specs=[blk, blk, pl.BlockSpec((None, cg, cg), lambda g, i: (g, 0, 0)), vec],
        out_specs=[blk, blk, vec], compiler_params=_cp("parallel", "arbitrary"),
    )(du, hraw, w, scale)


def _pool_dw(name, pooled, dh):
    t, d = pooled.shape
    cg = d // 4
    tk = _pick(t, 512)
    nk = t // tk

    def body(p_ref, dh_ref, o_ref, acc):
        k = pl.program_id(1)

        @pl.when(k == 0)
        def _():
            acc[...] = jnp.zeros_like(acc)

        acc[...] += lax.dot_general(p_ref[...], dh_ref[...], TN, preferred_element_type=F32)

        @pl.when(k == nk - 1)
        def _():
            o_ref[...] = acc[...].astype(o_ref.dtype)

    blk = pl.BlockSpec((tk, cg), lambda g, k: (k, g))
    return pl.pallas_call(
        body, name=name, grid=(4, nk), out_shape=SDS((4, cg, cg), BF16), in_specs=[blk, blk],
        out_specs=pl.BlockSpec((None, cg, cg), lambda g, k: (g, 0, 0)), scratch_shapes=[pltpu.VMEM((cg, cg), F32)],
        compiler_params=_cp("parallel", "arbitrary"),
    )(pooled, dh)


def _conv_pre(u, w_ref, b_ref, row):
    pre = b_ref[...] + _shift_down(u, 3, row) * w_ref[0:1, :]
    pre = pre + _shift_down(u, 2, row) * w_ref[1:2, :]
    pre = pre + _shift_down(u, 1, row) * w_ref[2:3, :]
    return pre + u * w_ref[3:4, :]


def _conv_fwd(name, zx, conv_w, conv_b, di):
    t = zx.shape[0]
    cd = conv_w.shape[1]
    cw = _pick(cd, 256)
    off = di // cw

    def body(u_ref, w_ref, b_ref, o_ref):
        row = lax.broadcasted_iota(jnp.int32, (t, 1), 0)
        pre = _conv_pre(u_ref[...], w_ref, b_ref, row)
        o_ref[...] = pre * _sigmoid(pre)

    return pl.pallas_call(
        body, name=name, grid=(cd // cw,), out_shape=SDS((t, cd), F32),
        in_specs=[pl.BlockSpec((t, cw), lambda j: (0, off + j)), pl.BlockSpec((CONV_WIDTH, cw), lambda j: (0, j)),
                  pl.BlockSpec((1, cw), lambda j: (0, j))],
        out_specs=pl.BlockSpec((t, cw), lambda j: (0, j)), compiler_params=_cp("parallel"),
    )(zx, conv_w, conv_b)


def _conv_bwd(name, zx, conv_w, conv_b, dact, di):
    t = zx.shape[0]
    cd = conv_w.shape[1]
    cw = _pick(cd, 256)
    off = di // cw

    def body(u_ref, w_ref, b_ref, da_ref, du_ref, dw_ref, db_ref):
        row = lax.broadcasted_iota(jnp.int32, (t, 1), 0)
        u = u_ref[...]
        pre = _conv_pre(u, w_ref, b_ref, row)
        sg = _sigmoid(pre)
        dpre = da_ref[...] * (sg * (1.0 + pre * (1.0 - sg)))
        du = dpre * w_ref[3:4, :]
        for j in (1, 2, 3):
            du = du + _shift_up(dpre, j, row) * w_ref[3 - j:4 - j, :]
            dw_ref[3 - j:4 - j, :] = jnp.sum(dpre * _shift_down(u, j, row), axis=0, keepdims=True)
        dw_ref[3:4, :] = jnp.sum(dpre * u, axis=0, keepdims=True)
        db_ref[...] = jnp.sum(dpre, axis=0, keepdims=True)
        du_ref[...] = du.astype(du_ref.dtype)

    wspec = pl.BlockSpec((CONV_WIDTH, cw), lambda j: (0, j))
    bspec = pl.BlockSpec((1, cw), lambda j: (0, j))
    ospec = pl.BlockSpec((t, cw), lambda j: (0, j))
    return pl.pallas_call(
        body, name=name, grid=(cd // cw,), out_shape=[SDS((t, cd), BF16), SDS((CONV_WIDTH, cd), F32), SDS((1, cd), F32)],
        in_specs=[pl.BlockSpec((t, cw), lambda j: (0, off + j)), wspec, bspec, ospec],
        out_specs=[ospec, wspec, bspec], compiler_params=_cp("parallel"),
    )(zx, conv_w, conv_b, dact)


def _softplus(v):
    return jnp.maximum(v, 0.0) + jnp.log(1.0 + jnp.exp(-jnp.abs(v)))


def _dt_fwd(name, zx, bias, a_log, col_block):
    t = zx.shape[0]

    def body(r_ref, b_ref, al_ref, dt_ref, acs_ref, ein_ref, eout_ref):
        row = lax.broadcasted_iota(jnp.int32, (t, 1), 0) % CHUNK
        dt = _softplus(r_ref[...] + b_ref[...])
        da = dt * (-jnp.exp(al_ref[...]))
        s, r = da, da
        j = 1
        while j < CHUNK:
            s = s + jnp.where(row >= j, pltpu.roll(s, j, 0), 0.0)
            r = r + jnp.where(row < CHUNK - j, pltpu.roll(r, t - j, 0), 0.0)
            j *= 2
        dt_ref[...] = dt
        acs_ref[...] = s
        ein_ref[...] = jnp.exp(s)
        eout_ref[...] = jnp.exp(r - da)

    vec = pl.BlockSpec((1, V7X_LANES), lambda i: (0, 0))
    full = pl.BlockSpec((t, V7X_LANES), lambda i: (0, 0))
    return pl.pallas_call(
        body, name=name, grid=(1,), out_shape=[SDS((t, V7X_LANES), F32)] * 4,
        in_specs=[pl.BlockSpec((t, V7X_LANES), lambda i: (0, col_block)), vec, vec], out_specs=[full] * 4,
        compiler_params=_cp("arbitrary"),
    )(zx, bias, a_log)


def _dt_bwd(name, zx, bias, a_log, d_acs, d_dt, col_block):
    t = zx.shape[0]

    def body(r_ref, b_ref, al_ref, da_ref, dd_ref, draw_ref, db_ref, dal_ref):
        row = lax.broadcasted_iota(jnp.int32, (t, 1), 0) % CHUNK
        pre = r_ref[...] + b_ref[...]
        dt = _softplus(pre)
        a = -jnp.exp(al_ref[...])
        s = da_ref[...]
        j = 1
        while j < CHUNK:
            s = s + jnp.where(row < CHUNK - j, pltpu.roll(s, t - j, 0), 0.0)
            j *= 2
        ddt = dd_ref[...] + s * a
        dal_ref[...] = jnp.sum(s * dt, axis=0, keepdims=True) * a
        draw = ddt * _sigmoid(pre)
        db_ref[...] = jnp.sum(draw, axis=0, keepdims=True)
        draw_ref[...] = draw.astype(draw_ref.dtype)

    vec = pl.BlockSpec((1, V7X_LANES), lambda i: (0, 0))
    full = pl.BlockSpec((t, V7X_LANES), lambda i: (0, 0))
    return pl.pallas_call(
        body, name=name, grid=(1,), out_shape=[SDS((t, V7X_LANES), BF16), SDS((1, V7X_LANES), F32), SDS((1, V7X_LANES), F32)],
        in_specs=[pl.BlockSpec((t, V7X_LANES), lambda i: (0, col_block)), vec, vec, full, full],
        out_specs=[full, vec, vec], compiler_params=_cp("arbitrary"),
    )(zx, bias, a_log, d_acs, d_dt)


def _ssd_specs(t, di, g_n, hpg, p, rev):
    nc = t // CHUNK
    w = hpg * p
    nb = di // D_STATE

    def cc(c):
        return nc - 1 - c if rev else c

    return dict(
        xs=pl.BlockSpec((CHUNK, w), lambda g, c: (cc(c), g)),
        bm=pl.BlockSpec((CHUNK, D_STATE), lambda g, c: (cc(c), nb + g)),
        cm=pl.BlockSpec((CHUNK, D_STATE), lambda g, c: (cc(c), nb + g_n + g)),
        col=pl.BlockSpec((None, CHUNK, hpg), lambda g, c: (g, cc(c), 0)),
        rowv=pl.BlockSpec((None, hpg, CHUNK), lambda g, c: (g, 0, cc(c))),
        head=pl.BlockSpec((None, 1, hpg), lambda g, c: (g, 0, 0)),
        lanes=pl.BlockSpec((1, w), lambda g, c: (0, g)),
        bc=pl.BlockSpec((CHUNK, D_STATE), lambda g, c: (cc(c), g)),
        prev=pl.BlockSpec((None, None, D_STATE, w), lambda g, c: (cc(c), g, 0, 0)),
        seg=pl.BlockSpec((w, V7X_LANES), lambda g, c: (0, 0)),
    )


def _decay_masks(cb, ac_ref, ar_ref, heads):
    li = lax.broadcasted_iota(jnp.int32, (CHUNK, CHUNK), 0)
    si = lax.broadcasted_iota(jnp.int32, (CHUNK, CHUNK), 1)
    lms = [jnp.exp(jnp.where(li >= si, ac_ref[:, hh:hh + 1] - ar_ref[hh:hh + 1, :], -jnp.inf)) for hh in heads]
    return lms, [(cb * lm).astype(BF16) for lm in lms]


def _ssd_fwd(name, xbc, dt_x, ein_x, eout_x, a_col, a_row, d_x, di, g_n, hpg, p):
    t = xbc.shape[0]
    nc = t // CHUNK
    w = hpg * p
    assert 2 * p == V7X_LANES and hpg % 2 == 0
    sp = _ssd_specs(t, di, g_n, hpg, p, False)

    def body(xs_ref, bm_ref, cm_ref, dt_ref, ein_ref, eout_ref, ac_ref, ar_ref, d_ref, y_ref, prev_ref, h_ref):
        @pl.when(pl.program_id(1) == 0)
        def _():
            h_ref[...] = jnp.zeros_like(h_ref)

        bm = bm_ref[...].astype(BF16)
        cm = cm_ref[...].astype(BF16)
        cb = lax.dot_general(cm, bm, NT, preferred_element_type=F32)
        first = lax.broadcasted_iota(jnp.int32, (1, V7X_LANES), 1) < p
        xs = xs_ref[...]
        e_in = ein_ref[...]
        xdt = xs * dt_ref[...]
        ys = []
        for pr in range(hpg // 2):
            _, ms = _decay_masks(cb, ac_ref, ar_ref, (2 * pr, 2 * pr + 1))
            xp = xdt[:, pr * V7X_LANES:(pr + 1) * V7X_LANES]
            rhs = jnp.concatenate([jnp.where(first, xp, 0.0), jnp.where(first, 0.0, xp)], axis=0).astype(BF16)
            ys.append(jnp.dot(jnp.concatenate(ms, axis=1), rhs, preferred_element_type=F32))
        h_prev = h_ref[...]
        prev_ref[...] = h_prev
        y = jnp.concatenate(ys, axis=1) + jnp.dot(cm, h_prev.astype(BF16), preferred_element_type=F32) * e_in
        y_ref[...] = y + xs * d_ref[...]
        st = lax.dot_general(bm, (xdt * eout_ref[...]).astype(BF16), TN, preferred_element_type=F32)
        h_ref[...] = e_in[CHUNK - 1:CHUNK, :] * h_prev + st

    return pl.pallas_call(
        body, name=name, grid=(g_n, nc),
        out_shape=[SDS((t, di), F32), SDS((nc, g_n, D_STATE, w), F32)],
        in_specs=[sp["xs"], sp["bm"], sp["cm"], sp["xs"], sp["xs"], sp["xs"], sp["col"], sp["rowv"], sp["lanes"]],
        out_specs=[sp["xs"], sp["prev"]], scratch_shapes=[pltpu.VMEM((D_STATE, w), F32)],
        compiler_params=_cp("parallel", "arbitrary"),
    )(xbc, xbc, xbc, dt_x, ein_x, eout_x, a_col, a_row, d_x)


def _head_sums(v, seg):
    hi = v.astype(BF16)
    lo = (v - hi.astype(F32)).astype(BF16)
    return jnp.dot(hi, seg, preferred_element_type=F32) + jnp.dot(lo, seg, preferred_element_type=F32)


def _head_totals(v, seg):
    part = v[0:8]
    for r in range(8, v.shape[0], 8):
        part = part + v[r:r + 8]
    return jnp.sum(_head_sums(part, seg), axis=0, keepdims=True)


def _ssd_bwd(name, xbc, dt_x, ein_x, eout_x, a_col, a_row, d_x, prev, dy, di, g_n, hpg, p):
    t = xbc.shape[0]
    nc = t // CHUNK
    w = hpg * p
    sp = _ssd_specs(t, di, g_n, hpg, p, True)
    seg = (lax.broadcasted_iota(jnp.int32, (w, V7X_LANES), 0) // p
           == lax.broadcasted_iota(jnp.int32, (w, V7X_LANES), 1)).astype(BF16)

    def body(xs_ref, bm_ref, cm_ref, dt_ref, ein_ref, eout_ref, ac_ref, ar_ref, d_ref, prev_ref, dy_ref,
             seg_ref, dx_ref, dbm_ref, dcm_ref, ddt_ref, dacs_ref, dd_ref, dh_ref):
        @pl.when(pl.program_id(1) == 0)
        def _():
            dh_ref[...] = jnp.zeros_like(dh_ref)
            dd_ref[...] = jnp.zeros_like(dd_ref)

        bm = bm_ref[...].astype(BF16)
        cm = cm_ref[...].astype(BF16)
        cb = lax.dot_general(cm, bm, NT, preferred_element_type=F32)
        first = lax.broadcasted_iota(jnp.int32, (1, V7X_LANES), 1) < p
        last_row = lax.broadcasted_iota(jnp.int32, (CHUNK, 1), 0) == CHUNK - 1
        seg_m = seg_ref[...]
        xs, dy, e_in, e_out, d_skip = xs_ref[...], dy_ref[...], ein_ref[...], eout_ref[...], d_ref[...]
        dt_l = dt_ref[...]
        xdt = xs * dt_l
        h_prev = prev_ref[...]
        h_prev_b = h_prev.astype(BF16)
        dh_next = dh_ref[...]
        dh_next_b = dh_next.astype(BF16)
        dy_e = (dy * e_in).astype(BF16)
        d_cm = lax.dot_general(dy_e, h_prev_b, NT, preferred_element_type=F32)
        dh_ref[...] = e_in[CHUNK - 1:CHUNK, :] * dh_next + lax.dot_general(cm, dy_e, TN, preferred_element_type=F32)
        q = jnp.dot(bm, dh_next_b, preferred_element_type=F32)
        xf = xdt * e_out
        d_bm = lax.dot_general(xf.astype(BF16), dh_next_b, NT, preferred_element_type=F32)
        d_cb = jnp.zeros((CHUNK, CHUNK), F32)
        parts, w_parts = [], []
        for pr in range(hpg // 2):
            lanes = slice(pr * V7X_LANES, (pr + 1) * V7X_LANES)
            lms, ms = _decay_masks(cb, ac_ref, ar_ref, (2 * pr, 2 * pr + 1))
            xp = xdt[:, lanes]
            xp_b = xp.astype(BF16)
            dyp = dy[:, lanes]
            halves = [jnp.where(first, dyp, 0.0).astype(BF16), jnp.where(first, 0.0, dyp).astype(BF16)]
            for lm, half in zip(lms, halves):
                d_cb = d_cb + lax.dot_general(half, xp_b, NT, preferred_element_type=F32) * lm
            dxd = lax.dot_general(jnp.concatenate(ms, axis=0), jnp.concatenate(halves, axis=0), TN,
                                  preferred_element_type=F32)
            stacked = jnp.concatenate([jnp.where(first, xp, 0.0), jnp.where(first, 0.0, xp)], axis=0).astype(BF16)
            y_diag = jnp.dot(jnp.concatenate(ms, axis=1), stacked, preferred_element_type=F32)
            parts.append(dxd)
            w_parts.append(dyp.astype(BF16).astype(F32) * y_diag - xp_b.astype(F32) * dxd)
        d_xdt = jnp.concatenate(parts, axis=1) + q * e_out
        dx_ref[...] = d_xdt * dt_l + dy * d_skip
        ch = jnp.dot(cm, h_prev_b, preferred_element_type=F32)
        qx = q * xf
        s_a = _head_sums(dy * ch * e_in - qx + jnp.concatenate(w_parts, axis=1), seg_m)[:, :hpg]
        d_last = (_head_totals(qx, seg_m)[:, :hpg]
                  + jnp.exp(ac_ref[CHUNK - 1:CHUNK, :]) * _head_totals(dh_next * h_prev, seg_m)[:, :hpg])
        ddt_ref[...] = _head_sums(d_xdt * xs, seg_m)[:, :hpg]
        dacs_ref[...] = s_a + jnp.where(last_row, d_last, 0.0)
        dd_ref[...] += _head_totals(dy * xs, seg_m)[:, :hpg]
        d_cb_b = d_cb.astype(BF16)
        dcm_ref[...] = d_cm + jnp.dot(d_cb_b, bm, preferred_element_type=F32)
        dbm_ref[...] = d_bm + lax.dot_general(d_cb_b, cm, TN, preferred_element_type=F32)

    gn = g_n * D_STATE
    return pl.pallas_call(
        body, name=name, grid=(g_n, nc),
        out_shape=[SDS((t, di), F32), SDS((t, gn), F32), SDS((t, gn), F32), SDS((g_n, t, hpg), F32),
                   SDS((g_n, t, hpg), F32), SDS((g_n, 1, hpg), F32)],
        in_specs=[sp["xs"], sp["bm"], sp["cm"], sp["xs"], sp["xs"], sp["xs"], sp["col"], sp["rowv"],
                  sp["lanes"], sp["prev"], sp["xs"], sp["seg"]],
        out_specs=[sp["xs"], sp["bc"], sp["bc"], sp["col"], sp["col"], sp["head"]],
        scratch_shapes=[pltpu.VMEM((D_STATE, w), F32)],
        compiler_params=_cp("parallel", "arbitrary"),
    )(xbc, xbc, xbc, dt_x, ein_x, eout_x, a_col, a_row, d_x, prev, dy, seg)


def _as3d(a):
    return a.reshape(a.shape[0], -1, a.shape[-1])


def _pair_sum(name, own, recv, core):
    shape = recv.shape
    cols = shape[-1]
    own3, recv3 = own.reshape(8, -1, cols), recv.reshape(4, -1, cols)
    rows = recv3.shape[1]
    tr = _row_tile(rows, cols, 2)

    def body(c_ref, a_ref, b_ref, o_ref):
        o_ref[...] = (a_ref[...].astype(F32) + b_ref[...].astype(F32)).astype(o_ref.dtype)

    blk = pl.BlockSpec((None, tr, cols), lambda q, i, c_ref: (q, i, 0))
    out = pl.pallas_call(
        body, name=name, out_shape=SDS(recv3.shape, recv.dtype),
        grid_spec=pltpu.PrefetchScalarGridSpec(
            num_scalar_prefetch=1, grid=(4, rows // tr),
            in_specs=[pl.BlockSpec((None, tr, cols), lambda q, i, c_ref: (4 * c_ref[0] + q, i, 0)), blk], out_specs=blk),
        compiler_params=_cp("parallel", "parallel"),
    )(core, own3, recv3)
    return out.reshape(shape)


def _adamw(name, w, m, v, parts, layer, prev=None, sel=None):
    lyr, rows, cols = w.shape
    n = len(parts)
    tr = _row_tile(rows, cols)
    np_ = 0 if prev is None else 4
    if sel is None:
        sel = jnp.zeros((1,), jnp.int32)

    def body(sel_ref, *refs):
        w_ref, m_ref, v_ref = refs[:3]
        p_refs = refs[3:3 + n]
        g_ref, d_ref, nm_ref, nv_ref = refs[3 + n + np_:]
        g = p_refs[0][...].astype(F32)
        for r in p_refs[1:]:
            g = g + r[...].astype(F32)
        nm = ADAM_B1 * m_ref[...] + (1.0 - ADAM_B1) * g
        nv = ADAM_B2 * v_ref[...] + (1.0 - ADAM_B2) * (g * g)
        m_hat = nm / (1.0 - ADAM_B1 ** ADAM_STEP)
        v_hat = nv / (1.0 - ADAM_B2 ** ADAM_STEP)
        g_ref[...] = g
        d_ref[...] = -ADAM_LR * (m_hat / (jnp.sqrt(v_hat) + ADAM_EPS) + ADAM_WD * w_ref[...])
        nm_ref[...] = nm
        nv_ref[...] = nv

    lspec = pl.BlockSpec((None, tr, cols), lambda i, s: (layer, i, 0))
    pspecs = [pl.BlockSpec((None, tr, cols), (lambda i, s: (s[0], i, 0)) if q is None else (lambda i, s, q=q: (q, i, 0)))
              for _, q in parts]
    aliases = {} if prev is None else {4 + n + q: q for q in range(4)}
    return pl.pallas_call(
        body, name=name, out_shape=[SDS(w.shape, F32)] * 4,
        grid_spec=pltpu.PrefetchScalarGridSpec(
            num_scalar_prefetch=1, grid=(rows // tr,), in_specs=[lspec] * 3 + pspecs + [ANY] * np_,
            out_specs=[lspec] * 4),
        input_output_aliases=aliases, compiler_params=_cp("parallel"),
    )(sel, w, m, v, *[arr for arr, _ in parts], *(prev or ()))


def _sum8(name, parts):
    rows = parts.shape[1]

    def body(p_ref, o_ref):
        s = p_ref[0]
        for q in range(1, N_DEV):
            s = s + p_ref[q]
        o_ref[...] = s

    return pl.pallas_call(
        body, name=name, grid=(1,), out_shape=SDS((rows, V7X_LANES), F32),
        in_specs=[pl.BlockSpec((N_DEV, rows, V7X_LANES), lambda i: (0, 0, 0))],
        out_specs=pl.BlockSpec((rows, V7X_LANES), lambda i: (0, 0)), compiler_params=_cp("arbitrary"),
    )(parts)


def _pack(vectors, align):
    flat = jnp.concatenate([v.reshape(-1) for v in vectors])
    pad = (-flat.shape[0]) % align
    if pad:
        flat = jnp.concatenate([flat, jnp.zeros((pad,), F32)])
    return flat.reshape(-1, V7X_LANES)


def _unpack(packed, shapes):
    flat = packed.reshape(-1)
    out, o = [], 0
    for s in shapes:
        size = 1
        for dim in s:
            size *= dim
        out.append(flat[o:o + size].reshape(s))
        o += size
    return out


def _relu2_epilogue(acc):
    r = jnp.maximum(acc, 0.0)
    return acc, r * r


def _residual_epilogue(acc, res):
    return (ALPHA * res + acc,)


def _plain_add_epilogue(acc, res):
    return (res + acc,)


def _gate_epilogue(acc, y, e):
    gate = _sigmoid(acc)
    xn = y + gate * e
    return xn, gate, xn


def _relu2_bwd_epilogue(acc, pre):
    return (acc * (2.0 * jnp.maximum(pre, 0.0)),)


def _tail_fwd(tag, u_a, wts, lng, lnb, p_l):
    y1, y1_b = _ln_fwd(f"ln1_{tag}", u_a, lng[0], lnb[0])
    pre, act = _mm_fwd(f"mlp1_{tag}", y1_b, wts["w1"], "col", [F32, BF16], _relu2_epilogue)
    (u_b,) = _mm_fwd(f"mlp2_{tag}", act, wts["w2"], "row", [F32], _residual_epilogue, (y1,))
    y2, y2_b = _ln_fwd(f"ln2_{tag}", u_b, lng[1], lnb[1])
    (e,) = _mm_fwd(f"ple_{tag}", p_l, wts["plew"], "col", [F32])
    xn, gate, xn_b = _mm_fwd(f"gate_{tag}", y2_b, wts["gate"], "row", [F32, F32, BF16], _gate_epilogue, (y2, e))
    return xn, xn_b, (u_a, y1_b, pre, act, u_b, y2_b, e, gate)


def _tail_bwd(tag, dxn, saved, wts, lng, p_l, emit):
    u_a, y1_b, pre, act, u_b, y2_b, e, gate = saved
    dgpre, de = _ple_bwd(f"ple_bwd_{tag}", dxn, e, gate)
    toks = emit(f"{tag}_ple", dict(gate=_mm_dw(f"gate_dw_{tag}", y2_b, dgpre, "row"),
                                   plew=_mm_dw(f"ple_dw_{tag}", p_l, de, "col")))
    (dy2,) = _mm_dx(f"gate_dx_{tag}", dgpre, wts["gate"], "row", [F32], _plain_add_epilogue, (dxn,))
    du_b, du_b16, dg2, db2 = _ln_bwd(f"ln2_bwd_{tag}", u_b, dy2, _token_add(lng[1], *toks))
    toks = emit(f"{tag}_w2", dict(w2=_mm_dw(f"mlp2_dw_{tag}", act, du_b16, "row")))
    (dpre,) = _mm_dx(f"mlp2_dx_{tag}", du_b16, wts["w2"], "row", [BF16], _relu2_bwd_epilogue, (pre,))
    toks += emit(f"{tag}_w1", dict(w1=_mm_dw(f"mlp1_dw_{tag}", y1_b, dpre, "col")))
    (dy1,) = _mm_dx(f"mlp1_dx_{tag}", dpre, wts["w1"], "col", [F32], _residual_epilogue, (du_b,))
    du_a, du_a16, dg1, db1 = _ln_bwd(f"ln1_bwd_{tag}", u_a, dy1, _token_add(lng[0], *toks))
    return du_a, du_a16, [dg1, dg2], [db1, db2]


def _to_slots(a, axis):
    shape = a.shape
    per = shape[axis] // N_DEV
    v = a.reshape(shape[:axis] + (2, 2, 2, per) + shape[axis + 1:])
    perm = (axis + 2, axis, axis + 1) + tuple(range(axis)) + tuple(range(axis + 3, v.ndim))
    v = v.transpose(perm)
    return v.reshape((N_DEV,) + shape[:axis] + (per,) + shape[axis + 1:])


def _pad_lanes(a):
    return jnp.pad(a, [(0, 0)] * (a.ndim - 1) + [(0, V7X_LANES - a.shape[-1])])


def kernel(x, p, pool_w, pool_scale, ssm_in_w, ssm_conv_w, ssm_conv_b, ssm_dt_bias, ssm_a_log, ssm_d, ssm_norm_w, ssm_out_w, mlp_w1, mlp_w2, ln_g, ln_b, ple_w, ple_gate_w, loss_target, m_pool_w, m_pool_scale, m_ssm_in_w, m_ssm_conv_w, m_ssm_conv_b, m_ssm_dt_bias, m_ssm_a_log, m_ssm_d, m_ssm_norm_w, m_ssm_out_w, m_mlp_w1, m_mlp_w2, m_ln_g, m_ln_b, m_ple_w, m_ple_gate_w, v_pool_w, v_pool_scale, v_ssm_in_w, v_ssm_conv_w, v_ssm_conv_b, v_ssm_dt_bias, v_ssm_a_log, v_ssm_d, v_ssm_norm_w, v_ssm_out_w, v_mlp_w1, v_mlp_w2, v_ln_g, v_ln_b, v_ple_w, v_ple_gate_w):
    t, d = x.shape[1:]
    h_n = ssm_dt_bias.shape[-1]
    di_s, cd_s, dp_s, d_s = ssm_norm_w.shape[-1], ssm_conv_b.shape[-1], ssm_in_w.shape[-1], ln_g.shape[-1]
    di, cd, dp = N_DEV * di_s, N_DEV * cd_s, N_DEV * dp_s
    p_dim = di // h_n
    g_n = (cd - di) // (2 * D_STATE)
    hpg = h_n // g_n
    dpp = di + cd + V7X_LANES
    assert h_n <= V7X_LANES and dp == di + cd + h_n and (di + cd) % V7X_LANES == 0
    dt_block = (di + cd) // V7X_LANES
    cg = d // 4
    me = 4 * lax.axis_index("x") + 2 * lax.axis_index("y") + lax.axis_index("c")

    x0, target = x[0], loss_target[0]
    p_l = [p[0, 0].astype(BF16), p[1, 0].astype(BF16)]

    small_shapes = [(CONV_WIDTH, cd_s), (1, cd_s), (1, di_s), (2, 2, d_s), (2, 2, d_s)]
    small = _pack([ssm_conv_w[0], ssm_conv_b, ssm_norm_w, ln_g, ln_b], 8 * V7X_LANES)
    first = [w.astype(BF16) for w in (pool_w[0], mlp_w1[0], mlp_w2[0], ple_w[0], ple_gate_w[0])]
    own_ssm = [w.astype(BF16) for w in (ssm_in_w[0], ssm_out_w[0])]
    own_mlp = [w.astype(BF16) for w in (mlp_w1[1], mlp_w2[1], ple_w[1], ple_gate_w[1])]
    pool_g, w1_0, w2_0, plew_0, gate_0, small_g = _all_gather("ag_layer0", first + [small])

    def gather_start(tag, own, after):
        lands = [lax.empty((N_DEV,) + w.shape, w.dtype) for w in own]
        return _async_start(f"ag_{tag}_start", _ag_first_copies, (4 * len(own),), own, lands, after)

    def gather_finish(tag, handle, after):
        n = len(handle[2]) // 2
        own, lands = _async_wait(f"ag_{tag}_wait", _ag_first_copies, handle, n, after)
        fwd = _async_start(f"ag_{tag}_forward_start", _ag_forward_copies, (3 * n,), [], lands)
        _, lands = _async_wait(f"ag_{tag}_forward_wait", _ag_forward_copies, fwd, 0)
        return [lax.dynamic_update_slice_in_dim(g, w[None], me, 0) for g, w in zip(lands, own)]

    ag_ssm = gather_start("ssm", own_ssm, (pool_g,))
    ag_mlp = gather_start("mlp1", own_mlp, (ag_ssm[3],))
    pool_scale_fwd = _token_add(pool_scale, ag_ssm[3], ag_mlp[3])
    pool_full = pool_g.transpose(1, 0, 2, 3).reshape(4, cg, cg)
    sm = small_g.reshape(N_DEV, -1)
    o = 0
    parts = []
    for shp in small_shapes:
        size = 1
        for s in shp:
            size *= s
        parts.append(sm[:, o:o + size].reshape((N_DEV,) + shp))
        o += size
    conv_w_full = parts[0].transpose(1, 0, 2).reshape(CONV_WIDTH, cd)
    conv_b_full = parts[1].transpose(1, 0, 2).reshape(1, cd)
    norm_w_full = parts[2].transpose(1, 0, 2).reshape(1, di)
    ln_g_full = parts[3].transpose(1, 2, 0, 3).reshape(2, 2, 1, d)
    ln_b_full = parts[4].transpose(1, 2, 0, 3).reshape(2, 2, 1, d)
    bias128, alog128 = _pad_lanes(ssm_dt_bias), _pad_lanes(ssm_a_log)

    pooled = _pool_windows("pool_fwd", x0, False)
    u0, hraw = _pool_mm("pool_mm", pooled, pool_full, pool_scale_fwd, x0)
    wts = [dict(w1=w1_0, w2=w2_0, plew=plew_0, gate=gate_0)]
    x1, x1_b, saved0 = _tail_fwd("l0", u0, wts[0], ln_g_full[0], ln_b_full[0], p_l[0])

    in_g, out_g = gather_finish("ssm", ag_ssm, (x1,))
    in_full = jnp.pad(in_g.transpose(1, 0, 2).reshape(d, dp), ((0, 0), (0, dpp - dp)))
    (zx,) = _mm_fwd("in_proj", x1_b, in_full, "plain", [F32])
    xbc = _conv_fwd("conv_fwd", zx, conv_w_full, conv_b_full, di)
    dt, acs, e_in, e_out = _dt_fwd("dt_fwd", zx, bias128, alog128, dt_block)

    def per_lane(a):
        return jnp.repeat(a[:, :h_n], p_dim, axis=1)

    dt_x, ein_x, eout_x, d_x = per_lane(dt), per_lane(e_in), per_lane(e_out), per_lane(ssm_d)

    def to_col(a):
        return a[:, :h_n].reshape(t, g_n, hpg).transpose(1, 0, 2)

    def to_row(a):
        return a[:, :h_n].reshape(t, g_n, hpg).transpose(1, 2, 0)

    def from_col(a):
        return _pad_lanes(a.transpose(1, 0, 2).reshape(t, h_n))

    a_col, a_row = to_col(acs), to_row(acs)
    y_ssd, prev = _ssd_fwd("ssd_fwd", xbc, dt_x, ein_x, eout_x, a_col, a_row, d_x, di, g_n, hpg, p_dim)
    yn = _gated_rms_fwd("gated_rms_fwd", y_ssd, zx, norm_w_full)
    (u2,) = _mm_fwd("out_proj", yn, out_g, "row", [F32], _residual_epilogue, (x1,))
    w1_1, w2_1, plew_1, gate_1 = gather_finish("mlp1", ag_mlp, (u2,))
    wts.append(dict(w1=w1_1, w2=w2_1, plew=plew_1, gate=gate_1))
    x2, _, saved1 = _tail_fwd("l1", u2, wts[1], ln_g_full[1], ln_b_full[1], p_l[1])

    core = lax.axis_index("c").astype(jnp.int32).reshape(1)
    chip = (2 * lax.axis_index("x") + lax.axis_index("y")).astype(jnp.int32).reshape(1)
    scattering = {}

    def emit(tag, grads):
        names, arrays = list(grads), list(grads.values())
        halves = _rs_sibling_exchange(f"rs_{tag}_sibling", arrays)
        sums = [_pair_sum(f"rs_{tag}_pair_sum_{n}", g, hv, core) for n, g, hv in zip(names, arrays, halves)]
        lands = [lax.empty((3,) + s.shape[1:], s.dtype) for s in sums]
        handle = _async_start(f"rs_{tag}_start", _rs_chip_copies, (3 * len(sums),), sums, lands)
        scattering[tag] = (names, handle)
        return [handle[3]]

    def collect(tag, after):
        names, handle = scattering.pop(tag)
        sums, thirds = _async_wait(f"rs_{tag}_wait", _rs_chip_copies, handle, len(names), after)
        return {n: (s.reshape(4, -1, s.shape[-1]), r.reshape(3, -1, r.shape[-1])) for n, s, r in zip(names, sums, thirds)}

    dx2, loss_cols = _loss_bwd("loss", x2, target)
    du2, du2_b, dg_1, db_1 = _tail_bwd("l1", dx2, saved1, wts[1], ln_g_full[1], p_l[1], emit)
    toks = emit("ssm_out", dict(out=_mm_dw("out_proj_dw", yn, du2_b, "row")))
    (dyn,) = _mm_dx("out_proj_dx", du2_b, out_g, "row", [F32])
    dy_ssd, dz, d_norm_w = _gated_rms_bwd("gated_rms_bwd", y_ssd, zx, _token_add(norm_w_full, *toks), dyn)
    dxs, dbm, dcm, ddt_x, dacs, dd = _ssd_bwd("ssd_bwd", xbc, dt_x, ein_x, eout_x, a_col, a_row, d_x, prev, dy_ssd,
                                              di, g_n, hpg, p_dim)
    draw, d_bias, d_alog = _dt_bwd("dt_bwd", zx, bias128, alog128, from_col(dacs), from_col(ddt_x), dt_block)
    dxbc, d_conv_w, d_conv_b = _conv_bwd("conv_bwd", zx, conv_w_full, conv_b_full,
                                         jnp.concatenate([dxs, dbm, dcm], axis=1), di)
    dzx = jnp.concatenate([dz, dxbc, draw], axis=1)
    toks = emit("ssm_in", {"in": _to_slots(_mm_dw("in_proj_dw", x1_b, dzx, "plain")[:, :dp], 1)})
    (dx1,) = _mm_dx("in_proj_dx", dzx, in_full, "plain", [F32], _residual_epilogue, (du2,))

    du0, _, dg_0, db_0 = _tail_bwd("l0", dx1, saved0, wts[0], _token_add(ln_g_full[0], *toks), p_l[0], emit)
    dh, dpool, d_scale = _pool_bwd_mm("pool_bwd_mm", du0, hraw, pool_full, pool_scale)
    toks = emit("pool", dict(pool=_to_slots(_pool_dw("pool_dw", pooled, dh), 1)))
    grad_x = _pool_windows("pool_bwd", dpool, True, du0)

    d_ln_g = jnp.stack([jnp.stack(dg_0), jnp.stack(dg_1)]).reshape(2, 2, d)
    d_ln_b = jnp.stack([jnp.stack(db_0), jnp.stack(db_1)]).reshape(2, 2, d)
    partial_shapes = [(CONV_WIDTH, cd), (1, cd), (1, di), (2, 2, d), (2, 2, d), (1, d), (1, h_n), (1, h_n), (1, h_n),
                      (1, d)]
    partial = _pack([d_conv_w, d_conv_b, d_norm_w, d_ln_g, d_ln_b, d_scale, d_bias[:, :h_n], d_alog[:, :h_n],
                     dd.reshape(1, h_n), loss_cols], 8 * V7X_LANES)
    (all_partials,) = _all_gather("ag_small_grads", [_token_add(partial, *toks)])
    tot = _unpack(_sum8("sum_small_grads", all_partials), partial_shapes)
    t_conv_w, t_conv_b, t_norm_w, t_ln_g, t_ln_b, t_scale, t_bias, t_alog, t_dd, t_loss = tot
    loss = jnp.sum(t_loss)

    def mine(a, per):
        return lax.dynamic_slice_in_dim(a, me * per, per, axis=a.ndim - 1)

    small_names = ["ssm_conv_w", "ssm_conv_b", "ssm_norm_w", "ln_g", "ln_b", "pool_scale", "ssm_dt_bias", "ssm_a_log",
                   "ssm_d"]
    small_w = [ssm_conv_w, ssm_conv_b, ssm_norm_w, ln_g, ln_b, pool_scale, ssm_dt_bias, ssm_a_log, ssm_d]
    small_m = [m_ssm_conv_w, m_ssm_conv_b, m_ssm_norm_w, m_ln_g, m_ln_b, m_pool_scale, m_ssm_dt_bias, m_ssm_a_log,
               m_ssm_d]
    small_v = [v_ssm_conv_w, v_ssm_conv_b, v_ssm_norm_w, v_ln_g, v_ln_b, v_pool_scale, v_ssm_dt_bias, v_ssm_a_log,
               v_ssm_d]
    small_grads = [mine(t_conv_w, cd_s), mine(t_conv_b, cd_s), mine(t_norm_w, di_s), mine(t_ln_g, d_s),
                   mine(t_ln_b, d_s), t_scale, t_bias, t_alog, t_dd]
    shapes = [w.shape for w in small_w]
    pk = [_pack(group, 8 * V7X_LANES)[None] for group in (small_w, small_m, small_v, small_grads)]
    res = _adamw("adamw_small", pk[0], pk[1], pk[2], [(pk[3], 0)], 0)
    upd = {}
    for name, vals in zip(small_names, zip(*[_unpack(r, shapes) for r in res])):
        upd[name] = list(vals)

    def update(tag, w, m, v, parts, layer, prev=None):
        own, recv = parts
        return _adamw(f"adamw_{tag}_{layer}", _as3d(w), _as3d(m), _as3d(v),
                      [(own, None), (recv, 0), (recv, 1), (recv, 2)], layer, prev, chip)

    q = collect("l1_ple", (res[0],))
    r_gate = update("ple_gate_w", ple_gate_w, m_ple_gate_w, v_ple_gate_w, q["gate"], 1)
    r_plew = update("ple_w", ple_w, m_ple_w, v_ple_w, q["plew"], 1)
    r_w2 = update("mlp_w2", mlp_w2, m_mlp_w2, v_mlp_w2, collect("l1_w2", (r_gate[0],))["w2"], 1)
    r_w1 = update("mlp_w1", mlp_w1, m_mlp_w1, v_mlp_w1, collect("l1_w1", (r_w2[0],))["w1"], 1)
    r_out = update("ssm_out_w", ssm_out_w, m_ssm_out_w, v_ssm_out_w, collect("ssm_out", (r_w1[0],))["out"], 0)
    r_in = update("ssm_in_w", ssm_in_w, m_ssm_in_w, v_ssm_in_w, collect("ssm_in", (r_out[0],))["in"], 0)
    q = collect("l0_ple", (r_in[0],))
    r_gate = update("ple_gate_w", ple_gate_w, m_ple_gate_w, v_ple_gate_w, q["gate"], 0, r_gate)
    r_plew = update("ple_w", ple_w, m_ple_w, v_ple_w, q["plew"], 0, r_plew)
    r_w2 = update("mlp_w2", mlp_w2, m_mlp_w2, v_mlp_w2, collect("l0_w2", (r_gate[0],))["w2"], 0, r_w2)
    r_w1 = update("mlp_w1", mlp_w1, m_mlp_w1, v_mlp_w1, collect("l0_w1", (r_w2[0],))["w1"], 0, r_w1)
    r_pool = update("pool_w", pool_w, m_pool_w, v_pool_w, collect("pool", (r_w1[0],))["pool"], 0)
    assert not scattering
    large = {"pool_w": (pool_w, r_pool), "ssm_in_w": (ssm_in_w, r_in), "ssm_out_w": (ssm_out_w, r_out),
             "mlp_w1": (mlp_w1, r_w1), "mlp_w2": (mlp_w2, r_w2), "ple_w": (ple_w, r_plew),
             "ple_gate_w": (ple_gate_w, r_gate)}
    for name, (w, rs) in large.items():
        upd[name] = [r.reshape(w.shape) for r in rs]

    order = ["pool_w", "pool_scale", "ssm_in_w", "ssm_conv_w", "ssm_conv_b", "ssm_dt_bias", "ssm_a_log", "ssm_d",
             "ssm_norm_w", "ssm_out_w", "mlp_w1", "mlp_w2", "ln_g", "ln_b", "ple_w", "ple_gate_w"]
    out = [loss, grad_x[None]]
    for k in range(4):
        out += [upd[name][k] for name in order]
    return tuple(out)
```

```python
import jax
import jax.numpy as jnp
from jax import lax
from jax.experimental import pallas as pl
from jax.experimental.pallas import tpu as pltpu

F32 = jnp.float32
BF16 = jnp.bfloat16
SDS = jax.ShapeDtypeStruct
MESH = pl.DeviceIdType.MESH
ANY = pl.BlockSpec(memory_space=pl.ANY)

N_DEV = 8
DEPTH = 2
ALPHA = (2.0 * DEPTH) ** 0.25
LN_EPS = 1e-5
RMS_EPS = 1e-5
POOL_WINDOW_LOG2 = (1, 2, 3, 4)
D_STATE = 128
CHUNK = 128
CONV_WIDTH = 4
ADAM_LR = 0.001
ADAM_B1 = 0.9
ADAM_B2 = 0.999
ADAM_EPS = 1e-08
ADAM_WD = 0.01
ADAM_STEP = 10

V7X_LANES = 128
V7X_VMEM_LIMIT = 48 * 1024 * 1024


def _cp(*sem):
    return pltpu.CompilerParams(dimension_semantics=sem, vmem_limit_bytes=V7X_VMEM_LIMIT)


def _pick(dim, cap):
    if dim <= cap:
        return dim
    best = None
    for t in range(V7X_LANES, cap + 1, V7X_LANES):
        if dim % t == 0:
            best = t
    assert best is not None, (dim, cap)
    return best


def _row_tile(rows, cols, itemsize=4, target=1 << 20):
    t = rows
    while t % 2 == 0 and t // 2 >= 16 and (t // 2) % 16 == 0 and t * cols * itemsize > target:
        t //= 2
    return t


def _slot(s):
    return (s % 2) * 4 + s // 2


def _all_gather(name, shards):
    n = len(shards)

    def body(*refs):
        ins, outs = refs[:n], refs[n:2 * n]
        send_sems, recv_sems, local_sems = refs[2 * n:]
        x, y, c = lax.axis_index("x"), lax.axis_index("y"), lax.axis_index("c")
        me, sibling = (x, y, c), (x, y, 1 - c)
        chips = [(1 - x, y), (x, 1 - y), (1 - x, 1 - y)]

        def copy(a, k, block, to, src=None):
            dst = outs[a].at[4 * block[0] + 2 * block[1] + block[2]]
            return pltpu.make_async_remote_copy(
                src_ref=dst if src is None else src, dst_ref=dst, send_sem=send_sems.at[a, k],
                recv_sem=recv_sems.at[a, k], device_id=to, device_id_type=MESH)

        mine = [pltpu.make_async_copy(ins[a], outs[a].at[4 * x + 2 * y + c], local_sems.at[a]) for a in range(n)]
        for cp in mine:
            cp.start()
        first = []
        for a in range(n):
            first.append(copy(a, 0, me, sibling, src=ins[a]))
            first += [copy(a, 1 + j, me, (*chip, c), src=ins[a]) for j, chip in enumerate(chips)]
        for cp in first:
            cp.start()
        passed = []
        for j, chip in enumerate(chips):
            for a in range(n):
                copy(a, 1 + j, (*chip, c), me).wait_recv()
                fwd = copy(a, 4 + j, (*chip, c), sibling)
                fwd.start()
                passed.append(fwd)
        for a in range(n):
            copy(a, 0, sibling, me).wait_recv()
            for j, chip in enumerate(chips):
                copy(a, 4 + j, (*chip, 1 - c), me).wait_recv()
        for cp in first + passed:
            cp.wait_send()
        for cp in mine:
            cp.wait()

    return pl.pallas_call(
        body, name=name,
        out_shape=[SDS((N_DEV,) + s.shape, s.dtype) for s in shards],
        in_specs=[ANY] * n, out_specs=[ANY] * n,
        scratch_shapes=[pltpu.SemaphoreType.DMA((n, 7)), pltpu.SemaphoreType.DMA((n, 7)),
                        pltpu.SemaphoreType.DMA((n,))],
    )(*shards)


def _rs_sibling_exchange(name, grads):
    n = len(grads)

    def body(*refs):
        ins, outs = refs[:n], refs[n:2 * n]
        send_sems, recv_sems = refs[2 * n:]
        x, y, c = lax.axis_index("x"), lax.axis_index("y"), lax.axis_index("c")
        remote = [pltpu.make_async_remote_copy(
            src_ref=ins[a].at[pl.ds(4 * (1 - c), 4)], dst_ref=outs[a], send_sem=send_sems.at[a],
            recv_sem=recv_sems.at[a], device_id=(x, y, 1 - c), device_id_type=MESH) for a in range(n)]
        for cp in remote:
            cp.start()
        for cp in remote:
            cp.wait()

    return pl.pallas_call(
        body, name=name,
        out_shape=[SDS((4,) + g.shape[1:], g.dtype) for g in grads],
        in_specs=[ANY] * n, out_specs=[ANY] * n,
        scratch_shapes=[pltpu.SemaphoreType.DMA((n,)), pltpu.SemaphoreType.DMA((n,))],
    )(*grads)


HBM_SPEC = pl.BlockSpec(memory_space=pltpu.HBM)
SEM_SPEC = pl.BlockSpec(memory_space=pltpu.SEMAPHORE)
EFFECT = pltpu.SideEffectType.DATAFLOW_SIDE_EFFECTING


def _ag_first_copies(ins, lands, send_sems, recv_sems):
    x, y, c = lax.axis_index("x"), lax.axis_index("y"), lax.axis_index("c")
    targets = [(x, y, 1 - c), (1 - x, y, c), (x, 1 - y, c), (1 - x, 1 - y, c)]
    return [pltpu.make_async_remote_copy(
        src_ref=ins[a], dst_ref=lands[a].at[4 * x + 2 * y + c], send_sem=send_sems.at[4 * a + k],
        recv_sem=recv_sems.at[4 * a + k], device_id=to, device_id_type=MESH)
        for a in range(len(ins)) for k, to in enumerate(targets)]


def _ag_forward_copies(ins, lands, send_sems, recv_sems):
    x, y, c = lax.axis_index("x"), lax.axis_index("y"), lax.axis_index("c")
    cps = []
    for a in range(len(lands)):
        for j, (px, py) in enumerate([(1 - x, y), (x, 1 - y), (1 - x, 1 - y)]):
            blk = lands[a].at[4 * px + 2 * py + c]
            cps.append(pltpu.make_async_remote_copy(
                src_ref=blk, dst_ref=blk, send_sem=send_sems.at[3 * a + j], recv_sem=recv_sems.at[3 * a + j],
                device_id=(x, y, 1 - c), device_id_type=MESH))
    return cps


def _rs_chip_copies(ins, lands, send_sems, recv_sems):
    x, y, c = lax.axis_index("x"), lax.axis_index("y"), lax.axis_index("c")
    cps = []
    for a in range(len(ins)):
        for j, (px, py) in enumerate([(1 - x, y), (x, 1 - y), (1 - x, 1 - y)]):
            cps.append(pltpu.make_async_remote_copy(
                src_ref=ins[a].at[2 * px + py], dst_ref=lands[a].at[j], send_sem=send_sems.at[3 * a + j],
                recv_sem=recv_sems.at[3 * a + j], device_id=(px, py, c), device_id_type=MESH))
    return cps


def _async_start(name, build, sem_shape, ins, lands, after=()):
    arrays = [*ins, *lands]
    n_i, n_t, n_a = len(ins), len(arrays), len(after)

    def body(*refs):
        outs = refs[n_t + n_a:]
        for cp in build(refs[:n_i], refs[n_i:n_t], outs[0], outs[1]):
            cp.start()
        outs[-1][...] = jnp.zeros_like(outs[-1])

    res = pl.pallas_call(
        body, name=name,
        out_shape=(pltpu.SemaphoreType.DMA(sem_shape), pltpu.SemaphoreType.DMA(sem_shape),
                   *[pltpu.HBM(a.shape, a.dtype) for a in arrays], SDS((8, V7X_LANES), F32)),
        in_specs=[HBM_SPEC] * n_t + [ANY] * n_a,
        out_specs=(SEM_SPEC, SEM_SPEC, *[HBM_SPEC] * n_t, pl.BlockSpec(memory_space=pltpu.VMEM)),
        input_output_aliases={i: 2 + i for i in range(n_t)},
        compiler_params=pltpu.CompilerParams(has_side_effects=EFFECT),
    )(*[pltpu.with_memory_space_constraint(a, pltpu.HBM) for a in arrays], *after)
    return res[0], res[1], list(res[2:2 + n_t]), res[-1]


def _async_wait(name, build, handle, n_i, after=()):
    send_sems, recv_sems, arrays, _ = handle
    n_t, n_a = len(arrays), len(after)

    def body(*refs):
        for cp in build(refs[:n_i], refs[n_i:n_t], refs[n_t], refs[n_t + 1]):
            cp.wait_send()
            cp.wait_recv()

    res = pl.pallas_call(
        body, name=name, out_shape=tuple(pltpu.HBM(a.shape, a.dtype) for a in arrays),
        in_specs=[HBM_SPEC] * n_t + [SEM_SPEC, SEM_SPEC] + [ANY] * n_a, out_specs=tuple([HBM_SPEC] * n_t),
        input_output_aliases={i: i for i in range(n_t)},
        compiler_params=pltpu.CompilerParams(has_side_effects=EFFECT),
    )(*arrays, send_sems, recv_sems, *after)
    return list(res[:n_i]), list(res[n_i:])


def _token_add(a, *tokens):
    for tok in tokens:
        a = a + tok[0:1, 0:1].reshape((1,) * a.ndim)
    return a


def _mm_core(name, a, b, *, grid, a_spec, b_spec, dims, acc_shape, outs, out_spec, epilogue=None, extras=(),
             extra_specs=(), a_fn=None, after=()):
    nk = grid[2]
    ne, no, na = len(extras), len(outs), len(after)

    def body(a_ref, b_ref, *rest):
        e_refs, o_refs, acc = rest[:ne], rest[ne + na:ne + na + no], rest[ne + na + no]
        k = pl.program_id(2)
        lhs = a_ref[...] if a_fn is None else a_fn(a_ref[...])
        part = lax.dot_general(lhs.astype(BF16), b_ref[...].astype(BF16), dims, preferred_element_type=F32)

        @pl.when(k == 0)
        def _():
            acc[...] = part

        @pl.when(k > 0)
        def _():
            acc[...] += part

        @pl.when(k == nk - 1)
        def _():
            r = acc[...]
            vals = epilogue(r, *[e[...] for e in e_refs]) if epilogue is not None else (r,)
            for o, v in zip(o_refs, vals):
                o[...] = v.astype(o.dtype)

    res = pl.pallas_call(
        body, name=name, grid=grid, out_shape=list(outs),
        in_specs=[a_spec, b_spec, *extra_specs, *[ANY] * na], out_specs=[out_spec] * no,
        scratch_shapes=[pltpu.VMEM(acc_shape, F32)],
        compiler_params=_cp("parallel", "parallel", "arbitrary"),
    )(a, b, *extras, *after)
    return res


NN = (((1,), (0,)), ((), ()))
NT = (((1,), (1,)), ((), ()))
TN = (((0,), (0,)), ((), ()))


def _w_dims(w, kind):
    if kind == "col":
        return w.shape[1], N_DEV * w.shape[2], w.shape[1], w.shape[2]
    if kind == "row":
        return N_DEV * w.shape[1], w.shape[2], w.shape[1], w.shape[2]
    return w.shape[0], w.shape[1], w.shape[0], w.shape[1]


MM_VMEM_BUDGET = 36 * 1024 * 1024


def _row_block(m, tn, tk, a, out_dtypes, extras):
    for tm in (_pick(m, 2048), _pick(m, 1024), _pick(m, 512)):
        per_out = sum(jnp.dtype(dt).itemsize for dt in out_dtypes) + sum(e.dtype.itemsize for e in extras)
        used = tm * tn * (4 + 2 * per_out) + 2 * (tm * tk * a.dtype.itemsize + tk * tn * 2)
        if used <= MM_VMEM_BUDGET:
            return tm
    return tm


def _mm_fwd(name, a, w, kind, out_dtypes, epilogue=None, extras=(), a_fn=None, after=()):
    if kind == "row":
        w, kind = w.reshape(-1, w.shape[-1]), "plain"
    m, k_dim = a.shape
    kk, n, ks, ns = _w_dims(w, kind)
    assert kk == k_dim
    if kind == "col":
        tn, tk = _pick(ns, 1024), _pick(kk, 512)
        nb = ns // tn
        b_spec = pl.BlockSpec((None, tk, tn), lambda i, j, k: (j // nb, k, j % nb))
    else:
        tn, tk = _pick(n, 1152), _pick(kk, 512)
        b_spec = pl.BlockSpec((tk, tn), lambda i, j, k: (k, j))
    tm = _row_block(m, tn, tk, a, out_dtypes, extras)
    mn_spec = pl.BlockSpec((tm, tn), lambda i, j, k: (i, j))
    return _mm_core(
        name, a, w, grid=(m // tm, n // tn, kk // tk),
        a_spec=pl.BlockSpec((tm, tk), lambda i, j, k: (i, k)), b_spec=b_spec, dims=NN, acc_shape=(tm, tn),
        outs=[SDS((m, n), dt) for dt in out_dtypes], out_spec=mn_spec, epilogue=epilogue, extras=extras,
        extra_specs=[mn_spec] * len(extras), a_fn=a_fn, after=after)


def _mm_dx(name, dy, w, kind, out_dtypes, epilogue=None, extras=(), after=()):
    if kind == "row":
        w, kind = w.reshape(-1, w.shape[-1]), "plain"
    m, n_dim = dy.shape
    kk, n, ks, ns = _w_dims(w, kind)
    assert n == n_dim
    if kind == "col":
        tn, tk = _pick(kk, 1024), _pick(ns, 512)
        kb = ns // tk
        b_spec = pl.BlockSpec((None, tn, tk), lambda i, j, k: (k // kb, j, k % kb))
    else:
        tn, tk = _pick(kk, 1024), _pick(n, 1152)
        b_spec = pl.BlockSpec((tn, tk), lambda i, j, k: (j, k))
    tm = _row_block(m, tn, tk, dy, out_dtypes, extras)
    mk_spec = pl.BlockSpec((tm, tn), lambda i, j, k: (i, j))
    return _mm_core(
        name, dy, w, grid=(m // tm, kk // tn, n // tk),
        a_spec=pl.BlockSpec((tm, tk), lambda i, j, k: (i, k)), b_spec=b_spec, dims=NT, acc_shape=(tm, tn),
        outs=[SDS((m, kk), dt) for dt in out_dtypes], out_spec=mk_spec, epilogue=epilogue, extras=extras,
        extra_specs=[mk_spec] * len(extras), after=after)


def _mm_dw(name, a, dy, kind, a_fn=None):
    m, kk = a.shape
    n = dy.shape[1]
    tk = _pick(m, 512)
    if kind == "col":
        ns = n // N_DEV
        tm, tn = _pick(kk, 1024), _pick(ns, 1024)
        nb = ns // tn
        out = SDS((N_DEV, kk, ns), BF16)
        out_spec = pl.BlockSpec((None, tm, tn), lambda i, j, k: (_slot(j // nb), i, j % nb))
    elif kind == "row":
        ks = kk // N_DEV
        tm, tn = _pick(ks, 1024), _pick(n, 1024)
        mb = ks // tm
        out = SDS((N_DEV, ks, n), BF16)
        out_spec = pl.BlockSpec((None, tm, tn), lambda i, j, k: (_slot(i // mb), i % mb, j))
    else:
        tm, tn = _pick(kk, 1024), _pick(n, 1152)
        out = SDS((kk, n), BF16)
        out_spec = pl.BlockSpec((tm, tn), lambda i, j, k: (i, j))
    return _mm_core(
        name, a, dy, grid=(kk // tm, n // tn, m // tk),
        a_spec=pl.BlockSpec((tk, tm), lambda i, j, k: (k, i)), b_spec=pl.BlockSpec((tk, tn), lambda i, j, k: (k, j)),
        dims=TN, acc_shape=(tm, tn), outs=[out], out_spec=out_spec, a_fn=a_fn)[0]


def _rowwise(name, fn, ins, outs, rows, tile):
    arrays, specs = [], []
    for arr, kind in ins:
        arrays.append(arr)
        if kind == "row":
            specs.append(pl.BlockSpec((tile, arr.shape[1]), lambda i: (i, 0)))
        elif kind == "vec":
            specs.append(pl.BlockSpec(arr.shape, lambda i, nd=arr.ndim: (0,) * nd))
        else:
            specs.append(kind)
    out_shapes, out_specs, kinds = [], [], []
    for cols, dt, kind in outs:
        kinds.append(kind)
        if kind == "row":
            out_shapes.append(SDS((rows, cols), dt))
            out_specs.append(pl.BlockSpec((tile, cols), lambda i: (i, 0)))
        else:
            out_shapes.append(SDS((1, cols), F32))
            out_specs.append(pl.BlockSpec((1, cols), lambda i: (0, 0)))
    ni = len(arrays)
    has_acc = "acc" in kinds

    def body(*refs):
        vals = fn(*[r[...] for r in refs[:ni]])
        i = pl.program_id(0)
        for o, v, kind in zip(refs[ni:], vals, kinds):
            if kind == "row":
                o[...] = v.astype(o.dtype)
            else:
                @pl.when(i == 0)
                def _(o=o):
                    o[...] = jnp.zeros_like(o)

                o[...] += v

    return pl.pallas_call(
        body, name=name, grid=(rows // tile,), out_shape=out_shapes, in_specs=specs, out_specs=out_specs,
        compiler_params=_cp("arbitrary" if has_acc else "parallel"),
    )(*arrays)


def _ln_fwd(name, u, g, b):
    d = u.shape[1]

    def fn(u, g, b):
        mu = jnp.mean(u, axis=1, keepdims=True)
        xc = u - mu
        var = jnp.mean(xc * xc, axis=1, keepdims=True)
        y = xc * lax.rsqrt(var + LN_EPS) * g + b
        return y, y

    return _rowwise(name, fn, [(u, "row"), (g, "vec"), (b, "vec")], [(d, F32, "row"), (d, BF16, "row")], u.shape[0], 256)


def _ln_bwd(name, u, dy, g):
    d = u.shape[1]

    def fn(u, dy, g):
        mu = jnp.mean(u, axis=1, keepdims=True)
        xc = u - mu
        var = jnp.mean(xc * xc, axis=1, keepdims=True)
        rstd = lax.rsqrt(var + LN_EPS)
        xhat = xc * rstd
        dxhat = dy * g
        m1 = jnp.mean(dxhat, axis=1, keepdims=True)
        m2 = jnp.mean(dxhat * xhat, axis=1, keepdims=True)
        du = rstd * (dxhat - m1 - xhat * m2)
        return du, du, jnp.sum(dy * xhat, axis=0, keepdims=True), jnp.sum(dy, axis=0, keepdims=True)

    return _rowwise(name, fn, [(u, "row"), (dy, "row"), (g, "vec")],
                    [(d, F32, "row"), (d, BF16, "row"), (d, F32, "acc"), (d, F32, "acc")], u.shape[0], 256)


def _loss_bwd(name, y, target):
    d = y.shape[1]

    def fn(y, t):
        e = y - t
        return e * (1.0 / d), jnp.sum(e * e, axis=0, keepdims=True) * (0.5 / d)

    return _rowwise(name, fn, [(y, "row"), (target, "row")], [(d, F32, "row"), (d, F32, "acc")], y.shape[0], 256)


def _ple_bwd(name, dx, e, gate):
    d = dx.shape[1]

    def fn(dx, e, gate):
        return dx * e * gate * (1.0 - gate), dx * gate

    return _rowwise(name, fn, [(dx, "row"), (e, "row"), (gate, "row")], [(d, BF16, "row"), (d, BF16, "row")],
                    dx.shape[0], 256)


def _sigmoid(v):
    return 1.0 / (1.0 + jnp.exp(-v))


def _gated_rms_fwd(name, y, zx, norm_w):
    di = y.shape[1]

    def fn(y, z, w):
        yg = y * (z * _sigmoid(z))
        r = lax.rsqrt(jnp.mean(yg * yg, axis=1, keepdims=True) + RMS_EPS)
        return (yg * r * w,)

    z_spec = pl.BlockSpec((128, di), lambda i: (i, 0))
    return _rowwise(name, fn, [(y, "row"), (zx, z_spec), (norm_w, "vec")], [(di, BF16, "row")], y.shape[0], 128)[0]


def _gated_rms_bwd(name, y, zx, norm_w, dout):
    di = y.shape[1]

    def fn(y, z, w, dout):
        sg = _sigmoid(z)
        sz = z * sg
        yg = y * sz
        r = lax.rsqrt(jnp.mean(yg * yg, axis=1, keepdims=True) + RMS_EPS)
        dn = dout * w
        dyg = r * (dn - yg * (r * r) * jnp.mean(dn * yg, axis=1, keepdims=True))
        dy = dyg * sz
        dz = dyg * y * (sg * (1.0 + z * (1.0 - sg)))
        return dy, dz, jnp.sum(dout * yg * r, axis=0, keepdims=True)

    z_spec = pl.BlockSpec((128, di), lambda i: (i, 0))
    return _rowwise(name, fn, [(y, "row"), (zx, z_spec), (norm_w, "vec"), (dout, "row")],
                    [(di, F32, "row"), (di, BF16, "row"), (di, F32, "acc")], y.shape[0], 128)


def _shift_down(v, j, row):
    return jnp.where(row >= j, pltpu.roll(v, j, 0), 0.0)


def _shift_up(v, j, row):
    t = v.shape[0]
    return jnp.where(row < t - j, pltpu.roll(v, t - j, 0), 0.0)


def _pool_select(parts, g):
    return jnp.where(g == 0, parts[0], jnp.where(g == 1, parts[1], jnp.where(g == 2, parts[2], parts[3])))


def _pool_windows(name, x, transpose, scale_by=None):
    t, d = x.shape
    cg = d // 4
    cw = V7X_LANES
    per = cg // cw

    def body(*refs):
        x_ref, o_ref = refs[0], refs[-1]
        g = pl.program_id(0) // per
        xv = x_ref[...]
        row = lax.broadcasted_iota(jnp.int32, (t, 1), 0)
        cnt = jnp.minimum(row + 1, jnp.left_shift(2, g)).astype(F32)
        s = xv / cnt if transpose else xv
        parts = []
        for lg in POOL_WINDOW_LOG2:
            j = 1 << (lg - 1)
            s = s + (_shift_up(s, j, row) if transpose else _shift_down(s, j, row))
            parts.append(s)
        sel = _pool_select(parts, g)
        if transpose:
            o_ref[...] = ALPHA * refs[1][...] + sel - xv
        else:
            o_ref[...] = (sel / cnt - xv).astype(o_ref.dtype)

    col = pl.BlockSpec((t, cw), lambda j: (0, j))
    ins = [x] if scale_by is None else [x, scale_by]
    return pl.pallas_call(
        body, name=name, grid=(d // cw,), out_shape=SDS((t, d), F32 if transpose else BF16),
        in_specs=[col] * len(ins), out_specs=col, compiler_params=_cp("parallel"),
    )(*ins)


def _pool_mm(name, pooled, w, scale, x):
    t, d = x.shape
    cg = d // 4
    tm = _pick(t, 1024)

    def body(p_ref, w_ref, s_ref, x_ref, u_ref, h_ref):
        h = jnp.dot(p_ref[...], w_ref[...], preferred_element_type=F32)
        h_ref[...] = h
        u_ref[...] = ALPHA * x_ref[...] + h * s_ref[...]

    blk = pl.BlockSpec((tm, cg), lambda g, i: (i, g))
    return pl.pallas_call(
        body, name=name, grid=(4, t // tm), out_shape=[SDS((t, d), F32), SDS((t, d), F32)],
        in_specs=[blk, pl.BlockSpec((None, cg, cg), lambda g, i: (g, 0, 0)), pl.BlockSpec((1, cg), lambda g, i: (0, g)),
                  blk],
        out_specs=[blk, blk], compiler_params=_cp("parallel", "parallel"),
    )(pooled, w, scale, x)


def _pool_bwd_mm(name, du, hraw, w, scale):
    t, d = du.shape
    cg = d // 4
    tm = _pick(t, 1024)

    def body(du_ref, h_ref, w_ref, s_ref, dh_ref, dp_ref, ds_ref):
        @pl.when(pl.program_id(1) == 0)
        def _():
            ds_ref[...] = jnp.zeros_like(ds_ref)

        duv = du_ref[...]
        ds_ref[...] += jnp.sum(duv * h_ref[...], axis=0, keepdims=True)
        dh = (duv * s_ref[...]).astype(BF16)
        dh_ref[...] = dh
        dp_ref[...] = lax.dot_general(dh, w_ref[...], NT, preferred_element_type=F32)

    blk = pl.BlockSpec((tm, cg), lambda g, i: (i, g))
    vec = pl.BlockSpec((1, cg), lambda g, i: (0, g))
    return pl.pallas_call(
        body, name=name, grid=(4, t // tm), out_shape=[SDS((t, d), BF16), SDS((t, d), F32), SDS((1, d), F32)],
        in_specs=[blk, blk, pl.BlockSpec((None, cg, cg), lambda g, i: (g, 0, 0)), vec],
        out_specs=[blk, blk, vec], compiler_params=_cp("parallel", "arbitrary"),
    )(du, hraw, w, scale)


def _pool_dw(name, pooled, dh):
    t, d = pooled.shape
    cg = d // 4
    tk = _pick(t, 512)
    nk = t // tk

    def body(p_ref, dh_ref, o_ref, acc):
        k = pl.program_id(1)

        @pl.when(k == 0)
        def _():
            acc[...] = jnp.zeros_like(acc)

        acc[...] += lax.dot_general(p_ref[...], dh_ref[...], TN, preferred_element_type=F32)

        @pl.when(k == nk - 1)
        def _():
            o_ref[...] = acc[...].astype(o_ref.dtype)

    blk = pl.BlockSpec((tk, cg), lambda g, k: (k, g))
    return pl.pallas_call(
        body, name=name, grid=(4, nk), out_shape=SDS((4, cg, cg), BF16), in_specs=[blk, blk],
        out_specs=pl.BlockSpec((None, cg, cg), lambda g, k: (g, 0, 0)), scratch_shapes=[pltpu.VMEM((cg, cg), F32)],
        compiler_params=_cp("parallel", "arbitrary"),
    )(pooled, dh)


def _conv_pre(u, w_ref, b_ref, row):
    pre = b_ref[...] + _shift_down(u, 3, row) * w_ref[0:1, :]
    pre = pre + _shift_down(u, 2, row) * w_ref[1:2, :]
    pre = pre + _shift_down(u, 1, row) * w_ref[2:3, :]
    return pre + u * w_ref[3:4, :]


def _conv_fwd(name, zx, conv_w, conv_b, di):
    t = zx.shape[0]
    cd = conv_w.shape[1]
    cw = _pick(cd, 256)
    off = di // cw

    def body(u_ref, w_ref, b_ref, o_ref):
        row = lax.broadcasted_iota(jnp.int32, (t, 1), 0)
        pre = _conv_pre(u_ref[...], w_ref, b_ref, row)
        o_ref[...] = pre * _sigmoid(pre)

    return pl.pallas_call(
        body, name=name, grid=(cd // cw,), out_shape=SDS((t, cd), F32),
        in_specs=[pl.BlockSpec((t, cw), lambda j: (0, off + j)), pl.BlockSpec((CONV_WIDTH, cw), lambda j: (0, j)),
                  pl.BlockSpec((1, cw), lambda j: (0, j))],
        out_specs=pl.BlockSpec((t, cw), lambda j: (0, j)), compiler_params=_cp("parallel"),
    )(zx, conv_w, conv_b)


def _conv_bwd(name, zx, conv_w, conv_b, dact, di, first):
    t, cd = dact.shape
    cw = _pick(cd, 256)
    off, woff = (di + first) // cw, first // cw

    def body(u_ref, w_ref, b_ref, da_ref, du_ref, dw_ref, db_ref):
        row = lax.broadcasted_iota(jnp.int32, (t, 1), 0)
        u = u_ref[...]
        pre = _conv_pre(u, w_ref, b_ref, row)
        sg = _sigmoid(pre)
        dpre = da_ref[...] * (sg * (1.0 + pre * (1.0 - sg)))
        du = dpre * w_ref[3:4, :]
        for j in (1, 2, 3):
            du = du + _shift_up(dpre, j, row) * w_ref[3 - j:4 - j, :]
            dw_ref[3 - j:4 - j, :] = jnp.sum(dpre * _shift_down(u, j, row), axis=0, keepdims=True)
        dw_ref[3:4, :] = jnp.sum(dpre * u, axis=0, keepdims=True)
        db_ref[...] = jnp.sum(dpre, axis=0, keepdims=True)
        du_ref[...] = du.astype(du_ref.dtype)

    wspec = pl.BlockSpec((CONV_WIDTH, cw), lambda j: (0, j))
    bspec = pl.BlockSpec((1, cw), lambda j: (0, j))
    ospec = pl.BlockSpec((t, cw), lambda j: (0, j))
    return pl.pallas_call(
        body, name=name, grid=(cd // cw,), out_shape=[SDS((t, cd), BF16), SDS((CONV_WIDTH, cd), F32), SDS((1, cd), F32)],
        in_specs=[pl.BlockSpec((t, cw), lambda j: (0, off + j)), pl.BlockSpec((CONV_WIDTH, cw), lambda j: (0, woff + j)),
                  pl.BlockSpec((1, cw), lambda j: (0, woff + j)), ospec],
        out_specs=[ospec, wspec, bspec], compiler_params=_cp("parallel"),
    )(zx, conv_w, conv_b, dact)


def _expand_heads(name, arrays, h_n, p):
    t = arrays[0].shape[0]
    n = len(arrays)
    w = _pick(h_n * p, 512)

    def body(*refs):
        j = pl.program_id(0)
        head = lax.broadcasted_iota(jnp.int32, (V7X_LANES, w), 0)
        lane = lax.broadcasted_iota(jnp.int32, (V7X_LANES, w), 1)
        spread = (head == j * (w // p) + lane // p).astype(BF16)
        for a_ref, o_ref in zip(refs[:n], refs[n:]):
            rest = a_ref[...]
            out = jnp.zeros((t, w), F32)
            for _ in range(3):
                piece = rest.astype(BF16)
                out = out + jnp.dot(piece, spread, preferred_element_type=F32)
                rest = rest - piece.astype(F32)
            o_ref[...] = out

    full = pl.BlockSpec((t, V7X_LANES), lambda j: (0, 0))
    return pl.pallas_call(
        body, name=name, grid=(h_n * p // w,), out_shape=[SDS((t, h_n * p), F32)] * n, in_specs=[full] * n,
        out_specs=[pl.BlockSpec((t, w), lambda j: (0, j))] * n, compiler_params=_cp("parallel"),
    )(*arrays)


def _softplus(v):
    return jnp.maximum(v, 0.0) + jnp.log(1.0 + jnp.exp(-jnp.abs(v)))


def _dt_fwd(name, zx, bias, a_log, col_block):
    t = zx.shape[0]

    def body(r_ref, b_ref, al_ref, dt_ref, acs_ref, ein_ref, eout_ref):
        row = lax.broadcasted_iota(jnp.int32, (t, 1), 0) % CHUNK
        dt = _softplus(r_ref[...] + b_ref[...])
        da = dt * (-jnp.exp(al_ref[...]))
        s, r = da, da
        j = 1
        while j < CHUNK:
            s = s + jnp.where(row >= j, pltpu.roll(s, j, 0), 0.0)
            r = r + jnp.where(row < CHUNK - j, pltpu.roll(r, t - j, 0), 0.0)
            j *= 2
        dt_ref[...] = dt
        acs_ref[...] = s
        ein_ref[...] = jnp.exp(s)
        eout_ref[...] = jnp.exp(r - da)

    vec = pl.BlockSpec((1, V7X_LANES), lambda i: (0, 0))
    full = pl.BlockSpec((t, V7X_LANES), lambda i: (0, 0))
    return pl.pallas_call(
        body, name=name, grid=(1,), out_shape=[SDS((t, V7X_LANES), F32)] * 4,
        in_specs=[pl.BlockSpec((t, V7X_LANES), lambda i: (0, col_block)), vec, vec], out_specs=[full] * 4,
        compiler_params=_cp("arbitrary"),
    )(zx, bias, a_log)


def _dt_bwd(name, zx, bias, a_log, d_acs, d_dt, col_block):
    t = zx.shape[0]

    def body(r_ref, b_ref, al_ref, da_ref, dd_ref, draw_ref, db_ref, dal_ref):
        row = lax.broadcasted_iota(jnp.int32, (t, 1), 0) % CHUNK
        pre = r_ref[...] + b_ref[...]
        dt = _softplus(pre)
        a = -jnp.exp(al_ref[...])
        s = da_ref[...]
        j = 1
        while j < CHUNK:
            s = s + jnp.where(row < CHUNK - j, pltpu.roll(s, t - j, 0), 0.0)
            j *= 2
        ddt = dd_ref[...] + s * a
        dal_ref[...] = jnp.sum(s * dt, axis=0, keepdims=True) * a
        draw = ddt * _sigmoid(pre)
        db_ref[...] = jnp.sum(draw, axis=0, keepdims=True)
        draw_ref[...] = draw.astype(draw_ref.dtype)

    vec = pl.BlockSpec((1, V7X_LANES), lambda i: (0, 0))
    full = pl.BlockSpec((t, V7X_LANES), lambda i: (0, 0))
    return pl.pallas_call(
        body, name=name, grid=(1,), out_shape=[SDS((t, V7X_LANES), BF16), SDS((1, V7X_LANES), F32), SDS((1, V7X_LANES), F32)],
        in_specs=[pl.BlockSpec((t, V7X_LANES), lambda i: (0, col_block)), vec, vec, full, full],
        out_specs=[full, vec, vec], compiler_params=_cp("arbitrary"),
    )(zx, bias, a_log, d_acs, d_dt)


def _ssd_specs(t, di, g_n, hpg, p, rev):
    nc = t // CHUNK
    w = hpg * p
    nb = di // D_STATE

    def cc(c):
        return nc - 1 - c if rev else c

    return dict(
        xs=pl.BlockSpec((CHUNK, w), lambda g, c: (cc(c), g)),
        bm=pl.BlockSpec((CHUNK, D_STATE), lambda g, c: (cc(c), nb + g)),
        cm=pl.BlockSpec((CHUNK, D_STATE), lambda g, c: (cc(c), nb + g_n + g)),
        col=pl.BlockSpec((None, CHUNK, hpg), lambda g, c: (g, cc(c), 0)),
        rowv=pl.BlockSpec((None, hpg, CHUNK), lambda g, c: (g, 0, cc(c))),
        head=pl.BlockSpec((None, 1, hpg), lambda g, c: (g, 0, 0)),
        lanes=pl.BlockSpec((1, w), lambda g, c: (0, g)),
        bc=pl.BlockSpec((CHUNK, D_STATE), lambda g, c: (cc(c), g)),
        prev=pl.BlockSpec((None, None, D_STATE, w), lambda g, c: (cc(c), g, 0, 0)),
        seg=pl.BlockSpec((w, V7X_LANES), lambda g, c: (0, 0)),
    )


def _decay_masks(cb, ac_ref, ar_ref, heads):
    li = lax.broadcasted_iota(jnp.int32, (CHUNK, CHUNK), 0)
    si = lax.broadcasted_iota(jnp.int32, (CHUNK, CHUNK), 1)
    lms = [jnp.exp(jnp.where(li >= si, ac_ref[:, hh:hh + 1] - ar_ref[hh:hh + 1, :], -jnp.inf)) for hh in heads]
    return lms, [(cb * lm).astype(BF16) for lm in lms]


def _ssd_fwd(name, xbc, dt_x, ein_x, eout_x, a_col, a_row, d_x, di, g_n, hpg, p):
    t = xbc.shape[0]
    nc = t // CHUNK
    w = hpg * p
    assert 2 * p == V7X_LANES and hpg % 2 == 0
    sp = _ssd_specs(t, di, g_n, hpg, p, False)

    def body(xs_ref, bm_ref, cm_ref, dt_ref, ein_ref, eout_ref, ac_ref, ar_ref, d_ref, y_ref, prev_ref, h_ref):
        @pl.when(pl.program_id(1) == 0)
        def _():
            h_ref[...] = jnp.zeros_like(h_ref)

        bm = bm_ref[...].astype(BF16)
        cm = cm_ref[...].astype(BF16)
        cb = lax.dot_general(cm, bm, NT, preferred_element_type=F32)
        first = lax.broadcasted_iota(jnp.int32, (1, V7X_LANES), 1) < p
        xs = xs_ref[...]
        e_in = ein_ref[...]
        xdt = xs * dt_ref[...]
        ys = []
        for pr in range(hpg // 2):
            _, ms = _decay_masks(cb, ac_ref, ar_ref, (2 * pr, 2 * pr + 1))
            xp = xdt[:, pr * V7X_LANES:(pr + 1) * V7X_LANES]
            rhs = jnp.concatenate([jnp.where(first, xp, 0.0), jnp.where(first, 0.0, xp)], axis=0).astype(BF16)
            ys.append(jnp.dot(jnp.concatenate(ms, axis=1), rhs, preferred_element_type=F32))
        h_prev = h_ref[...]
        prev_ref[...] = h_prev
        y = jnp.concatenate(ys, axis=1) + jnp.dot(cm, h_prev.astype(BF16), preferred_element_type=F32) * e_in
        y_ref[...] = y + xs * d_ref[...]
        st = lax.dot_general(bm, (xdt * eout_ref[...]).astype(BF16), TN, preferred_element_type=F32)
        h_ref[...] = e_in[CHUNK - 1:CHUNK, :] * h_prev + st

    return pl.pallas_call(
        body, name=name, grid=(g_n, nc),
        out_shape=[SDS((t, di), F32), SDS((nc, g_n, D_STATE, w), F32)],
        in_specs=[sp["xs"], sp["bm"], sp["cm"], sp["xs"], sp["xs"], sp["xs"], sp["col"], sp["rowv"], sp["lanes"]],
        out_specs=[sp["xs"], sp["prev"]], scratch_shapes=[pltpu.VMEM((D_STATE, w), F32)],
        compiler_params=_cp("parallel", "arbitrary"),
    )(xbc, xbc, xbc, dt_x, ein_x, eout_x, a_col, a_row, d_x)


def _head_sums(v, seg):
    hi = v.astype(BF16)
    lo = (v - hi.astype(F32)).astype(BF16)
    return jnp.dot(hi, seg, preferred_element_type=F32) + jnp.dot(lo, seg, preferred_element_type=F32)


def _head_totals(v, seg):
    part = v[0:8]
    for r in range(8, v.shape[0], 8):
        part = part + v[r:r + 8]
    return jnp.sum(_head_sums(part, seg), axis=0, keepdims=True)


def _ssd_bwd(name, xbc, dt_x, ein_x, eout_x, a_col, a_row, d_x, prev, dy, di, g_n, hpg, p):
    t = xbc.shape[0]
    nc = t // CHUNK
    w = hpg * p
    sp = _ssd_specs(t, di, g_n, hpg, p, True)
    seg = (lax.broadcasted_iota(jnp.int32, (w, V7X_LANES), 0) // p
           == lax.broadcasted_iota(jnp.int32, (w, V7X_LANES), 1)).astype(BF16)

    def body(xs_ref, bm_ref, cm_ref, dt_ref, ein_ref, eout_ref, ac_ref, ar_ref, d_ref, prev_ref, dy_ref,
             seg_ref, dx_ref, dbm_ref, dcm_ref, ddt_ref, dacs_ref, dd_ref, dh_ref):
        @pl.when(pl.program_id(1) == 0)
        def _():
            dh_ref[...] = jnp.zeros_like(dh_ref)
            dd_ref[...] = jnp.zeros_like(dd_ref)

        bm = bm_ref[...].astype(BF16)
        cm = cm_ref[...].astype(BF16)
        cb = lax.dot_general(cm, bm, NT, preferred_element_type=F32)
        first = lax.broadcasted_iota(jnp.int32, (1, V7X_LANES), 1) < p
        last_row = lax.broadcasted_iota(jnp.int32, (CHUNK, 1), 0) == CHUNK - 1
        seg_m = seg_ref[...]
        xs, dy, e_in, e_out, d_skip = xs_ref[...], dy_ref[...], ein_ref[...], eout_ref[...], d_ref[...]
        dt_l = dt_ref[...]
        xdt = xs * dt_l
        h_prev = prev_ref[...]
        h_prev_b = h_prev.astype(BF16)
        dh_next = dh_ref[...]
        dh_next_b = dh_next.astype(BF16)
        dy_e = (dy * e_in).astype(BF16)
        d_cm = lax.dot_general(dy_e, h_prev_b, NT, preferred_element_type=F32)
        dh_ref[...] = e_in[CHUNK - 1:CHUNK, :] * dh_next + lax.dot_general(cm, dy_e, TN, preferred_element_type=F32)
        q = jnp.dot(bm, dh_next_b, preferred_element_type=F32)
        xf = xdt * e_out
        d_bm = lax.dot_general(xf.astype(BF16), dh_next_b, NT, preferred_element_type=F32)
        d_cb = jnp.zeros((CHUNK, CHUNK), F32)
        parts, w_parts = [], []
        for pr in range(hpg // 2):
            lanes = slice(pr * V7X_LANES, (pr + 1) * V7X_LANES)
            lms, ms = _decay_masks(cb, ac_ref, ar_ref, (2 * pr, 2 * pr + 1))
            xp = xdt[:, lanes]
            xp_b = xp.astype(BF16)
            dyp = dy[:, lanes]
            halves = [jnp.where(first, dyp, 0.0).astype(BF16), jnp.where(first, 0.0, dyp).astype(BF16)]
            for lm, half in zip(lms, halves):
                d_cb = d_cb + lax.dot_general(half, xp_b, NT, preferred_element_type=F32) * lm
            dxd = lax.dot_general(jnp.concatenate(ms, axis=0), jnp.concatenate(halves, axis=0), TN,
                                  preferred_element_type=F32)
            stacked = jnp.concatenate([jnp.where(first, xp, 0.0), jnp.where(first, 0.0, xp)], axis=0).astype(BF16)
            y_diag = jnp.dot(jnp.concatenate(ms, axis=1), stacked, preferred_element_type=F32)
            parts.append(dxd)
            w_parts.append(dyp.astype(BF16).astype(F32) * y_diag - xp_b.astype(F32) * dxd)
        d_xdt = jnp.concatenate(parts, axis=1) + q * e_out
        dx_ref[...] = d_xdt * dt_l + dy * d_skip
        ch = jnp.dot(cm, h_prev_b, preferred_element_type=F32)
        qx = q * xf
        s_a = _head_sums(dy * ch * e_in - qx + jnp.concatenate(w_parts, axis=1), seg_m)[:, :hpg]
        d_last = (_head_totals(qx, seg_m)[:, :hpg]
                  + jnp.exp(ac_ref[CHUNK - 1:CHUNK, :]) * _head_totals(dh_next * h_prev, seg_m)[:, :hpg])
        ddt_ref[...] = _head_sums(d_xdt * xs, seg_m)[:, :hpg]
        dacs_ref[...] = s_a + jnp.where(last_row, d_last, 0.0)
        dd_ref[...] += _head_totals(dy * xs, seg_m)[:, :hpg]
        d_cb_b = d_cb.astype(BF16)
        dcm_ref[...] = d_cm + jnp.dot(d_cb_b, bm, preferred_element_type=F32)
        dbm_ref[...] = d_bm + lax.dot_general(d_cb_b, cm, TN, preferred_element_type=F32)

    gn = g_n * D_STATE
    return pl.pallas_call(
        body, name=name, grid=(g_n, nc),
        out_shape=[SDS((t, di), F32), SDS((t, gn), F32), SDS((t, gn), F32), SDS((g_n, t, hpg), F32),
                   SDS((g_n, t, hpg), F32), SDS((g_n, 1, hpg), F32)],
        in_specs=[sp["xs"], sp["bm"], sp["cm"], sp["xs"], sp["xs"], sp["xs"], sp["col"], sp["rowv"],
                  sp["lanes"], sp["prev"], sp["xs"], sp["seg"]],
        out_specs=[sp["xs"], sp["bc"], sp["bc"], sp["col"], sp["col"], sp["head"]],
        scratch_shapes=[pltpu.VMEM((D_STATE, w), F32)],
        compiler_params=_cp("parallel", "arbitrary"),
    )(xbc, xbc, xbc, dt_x, ein_x, eout_x, a_col, a_row, d_x, prev, dy, seg)


def _as3d(a):
    return a.reshape(a.shape[0], -1, a.shape[-1])


def _pair_sum(name, own, recv, core):
    shape = recv.shape
    cols = shape[-1]
    own3, recv3 = own.reshape(8, -1, cols), recv.reshape(4, -1, cols)
    rows = recv3.shape[1]
    tr = _row_tile(rows, cols, 2)

    def body(c_ref, a_ref, b_ref, o_ref):
        o_ref[...] = (a_ref[...].astype(F32) + b_ref[...].astype(F32)).astype(o_ref.dtype)

    blk = pl.BlockSpec((None, tr, cols), lambda q, i, c_ref: (q, i, 0))
    out = pl.pallas_call(
        body, name=name, out_shape=SDS(recv3.shape, recv.dtype),
        grid_spec=pltpu.PrefetchScalarGridSpec(
            num_scalar_prefetch=1, grid=(4, rows // tr),
            in_specs=[pl.BlockSpec((None, tr, cols), lambda q, i, c_ref: (4 * c_ref[0] + q, i, 0)), blk], out_specs=blk),
        compiler_params=_cp("parallel", "parallel"),
    )(core, own3, recv3)
    return out.reshape(shape)


def _adamw(name, w, m, v, parts, layer, prev=None, sel=None):
    lyr, rows, cols = w.shape
    n = len(parts)
    tr = _row_tile(rows, cols)
    np_ = 0 if prev is None else 4
    if sel is None:
        sel = jnp.zeros((1,), jnp.int32)

    def body(sel_ref, *refs):
        w_ref, m_ref, v_ref = refs[:3]
        p_refs = refs[3:3 + n]
        g_ref, d_ref, nm_ref, nv_ref = refs[3 + n + np_:]
        g = p_refs[0][...].astype(F32)
        for r in p_refs[1:]:
            g = g + r[...].astype(F32)
        nm = ADAM_B1 * m_ref[...] + (1.0 - ADAM_B1) * g
        nv = ADAM_B2 * v_ref[...] + (1.0 - ADAM_B2) * (g * g)
        m_hat = nm / (1.0 - ADAM_B1 ** ADAM_STEP)
        v_hat = nv / (1.0 - ADAM_B2 ** ADAM_STEP)
        g_ref[...] = g
        d_ref[...] = -ADAM_LR * (m_hat / (jnp.sqrt(v_hat) + ADAM_EPS) + ADAM_WD * w_ref[...])
        nm_ref[...] = nm
        nv_ref[...] = nv

    lspec = pl.BlockSpec((None, tr, cols), lambda i, s: (layer, i, 0))
    pspecs = [pl.BlockSpec((None, tr, cols), (lambda i, s: (s[0], i, 0)) if q is None else (lambda i, s, q=q: (q, i, 0)))
              for _, q in parts]
    aliases = {} if prev is None else {4 + n + q: q for q in range(4)}
    return pl.pallas_call(
        body, name=name, out_shape=[SDS(w.shape, F32)] * 4,
        grid_spec=pltpu.PrefetchScalarGridSpec(
            num_scalar_prefetch=1, grid=(rows // tr,), in_specs=[lspec] * 3 + pspecs + [ANY] * np_,
            out_specs=[lspec] * 4),
        input_output_aliases=aliases, compiler_params=_cp("parallel"),
    )(sel, w, m, v, *[arr for arr, _ in parts], *(prev or ()))


def _sum8(name, parts):
    rows = parts.shape[1]

    def body(p_ref, o_ref):
        s = p_ref[0]
        for q in range(1, N_DEV):
            s = s + p_ref[q]
        o_ref[...] = s

    return pl.pallas_call(
        body, name=name, grid=(1,), out_shape=SDS((rows, V7X_LANES), F32),
        in_specs=[pl.BlockSpec((N_DEV, rows, V7X_LANES), lambda i: (0, 0, 0))],
        out_specs=pl.BlockSpec((rows, V7X_LANES), lambda i: (0, 0)), compiler_params=_cp("arbitrary"),
    )(parts)


def _pack(vectors, align):
    flat = jnp.concatenate([v.reshape(-1) for v in vectors])
    pad = (-flat.shape[0]) % align
    if pad:
        flat = jnp.concatenate([flat, jnp.zeros((pad,), F32)])
    return flat.reshape(-1, V7X_LANES)


def _unpack(packed, shapes):
    flat = packed.reshape(-1)
    out, o = [], 0
    for s in shapes:
        size = 1
        for dim in s:
            size *= dim
        out.append(flat[o:o + size].reshape(s))
        o += size
    return out


def _residual_epilogue(acc, res):
    return (ALPHA * res + acc,)


def _plain_add_epilogue(acc, res):
    return (res + acc,)


def _gate_epilogue(acc, y, e):
    gate = _sigmoid(acc)
    xn = y + gate * e
    return xn, gate, xn


def _relu2(pre):
    r = jnp.maximum(pre, 0.0)
    return r * r


def _relu2_bwd_epilogue(acc, pre):
    return (acc * (2.0 * jnp.maximum(pre.astype(F32), 0.0)),)


def _tail_fwd(tag, u_a, wts, lng, lnb, p_l, finish_w2):
    y1, y1_b = _ln_fwd(f"ln1_{tag}", u_a, lng[0], lnb[0])
    (pre,) = _mm_fwd(f"mlp1_{tag}", y1_b, wts["w1"], "col", [BF16])
    finish_w2(pre)
    (u_b,) = _mm_fwd(f"mlp2_{tag}", pre, wts["w2"], "row", [F32], _residual_epilogue, (y1,), a_fn=_relu2)
    y2, y2_b = _ln_fwd(f"ln2_{tag}", u_b, lng[1], lnb[1])
    (e,) = _mm_fwd(f"ple_{tag}", p_l, wts["plew"], "col", [F32])
    xn, gate, xn_b = _mm_fwd(f"gate_{tag}", y2_b, wts["gate"], "row", [F32, F32, BF16], _gate_epilogue, (y2, e))
    return xn, xn_b, (u_a, y1_b, pre, u_b, y2_b, e, gate)


def _tail_bwd(tag, dxn, saved, wts, lng, p_l, emit):
    u_a, y1_b, pre, u_b, y2_b, e, gate = saved
    dgpre, de = _ple_bwd(f"ple_bwd_{tag}", dxn, e, gate)
    toks = emit(f"{tag}_ple", dict(gate=_mm_dw(f"gate_dw_{tag}", y2_b, dgpre, "row"),
                                   plew=_mm_dw(f"ple_dw_{tag}", p_l, de, "col")))
    (dy2,) = _mm_dx(f"gate_dx_{tag}", dgpre, wts["gate"], "row", [F32], _plain_add_epilogue, (dxn,), after=toks)
    du_b, du_b16, dg2, db2 = _ln_bwd(f"ln2_bwd_{tag}", u_b, dy2, lng[1])
    toks = emit(f"{tag}_w2", dict(w2=_mm_dw(f"mlp2_dw_{tag}", pre, du_b16, "row", a_fn=_relu2)))
    (dpre,) = _mm_dx(f"mlp2_dx_{tag}", du_b16, wts["w2"], "row", [BF16], _relu2_bwd_epilogue, (pre,), after=toks)
    toks = emit(f"{tag}_w1", dict(w1=_mm_dw(f"mlp1_dw_{tag}", y1_b, dpre, "col")))
    (dy1,) = _mm_dx(f"mlp1_dx_{tag}", dpre, wts["w1"], "col", [F32], _residual_epilogue, (du_b,), after=toks)
    du_a, du_a16, dg1, db1 = _ln_bwd(f"ln1_bwd_{tag}", u_a, dy1, lng[0])
    return du_a, du_a16, [dg1, dg2], [db1, db2]


def _to_slots(a, axis):
    shape = a.shape
    per = shape[axis] // N_DEV
    v = a.reshape(shape[:axis] + (2, 2, 2, per) + shape[axis + 1:])
    perm = (axis + 2, axis, axis + 1) + tuple(range(axis)) + tuple(range(axis + 3, v.ndim))
    v = v.transpose(perm)
    return v.reshape((N_DEV,) + shape[:axis] + (per,) + shape[axis + 1:])


def _pad_lanes(a):
    return jnp.pad(a, [(0, 0)] * (a.ndim - 1) + [(0, V7X_LANES - a.shape[-1])])


def kernel(x, p, pool_w, pool_scale, ssm_in_w, ssm_conv_w, ssm_conv_b, ssm_dt_bias, ssm_a_log, ssm_d, ssm_norm_w, ssm_out_w, mlp_w1, mlp_w2, ln_g, ln_b, ple_w, ple_gate_w, loss_target, m_pool_w, m_pool_scale, m_ssm_in_w, m_ssm_conv_w, m_ssm_conv_b, m_ssm_dt_bias, m_ssm_a_log, m_ssm_d, m_ssm_norm_w, m_ssm_out_w, m_mlp_w1, m_mlp_w2, m_ln_g, m_ln_b, m_ple_w, m_ple_gate_w, v_pool_w, v_pool_scale, v_ssm_in_w, v_ssm_conv_w, v_ssm_conv_b, v_ssm_dt_bias, v_ssm_a_log, v_ssm_d, v_ssm_norm_w, v_ssm_out_w, v_mlp_w1, v_mlp_w2, v_ln_g, v_ln_b, v_ple_w, v_ple_gate_w):
    t, d = x.shape[1:]
    h_n = ssm_dt_bias.shape[-1]
    di_s, cd_s, dp_s, d_s = ssm_norm_w.shape[-1], ssm_conv_b.shape[-1], ssm_in_w.shape[-1], ln_g.shape[-1]
    di, cd, dp = N_DEV * di_s, N_DEV * cd_s, N_DEV * dp_s
    p_dim = di // h_n
    g_n = (cd - di) // (2 * D_STATE)
    hpg = h_n // g_n
    dpp = di + cd + V7X_LANES
    assert h_n <= V7X_LANES and dp == di + cd + h_n and (di + cd) % V7X_LANES == 0
    dt_block = (di + cd) // V7X_LANES
    cg = d // 4
    me = 4 * lax.axis_index("x") + 2 * lax.axis_index("y") + lax.axis_index("c")

    x0, target = x[0], loss_target[0]
    p_l = [p[0, 0].astype(BF16), p[1, 0].astype(BF16)]

    small_shapes = [(CONV_WIDTH, cd_s), (1, cd_s), (1, di_s), (2, 2, d_s), (2, 2, d_s)]
    small = _pack([ssm_conv_w[0], ssm_conv_b, ssm_norm_w, ln_g, ln_b], 8 * V7X_LANES)
    first = [w.astype(BF16) for w in (pool_w[0], mlp_w1[0])]
    own_l0 = [w.astype(BF16) for w in (mlp_w2[0], ple_w[0], ple_gate_w[0])]
    own_ssm = [w.astype(BF16) for w in (ssm_in_w[0], ssm_out_w[0])]
    own_mlp = [w.astype(BF16) for w in (mlp_w1[1], mlp_w2[1], ple_w[1], ple_gate_w[1])]
    pool_g, w1_0, small_g = _all_gather("ag_layer0", first + [small])

    def gather_start(tag, own, after):
        lands = [lax.empty((N_DEV,) + w.shape, w.dtype) for w in own]
        return _async_start(f"ag_{tag}_start", _ag_first_copies, (4 * len(own),), own, lands, after)

    def gather_finish(tag, handle, after):
        n = len(handle[2]) // 2
        own, lands = _async_wait(f"ag_{tag}_wait", _ag_first_copies, handle, n, after)
        fwd = _async_start(f"ag_{tag}_forward_start", _ag_forward_copies, (3 * n,), [], lands)
        _, lands = _async_wait(f"ag_{tag}_forward_wait", _ag_forward_copies, fwd, 0)
        return [lax.dynamic_update_slice_in_dim(g, w[None], me, 0) for g, w in zip(lands, own)]

    ag_l0 = gather_start("l0", own_l0, (pool_g,))
    ag_ssm = gather_start("ssm", own_ssm, (ag_l0[3],))
    ag_mlp = gather_start("mlp1", own_mlp, (ag_ssm[3],))
    pool_scale_fwd = _token_add(pool_scale, ag_l0[3], ag_ssm[3], ag_mlp[3])
    pool_full = pool_g.transpose(1, 0, 2, 3).reshape(4, cg, cg)
    sm = small_g.reshape(N_DEV, -1)
    o = 0
    parts = []
    for shp in small_shapes:
        size = 1
        for s in shp:
            size *= s
        parts.append(sm[:, o:o + size].reshape((N_DEV,) + shp))
        o += size
    conv_w_full = parts[0].transpose(1, 0, 2).reshape(CONV_WIDTH, cd)
    conv_b_full = parts[1].transpose(1, 0, 2).reshape(1, cd)
    norm_w_full = parts[2].transpose(1, 0, 2).reshape(1, di)
    ln_g_full = parts[3].transpose(1, 2, 0, 3).reshape(2, 2, 1, d)
    ln_b_full = parts[4].transpose(1, 2, 0, 3).reshape(2, 2, 1, d)
    bias128, alog128 = _pad_lanes(ssm_dt_bias), _pad_lanes(ssm_a_log)

    pooled = _pool_windows("pool_fwd", x0, False)
    u0, hraw = _pool_mm("pool_mm", pooled, pool_full, pool_scale_fwd, x0)
    wts = [dict(w1=w1_0)]

    def finish_l0(after):
        w2_0, plew_0, gate_0 = gather_finish("l0", ag_l0, (after,))
        wts[0].update(w2=w2_0, plew=plew_0, gate=gate_0)

    x1, x1_b, saved0 = _tail_fwd("l0", u0, wts[0], ln_g_full[0], ln_b_full[0], p_l[0], finish_l0)

    in_g, out_g = gather_finish("ssm", ag_ssm, (x1,))
    in_full = jnp.pad(in_g.transpose(1, 0, 2).reshape(d, dp), ((0, 0), (0, dpp - dp)))
    (zx,) = _mm_fwd("in_proj", x1_b, in_full, "plain", [F32])
    xbc = _conv_fwd("conv_fwd", zx, conv_w_full, conv_b_full, di)
    dt, acs, e_in, e_out = _dt_fwd("dt_fwd", zx, bias128, alog128, dt_block)

    dt_x, ein_x, eout_x = _expand_heads("expand_heads", [dt, e_in, e_out], h_n, p_dim)
    d_x = jnp.repeat(ssm_d, p_dim, axis=1)

    def to_col(a):
        return a[:, :h_n].reshape(t, g_n, hpg).transpose(1, 0, 2)

    def to_row(a):
        return a[:, :h_n].reshape(t, g_n, hpg).transpose(1, 2, 0)

    def from_col(a):
        return _pad_lanes(a.transpose(1, 0, 2).reshape(t, h_n))

    a_col, a_row = to_col(acs), to_row(acs)
    y_ssd, prev = _ssd_fwd("ssd_fwd", xbc, dt_x, ein_x, eout_x, a_col, a_row, d_x, di, g_n, hpg, p_dim)
    yn = _gated_rms_fwd("gated_rms_fwd", y_ssd, zx, norm_w_full)
    (u2,) = _mm_fwd("out_proj", yn, out_g, "row", [F32], _residual_epilogue, (x1,))
    w1_1, w2_1, plew_1, gate_1 = gather_finish("mlp1", ag_mlp, (u2,))
    wts.append(dict(w1=w1_1, w2=w2_1, plew=plew_1, gate=gate_1))
    x2, _, saved1 = _tail_fwd("l1", u2, wts[1], ln_g_full[1], ln_b_full[1], p_l[1], lambda after: None)

    core = lax.axis_index("c").astype(jnp.int32).reshape(1)
    chip = (2 * lax.axis_index("x") + lax.axis_index("y")).astype(jnp.int32).reshape(1)
    scattering = {}

    def emit(tag, grads):
        names, arrays = list(grads), list(grads.values())
        halves = _rs_sibling_exchange(f"rs_{tag}_sibling", arrays)
        sums = [_pair_sum(f"rs_{tag}_pair_sum_{n}", g, hv, core) for n, g, hv in zip(names, arrays, halves)]
        lands = [lax.empty((3,) + s.shape[1:], s.dtype) for s in sums]
        handle = _async_start(f"rs_{tag}_start", _rs_chip_copies, (3 * len(sums),), sums, lands)
        scattering[tag] = (names, handle)
        return [handle[3]]

    def collect(tag, after):
        names, handle = scattering.pop(tag)
        sums, thirds = _async_wait(f"rs_{tag}_wait", _rs_chip_copies, handle, len(names), after)
        return {n: (s.reshape(4, -1, s.shape[-1]), r.reshape(3, -1, r.shape[-1])) for n, s, r in zip(names, sums, thirds)}

    dx2, loss_cols = _loss_bwd("loss", x2, target)
    du2, du2_b, dg_1, db_1 = _tail_bwd("l1", dx2, saved1, wts[1], ln_g_full[1], p_l[1], emit)
    toks = emit("ssm_out", dict(out=_mm_dw("out_proj_dw", yn, du2_b, "row")))
    (dyn,) = _mm_dx("out_proj_dx", du2_b, out_g, "row", [F32], after=toks)
    dy_ssd, dz, d_norm_w = _gated_rms_bwd("gated_rms_bwd", y_ssd, zx, norm_w_full, dyn)
    dxs, dbm, dcm, ddt_x, dacs, dd = _ssd_bwd("ssd_bwd", xbc, dt_x, ein_x, eout_x, a_col, a_row, d_x, prev, dy_ssd,
                                              di, g_n, hpg, p_dim)
    draw, d_bias, d_alog = _dt_bwd("dt_bwd", zx, bias128, alog128, from_col(dacs), from_col(ddt_x), dt_block)
    conv_parts = [_conv_bwd(f"conv_bwd_{tag}", zx, conv_w_full, conv_b_full, dact, di, first)
                  for tag, dact, first in (("xs", dxs, 0), ("b", dbm, di), ("c", dcm, di + g_n * D_STATE))]
    d_conv_w = jnp.concatenate([c[1] for c in conv_parts], axis=1)
    d_conv_b = jnp.concatenate([c[2] for c in conv_parts], axis=1)
    dzx = jnp.concatenate([dz] + [c[0] for c in conv_parts] + [draw], axis=1)
    toks = emit("ssm_in", {"in": _to_slots(_mm_dw("in_proj_dw", x1_b, dzx, "plain")[:, :dp], 1)})
    (dx1,) = _mm_dx("in_proj_dx", dzx, in_full, "plain", [F32], _residual_epilogue, (du2,), after=toks)

    du0, _, dg_0, db_0 = _tail_bwd("l0", dx1, saved0, wts[0], ln_g_full[0], p_l[0], emit)
    dh, dpool, d_scale = _pool_bwd_mm("pool_bwd_mm", du0, hraw, pool_full, pool_scale)
    toks = emit("pool", dict(pool=_to_slots(_pool_dw("pool_dw", pooled, dh), 1)))
    grad_x = _pool_windows("pool_bwd", dpool, True, du0)

    d_ln_g = jnp.stack([jnp.stack(dg_0), jnp.stack(dg_1)]).reshape(2, 2, d)
    d_ln_b = jnp.stack([jnp.stack(db_0), jnp.stack(db_1)]).reshape(2, 2, d)
    partial_shapes = [(CONV_WIDTH, cd), (1, cd), (1, di), (2, 2, d), (2, 2, d), (1, d), (1, h_n), (1, h_n), (1, h_n),
                      (1, d)]
    partial = _pack([d_conv_w, d_conv_b, d_norm_w, d_ln_g, d_ln_b, d_scale, d_bias[:, :h_n], d_alog[:, :h_n],
                     dd.reshape(1, h_n), loss_cols], 8 * V7X_LANES)
    (all_partials,) = _all_gather("ag_small_grads", [_token_add(partial, *toks)])
    tot = _unpack(_sum8("sum_small_grads", all_partials), partial_shapes)
    t_conv_w, t_conv_b, t_norm_w, t_ln_g, t_ln_b, t_scale, t_bias, t_alog, t_dd, t_loss = tot
    loss = jnp.sum(t_loss)

    def mine(a, per):
        return lax.dynamic_slice_in_dim(a, me * per, per, axis=a.ndim - 1)

    small_names = ["ssm_conv_w", "ssm_conv_b", "ssm_norm_w", "ln_g", "ln_b", "pool_scale", "ssm_dt_bias", "ssm_a_log",
                   "ssm_d"]
    small_w = [ssm_conv_w, ssm_conv_b, ssm_norm_w, ln_g, ln_b, pool_scale, ssm_dt_bias, ssm_a_log, ssm_d]
    small_m = [m_ssm_conv_w, m_ssm_conv_b, m_ssm_norm_w, m_ln_g, m_ln_b, m_pool_scale, m_ssm_dt_bias, m_ssm_a_log,
               m_ssm_d]
    small_v = [v_ssm_conv_w, v_ssm_conv_b, v_ssm_norm_w, v_ln_g, v_ln_b, v_pool_scale, v_ssm_dt_bias, v_ssm_a_log,
               v_ssm_d]
    small_grads = [mine(t_conv_w, cd_s), mine(t_conv_b, cd_s), mine(t_norm_w, di_s), mine(t_ln_g, d_s),
                   mine(t_ln_b, d_s), t_scale, t_bias, t_alog, t_dd]
    shapes = [w.shape for w in small_w]
    pk = [_pack(group, 8 * V7X_LANES)[None] for group in (small_w, small_m, small_v, small_grads)]
    res = _adamw("adamw_small", pk[0], pk[1], pk[2], [(pk[3], 0)], 0)
    upd = {}
    for name, vals in zip(small_names, zip(*[_unpack(r, shapes) for r in res])):
        upd[name] = list(vals)

    def update(tag, w, m, v, parts, layer, prev=None):
        own, recv = parts
        return _adamw(f"adamw_{tag}_{layer}", _as3d(w), _as3d(m), _as3d(v),
                      [(own, None), (recv, 0), (recv, 1), (recv, 2)], layer, prev, chip)

    q = collect("l1_ple", (res[0],))
    r_gate = update("ple_gate_w", ple_gate_w, m_ple_gate_w, v_ple_gate_w, q["gate"], 1)
    r_plew = update("ple_w", ple_w, m_ple_w, v_ple_w, q["plew"], 1)
    r_w2 = update("mlp_w2", mlp_w2, m_mlp_w2, v_mlp_w2, collect("l1_w2", (r_gate[0],))["w2"], 1)
    r_w1 = update("mlp_w1", mlp_w1, m_mlp_w1, v_mlp_w1, collect("l1_w1", (r_w2[0],))["w1"], 1)
    r_out = update("ssm_out_w", ssm_out_w, m_ssm_out_w, v_ssm_out_w, collect("ssm_out", (r_w1[0],))["out"], 0)
    r_in = update("ssm_in_w", ssm_in_w, m_ssm_in_w, v_ssm_in_w, collect("ssm_in", (r_out[0],))["in"], 0)
    q = collect("l0_ple", (r_in[0],))
    r_gate = update("ple_gate_w", ple_gate_w, m_ple_gate_w, v_ple_gate_w, q["gate"], 0, r_gate)
    r_plew = update("ple_w", ple_w, m_ple_w, v_ple_w, q["plew"], 0, r_plew)
    r_w2 = update("mlp_w2", mlp_w2, m_mlp_w2, v_mlp_w2, collect("l0_w2", (r_gate[0],))["w2"], 0, r_w2)
    r_w1 = update("mlp_w1", mlp_w1, m_mlp_w1, v_mlp_w1, collect("l0_w1", (r_w2[0],))["w1"], 0, r_w1)
    r_pool = update("pool_w", pool_w, m_pool_w, v_pool_w, collect("pool", (r_w1[0],))["pool"], 0)
    assert not scattering
    large = {"pool_w": (pool_w, r_pool), "ssm_in_w": (ssm_in_w, r_in), "ssm_out_w": (ssm_out_w, r_out),
             "mlp_w1": (mlp_w1, r_w1), "mlp_w2": (mlp_w2, r_w2), "ple_w": (ple_w, r_plew),
             "ple_gate_w": (ple_gate_w, r_gate)}
    for name, (w, rs) in large.items():
        upd[name] = [r.reshape(w.shape) for r in rs]

    order = ["pool_w", "pool_scale", "ssm_in_w", "ssm_conv_w", "ssm_conv_b", "ssm_dt_bias", "ssm_a_log", "ssm_d",
             "ssm_norm_w", "ssm_out_w", "mlp_w1", "mlp_w2", "ln_g", "ln_b", "ple_w", "ple_gate_w"]
    out = [loss, grad_x[None]]
    for k in range(4):
        out += [upd[name][k] for name in order]
    return tuple(out)
```

```python
import jax
import jax.numpy as jnp
from jax import lax
from jax.experimental import pallas as pl
from jax.experimental.pallas import tpu as pltpu

F32 = jnp.float32
BF16 = jnp.bfloat16
SDS = jax.ShapeDtypeStruct
MESH = pl.DeviceIdType.MESH
ANY = pl.BlockSpec(memory_space=pl.ANY)

N_DEV = 8
DEPTH = 2
ALPHA = (2.0 * DEPTH) ** 0.25
LN_EPS = 1e-5
RMS_EPS = 1e-5
POOL_WINDOW_LOG2 = (1, 2, 3, 4)
D_STATE = 128
CHUNK = 128
CONV_WIDTH = 4
ADAM_LR = 0.001
ADAM_B1 = 0.9
ADAM_B2 = 0.999
ADAM_EPS = 1e-08
ADAM_WD = 0.01
ADAM_STEP = 10

V7X_LANES = 128
V7X_VMEM_LIMIT = 48 * 1024 * 1024


def _cp(*sem):
    return pltpu.CompilerParams(dimension_semantics=sem, vmem_limit_bytes=V7X_VMEM_LIMIT)


def _pick(dim, cap):
    if dim <= cap:
        return dim
    best = None
    for t in range(V7X_LANES, cap + 1, V7X_LANES):
        if dim % t == 0:
            best = t
    assert best is not None, (dim, cap)
    return best


def _row_tile(rows, cols, itemsize=4, target=1 << 20):
    t = rows
    while t % 2 == 0 and t // 2 >= 16 and (t // 2) % 16 == 0 and t * cols * itemsize > target:
        t //= 2
    return t


def _slot(s):
    return (s % 2) * 4 + s // 2


def _all_gather(name, shards):
    n = len(shards)

    def body(*refs):
        ins, outs = refs[:n], refs[n:2 * n]
        send_sems, recv_sems, local_sems = refs[2 * n:]
        x, y, c = lax.axis_index("x"), lax.axis_index("y"), lax.axis_index("c")
        me, sibling = (x, y, c), (x, y, 1 - c)
        chips = [(1 - x, y), (x, 1 - y), (1 - x, 1 - y)]

        def copy(a, k, block, to, src=None):
            dst = outs[a].at[4 * block[0] + 2 * block[1] + block[2]]
            return pltpu.make_async_remote_copy(
                src_ref=dst if src is None else src, dst_ref=dst, send_sem=send_sems.at[a, k],
                recv_sem=recv_sems.at[a, k], device_id=to, device_id_type=MESH)

        mine = [pltpu.make_async_copy(ins[a], outs[a].at[4 * x + 2 * y + c], local_sems.at[a]) for a in range(n)]
        for cp in mine:
            cp.start()
        first = []
        for a in range(n):
            first.append(copy(a, 0, me, sibling, src=ins[a]))
            first += [copy(a, 1 + j, me, (*chip, c), src=ins[a]) for j, chip in enumerate(chips)]
        for cp in first:
            cp.start()
        passed = []
        for j, chip in enumerate(chips):
            for a in range(n):
                copy(a, 1 + j, (*chip, c), me).wait_recv()
                fwd = copy(a, 4 + j, (*chip, c), sibling)
                fwd.start()
                passed.append(fwd)
        for a in range(n):
            copy(a, 0, sibling, me).wait_recv()
            for j, chip in enumerate(chips):
                copy(a, 4 + j, (*chip, 1 - c), me).wait_recv()
        for cp in first + passed:
            cp.wait_send()
        for cp in mine:
            cp.wait()

    return pl.pallas_call(
        body, name=name,
        out_shape=[SDS((N_DEV,) + s.shape, s.dtype) for s in shards],
        in_specs=[ANY] * n, out_specs=[ANY] * n,
        scratch_shapes=[pltpu.SemaphoreType.DMA((n, 7)), pltpu.SemaphoreType.DMA((n, 7)),
                        pltpu.SemaphoreType.DMA((n,))],
    )(*shards)


HBM_SPEC = pl.BlockSpec(memory_space=pltpu.HBM)
SEM_SPEC = pl.BlockSpec(memory_space=pltpu.SEMAPHORE)
EFFECT = pltpu.SideEffectType.DATAFLOW_SIDE_EFFECTING


def _ag_first_copies(ins, lands, send_sems, recv_sems):
    x, y, c = lax.axis_index("x"), lax.axis_index("y"), lax.axis_index("c")
    targets = [(x, y, 1 - c), (1 - x, y, c), (x, 1 - y, c), (1 - x, 1 - y, c)]
    return [pltpu.make_async_remote_copy(
        src_ref=ins[a], dst_ref=lands[a].at[4 * x + 2 * y + c], send_sem=send_sems.at[4 * a + k],
        recv_sem=recv_sems.at[4 * a + k], device_id=to, device_id_type=MESH)
        for a in range(len(ins)) for k, to in enumerate(targets)]


def _ag_forward_copies(ins, lands, send_sems, recv_sems):
    x, y, c = lax.axis_index("x"), lax.axis_index("y"), lax.axis_index("c")
    cps = []
    for a in range(len(lands)):
        for j, (px, py) in enumerate([(1 - x, y), (x, 1 - y), (1 - x, 1 - y)]):
            blk = lands[a].at[4 * px + 2 * py + c]
            cps.append(pltpu.make_async_remote_copy(
                src_ref=blk, dst_ref=blk, send_sem=send_sems.at[3 * a + j], recv_sem=recv_sems.at[3 * a + j],
                device_id=(x, y, 1 - c), device_id_type=MESH))
    return cps


def _rs_sibling_copies(ins, lands, send_sems, recv_sems):
    x, y, c = lax.axis_index("x"), lax.axis_index("y"), lax.axis_index("c")
    return [pltpu.make_async_remote_copy(
        src_ref=ins[a].at[pl.ds(4 * (1 - c), 4)], dst_ref=lands[a], send_sem=send_sems.at[a], recv_sem=recv_sems.at[a],
        device_id=(x, y, 1 - c), device_id_type=MESH) for a in range(len(ins))]


def _rs_chip_copies(ins, lands, send_sems, recv_sems):
    x, y, c = lax.axis_index("x"), lax.axis_index("y"), lax.axis_index("c")
    cps = []
    for a in range(len(ins)):
        for j, (px, py) in enumerate([(1 - x, y), (x, 1 - y), (1 - x, 1 - y)]):
            cps.append(pltpu.make_async_remote_copy(
                src_ref=ins[a].at[2 * px + py], dst_ref=lands[a].at[j], send_sem=send_sems.at[3 * a + j],
                recv_sem=recv_sems.at[3 * a + j], device_id=(px, py, c), device_id_type=MESH))
    return cps


def _async_start(name, build, sem_shape, ins, lands, after=()):
    arrays = [*ins, *lands]
    n_i, n_t, n_a = len(ins), len(arrays), len(after)

    def body(*refs):
        outs = refs[n_t + n_a:]
        for cp in build(refs[:n_i], refs[n_i:n_t], outs[0], outs[1]):
            cp.start()
        outs[-1][...] = jnp.zeros_like(outs[-1])

    res = pl.pallas_call(
        body, name=name,
        out_shape=(pltpu.SemaphoreType.DMA(sem_shape), pltpu.SemaphoreType.DMA(sem_shape),
                   *[pltpu.HBM(a.shape, a.dtype) for a in arrays], SDS((8, V7X_LANES), F32)),
        in_specs=[HBM_SPEC] * n_t + [ANY] * n_a,
        out_specs=(SEM_SPEC, SEM_SPEC, *[HBM_SPEC] * n_t, pl.BlockSpec(memory_space=pltpu.VMEM)),
        input_output_aliases={i: 2 + i for i in range(n_t)},
        compiler_params=pltpu.CompilerParams(has_side_effects=EFFECT),
    )(*[pltpu.with_memory_space_constraint(a, pltpu.HBM) for a in arrays], *after)
    return res[0], res[1], list(res[2:2 + n_t]), res[-1]


def _async_wait(name, build, handle, n_i, after=()):
    send_sems, recv_sems, arrays, _ = handle
    n_t, n_a = len(arrays), len(after)

    def body(*refs):
        for cp in build(refs[:n_i], refs[n_i:n_t], refs[n_t], refs[n_t + 1]):
            cp.wait_send()
            cp.wait_recv()

    res = pl.pallas_call(
        body, name=name, out_shape=tuple(pltpu.HBM(a.shape, a.dtype) for a in arrays),
        in_specs=[HBM_SPEC] * n_t + [SEM_SPEC, SEM_SPEC] + [ANY] * n_a, out_specs=tuple([HBM_SPEC] * n_t),
        input_output_aliases={i: i for i in range(n_t)},
        compiler_params=pltpu.CompilerParams(has_side_effects=EFFECT),
    )(*arrays, send_sems, recv_sems, *after)
    return list(res[:n_i]), list(res[n_i:])


def _token_add(a, *tokens):
    for tok in tokens:
        a = a + tok[0:1, 0:1].reshape((1,) * a.ndim)
    return a


def _mm_core(name, a, b, *, grid, a_spec, b_spec, dims, acc_shape, outs, out_spec, epilogue=None, extras=(),
             extra_specs=(), a_fn=None, after=()):
    nk = grid[2]
    ne, no, na = len(extras), len(outs), len(after)

    def body(a_ref, b_ref, *rest):
        e_refs, o_refs, acc = rest[:ne], rest[ne + na:ne + na + no], rest[ne + na + no]
        k = pl.program_id(2)

        def product():
            lhs = a_ref[...] if a_fn is None else a_fn(a_ref[...])
            return lax.dot_general(lhs.astype(BF16), b_ref[...].astype(BF16), dims, preferred_element_type=F32)

        @pl.when(k == 0)
        def _():
            acc[...] = product()

        @pl.when(k > 0)
        def _():
            acc[...] += product()

        @pl.when(k == nk - 1)
        def _():
            r = acc[...]
            vals = epilogue(r, *[e[...] for e in e_refs]) if epilogue is not None else (r,)
            for o, v in zip(o_refs, vals):
                o[...] = v.astype(o.dtype)

    res = pl.pallas_call(
        body, name=name, grid=grid, out_shape=list(outs),
        in_specs=[a_spec, b_spec, *extra_specs, *[ANY] * na], out_specs=[out_spec] * no,
        scratch_shapes=[pltpu.VMEM(acc_shape, F32)],
        compiler_params=_cp("parallel", "parallel", "arbitrary"),
    )(a, b, *extras, *after)
    return res


NN = (((1,), (0,)), ((), ()))
NT = (((1,), (1,)), ((), ()))
TN = (((0,), (0,)), ((), ()))


def _w_dims(w, kind):
    if kind == "col":
        return w.shape[1], N_DEV * w.shape[2], w.shape[1], w.shape[2]
    if kind == "row":
        return N_DEV * w.shape[1], w.shape[2], w.shape[1], w.shape[2]
    return w.shape[0], w.shape[1], w.shape[0], w.shape[1]


MM_VMEM_BUDGET = 36 * 1024 * 1024


def _row_block(m, tn, tk, a, out_dtypes, extras):
    for tm in (_pick(m, 2048), _pick(m, 1024), _pick(m, 512)):
        per_out = sum(jnp.dtype(dt).itemsize for dt in out_dtypes) + sum(e.dtype.itemsize for e in extras)
        used = tm * tn * (4 + 2 * per_out) + 2 * (tm * tk * a.dtype.itemsize + tk * tn * 2)
        if used <= MM_VMEM_BUDGET:
            return tm
    return tm


def _mm_fwd(name, a, w, kind, out_dtypes, epilogue=None, extras=(), a_fn=None, after=()):
    if kind == "row":
        w, kind = w.reshape(-1, w.shape[-1]), "plain"
    m, k_dim = a.shape
    kk, n, ks, ns = _w_dims(w, kind)
    assert kk == k_dim
    if kind == "col":
        tn, tk = _pick(ns, 1024), _pick(kk, 512)
        nb = ns // tn
        b_spec = pl.BlockSpec((None, tk, tn), lambda i, j, k: (j // nb, k, j % nb))
    else:
        tn, tk = _pick(n, 1152), _pick(kk, 512)
        b_spec = pl.BlockSpec((tk, tn), lambda i, j, k: (k, j))
    tm = _row_block(m, tn, tk, a, out_dtypes, extras)
    mn_spec = pl.BlockSpec((tm, tn), lambda i, j, k: (i, j))
    return _mm_core(
        name, a, w, grid=(m // tm, n // tn, kk // tk),
        a_spec=pl.BlockSpec((tm, tk), lambda i, j, k: (i, k)), b_spec=b_spec, dims=NN, acc_shape=(tm, tn),
        outs=[SDS((m, n), dt) for dt in out_dtypes], out_spec=mn_spec, epilogue=epilogue, extras=extras,
        extra_specs=[mn_spec] * len(extras), a_fn=a_fn, after=after)


def _mm_dx(name, dy, w, kind, out_dtypes, epilogue=None, extras=(), after=()):
    if kind == "row":
        w, kind = w.reshape(-1, w.shape[-1]), "plain"
    m, n_dim = dy.shape
    kk, n, ks, ns = _w_dims(w, kind)
    assert n == n_dim
    if kind == "col":
        tn, tk = _pick(kk, 1024), _pick(ns, 512)
        kb = ns // tk
        b_spec = pl.BlockSpec((None, tn, tk), lambda i, j, k: (k // kb, j, k % kb))
    else:
        tn, tk = _pick(kk, 1024), _pick(n, 1152)
        b_spec = pl.BlockSpec((tn, tk), lambda i, j, k: (j, k))
    tm = _row_block(m, tn, tk, dy, out_dtypes, extras)
    mk_spec = pl.BlockSpec((tm, tn), lambda i, j, k: (i, j))
    return _mm_core(
        name, dy, w, grid=(m // tm, kk // tn, n // tk),
        a_spec=pl.BlockSpec((tm, tk), lambda i, j, k: (i, k)), b_spec=b_spec, dims=NT, acc_shape=(tm, tn),
        outs=[SDS((m, kk), dt) for dt in out_dtypes], out_spec=mk_spec, epilogue=epilogue, extras=extras,
        extra_specs=[mk_spec] * len(extras), after=after)


def _mm_dw(name, a, dy, kind, a_fn=None):
    m, kk = a.shape
    n = dy.shape[1]
    tk = _pick(m, 512)
    if kind == "col":
        ns = n // N_DEV
        tm, tn = _pick(kk, 1024), _pick(ns, 1024)
        nb = ns // tn
        out = SDS((N_DEV, kk, ns), BF16)
        out_spec = pl.BlockSpec((None, tm, tn), lambda i, j, k: (_slot(j // nb), i, j % nb))
    elif kind == "row":
        ks = kk // N_DEV
        tm, tn = _pick(ks, 1024), _pick(n, 1024)
        mb = ks // tm
        out = SDS((N_DEV, ks, n), BF16)
        out_spec = pl.BlockSpec((None, tm, tn), lambda i, j, k: (_slot(i // mb), i % mb, j))
    else:
        tm, tn = _pick(kk, 1024), _pick(n, 1152)
        out = SDS((kk, n), BF16)
        out_spec = pl.BlockSpec((tm, tn), lambda i, j, k: (i, j))
    return _mm_core(
        name, a, dy, grid=(kk // tm, n // tn, m // tk),
        a_spec=pl.BlockSpec((tk, tm), lambda i, j, k: (k, i)), b_spec=pl.BlockSpec((tk, tn), lambda i, j, k: (k, j)),
        dims=TN, acc_shape=(tm, tn), outs=[out], out_spec=out_spec, a_fn=a_fn)[0]


def _rowwise(name, fn, ins, outs, rows, tile):
    arrays, specs = [], []
    for arr, kind in ins:
        arrays.append(arr)
        if kind == "row":
            specs.append(pl.BlockSpec((tile, arr.shape[1]), lambda i: (i, 0)))
        elif kind == "vec":
            specs.append(pl.BlockSpec(arr.shape, lambda i, nd=arr.ndim: (0,) * nd))
        else:
            specs.append(kind)
    out_shapes, out_specs, kinds = [], [], []
    for cols, dt, kind in outs:
        kinds.append(kind)
        if kind == "row":
            out_shapes.append(SDS((rows, cols), dt))
            out_specs.append(pl.BlockSpec((tile, cols), lambda i: (i, 0)))
        else:
            out_shapes.append(SDS((1, cols), F32))
            out_specs.append(pl.BlockSpec((1, cols), lambda i: (0, 0)))
    ni = len(arrays)
    has_acc = "acc" in kinds

    def body(*refs):
        vals = fn(*[r[...] for r in refs[:ni]])
        i = pl.program_id(0)
        for o, v, kind in zip(refs[ni:], vals, kinds):
            if kind == "row":
                o[...] = v.astype(o.dtype)
            else:
                @pl.when(i == 0)
                def _(o=o):
                    o[...] = jnp.zeros_like(o)

                o[...] += v

    return pl.pallas_call(
        body, name=name, grid=(rows // tile,), out_shape=out_shapes, in_specs=specs, out_specs=out_specs,
        compiler_params=_cp("arbitrary" if has_acc else "parallel"),
    )(*arrays)


def _ln_fwd(name, u, g, b):
    d = u.shape[1]

    def fn(u, g, b):
        mu = jnp.mean(u, axis=1, keepdims=True)
        xc = u - mu
        var = jnp.mean(xc * xc, axis=1, keepdims=True)
        y = xc * lax.rsqrt(var + LN_EPS) * g + b
        return y, y

    return _rowwise(name, fn, [(u, "row"), (g, "vec"), (b, "vec")], [(d, F32, "row"), (d, BF16, "row")], u.shape[0], 256)


def _ln_bwd(name, u, dy, g):
    d = u.shape[1]

    def fn(u, dy, g):
        mu = jnp.mean(u, axis=1, keepdims=True)
        xc = u - mu
        var = jnp.mean(xc * xc, axis=1, keepdims=True)
        rstd = lax.rsqrt(var + LN_EPS)
        xhat = xc * rstd
        dxhat = dy * g
        m1 = jnp.mean(dxhat, axis=1, keepdims=True)
        m2 = jnp.mean(dxhat * xhat, axis=1, keepdims=True)
        du = rstd * (dxhat - m1 - xhat * m2)
        return du, du, jnp.sum(dy * xhat, axis=0, keepdims=True), jnp.sum(dy, axis=0, keepdims=True)

    return _rowwise(name, fn, [(u, "row"), (dy, "row"), (g, "vec")],
                    [(d, F32, "row"), (d, BF16, "row"), (d, F32, "acc"), (d, F32, "acc")], u.shape[0], 256)


def _loss_bwd(name, y, target):
    d = y.shape[1]

    def fn(y, t):
        e = y - t
        return e * (1.0 / d), jnp.sum(e * e, axis=0, keepdims=True) * (0.5 / d)

    return _rowwise(name, fn, [(y, "row"), (target, "row")], [(d, F32, "row"), (d, F32, "acc")], y.shape[0], 256)


def _ple_bwd(name, dx, e, gate):
    d = dx.shape[1]

    def fn(dx, e, gate):
        return dx * e * gate * (1.0 - gate), dx * gate

    return _rowwise(name, fn, [(dx, "row"), (e, "row"), (gate, "row")], [(d, BF16, "row"), (d, BF16, "row")],
                    dx.shape[0], 256)


def _sigmoid(v):
    return 1.0 / (1.0 + jnp.exp(-v))


def _gated_rms_fwd(name, y, zx, norm_w):
    di = y.shape[1]

    def fn(y, z, w):
        yg = y * (z * _sigmoid(z))
        r = lax.rsqrt(jnp.mean(yg * yg, axis=1, keepdims=True) + RMS_EPS)
        return (yg * r * w,)

    z_spec = pl.BlockSpec((128, di), lambda i: (i, 0))
    return _rowwise(name, fn, [(y, "row"), (zx, z_spec), (norm_w, "vec")], [(di, BF16, "row")], y.shape[0], 128)[0]


def _gated_rms_bwd(name, y, zx, norm_w, dout):
    di = y.shape[1]

    def fn(y, z, w, dout):
        sg = _sigmoid(z)
        sz = z * sg
        yg = y * sz
        r = lax.rsqrt(jnp.mean(yg * yg, axis=1, keepdims=True) + RMS_EPS)
        dn = dout * w
        dyg = r * (dn - yg * (r * r) * jnp.mean(dn * yg, axis=1, keepdims=True))
        dy = dyg * sz
        dz = dyg * y * (sg * (1.0 + z * (1.0 - sg)))
        return dy, dz, jnp.sum(dout * yg * r, axis=0, keepdims=True)

    z_spec = pl.BlockSpec((128, di), lambda i: (i, 0))
    return _rowwise(name, fn, [(y, "row"), (zx, z_spec), (norm_w, "vec"), (dout, "row")],
                    [(di, F32, "row"), (di, BF16, "row"), (di, F32, "acc")], y.shape[0], 128)


def _shift_down(v, j, row):
    return jnp.where(row >= j, pltpu.roll(v, j, 0), 0.0)


def _shift_up(v, j, row):
    t = v.shape[0]
    return jnp.where(row < t - j, pltpu.roll(v, t - j, 0), 0.0)


def _pool_select(parts, g):
    return jnp.where(g == 0, parts[0], jnp.where(g == 1, parts[1], jnp.where(g == 2, parts[2], parts[3])))


def _pool_windows(name, x, transpose, scale_by=None, after=()):
    t, d = x.shape
    cg = d // 4
    cw = V7X_LANES
    per = cg // cw

    def body(*refs):
        x_ref, o_ref = refs[0], refs[-1]
        g = pl.program_id(0) // per
        xv = x_ref[...]
        row = lax.broadcasted_iota(jnp.int32, (t, 1), 0)
        cnt = jnp.minimum(row + 1, jnp.left_shift(2, g)).astype(F32)
        s = xv / cnt if transpose else xv
        parts = []
        for lg in POOL_WINDOW_LOG2:
            j = 1 << (lg - 1)
            s = s + (_shift_up(s, j, row) if transpose else _shift_down(s, j, row))
            parts.append(s)
        sel = _pool_select(parts, g)
        if transpose:
            o_ref[...] = ALPHA * refs[1][...] + sel - xv
        else:
            o_ref[...] = (sel / cnt - xv).astype(o_ref.dtype)

    col = pl.BlockSpec((t, cw), lambda j: (0, j))
    ins = [x] if scale_by is None else [x, scale_by]
    return pl.pallas_call(
        body, name=name, grid=(d // cw,), out_shape=SDS((t, d), F32 if transpose else BF16),
        in_specs=[col] * len(ins) + [ANY] * len(after), out_specs=col, compiler_params=_cp("parallel"),
    )(*ins, *after)


def _pool_mm(name, pooled, w, scale, x):
    t, d = x.shape
    cg = d // 4
    tm = _pick(t, 1024)

    def body(p_ref, w_ref, s_ref, x_ref, u_ref, h_ref):
        h = jnp.dot(p_ref[...], w_ref[...], preferred_element_type=F32)
        h_ref[...] = h
        u_ref[...] = ALPHA * x_ref[...] + h * s_ref[...]

    blk = pl.BlockSpec((tm, cg), lambda g, i: (i, g))
    return pl.pallas_call(
        body, name=name, grid=(4, t // tm), out_shape=[SDS((t, d), F32), SDS((t, d), F32)],
        in_specs=[blk, pl.BlockSpec((None, cg, cg), lambda g, i: (g, 0, 0)), pl.BlockSpec((1, cg), lambda g, i: (0, g)),
                  blk],
        out_specs=[blk, blk], compiler_params=_cp("parallel", "parallel"),
    )(pooled, w, scale, x)


def _pool_bwd_mm(name, du, hraw, w, scale):
    t, d = du.shape
    cg = d // 4
    tm = _pick(t, 1024)

    def body(du_ref, h_ref, w_ref, s_ref, dh_ref, dp_ref, ds_ref):
        @pl.when(pl.program_id(1) == 0)
        def _():
            ds_ref[...] = jnp.zeros_like(ds_ref)

        duv = du_ref[...]
        ds_ref[...] += jnp.sum(duv * h_ref[...], axis=0, keepdims=True)
        dh = (duv * s_ref[...]).astype(BF16)
        dh_ref[...] = dh
        dp_ref[...] = lax.dot_general(dh, w_ref[...], NT, preferred_element_type=F32)

    blk = pl.BlockSpec((tm, cg), lambda g, i: (i, g))
    vec = pl.BlockSpec((1, cg), lambda g, i: (0, g))
    return pl.pallas_call(
        body, name=name, grid=(4, t // tm), out_shape=[SDS((t, d), BF16), SDS((t, d), F32), SDS((1, d), F32)],
        in_specs=[blk, blk, pl.BlockSpec((None, cg, cg), lambda g, i: (g, 0, 0)), vec],
        out_specs=[blk, blk, vec], compiler_params=_cp("parallel", "arbitrary"),
    )(du, hraw, w, scale)


def _pool_dw(name, pooled, dh):
    t, d = pooled.shape
    cg = d // 4
    tk = _pick(t, 512)
    nk = t // tk

    def body(p_ref, dh_ref, o_ref, acc):
        k = pl.program_id(1)

        @pl.when(k == 0)
        def _():
            acc[...] = jnp.zeros_like(acc)

        acc[...] += lax.dot_general(p_ref[...], dh_ref[...], TN, preferred_element_type=F32)

        @pl.when(k == nk - 1)
        def _():
            o_ref[...] = acc[...].astype(o_ref.dtype)

    blk = pl.BlockSpec((tk, cg), lambda g, k: (k, g))
    return pl.pallas_call(
        body, name=name, grid=(4, nk), out_shape=SDS((4, cg, cg), BF16), in_specs=[blk, blk],
        out_specs=pl.BlockSpec((None, cg, cg), lambda g, k: (g, 0, 0)), scratch_shapes=[pltpu.VMEM((cg, cg), F32)],
        compiler_params=_cp("parallel", "arbitrary"),
    )(pooled, dh)


def _conv_pre(u, w_ref, b_ref, row):
    pre = b_ref[...] + _shift_down(u, 3, row) * w_ref[0:1, :]
    pre = pre + _shift_down(u, 2, row) * w_ref[1:2, :]
    pre = pre + _shift_down(u, 1, row) * w_ref[2:3, :]
    return pre + u * w_ref[3:4, :]


def _conv_fwd(name, zx, conv_w, conv_b, di):
    t = zx.shape[0]
    cd = conv_w.shape[1]
    cw = _pick(cd, 256)
    off = di // cw

    def body(u_ref, w_ref, b_ref, o_ref):
        row = lax.broadcasted_iota(jnp.int32, (t, 1), 0)
        pre = _conv_pre(u_ref[...], w_ref, b_ref, row)
        o_ref[...] = pre * _sigmoid(pre)

    return pl.pallas_call(
        body, name=name, grid=(cd // cw,), out_shape=SDS((t, cd), F32),
        in_specs=[pl.BlockSpec((t, cw), lambda j: (0, off + j)), pl.BlockSpec((CONV_WIDTH, cw), lambda j: (0, j)),
                  pl.BlockSpec((1, cw), lambda j: (0, j))],
        out_specs=pl.BlockSpec((t, cw), lambda j: (0, j)), compiler_params=_cp("parallel"),
    )(zx, conv_w, conv_b)


def _conv_bwd(name, zx, conv_w, conv_b, dact, di, first):
    t, cd = dact.shape
    cw = _pick(cd, 256)
    off, woff = (di + first) // cw, first // cw

    def body(u_ref, w_ref, b_ref, da_ref, du_ref, dw_ref, db_ref):
        row = lax.broadcasted_iota(jnp.int32, (t, 1), 0)
        u = u_ref[...]
        pre = _conv_pre(u, w_ref, b_ref, row)
        sg = _sigmoid(pre)
        dpre = da_ref[...] * (sg * (1.0 + pre * (1.0 - sg)))
        du = dpre * w_ref[3:4, :]
        for j in (1, 2, 3):
            du = du + _shift_up(dpre, j, row) * w_ref[3 - j:4 - j, :]
            dw_ref[3 - j:4 - j, :] = jnp.sum(dpre * _shift_down(u, j, row), axis=0, keepdims=True)
        dw_ref[3:4, :] = jnp.sum(dpre * u, axis=0, keepdims=True)
        db_ref[...] = jnp.sum(dpre, axis=0, keepdims=True)
        du_ref[...] = du.astype(du_ref.dtype)

    wspec = pl.BlockSpec((CONV_WIDTH, cw), lambda j: (0, j))
    bspec = pl.BlockSpec((1, cw), lambda j: (0, j))
    ospec = pl.BlockSpec((t, cw), lambda j: (0, j))
    return pl.pallas_call(
        body, name=name, grid=(cd // cw,), out_shape=[SDS((t, cd), BF16), SDS((CONV_WIDTH, cd), F32), SDS((1, cd), F32)],
        in_specs=[pl.BlockSpec((t, cw), lambda j: (0, off + j)), pl.BlockSpec((CONV_WIDTH, cw), lambda j: (0, woff + j)),
                  pl.BlockSpec((1, cw), lambda j: (0, woff + j)), ospec],
        out_specs=[ospec, wspec, bspec], compiler_params=_cp("parallel"),
    )(zx, conv_w, conv_b, dact)


def _expand_heads(name, arrays, h_n, p):
    t = arrays[0].shape[0]
    n = len(arrays)
    w = _pick(h_n * p, 512)

    def body(*refs):
        j = pl.program_id(0)
        head = lax.broadcasted_iota(jnp.int32, (V7X_LANES, w), 0)
        lane = lax.broadcasted_iota(jnp.int32, (V7X_LANES, w), 1)
        spread = (head == j * (w // p) + lane // p).astype(BF16)
        for a_ref, o_ref in zip(refs[:n], refs[n:]):
            rest = a_ref[...]
            out = jnp.zeros((t, w), F32)
            for _ in range(3):
                piece = rest.astype(BF16)
                out = out + jnp.dot(piece, spread, preferred_element_type=F32)
                rest = rest - piece.astype(F32)
            o_ref[...] = out

    full = pl.BlockSpec((t, V7X_LANES), lambda j: (0, 0))
    return pl.pallas_call(
        body, name=name, grid=(h_n * p // w,), out_shape=[SDS((t, h_n * p), F32)] * n, in_specs=[full] * n,
        out_specs=[pl.BlockSpec((t, w), lambda j: (0, j))] * n, compiler_params=_cp("parallel"),
    )(*arrays)


def _softplus(v):
    return jnp.maximum(v, 0.0) + jnp.log(1.0 + jnp.exp(-jnp.abs(v)))


def _dt_fwd(name, zx, bias, a_log, col_block):
    t = zx.shape[0]

    def body(r_ref, b_ref, al_ref, dt_ref, acs_ref, ein_ref, eout_ref):
        row = lax.broadcasted_iota(jnp.int32, (t, 1), 0) % CHUNK
        dt = _softplus(r_ref[...] + b_ref[...])
        da = dt * (-jnp.exp(al_ref[...]))
        s, r = da, da
        j = 1
        while j < CHUNK:
            s = s + jnp.where(row >= j, pltpu.roll(s, j, 0), 0.0)
            r = r + jnp.where(row < CHUNK - j, pltpu.roll(r, t - j, 0), 0.0)
            j *= 2
        dt_ref[...] = dt
        acs_ref[...] = s
        ein_ref[...] = jnp.exp(s)
        eout_ref[...] = jnp.exp(r - da)

    vec = pl.BlockSpec((1, V7X_LANES), lambda i: (0, 0))
    full = pl.BlockSpec((t, V7X_LANES), lambda i: (0, 0))
    return pl.pallas_call(
        body, name=name, grid=(1,), out_shape=[SDS((t, V7X_LANES), F32)] * 4,
        in_specs=[pl.BlockSpec((t, V7X_LANES), lambda i: (0, col_block)), vec, vec], out_specs=[full] * 4,
        compiler_params=_cp("arbitrary"),
    )(zx, bias, a_log)


def _dt_bwd(name, zx, bias, a_log, d_acs, d_dt, col_block):
    t = zx.shape[0]

    def body(r_ref, b_ref, al_ref, da_ref, dd_ref, draw_ref, db_ref, dal_ref):
        row = lax.broadcasted_iota(jnp.int32, (t, 1), 0) % CHUNK
        pre = r_ref[...] + b_ref[...]
        dt = _softplus(pre)
        a = -jnp.exp(al_ref[...])
        s = da_ref[...]
        j = 1
        while j < CHUNK:
            s = s + jnp.where(row < CHUNK - j, pltpu.roll(s, t - j, 0), 0.0)
            j *= 2
        ddt = dd_ref[...] + s * a
        dal_ref[...] = jnp.sum(s * dt, axis=0, keepdims=True) * a
        draw = ddt * _sigmoid(pre)
        db_ref[...] = jnp.sum(draw, axis=0, keepdims=True)
        draw_ref[...] = draw.astype(draw_ref.dtype)

    vec = pl.BlockSpec((1, V7X_LANES), lambda i: (0, 0))
    full = pl.BlockSpec((t, V7X_LANES), lambda i: (0, 0))
    return pl.pallas_call(
        body, name=name, grid=(1,), out_shape=[SDS((t, V7X_LANES), BF16), SDS((1, V7X_LANES), F32), SDS((1, V7X_LANES), F32)],
        in_specs=[pl.BlockSpec((t, V7X_LANES), lambda i: (0, col_block)), vec, vec, full, full],
        out_specs=[full, vec, vec], compiler_params=_cp("arbitrary"),
    )(zx, bias, a_log, d_acs, d_dt)


def _ssd_specs(t, di, g_n, hpg, p, rev):
    nc = t // CHUNK
    w = hpg * p
    nb = di // D_STATE

    def cc(c):
        return nc - 1 - c if rev else c

    return dict(
        xs=pl.BlockSpec((CHUNK, w), lambda g, c: (cc(c), g)),
        bm=pl.BlockSpec((CHUNK, D_STATE), lambda g, c: (cc(c), nb + g)),
        cm=pl.BlockSpec((CHUNK, D_STATE), lambda g, c: (cc(c), nb + g_n + g)),
        col=pl.BlockSpec((None, CHUNK, hpg), lambda g, c: (g, cc(c), 0)),
        rowv=pl.BlockSpec((None, hpg, CHUNK), lambda g, c: (g, 0, cc(c))),
        head=pl.BlockSpec((None, 1, hpg), lambda g, c: (g, 0, 0)),
        lanes=pl.BlockSpec((1, w), lambda g, c: (0, g)),
        bc=pl.BlockSpec((CHUNK, D_STATE), lambda g, c: (cc(c), g)),
        prev=pl.BlockSpec((None, None, D_STATE, w), lambda g, c: (cc(c), g, 0, 0)),
        seg=pl.BlockSpec((w, V7X_LANES), lambda g, c: (0, 0)),
    )


def _decay_masks(cb, ac_ref, ar_ref, heads):
    li = lax.broadcasted_iota(jnp.int32, (CHUNK, CHUNK), 0)
    si = lax.broadcasted_iota(jnp.int32, (CHUNK, CHUNK), 1)
    lms = [jnp.exp(jnp.where(li >= si, ac_ref[:, hh:hh + 1] - ar_ref[hh:hh + 1, :], -jnp.inf)) for hh in heads]
    return lms, [(cb * lm).astype(BF16) for lm in lms]


def _ssd_fwd(name, xbc, dt_x, ein_x, eout_x, a_col, a_row, d_x, di, g_n, hpg, p):
    t = xbc.shape[0]
    nc = t // CHUNK
    w = hpg * p
    assert 2 * p == V7X_LANES and hpg % 2 == 0
    sp = _ssd_specs(t, di, g_n, hpg, p, False)

    def body(xs_ref, bm_ref, cm_ref, dt_ref, ein_ref, eout_ref, ac_ref, ar_ref, d_ref, y_ref, prev_ref, h_ref):
        @pl.when(pl.program_id(1) == 0)
        def _():
            h_ref[...] = jnp.zeros_like(h_ref)

        bm = bm_ref[...].astype(BF16)
        cm = cm_ref[...].astype(BF16)
        cb = lax.dot_general(cm, bm, NT, preferred_element_type=F32)
        first = lax.broadcasted_iota(jnp.int32, (1, V7X_LANES), 1) < p
        xs = xs_ref[...]
        e_in = ein_ref[...]
        xdt = xs * dt_ref[...]
        ys = []
        for pr in range(hpg // 2):
            _, ms = _decay_masks(cb, ac_ref, ar_ref, (2 * pr, 2 * pr + 1))
            xp = xdt[:, pr * V7X_LANES:(pr + 1) * V7X_LANES]
            rhs = jnp.concatenate([jnp.where(first, xp, 0.0), jnp.where(first, 0.0, xp)], axis=0).astype(BF16)
            ys.append(jnp.dot(jnp.concatenate(ms, axis=1), rhs, preferred_element_type=F32))
        h_prev = h_ref[...]
        prev_ref[...] = h_prev
        y = jnp.concatenate(ys, axis=1) + jnp.dot(cm, h_prev.astype(BF16), preferred_element_type=F32) * e_in
        y_ref[...] = y + xs * d_ref[...]
        st = lax.dot_general(bm, (xdt * eout_ref[...]).astype(BF16), TN, preferred_element_type=F32)
        h_ref[...] = e_in[CHUNK - 1:CHUNK, :] * h_prev + st

    return pl.pallas_call(
        body, name=name, grid=(g_n, nc),
        out_shape=[SDS((t, di), F32), SDS((nc, g_n, D_STATE, w), F32)],
        in_specs=[sp["xs"], sp["bm"], sp["cm"], sp["xs"], sp["xs"], sp["xs"], sp["col"], sp["rowv"], sp["lanes"]],
        out_specs=[sp["xs"], sp["prev"]], scratch_shapes=[pltpu.VMEM((D_STATE, w), F32)],
        compiler_params=_cp("parallel", "arbitrary"),
    )(xbc, xbc, xbc, dt_x, ein_x, eout_x, a_col, a_row, d_x)


def _head_sums(v, seg):
    hi = v.astype(BF16)
    lo = (v - hi.astype(F32)).astype(BF16)
    return jnp.dot(hi, seg, preferred_element_type=F32) + jnp.dot(lo, seg, preferred_element_type=F32)


def _head_totals(v, seg):
    part = v[0:8]
    for r in range(8, v.shape[0], 8):
        part = part + v[r:r + 8]
    return jnp.sum(_head_sums(part, seg), axis=0, keepdims=True)


def _ssd_bwd(name, xbc, dt_x, ein_x, eout_x, a_col, a_row, d_x, prev, dy, di, g_n, hpg, p):
    t = xbc.shape[0]
    nc = t // CHUNK
    w = hpg * p
    sp = _ssd_specs(t, di, g_n, hpg, p, True)
    seg = (lax.broadcasted_iota(jnp.int32, (w, V7X_LANES), 0) // p
           == lax.broadcasted_iota(jnp.int32, (w, V7X_LANES), 1)).astype(BF16)

    def body(xs_ref, bm_ref, cm_ref, dt_ref, ein_ref, eout_ref, ac_ref, ar_ref, d_ref, prev_ref, dy_ref,
             seg_ref, dx_ref, dbm_ref, dcm_ref, ddt_ref, dacs_ref, dd_ref, dh_ref):
        @pl.when(pl.program_id(1) == 0)
        def _():
            dh_ref[...] = jnp.zeros_like(dh_ref)
            dd_ref[...] = jnp.zeros_like(dd_ref)

        bm = bm_ref[...].astype(BF16)
        cm = cm_ref[...].astype(BF16)
        cb = lax.dot_general(cm, bm, NT, preferred_element_type=F32)
        first = lax.broadcasted_iota(jnp.int32, (1, V7X_LANES), 1) < p
        last_row = lax.broadcasted_iota(jnp.int32, (CHUNK, 1), 0) == CHUNK - 1
        seg_m = seg_ref[...]
        xs, dy, e_in, e_out, d_skip = xs_ref[...], dy_ref[...], ein_ref[...], eout_ref[...], d_ref[...]
        dt_l = dt_ref[...]
        xdt = xs * dt_l
        h_prev = prev_ref[...]
        h_prev_b = h_prev.astype(BF16)
        dh_next = dh_ref[...]
        dh_next_b = dh_next.astype(BF16)
        dy_e = (dy * e_in).astype(BF16)
        d_cm = lax.dot_general(dy_e, h_prev_b, NT, preferred_element_type=F32)
        dh_ref[...] = e_in[CHUNK - 1:CHUNK, :] * dh_next + lax.dot_general(cm, dy_e, TN, preferred_element_type=F32)
        q = jnp.dot(bm, dh_next_b, preferred_element_type=F32)
        xf = xdt * e_out
        d_bm = lax.dot_general(xf.astype(BF16), dh_next_b, NT, preferred_element_type=F32)
        d_cb = jnp.zeros((CHUNK, CHUNK), F32)
        parts, w_parts = [], []
        for pr in range(hpg // 2):
            lanes = slice(pr * V7X_LANES, (pr + 1) * V7X_LANES)
            lms, ms = _decay_masks(cb, ac_ref, ar_ref, (2 * pr, 2 * pr + 1))
            xp = xdt[:, lanes]
            xp_b = xp.astype(BF16)
            dyp = dy[:, lanes]
            halves = [jnp.where(first, dyp, 0.0).astype(BF16), jnp.where(first, 0.0, dyp).astype(BF16)]
            for lm, half in zip(lms, halves):
                d_cb = d_cb + lax.dot_general(half, xp_b, NT, preferred_element_type=F32) * lm
            dxd = lax.dot_general(jnp.concatenate(ms, axis=0), jnp.concatenate(halves, axis=0), TN,
                                  preferred_element_type=F32)
            stacked = jnp.concatenate([jnp.where(first, xp, 0.0), jnp.where(first, 0.0, xp)], axis=0).astype(BF16)
            y_diag = jnp.dot(jnp.concatenate(ms, axis=1), stacked, preferred_element_type=F32)
            parts.append(dxd)
            w_parts.append(dyp.astype(BF16).astype(F32) * y_diag - xp_b.astype(F32) * dxd)
        d_xdt = jnp.concatenate(parts, axis=1) + q * e_out
        dx_ref[...] = d_xdt * dt_l + dy * d_skip
        ch = jnp.dot(cm, h_prev_b, preferred_element_type=F32)
        qx = q * xf
        s_a = _head_sums(dy * ch * e_in - qx + jnp.concatenate(w_parts, axis=1), seg_m)[:, :hpg]
        d_last = (_head_totals(qx, seg_m)[:, :hpg]
                  + jnp.exp(ac_ref[CHUNK - 1:CHUNK, :]) * _head_totals(dh_next * h_prev, seg_m)[:, :hpg])
        ddt_ref[...] = _head_sums(d_xdt * xs, seg_m)[:, :hpg]
        dacs_ref[...] = s_a + jnp.where(last_row, d_last, 0.0)
        dd_ref[...] += _head_totals(dy * xs, seg_m)[:, :hpg]
        d_cb_b = d_cb.astype(BF16)
        dcm_ref[...] = d_cm + jnp.dot(d_cb_b, bm, preferred_element_type=F32)
        dbm_ref[...] = d_bm + lax.dot_general(d_cb_b, cm, TN, preferred_element_type=F32)

    gn = g_n * D_STATE
    return pl.pallas_call(
        body, name=name, grid=(g_n, nc),
        out_shape=[SDS((t, di), F32), SDS((t, gn), F32), SDS((t, gn), F32), SDS((g_n, t, hpg), F32),
                   SDS((g_n, t, hpg), F32), SDS((g_n, 1, hpg), F32)],
        in_specs=[sp["xs"], sp["bm"], sp["cm"], sp["xs"], sp["xs"], sp["xs"], sp["col"], sp["rowv"],
                  sp["lanes"], sp["prev"], sp["xs"], sp["seg"]],
        out_specs=[sp["xs"], sp["bc"], sp["bc"], sp["col"], sp["col"], sp["head"]],
        scratch_shapes=[pltpu.VMEM((D_STATE, w), F32)],
        compiler_params=_cp("parallel", "arbitrary"),
    )(xbc, xbc, xbc, dt_x, ein_x, eout_x, a_col, a_row, d_x, prev, dy, seg)


def _as3d(a):
    return a.reshape(a.shape[0], -1, a.shape[-1])


def _pair_sum(name, own, recv, core):
    shape = recv.shape
    cols = shape[-1]
    own3, recv3 = own.reshape(8, -1, cols), recv.reshape(4, -1, cols)
    rows = recv3.shape[1]
    tr = _row_tile(rows, cols, 2)

    def body(c_ref, a_ref, b_ref, o_ref):
        o_ref[...] = (a_ref[...].astype(F32) + b_ref[...].astype(F32)).astype(o_ref.dtype)

    blk = pl.BlockSpec((None, tr, cols), lambda q, i, c_ref: (q, i, 0))
    out = pl.pallas_call(
        body, name=name, out_shape=SDS(recv3.shape, recv.dtype),
        grid_spec=pltpu.PrefetchScalarGridSpec(
            num_scalar_prefetch=1, grid=(4, rows // tr),
            in_specs=[pl.BlockSpec((None, tr, cols), lambda q, i, c_ref: (4 * c_ref[0] + q, i, 0)), blk], out_specs=blk),
        compiler_params=_cp("parallel", "parallel"),
    )(core, own3, recv3)
    return out.reshape(shape)


def _adamw(name, w, m, v, parts, layer, prev=None, sel=None):
    lyr, rows, cols = w.shape
    n = len(parts)
    tr = _row_tile(rows, cols)
    np_ = 0 if prev is None else 4
    if sel is None:
        sel = jnp.zeros((1,), jnp.int32)

    def body(sel_ref, *refs):
        w_ref, m_ref, v_ref = refs[:3]
        p_refs = refs[3:3 + n]
        g_ref, d_ref, nm_ref, nv_ref = refs[3 + n + np_:]
        g = p_refs[0][...].astype(F32)
        for r in p_refs[1:]:
            g = g + r[...].astype(F32)
        nm = ADAM_B1 * m_ref[...] + (1.0 - ADAM_B1) * g
        nv = ADAM_B2 * v_ref[...] + (1.0 - ADAM_B2) * (g * g)
        m_hat = nm / (1.0 - ADAM_B1 ** ADAM_STEP)
        v_hat = nv / (1.0 - ADAM_B2 ** ADAM_STEP)
        g_ref[...] = g
        d_ref[...] = -ADAM_LR * (m_hat / (jnp.sqrt(v_hat) + ADAM_EPS) + ADAM_WD * w_ref[...])
        nm_ref[...] = nm
        nv_ref[...] = nv

    lspec = pl.BlockSpec((None, tr, cols), lambda i, s: (layer, i, 0))
    pspecs = [pl.BlockSpec((None, tr, cols), (lambda i, s: (s[0], i, 0)) if q is None else (lambda i, s, q=q: (q, i, 0)))
              for _, q in parts]
    aliases = {} if prev is None else {4 + n + q: q for q in range(4)}
    return pl.pallas_call(
        body, name=name, out_shape=[SDS(w.shape, F32)] * 4,
        grid_spec=pltpu.PrefetchScalarGridSpec(
            num_scalar_prefetch=1, grid=(rows // tr,), in_specs=[lspec] * 3 + pspecs + [ANY] * np_,
            out_specs=[lspec] * 4),
        input_output_aliases=aliases, compiler_params=_cp("parallel"),
    )(sel, w, m, v, *[arr for arr, _ in parts], *(prev or ()))


def _sum8(name, parts):
    rows = parts.shape[1]

    def body(p_ref, o_ref):
        s = p_ref[0]
        for q in range(1, N_DEV):
            s = s + p_ref[q]
        o_ref[...] = s

    return pl.pallas_call(
        body, name=name, grid=(1,), out_shape=SDS((rows, V7X_LANES), F32),
        in_specs=[pl.BlockSpec((N_DEV, rows, V7X_LANES), lambda i: (0, 0, 0))],
        out_specs=pl.BlockSpec((rows, V7X_LANES), lambda i: (0, 0)), compiler_params=_cp("arbitrary"),
    )(parts)


def _pack(vectors, align):
    flat = jnp.concatenate([v.reshape(-1) for v in vectors])
    pad = (-flat.shape[0]) % align
    if pad:
        flat = jnp.concatenate([flat, jnp.zeros((pad,), F32)])
    return flat.reshape(-1, V7X_LANES)


def _unpack(packed, shapes):
    flat = packed.reshape(-1)
    out, o = [], 0
    for s in shapes:
        size = 1
        for dim in s:
            size *= dim
        out.append(flat[o:o + size].reshape(s))
        o += size
    return out


def _residual_epilogue(acc, res):
    return (ALPHA * res + acc,)


def _plain_add_epilogue(acc, res):
    return (res + acc,)


def _gate_epilogue(acc, y, e):
    gate = _sigmoid(acc)
    xn = y + gate * e
    return xn, gate, xn


def _relu2(pre):
    r = jnp.maximum(pre, 0.0)
    return r * r


def _relu2_bwd_epilogue(acc, pre):
    return (acc * (2.0 * jnp.maximum(pre.astype(F32), 0.0)),)


def _tail_fwd(tag, u_a, wts, lng, lnb, p_l, finish_w2):
    y1, y1_b = _ln_fwd(f"ln1_{tag}", u_a, lng[0], lnb[0])
    (pre,) = _mm_fwd(f"mlp1_{tag}", y1_b, wts["w1"], "col", [BF16])
    finish_w2(pre)
    (u_b,) = _mm_fwd(f"mlp2_{tag}", pre, wts["w2"], "row", [F32], _residual_epilogue, (y1,), a_fn=_relu2)
    y2, y2_b = _ln_fwd(f"ln2_{tag}", u_b, lng[1], lnb[1])
    (e,) = _mm_fwd(f"ple_{tag}", p_l, wts["plew"], "col", [F32])
    xn, gate, xn_b = _mm_fwd(f"gate_{tag}", y2_b, wts["gate"], "row", [F32, F32, BF16], _gate_epilogue, (y2, e))
    return xn, xn_b, (u_a, y1_b, pre, u_b, y2_b, e, gate)


def _tail_bwd(tag, dxn, saved, wts, lng, p_l, emit):
    u_a, y1_b, pre, u_b, y2_b, e, gate = saved
    dgpre, de = _ple_bwd(f"ple_bwd_{tag}", dxn, e, gate)
    toks = emit(f"{tag}_ple", dict(gate=_mm_dw(f"gate_dw_{tag}", y2_b, dgpre, "row"),
                                   plew=_mm_dw(f"ple_dw_{tag}", p_l, de, "col")))
    (dy2,) = _mm_dx(f"gate_dx_{tag}", dgpre, wts["gate"], "row", [F32], _plain_add_epilogue, (dxn,), after=toks)
    du_b, du_b16, dg2, db2 = _ln_bwd(f"ln2_bwd_{tag}", u_b, dy2, lng[1])
    toks = emit(f"{tag}_w2", dict(w2=_mm_dw(f"mlp2_dw_{tag}", pre, du_b16, "row", a_fn=_relu2)))
    (dpre,) = _mm_dx(f"mlp2_dx_{tag}", du_b16, wts["w2"], "row", [BF16], _relu2_bwd_epilogue, (pre,), after=toks)
    toks = emit(f"{tag}_w1", dict(w1=_mm_dw(f"mlp1_dw_{tag}", y1_b, dpre, "col")))
    (dy1,) = _mm_dx(f"mlp1_dx_{tag}", dpre, wts["w1"], "col", [F32], _residual_epilogue, (du_b,), after=toks)
    du_a, du_a16, dg1, db1 = _ln_bwd(f"ln1_bwd_{tag}", u_a, dy1, lng[0])
    return du_a, du_a16, [dg1, dg2], [db1, db2]


def _to_slots(a, axis):
    shape = a.shape
    per = shape[axis] // N_DEV
    v = a.reshape(shape[:axis] + (2, 2, 2, per) + shape[axis + 1:])
    perm = (axis + 2, axis, axis + 1) + tuple(range(axis)) + tuple(range(axis + 3, v.ndim))
    v = v.transpose(perm)
    return v.reshape((N_DEV,) + shape[:axis] + (per,) + shape[axis + 1:])


def _pad_lanes(a):
    return jnp.pad(a, [(0, 0)] * (a.ndim - 1) + [(0, V7X_LANES - a.shape[-1])])


def kernel(x, p, pool_w, pool_scale, ssm_in_w, ssm_conv_w, ssm_conv_b, ssm_dt_bias, ssm_a_log, ssm_d, ssm_norm_w, ssm_out_w, mlp_w1, mlp_w2, ln_g, ln_b, ple_w, ple_gate_w, loss_target, m_pool_w, m_pool_scale, m_ssm_in_w, m_ssm_conv_w, m_ssm_conv_b, m_ssm_dt_bias, m_ssm_a_log, m_ssm_d, m_ssm_norm_w, m_ssm_out_w, m_mlp_w1, m_mlp_w2, m_ln_g, m_ln_b, m_ple_w, m_ple_gate_w, v_pool_w, v_pool_scale, v_ssm_in_w, v_ssm_conv_w, v_ssm_conv_b, v_ssm_dt_bias, v_ssm_a_log, v_ssm_d, v_ssm_norm_w, v_ssm_out_w, v_mlp_w1, v_mlp_w2, v_ln_g, v_ln_b, v_ple_w, v_ple_gate_w):
    t, d = x.shape[1:]
    h_n = ssm_dt_bias.shape[-1]
    di_s, cd_s, dp_s, d_s = ssm_norm_w.shape[-1], ssm_conv_b.shape[-1], ssm_in_w.shape[-1], ln_g.shape[-1]
    di, cd, dp = N_DEV * di_s, N_DEV * cd_s, N_DEV * dp_s
    p_dim = di // h_n
    g_n = (cd - di) // (2 * D_STATE)
    hpg = h_n // g_n
    dpp = di + cd + V7X_LANES
    assert h_n <= V7X_LANES and dp == di + cd + h_n and (di + cd) % V7X_LANES == 0
    dt_block = (di + cd) // V7X_LANES
    cg = d // 4
    me = 4 * lax.axis_index("x") + 2 * lax.axis_index("y") + lax.axis_index("c")

    x0, target = x[0], loss_target[0]
    p_l = [p[0, 0].astype(BF16), p[1, 0].astype(BF16)]

    small_shapes = [(CONV_WIDTH, cd_s), (1, cd_s), (1, di_s), (2, 2, d_s), (2, 2, d_s)]
    small = _pack([ssm_conv_w[0], ssm_conv_b, ssm_norm_w, ln_g, ln_b], 8 * V7X_LANES)
    first = [w.astype(BF16) for w in (pool_w[0], mlp_w1[0])]
    own_l0 = [w.astype(BF16) for w in (mlp_w2[0], ple_w[0], ple_gate_w[0])]
    own_ssm = [w.astype(BF16) for w in (ssm_in_w[0], ssm_out_w[0])]
    own_mlp = [w.astype(BF16) for w in (mlp_w1[1], mlp_w2[1], ple_w[1], ple_gate_w[1])]

    def gather_start(tag, own, after):
        lands = [lax.empty((N_DEV,) + w.shape, w.dtype) for w in own]
        return _async_start(f"ag_{tag}_start", _ag_first_copies, (4 * len(own),), own, lands, after)

    def gather_finish(tag, handle, after):
        n = len(handle[2]) // 2
        own, lands = _async_wait(f"ag_{tag}_wait", _ag_first_copies, handle, n, after)
        fwd = _async_start(f"ag_{tag}_forward_start", _ag_forward_copies, (3 * n,), [], lands)
        _, lands = _async_wait(f"ag_{tag}_forward_wait", _ag_forward_copies, fwd, 0)
        return [lax.dynamic_update_slice_in_dim(g, w[None], me, 0) for g, w in zip(lands, own)]

    ag_first = gather_start("first", first + [small], ())
    ag_l0 = gather_start("l0", own_l0, (ag_first[3],))
    ag_ssm = gather_start("ssm", own_ssm, (ag_l0[3],))
    ag_mlp = gather_start("mlp1", own_mlp, (ag_ssm[3],))
    pooled = _pool_windows("pool_fwd", x0, False, after=(ag_mlp[3],))
    pool_g, w1_0, small_g = gather_finish("first", ag_first, (pooled,))
    pool_full = pool_g.transpose(1, 0, 2, 3).reshape(4, cg, cg)
    sm = small_g.reshape(N_DEV, -1)
    o = 0
    parts = []
    for shp in small_shapes:
        size = 1
        for s in shp:
            size *= s
        parts.append(sm[:, o:o + size].reshape((N_DEV,) + shp))
        o += size
    conv_w_full = parts[0].transpose(1, 0, 2).reshape(CONV_WIDTH, cd)
    conv_b_full = parts[1].transpose(1, 0, 2).reshape(1, cd)
    norm_w_full = parts[2].transpose(1, 0, 2).reshape(1, di)
    ln_g_full = parts[3].transpose(1, 2, 0, 3).reshape(2, 2, 1, d)
    ln_b_full = parts[4].transpose(1, 2, 0, 3).reshape(2, 2, 1, d)
    bias128, alog128 = _pad_lanes(ssm_dt_bias), _pad_lanes(ssm_a_log)

    u0, hraw = _pool_mm("pool_mm", pooled, pool_full, pool_scale, x0)
    wts = [dict(w1=w1_0)]

    def finish_l0(after):
        w2_0, plew_0, gate_0 = gather_finish("l0", ag_l0, (after,))
        wts[0].update(w2=w2_0, plew=plew_0, gate=gate_0)

    x1, x1_b, saved0 = _tail_fwd("l0", u0, wts[0], ln_g_full[0], ln_b_full[0], p_l[0], finish_l0)

    in_g, out_g = gather_finish("ssm", ag_ssm, (x1,))
    in_full = jnp.pad(in_g.transpose(1, 0, 2).reshape(d, dp), ((0, 0), (0, dpp - dp)))
    (zx,) = _mm_fwd("in_proj", x1_b, in_full, "plain", [F32])
    xbc = _conv_fwd("conv_fwd", zx, conv_w_full, conv_b_full, di)
    dt, acs, e_in, e_out = _dt_fwd("dt_fwd", zx, bias128, alog128, dt_block)

    dt_x, ein_x, eout_x = _expand_heads("expand_heads", [dt, e_in, e_out], h_n, p_dim)
    d_x = jnp.repeat(ssm_d, p_dim, axis=1)

    def to_col(a):
        return a[:, :h_n].reshape(t, g_n, hpg).transpose(1, 0, 2)

    def to_row(a):
        return a[:, :h_n].reshape(t, g_n, hpg).transpose(1, 2, 0)

    def from_col(a):
        return _pad_lanes(a.transpose(1, 0, 2).reshape(t, h_n))

    a_col, a_row = to_col(acs), to_row(acs)
    y_ssd, prev = _ssd_fwd("ssd_fwd", xbc, dt_x, ein_x, eout_x, a_col, a_row, d_x, di, g_n, hpg, p_dim)
    yn = _gated_rms_fwd("gated_rms_fwd", y_ssd, zx, norm_w_full)
    (u2,) = _mm_fwd("out_proj", yn, out_g, "row", [F32], _residual_epilogue, (x1,))
    w1_1, w2_1, plew_1, gate_1 = gather_finish("mlp1", ag_mlp, (u2,))
    wts.append(dict(w1=w1_1, w2=w2_1, plew=plew_1, gate=gate_1))
    x2, _, saved1 = _tail_fwd("l1", u2, wts[1], ln_g_full[1], ln_b_full[1], p_l[1], lambda after: None)

    core = lax.axis_index("c").astype(jnp.int32).reshape(1)
    chip = (2 * lax.axis_index("x") + lax.axis_index("y")).astype(jnp.int32).reshape(1)
    scattering = {}
    pending = []

    def to_chips(after):
        if not pending:
            return []
        tag, names, handle = pending.pop()
        own, halves = _async_wait(f"rs_{tag}_sibling_wait", _rs_sibling_copies, handle, len(names), after)
        sums = [_pair_sum(f"rs_{tag}_pair_sum_{n}", g, hv, core) for n, g, hv in zip(names, own, halves)]
        lands = [lax.empty((3,) + s.shape[1:], s.dtype) for s in sums]
        handle = _async_start(f"rs_{tag}_start", _rs_chip_copies, (3 * len(sums),), sums, lands)
        scattering[tag] = (names, handle)
        return [handle[3]]

    def emit(tag, grads):
        names, arrays = list(grads), list(grads.values())
        toks = to_chips(tuple(arrays))
        lands = [lax.empty((4,) + g.shape[1:], g.dtype) for g in arrays]
        handle = _async_start(f"rs_{tag}_sibling_start", _rs_sibling_copies, (len(arrays),), arrays, lands, tuple(toks))
        pending.append((tag, names, handle))
        return toks + [handle[3]]

    def collect(tag, after):
        names, handle = scattering.pop(tag)
        sums, thirds = _async_wait(f"rs_{tag}_wait", _rs_chip_copies, handle, len(names), after)
        return {n: (s.reshape(4, -1, s.shape[-1]), r.reshape(3, -1, r.shape[-1])) for n, s, r in zip(names, sums, thirds)}

    dx2, loss_cols = _loss_bwd("loss", x2, target)
    du2, du2_b, dg_1, db_1 = _tail_bwd("l1", dx2, saved1, wts[1], ln_g_full[1], p_l[1], emit)
    toks = emit("ssm_out", dict(out=_mm_dw("out_proj_dw", yn, du2_b, "row")))
    (dyn,) = _mm_dx("out_proj_dx", du2_b, out_g, "row", [F32], after=toks)
    dy_ssd, dz, d_norm_w = _gated_rms_bwd("gated_rms_bwd", y_ssd, zx, norm_w_full, dyn)
    dxs, dbm, dcm, ddt_x, dacs, dd = _ssd_bwd("ssd_bwd", xbc, dt_x, ein_x, eout_x, a_col, a_row, d_x, prev, dy_ssd,
                                              di, g_n, hpg, p_dim)
    draw, d_bias, d_alog = _dt_bwd("dt_bwd", zx, bias128, alog128, from_col(dacs), from_col(ddt_x), dt_block)
    conv_parts = [_conv_bwd(f"conv_bwd_{tag}", zx, conv_w_full, conv_b_full, dact, di, first)
                  for tag, dact, first in (("xs", dxs, 0), ("b", dbm, di), ("c", dcm, di + g_n * D_STATE))]
    d_conv_w = jnp.concatenate([c[1] for c in conv_parts], axis=1)
    d_conv_b = jnp.concatenate([c[2] for c in conv_parts], axis=1)
    dzx = jnp.concatenate([dz] + [c[0] for c in conv_parts] + [draw], axis=1)
    toks = emit("ssm_in", {"in": _to_slots(_mm_dw("in_proj_dw", x1_b, dzx, "plain")[:, :dp], 1)})
    (dx1,) = _mm_dx("in_proj_dx", dzx, in_full, "plain", [F32], _residual_epilogue, (du2,), after=toks)

    du0, _, dg_0, db_0 = _tail_bwd("l0", dx1, saved0, wts[0], ln_g_full[0], p_l[0], emit)
    dh, dpool, d_scale = _pool_bwd_mm("pool_bwd_mm", du0, hraw, pool_full, pool_scale)
    toks = emit("pool", dict(pool=_to_slots(_pool_dw("pool_dw", pooled, dh), 1)))
    grad_x = _pool_windows("pool_bwd", dpool, True, du0, after=tuple(toks))
    toks = to_chips((grad_x,))

    d_ln_g = jnp.stack([jnp.stack(dg_0), jnp.stack(dg_1)]).reshape(2, 2, d)
    d_ln_b = jnp.stack([jnp.stack(db_0), jnp.stack(db_1)]).reshape(2, 2, d)
    partial_shapes = [(CONV_WIDTH, cd), (1, cd), (1, di), (2, 2, d), (2, 2, d), (1, d), (1, h_n), (1, h_n), (1, h_n),
                      (1, d)]
    partial = _pack([d_conv_w, d_conv_b, d_norm_w, d_ln_g, d_ln_b, d_scale, d_bias[:, :h_n], d_alog[:, :h_n],
                     dd.reshape(1, h_n), loss_cols], 8 * V7X_LANES)
    (all_partials,) = _all_gather("ag_small_grads", [_token_add(partial, *toks)])
    tot = _unpack(_sum8("sum_small_grads", all_partials), partial_shapes)
    t_conv_w, t_conv_b, t_norm_w, t_ln_g, t_ln_b, t_scale, t_bias, t_alog, t_dd, t_loss = tot
    loss = jnp.sum(t_loss)

    def mine(a, per):
        return lax.dynamic_slice_in_dim(a, me * per, per, axis=a.ndim - 1)

    small_names = ["ssm_conv_w", "ssm_conv_b", "ssm_norm_w", "ln_g", "ln_b", "pool_scale", "ssm_dt_bias", "ssm_a_log",
                   "ssm_d"]
    small_w = [ssm_conv_w, ssm_conv_b, ssm_norm_w, ln_g, ln_b, pool_scale, ssm_dt_bias, ssm_a_log, ssm_d]
    small_m = [m_ssm_conv_w, m_ssm_conv_b, m_ssm_norm_w, m_ln_g, m_ln_b, m_pool_scale, m_ssm_dt_bias, m_ssm_a_log,
               m_ssm_d]
    small_v = [v_ssm_conv_w, v_ssm_conv_b, v_ssm_norm_w, v_ln_g, v_ln_b, v_pool_scale, v_ssm_dt_bias, v_ssm_a_log,
               v_ssm_d]
    small_grads = [mine(t_conv_w, cd_s), mine(t_conv_b, cd_s), mine(t_norm_w, di_s), mine(t_ln_g, d_s),
                   mine(t_ln_b, d_s), t_scale, t_bias, t_alog, t_dd]
    shapes = [w.shape for w in small_w]
    pk = [_pack(group, 8 * V7X_LANES)[None] for group in (small_w, small_m, small_v, small_grads)]
    res = _adamw("adamw_small", pk[0], pk[1], pk[2], [(pk[3], 0)], 0)
    upd = {}
    for name, vals in zip(small_names, zip(*[_unpack(r, shapes) for r in res])):
        upd[name] = list(vals)

    def update(tag, w, m, v, parts, layer, prev=None):
        own, recv = parts
        return _adamw(f"adamw_{tag}_{layer}", _as3d(w), _as3d(m), _as3d(v),
                      [(own, None), (recv, 0), (recv, 1), (recv, 2)], layer, prev, chip)

    q = collect("l1_ple", (res[0],))
    r_gate = update("ple_gate_w", ple_gate_w, m_ple_gate_w, v_ple_gate_w, q["gate"], 1)
    r_plew = update("ple_w", ple_w, m_ple_w, v_ple_w, q["plew"], 1)
    r_w2 = update("mlp_w2", mlp_w2, m_mlp_w2, v_mlp_w2, collect("l1_w2", (r_gate[0],))["w2"], 1)
    r_w1 = update("mlp_w1", mlp_w1, m_mlp_w1, v_mlp_w1, collect("l1_w1", (r_w2[0],))["w1"], 1)
    r_out = update("ssm_out_w", ssm_out_w, m_ssm_out_w, v_ssm_out_w, collect("ssm_out", (r_w1[0],))["out"], 0)
    r_in = update("ssm_in_w", ssm_in_w, m_ssm_in_w, v_ssm_in_w, collect("ssm_in", (r_out[0],))["in"], 0)
    q = collect("l0_ple", (r_in[0],))
    r_gate = update("ple_gate_w", ple_gate_w, m_ple_gate_w, v_ple_gate_w, q["gate"], 0, r_gate)
    r_plew = update("ple_w", ple_w, m_ple_w, v_ple_w, q["plew"], 0, r_plew)
    r_w2 = update("mlp_w2", mlp_w2, m_mlp_w2, v_mlp_w2, collect("l0_w2", (r_gate[0],))["w2"], 0, r_w2)
    r_w1 = update("mlp_w1", mlp_w1, m_mlp_w1, v_mlp_w1, collect("l0_w1", (r_w2[0],))["w1"], 0, r_w1)
    r_pool = update("pool_w", pool_w, m_pool_w, v_pool_w, collect("pool", (r_w1[0],))["pool"], 0)
    assert not scattering
    large = {"pool_w": (pool_w, r_pool), "ssm_in_w": (ssm_in_w, r_in), "ssm_out_w": (ssm_out_w, r_out),
             "mlp_w1": (mlp_w1, r_w1), "mlp_w2": (mlp_w2, r_w2), "ple_w": (ple_w, r_plew),
             "ple_gate_w": (ple_gate_w, r_gate)}
    for name, (w, rs) in large.items():
        upd[name] = [r.reshape(w.shape) for r in rs]

    order = ["pool_w", "pool_scale", "ssm_in_w", "ssm_conv_w", "ssm_conv_b", "ssm_dt_bias", "ssm_a_log", "ssm_d",
             "ssm_norm_w", "ssm_out_w", "mlp_w1", "mlp_w2", "ln_g", "ln_b", "ple_w", "ple_gate_w"]
    out = [loss, grad_x[None]]
    for k in range(4):
        out += [upd[name][k] for name in order]
    return tuple(out)
```

```python
import jax
import jax.numpy as jnp
from jax import lax
from jax.experimental import pallas as pl
from jax.experimental.pallas import tpu as pltpu

F32 = jnp.float32
BF16 = jnp.bfloat16
SDS = jax.ShapeDtypeStruct
MESH = pl.DeviceIdType.MESH
ANY = pl.BlockSpec(memory_space=pl.ANY)

N_DEV = 8
DEPTH = 2
ALPHA = (2.0 * DEPTH) ** 0.25
LN_EPS = 1e-5
RMS_EPS = 1e-5
POOL_WINDOW_LOG2 = (1, 2, 3, 4)
D_STATE = 128
CHUNK = 128
CONV_WIDTH = 4
ADAM_LR = 0.001
ADAM_B1 = 0.9
ADAM_B2 = 0.999
ADAM_EPS = 1e-08
ADAM_WD = 0.01
ADAM_STEP = 10

V7X_LANES = 128
V7X_VMEM_LIMIT = 48 * 1024 * 1024


def _cp(*sem):
    return pltpu.CompilerParams(dimension_semantics=sem, vmem_limit_bytes=V7X_VMEM_LIMIT)


def _pick(dim, cap):
    if dim <= cap:
        return dim
    best = None
    for t in range(V7X_LANES, cap + 1, V7X_LANES):
        if dim % t == 0:
            best = t
    assert best is not None, (dim, cap)
    return best


def _row_tile(rows, cols, itemsize=4, target=1 << 20):
    t = rows
    while t % 2 == 0 and t // 2 >= 16 and (t // 2) % 16 == 0 and t * cols * itemsize > target:
        t //= 2
    return t


def _all_gather(name, shards, after=()):
    n, na = len(shards), len(after)

    def body(*refs):
        ins, outs = refs[:n], refs[n + na:2 * n + na]
        send_sems, recv_sems, local_sems = refs[2 * n + na:]
        x, y, c = lax.axis_index("x"), lax.axis_index("y"), lax.axis_index("c")
        me, sibling = (x, y, c), (x, y, 1 - c)
        chips = [(1 - x, y), (x, 1 - y), (1 - x, 1 - y)]

        def copy(a, k, block, to, src=None):
            dst = outs[a].at[4 * block[0] + 2 * block[1] + block[2]]
            return pltpu.make_async_remote_copy(
                src_ref=dst if src is None else src, dst_ref=dst, send_sem=send_sems.at[a, k],
                recv_sem=recv_sems.at[a, k], device_id=to, device_id_type=MESH)

        mine = [pltpu.make_async_copy(ins[a], outs[a].at[4 * x + 2 * y + c], local_sems.at[a]) for a in range(n)]
        for cp in mine:
            cp.start()
        first = []
        for a in range(n):
            first.append(copy(a, 0, me, sibling, src=ins[a]))
            first += [copy(a, 1 + j, me, (*chip, c), src=ins[a]) for j, chip in enumerate(chips)]
        for cp in first:
            cp.start()
        passed = []
        for j, chip in enumerate(chips):
            for a in range(n):
                copy(a, 1 + j, (*chip, c), me).wait_recv()
                fwd = copy(a, 4 + j, (*chip, c), sibling)
                fwd.start()
                passed.append(fwd)
        for a in range(n):
            copy(a, 0, sibling, me).wait_recv()
            for j, chip in enumerate(chips):
                copy(a, 4 + j, (*chip, 1 - c), me).wait_recv()
        for cp in first + passed:
            cp.wait_send()
        for cp in mine:
            cp.wait()

    return pl.pallas_call(
        body, name=name,
        out_shape=[SDS((N_DEV,) + s.shape, s.dtype) for s in shards],
        in_specs=[ANY] * (n + na), out_specs=[ANY] * n,
        scratch_shapes=[pltpu.SemaphoreType.DMA((n, 7)), pltpu.SemaphoreType.DMA((n, 7)),
                        pltpu.SemaphoreType.DMA((n,))],
    )(*shards, *after)


HBM_SPEC = pl.BlockSpec(memory_space=pltpu.HBM)
SEM_SPEC = pl.BlockSpec(memory_space=pltpu.SEMAPHORE)
EFFECT = pltpu.SideEffectType.DATAFLOW_SIDE_EFFECTING


def _ag_first_copies(ins, lands, send_sems, recv_sems):
    x, y, c = lax.axis_index("x"), lax.axis_index("y"), lax.axis_index("c")
    targets = [(x, y, 1 - c), (1 - x, y, c), (x, 1 - y, c), (1 - x, 1 - y, c)]
    return [pltpu.make_async_remote_copy(
        src_ref=ins[a], dst_ref=lands[a].at[4 * x + 2 * y + c], send_sem=send_sems.at[4 * a + k],
        recv_sem=recv_sems.at[4 * a + k], device_id=to, device_id_type=MESH)
        for a in range(len(ins)) for k, to in enumerate(targets)]


def _ag_forward_copies(ins, lands, send_sems, recv_sems):
    x, y, c = lax.axis_index("x"), lax.axis_index("y"), lax.axis_index("c")
    cps = []
    for a in range(len(lands)):
        for j, (px, py) in enumerate([(1 - x, y), (x, 1 - y), (1 - x, 1 - y)]):
            blk = lands[a].at[4 * px + 2 * py + c]
            cps.append(pltpu.make_async_remote_copy(
                src_ref=blk, dst_ref=blk, send_sem=send_sems.at[3 * a + j], recv_sem=recv_sems.at[3 * a + j],
                device_id=(x, y, 1 - c), device_id_type=MESH))
    return cps


def _rs_sibling_copies(ins, lands, send_sems, recv_sems):
    x, y, c = lax.axis_index("x"), lax.axis_index("y"), lax.axis_index("c")
    return [pltpu.make_async_remote_copy(
        src_ref=ins[a].at[2 * q + 1 - c], dst_ref=lands[a].at[q], send_sem=send_sems.at[4 * a + q],
        recv_sem=recv_sems.at[4 * a + q], device_id=(x, y, 1 - c), device_id_type=MESH)
        for a in range(len(ins)) for q in range(4)]


def _rs_chip_copies(ins, lands, send_sems, recv_sems):
    x, y, c = lax.axis_index("x"), lax.axis_index("y"), lax.axis_index("c")
    cps = []
    for a in range(len(ins)):
        for j, (px, py) in enumerate([(1 - x, y), (x, 1 - y), (1 - x, 1 - y)]):
            cps.append(pltpu.make_async_remote_copy(
                src_ref=ins[a].at[2 * px + py], dst_ref=lands[a].at[j], send_sem=send_sems.at[3 * a + j],
                recv_sem=recv_sems.at[3 * a + j], device_id=(px, py, c), device_id_type=MESH))
    return cps


def _async_start(name, build, sem_shape, ins, lands, after=()):
    arrays = [*ins, *lands]
    n_i, n_t, n_a = len(ins), len(arrays), len(after)

    def body(*refs):
        outs = refs[n_t + n_a:]
        for cp in build(refs[:n_i], refs[n_i:n_t], outs[0], outs[1]):
            cp.start()
        outs[-1][...] = jnp.zeros_like(outs[-1])

    res = pl.pallas_call(
        body, name=name,
        out_shape=(pltpu.SemaphoreType.DMA(sem_shape), pltpu.SemaphoreType.DMA(sem_shape),
                   *[pltpu.HBM(a.shape, a.dtype) for a in arrays], SDS((8, V7X_LANES), F32)),
        in_specs=[HBM_SPEC] * n_t + [ANY] * n_a,
        out_specs=(SEM_SPEC, SEM_SPEC, *[HBM_SPEC] * n_t, pl.BlockSpec(memory_space=pltpu.VMEM)),
        input_output_aliases={i: 2 + i for i in range(n_t)},
        compiler_params=pltpu.CompilerParams(has_side_effects=EFFECT),
    )(*[pltpu.with_memory_space_constraint(a, pltpu.HBM) for a in arrays], *after)
    return res[0], res[1], list(res[2:2 + n_t]), res[-1]


def _async_wait(name, build, handle, n_i, after=()):
    send_sems, recv_sems, arrays, _ = handle
    n_t, n_a = len(arrays), len(after)

    def body(*refs):
        for cp in build(refs[:n_i], refs[n_i:n_t], refs[n_t], refs[n_t + 1]):
            cp.wait_send()
            cp.wait_recv()

    res = pl.pallas_call(
        body, name=name, out_shape=tuple(pltpu.HBM(a.shape, a.dtype) for a in arrays),
        in_specs=[HBM_SPEC] * n_t + [SEM_SPEC, SEM_SPEC] + [ANY] * n_a, out_specs=tuple([HBM_SPEC] * n_t),
        input_output_aliases={i: i for i in range(n_t)},
        compiler_params=pltpu.CompilerParams(has_side_effects=EFFECT),
    )(*arrays, send_sems, recv_sems, *after)
    return list(res[:n_i]), list(res[n_i:])


def _mm_core(name, a, b, *, grid, a_spec, b_spec, dims, acc_shape, outs, out_spec, epilogue=None, extras=(),
             extra_specs=(), a_fn=None, after=()):
    nk = grid[2]
    ne, no, na = len(extras), len(outs), len(after)

    def body(a_ref, b_ref, *rest):
        e_refs, o_refs, acc = rest[:ne], rest[ne + na:ne + na + no], rest[ne + na + no]
        k = pl.program_id(2)

        def product():
            lhs = a_ref[...] if a_fn is None else a_fn(a_ref[...])
            return lax.dot_general(lhs.astype(BF16), b_ref[...].astype(BF16), dims, preferred_element_type=F32)

        @pl.when(k == 0)
        def _():
            acc[...] = product()

        @pl.when(k > 0)
        def _():
            acc[...] += product()

        @pl.when(k == nk - 1)
        def _():
            r = acc[...]
            vals = epilogue(r, *[e[...] for e in e_refs]) if epilogue is not None else (r,)
            for o, v in zip(o_refs, vals):
                o[...] = v.astype(o.dtype)

    res = pl.pallas_call(
        body, name=name, grid=grid, out_shape=list(outs),
        in_specs=[a_spec, b_spec, *extra_specs, *[ANY] * na], out_specs=[out_spec] * no,
        scratch_shapes=[pltpu.VMEM(acc_shape, F32)],
        compiler_params=_cp("parallel", "parallel", "arbitrary"),
    )(a, b, *extras, *after)
    return res


NN = (((1,), (0,)), ((), ()))
NT = (((1,), (1,)), ((), ()))
TN = (((0,), (0,)), ((), ()))


def _w_dims(w, kind):
    if kind == "col":
        return w.shape[1], N_DEV * w.shape[2], w.shape[1], w.shape[2]
    if kind == "row":
        return N_DEV * w.shape[1], w.shape[2], w.shape[1], w.shape[2]
    return w.shape[0], w.shape[1], w.shape[0], w.shape[1]


MM_VMEM_BUDGET = 36 * 1024 * 1024


def _row_block(m, tn, tk, a, out_dtypes, extras):
    for tm in (_pick(m, 2048), _pick(m, 1024), _pick(m, 512)):
        per_out = sum(jnp.dtype(dt).itemsize for dt in out_dtypes) + sum(e.dtype.itemsize for e in extras)
        used = tm * tn * (4 + 2 * per_out) + 2 * (tm * tk * a.dtype.itemsize + tk * tn * 2)
        if used <= MM_VMEM_BUDGET:
            return tm
    return tm


def _mm_fwd(name, a, w, kind, out_dtypes, epilogue=None, extras=(), a_fn=None, after=()):
    if kind == "row":
        w, kind = w.reshape(-1, w.shape[-1]), "plain"
    m, k_dim = a.shape
    kk, n, ks, ns = _w_dims(w, kind)
    assert kk == k_dim
    if kind == "col":
        tn, tk = _pick(ns, 1024), _pick(kk, 512)
        nb = ns // tn
        b_spec = pl.BlockSpec((None, tk, tn), lambda i, j, k: (j // nb, k, j % nb))
    else:
        tn, tk = _pick(n, 1152), _pick(kk, 512)
        b_spec = pl.BlockSpec((tk, tn), lambda i, j, k: (k, j))
    tm = _row_block(m, tn, tk, a, out_dtypes, extras)
    mn_spec = pl.BlockSpec((tm, tn), lambda i, j, k: (i, j))
    return _mm_core(
        name, a, w, grid=(m // tm, n // tn, kk // tk),
        a_spec=pl.BlockSpec((tm, tk), lambda i, j, k: (i, k)), b_spec=b_spec, dims=NN, acc_shape=(tm, tn),
        outs=[SDS((m, n), dt) for dt in out_dtypes], out_spec=mn_spec, epilogue=epilogue, extras=extras,
        extra_specs=[mn_spec] * len(extras), a_fn=a_fn, after=after)


def _mm_dx(name, dy, w, kind, out_dtypes, epilogue=None, extras=(), after=()):
    if kind == "row":
        w, kind = w.reshape(-1, w.shape[-1]), "plain"
    m, n_dim = dy.shape
    kk, n, ks, ns = _w_dims(w, kind)
    assert n == n_dim
    if kind == "col":
        tn, tk = _pick(kk, 1024), _pick(ns, 512)
        kb = ns // tk
        b_spec = pl.BlockSpec((None, tn, tk), lambda i, j, k: (k // kb, j, k % kb))
    else:
        tn, tk = _pick(kk, 1024), _pick(n, 1152)
        b_spec = pl.BlockSpec((tn, tk), lambda i, j, k: (j, k))
    tm = _row_block(m, tn, tk, dy, out_dtypes, extras)
    mk_spec = pl.BlockSpec((tm, tn), lambda i, j, k: (i, j))
    return _mm_core(
        name, dy, w, grid=(m // tm, kk // tn, n // tk),
        a_spec=pl.BlockSpec((tm, tk), lambda i, j, k: (i, k)), b_spec=b_spec, dims=NT, acc_shape=(tm, tn),
        outs=[SDS((m, kk), dt) for dt in out_dtypes], out_spec=mk_spec, epilogue=epilogue, extras=extras,
        extra_specs=[mk_spec] * len(extras), after=after)


def _mm_dw(name, a, dy, kind, a_fn=None, after=()):
    m, kk = a.shape
    n = dy.shape[1]
    tk = _pick(m, 512)
    if kind == "col":
        ns = n // N_DEV
        tm, tn = _pick(kk, 1024), _pick(ns, 1024)
        nb = ns // tn
        out = SDS((N_DEV, kk, ns), BF16)
        out_spec = pl.BlockSpec((None, tm, tn), lambda i, j, k: (j // nb, i, j % nb))
    else:
        tm, tn = _pick(kk, 1024), _pick(n, 1152)
        out = SDS((kk, n), BF16)
        out_spec = pl.BlockSpec((tm, tn), lambda i, j, k: (i, j))
    res = _mm_core(
        name, a, dy, grid=(kk // tm, n // tn, m // tk),
        a_spec=pl.BlockSpec((tk, tm), lambda i, j, k: (k, i)), b_spec=pl.BlockSpec((tk, tn), lambda i, j, k: (k, j)),
        dims=TN, acc_shape=(tm, tn), outs=[out], out_spec=out_spec, a_fn=a_fn, after=after)[0]
    return res.reshape(N_DEV, kk // N_DEV, n) if kind == "row" else res


def _rowwise(name, fn, ins, outs, rows, tile):
    arrays, specs = [], []
    for arr, kind in ins:
        arrays.append(arr)
        if kind == "row":
            specs.append(pl.BlockSpec((tile, arr.shape[1]), lambda i: (i, 0)))
        elif kind == "vec":
            specs.append(pl.BlockSpec(arr.shape, lambda i, nd=arr.ndim: (0,) * nd))
        else:
            specs.append(kind)
    out_shapes, out_specs, kinds = [], [], []
    for cols, dt, kind in outs:
        kinds.append(kind)
        if kind == "row":
            out_shapes.append(SDS((rows, cols), dt))
            out_specs.append(pl.BlockSpec((tile, cols), lambda i: (i, 0)))
        else:
            out_shapes.append(SDS((1, cols), F32))
            out_specs.append(pl.BlockSpec((1, cols), lambda i: (0, 0)))
    ni = len(arrays)
    has_acc = "acc" in kinds

    def body(*refs):
        vals = fn(*[r[...] for r in refs[:ni]])
        i = pl.program_id(0)
        for o, v, kind in zip(refs[ni:], vals, kinds):
            if kind == "row":
                o[...] = v.astype(o.dtype)
            else:
                @pl.when(i == 0)
                def _(o=o):
                    o[...] = jnp.zeros_like(o)

                o[...] += v

    return pl.pallas_call(
        body, name=name, grid=(rows // tile,), out_shape=out_shapes, in_specs=specs, out_specs=out_specs,
        compiler_params=_cp("arbitrary" if has_acc else "parallel"),
    )(*arrays)


def _ln_fwd(name, u, g, b):
    d = u.shape[1]

    def fn(u, g, b):
        mu = jnp.mean(u, axis=1, keepdims=True)
        xc = u - mu
        var = jnp.mean(xc * xc, axis=1, keepdims=True)
        y = xc * lax.rsqrt(var + LN_EPS) * g + b
        return y, y

    return _rowwise(name, fn, [(u, "row"), (g, "vec"), (b, "vec")], [(d, F32, "row"), (d, BF16, "row")], u.shape[0], 256)


def _ln_bwd(name, u, dy, g):
    d = u.shape[1]

    def fn(u, dy, g):
        mu = jnp.mean(u, axis=1, keepdims=True)
        xc = u - mu
        var = jnp.mean(xc * xc, axis=1, keepdims=True)
        rstd = lax.rsqrt(var + LN_EPS)
        xhat = xc * rstd
        dxhat = dy * g
        m1 = jnp.mean(dxhat, axis=1, keepdims=True)
        m2 = jnp.mean(dxhat * xhat, axis=1, keepdims=True)
        du = rstd * (dxhat - m1 - xhat * m2)
        return du, du, jnp.sum(dy * xhat, axis=0, keepdims=True), jnp.sum(dy, axis=0, keepdims=True)

    return _rowwise(name, fn, [(u, "row"), (dy, "row"), (g, "vec")],
                    [(d, F32, "row"), (d, BF16, "row"), (d, F32, "acc"), (d, F32, "acc")], u.shape[0], 256)


def _loss_bwd(name, y, target):
    d = y.shape[1]

    def fn(y, t):
        e = y - t
        return e * (1.0 / d), jnp.sum(e * e, axis=0, keepdims=True) * (0.5 / d)

    return _rowwise(name, fn, [(y, "row"), (target, "row")], [(d, F32, "row"), (d, F32, "acc")], y.shape[0], 256)


def _ple_bwd(name, dx, e, gate):
    d = dx.shape[1]

    def fn(dx, e, gate):
        return dx * e * gate * (1.0 - gate), dx * gate

    return _rowwise(name, fn, [(dx, "row"), (e, "row"), (gate, "row")], [(d, BF16, "row"), (d, BF16, "row")],
                    dx.shape[0], 256)


def _sigmoid(v):
    return 1.0 / (1.0 + jnp.exp(-v))


def _gated_rms_fwd(name, y, zx, norm_w):
    di = y.shape[1]

    def fn(y, z, w):
        yg = y * (z * _sigmoid(z))
        r = lax.rsqrt(jnp.mean(yg * yg, axis=1, keepdims=True) + RMS_EPS)
        return (yg * r * w,)

    z_spec = pl.BlockSpec((128, di), lambda i: (i, 0))
    return _rowwise(name, fn, [(y, "row"), (zx, z_spec), (norm_w, "vec")], [(di, BF16, "row")], y.shape[0], 128)[0]


def _gated_rms_bwd(name, y, zx, norm_w, dout):
    di = y.shape[1]

    def fn(y, z, w, dout):
        sg = _sigmoid(z)
        sz = z * sg
        yg = y * sz
        r = lax.rsqrt(jnp.mean(yg * yg, axis=1, keepdims=True) + RMS_EPS)
        dn = dout * w
        dyg = r * (dn - yg * (r * r) * jnp.mean(dn * yg, axis=1, keepdims=True))
        dy = dyg * sz
        dz = dyg * y * (sg * (1.0 + z * (1.0 - sg)))
        return dy, dz, jnp.sum(dout * yg * r, axis=0, keepdims=True)

    z_spec = pl.BlockSpec((128, di), lambda i: (i, 0))
    return _rowwise(name, fn, [(y, "row"), (zx, z_spec), (norm_w, "vec"), (dout, "row")],
                    [(di, F32, "row"), (di, BF16, "row"), (di, F32, "acc")], y.shape[0], 128)


def _shift_down(v, j, row):
    return jnp.where(row >= j, pltpu.roll(v, j, 0), 0.0)


def _shift_up(v, j, row):
    t = v.shape[0]
    return jnp.where(row < t - j, pltpu.roll(v, t - j, 0), 0.0)


def _pool_select(parts, g):
    return jnp.where(g == 0, parts[0], jnp.where(g == 1, parts[1], jnp.where(g == 2, parts[2], parts[3])))


def _pool_windows(name, x, transpose, scale_by=None, after=()):
    t, d = x.shape
    cg = d // 4
    cw = V7X_LANES
    per = cg // cw

    def body(*refs):
        x_ref, o_ref = refs[0], refs[-1]
        g = pl.program_id(0) // per
        xv = x_ref[...]
        row = lax.broadcasted_iota(jnp.int32, (t, 1), 0)
        cnt = jnp.minimum(row + 1, jnp.left_shift(2, g)).astype(F32)
        s = xv / cnt if transpose else xv
        parts = []
        for lg in POOL_WINDOW_LOG2:
            j = 1 << (lg - 1)
            s = s + (_shift_up(s, j, row) if transpose else _shift_down(s, j, row))
            parts.append(s)
        sel = _pool_select(parts, g)
        if transpose:
            o_ref[...] = ALPHA * refs[1][...] + sel - xv
        else:
            o_ref[...] = (sel / cnt - xv).astype(o_ref.dtype)

    col = pl.BlockSpec((t, cw), lambda j: (0, j))
    ins = [x] if scale_by is None else [x, scale_by]
    return pl.pallas_call(
        body, name=name, grid=(d // cw,), out_shape=SDS((t, d), F32 if transpose else BF16),
        in_specs=[col] * len(ins) + [ANY] * len(after), out_specs=col, compiler_params=_cp("parallel"),
    )(*ins, *after)


def _pool_mm(name, pooled, w, scale, x):
    t, d = x.shape
    cg = d // 4
    tm = _pick(t, 1024)

    def body(p_ref, w_ref, s_ref, x_ref, u_ref, h_ref):
        h = jnp.dot(p_ref[...], w_ref[...], preferred_element_type=F32)
        h_ref[...] = h
        u_ref[...] = ALPHA * x_ref[...] + h * s_ref[...]

    blk = pl.BlockSpec((tm, cg), lambda g, i: (i, g))
    return pl.pallas_call(
        body, name=name, grid=(4, t // tm), out_shape=[SDS((t, d), F32), SDS((t, d), F32)],
        in_specs=[blk, pl.BlockSpec((None, cg, cg), lambda g, i: (g, 0, 0)), pl.BlockSpec((1, cg), lambda g, i: (0, g)),
                  blk],
        out_specs=[blk, blk], compiler_params=_cp("parallel", "parallel"),
    )(pooled, w, scale, x)


def _pool_bwd_mm(name, du, hraw, w, scale):
    t, d = du.shape
    cg = d // 4
    tm = _pick(t, 1024)

    def body(du_ref, h_ref, w_ref, s_ref, dh_ref, dp_ref, ds_ref):
        @pl.when(pl.program_id(1) == 0)
        def _():
            ds_ref[...] = jnp.zeros_like(ds_ref)

        duv = du_ref[...]
        ds_ref[...] += jnp.sum(duv * h_ref[...], axis=0, keepdims=True)
        dh = (duv * s_ref[...]).astype(BF16)
        dh_ref[...] = dh
        dp_ref[...] = lax.dot_general(dh, w_ref[...], NT, preferred_element_type=F32)

    blk = pl.BlockSpec((tm, cg), lambda g, i: (i, g))
    vec = pl.BlockSpec((1, cg), lambda g, i: (0, g))
    return pl.pallas_call(
        body, name=name, grid=(4, t // tm), out_shape=[SDS((t, d), BF16), SDS((t, d), F32), SDS((1, d), F32)],
        in_specs=[blk, blk, pl.BlockSpec((None, cg, cg), lambda g, i: (g, 0, 0)), vec],
        out_specs=[blk, blk, vec], compiler_params=_cp("parallel", "arbitrary"),
    )(du, hraw, w, scale)


def _pool_dw(name, pooled, dh):
    t, d = pooled.shape
    cg = d // 4
    tk = _pick(t, 512)
    nk = t // tk

    def body(p_ref, dh_ref, o_ref, acc):
        k = pl.program_id(1)

        @pl.when(k == 0)
        def _():
            acc[...] = jnp.zeros_like(acc)

        acc[...] += lax.dot_general(p_ref[...], dh_ref[...], TN, preferred_element_type=F32)

        @pl.when(k == nk - 1)
        def _():
            o_ref[...] = acc[...].astype(o_ref.dtype)

    blk = pl.BlockSpec((tk, cg), lambda g, k: (k, g))
    return pl.pallas_call(
        body, name=name, grid=(4, nk), out_shape=SDS((4, cg, cg), BF16), in_specs=[blk, blk],
        out_specs=pl.BlockSpec((None, cg, cg), lambda g, k: (g, 0, 0)), scratch_shapes=[pltpu.VMEM((cg, cg), F32)],
        compiler_params=_cp("parallel", "arbitrary"),
    )(pooled, dh)


def _conv_pre(u, w_ref, b_ref, row):
    pre = b_ref[...] + _shift_down(u, 3, row) * w_ref[0:1, :]
    pre = pre + _shift_down(u, 2, row) * w_ref[1:2, :]
    pre = pre + _shift_down(u, 1, row) * w_ref[2:3, :]
    return pre + u * w_ref[3:4, :]


def _conv_fwd(name, zx, conv_w, conv_b, di):
    t = zx.shape[0]
    cd = conv_w.shape[1]
    cw = _pick(cd, 256)
    off = di // cw

    def body(u_ref, w_ref, b_ref, o_ref):
        row = lax.broadcasted_iota(jnp.int32, (t, 1), 0)
        pre = _conv_pre(u_ref[...], w_ref, b_ref, row)
        o_ref[...] = pre * _sigmoid(pre)

    return pl.pallas_call(
        body, name=name, grid=(cd // cw,), out_shape=SDS((t, cd), F32),
        in_specs=[pl.BlockSpec((t, cw), lambda j: (0, off + j)), pl.BlockSpec((CONV_WIDTH, cw), lambda j: (0, j)),
                  pl.BlockSpec((1, cw), lambda j: (0, j))],
        out_specs=pl.BlockSpec((t, cw), lambda j: (0, j)), compiler_params=_cp("parallel"),
    )(zx, conv_w, conv_b)


def _conv_bwd(name, zx, conv_w, conv_b, dact, di, first):
    t, cd = dact.shape
    cw = _pick(cd, 256)
    off, woff = (di + first) // cw, first // cw

    def body(u_ref, w_ref, b_ref, da_ref, du_ref, dw_ref, db_ref):
        row = lax.broadcasted_iota(jnp.int32, (t, 1), 0)
        u = u_ref[...]
        pre = _conv_pre(u, w_ref, b_ref, row)
        sg = _sigmoid(pre)
        dpre = da_ref[...] * (sg * (1.0 + pre * (1.0 - sg)))
        du = dpre * w_ref[3:4, :]
        for j in (1, 2, 3):
            du = du + _shift_up(dpre, j, row) * w_ref[3 - j:4 - j, :]
            dw_ref[3 - j:4 - j, :] = jnp.sum(dpre * _shift_down(u, j, row), axis=0, keepdims=True)
        dw_ref[3:4, :] = jnp.sum(dpre * u, axis=0, keepdims=True)
        db_ref[...] = jnp.sum(dpre, axis=0, keepdims=True)
        du_ref[...] = du.astype(du_ref.dtype)

    wspec = pl.BlockSpec((CONV_WIDTH, cw), lambda j: (0, j))
    bspec = pl.BlockSpec((1, cw), lambda j: (0, j))
    ospec = pl.BlockSpec((t, cw), lambda j: (0, j))
    return pl.pallas_call(
        body, name=name, grid=(cd // cw,), out_shape=[SDS((t, cd), BF16), SDS((CONV_WIDTH, cd), F32), SDS((1, cd), F32)],
        in_specs=[pl.BlockSpec((t, cw), lambda j: (0, off + j)), pl.BlockSpec((CONV_WIDTH, cw), lambda j: (0, woff + j)),
                  pl.BlockSpec((1, cw), lambda j: (0, woff + j)), ospec],
        out_specs=[ospec, wspec, bspec], compiler_params=_cp("parallel"),
    )(zx, conv_w, conv_b, dact)


def _expand_heads(name, arrays, h_n, p):
    t = arrays[0].shape[0]
    n = len(arrays)
    w = _pick(h_n * p, 512)

    def body(*refs):
        j = pl.program_id(0)
        head = lax.broadcasted_iota(jnp.int32, (V7X_LANES, w), 0)
        lane = lax.broadcasted_iota(jnp.int32, (V7X_LANES, w), 1)
        spread = (head == j * (w // p) + lane // p).astype(BF16)
        for a_ref, o_ref in zip(refs[:n], refs[n:]):
            rest = a_ref[...]
            out = jnp.zeros((t, w), F32)
            for _ in range(3):
                piece = rest.astype(BF16)
                out = out + jnp.dot(piece, spread, preferred_element_type=F32)
                rest = rest - piece.astype(F32)
            o_ref[...] = out

    full = pl.BlockSpec((t, V7X_LANES), lambda j: (0, 0))
    return pl.pallas_call(
        body, name=name, grid=(h_n * p // w,), out_shape=[SDS((t, h_n * p), F32)] * n, in_specs=[full] * n,
        out_specs=[pl.BlockSpec((t, w), lambda j: (0, j))] * n, compiler_params=_cp("parallel"),
    )(*arrays)


def _softplus(v):
    return jnp.maximum(v, 0.0) + jnp.log(1.0 + jnp.exp(-jnp.abs(v)))


def _dt_fwd(name, zx, bias, a_log, col_block):
    t = zx.shape[0]

    def body(r_ref, b_ref, al_ref, dt_ref, acs_ref, ein_ref, eout_ref):
        row = lax.broadcasted_iota(jnp.int32, (t, 1), 0) % CHUNK
        dt = _softplus(r_ref[...] + b_ref[...])
        da = dt * (-jnp.exp(al_ref[...]))
        s, r = da, da
        j = 1
        while j < CHUNK:
            s = s + jnp.where(row >= j, pltpu.roll(s, j, 0), 0.0)
            r = r + jnp.where(row < CHUNK - j, pltpu.roll(r, t - j, 0), 0.0)
            j *= 2
        dt_ref[...] = dt
        acs_ref[...] = s
        ein_ref[...] = jnp.exp(s)
        eout_ref[...] = jnp.exp(r - da)

    vec = pl.BlockSpec((1, V7X_LANES), lambda i: (0, 0))
    full = pl.BlockSpec((t, V7X_LANES), lambda i: (0, 0))
    return pl.pallas_call(
        body, name=name, grid=(1,), out_shape=[SDS((t, V7X_LANES), F32)] * 4,
        in_specs=[pl.BlockSpec((t, V7X_LANES), lambda i: (0, col_block)), vec, vec], out_specs=[full] * 4,
        compiler_params=_cp("arbitrary"),
    )(zx, bias, a_log)


def _dt_bwd(name, zx, bias, a_log, d_acs, d_dt, col_block):
    t = zx.shape[0]

    def body(r_ref, b_ref, al_ref, da_ref, dd_ref, draw_ref, db_ref, dal_ref):
        row = lax.broadcasted_iota(jnp.int32, (t, 1), 0) % CHUNK
        pre = r_ref[...] + b_ref[...]
        dt = _softplus(pre)
        a = -jnp.exp(al_ref[...])
        s = da_ref[...]
        j = 1
        while j < CHUNK:
            s = s + jnp.where(row < CHUNK - j, pltpu.roll(s, t - j, 0), 0.0)
            j *= 2
        ddt = dd_ref[...] + s * a
        dal_ref[...] = jnp.sum(s * dt, axis=0, keepdims=True) * a
        draw = ddt * _sigmoid(pre)
        db_ref[...] = jnp.sum(draw, axis=0, keepdims=True)
        draw_ref[...] = draw.astype(draw_ref.dtype)

    vec = pl.BlockSpec((1, V7X_LANES), lambda i: (0, 0))
    full = pl.BlockSpec((t, V7X_LANES), lambda i: (0, 0))
    return pl.pallas_call(
        body, name=name, grid=(1,), out_shape=[SDS((t, V7X_LANES), BF16), SDS((1, V7X_LANES), F32), SDS((1, V7X_LANES), F32)],
        in_specs=[pl.BlockSpec((t, V7X_LANES), lambda i: (0, col_block)), vec, vec, full, full],
        out_specs=[full, vec, vec], compiler_params=_cp("arbitrary"),
    )(zx, bias, a_log, d_acs, d_dt)


def _ssd_specs(t, di, g_n, hpg, p, rev):
    nc = t // CHUNK
    w = hpg * p
    nb = di // D_STATE

    def cc(c):
        return nc - 1 - c if rev else c

    return dict(
        xs=pl.BlockSpec((CHUNK, w), lambda g, c: (cc(c), g)),
        bm=pl.BlockSpec((CHUNK, D_STATE), lambda g, c: (cc(c), nb + g)),
        cm=pl.BlockSpec((CHUNK, D_STATE), lambda g, c: (cc(c), nb + g_n + g)),
        col=pl.BlockSpec((None, CHUNK, hpg), lambda g, c: (g, cc(c), 0)),
        rowv=pl.BlockSpec((None, hpg, CHUNK), lambda g, c: (g, 0, cc(c))),
        head=pl.BlockSpec((None, 1, hpg), lambda g, c: (g, 0, 0)),
        lanes=pl.BlockSpec((1, w), lambda g, c: (0, g)),
        bc=pl.BlockSpec((CHUNK, D_STATE), lambda g, c: (cc(c), g)),
        prev=pl.BlockSpec((None, None, D_STATE, w), lambda g, c: (cc(c), g, 0, 0)),
        seg=pl.BlockSpec((w, V7X_LANES), lambda g, c: (0, 0)),
    )


def _decay_masks(cb, ac_ref, ar_ref, heads):
    li = lax.broadcasted_iota(jnp.int32, (CHUNK, CHUNK), 0)
    si = lax.broadcasted_iota(jnp.int32, (CHUNK, CHUNK), 1)
    lms = [jnp.exp(jnp.where(li >= si, ac_ref[:, hh:hh + 1] - ar_ref[hh:hh + 1, :], -jnp.inf)) for hh in heads]
    return lms, [(cb * lm).astype(BF16) for lm in lms]


def _ssd_fwd(name, xbc, dt_x, ein_x, eout_x, a_col, a_row, d_x, di, g_n, hpg, p):
    t = xbc.shape[0]
    nc = t // CHUNK
    w = hpg * p
    assert 2 * p == V7X_LANES and hpg % 2 == 0
    sp = _ssd_specs(t, di, g_n, hpg, p, False)

    def body(xs_ref, bm_ref, cm_ref, dt_ref, ein_ref, eout_ref, ac_ref, ar_ref, d_ref, y_ref, prev_ref, h_ref):
        @pl.when(pl.program_id(1) == 0)
        def _():
            h_ref[...] = jnp.zeros_like(h_ref)

        bm = bm_ref[...].astype(BF16)
        cm = cm_ref[...].astype(BF16)
        cb = lax.dot_general(cm, bm, NT, preferred_element_type=F32)
        first = lax.broadcasted_iota(jnp.int32, (1, V7X_LANES), 1) < p
        xs = xs_ref[...]
        e_in = ein_ref[...]
        xdt = xs * dt_ref[...]
        ys = []
        for pr in range(hpg // 2):
            _, ms = _decay_masks(cb, ac_ref, ar_ref, (2 * pr, 2 * pr + 1))
            xp = xdt[:, pr * V7X_LANES:(pr + 1) * V7X_LANES]
            rhs = jnp.concatenate([jnp.where(first, xp, 0.0), jnp.where(first, 0.0, xp)], axis=0).astype(BF16)
            ys.append(jnp.dot(jnp.concatenate(ms, axis=1), rhs, preferred_element_type=F32))
        h_prev = h_ref[...]
        prev_ref[...] = h_prev
        y = jnp.concatenate(ys, axis=1) + jnp.dot(cm, h_prev.astype(BF16), preferred_element_type=F32) * e_in
        y_ref[...] = y + xs * d_ref[...]
        st = lax.dot_general(bm, (xdt * eout_ref[...]).astype(BF16), TN, preferred_element_type=F32)
        h_ref[...] = e_in[CHUNK - 1:CHUNK, :] * h_prev + st

    return pl.pallas_call(
        body, name=name, grid=(g_n, nc),
        out_shape=[SDS((t, di), F32), SDS((nc, g_n, D_STATE, w), F32)],
        in_specs=[sp["xs"], sp["bm"], sp["cm"], sp["xs"], sp["xs"], sp["xs"], sp["col"], sp["rowv"], sp["lanes"]],
        out_specs=[sp["xs"], sp["prev"]], scratch_shapes=[pltpu.VMEM((D_STATE, w), F32)],
        compiler_params=_cp("parallel", "arbitrary"),
    )(xbc, xbc, xbc, dt_x, ein_x, eout_x, a_col, a_row, d_x)


def _head_sums(v, seg):
    hi = v.astype(BF16)
    lo = (v - hi.astype(F32)).astype(BF16)
    return jnp.dot(hi, seg, preferred_element_type=F32) + jnp.dot(lo, seg, preferred_element_type=F32)


def _head_totals(v, seg):
    part = v[0:8]
    for r in range(8, v.shape[0], 8):
        part = part + v[r:r + 8]
    return jnp.sum(_head_sums(part, seg), axis=0, keepdims=True)


def _ssd_bwd(name, xbc, dt_x, ein_x, eout_x, a_col, a_row, d_x, prev, dy, di, g_n, hpg, p):
    t = xbc.shape[0]
    nc = t // CHUNK
    w = hpg * p
    sp = _ssd_specs(t, di, g_n, hpg, p, True)
    seg = (lax.broadcasted_iota(jnp.int32, (w, V7X_LANES), 0) // p
           == lax.broadcasted_iota(jnp.int32, (w, V7X_LANES), 1)).astype(BF16)

    def body(xs_ref, bm_ref, cm_ref, dt_ref, ein_ref, eout_ref, ac_ref, ar_ref, d_ref, prev_ref, dy_ref,
             seg_ref, dx_ref, dbm_ref, dcm_ref, ddt_ref, dacs_ref, dd_ref, dh_ref):
        @pl.when(pl.program_id(1) == 0)
        def _():
            dh_ref[...] = jnp.zeros_like(dh_ref)
            dd_ref[...] = jnp.zeros_like(dd_ref)

        bm = bm_ref[...].astype(BF16)
        cm = cm_ref[...].astype(BF16)
        cb = lax.dot_general(cm, bm, NT, preferred_element_type=F32)
        first = lax.broadcasted_iota(jnp.int32, (1, V7X_LANES), 1) < p
        last_row = lax.broadcasted_iota(jnp.int32, (CHUNK, 1), 0) == CHUNK - 1
        seg_m = seg_ref[...]
        xs, dy, e_in, e_out, d_skip = xs_ref[...], dy_ref[...], ein_ref[...], eout_ref[...], d_ref[...]
        dt_l = dt_ref[...]
        xdt = xs * dt_l
        h_prev = prev_ref[...]
        h_prev_b = h_prev.astype(BF16)
        dh_next = dh_ref[...]
        dh_next_b = dh_next.astype(BF16)
        dy_e = (dy * e_in).astype(BF16)
        d_cm = lax.dot_general(dy_e, h_prev_b, NT, preferred_element_type=F32)
        dh_ref[...] = e_in[CHUNK - 1:CHUNK, :] * dh_next + lax.dot_general(cm, dy_e, TN, preferred_element_type=F32)
        q = jnp.dot(bm, dh_next_b, preferred_element_type=F32)
        xf = xdt * e_out
        d_bm = lax.dot_general(xf.astype(BF16), dh_next_b, NT, preferred_element_type=F32)
        d_cb = jnp.zeros((CHUNK, CHUNK), F32)
        parts, w_parts = [], []
        for pr in range(hpg // 2):
            lanes = slice(pr * V7X_LANES, (pr + 1) * V7X_LANES)
            lms, ms = _decay_masks(cb, ac_ref, ar_ref, (2 * pr, 2 * pr + 1))
            xp = xdt[:, lanes]
            xp_b = xp.astype(BF16)
            dyp = dy[:, lanes]
            halves = [jnp.where(first, dyp, 0.0).astype(BF16), jnp.where(first, 0.0, dyp).astype(BF16)]
            for lm, half in zip(lms, halves):
                d_cb = d_cb + lax.dot_general(half, xp_b, NT, preferred_element_type=F32) * lm
            dxd = lax.dot_general(jnp.concatenate(ms, axis=0), jnp.concatenate(halves, axis=0), TN,
                                  preferred_element_type=F32)
            stacked = jnp.concatenate([jnp.where(first, xp, 0.0), jnp.where(first, 0.0, xp)], axis=0).astype(BF16)
            y_diag = jnp.dot(jnp.concatenate(ms, axis=1), stacked, preferred_element_type=F32)
            parts.append(dxd)
            w_parts.append(dyp.astype(BF16).astype(F32) * y_diag - xp_b.astype(F32) * dxd)
        d_xdt = jnp.concatenate(parts, axis=1) + q * e_out
        dx_ref[...] = d_xdt * dt_l + dy * d_skip
        ch = jnp.dot(cm, h_prev_b, preferred_element_type=F32)
        qx = q * xf
        s_a = _head_sums(dy * ch * e_in - qx + jnp.concatenate(w_parts, axis=1), seg_m)[:, :hpg]
        d_last = (_head_totals(qx, seg_m)[:, :hpg]
                  + jnp.exp(ac_ref[CHUNK - 1:CHUNK, :]) * _head_totals(dh_next * h_prev, seg_m)[:, :hpg])
        ddt_ref[...] = _head_sums(d_xdt * xs, seg_m)[:, :hpg]
        dacs_ref[...] = s_a + jnp.where(last_row, d_last, 0.0)
        dd_ref[...] += _head_totals(dy * xs, seg_m)[:, :hpg]
        d_cb_b = d_cb.astype(BF16)
        dcm_ref[...] = d_cm + jnp.dot(d_cb_b, bm, preferred_element_type=F32)
        dbm_ref[...] = d_bm + lax.dot_general(d_cb_b, cm, TN, preferred_element_type=F32)

    gn = g_n * D_STATE
    return pl.pallas_call(
        body, name=name, grid=(g_n, nc),
        out_shape=[SDS((t, di), F32), SDS((t, gn), F32), SDS((t, gn), F32), SDS((g_n, t, hpg), F32),
                   SDS((g_n, t, hpg), F32), SDS((g_n, 1, hpg), F32)],
        in_specs=[sp["xs"], sp["bm"], sp["cm"], sp["xs"], sp["xs"], sp["xs"], sp["col"], sp["rowv"],
                  sp["lanes"], sp["prev"], sp["xs"], sp["seg"]],
        out_specs=[sp["xs"], sp["bc"], sp["bc"], sp["col"], sp["col"], sp["head"]],
        scratch_shapes=[pltpu.VMEM((D_STATE, w), F32)],
        compiler_params=_cp("parallel", "arbitrary"),
    )(xbc, xbc, xbc, dt_x, ein_x, eout_x, a_col, a_row, d_x, prev, dy, seg)


def _as3d(a):
    return a.reshape(a.shape[0], -1, a.shape[-1])


def _pair_sum(name, own, recv, core):
    shape = recv.shape
    cols = shape[-1]
    own3, recv3 = own.reshape(8, -1, cols), recv.reshape(4, -1, cols)
    rows = recv3.shape[1]
    tr = _row_tile(rows, cols, 2)

    def body(c_ref, a_ref, b_ref, o_ref):
        o_ref[...] = (a_ref[...].astype(F32) + b_ref[...].astype(F32)).astype(o_ref.dtype)

    blk = pl.BlockSpec((None, tr, cols), lambda q, i, c_ref: (q, i, 0))
    out = pl.pallas_call(
        body, name=name, out_shape=SDS(recv3.shape, recv.dtype),
        grid_spec=pltpu.PrefetchScalarGridSpec(
            num_scalar_prefetch=1, grid=(4, rows // tr),
            in_specs=[pl.BlockSpec((None, tr, cols), lambda q, i, c_ref: (2 * q + c_ref[0], i, 0)), blk], out_specs=blk),
        compiler_params=_cp("parallel", "parallel"),
    )(core, own3, recv3)
    return out.reshape(shape)


def _adamw(name, w, m, v, parts, layer, prev=None, sel=None):
    lyr, rows, cols = w.shape
    n = len(parts)
    tr = _row_tile(rows, cols)
    np_ = 0 if prev is None else 4
    if sel is None:
        sel = jnp.zeros((1,), jnp.int32)

    def body(sel_ref, *refs):
        w_ref, m_ref, v_ref = refs[:3]
        p_refs = refs[3:3 + n]
        g_ref, d_ref, nm_ref, nv_ref = refs[3 + n + np_:]
        g = p_refs[0][...].astype(F32)
        for r in p_refs[1:]:
            g = g + r[...].astype(F32)
        nm = ADAM_B1 * m_ref[...] + (1.0 - ADAM_B1) * g
        nv = ADAM_B2 * v_ref[...] + (1.0 - ADAM_B2) * (g * g)
        m_hat = nm / (1.0 - ADAM_B1 ** ADAM_STEP)
        v_hat = nv / (1.0 - ADAM_B2 ** ADAM_STEP)
        g_ref[...] = g
        d_ref[...] = -ADAM_LR * (m_hat / (jnp.sqrt(v_hat) + ADAM_EPS) + ADAM_WD * w_ref[...])
        nm_ref[...] = nm
        nv_ref[...] = nv

    lspec = pl.BlockSpec((None, tr, cols), lambda i, s: (layer, i, 0))
    pspecs = [pl.BlockSpec((None, tr, cols), (lambda i, s: (s[0], i, 0)) if q is None else (lambda i, s, q=q: (q, i, 0)))
              for _, q in parts]
    aliases = {} if prev is None else {4 + n + q: q for q in range(4)}
    return pl.pallas_call(
        body, name=name, out_shape=[SDS(w.shape, F32)] * 4,
        grid_spec=pltpu.PrefetchScalarGridSpec(
            num_scalar_prefetch=1, grid=(rows // tr,), in_specs=[lspec] * 3 + pspecs + [ANY] * np_,
            out_specs=[lspec] * 4),
        input_output_aliases=aliases, compiler_params=_cp("parallel"),
    )(sel, w, m, v, *[arr for arr, _ in parts], *(prev or ()))


def _sum8(name, parts):
    rows = parts.shape[1]

    def body(p_ref, o_ref):
        s = p_ref[0]
        for q in range(1, N_DEV):
            s = s + p_ref[q]
        o_ref[...] = s

    return pl.pallas_call(
        body, name=name, grid=(1,), out_shape=SDS((rows, V7X_LANES), F32),
        in_specs=[pl.BlockSpec((N_DEV, rows, V7X_LANES), lambda i: (0, 0, 0))],
        out_specs=pl.BlockSpec((rows, V7X_LANES), lambda i: (0, 0)), compiler_params=_cp("arbitrary"),
    )(parts)


def _pack(vectors, align):
    flat = jnp.concatenate([v.reshape(-1) for v in vectors])
    pad = (-flat.shape[0]) % align
    if pad:
        flat = jnp.concatenate([flat, jnp.zeros((pad,), F32)])
    return flat.reshape(-1, V7X_LANES)


def _unpack(packed, shapes):
    flat = packed.reshape(-1)
    out, o = [], 0
    for s in shapes:
        size = 1
        for dim in s:
            size *= dim
        out.append(flat[o:o + size].reshape(s))
        o += size
    return out


def _residual_epilogue(acc, res):
    return (ALPHA * res + acc,)


def _plain_add_epilogue(acc, res):
    return (res + acc,)


def _gate_epilogue(acc, y, e):
    gate = _sigmoid(acc)
    xn = y + gate * e
    return xn, gate, xn


def _relu2(pre):
    r = jnp.maximum(pre, 0.0)
    return r * r


def _relu2_bwd_epilogue(acc, pre):
    return (acc * (2.0 * jnp.maximum(pre.astype(F32), 0.0)),)


def _tail_fwd(tag, u_a, wts, lng, lnb, p_l, finish_w2):
    y1, y1_b = _ln_fwd(f"ln1_{tag}", u_a, lng[0], lnb[0])
    (pre,) = _mm_fwd(f"mlp1_{tag}", y1_b, wts["w1"], "col", [BF16])
    finish_w2(pre)
    (u_b,) = _mm_fwd(f"mlp2_{tag}", pre, wts["w2"], "row", [F32], _residual_epilogue, (y1,), a_fn=_relu2)
    y2, y2_b = _ln_fwd(f"ln2_{tag}", u_b, lng[1], lnb[1])
    (e,) = _mm_fwd(f"ple_{tag}", p_l, wts["plew"], "col", [F32])
    xn, gate, xn_b = _mm_fwd(f"gate_{tag}", y2_b, wts["gate"], "row", [F32, F32, BF16], _gate_epilogue, (y2, e))
    return xn, xn_b, (u_a, y1_b, pre, u_b, y2_b, e, gate)


def _tail_bwd(tag, dxn, saved, wts, lng, p_l, emit, advance, toks):
    u_a, y1_b, pre, u_b, y2_b, e, gate = saved
    dgpre, de = _ple_bwd(f"ple_bwd_{tag}", dxn, e, gate)
    toks = emit(f"{tag}_ple", dict(gate=_mm_dw(f"gate_dw_{tag}", y2_b, dgpre, "row", after=toks),
                                   plew=_mm_dw(f"ple_dw_{tag}", p_l, de, "col")))
    (dy2,) = _mm_dx(f"gate_dx_{tag}", dgpre, wts["gate"], "row", [F32], _plain_add_epilogue, (dxn,), after=toks)
    toks = advance((dy2,))
    du_b, du_b16, dg2, db2 = _ln_bwd(f"ln2_bwd_{tag}", u_b, dy2, lng[1])
    toks = emit(f"{tag}_w2", dict(w2=_mm_dw(f"mlp2_dw_{tag}", pre, du_b16, "row", a_fn=_relu2, after=toks)))
    (dpre,) = _mm_dx(f"mlp2_dx_{tag}", du_b16, wts["w2"], "row", [BF16], _relu2_bwd_epilogue, (pre,), after=toks)
    toks = advance((dpre,))
    toks = emit(f"{tag}_w1", dict(w1=_mm_dw(f"mlp1_dw_{tag}", y1_b, dpre, "col", after=toks)))
    (dy1,) = _mm_dx(f"mlp1_dx_{tag}", dpre, wts["w1"], "col", [F32], _residual_epilogue, (du_b,), after=toks)
    toks = advance((dy1,))
    du_a, du_a16, dg1, db1 = _ln_bwd(f"ln1_bwd_{tag}", u_a, dy1, lng[0])
    return du_a, du_a16, [dg1, dg2], [db1, db2], toks


def _to_slots(a, axis):
    shape = a.shape
    per = shape[axis] // N_DEV
    v = a.reshape(shape[:axis] + (N_DEV, per) + shape[axis + 1:])
    return jnp.moveaxis(v, axis, 0)


def _pad_lanes(a):
    return jnp.pad(a, [(0, 0)] * (a.ndim - 1) + [(0, V7X_LANES - a.shape[-1])])


def kernel(x, p, pool_w, pool_scale, ssm_in_w, ssm_conv_w, ssm_conv_b, ssm_dt_bias, ssm_a_log, ssm_d, ssm_norm_w, ssm_out_w, mlp_w1, mlp_w2, ln_g, ln_b, ple_w, ple_gate_w, loss_target, m_pool_w, m_pool_scale, m_ssm_in_w, m_ssm_conv_w, m_ssm_conv_b, m_ssm_dt_bias, m_ssm_a_log, m_ssm_d, m_ssm_norm_w, m_ssm_out_w, m_mlp_w1, m_mlp_w2, m_ln_g, m_ln_b, m_ple_w, m_ple_gate_w, v_pool_w, v_pool_scale, v_ssm_in_w, v_ssm_conv_w, v_ssm_conv_b, v_ssm_dt_bias, v_ssm_a_log, v_ssm_d, v_ssm_norm_w, v_ssm_out_w, v_mlp_w1, v_mlp_w2, v_ln_g, v_ln_b, v_ple_w, v_ple_gate_w):
    t, d = x.shape[1:]
    h_n = ssm_dt_bias.shape[-1]
    di_s, cd_s, dp_s, d_s = ssm_norm_w.shape[-1], ssm_conv_b.shape[-1], ssm_in_w.shape[-1], ln_g.shape[-1]
    di, cd, dp = N_DEV * di_s, N_DEV * cd_s, N_DEV * dp_s
    p_dim = di // h_n
    g_n = (cd - di) // (2 * D_STATE)
    hpg = h_n // g_n
    dpp = di + cd + V7X_LANES
    assert h_n <= V7X_LANES and dp == di + cd + h_n and (di + cd) % V7X_LANES == 0
    dt_block = (di + cd) // V7X_LANES
    cg = d // 4
    me = 4 * lax.axis_index("x") + 2 * lax.axis_index("y") + lax.axis_index("c")

    x0, target = x[0], loss_target[0]
    p_l = [p[0, 0].astype(BF16), p[1, 0].astype(BF16)]

    small_shapes = [(CONV_WIDTH, cd_s), (1, cd_s), (1, di_s), (2, 2, d_s), (2, 2, d_s)]
    small = _pack([ssm_conv_w[0], ssm_conv_b, ssm_norm_w, ln_g, ln_b], 8 * V7X_LANES)
    first = [w.astype(BF16) for w in (pool_w[0], mlp_w1[0])]
    own_l0 = [w.astype(BF16) for w in (mlp_w2[0], ple_w[0], ple_gate_w[0])]
    own_ssm = [w.astype(BF16) for w in (ssm_in_w[0], ssm_out_w[0])]
    own_mlp = [w.astype(BF16) for w in (mlp_w1[1], mlp_w2[1], ple_w[1], ple_gate_w[1])]

    def gather_start(tag, own, after):
        lands = [lax.empty((N_DEV,) + w.shape, w.dtype) for w in own]
        return _async_start(f"ag_{tag}_start", _ag_first_copies, (4 * len(own),), own, lands, after)

    def gather_finish(tag, handle, after):
        n = len(handle[2]) // 2
        own, lands = _async_wait(f"ag_{tag}_wait", _ag_first_copies, handle, n, after)
        fwd = _async_start(f"ag_{tag}_forward_start", _ag_forward_copies, (3 * n,), [], lands)
        _, lands = _async_wait(f"ag_{tag}_forward_wait", _ag_forward_copies, fwd, 0)
        return [lax.dynamic_update_slice_in_dim(g, w[None], me, 0) for g, w in zip(lands, own)]

    ag_first = gather_start("first", first + [small], ())
    ag_l0 = gather_start("l0", own_l0, (ag_first[3],))
    ag_ssm = gather_start("ssm", own_ssm, (ag_l0[3],))
    ag_mlp = gather_start("mlp1", own_mlp, (ag_ssm[3],))
    pooled = _pool_windows("pool_fwd", x0, False, after=(ag_mlp[3],))
    pool_g, w1_0, small_g = gather_finish("first", ag_first, (pooled,))
    pool_full = pool_g.transpose(1, 0, 2, 3).reshape(4, cg, cg)
    sm = small_g.reshape(N_DEV, -1)
    o = 0
    parts = []
    for shp in small_shapes:
        size = 1
        for s in shp:
            size *= s
        parts.append(sm[:, o:o + size].reshape((N_DEV,) + shp))
        o += size
    conv_w_full = parts[0].transpose(1, 0, 2).reshape(CONV_WIDTH, cd)
    conv_b_full = parts[1].transpose(1, 0, 2).reshape(1, cd)
    norm_w_full = parts[2].transpose(1, 0, 2).reshape(1, di)
    ln_g_full = parts[3].transpose(1, 2, 0, 3).reshape(2, 2, 1, d)
    ln_b_full = parts[4].transpose(1, 2, 0, 3).reshape(2, 2, 1, d)
    bias128, alog128 = _pad_lanes(ssm_dt_bias), _pad_lanes(ssm_a_log)

    u0, hraw = _pool_mm("pool_mm", pooled, pool_full, pool_scale, x0)
    wts = [dict(w1=w1_0)]

    def finish_l0(after):
        w2_0, plew_0, gate_0 = gather_finish("l0", ag_l0, (after,))
        wts[0].update(w2=w2_0, plew=plew_0, gate=gate_0)

    x1, x1_b, saved0 = _tail_fwd("l0", u0, wts[0], ln_g_full[0], ln_b_full[0], p_l[0], finish_l0)

    in_g, out_g = gather_finish("ssm", ag_ssm, (x1,))
    in_full = jnp.pad(in_g.transpose(1, 0, 2).reshape(d, dp), ((0, 0), (0, dpp - dp)))
    (zx,) = _mm_fwd("in_proj", x1_b, in_full, "plain", [F32])
    xbc = _conv_fwd("conv_fwd", zx, conv_w_full, conv_b_full, di)
    dt, acs, e_in, e_out = _dt_fwd("dt_fwd", zx, bias128, alog128, dt_block)

    dt_x, ein_x, eout_x = _expand_heads("expand_heads", [dt, e_in, e_out], h_n, p_dim)
    d_x = jnp.repeat(ssm_d, p_dim, axis=1)

    def to_col(a):
        return a[:, :h_n].reshape(t, g_n, hpg).transpose(1, 0, 2)

    def to_row(a):
        return a[:, :h_n].reshape(t, g_n, hpg).transpose(1, 2, 0)

    def from_col(a):
        return _pad_lanes(a.transpose(1, 0, 2).reshape(t, h_n))

    a_col, a_row = to_col(acs), to_row(acs)
    y_ssd, prev = _ssd_fwd("ssd_fwd", xbc, dt_x, ein_x, eout_x, a_col, a_row, d_x, di, g_n, hpg, p_dim)
    yn = _gated_rms_fwd("gated_rms_fwd", y_ssd, zx, norm_w_full)
    (u2,) = _mm_fwd("out_proj", yn, out_g, "row", [F32], _residual_epilogue, (x1,))
    w1_1, w2_1, plew_1, gate_1 = gather_finish("mlp1", ag_mlp, (u2,))
    wts.append(dict(w1=w1_1, w2=w2_1, plew=plew_1, gate=gate_1))
    x2, _, saved1 = _tail_fwd("l1", u2, wts[1], ln_g_full[1], ln_b_full[1], p_l[1], lambda after: None)

    core = lax.axis_index("c").astype(jnp.int32).reshape(1)
    chip = (2 * lax.axis_index("x") + lax.axis_index("y")).astype(jnp.int32).reshape(1)
    scattering = {}
    pending = []

    def to_chips(after):
        if not pending:
            return []
        tag, names, handle = pending.pop()
        own, halves = _async_wait(f"rs_{tag}_sibling_wait", _rs_sibling_copies, handle, len(names), after)
        sums = [_pair_sum(f"rs_{tag}_pair_sum_{n}", g, hv, core) for n, g, hv in zip(names, own, halves)]
        lands = [lax.empty((3,) + s.shape[1:], s.dtype) for s in sums]
        handle = _async_start(f"rs_{tag}_start", _rs_chip_copies, (3 * len(sums),), sums, lands)
        scattering[tag] = (names, handle)
        return [handle[3]]

    def emit(tag, grads):
        names, arrays = list(grads), list(grads.values())
        toks = to_chips(tuple(arrays))
        lands = [lax.empty((4,) + g.shape[1:], g.dtype) for g in arrays]
        handle = _async_start(f"rs_{tag}_sibling_start", _rs_sibling_copies, (4 * len(arrays),), arrays, lands,
                              tuple(toks))
        pending.append((tag, names, handle))
        return toks + [handle[3]]

    def collect(tag, after):
        names, handle = scattering.pop(tag)
        sums, thirds = _async_wait(f"rs_{tag}_wait", _rs_chip_copies, handle, len(names), after)
        return {n: (s.reshape(4, -1, s.shape[-1]), r.reshape(3, -1, r.shape[-1])) for n, s, r in zip(names, sums, thirds)}

    dx2, loss_cols = _loss_bwd("loss", x2, target)
    du2, du2_b, dg_1, db_1, toks = _tail_bwd("l1", dx2, saved1, wts[1], ln_g_full[1], p_l[1], emit, to_chips, [])
    toks = emit("ssm_out", dict(out=_mm_dw("out_proj_dw", yn, du2_b, "row", after=toks)))
    (dyn,) = _mm_dx("out_proj_dx", du2_b, out_g, "row", [F32], after=toks)
    toks = to_chips((dyn,))
    dy_ssd, dz, d_norm_w = _gated_rms_bwd("gated_rms_bwd", y_ssd, zx, norm_w_full, dyn)
    dxs, dbm, dcm, ddt_x, dacs, dd = _ssd_bwd("ssd_bwd", xbc, dt_x, ein_x, eout_x, a_col, a_row, d_x, prev, dy_ssd,
                                              di, g_n, hpg, p_dim)
    draw, d_bias, d_alog = _dt_bwd("dt_bwd", zx, bias128, alog128, from_col(dacs), from_col(ddt_x), dt_block)
    conv_parts = [_conv_bwd(f"conv_bwd_{tag}", zx, conv_w_full, conv_b_full, dact, di, first)
                  for tag, dact, first in (("xs", dxs, 0), ("b", dbm, di), ("c", dcm, di + g_n * D_STATE))]
    d_conv_w = jnp.concatenate([c[1] for c in conv_parts], axis=1)
    d_conv_b = jnp.concatenate([c[2] for c in conv_parts], axis=1)
    dzx = jnp.concatenate([dz] + [c[0] for c in conv_parts] + [draw], axis=1)
    toks = emit("ssm_in", {"in": _to_slots(_mm_dw("in_proj_dw", x1_b, dzx, "plain", after=toks)[:, :dp], 1)})
    (dx1,) = _mm_dx("in_proj_dx", dzx, in_full, "plain", [F32], _residual_epilogue, (du2,), after=toks)
    toks = to_chips((dx1,))

    du0, _, dg_0, db_0, toks = _tail_bwd("l0", dx1, saved0, wts[0], ln_g_full[0], p_l[0], emit, to_chips, toks)
    dh, dpool, d_scale = _pool_bwd_mm("pool_bwd_mm", du0, hraw, pool_full, pool_scale)
    toks += emit("pool", dict(pool=_to_slots(_pool_dw("pool_dw", pooled, dh), 1)))
    grad_x = _pool_windows("pool_bwd", dpool, True, du0, after=tuple(toks))
    toks = to_chips((grad_x,))

    def update(tag, w, m, v, parts, layer, prev=None):
        own, recv = parts
        return _adamw(f"adamw_{tag}_{layer}", _as3d(w), _as3d(m), _as3d(v),
                      [(own, None), (recv, 0), (recv, 1), (recv, 2)], layer, prev, chip)

    q = collect("l1_ple", (grad_x, *toks))
    r_gate = update("ple_gate_w", ple_gate_w, m_ple_gate_w, v_ple_gate_w, q["gate"], 1)
    r_plew = update("ple_w", ple_w, m_ple_w, v_ple_w, q["plew"], 1)
    r_w2 = update("mlp_w2", mlp_w2, m_mlp_w2, v_mlp_w2, collect("l1_w2", (r_gate[0],))["w2"], 1)
    r_w1 = update("mlp_w1", mlp_w1, m_mlp_w1, v_mlp_w1, collect("l1_w1", (r_w2[0],))["w1"], 1)
    r_out = update("ssm_out_w", ssm_out_w, m_ssm_out_w, v_ssm_out_w, collect("ssm_out", (r_w1[0],))["out"], 0)
    r_in = update("ssm_in_w", ssm_in_w, m_ssm_in_w, v_ssm_in_w, collect("ssm_in", (r_out[0],))["in"], 0)

    d_ln_g = jnp.stack([jnp.stack(dg_0), jnp.stack(dg_1)]).reshape(2, 2, d)
    d_ln_b = jnp.stack([jnp.stack(db_0), jnp.stack(db_1)]).reshape(2, 2, d)
    partial_shapes = [(CONV_WIDTH, cd), (1, cd), (1, di), (2, 2, d), (2, 2, d), (1, d), (1, h_n), (1, h_n), (1, h_n),
                      (1, d)]
    partial = _pack([d_conv_w, d_conv_b, d_norm_w, d_ln_g, d_ln_b, d_scale, d_bias[:, :h_n], d_alog[:, :h_n],
                     dd.reshape(1, h_n), loss_cols], 8 * V7X_LANES)
    (all_partials,) = _all_gather("ag_small_grads", [partial], after=(r_in[0],))
    tot = _unpack(_sum8("sum_small_grads", all_partials), partial_shapes)
    t_conv_w, t_conv_b, t_norm_w, t_ln_g, t_ln_b, t_scale, t_bias, t_alog, t_dd, t_loss = tot
    loss = jnp.sum(t_loss)

    def mine(a, per):
        return lax.dynamic_slice_in_dim(a, me * per, per, axis=a.ndim - 1)

    small_names = ["ssm_conv_w", "ssm_conv_b", "ssm_norm_w", "ln_g", "ln_b", "pool_scale", "ssm_dt_bias", "ssm_a_log",
                   "ssm_d"]
    small_w = [ssm_conv_w, ssm_conv_b, ssm_norm_w, ln_g, ln_b, pool_scale, ssm_dt_bias, ssm_a_log, ssm_d]
    small_m = [m_ssm_conv_w, m_ssm_conv_b, m_ssm_norm_w, m_ln_g, m_ln_b, m_pool_scale, m_ssm_dt_bias, m_ssm_a_log,
               m_ssm_d]
    small_v = [v_ssm_conv_w, v_ssm_conv_b, v_ssm_norm_w, v_ln_g, v_ln_b, v_pool_scale, v_ssm_dt_bias, v_ssm_a_log,
               v_ssm_d]
    small_grads = [mine(t_conv_w, cd_s), mine(t_conv_b, cd_s), mine(t_norm_w, di_s), mine(t_ln_g, d_s),
                   mine(t_ln_b, d_s), t_scale, t_bias, t_alog, t_dd]
    shapes = [w.shape for w in small_w]
    pk = [_pack(group, 8 * V7X_LANES)[None] for group in (small_w, small_m, small_v, small_grads)]
    res = _adamw("adamw_small", pk[0], pk[1], pk[2], [(pk[3], 0)], 0)
    upd = {}
    for name, vals in zip(small_names, zip(*[_unpack(r, shapes) for r in res])):
        upd[name] = list(vals)

    q = collect("l0_ple", (res[0],))
    r_gate = update("ple_gate_w", ple_gate_w, m_ple_gate_w, v_ple_gate_w, q["gate"], 0, r_gate)
    r_plew = update("ple_w", ple_w, m_ple_w, v_ple_w, q["plew"], 0, r_plew)
    r_w2 = update("mlp_w2", mlp_w2, m_mlp_w2, v_mlp_w2, collect("l0_w2", (r_gate[0],))["w2"], 0, r_w2)
    r_w1 = update("mlp_w1", mlp_w1, m_mlp_w1, v_mlp_w1, collect("l0_w1", (r_w2[0],))["w1"], 0, r_w1)
    r_pool = update("pool_w", pool_w, m_pool_w, v_pool_w, collect("pool", (r_w1[0],))["pool"], 0)
    assert not scattering
    large = {"pool_w": (pool_w, r_pool), "ssm_in_w": (ssm_in_w, r_in), "ssm_out_w": (ssm_out_w, r_out),
             "mlp_w1": (mlp_w1, r_w1), "mlp_w2": (mlp_w2, r_w2), "ple_w": (ple_w, r_plew),
             "ple_gate_w": (ple_gate_w, r_gate)}
    for name, (w, rs) in large.items():
        upd[name] = [r.reshape(w.shape) for r in rs]

    order = ["pool_w", "pool_scale", "ssm_in_w", "ssm_conv_w", "ssm_conv_b", "ssm_dt_bias", "ssm_a_log", "ssm_d",
             "ssm_norm_w", "ssm_out_w", "mlp_w1", "mlp_w2", "ln_g", "ln_b", "ple_w", "ple_gate_w"]
    out = [loss, grad_x[None]]
    for k in range(4):
        out += [upd[name][k] for name in order]
    return tuple(out)
```

```python
import jax
import jax.numpy as jnp
from jax import lax
from jax.experimental import pallas as pl
from jax.experimental.pallas import tpu as pltpu

F32 = jnp.float32
BF16 = jnp.bfloat16
SDS = jax.ShapeDtypeStruct
MESH = pl.DeviceIdType.MESH
ANY = pl.BlockSpec(memory_space=pl.ANY)

N_DEV = 8
DEPTH = 2
ALPHA = (2.0 * DEPTH) ** 0.25
LN_EPS = 1e-5
RMS_EPS = 1e-5
POOL_WINDOW_LOG2 = (1, 2, 3, 4)
D_STATE = 128
CHUNK = 128
CONV_WIDTH = 4
ADAM_LR = 0.001
ADAM_B1 = 0.9
ADAM_B2 = 0.999
ADAM_EPS = 1e-08
ADAM_WD = 0.01
ADAM_STEP = 10

V7X_LANES = 128
V7X_VMEM_LIMIT = 48 * 1024 * 1024


def _cp(*sem):
    return pltpu.CompilerParams(dimension_semantics=sem, vmem_limit_bytes=V7X_VMEM_LIMIT)


def _pick(dim, cap):
    if dim <= cap:
        return dim
    best = None
    for t in range(V7X_LANES, cap + 1, V7X_LANES):
        if dim % t == 0:
            best = t
    assert best is not None, (dim, cap)
    return best


def _row_tile(rows, cols, itemsize=4, target=1 << 20):
    t = rows
    while t % 2 == 0 and t // 2 >= 16 and (t // 2) % 16 == 0 and t * cols * itemsize > target:
        t //= 2
    return t


def _all_gather(name, shards, after=()):
    n, na = len(shards), len(after)

    def body(*refs):
        ins, outs = refs[:n], refs[n + na:2 * n + na]
        send_sems, recv_sems, local_sems = refs[2 * n + na:]
        x, y, c = lax.axis_index("x"), lax.axis_index("y"), lax.axis_index("c")
        me, sibling = (x, y, c), (x, y, 1 - c)
        chips = [(1 - x, y), (x, 1 - y), (1 - x, 1 - y)]

        def copy(a, k, block, to, src=None):
            dst = outs[a].at[4 * block[0] + 2 * block[1] + block[2]]
            return pltpu.make_async_remote_copy(
                src_ref=dst if src is None else src, dst_ref=dst, send_sem=send_sems.at[a, k],
                recv_sem=recv_sems.at[a, k], device_id=to, device_id_type=MESH)

        mine = [pltpu.make_async_copy(ins[a], outs[a].at[4 * x + 2 * y + c], local_sems.at[a]) for a in range(n)]
        for cp in mine:
            cp.start()
        first = []
        for a in range(n):
            first.append(copy(a, 0, me, sibling, src=ins[a]))
            first += [copy(a, 1 + j, me, (*chip, c), src=ins[a]) for j, chip in enumerate(chips)]
        for cp in first:
            cp.start()
        passed = []
        for j, chip in enumerate(chips):
            for a in range(n):
                copy(a, 1 + j, (*chip, c), me).wait_recv()
                fwd = copy(a, 4 + j, (*chip, c), sibling)
                fwd.start()
                passed.append(fwd)
        for a in range(n):
            copy(a, 0, sibling, me).wait_recv()
            for j, chip in enumerate(chips):
                copy(a, 4 + j, (*chip, 1 - c), me).wait_recv()
        for cp in first + passed:
            cp.wait_send()
        for cp in mine:
            cp.wait()

    return pl.pallas_call(
        body, name=name,
        out_shape=[SDS((N_DEV,) + s.shape, s.dtype) for s in shards],
        in_specs=[ANY] * (n + na), out_specs=[ANY] * n,
        scratch_shapes=[pltpu.SemaphoreType.DMA((n, 7)), pltpu.SemaphoreType.DMA((n, 7)),
                        pltpu.SemaphoreType.DMA((n,))],
    )(*shards, *after)


HBM_SPEC = pl.BlockSpec(memory_space=pltpu.HBM)
SEM_SPEC = pl.BlockSpec(memory_space=pltpu.SEMAPHORE)
EFFECT = pltpu.SideEffectType.DATAFLOW_SIDE_EFFECTING


def _ag_first_copies(ins, lands, send_sems, recv_sems):
    x, y, c = lax.axis_index("x"), lax.axis_index("y"), lax.axis_index("c")
    targets = [(x, y, 1 - c), (1 - x, y, c), (x, 1 - y, c), (1 - x, 1 - y, c)]
    return [pltpu.make_async_remote_copy(
        src_ref=ins[a], dst_ref=lands[a].at[4 * x + 2 * y + c], send_sem=send_sems.at[4 * a + k],
        recv_sem=recv_sems.at[4 * a + k], device_id=to, device_id_type=MESH)
        for a in range(len(ins)) for k, to in enumerate(targets)]


def _ag_forward_copies(ins, lands, send_sems, recv_sems):
    x, y, c = lax.axis_index("x"), lax.axis_index("y"), lax.axis_index("c")
    cps = []
    for a in range(len(lands)):
        for j, (px, py) in enumerate([(1 - x, y), (x, 1 - y), (1 - x, 1 - y)]):
            blk = lands[a].at[4 * px + 2 * py + c]
            cps.append(pltpu.make_async_remote_copy(
                src_ref=blk, dst_ref=blk, send_sem=send_sems.at[3 * a + j], recv_sem=recv_sems.at[3 * a + j],
                device_id=(x, y, 1 - c), device_id_type=MESH))
    return cps


def _rs_sibling_copies(ins, lands, send_sems, recv_sems):
    x, y, c = lax.axis_index("x"), lax.axis_index("y"), lax.axis_index("c")
    return [pltpu.make_async_remote_copy(
        src_ref=ins[a].at[2 * q + 1 - c], dst_ref=lands[a].at[q], send_sem=send_sems.at[4 * a + q],
        recv_sem=recv_sems.at[4 * a + q], device_id=(x, y, 1 - c), device_id_type=MESH)
        for a in range(len(ins)) for q in range(4)]


def _rs_chip_copies(ins, lands, send_sems, recv_sems):
    x, y, c = lax.axis_index("x"), lax.axis_index("y"), lax.axis_index("c")
    cps = []
    for a in range(len(ins)):
        for j, (px, py) in enumerate([(1 - x, y), (x, 1 - y), (1 - x, 1 - y)]):
            cps.append(pltpu.make_async_remote_copy(
                src_ref=ins[a].at[2 * px + py], dst_ref=lands[a].at[j], send_sem=send_sems.at[3 * a + j],
                recv_sem=recv_sems.at[3 * a + j], device_id=(px, py, c), device_id_type=MESH))
    return cps


def _async_start(name, build, sem_shape, ins, lands, after=()):
    arrays = [*ins, *lands]
    n_i, n_t, n_a = len(ins), len(arrays), len(after)

    def body(*refs):
        outs = refs[n_t + n_a:]
        for cp in build(refs[:n_i], refs[n_i:n_t], outs[0], outs[1]):
            cp.start()
        outs[-1][...] = jnp.zeros_like(outs[-1])

    res = pl.pallas_call(
        body, name=name,
        out_shape=(pltpu.SemaphoreType.DMA(sem_shape), pltpu.SemaphoreType.DMA(sem_shape),
                   *[pltpu.HBM(a.shape, a.dtype) for a in arrays], SDS((8, V7X_LANES), F32)),
        in_specs=[HBM_SPEC] * n_t + [ANY] * n_a,
        out_specs=(SEM_SPEC, SEM_SPEC, *[HBM_SPEC] * n_t, pl.BlockSpec(memory_space=pltpu.VMEM)),
        input_output_aliases={i: 2 + i for i in range(n_t)},
        compiler_params=pltpu.CompilerParams(has_side_effects=EFFECT),
    )(*[pltpu.with_memory_space_constraint(a, pltpu.HBM) for a in arrays], *after)
    return res[0], res[1], list(res[2:2 + n_t]), res[-1]


def _async_wait(name, build, handle, n_i, after=()):
    send_sems, recv_sems, arrays, _ = handle
    n_t, n_a = len(arrays), len(after)

    def body(*refs):
        for cp in build(refs[:n_i], refs[n_i:n_t], refs[n_t], refs[n_t + 1]):
            cp.wait_send()
            cp.wait_recv()

    res = pl.pallas_call(
        body, name=name, out_shape=tuple(pltpu.HBM(a.shape, a.dtype) for a in arrays),
        in_specs=[HBM_SPEC] * n_t + [SEM_SPEC, SEM_SPEC] + [ANY] * n_a, out_specs=tuple([HBM_SPEC] * n_t),
        input_output_aliases={i: i for i in range(n_t)},
        compiler_params=pltpu.CompilerParams(has_side_effects=EFFECT),
    )(*arrays, send_sems, recv_sems, *after)
    return list(res[:n_i]), list(res[n_i:])


def _mm_core(name, a, b, *, grid, a_spec, b_spec, dims, acc_shape, outs, out_spec, epilogue=None, extras=(),
             extra_specs=(), a_fn=None, after=()):
    nk = grid[2]
    ne, no, na = len(extras), len(outs), len(after)

    def body(a_ref, b_ref, *rest):
        e_refs, o_refs, acc = rest[:ne], rest[ne + na:ne + na + no], rest[ne + na + no]
        k = pl.program_id(2)

        def product():
            lhs = a_ref[...] if a_fn is None else a_fn(a_ref[...])
            return lax.dot_general(lhs.astype(BF16), b_ref[...].astype(BF16), dims, preferred_element_type=F32)

        @pl.when(k == 0)
        def _():
            acc[...] = product()

        @pl.when(k > 0)
        def _():
            acc[...] += product()

        @pl.when(k == nk - 1)
        def _():
            r = acc[...]
            vals = epilogue(r, *[e[...] for e in e_refs]) if epilogue is not None else (r,)
            for o, v in zip(o_refs, vals):
                o[...] = v.astype(o.dtype)

    res = pl.pallas_call(
        body, name=name, grid=grid, out_shape=list(outs),
        in_specs=[a_spec, b_spec, *extra_specs, *[ANY] * na], out_specs=[out_spec] * no,
        scratch_shapes=[pltpu.VMEM(acc_shape, F32)],
        compiler_params=_cp("parallel", "parallel", "arbitrary"),
    )(a, b, *extras, *after)
    return res


NN = (((1,), (0,)), ((), ()))
NT = (((1,), (1,)), ((), ()))
TN = (((0,), (0,)), ((), ()))


def _w_dims(w, kind):
    if kind == "col":
        return w.shape[1], N_DEV * w.shape[2], w.shape[1], w.shape[2]
    if kind == "row":
        return N_DEV * w.shape[1], w.shape[2], w.shape[1], w.shape[2]
    return w.shape[0], w.shape[1], w.shape[0], w.shape[1]


MM_VMEM_BUDGET = 36 * 1024 * 1024


def _row_and_k_blocks(m, tn, k_len, k_caps, a, out_dtypes, extras):
    per_out = sum(jnp.dtype(dt).itemsize for dt in out_dtypes) + sum(e.dtype.itemsize for e in extras)
    best = None
    for tm in (_pick(m, 2048), _pick(m, 1024), _pick(m, 512)):
        for tk in [_pick(k_len, cap) for cap in k_caps]:
            used = tm * tn * (4 + 2 * per_out) + 2 * (tm * tk * a.dtype.itemsize + tk * tn * 2)
            key = ((m // tm) * (k_len // tk), -tm)
            if used <= MM_VMEM_BUDGET and (best is None or key < best[0]):
                best = (key, tm, tk)
    assert best is not None, (m, tn, k_len)
    return best[1], best[2]


def _mm_fwd(name, a, w, kind, out_dtypes, epilogue=None, extras=(), a_fn=None, after=(), k_rows=None):
    if kind == "row":
        w, kind = w.reshape(-1, w.shape[-1]), "plain"
    m = a.shape[0]
    kk, n, ks, ns = _w_dims(w, kind)
    if k_rows is None:
        assert kk == a.shape[1]
    else:
        assert kind == "plain" and k_rows <= min(kk, a.shape[1])
        kk = k_rows
    tn = _pick(ns, 1024) if kind == "col" else _pick(n, 1152)
    tm, tk = _row_and_k_blocks(m, tn, kk, (1024, 512), a, out_dtypes, extras)
    if kind == "col":
        nb = ns // tn
        b_spec = pl.BlockSpec((None, tk, tn), lambda i, j, k: (j // nb, k, j % nb))
    else:
        b_spec = pl.BlockSpec((tk, tn), lambda i, j, k: (k, j))
    mn_spec = pl.BlockSpec((tm, tn), lambda i, j, k: (i, j))
    return _mm_core(
        name, a, w, grid=(m // tm, n // tn, kk // tk),
        a_spec=pl.BlockSpec((tm, tk), lambda i, j, k: (i, k)), b_spec=b_spec, dims=NN, acc_shape=(tm, tn),
        outs=[SDS((m, n), dt) for dt in out_dtypes], out_spec=mn_spec, epilogue=epilogue, extras=extras,
        extra_specs=[mn_spec] * len(extras), a_fn=a_fn, after=after)


def _mm_dx(name, dy, w, kind, out_dtypes, epilogue=None, extras=(), after=(), k_rows=None):
    if kind == "row":
        w, kind = w.reshape(-1, w.shape[-1]), "plain"
    m, n_dim = dy.shape
    kk, n, ks, ns = _w_dims(w, kind)
    assert n == n_dim
    if k_rows is not None:
        assert kind == "plain" and k_rows <= kk
        kk = k_rows
    tn = _pick(kk, 1024)
    if kind == "col":
        tm, tk = _row_and_k_blocks(m, tn, ns, (1024, 512), dy, out_dtypes, extras)
        kb = ns // tk
        b_spec = pl.BlockSpec((None, tn, tk), lambda i, j, k: (k // kb, j, k % kb))
    else:
        tm, tk = _row_and_k_blocks(m, tn, n, (1152, 512), dy, out_dtypes, extras)
        b_spec = pl.BlockSpec((tn, tk), lambda i, j, k: (j, k))
    mk_spec = pl.BlockSpec((tm, tn), lambda i, j, k: (i, j))
    return _mm_core(
        name, dy, w, grid=(m // tm, kk // tn, n // tk),
        a_spec=pl.BlockSpec((tm, tk), lambda i, j, k: (i, k)), b_spec=b_spec, dims=NT, acc_shape=(tm, tn),
        outs=[SDS((m, kk), dt) for dt in out_dtypes], out_spec=mk_spec, epilogue=epilogue, extras=extras,
        extra_specs=[mk_spec] * len(extras), after=after)


def _in_proj_dt(name, a, w_t, first, h_n):
    m, kk = a.shape
    tm, tk = _pick(m, 1024), _pick(kk, 1024)
    blk = first // h_n
    return _mm_core(
        name, a, w_t, grid=(m // tm, 1, kk // tk), a_spec=pl.BlockSpec((tm, tk), lambda i, j, k: (i, k)),
        b_spec=pl.BlockSpec((h_n, tk), lambda i, j, k: (blk, k)), dims=NT, acc_shape=(tm, h_n),
        outs=[SDS((m, h_n), F32)], out_spec=pl.BlockSpec((tm, h_n), lambda i, j, k: (i, 0)))[0]


def _in_proj_dt_dx(name, d_dt, w_t, first, res):
    m, h_n = d_dt.shape
    n = w_t.shape[1]
    tm, tn = _pick(m, 1024), _pick(n, 1024)
    blk = first // h_n
    mn_spec = pl.BlockSpec((tm, tn), lambda i, j, k: (i, j))
    return _mm_core(
        name, d_dt, w_t, grid=(m // tm, n // tn, 1), a_spec=pl.BlockSpec((tm, h_n), lambda i, j, k: (i, 0)),
        b_spec=pl.BlockSpec((h_n, tn), lambda i, j, k: (blk, j)), dims=NN, acc_shape=(tm, tn),
        outs=[SDS((m, n), F32)], out_spec=mn_spec, epilogue=_residual_epilogue, extras=(res,),
        extra_specs=[mn_spec])[0]


def _mm_dw(name, a, dy, kind, a_fn=None, after=()):
    m, kk = a.shape
    n = dy.shape[1]
    tk = _pick(m, 1024)
    if kind == "col":
        ns = n // N_DEV
        tm, tn = _pick(kk, 1024), _pick(ns, 1024)
        nb = ns // tn
        out = SDS((N_DEV, kk, ns), BF16)
        out_spec = pl.BlockSpec((None, tm, tn), lambda i, j, k: (j // nb, i, j % nb))
    else:
        tm, tn = _pick(kk, 1024), _pick(n, 1152)
        out = SDS((kk, n), BF16)
        out_spec = pl.BlockSpec((tm, tn), lambda i, j, k: (i, j))
    res = _mm_core(
        name, a, dy, grid=(kk // tm, n // tn, m // tk),
        a_spec=pl.BlockSpec((tk, tm), lambda i, j, k: (k, i)), b_spec=pl.BlockSpec((tk, tn), lambda i, j, k: (k, j)),
        dims=TN, acc_shape=(tm, tn), outs=[out], out_spec=out_spec, a_fn=a_fn, after=after)[0]
    return res.reshape(N_DEV, kk // N_DEV, n) if kind == "row" else res


def _rowwise(name, fn, ins, outs, rows, tile):
    arrays, specs = [], []
    for arr, kind in ins:
        arrays.append(arr)
        if kind == "row":
            specs.append(pl.BlockSpec((tile, arr.shape[1]), lambda i: (i, 0)))
        elif kind == "vec":
            specs.append(pl.BlockSpec(arr.shape, lambda i, nd=arr.ndim: (0,) * nd))
        else:
            specs.append(kind)
    out_shapes, out_specs, kinds = [], [], []
    for cols, dt, kind in outs:
        kinds.append(kind)
        if kind == "row":
            out_shapes.append(SDS((rows, cols), dt))
            out_specs.append(pl.BlockSpec((tile, cols), lambda i: (i, 0)))
        else:
            out_shapes.append(SDS((1, cols), F32))
            out_specs.append(pl.BlockSpec((1, cols), lambda i: (0, 0)))
    ni = len(arrays)
    has_acc = "acc" in kinds

    def body(*refs):
        vals = fn(*[r[...] for r in refs[:ni]])
        i = pl.program_id(0)
        for o, v, kind in zip(refs[ni:], vals, kinds):
            if kind == "row":
                o[...] = v.astype(o.dtype)
            else:
                @pl.when(i == 0)
                def _(o=o):
                    o[...] = jnp.zeros_like(o)

                o[...] += v

    return pl.pallas_call(
        body, name=name, grid=(rows // tile,), out_shape=out_shapes, in_specs=specs, out_specs=out_specs,
        compiler_params=_cp("arbitrary" if has_acc else "parallel"),
    )(*arrays)


def _ln_fwd(name, u, g, b):
    d = u.shape[1]

    def fn(u, g, b):
        mu = jnp.mean(u, axis=1, keepdims=True)
        xc = u - mu
        var = jnp.mean(xc * xc, axis=1, keepdims=True)
        y = xc * lax.rsqrt(var + LN_EPS) * g + b
        return y, y

    return _rowwise(name, fn, [(u, "row"), (g, "vec"), (b, "vec")], [(d, F32, "row"), (d, BF16, "row")], u.shape[0], 256)


def _ln_bwd(name, u, dy, g):
    d = u.shape[1]

    def fn(u, dy, g):
        mu = jnp.mean(u, axis=1, keepdims=True)
        xc = u - mu
        var = jnp.mean(xc * xc, axis=1, keepdims=True)
        rstd = lax.rsqrt(var + LN_EPS)
        xhat = xc * rstd
        dxhat = dy * g
        m1 = jnp.mean(dxhat, axis=1, keepdims=True)
        m2 = jnp.mean(dxhat * xhat, axis=1, keepdims=True)
        du = rstd * (dxhat - m1 - xhat * m2)
        return du, du, jnp.sum(dy * xhat, axis=0, keepdims=True), jnp.sum(dy, axis=0, keepdims=True)

    return _rowwise(name, fn, [(u, "row"), (dy, "row"), (g, "vec")],
                    [(d, F32, "row"), (d, BF16, "row"), (d, F32, "acc"), (d, F32, "acc")], u.shape[0], 256)


def _loss_bwd(name, y, target):
    d = y.shape[1]

    def fn(y, t):
        e = y - t
        return e * (1.0 / d), jnp.sum(e * e, axis=0, keepdims=True) * (0.5 / d)

    return _rowwise(name, fn, [(y, "row"), (target, "row")], [(d, F32, "row"), (d, F32, "acc")], y.shape[0], 256)


def _ple_bwd(name, dx, e, gate):
    d = dx.shape[1]

    def fn(dx, e, gate):
        return dx * e * gate * (1.0 - gate), dx * gate

    return _rowwise(name, fn, [(dx, "row"), (e, "row"), (gate, "row")], [(d, BF16, "row"), (d, BF16, "row")],
                    dx.shape[0], 256)


def _sigmoid(v):
    return 1.0 / (1.0 + jnp.exp(-v))


def _gated_rms_fwd(name, y, zx, norm_w):
    di = y.shape[1]

    def fn(y, z, w):
        yg = y * (z * _sigmoid(z))
        r = lax.rsqrt(jnp.mean(yg * yg, axis=1, keepdims=True) + RMS_EPS)
        return (yg * r * w,)

    z_spec = pl.BlockSpec((128, di), lambda i: (i, 0))
    return _rowwise(name, fn, [(y, "row"), (zx, z_spec), (norm_w, "vec")], [(di, BF16, "row")], y.shape[0], 128)[0]


def _gated_rms_bwd(name, y, zx, norm_w, dout):
    di = y.shape[1]

    def fn(y, z, w, dout):
        sg = _sigmoid(z)
        sz = z * sg
        yg = y * sz
        r = lax.rsqrt(jnp.mean(yg * yg, axis=1, keepdims=True) + RMS_EPS)
        dn = dout * w
        dyg = r * (dn - yg * (r * r) * jnp.mean(dn * yg, axis=1, keepdims=True))
        dy = dyg * sz
        dz = dyg * y * (sg * (1.0 + z * (1.0 - sg)))
        return dy, dz, jnp.sum(dout * yg * r, axis=0, keepdims=True)

    z_spec = pl.BlockSpec((128, di), lambda i: (i, 0))
    return _rowwise(name, fn, [(y, "row"), (zx, z_spec), (norm_w, "vec"), (dout, "row")],
                    [(di, F32, "row"), (di, BF16, "row"), (di, F32, "acc")], y.shape[0], 128)


def _shift_down(v, j, row):
    return jnp.where(row >= j, pltpu.roll(v, j, 0), 0.0)


def _shift_up(v, j, row):
    t = v.shape[0]
    return jnp.where(row < t - j, pltpu.roll(v, t - j, 0), 0.0)


def _pool_select(parts, g):
    return jnp.where(g == 0, parts[0], jnp.where(g == 1, parts[1], jnp.where(g == 2, parts[2], parts[3])))


def _pool_windows(name, x, transpose, scale_by=None, after=()):
    t, d = x.shape
    cg = d // 4
    cw = V7X_LANES
    per = cg // cw

    def body(*refs):
        x_ref, o_ref = refs[0], refs[-1]
        g = pl.program_id(0) // per
        xv = x_ref[...]
        row = lax.broadcasted_iota(jnp.int32, (t, 1), 0)
        cnt = jnp.minimum(row + 1, jnp.left_shift(2, g)).astype(F32)
        s = xv / cnt if transpose else xv
        parts = []
        for lg in POOL_WINDOW_LOG2:
            j = 1 << (lg - 1)
            s = s + (_shift_up(s, j, row) if transpose else _shift_down(s, j, row))
            parts.append(s)
        sel = _pool_select(parts, g)
        if transpose:
            o_ref[...] = ALPHA * refs[1][...] + sel - xv
        else:
            o_ref[...] = (sel / cnt - xv).astype(o_ref.dtype)

    col = pl.BlockSpec((t, cw), lambda j: (0, j))
    ins = [x] if scale_by is None else [x, scale_by]
    return pl.pallas_call(
        body, name=name, grid=(d // cw,), out_shape=SDS((t, d), F32 if transpose else BF16),
        in_specs=[col] * len(ins) + [ANY] * len(after), out_specs=col, compiler_params=_cp("parallel"),
    )(*ins, *after)


def _pool_mm(name, pooled, w, scale, x):
    t, d = x.shape
    cg = d // 4
    tm = _pick(t, 1024)

    def body(p_ref, w_ref, s_ref, x_ref, u_ref, h_ref):
        h = jnp.dot(p_ref[...], w_ref[...], preferred_element_type=F32)
        h_ref[...] = h
        u_ref[...] = ALPHA * x_ref[...] + h * s_ref[...]

    blk = pl.BlockSpec((tm, cg), lambda g, i: (i, g))
    return pl.pallas_call(
        body, name=name, grid=(4, t // tm), out_shape=[SDS((t, d), F32), SDS((t, d), F32)],
        in_specs=[blk, pl.BlockSpec((None, cg, cg), lambda g, i: (g, 0, 0)), pl.BlockSpec((1, cg), lambda g, i: (0, g)),
                  blk],
        out_specs=[blk, blk], compiler_params=_cp("parallel", "parallel"),
    )(pooled, w, scale, x)


def _pool_bwd_mm(name, du, hraw, w, scale):
    t, d = du.shape
    cg = d // 4
    tm = _pick(t, 1024)

    def body(du_ref, h_ref, w_ref, s_ref, dh_ref, dp_ref, ds_ref):
        @pl.when(pl.program_id(1) == 0)
        def _():
            ds_ref[...] = jnp.zeros_like(ds_ref)

        duv = du_ref[...]
        ds_ref[...] += jnp.sum(duv * h_ref[...], axis=0, keepdims=True)
        dh = (duv * s_ref[...]).astype(BF16)
        dh_ref[...] = dh
        dp_ref[...] = lax.dot_general(dh, w_ref[...], NT, preferred_element_type=F32)

    blk = pl.BlockSpec((tm, cg), lambda g, i: (i, g))
    vec = pl.BlockSpec((1, cg), lambda g, i: (0, g))
    return pl.pallas_call(
        body, name=name, grid=(4, t // tm), out_shape=[SDS((t, d), BF16), SDS((t, d), F32), SDS((1, d), F32)],
        in_specs=[blk, blk, pl.BlockSpec((None, cg, cg), lambda g, i: (g, 0, 0)), vec],
        out_specs=[blk, blk, vec], compiler_params=_cp("parallel", "arbitrary"),
    )(du, hraw, w, scale)


def _pool_dw(name, pooled, dh):
    t, d = pooled.shape
    cg = d // 4
    tk = _pick(t, 512)
    nk = t // tk

    def body(p_ref, dh_ref, o_ref, acc):
        k = pl.program_id(1)

        @pl.when(k == 0)
        def _():
            acc[...] = jnp.zeros_like(acc)

        acc[...] += lax.dot_general(p_ref[...], dh_ref[...], TN, preferred_element_type=F32)

        @pl.when(k == nk - 1)
        def _():
            o_ref[...] = acc[...].astype(o_ref.dtype)

    blk = pl.BlockSpec((tk, cg), lambda g, k: (k, g))
    return pl.pallas_call(
        body, name=name, grid=(4, nk), out_shape=SDS((4, cg, cg), BF16), in_specs=[blk, blk],
        out_specs=pl.BlockSpec((None, cg, cg), lambda g, k: (g, 0, 0)), scratch_shapes=[pltpu.VMEM((cg, cg), F32)],
        compiler_params=_cp("parallel", "arbitrary"),
    )(pooled, dh)


def _conv_pre(u, w_ref, b_ref, row):
    pre = b_ref[...] + _shift_down(u, 3, row) * w_ref[0:1, :]
    pre = pre + _shift_down(u, 2, row) * w_ref[1:2, :]
    pre = pre + _shift_down(u, 1, row) * w_ref[2:3, :]
    return pre + u * w_ref[3:4, :]


def _conv_fwd(name, zx, conv_w, conv_b, di):
    t = zx.shape[0]
    cd = conv_w.shape[1]
    cw = _pick(cd, 256)
    off = di // cw

    def body(u_ref, w_ref, b_ref, o_ref):
        row = lax.broadcasted_iota(jnp.int32, (t, 1), 0)
        pre = _conv_pre(u_ref[...], w_ref, b_ref, row)
        o_ref[...] = pre * _sigmoid(pre)

    return pl.pallas_call(
        body, name=name, grid=(cd // cw,), out_shape=SDS((t, cd), F32),
        in_specs=[pl.BlockSpec((t, cw), lambda j: (0, off + j)), pl.BlockSpec((CONV_WIDTH, cw), lambda j: (0, j)),
                  pl.BlockSpec((1, cw), lambda j: (0, j))],
        out_specs=pl.BlockSpec((t, cw), lambda j: (0, j)), compiler_params=_cp("parallel"),
    )(zx, conv_w, conv_b)


def _conv_bwd(name, zx, conv_w, conv_b, dact, di, first):
    t, cd = dact.shape
    cw = _pick(cd, 256)
    off, woff = (di + first) // cw, first // cw

    def body(u_ref, w_ref, b_ref, da_ref, du_ref, dw_ref, db_ref):
        row = lax.broadcasted_iota(jnp.int32, (t, 1), 0)
        u = u_ref[...]
        pre = _conv_pre(u, w_ref, b_ref, row)
        sg = _sigmoid(pre)
        dpre = da_ref[...] * (sg * (1.0 + pre * (1.0 - sg)))
        du = dpre * w_ref[3:4, :]
        for j in (1, 2, 3):
            du = du + _shift_up(dpre, j, row) * w_ref[3 - j:4 - j, :]
            dw_ref[3 - j:4 - j, :] = jnp.sum(dpre * _shift_down(u, j, row), axis=0, keepdims=True)
        dw_ref[3:4, :] = jnp.sum(dpre * u, axis=0, keepdims=True)
        db_ref[...] = jnp.sum(dpre, axis=0, keepdims=True)
        du_ref[...] = du.astype(du_ref.dtype)

    wspec = pl.BlockSpec((CONV_WIDTH, cw), lambda j: (0, j))
    bspec = pl.BlockSpec((1, cw), lambda j: (0, j))
    ospec = pl.BlockSpec((t, cw), lambda j: (0, j))
    return pl.pallas_call(
        body, name=name, grid=(cd // cw,), out_shape=[SDS((t, cd), BF16), SDS((CONV_WIDTH, cd), F32), SDS((1, cd), F32)],
        in_specs=[pl.BlockSpec((t, cw), lambda j: (0, off + j)), pl.BlockSpec((CONV_WIDTH, cw), lambda j: (0, woff + j)),
                  pl.BlockSpec((1, cw), lambda j: (0, woff + j)), ospec],
        out_specs=[ospec, wspec, bspec], compiler_params=_cp("parallel"),
    )(zx, conv_w, conv_b, dact)


def _expand_heads(name, arrays, h_n, p):
    t = arrays[0].shape[0]
    n = len(arrays)
    w = _pick(h_n * p, 512)

    def body(*refs):
        j = pl.program_id(0)
        head = lax.broadcasted_iota(jnp.int32, (V7X_LANES, w), 0)
        lane = lax.broadcasted_iota(jnp.int32, (V7X_LANES, w), 1)
        spread = (head == j * (w // p) + lane // p).astype(BF16)
        for a_ref, o_ref in zip(refs[:n], refs[n:]):
            rest = a_ref[...]
            out = jnp.zeros((t, w), F32)
            for _ in range(3):
                piece = rest.astype(BF16)
                out = out + jnp.dot(piece, spread, preferred_element_type=F32)
                rest = rest - piece.astype(F32)
            o_ref[...] = out

    full = pl.BlockSpec((t, V7X_LANES), lambda j: (0, 0))
    return pl.pallas_call(
        body, name=name, grid=(h_n * p // w,), out_shape=[SDS((t, h_n * p), F32)] * n, in_specs=[full] * n,
        out_specs=[pl.BlockSpec((t, w), lambda j: (0, j))] * n, compiler_params=_cp("parallel"),
    )(*arrays)


def _softplus(v):
    return jnp.maximum(v, 0.0) + jnp.log(1.0 + jnp.exp(-jnp.abs(v)))


def _dt_fwd(name, zx, bias, a_log, col_block):
    t = zx.shape[0]

    def body(r_ref, b_ref, al_ref, dt_ref, acs_ref, ein_ref, eout_ref):
        row = lax.broadcasted_iota(jnp.int32, (t, 1), 0) % CHUNK
        dt = _softplus(r_ref[...] + b_ref[...])
        da = dt * (-jnp.exp(al_ref[...]))
        s, r = da, da
        j = 1
        while j < CHUNK:
            s = s + jnp.where(row >= j, pltpu.roll(s, j, 0), 0.0)
            r = r + jnp.where(row < CHUNK - j, pltpu.roll(r, t - j, 0), 0.0)
            j *= 2
        dt_ref[...] = dt
        acs_ref[...] = s
        ein_ref[...] = jnp.exp(s)
        eout_ref[...] = jnp.exp(r - da)

    vec = pl.BlockSpec((1, V7X_LANES), lambda i: (0, 0))
    full = pl.BlockSpec((t, V7X_LANES), lambda i: (0, 0))
    return pl.pallas_call(
        body, name=name, grid=(1,), out_shape=[SDS((t, V7X_LANES), F32)] * 4,
        in_specs=[pl.BlockSpec((t, V7X_LANES), lambda i: (0, col_block)), vec, vec], out_specs=[full] * 4,
        compiler_params=_cp("arbitrary"),
    )(zx, bias, a_log)


def _dt_bwd(name, zx, bias, a_log, d_acs, d_dt, col_block):
    t = zx.shape[0]

    def body(r_ref, b_ref, al_ref, da_ref, dd_ref, draw_ref, db_ref, dal_ref):
        row = lax.broadcasted_iota(jnp.int32, (t, 1), 0) % CHUNK
        pre = r_ref[...] + b_ref[...]
        dt = _softplus(pre)
        a = -jnp.exp(al_ref[...])
        s = da_ref[...]
        j = 1
        while j < CHUNK:
            s = s + jnp.where(row < CHUNK - j, pltpu.roll(s, t - j, 0), 0.0)
            j *= 2
        ddt = dd_ref[...] + s * a
        dal_ref[...] = jnp.sum(s * dt, axis=0, keepdims=True) * a
        draw = ddt * _sigmoid(pre)
        db_ref[...] = jnp.sum(draw, axis=0, keepdims=True)
        draw_ref[...] = draw.astype(draw_ref.dtype)

    vec = pl.BlockSpec((1, V7X_LANES), lambda i: (0, 0))
    full = pl.BlockSpec((t, V7X_LANES), lambda i: (0, 0))
    return pl.pallas_call(
        body, name=name, grid=(1,), out_shape=[SDS((t, V7X_LANES), BF16), SDS((1, V7X_LANES), F32), SDS((1, V7X_LANES), F32)],
        in_specs=[pl.BlockSpec((t, V7X_LANES), lambda i: (0, col_block)), vec, vec, full, full],
        out_specs=[full, vec, vec], compiler_params=_cp("arbitrary"),
    )(zx, bias, a_log, d_acs, d_dt)


def _ssd_specs(t, di, g_n, hpg, p, rev):
    nc = t // CHUNK
    w = hpg * p
    nb = di // D_STATE

    def cc(c):
        return nc - 1 - c if rev else c

    return dict(
        xs=pl.BlockSpec((CHUNK, w), lambda g, c: (cc(c), g)),
        bm=pl.BlockSpec((CHUNK, D_STATE), lambda g, c: (cc(c), nb + g)),
        cm=pl.BlockSpec((CHUNK, D_STATE), lambda g, c: (cc(c), nb + g_n + g)),
        col=pl.BlockSpec((None, CHUNK, hpg), lambda g, c: (g, cc(c), 0)),
        rowv=pl.BlockSpec((None, hpg, CHUNK), lambda g, c: (g, 0, cc(c))),
        head=pl.BlockSpec((None, 1, hpg), lambda g, c: (g, 0, 0)),
        lanes=pl.BlockSpec((1, w), lambda g, c: (0, g)),
        bc=pl.BlockSpec((CHUNK, D_STATE), lambda g, c: (cc(c), g)),
        prev=pl.BlockSpec((None, None, D_STATE, w), lambda g, c: (cc(c), g, 0, 0)),
        seg=pl.BlockSpec((w, V7X_LANES), lambda g, c: (0, 0)),
    )


def _decay_masks(cb, ac_ref, ar_ref, heads):
    li = lax.broadcasted_iota(jnp.int32, (CHUNK, CHUNK), 0)
    si = lax.broadcasted_iota(jnp.int32, (CHUNK, CHUNK), 1)
    lms = [jnp.exp(jnp.where(li >= si, ac_ref[:, hh:hh + 1] - ar_ref[hh:hh + 1, :], -jnp.inf)) for hh in heads]
    return lms, [(cb * lm).astype(BF16) for lm in lms]


def _ssd_fwd(name, xbc, dt_x, ein_x, eout_x, a_col, a_row, d_x, di, g_n, hpg, p):
    t = xbc.shape[0]
    nc = t // CHUNK
    w = hpg * p
    assert 2 * p == V7X_LANES and hpg % 2 == 0
    sp = _ssd_specs(t, di, g_n, hpg, p, False)

    def body(xs_ref, bm_ref, cm_ref, dt_ref, ein_ref, eout_ref, ac_ref, ar_ref, d_ref, y_ref, prev_ref, h_ref):
        @pl.when(pl.program_id(1) == 0)
        def _():
            h_ref[...] = jnp.zeros_like(h_ref)

        bm = bm_ref[...].astype(BF16)
        cm = cm_ref[...].astype(BF16)
        cb = lax.dot_general(cm, bm, NT, preferred_element_type=F32)
        first = lax.broadcasted_iota(jnp.int32, (1, V7X_LANES), 1) < p
        xs = xs_ref[...]
        e_in = ein_ref[...]
        xdt = xs * dt_ref[...]
        ys = []
        for pr in range(hpg // 2):
            _, ms = _decay_masks(cb, ac_ref, ar_ref, (2 * pr, 2 * pr + 1))
            xp = xdt[:, pr * V7X_LANES:(pr + 1) * V7X_LANES]
            rhs = jnp.concatenate([jnp.where(first, xp, 0.0), jnp.where(first, 0.0, xp)], axis=0).astype(BF16)
            ys.append(jnp.dot(jnp.concatenate(ms, axis=1), rhs, preferred_element_type=F32))
        h_prev = h_ref[...]
        prev_ref[...] = h_prev
        y = jnp.concatenate(ys, axis=1) + jnp.dot(cm, h_prev.astype(BF16), preferred_element_type=F32) * e_in
        y_ref[...] = y + xs * d_ref[...]
        st = lax.dot_general(bm, (xdt * eout_ref[...]).astype(BF16), TN, preferred_element_type=F32)
        h_ref[...] = e_in[CHUNK - 1:CHUNK, :] * h_prev + st

    return pl.pallas_call(
        body, name=name, grid=(g_n, nc),
        out_shape=[SDS((t, di), F32), SDS((nc, g_n, D_STATE, w), F32)],
        in_specs=[sp["xs"], sp["bm"], sp["cm"], sp["xs"], sp["xs"], sp["xs"], sp["col"], sp["rowv"], sp["lanes"]],
        out_specs=[sp["xs"], sp["prev"]], scratch_shapes=[pltpu.VMEM((D_STATE, w), F32)],
        compiler_params=_cp("parallel", "arbitrary"),
    )(xbc, xbc, xbc, dt_x, ein_x, eout_x, a_col, a_row, d_x)


def _head_sums(v, seg):
    hi = v.astype(BF16)
    lo = (v - hi.astype(F32)).astype(BF16)
    return jnp.dot(hi, seg, preferred_element_type=F32) + jnp.dot(lo, seg, preferred_element_type=F32)


def _head_totals(v, seg):
    part = v[0:8]
    for r in range(8, v.shape[0], 8):
        part = part + v[r:r + 8]
    return jnp.sum(_head_sums(part, seg), axis=0, keepdims=True)


def _ssd_bwd(name, xbc, dt_x, ein_x, eout_x, a_col, a_row, d_x, prev, dy, di, g_n, hpg, p):
    t = xbc.shape[0]
    nc = t // CHUNK
    w = hpg * p
    sp = _ssd_specs(t, di, g_n, hpg, p, True)
    seg = (lax.broadcasted_iota(jnp.int32, (w, V7X_LANES), 0) // p
           == lax.broadcasted_iota(jnp.int32, (w, V7X_LANES), 1)).astype(BF16)

    def body(xs_ref, bm_ref, cm_ref, dt_ref, ein_ref, eout_ref, ac_ref, ar_ref, d_ref, prev_ref, dy_ref,
             seg_ref, dx_ref, dbm_ref, dcm_ref, ddt_ref, dacs_ref, dd_ref, dh_ref):
        @pl.when(pl.program_id(1) == 0)
        def _():
            dh_ref[...] = jnp.zeros_like(dh_ref)
            dd_ref[...] = jnp.zeros_like(dd_ref)

        bm = bm_ref[...].astype(BF16)
        cm = cm_ref[...].astype(BF16)
        cb = lax.dot_general(cm, bm, NT, preferred_element_type=F32)
        first = lax.broadcasted_iota(jnp.int32, (1, V7X_LANES), 1) < p
        last_row = lax.broadcasted_iota(jnp.int32, (CHUNK, 1), 0) == CHUNK - 1
        seg_m = seg_ref[...]
        xs, dy, e_in, e_out, d_skip = xs_ref[...], dy_ref[...], ein_ref[...], eout_ref[...], d_ref[...]
        dt_l = dt_ref[...]
        xdt = xs * dt_l
        h_prev = prev_ref[...]
        h_prev_b = h_prev.astype(BF16)
        dh_next = dh_ref[...]
        dh_next_b = dh_next.astype(BF16)
        dy_e = (dy * e_in).astype(BF16)
        d_cm = lax.dot_general(dy_e, h_prev_b, NT, preferred_element_type=F32)
        dh_ref[...] = e_in[CHUNK - 1:CHUNK, :] * dh_next + lax.dot_general(cm, dy_e, TN, preferred_element_type=F32)
        q = jnp.dot(bm, dh_next_b, preferred_element_type=F32)
        xf = xdt * e_out
        d_bm = lax.dot_general(xf.astype(BF16), dh_next_b, NT, preferred_element_type=F32)
        d_cb = jnp.zeros((CHUNK, CHUNK), F32)
        parts, w_parts = [], []
        for pr in range(hpg // 2):
            lanes = slice(pr * V7X_LANES, (pr + 1) * V7X_LANES)
            lms, ms = _decay_masks(cb, ac_ref, ar_ref, (2 * pr, 2 * pr + 1))
            xp = xdt[:, lanes]
            xp_b = xp.astype(BF16)
            dyp = dy[:, lanes]
            halves = [jnp.where(first, dyp, 0.0).astype(BF16), jnp.where(first, 0.0, dyp).astype(BF16)]
            for lm, half in zip(lms, halves):
                d_cb = d_cb + lax.dot_general(half, xp_b, NT, preferred_element_type=F32) * lm
            dxd = lax.dot_general(jnp.concatenate(ms, axis=0), jnp.concatenate(halves, axis=0), TN,
                                  preferred_element_type=F32)
            stacked = jnp.concatenate([jnp.where(first, xp, 0.0), jnp.where(first, 0.0, xp)], axis=0).astype(BF16)
            y_diag = jnp.dot(jnp.concatenate(ms, axis=1), stacked, preferred_element_type=F32)
            parts.append(dxd)
            w_parts.append(dyp.astype(BF16).astype(F32) * y_diag - xp_b.astype(F32) * dxd)
        d_xdt = jnp.concatenate(parts, axis=1) + q * e_out
        dx_ref[...] = d_xdt * dt_l + dy * d_skip
        ch = jnp.dot(cm, h_prev_b, preferred_element_type=F32)
        qx = q * xf
        s_a = _head_sums(dy * ch * e_in - qx + jnp.concatenate(w_parts, axis=1), seg_m)[:, :hpg]
        d_last = (_head_totals(qx, seg_m)[:, :hpg]
                  + jnp.exp(ac_ref[CHUNK - 1:CHUNK, :]) * _head_totals(dh_next * h_prev, seg_m)[:, :hpg])
        ddt_ref[...] = _head_sums(d_xdt * xs, seg_m)[:, :hpg]
        dacs_ref[...] = s_a + jnp.where(last_row, d_last, 0.0)
        dd_ref[...] += _head_totals(dy * xs, seg_m)[:, :hpg]
        d_cb_b = d_cb.astype(BF16)
        dcm_ref[...] = d_cm + jnp.dot(d_cb_b, bm, preferred_element_type=F32)
        dbm_ref[...] = d_bm + lax.dot_general(d_cb_b, cm, TN, preferred_element_type=F32)

    gn = g_n * D_STATE
    return pl.pallas_call(
        body, name=name, grid=(g_n, nc),
        out_shape=[SDS((t, di), F32), SDS((t, gn), F32), SDS((t, gn), F32), SDS((g_n, t, hpg), F32),
                   SDS((g_n, t, hpg), F32), SDS((g_n, 1, hpg), F32)],
        in_specs=[sp["xs"], sp["bm"], sp["cm"], sp["xs"], sp["xs"], sp["xs"], sp["col"], sp["rowv"],
                  sp["lanes"], sp["prev"], sp["xs"], sp["seg"]],
        out_specs=[sp["xs"], sp["bc"], sp["bc"], sp["col"], sp["col"], sp["head"]],
        scratch_shapes=[pltpu.VMEM((D_STATE, w), F32)],
        compiler_params=_cp("parallel", "arbitrary"),
    )(xbc, xbc, xbc, dt_x, ein_x, eout_x, a_col, a_row, d_x, prev, dy, seg)


def _as3d(a):
    return a.reshape(a.shape[0], -1, a.shape[-1])


def _pair_sum(name, own, recv, core):
    shape = recv.shape
    cols = shape[-1]
    own3, recv3 = own.reshape(8, -1, cols), recv.reshape(4, -1, cols)
    rows = recv3.shape[1]
    tr = _row_tile(rows, cols, 2)

    def body(c_ref, a_ref, b_ref, o_ref):
        o_ref[...] = (a_ref[...].astype(F32) + b_ref[...].astype(F32)).astype(o_ref.dtype)

    blk = pl.BlockSpec((None, tr, cols), lambda q, i, c_ref: (q, i, 0))
    out = pl.pallas_call(
        body, name=name, out_shape=SDS(recv3.shape, recv.dtype),
        grid_spec=pltpu.PrefetchScalarGridSpec(
            num_scalar_prefetch=1, grid=(4, rows // tr),
            in_specs=[pl.BlockSpec((None, tr, cols), lambda q, i, c_ref: (2 * q + c_ref[0], i, 0)), blk], out_specs=blk),
        compiler_params=_cp("parallel", "parallel"),
    )(core, own3, recv3)
    return out.reshape(shape)


def _adamw(name, w, m, v, parts, layer, prev=None, sel=None):
    lyr, rows, cols = w.shape
    n = len(parts)
    tr = _row_tile(rows, cols)
    np_ = 0 if prev is None else 4
    if sel is None:
        sel = jnp.zeros((1,), jnp.int32)

    def body(sel_ref, *refs):
        w_ref, m_ref, v_ref = refs[:3]
        p_refs = refs[3:3 + n]
        g_ref, d_ref, nm_ref, nv_ref = refs[3 + n + np_:]
        g = p_refs[0][...].astype(F32)
        for r in p_refs[1:]:
            g = g + r[...].astype(F32)
        nm = ADAM_B1 * m_ref[...] + (1.0 - ADAM_B1) * g
        nv = ADAM_B2 * v_ref[...] + (1.0 - ADAM_B2) * (g * g)
        m_hat = nm / (1.0 - ADAM_B1 ** ADAM_STEP)
        v_hat = nv / (1.0 - ADAM_B2 ** ADAM_STEP)
        g_ref[...] = g
        d_ref[...] = -ADAM_LR * (m_hat / (jnp.sqrt(v_hat) + ADAM_EPS) + ADAM_WD * w_ref[...])
        nm_ref[...] = nm
        nv_ref[...] = nv

    lspec = pl.BlockSpec((None, tr, cols), lambda i, s: (layer, i, 0))
    pspecs = [pl.BlockSpec((None, tr, cols), (lambda i, s: (s[0], i, 0)) if q is None else (lambda i, s, q=q: (q, i, 0)))
              for _, q in parts]
    aliases = {} if prev is None else {4 + n + q: q for q in range(4)}
    return pl.pallas_call(
        body, name=name, out_shape=[SDS(w.shape, F32)] * 4,
        grid_spec=pltpu.PrefetchScalarGridSpec(
            num_scalar_prefetch=1, grid=(rows // tr,), in_specs=[lspec] * 3 + pspecs + [ANY] * np_,
            out_specs=[lspec] * 4),
        input_output_aliases=aliases, compiler_params=_cp("parallel"),
    )(sel, w, m, v, *[arr for arr, _ in parts], *(prev or ()))


def _sum8(name, parts):
    rows = parts.shape[1]

    def body(p_ref, o_ref):
        s = p_ref[0]
        for q in range(1, N_DEV):
            s = s + p_ref[q]
        o_ref[...] = s

    return pl.pallas_call(
        body, name=name, grid=(1,), out_shape=SDS((rows, V7X_LANES), F32),
        in_specs=[pl.BlockSpec((N_DEV, rows, V7X_LANES), lambda i: (0, 0, 0))],
        out_specs=pl.BlockSpec((rows, V7X_LANES), lambda i: (0, 0)), compiler_params=_cp("arbitrary"),
    )(parts)


def _pack(vectors, align):
    flat = jnp.concatenate([v.reshape(-1) for v in vectors])
    pad = (-flat.shape[0]) % align
    if pad:
        flat = jnp.concatenate([flat, jnp.zeros((pad,), F32)])
    return flat.reshape(-1, V7X_LANES)


def _unpack(packed, shapes):
    flat = packed.reshape(-1)
    out, o = [], 0
    for s in shapes:
        size = 1
        for dim in s:
            size *= dim
        out.append(flat[o:o + size].reshape(s))
        o += size
    return out


def _residual_epilogue(acc, res):
    return (ALPHA * res + acc,)


def _plain_add_epilogue(acc, res):
    return (res + acc,)


def _gate_epilogue(acc, y, e):
    gate = _sigmoid(acc)
    xn = y + gate * e
    return xn, gate, xn


def _relu2(pre):
    r = jnp.maximum(pre, 0.0)
    return r * r


def _relu2_bwd_epilogue(acc, pre):
    return (acc * (2.0 * jnp.maximum(pre.astype(F32), 0.0)),)


def _tail_fwd(tag, u_a, wts, lng, lnb, p_l, finish_w2):
    y1, y1_b = _ln_fwd(f"ln1_{tag}", u_a, lng[0], lnb[0])
    (pre,) = _mm_fwd(f"mlp1_{tag}", y1_b, wts["w1"], "col", [BF16])
    finish_w2(pre)
    (u_b,) = _mm_fwd(f"mlp2_{tag}", pre, wts["w2"], "row", [F32], _residual_epilogue, (y1,), a_fn=_relu2)
    y2, y2_b = _ln_fwd(f"ln2_{tag}", u_b, lng[1], lnb[1])
    (e,) = _mm_fwd(f"ple_{tag}", p_l, wts["plew"], "col", [F32])
    xn, gate, xn_b = _mm_fwd(f"gate_{tag}", y2_b, wts["gate"], "row", [F32, F32, BF16], _gate_epilogue, (y2, e))
    return xn, xn_b, (u_a, y1_b, pre, u_b, y2_b, e, gate)


def _tail_bwd(tag, dxn, saved, wts, lng, p_l, emit, advance, toks):
    u_a, y1_b, pre, u_b, y2_b, e, gate = saved
    dgpre, de = _ple_bwd(f"ple_bwd_{tag}", dxn, e, gate)
    toks = emit(f"{tag}_ple", dict(gate=_mm_dw(f"gate_dw_{tag}", y2_b, dgpre, "row", after=toks),
                                   plew=_mm_dw(f"ple_dw_{tag}", p_l, de, "col")))
    (dy2,) = _mm_dx(f"gate_dx_{tag}", dgpre, wts["gate"], "row", [F32], _plain_add_epilogue, (dxn,), after=toks)
    toks = advance((dy2,))
    du_b, du_b16, dg2, db2 = _ln_bwd(f"ln2_bwd_{tag}", u_b, dy2, lng[1])
    toks = emit(f"{tag}_w2", dict(w2=_mm_dw(f"mlp2_dw_{tag}", pre, du_b16, "row", a_fn=_relu2, after=toks)))
    (dpre,) = _mm_dx(f"mlp2_dx_{tag}", du_b16, wts["w2"], "row", [BF16], _relu2_bwd_epilogue, (pre,), after=toks)
    toks = advance((dpre,))
    toks = emit(f"{tag}_w1", dict(w1=_mm_dw(f"mlp1_dw_{tag}", y1_b, dpre, "col", after=toks)))
    (dy1,) = _mm_dx(f"mlp1_dx_{tag}", dpre, wts["w1"], "col", [F32], _residual_epilogue, (du_b,), after=toks)
    toks = advance((dy1,))
    du_a, du_a16, dg1, db1 = _ln_bwd(f"ln1_bwd_{tag}", u_a, dy1, lng[0])
    return du_a, du_a16, [dg1, dg2], [db1, db2], toks


def _to_slots(a, axis):
    shape = a.shape
    per = shape[axis] // N_DEV
    v = a.reshape(shape[:axis] + (N_DEV, per) + shape[axis + 1:])
    return jnp.moveaxis(v, axis, 0)


def _pad_lanes(a):
    return jnp.pad(a, [(0, 0)] * (a.ndim - 1) + [(0, V7X_LANES - a.shape[-1])])


def kernel(x, p, pool_w, pool_scale, ssm_in_w, ssm_conv_w, ssm_conv_b, ssm_dt_bias, ssm_a_log, ssm_d, ssm_norm_w, ssm_out_w, mlp_w1, mlp_w2, ln_g, ln_b, ple_w, ple_gate_w, loss_target, m_pool_w, m_pool_scale, m_ssm_in_w, m_ssm_conv_w, m_ssm_conv_b, m_ssm_dt_bias, m_ssm_a_log, m_ssm_d, m_ssm_norm_w, m_ssm_out_w, m_mlp_w1, m_mlp_w2, m_ln_g, m_ln_b, m_ple_w, m_ple_gate_w, v_pool_w, v_pool_scale, v_ssm_in_w, v_ssm_conv_w, v_ssm_conv_b, v_ssm_dt_bias, v_ssm_a_log, v_ssm_d, v_ssm_norm_w, v_ssm_out_w, v_mlp_w1, v_mlp_w2, v_ln_g, v_ln_b, v_ple_w, v_ple_gate_w):
    t, d = x.shape[1:]
    h_n = ssm_dt_bias.shape[-1]
    di_s, cd_s, dp_s, d_s = ssm_norm_w.shape[-1], ssm_conv_b.shape[-1], ssm_in_w.shape[-1], ln_g.shape[-1]
    di, cd, dp = N_DEV * di_s, N_DEV * cd_s, N_DEV * dp_s
    p_dim = di // h_n
    g_n = (cd - di) // (2 * D_STATE)
    hpg = h_n // g_n
    zw = di + cd
    assert h_n <= V7X_LANES and dp == zw + h_n and zw % V7X_LANES == 0 and zw % h_n == 0
    cg = d // 4
    me = 4 * lax.axis_index("x") + 2 * lax.axis_index("y") + lax.axis_index("c")

    x0, target = x[0], loss_target[0]
    p_l = [p[0, 0].astype(BF16), p[1, 0].astype(BF16)]

    small_shapes = [(CONV_WIDTH, cd_s), (1, cd_s), (1, di_s), (2, 2, d_s), (2, 2, d_s)]
    small = _pack([ssm_conv_w[0], ssm_conv_b, ssm_norm_w, ln_g, ln_b], 8 * V7X_LANES)
    first = [w.astype(BF16) for w in (pool_w[0], mlp_w1[0])]

    def gather_start(tag, own, after):
        lands = [lax.empty((N_DEV,) + w.shape, w.dtype) for w in own]
        return _async_start(f"ag_{tag}_start", _ag_first_copies, (4 * len(own),), own, lands, after)

    def gather_finish(tag, handle, after):
        n = len(handle[2]) // 2
        own, lands = _async_wait(f"ag_{tag}_wait", _ag_first_copies, handle, n, after)
        fwd = _async_start(f"ag_{tag}_forward_start", _ag_forward_copies, (3 * n,), [], lands)
        _, lands = _async_wait(f"ag_{tag}_forward_wait", _ag_forward_copies, fwd, 0)
        return [lax.dynamic_update_slice_in_dim(g, w[None], me, 0) for g, w in zip(lands, own)]

    ag_first = gather_start("first", first + [small], ())
    zero = ag_first[3][0, 0]
    own_l0 = [(w + zero).astype(BF16) for w in (mlp_w2[0], ple_w[0], ple_gate_w[0])]
    own_ssm = [(ssm_in_w[0] + zero).astype(BF16).T, (ssm_out_w[0] + zero).astype(BF16)]
    own_mlp = [(w + zero).astype(BF16) for w in (mlp_w1[1], mlp_w2[1], ple_w[1], ple_gate_w[1])]
    ag_l0 = gather_start("l0", own_l0, (ag_first[3],))
    ag_ssm = gather_start("ssm", own_ssm, (ag_l0[3],))
    ag_mlp = gather_start("mlp1", own_mlp, (ag_ssm[3],))
    pooled = _pool_windows("pool_fwd", x0, False, after=(ag_mlp[3],))
    pool_g, w1_0, small_g = gather_finish("first", ag_first, (pooled,))
    pool_full = pool_g.transpose(1, 0, 2, 3).reshape(4, cg, cg)
    sm = small_g.reshape(N_DEV, -1)
    o = 0
    parts = []
    for shp in small_shapes:
        size = 1
        for s in shp:
            size *= s
        parts.append(sm[:, o:o + size].reshape((N_DEV,) + shp))
        o += size
    conv_w_full = parts[0].transpose(1, 0, 2).reshape(CONV_WIDTH, cd)
    conv_b_full = parts[1].transpose(1, 0, 2).reshape(1, cd)
    norm_w_full = parts[2].transpose(1, 0, 2).reshape(1, di)
    ln_g_full = parts[3].transpose(1, 2, 0, 3).reshape(2, 2, 1, d)
    ln_b_full = parts[4].transpose(1, 2, 0, 3).reshape(2, 2, 1, d)
    bias128, alog128 = _pad_lanes(ssm_dt_bias), _pad_lanes(ssm_a_log)

    u0, hraw = _pool_mm("pool_mm", pooled, pool_full, pool_scale, x0)
    wts = [dict(w1=w1_0)]

    def finish_l0(after):
        w2_0, plew_0, gate_0 = gather_finish("l0", ag_l0, (after,))
        wts[0].update(w2=w2_0, plew=plew_0, gate=gate_0)

    x1, x1_b, saved0 = _tail_fwd("l0", u0, wts[0], ln_g_full[0], ln_b_full[0], p_l[0], finish_l0)

    in_g, out_g = gather_finish("ssm", ag_ssm, (x1,))
    in_t = in_g.reshape(dp, d)
    (zx,) = _mm_dx("in_proj", x1_b, in_t, "plain", [F32], k_rows=zw)
    dt_raw = _pad_lanes(_in_proj_dt("in_proj_dt", x1_b, in_t, zw, h_n))
    xbc = _conv_fwd("conv_fwd", zx, conv_w_full, conv_b_full, di)
    dt, acs, e_in, e_out = _dt_fwd("dt_fwd", dt_raw, bias128, alog128, 0)

    dt_x, ein_x, eout_x = _expand_heads("expand_heads", [dt, e_in, e_out], h_n, p_dim)
    d_x = jnp.repeat(ssm_d, p_dim, axis=1)

    def to_col(a):
        return a[:, :h_n].reshape(t, g_n, hpg).transpose(1, 0, 2)

    def to_row(a):
        return a[:, :h_n].reshape(t, g_n, hpg).transpose(1, 2, 0)

    def from_col(a):
        return _pad_lanes(a.transpose(1, 0, 2).reshape(t, h_n))

    a_col, a_row = to_col(acs), to_row(acs)
    y_ssd, prev = _ssd_fwd("ssd_fwd", xbc, dt_x, ein_x, eout_x, a_col, a_row, d_x, di, g_n, hpg, p_dim)
    yn = _gated_rms_fwd("gated_rms_fwd", y_ssd, zx, norm_w_full)
    (u2,) = _mm_fwd("out_proj", yn, out_g, "row", [F32], _residual_epilogue, (x1,))
    w1_1, w2_1, plew_1, gate_1 = gather_finish("mlp1", ag_mlp, (u2,))
    wts.append(dict(w1=w1_1, w2=w2_1, plew=plew_1, gate=gate_1))
    x2, _, saved1 = _tail_fwd("l1", u2, wts[1], ln_g_full[1], ln_b_full[1], p_l[1], lambda after: None)

    core = lax.axis_index("c").astype(jnp.int32).reshape(1)
    chip = (2 * lax.axis_index("x") + lax.axis_index("y")).astype(jnp.int32).reshape(1)
    scattering = {}
    pending = []

    def to_chips(after):
        if not pending:
            return []
        tag, names, handle = pending.pop()
        own, halves = _async_wait(f"rs_{tag}_sibling_wait", _rs_sibling_copies, handle, len(names), after)
        sums = [_pair_sum(f"rs_{tag}_pair_sum_{n}", g, hv, core) for n, g, hv in zip(names, own, halves)]
        lands = [lax.empty((3,) + s.shape[1:], s.dtype) for s in sums]
        handle = _async_start(f"rs_{tag}_start", _rs_chip_copies, (3 * len(sums),), sums, lands)
        scattering[tag] = (names, handle)
        return [handle[3]]

    def emit(tag, grads):
        names, arrays = list(grads), list(grads.values())
        toks = to_chips(tuple(arrays))
        lands = [lax.empty((4,) + g.shape[1:], g.dtype) for g in arrays]
        handle = _async_start(f"rs_{tag}_sibling_start", _rs_sibling_copies, (4 * len(arrays),), arrays, lands,
                              tuple(toks))
        pending.append((tag, names, handle))
        return toks + [handle[3]]

    def collect(tag, after):
        names, handle = scattering.pop(tag)
        sums, thirds = _async_wait(f"rs_{tag}_wait", _rs_chip_copies, handle, len(names), after)
        return {n: (s.reshape(4, -1, s.shape[-1]), r.reshape(3, -1, r.shape[-1])) for n, s, r in zip(names, sums, thirds)}

    dx2, loss_cols = _loss_bwd("loss", x2, target)
    du2, du2_b, dg_1, db_1, toks = _tail_bwd("l1", dx2, saved1, wts[1], ln_g_full[1], p_l[1], emit, to_chips, [])
    toks = emit("ssm_out", dict(out=_mm_dw("out_proj_dw", yn, du2_b, "row", after=toks)))
    (dyn,) = _mm_dx("out_proj_dx", du2_b, out_g, "row", [F32], after=toks)
    toks = to_chips((dyn,))
    dy_ssd, dz, d_norm_w = _gated_rms_bwd("gated_rms_bwd", y_ssd, zx, norm_w_full, dyn)
    dxs, dbm, dcm, ddt_x, dacs, dd = _ssd_bwd("ssd_bwd", xbc, dt_x, ein_x, eout_x, a_col, a_row, d_x, prev, dy_ssd,
                                              di, g_n, hpg, p_dim)
    draw, d_bias, d_alog = _dt_bwd("dt_bwd", dt_raw, bias128, alog128, from_col(dacs), from_col(ddt_x), 0)
    conv_parts = [_conv_bwd(f"conv_bwd_{tag}", zx, conv_w_full, conv_b_full, dact, di, first)
                  for tag, dact, first in (("xs", dxs, 0), ("b", dbm, di), ("c", dcm, di + g_n * D_STATE))]
    d_conv_w = jnp.concatenate([c[1] for c in conv_parts], axis=1)
    d_conv_b = jnp.concatenate([c[2] for c in conv_parts], axis=1)
    dzx = jnp.concatenate([dz] + [c[0] for c in conv_parts] + [draw], axis=1)
    toks = emit("ssm_in", {"in": _to_slots(_mm_dw("in_proj_dw", x1_b, dzx, "plain", after=toks)[:, :dp], 1)})
    dx_dt = _in_proj_dt_dx("in_proj_dt_dx", draw[:, :h_n], in_t, zw, du2)
    (dx1,) = _mm_fwd("in_proj_dx", dzx, in_t, "plain", [F32], _plain_add_epilogue, (dx_dt,), after=toks, k_rows=zw)
    toks = to_chips((dx1,))

    du0, _, dg_0, db_0, toks = _tail_bwd("l0", dx1, saved0, wts[0], ln_g_full[0], p_l[0], emit, to_chips, toks)
    dh, dpool, d_scale = _pool_bwd_mm("pool_bwd_mm", du0, hraw, pool_full, pool_scale)
    toks += emit("pool", dict(pool=_to_slots(_pool_dw("pool_dw", pooled, dh), 1)))
    grad_x = _pool_windows("pool_bwd", dpool, True, du0, after=tuple(toks))
    toks = to_chips((grad_x,))

    def update(tag, w, m, v, parts, layer, prev=None):
        own, recv = parts
        return _adamw(f"adamw_{tag}_{layer}", _as3d(w), _as3d(m), _as3d(v),
                      [(own, None), (recv, 0), (recv, 1), (recv, 2)], layer, prev, chip)

    q = collect("l1_ple", (grad_x, *toks))
    r_gate = update("ple_gate_w", ple_gate_w, m_ple_gate_w, v_ple_gate_w, q["gate"], 1)
    r_plew = update("ple_w", ple_w, m_ple_w, v_ple_w, q["plew"], 1)
    r_w2 = update("mlp_w2", mlp_w2, m_mlp_w2, v_mlp_w2, collect("l1_w2", (r_gate[0],))["w2"], 1)
    r_w1 = update("mlp_w1", mlp_w1, m_mlp_w1, v_mlp_w1, collect("l1_w1", (r_w2[0],))["w1"], 1)
    r_out = update("ssm_out_w", ssm_out_w, m_ssm_out_w, v_ssm_out_w, collect("ssm_out", (r_w1[0],))["out"], 0)
    r_in = update("ssm_in_w", ssm_in_w, m_ssm_in_w, v_ssm_in_w, collect("ssm_in", (r_out[0],))["in"], 0)

    d_ln_g = jnp.stack([jnp.stack(dg_0), jnp.stack(dg_1)]).reshape(2, 2, d)
    d_ln_b = jnp.stack([jnp.stack(db_0), jnp.stack(db_1)]).reshape(2, 2, d)
    partial_shapes = [(CONV_WIDTH, cd), (1, cd), (1, di), (2, 2, d), (2, 2, d), (1, d), (1, h_n), (1, h_n), (1, h_n),
                      (1, d)]
    partial = _pack([d_conv_w, d_conv_b, d_norm_w, d_ln_g, d_ln_b, d_scale, d_bias[:, :h_n], d_alog[:, :h_n],
                     dd.reshape(1, h_n), loss_cols], 8 * V7X_LANES)
    (all_partials,) = _all_gather("ag_small_grads", [partial], after=(r_in[0],))
    tot = _unpack(_sum8("sum_small_grads", all_partials), partial_shapes)
    t_conv_w, t_conv_b, t_norm_w, t_ln_g, t_ln_b, t_scale, t_bias, t_alog, t_dd, t_loss = tot
    loss = jnp.sum(t_loss)

    def mine(a, per):
        return lax.dynamic_slice_in_dim(a, me * per, per, axis=a.ndim - 1)

    small_names = ["ssm_conv_w", "ssm_conv_b", "ssm_norm_w", "ln_g", "ln_b", "pool_scale", "ssm_dt_bias", "ssm_a_log",
                   "ssm_d"]
    small_w = [ssm_conv_w, ssm_conv_b, ssm_norm_w, ln_g, ln_b, pool_scale, ssm_dt_bias, ssm_a_log, ssm_d]
    small_m = [m_ssm_conv_w, m_ssm_conv_b, m_ssm_norm_w, m_ln_g, m_ln_b, m_pool_scale, m_ssm_dt_bias, m_ssm_a_log,
               m_ssm_d]
    small_v = [v_ssm_conv_w, v_ssm_conv_b, v_ssm_norm_w, v_ln_g, v_ln_b, v_pool_scale, v_ssm_dt_bias, v_ssm_a_log,
               v_ssm_d]
    small_grads = [mine(t_conv_w, cd_s), mine(t_conv_b, cd_s), mine(t_norm_w, di_s), mine(t_ln_g, d_s),
                   mine(t_ln_b, d_s), t_scale, t_bias, t_alog, t_dd]
    shapes = [w.shape for w in small_w]
    pk = [_pack(group, 8 * V7X_LANES)[None] for group in (small_w, small_m, small_v, small_grads)]
    res = _adamw("adamw_small", pk[0], pk[1], pk[2], [(pk[3], 0)], 0)
    upd = {}
    for name, vals in zip(small_names, zip(*[_unpack(r, shapes) for r in res])):
        upd[name] = list(vals)

    q = collect("l0_ple", (res[0],))
    r_gate = update("ple_gate_w", ple_gate_w, m_ple_gate_w, v_ple_gate_w, q["gate"], 0, r_gate)
    r_plew = update("ple_w", ple_w, m_ple_w, v_ple_w, q["plew"], 0, r_plew)
    r_w2 = update("mlp_w2", mlp_w2, m_mlp_w2, v_mlp_w2, collect("l0_w2", (r_gate[0],))["w2"], 0, r_w2)
    r_w1 = update("mlp_w1", mlp_w1, m_mlp_w1, v_mlp_w1, collect("l0_w1", (r_w2[0],))["w1"], 0, r_w1)
    r_pool = update("pool_w", pool_w, m_pool_w, v_pool_w, collect("pool", (r_w1[0],))["pool"], 0)
    assert not scattering
    large = {"pool_w": (pool_w, r_pool), "ssm_in_w": (ssm_in_w, r_in), "ssm_out_w": (ssm_out_w, r_out),
             "mlp_w1": (mlp_w1, r_w1), "mlp_w2": (mlp_w2, r_w2), "ple_w": (ple_w, r_plew),
             "ple_gate_w": (ple_gate_w, r_gate)}
    for name, (w, rs) in large.items():
        upd[name] = [r.reshape(w.shape) for r in rs]

    order = ["pool_w", "pool_scale", "ssm_in_w", "ssm_conv_w", "ssm_conv_b", "ssm_dt_bias", "ssm_a_log", "ssm_d",
             "ssm_norm_w", "ssm_out_w", "mlp_w1", "mlp_w2", "ln_g", "ln_b", "ple_w", "ple_gate_w"]
    out = [loss, grad_x[None]]
    for k in range(4):
        out += [upd[name][k] for name in order]
    return tuple(out)
```

```python
import jax
import jax.numpy as jnp
from jax import lax
from jax.experimental import pallas as pl
from jax.experimental.pallas import tpu as pltpu

F32 = jnp.float32
BF16 = jnp.bfloat16
SDS = jax.ShapeDtypeStruct
MESH = pl.DeviceIdType.MESH
ANY = pl.BlockSpec(memory_space=pl.ANY)

N_DEV = 8
DEPTH = 2
ALPHA = (2.0 * DEPTH) ** 0.25
LN_EPS = 1e-5
RMS_EPS = 1e-5
POOL_WINDOW_LOG2 = (1, 2, 3, 4)
D_STATE = 128
CHUNK = 128
CONV_WIDTH = 4
ADAM_LR = 0.001
ADAM_B1 = 0.9
ADAM_B2 = 0.999
ADAM_EPS = 1e-08
ADAM_WD = 0.01
ADAM_STEP = 10

V7X_LANES = 128
V7X_VMEM_LIMIT = 48 * 1024 * 1024


def _cp(*sem):
    return pltpu.CompilerParams(dimension_semantics=sem, vmem_limit_bytes=V7X_VMEM_LIMIT)


def _pick(dim, cap):
    if dim <= cap:
        return dim
    best = None
    for t in range(V7X_LANES, cap + 1, V7X_LANES):
        if dim % t == 0:
            best = t
    assert best is not None, (dim, cap)
    return best


def _row_tile(rows, cols, itemsize=4, target=1 << 20):
    t = rows
    while t % 2 == 0 and t // 2 >= 16 and (t // 2) % 16 == 0 and t * cols * itemsize > target:
        t //= 2
    return t


def _all_gather(name, shards, after=()):
    n, na = len(shards), len(after)

    def body(*refs):
        ins, outs = refs[:n], refs[n + na:2 * n + na]
        send_sems, recv_sems, local_sems = refs[2 * n + na:]
        x, y, c = lax.axis_index("x"), lax.axis_index("y"), lax.axis_index("c")
        me, sibling = (x, y, c), (x, y, 1 - c)
        chips = [(1 - x, y), (x, 1 - y), (1 - x, 1 - y)]

        def copy(a, k, block, to, src=None):
            dst = outs[a].at[4 * block[0] + 2 * block[1] + block[2]]
            return pltpu.make_async_remote_copy(
                src_ref=dst if src is None else src, dst_ref=dst, send_sem=send_sems.at[a, k],
                recv_sem=recv_sems.at[a, k], device_id=to, device_id_type=MESH)

        mine = [pltpu.make_async_copy(ins[a], outs[a].at[4 * x + 2 * y + c], local_sems.at[a]) for a in range(n)]
        for cp in mine:
            cp.start()
        first = []
        for a in range(n):
            first.append(copy(a, 0, me, sibling, src=ins[a]))
            first += [copy(a, 1 + j, me, (*chip, c), src=ins[a]) for j, chip in enumerate(chips)]
        for cp in first:
            cp.start()
        passed = []
        for j, chip in enumerate(chips):
            for a in range(n):
                copy(a, 1 + j, (*chip, c), me).wait_recv()
                fwd = copy(a, 4 + j, (*chip, c), sibling)
                fwd.start()
                passed.append(fwd)
        for a in range(n):
            copy(a, 0, sibling, me).wait_recv()
            for j, chip in enumerate(chips):
                copy(a, 4 + j, (*chip, 1 - c), me).wait_recv()
        for cp in first + passed:
            cp.wait_send()
        for cp in mine:
            cp.wait()

    return pl.pallas_call(
        body, name=name,
        out_shape=[SDS((N_DEV,) + s.shape, s.dtype) for s in shards],
        in_specs=[ANY] * (n + na), out_specs=[ANY] * n,
        scratch_shapes=[pltpu.SemaphoreType.DMA((n, 7)), pltpu.SemaphoreType.DMA((n, 7)),
                        pltpu.SemaphoreType.DMA((n,))],
    )(*shards, *after)


HBM_SPEC = pl.BlockSpec(memory_space=pltpu.HBM)
SEM_SPEC = pl.BlockSpec(memory_space=pltpu.SEMAPHORE)
EFFECT = pltpu.SideEffectType.DATAFLOW_SIDE_EFFECTING


def _ag_first_copies(ins, lands, send_sems, recv_sems):
    x, y, c = lax.axis_index("x"), lax.axis_index("y"), lax.axis_index("c")
    targets = [(x, y, 1 - c), (1 - x, y, c), (x, 1 - y, c), (1 - x, 1 - y, c)]
    return [pltpu.make_async_remote_copy(
        src_ref=ins[a], dst_ref=lands[a].at[4 * x + 2 * y + c], send_sem=send_sems.at[4 * a + k],
        recv_sem=recv_sems.at[4 * a + k], device_id=to, device_id_type=MESH)
        for a in range(len(ins)) for k, to in enumerate(targets)]


def _ag_forward_copies(ins, lands, send_sems, recv_sems):
    x, y, c = lax.axis_index("x"), lax.axis_index("y"), lax.axis_index("c")
    cps = []
    for a in range(len(lands)):
        for j, (px, py) in enumerate([(1 - x, y), (x, 1 - y), (1 - x, 1 - y)]):
            blk = lands[a].at[4 * px + 2 * py + c]
            cps.append(pltpu.make_async_remote_copy(
                src_ref=blk, dst_ref=blk, send_sem=send_sems.at[3 * a + j], recv_sem=recv_sems.at[3 * a + j],
                device_id=(x, y, 1 - c), device_id_type=MESH))
    return cps


def _rs_sibling_copies(ins, lands, send_sems, recv_sems):
    x, y, c = lax.axis_index("x"), lax.axis_index("y"), lax.axis_index("c")
    return [pltpu.make_async_remote_copy(
        src_ref=ins[a].at[2 * q + 1 - c], dst_ref=lands[a].at[q], send_sem=send_sems.at[4 * a + q],
        recv_sem=recv_sems.at[4 * a + q], device_id=(x, y, 1 - c), device_id_type=MESH)
        for a in range(len(ins)) for q in range(4)]


def _rs_chip_copies(ins, lands, send_sems, recv_sems):
    x, y, c = lax.axis_index("x"), lax.axis_index("y"), lax.axis_index("c")
    cps = []
    for a in range(len(ins)):
        for j, (px, py) in enumerate([(1 - x, y), (x, 1 - y), (1 - x, 1 - y)]):
            cps.append(pltpu.make_async_remote_copy(
                src_ref=ins[a].at[2 * px + py], dst_ref=lands[a].at[j], send_sem=send_sems.at[3 * a + j],
                recv_sem=recv_sems.at[3 * a + j], device_id=(px, py, c), device_id_type=MESH))
    return cps


def _async_start(name, build, sem_shape, ins, lands, after=()):
    arrays = [*ins, *lands]
    n_i, n_t, n_a = len(ins), len(arrays), len(after)

    def body(*refs):
        outs = refs[n_t + n_a:]
        for cp in build(refs[:n_i], refs[n_i:n_t], outs[0], outs[1]):
            cp.start()
        outs[-1][...] = jnp.zeros_like(outs[-1])

    res = pl.pallas_call(
        body, name=name,
        out_shape=(pltpu.SemaphoreType.DMA(sem_shape), pltpu.SemaphoreType.DMA(sem_shape),
                   *[pltpu.HBM(a.shape, a.dtype) for a in arrays], SDS((8, V7X_LANES), F32)),
        in_specs=[HBM_SPEC] * n_t + [ANY] * n_a,
        out_specs=(SEM_SPEC, SEM_SPEC, *[HBM_SPEC] * n_t, pl.BlockSpec(memory_space=pltpu.VMEM)),
        input_output_aliases={i: 2 + i for i in range(n_t)},
        compiler_params=pltpu.CompilerParams(has_side_effects=EFFECT),
    )(*[pltpu.with_memory_space_constraint(a, pltpu.HBM) for a in arrays], *after)
    return res[0], res[1], list(res[2:2 + n_t]), res[-1]


def _async_wait(name, build, handle, n_i, after=()):
    send_sems, recv_sems, arrays, _ = handle
    n_t, n_a = len(arrays), len(after)

    def body(*refs):
        for cp in build(refs[:n_i], refs[n_i:n_t], refs[n_t], refs[n_t + 1]):
            cp.wait_send()
            cp.wait_recv()

    res = pl.pallas_call(
        body, name=name, out_shape=tuple(pltpu.HBM(a.shape, a.dtype) for a in arrays),
        in_specs=[HBM_SPEC] * n_t + [SEM_SPEC, SEM_SPEC] + [ANY] * n_a, out_specs=tuple([HBM_SPEC] * n_t),
        input_output_aliases={i: i for i in range(n_t)},
        compiler_params=pltpu.CompilerParams(has_side_effects=EFFECT),
    )(*arrays, send_sems, recv_sems, *after)
    return list(res[:n_i]), list(res[n_i:])


def _mm_core(name, a, b, *, grid, a_spec, b_spec, dims, acc_shape, outs, out_spec, epilogue=None, extras=(),
             extra_specs=(), a_fn=None, after=(), carry=None):
    nk = grid[2]
    if carry is not None:
        after = (*after, carry)
    ne, no, na = len(extras), len(outs), len(after)

    def body(a_ref, b_ref, *rest):
        e_refs, o_refs, acc = rest[:ne], rest[ne + na:ne + na + no], rest[ne + na + no]
        k = pl.program_id(2)

        def product():
            lhs = a_ref[...] if a_fn is None else a_fn(a_ref[...])
            return lax.dot_general(lhs.astype(BF16), b_ref[...].astype(BF16), dims, preferred_element_type=F32)

        @pl.when(k == 0)
        def _():
            acc[...] = product()

        @pl.when(k > 0)
        def _():
            acc[...] += product()

        @pl.when(k == nk - 1)
        def _():
            r = acc[...]
            vals = epilogue(r, *[e[...] for e in e_refs]) if epilogue is not None else (r,)
            for o, v in zip(o_refs, vals):
                o[...] = v.astype(o.dtype)

    res = pl.pallas_call(
        body, name=name, grid=grid, out_shape=list(outs),
        in_specs=[a_spec, b_spec, *extra_specs, *[ANY] * na], out_specs=[out_spec] * no,
        scratch_shapes=[pltpu.VMEM(acc_shape, F32)],
        input_output_aliases={} if carry is None else {1 + ne + na: 0},
        compiler_params=_cp("parallel", "parallel", "arbitrary"),
    )(a, b, *extras, *after)
    return res


NN = (((1,), (0,)), ((), ()))
NT = (((1,), (1,)), ((), ()))
TN = (((0,), (0,)), ((), ()))


def _w_dims(w, kind):
    if kind == "col":
        return w.shape[1], N_DEV * w.shape[2], w.shape[1], w.shape[2]
    if kind == "row":
        return N_DEV * w.shape[1], w.shape[2], w.shape[1], w.shape[2]
    return w.shape[0], w.shape[1], w.shape[0], w.shape[1]


MM_VMEM_BUDGET = 36 * 1024 * 1024


def _row_and_k_blocks(m, tn, k_len, k_caps, a, out_dtypes, extras):
    per_out = sum(jnp.dtype(dt).itemsize for dt in out_dtypes) + sum(e.dtype.itemsize for e in extras)
    best = None
    for tm in (_pick(m, 2048), _pick(m, 1024), _pick(m, 512)):
        for tk in [_pick(k_len, cap) for cap in k_caps]:
            used = tm * tn * (4 + 2 * per_out) + 2 * (tm * tk * a.dtype.itemsize + tk * tn * 2)
            key = ((m // tm) * (k_len // tk), -tm)
            if used <= MM_VMEM_BUDGET and (best is None or key < best[0]):
                best = (key, tm, tk)
    assert best is not None, (m, tn, k_len)
    return best[1], best[2]


def _mm_fwd(name, a, w, kind, out_dtypes, epilogue=None, extras=(), a_fn=None, after=(), k_rows=None):
    if kind == "row":
        w, kind = w.reshape(-1, w.shape[-1]), "plain"
    m = a.shape[0]
    kk, n, ks, ns = _w_dims(w, kind)
    if k_rows is None:
        assert kk == a.shape[1]
    else:
        assert kind == "plain" and k_rows <= min(kk, a.shape[1])
        kk = k_rows
    tn = _pick(ns, 1024) if kind == "col" else _pick(n, 1152)
    tm, tk = _row_and_k_blocks(m, tn, kk, (1024, 512), a, out_dtypes, extras)
    if kind == "col":
        nb = ns // tn
        b_spec = pl.BlockSpec((None, tk, tn), lambda i, j, k: (j // nb, k, j % nb))
    else:
        b_spec = pl.BlockSpec((tk, tn), lambda i, j, k: (k, j))
    mn_spec = pl.BlockSpec((tm, tn), lambda i, j, k: (i, j))
    return _mm_core(
        name, a, w, grid=(m // tm, n // tn, kk // tk),
        a_spec=pl.BlockSpec((tm, tk), lambda i, j, k: (i, k)), b_spec=b_spec, dims=NN, acc_shape=(tm, tn),
        outs=[SDS((m, n), dt) for dt in out_dtypes], out_spec=mn_spec, epilogue=epilogue, extras=extras,
        extra_specs=[mn_spec] * len(extras), a_fn=a_fn, after=after)


def _mm_dx(name, dy, w, kind, out_dtypes, epilogue=None, extras=(), after=(), k_rows=None):
    if kind == "row":
        w, kind = w.reshape(-1, w.shape[-1]), "plain"
    m, n_dim = dy.shape
    kk, n, ks, ns = _w_dims(w, kind)
    assert n == n_dim
    if k_rows is not None:
        assert kind == "plain" and k_rows <= kk
        kk = k_rows
    tn = _pick(kk, 1024)
    if kind == "col":
        tm, tk = _row_and_k_blocks(m, tn, ns, (1024, 512), dy, out_dtypes, extras)
        kb = ns // tk
        b_spec = pl.BlockSpec((None, tn, tk), lambda i, j, k: (k // kb, j, k % kb))
    else:
        tm, tk = _row_and_k_blocks(m, tn, n, (1152, 512), dy, out_dtypes, extras)
        b_spec = pl.BlockSpec((tn, tk), lambda i, j, k: (j, k))
    mk_spec = pl.BlockSpec((tm, tn), lambda i, j, k: (i, j))
    return _mm_core(
        name, dy, w, grid=(m // tm, kk // tn, n // tk),
        a_spec=pl.BlockSpec((tm, tk), lambda i, j, k: (i, k)), b_spec=b_spec, dims=NT, acc_shape=(tm, tn),
        outs=[SDS((m, kk), dt) for dt in out_dtypes], out_spec=mk_spec, epilogue=epilogue, extras=extras,
        extra_specs=[mk_spec] * len(extras), after=after)


def _in_proj_dt(name, a, w_t, first, h_n):
    m, kk = a.shape
    tm, tk = _pick(m, 1024), _pick(kk, 1024)
    blk = first // h_n
    return _mm_core(
        name, a, w_t, grid=(m // tm, 1, kk // tk), a_spec=pl.BlockSpec((tm, tk), lambda i, j, k: (i, k)),
        b_spec=pl.BlockSpec((h_n, tk), lambda i, j, k: (blk, k)), dims=NT, acc_shape=(tm, h_n),
        outs=[SDS((m, h_n), F32)], out_spec=pl.BlockSpec((tm, h_n), lambda i, j, k: (i, 0)))[0]


def _in_proj_dt_dx(name, d_dt, w_t, first, res):
    m, h_n = d_dt.shape
    n = w_t.shape[1]
    tm, tn = _pick(m, 1024), _pick(n, 1024)
    blk = first // h_n
    mn_spec = pl.BlockSpec((tm, tn), lambda i, j, k: (i, j))
    return _mm_core(
        name, d_dt, w_t, grid=(m // tm, n // tn, 1), a_spec=pl.BlockSpec((tm, h_n), lambda i, j, k: (i, 0)),
        b_spec=pl.BlockSpec((h_n, tn), lambda i, j, k: (blk, j)), dims=NN, acc_shape=(tm, tn),
        outs=[SDS((m, n), F32)], out_spec=mn_spec, epilogue=_residual_epilogue, extras=(res,),
        extra_specs=[mn_spec])[0]


def _in_proj_dw_t(name, dz, d_dt, x, after=()):
    m, n = x.shape
    zw, h_n = dz.shape[1], d_dt.shape[1]
    tm, tn, tk = _pick(zw, 1024), _pick(n, 1024), _pick(m, 1024)
    out = SDS((zw + h_n, n), BF16)
    x_spec = pl.BlockSpec((tk, tn), lambda i, j, k: (k, j))
    main = _mm_core(
        name, dz, x, grid=(zw // tm, n // tn, m // tk), a_spec=pl.BlockSpec((tk, tm), lambda i, j, k: (k, i)),
        b_spec=x_spec, dims=TN, acc_shape=(tm, tn), outs=[out], out_spec=pl.BlockSpec((tm, tn), lambda i, j, k: (i, j)),
        after=after)[0]
    blk = zw // h_n
    return _mm_core(
        name + "_dt", d_dt, x, grid=(1, n // tn, m // tk), a_spec=pl.BlockSpec((tk, h_n), lambda i, j, k: (k, 0)),
        b_spec=x_spec, dims=TN, acc_shape=(h_n, tn), outs=[out],
        out_spec=pl.BlockSpec((h_n, tn), lambda i, j, k: (blk, j)), carry=main)[0]


def _mm_dw(name, a, dy, kind, a_fn=None, after=()):
    m, kk = a.shape
    n = dy.shape[1]
    tk = _pick(m, 1024)
    if kind == "col":
        ns = n // N_DEV
        tm, tn = _pick(kk, 1024), _pick(ns, 1024)
        nb = ns // tn
        out = SDS((N_DEV, kk, ns), BF16)
        out_spec = pl.BlockSpec((None, tm, tn), lambda i, j, k: (j // nb, i, j % nb))
    else:
        tm, tn = _pick(kk, 1024), _pick(n, 1152)
        out = SDS((kk, n), BF16)
        out_spec = pl.BlockSpec((tm, tn), lambda i, j, k: (i, j))
    res = _mm_core(
        name, a, dy, grid=(kk // tm, n // tn, m // tk),
        a_spec=pl.BlockSpec((tk, tm), lambda i, j, k: (k, i)), b_spec=pl.BlockSpec((tk, tn), lambda i, j, k: (k, j)),
        dims=TN, acc_shape=(tm, tn), outs=[out], out_spec=out_spec, a_fn=a_fn, after=after)[0]
    return res.reshape(N_DEV, kk // N_DEV, n) if kind == "row" else res


def _rowwise(name, fn, ins, outs, rows, tile):
    arrays, specs = [], []
    for arr, kind in ins:
        arrays.append(arr)
        if kind == "row":
            specs.append(pl.BlockSpec((tile, arr.shape[1]), lambda i: (i, 0)))
        elif kind == "vec":
            specs.append(pl.BlockSpec(arr.shape, lambda i, nd=arr.ndim: (0,) * nd))
        else:
            specs.append(kind)
    out_shapes, out_specs, kinds = [], [], []
    for cols, dt, kind in outs:
        kinds.append(kind)
        if kind == "row":
            out_shapes.append(SDS((rows, cols), dt))
            out_specs.append(pl.BlockSpec((tile, cols), lambda i: (i, 0)))
        else:
            out_shapes.append(SDS((1, cols), F32))
            out_specs.append(pl.BlockSpec((1, cols), lambda i: (0, 0)))
    ni = len(arrays)
    has_acc = "acc" in kinds

    def body(*refs):
        vals = fn(*[r[...] for r in refs[:ni]])
        i = pl.program_id(0)
        for o, v, kind in zip(refs[ni:], vals, kinds):
            if kind == "row":
                o[...] = v.astype(o.dtype)
            else:
                @pl.when(i == 0)
                def _(o=o):
                    o[...] = jnp.zeros_like(o)

                o[...] += v

    return pl.pallas_call(
        body, name=name, grid=(rows // tile,), out_shape=out_shapes, in_specs=specs, out_specs=out_specs,
        compiler_params=_cp("arbitrary" if has_acc else "parallel"),
    )(*arrays)


def _ln_fwd(name, u, g, b):
    d = u.shape[1]

    def fn(u, g, b):
        mu = jnp.mean(u, axis=1, keepdims=True)
        xc = u - mu
        var = jnp.mean(xc * xc, axis=1, keepdims=True)
        y = xc * lax.rsqrt(var + LN_EPS) * g + b
        return y, y

    return _rowwise(name, fn, [(u, "row"), (g, "vec"), (b, "vec")], [(d, F32, "row"), (d, BF16, "row")], u.shape[0], 256)


def _ln_bwd(name, u, dy, g):
    d = u.shape[1]

    def fn(u, dy, g):
        mu = jnp.mean(u, axis=1, keepdims=True)
        xc = u - mu
        var = jnp.mean(xc * xc, axis=1, keepdims=True)
        rstd = lax.rsqrt(var + LN_EPS)
        xhat = xc * rstd
        dxhat = dy * g
        m1 = jnp.mean(dxhat, axis=1, keepdims=True)
        m2 = jnp.mean(dxhat * xhat, axis=1, keepdims=True)
        du = rstd * (dxhat - m1 - xhat * m2)
        return du, du, jnp.sum(dy * xhat, axis=0, keepdims=True), jnp.sum(dy, axis=0, keepdims=True)

    return _rowwise(name, fn, [(u, "row"), (dy, "row"), (g, "vec")],
                    [(d, F32, "row"), (d, BF16, "row"), (d, F32, "acc"), (d, F32, "acc")], u.shape[0], 256)


def _loss_bwd(name, y, target):
    d = y.shape[1]

    def fn(y, t):
        e = y - t
        return e * (1.0 / d), jnp.sum(e * e, axis=0, keepdims=True) * (0.5 / d)

    return _rowwise(name, fn, [(y, "row"), (target, "row")], [(d, F32, "row"), (d, F32, "acc")], y.shape[0], 256)


def _ple_bwd(name, dx, e, gate):
    d = dx.shape[1]

    def fn(dx, e, gate):
        return dx * e * gate * (1.0 - gate), dx * gate

    return _rowwise(name, fn, [(dx, "row"), (e, "row"), (gate, "row")], [(d, BF16, "row"), (d, BF16, "row")],
                    dx.shape[0], 256)


def _sigmoid(v):
    return 1.0 / (1.0 + jnp.exp(-v))


def _gated_rms_fwd(name, y, zx, norm_w):
    di = y.shape[1]

    def fn(y, z, w):
        yg = y * (z * _sigmoid(z))
        r = lax.rsqrt(jnp.mean(yg * yg, axis=1, keepdims=True) + RMS_EPS)
        return (yg * r * w,)

    z_spec = pl.BlockSpec((128, di), lambda i: (i, 0))
    return _rowwise(name, fn, [(y, "row"), (zx, z_spec), (norm_w, "vec")], [(di, BF16, "row")], y.shape[0], 128)[0]


def _gated_rms_bwd(name, y, zx, norm_w, dout):
    di = y.shape[1]

    def fn(y, z, w, dout):
        sg = _sigmoid(z)
        sz = z * sg
        yg = y * sz
        r = lax.rsqrt(jnp.mean(yg * yg, axis=1, keepdims=True) + RMS_EPS)
        dn = dout * w
        dyg = r * (dn - yg * (r * r) * jnp.mean(dn * yg, axis=1, keepdims=True))
        dy = dyg * sz
        dz = dyg * y * (sg * (1.0 + z * (1.0 - sg)))
        return dy, dz, jnp.sum(dout * yg * r, axis=0, keepdims=True)

    z_spec = pl.BlockSpec((128, di), lambda i: (i, 0))
    return _rowwise(name, fn, [(y, "row"), (zx, z_spec), (norm_w, "vec"), (dout, "row")],
                    [(di, F32, "row"), (di, BF16, "row"), (di, F32, "acc")], y.shape[0], 128)


def _shift_down(v, j, row):
    return jnp.where(row >= j, pltpu.roll(v, j, 0), 0.0)


def _shift_up(v, j, row):
    t = v.shape[0]
    return jnp.where(row < t - j, pltpu.roll(v, t - j, 0), 0.0)


def _pool_select(parts, g):
    return jnp.where(g == 0, parts[0], jnp.where(g == 1, parts[1], jnp.where(g == 2, parts[2], parts[3])))


def _pool_windows(name, x, transpose, scale_by=None, after=()):
    t, d = x.shape
    cg = d // 4
    cw = V7X_LANES
    per = cg // cw

    def body(*refs):
        x_ref, o_ref = refs[0], refs[-1]
        g = pl.program_id(0) // per
        xv = x_ref[...]
        row = lax.broadcasted_iota(jnp.int32, (t, 1), 0)
        cnt = jnp.minimum(row + 1, jnp.left_shift(2, g)).astype(F32)
        s = xv / cnt if transpose else xv
        parts = []
        for lg in POOL_WINDOW_LOG2:
            j = 1 << (lg - 1)
            s = s + (_shift_up(s, j, row) if transpose else _shift_down(s, j, row))
            parts.append(s)
        sel = _pool_select(parts, g)
        if transpose:
            o_ref[...] = ALPHA * refs[1][...] + sel - xv
        else:
            o_ref[...] = (sel / cnt - xv).astype(o_ref.dtype)

    col = pl.BlockSpec((t, cw), lambda j: (0, j))
    ins = [x] if scale_by is None else [x, scale_by]
    return pl.pallas_call(
        body, name=name, grid=(d // cw,), out_shape=SDS((t, d), F32 if transpose else BF16),
        in_specs=[col] * len(ins) + [ANY] * len(after), out_specs=col, compiler_params=_cp("parallel"),
    )(*ins, *after)


def _pool_mm(name, pooled, w, scale, x):
    t, d = x.shape
    cg = d // 4
    tm = _pick(t, 1024)

    def body(p_ref, w_ref, s_ref, x_ref, u_ref, h_ref):
        h = jnp.dot(p_ref[...], w_ref[...], preferred_element_type=F32)
        h_ref[...] = h
        u_ref[...] = ALPHA * x_ref[...] + h * s_ref[...]

    blk = pl.BlockSpec((tm, cg), lambda g, i: (i, g))
    return pl.pallas_call(
        body, name=name, grid=(4, t // tm), out_shape=[SDS((t, d), F32), SDS((t, d), F32)],
        in_specs=[blk, pl.BlockSpec((None, cg, cg), lambda g, i: (g, 0, 0)), pl.BlockSpec((1, cg), lambda g, i: (0, g)),
                  blk],
        out_specs=[blk, blk], compiler_params=_cp("parallel", "parallel"),
    )(pooled, w, scale, x)


def _pool_bwd_mm(name, du, hraw, w, scale):
    t, d = du.shape
    cg = d // 4
    tm = _pick(t, 1024)

    def body(du_ref, h_ref, w_ref, s_ref, dh_ref, dp_ref, ds_ref):
        @pl.when(pl.program_id(1) == 0)
        def _():
            ds_ref[...] = jnp.zeros_like(ds_ref)

        duv = du_ref[...]
        ds_ref[...] += jnp.sum(duv * h_ref[...], axis=0, keepdims=True)
        dh = (duv * s_ref[...]).astype(BF16)
        dh_ref[...] = dh
        dp_ref[...] = lax.dot_general(dh, w_ref[...], NT, preferred_element_type=F32)

    blk = pl.BlockSpec((tm, cg), lambda g, i: (i, g))
    vec = pl.BlockSpec((1, cg), lambda g, i: (0, g))
    return pl.pallas_call(
        body, name=name, grid=(4, t // tm), out_shape=[SDS((t, d), BF16), SDS((t, d), F32), SDS((1, d), F32)],
        in_specs=[blk, blk, pl.BlockSpec((None, cg, cg), lambda g, i: (g, 0, 0)), vec],
        out_specs=[blk, blk, vec], compiler_params=_cp("parallel", "arbitrary"),
    )(du, hraw, w, scale)


def _pool_dw(name, pooled, dh):
    t, d = pooled.shape
    cg = d // 4
    tk = _pick(t, 512)
    nk = t // tk

    def body(p_ref, dh_ref, o_ref, acc):
        k = pl.program_id(1)

        @pl.when(k == 0)
        def _():
            acc[...] = jnp.zeros_like(acc)

        acc[...] += lax.dot_general(p_ref[...], dh_ref[...], TN, preferred_element_type=F32)

        @pl.when(k == nk - 1)
        def _():
            o_ref[...] = acc[...].astype(o_ref.dtype)

    blk = pl.BlockSpec((tk, cg), lambda g, k: (k, g))
    return pl.pallas_call(
        body, name=name, grid=(4, nk), out_shape=SDS((4, cg, cg), BF16), in_specs=[blk, blk],
        out_specs=pl.BlockSpec((None, cg, cg), lambda g, k: (g, 0, 0)), scratch_shapes=[pltpu.VMEM((cg, cg), F32)],
        compiler_params=_cp("parallel", "arbitrary"),
    )(pooled, dh)


def _conv_pre(u, w_ref, b_ref, row):
    pre = b_ref[...] + _shift_down(u, 3, row) * w_ref[0:1, :]
    pre = pre + _shift_down(u, 2, row) * w_ref[1:2, :]
    pre = pre + _shift_down(u, 1, row) * w_ref[2:3, :]
    return pre + u * w_ref[3:4, :]


def _conv_fwd(name, zx, conv_w, conv_b, di):
    t = zx.shape[0]
    cd = conv_w.shape[1]
    cw = _pick(cd, 256)
    off = di // cw

    def body(u_ref, w_ref, b_ref, o_ref):
        row = lax.broadcasted_iota(jnp.int32, (t, 1), 0)
        pre = _conv_pre(u_ref[...], w_ref, b_ref, row)
        o_ref[...] = pre * _sigmoid(pre)

    return pl.pallas_call(
        body, name=name, grid=(cd // cw,), out_shape=SDS((t, cd), F32),
        in_specs=[pl.BlockSpec((t, cw), lambda j: (0, off + j)), pl.BlockSpec((CONV_WIDTH, cw), lambda j: (0, j)),
                  pl.BlockSpec((1, cw), lambda j: (0, j))],
        out_specs=pl.BlockSpec((t, cw), lambda j: (0, j)), compiler_params=_cp("parallel"),
    )(zx, conv_w, conv_b)


def _conv_bwd(name, zx, conv_w, conv_b, dact, di, first):
    t, cd = dact.shape
    cw = _pick(cd, 256)
    off, woff = (di + first) // cw, first // cw

    def body(u_ref, w_ref, b_ref, da_ref, du_ref, dw_ref, db_ref):
        row = lax.broadcasted_iota(jnp.int32, (t, 1), 0)
        u = u_ref[...]
        pre = _conv_pre(u, w_ref, b_ref, row)
        sg = _sigmoid(pre)
        dpre = da_ref[...] * (sg * (1.0 + pre * (1.0 - sg)))
        du = dpre * w_ref[3:4, :]
        for j in (1, 2, 3):
            du = du + _shift_up(dpre, j, row) * w_ref[3 - j:4 - j, :]
            dw_ref[3 - j:4 - j, :] = jnp.sum(dpre * _shift_down(u, j, row), axis=0, keepdims=True)
        dw_ref[3:4, :] = jnp.sum(dpre * u, axis=0, keepdims=True)
        db_ref[...] = jnp.sum(dpre, axis=0, keepdims=True)
        du_ref[...] = du.astype(du_ref.dtype)

    wspec = pl.BlockSpec((CONV_WIDTH, cw), lambda j: (0, j))
    bspec = pl.BlockSpec((1, cw), lambda j: (0, j))
    ospec = pl.BlockSpec((t, cw), lambda j: (0, j))
    return pl.pallas_call(
        body, name=name, grid=(cd // cw,), out_shape=[SDS((t, cd), BF16), SDS((CONV_WIDTH, cd), F32), SDS((1, cd), F32)],
        in_specs=[pl.BlockSpec((t, cw), lambda j: (0, off + j)), pl.BlockSpec((CONV_WIDTH, cw), lambda j: (0, woff + j)),
                  pl.BlockSpec((1, cw), lambda j: (0, woff + j)), ospec],
        out_specs=[ospec, wspec, bspec], compiler_params=_cp("parallel"),
    )(zx, conv_w, conv_b, dact)


def _expand_heads(name, arrays, h_n, p):
    t = arrays[0].shape[0]
    n = len(arrays)
    w = _pick(h_n * p, 512)

    def body(*refs):
        j = pl.program_id(0)
        head = lax.broadcasted_iota(jnp.int32, (V7X_LANES, w), 0)
        lane = lax.broadcasted_iota(jnp.int32, (V7X_LANES, w), 1)
        spread = (head == j * (w // p) + lane // p).astype(BF16)
        for a_ref, o_ref in zip(refs[:n], refs[n:]):
            rest = a_ref[...]
            out = jnp.zeros((t, w), F32)
            for _ in range(3):
                piece = rest.astype(BF16)
                out = out + jnp.dot(piece, spread, preferred_element_type=F32)
                rest = rest - piece.astype(F32)
            o_ref[...] = out

    full = pl.BlockSpec((t, V7X_LANES), lambda j: (0, 0))
    return pl.pallas_call(
        body, name=name, grid=(h_n * p // w,), out_shape=[SDS((t, h_n * p), F32)] * n, in_specs=[full] * n,
        out_specs=[pl.BlockSpec((t, w), lambda j: (0, j))] * n, compiler_params=_cp("parallel"),
    )(*arrays)


def _softplus(v):
    return jnp.maximum(v, 0.0) + jnp.log(1.0 + jnp.exp(-jnp.abs(v)))


def _dt_fwd(name, zx, bias, a_log, col_block):
    t = zx.shape[0]

    def body(r_ref, b_ref, al_ref, dt_ref, acs_ref, ein_ref, eout_ref):
        row = lax.broadcasted_iota(jnp.int32, (t, 1), 0) % CHUNK
        dt = _softplus(r_ref[...] + b_ref[...])
        da = dt * (-jnp.exp(al_ref[...]))
        s, r = da, da
        j = 1
        while j < CHUNK:
            s = s + jnp.where(row >= j, pltpu.roll(s, j, 0), 0.0)
            r = r + jnp.where(row < CHUNK - j, pltpu.roll(r, t - j, 0), 0.0)
            j *= 2
        dt_ref[...] = dt
        acs_ref[...] = s
        ein_ref[...] = jnp.exp(s)
        eout_ref[...] = jnp.exp(r - da)

    vec = pl.BlockSpec((1, V7X_LANES), lambda i: (0, 0))
    full = pl.BlockSpec((t, V7X_LANES), lambda i: (0, 0))
    return pl.pallas_call(
        body, name=name, grid=(1,), out_shape=[SDS((t, V7X_LANES), F32)] * 4,
        in_specs=[pl.BlockSpec((t, V7X_LANES), lambda i: (0, col_block)), vec, vec], out_specs=[full] * 4,
        compiler_params=_cp("arbitrary"),
    )(zx, bias, a_log)


def _dt_bwd(name, zx, bias, a_log, d_acs, d_dt, col_block):
    t = zx.shape[0]

    def body(r_ref, b_ref, al_ref, da_ref, dd_ref, draw_ref, db_ref, dal_ref):
        row = lax.broadcasted_iota(jnp.int32, (t, 1), 0) % CHUNK
        pre = r_ref[...] + b_ref[...]
        dt = _softplus(pre)
        a = -jnp.exp(al_ref[...])
        s = da_ref[...]
        j = 1
        while j < CHUNK:
            s = s + jnp.where(row < CHUNK - j, pltpu.roll(s, t - j, 0), 0.0)
            j *= 2
        ddt = dd_ref[...] + s * a
        dal_ref[...] = jnp.sum(s * dt, axis=0, keepdims=True) * a
        draw = ddt * _sigmoid(pre)
        db_ref[...] = jnp.sum(draw, axis=0, keepdims=True)
        draw_ref[...] = draw.astype(draw_ref.dtype)

    vec = pl.BlockSpec((1, V7X_LANES), lambda i: (0, 0))
    full = pl.BlockSpec((t, V7X_LANES), lambda i: (0, 0))
    return pl.pallas_call(
        body, name=name, grid=(1,), out_shape=[SDS((t, V7X_LANES), BF16), SDS((1, V7X_LANES), F32), SDS((1, V7X_LANES), F32)],
        in_specs=[pl.BlockSpec((t, V7X_LANES), lambda i: (0, col_block)), vec, vec, full, full],
        out_specs=[full, vec, vec], compiler_params=_cp("arbitrary"),
    )(zx, bias, a_log, d_acs, d_dt)


def _ssd_specs(t, di, g_n, hpg, p, rev):
    nc = t // CHUNK
    w = hpg * p
    nb = di // D_STATE

    def cc(c):
        return nc - 1 - c if rev else c

    return dict(
        xs=pl.BlockSpec((CHUNK, w), lambda g, c: (cc(c), g)),
        bm=pl.BlockSpec((CHUNK, D_STATE), lambda g, c: (cc(c), nb + g)),
        cm=pl.BlockSpec((CHUNK, D_STATE), lambda g, c: (cc(c), nb + g_n + g)),
        col=pl.BlockSpec((None, CHUNK, hpg), lambda g, c: (g, cc(c), 0)),
        rowv=pl.BlockSpec((None, hpg, CHUNK), lambda g, c: (g, 0, cc(c))),
        head=pl.BlockSpec((None, 1, hpg), lambda g, c: (g, 0, 0)),
        lanes=pl.BlockSpec((1, w), lambda g, c: (0, g)),
        bc=pl.BlockSpec((CHUNK, D_STATE), lambda g, c: (cc(c), g)),
        prev=pl.BlockSpec((None, None, D_STATE, w), lambda g, c: (cc(c), g, 0, 0)),
        seg=pl.BlockSpec((w, V7X_LANES), lambda g, c: (0, 0)),
    )


def _decay_masks(cb, ac_ref, ar_ref, heads):
    li = lax.broadcasted_iota(jnp.int32, (CHUNK, CHUNK), 0)
    si = lax.broadcasted_iota(jnp.int32, (CHUNK, CHUNK), 1)
    lms = [jnp.exp(jnp.where(li >= si, ac_ref[:, hh:hh + 1] - ar_ref[hh:hh + 1, :], -jnp.inf)) for hh in heads]
    return lms, [(cb * lm).astype(BF16) for lm in lms]


def _ssd_fwd(name, xbc, dt_x, ein_x, eout_x, a_col, a_row, d_x, di, g_n, hpg, p):
    t = xbc.shape[0]
    nc = t // CHUNK
    w = hpg * p
    assert 2 * p == V7X_LANES and hpg % 2 == 0
    sp = _ssd_specs(t, di, g_n, hpg, p, False)

    def body(xs_ref, bm_ref, cm_ref, dt_ref, ein_ref, eout_ref, ac_ref, ar_ref, d_ref, y_ref, prev_ref, h_ref):
        @pl.when(pl.program_id(1) == 0)
        def _():
            h_ref[...] = jnp.zeros_like(h_ref)

        bm = bm_ref[...].astype(BF16)
        cm = cm_ref[...].astype(BF16)
        cb = lax.dot_general(cm, bm, NT, preferred_element_type=F32)
        first = lax.broadcasted_iota(jnp.int32, (1, V7X_LANES), 1) < p
        xs = xs_ref[...]
        e_in = ein_ref[...]
        xdt = xs * dt_ref[...]
        ys = []
        for pr in range(hpg // 2):
            _, ms = _decay_masks(cb, ac_ref, ar_ref, (2 * pr, 2 * pr + 1))
            xp = xdt[:, pr * V7X_LANES:(pr + 1) * V7X_LANES]
            rhs = jnp.concatenate([jnp.where(first, xp, 0.0), jnp.where(first, 0.0, xp)], axis=0).astype(BF16)
            ys.append(jnp.dot(jnp.concatenate(ms, axis=1), rhs, preferred_element_type=F32))
        h_prev = h_ref[...]
        prev_ref[...] = h_prev
        y = jnp.concatenate(ys, axis=1) + jnp.dot(cm, h_prev.astype(BF16), preferred_element_type=F32) * e_in
        y_ref[...] = y + xs * d_ref[...]
        st = lax.dot_general(bm, (xdt * eout_ref[...]).astype(BF16), TN, preferred_element_type=F32)
        h_ref[...] = e_in[CHUNK - 1:CHUNK, :] * h_prev + st

    return pl.pallas_call(
        body, name=name, grid=(g_n, nc),
        out_shape=[SDS((t, di), F32), SDS((nc, g_n, D_STATE, w), F32)],
        in_specs=[sp["xs"], sp["bm"], sp["cm"], sp["xs"], sp["xs"], sp["xs"], sp["col"], sp["rowv"], sp["lanes"]],
        out_specs=[sp["xs"], sp["prev"]], scratch_shapes=[pltpu.VMEM((D_STATE, w), F32)],
        compiler_params=_cp("parallel", "arbitrary"),
    )(xbc, xbc, xbc, dt_x, ein_x, eout_x, a_col, a_row, d_x)


def _head_sums(v, seg):
    hi = v.astype(BF16)
    lo = (v - hi.astype(F32)).astype(BF16)
    return jnp.dot(hi, seg, preferred_element_type=F32) + jnp.dot(lo, seg, preferred_element_type=F32)


def _head_totals(v, seg):
    part = v[0:8]
    for r in range(8, v.shape[0], 8):
        part = part + v[r:r + 8]
    return jnp.sum(_head_sums(part, seg), axis=0, keepdims=True)


def _ssd_bwd(name, xbc, dt_x, ein_x, eout_x, a_col, a_row, d_x, prev, dy, di, g_n, hpg, p):
    t = xbc.shape[0]
    nc = t // CHUNK
    w = hpg * p
    sp = _ssd_specs(t, di, g_n, hpg, p, True)
    seg = (lax.broadcasted_iota(jnp.int32, (w, V7X_LANES), 0) // p
           == lax.broadcasted_iota(jnp.int32, (w, V7X_LANES), 1)).astype(BF16)

    def body(xs_ref, bm_ref, cm_ref, dt_ref, ein_ref, eout_ref, ac_ref, ar_ref, d_ref, prev_ref, dy_ref,
             seg_ref, dx_ref, dbm_ref, dcm_ref, ddt_ref, dacs_ref, dd_ref, dh_ref):
        @pl.when(pl.program_id(1) == 0)
        def _():
            dh_ref[...] = jnp.zeros_like(dh_ref)
            dd_ref[...] = jnp.zeros_like(dd_ref)

        bm = bm_ref[...].astype(BF16)
        cm = cm_ref[...].astype(BF16)
        cb = lax.dot_general(cm, bm, NT, preferred_element_type=F32)
        first = lax.broadcasted_iota(jnp.int32, (1, V7X_LANES), 1) < p
        last_row = lax.broadcasted_iota(jnp.int32, (CHUNK, 1), 0) == CHUNK - 1
        seg_m = seg_ref[...]
        xs, dy, e_in, e_out, d_skip = xs_ref[...], dy_ref[...], ein_ref[...], eout_ref[...], d_ref[...]
        dt_l = dt_ref[...]
        xdt = xs * dt_l
        h_prev = prev_ref[...]
        h_prev_b = h_prev.astype(BF16)
        dh_next = dh_ref[...]
        dh_next_b = dh_next.astype(BF16)
        dy_e = (dy * e_in).astype(BF16)
        d_cm = lax.dot_general(dy_e, h_prev_b, NT, preferred_element_type=F32)
        dh_ref[...] = e_in[CHUNK - 1:CHUNK, :] * dh_next + lax.dot_general(cm, dy_e, TN, preferred_element_type=F32)
        q = jnp.dot(bm, dh_next_b, preferred_element_type=F32)
        xf = xdt * e_out
        d_bm = lax.dot_general(xf.astype(BF16), dh_next_b, NT, preferred_element_type=F32)
        d_cb = jnp.zeros((CHUNK, CHUNK), F32)
        parts, w_parts = [], []
        for pr in range(hpg // 2):
            lanes = slice(pr * V7X_LANES, (pr + 1) * V7X_LANES)
            lms, ms = _decay_masks(cb, ac_ref, ar_ref, (2 * pr, 2 * pr + 1))
            xp = xdt[:, lanes]
            xp_b = xp.astype(BF16)
            dyp = dy[:, lanes]
            halves = [jnp.where(first, dyp, 0.0).astype(BF16), jnp.where(first, 0.0, dyp).astype(BF16)]
            for lm, half in zip(lms, halves):
                d_cb = d_cb + lax.dot_general(half, xp_b, NT, preferred_element_type=F32) * lm
            dxd = lax.dot_general(jnp.concatenate(ms, axis=0), jnp.concatenate(halves, axis=0), TN,
                                  preferred_element_type=F32)
            stacked = jnp.concatenate([jnp.where(first, xp, 0.0), jnp.where(first, 0.0, xp)], axis=0).astype(BF16)
            y_diag = jnp.dot(jnp.concatenate(ms, axis=1), stacked, preferred_element_type=F32)
            parts.append(dxd)
            w_parts.append(dyp.astype(BF16).astype(F32) * y_diag - xp_b.astype(F32) * dxd)
        d_xdt = jnp.concatenate(parts, axis=1) + q * e_out
        dx_ref[...] = d_xdt * dt_l + dy * d_skip
        ch = jnp.dot(cm, h_prev_b, preferred_element_type=F32)
        qx = q * xf
        s_a = _head_sums(dy * ch * e_in - qx + jnp.concatenate(w_parts, axis=1), seg_m)[:, :hpg]
        d_last = (_head_totals(qx, seg_m)[:, :hpg]
                  + jnp.exp(ac_ref[CHUNK - 1:CHUNK, :]) * _head_totals(dh_next * h_prev, seg_m)[:, :hpg])
        ddt_ref[...] = _head_sums(d_xdt * xs, seg_m)[:, :hpg]
        dacs_ref[...] = s_a + jnp.where(last_row, d_last, 0.0)
        dd_ref[...] += _head_totals(dy * xs, seg_m)[:, :hpg]
        d_cb_b = d_cb.astype(BF16)
        dcm_ref[...] = d_cm + jnp.dot(d_cb_b, bm, preferred_element_type=F32)
        dbm_ref[...] = d_bm + lax.dot_general(d_cb_b, cm, TN, preferred_element_type=F32)

    gn = g_n * D_STATE
    return pl.pallas_call(
        body, name=name, grid=(g_n, nc),
        out_shape=[SDS((t, di), F32), SDS((t, gn), F32), SDS((t, gn), F32), SDS((g_n, t, hpg), F32),
                   SDS((g_n, t, hpg), F32), SDS((g_n, 1, hpg), F32)],
        in_specs=[sp["xs"], sp["bm"], sp["cm"], sp["xs"], sp["xs"], sp["xs"], sp["col"], sp["rowv"],
                  sp["lanes"], sp["prev"], sp["xs"], sp["seg"]],
        out_specs=[sp["xs"], sp["bc"], sp["bc"], sp["col"], sp["col"], sp["head"]],
        scratch_shapes=[pltpu.VMEM((D_STATE, w), F32)],
        compiler_params=_cp("parallel", "arbitrary"),
    )(xbc, xbc, xbc, dt_x, ein_x, eout_x, a_col, a_row, d_x, prev, dy, seg)


def _as3d(a):
    return a.reshape(a.shape[0], -1, a.shape[-1])


def _pair_sum(name, own, recv, core):
    shape = recv.shape
    cols = shape[-1]
    own3, recv3 = own.reshape(8, -1, cols), recv.reshape(4, -1, cols)
    rows = recv3.shape[1]
    tr = _row_tile(rows, cols, 2)

    def body(c_ref, a_ref, b_ref, o_ref):
        o_ref[...] = (a_ref[...].astype(F32) + b_ref[...].astype(F32)).astype(o_ref.dtype)

    blk = pl.BlockSpec((None, tr, cols), lambda q, i, c_ref: (q, i, 0))
    out = pl.pallas_call(
        body, name=name, out_shape=SDS(recv3.shape, recv.dtype),
        grid_spec=pltpu.PrefetchScalarGridSpec(
            num_scalar_prefetch=1, grid=(4, rows // tr),
            in_specs=[pl.BlockSpec((None, tr, cols), lambda q, i, c_ref: (2 * q + c_ref[0], i, 0)), blk], out_specs=blk),
        compiler_params=_cp("parallel", "parallel"),
    )(core, own3, recv3)
    return out.reshape(shape)


def _adamw(name, w, m, v, parts, layer, prev=None, sel=None):
    lyr, rows, cols = w.shape
    n = len(parts)
    by_rows = rows % 16 == 0
    tr, tc = (_row_tile(rows, cols), cols) if by_rows else (rows, _pick(cols, 256))
    np_ = 0 if prev is None else 4
    if sel is None:
        sel = jnp.zeros((1,), jnp.int32)

    def body(sel_ref, *refs):
        w_ref, m_ref, v_ref = refs[:3]
        p_refs = refs[3:3 + n]
        g_ref, d_ref, nm_ref, nv_ref = refs[3 + n + np_:]
        g = p_refs[0][...].astype(F32)
        for r in p_refs[1:]:
            g = g + r[...].astype(F32)
        nm = ADAM_B1 * m_ref[...] + (1.0 - ADAM_B1) * g
        nv = ADAM_B2 * v_ref[...] + (1.0 - ADAM_B2) * (g * g)
        m_hat = nm / (1.0 - ADAM_B1 ** ADAM_STEP)
        v_hat = nv / (1.0 - ADAM_B2 ** ADAM_STEP)
        g_ref[...] = g
        d_ref[...] = -ADAM_LR * (m_hat / (jnp.sqrt(v_hat) + ADAM_EPS) + ADAM_WD * w_ref[...])
        nm_ref[...] = nm
        nv_ref[...] = nv

    def at(lead):
        return pl.BlockSpec((None, tr, tc), lambda i, s: (lead(s), i, 0) if by_rows else (lead(s), 0, i))

    lspec = at(lambda s: layer)
    pspecs = [at(lambda s: s[0]) if q is None else at(lambda s, q=q: q) for _, q in parts]
    aliases = {} if prev is None else {4 + n + q: q for q in range(4)}
    return pl.pallas_call(
        body, name=name, out_shape=[SDS(w.shape, F32)] * 4,
        grid_spec=pltpu.PrefetchScalarGridSpec(
            num_scalar_prefetch=1, grid=(rows // tr if by_rows else cols // tc,),
            in_specs=[lspec] * 3 + pspecs + [ANY] * np_, out_specs=[lspec] * 4),
        input_output_aliases=aliases, compiler_params=_cp("parallel"),
    )(sel, w, m, v, *[arr for arr, _ in parts], *(prev or ()))


def _sum8(name, parts):
    rows = parts.shape[1]

    def body(p_ref, o_ref):
        s = p_ref[0]
        for q in range(1, N_DEV):
            s = s + p_ref[q]
        o_ref[...] = s

    return pl.pallas_call(
        body, name=name, grid=(1,), out_shape=SDS((rows, V7X_LANES), F32),
        in_specs=[pl.BlockSpec((N_DEV, rows, V7X_LANES), lambda i: (0, 0, 0))],
        out_specs=pl.BlockSpec((rows, V7X_LANES), lambda i: (0, 0)), compiler_params=_cp("arbitrary"),
    )(parts)


def _pack(vectors, align):
    flat = jnp.concatenate([v.reshape(-1) for v in vectors])
    pad = (-flat.shape[0]) % align
    if pad:
        flat = jnp.concatenate([flat, jnp.zeros((pad,), F32)])
    return flat.reshape(-1, V7X_LANES)


def _unpack(packed, shapes):
    flat = packed.reshape(-1)
    out, o = [], 0
    for s in shapes:
        size = 1
        for dim in s:
            size *= dim
        out.append(flat[o:o + size].reshape(s))
        o += size
    return out


def _residual_epilogue(acc, res):
    return (ALPHA * res + acc,)


def _plain_add_epilogue(acc, res):
    return (res + acc,)


def _gate_epilogue(acc, y, e):
    gate = _sigmoid(acc)
    xn = y + gate * e
    return xn, gate, xn


def _relu2(pre):
    r = jnp.maximum(pre, 0.0)
    return r * r


def _relu2_bwd_epilogue(acc, pre):
    return (acc * (2.0 * jnp.maximum(pre.astype(F32), 0.0)),)


def _tail_fwd(tag, u_a, wts, lng, lnb, p_l, finish_w2):
    y1, y1_b = _ln_fwd(f"ln1_{tag}", u_a, lng[0], lnb[0])
    (pre,) = _mm_fwd(f"mlp1_{tag}", y1_b, wts["w1"], "col", [BF16])
    finish_w2(pre)
    (u_b,) = _mm_fwd(f"mlp2_{tag}", pre, wts["w2"], "row", [F32], _residual_epilogue, (y1,), a_fn=_relu2)
    y2, y2_b = _ln_fwd(f"ln2_{tag}", u_b, lng[1], lnb[1])
    (e,) = _mm_fwd(f"ple_{tag}", p_l, wts["plew"], "col", [F32])
    xn, gate, xn_b = _mm_fwd(f"gate_{tag}", y2_b, wts["gate"], "row", [F32, F32, BF16], _gate_epilogue, (y2, e))
    return xn, xn_b, (u_a, y1_b, pre, u_b, y2_b, e, gate)


def _tail_bwd(tag, dxn, saved, wts, lng, p_l, emit, advance, toks):
    u_a, y1_b, pre, u_b, y2_b, e, gate = saved
    dgpre, de = _ple_bwd(f"ple_bwd_{tag}", dxn, e, gate)
    toks = emit(f"{tag}_ple", dict(gate=_mm_dw(f"gate_dw_{tag}", y2_b, dgpre, "row", after=toks),
                                   plew=_mm_dw(f"ple_dw_{tag}", p_l, de, "col")))
    (dy2,) = _mm_dx(f"gate_dx_{tag}", dgpre, wts["gate"], "row", [F32], _plain_add_epilogue, (dxn,), after=toks)
    toks = advance((dy2,))
    du_b, du_b16, dg2, db2 = _ln_bwd(f"ln2_bwd_{tag}", u_b, dy2, lng[1])
    toks = emit(f"{tag}_w2", dict(w2=_mm_dw(f"mlp2_dw_{tag}", pre, du_b16, "row", a_fn=_relu2, after=toks)))
    (dpre,) = _mm_dx(f"mlp2_dx_{tag}", du_b16, wts["w2"], "row", [BF16], _relu2_bwd_epilogue, (pre,), after=toks)
    toks = advance((dpre,))
    toks = emit(f"{tag}_w1", dict(w1=_mm_dw(f"mlp1_dw_{tag}", y1_b, dpre, "col", after=toks)))
    (dy1,) = _mm_dx(f"mlp1_dx_{tag}", dpre, wts["w1"], "col", [F32], _residual_epilogue, (du_b,), after=toks)
    toks = advance((dy1,))
    du_a, du_a16, dg1, db1 = _ln_bwd(f"ln1_bwd_{tag}", u_a, dy1, lng[0])
    return du_a, du_a16, [dg1, dg2], [db1, db2], toks


def _to_slots(a, axis):
    shape = a.shape
    per = shape[axis] // N_DEV
    v = a.reshape(shape[:axis] + (N_DEV, per) + shape[axis + 1:])
    return jnp.moveaxis(v, axis, 0)


def _pad_lanes(a):
    return jnp.pad(a, [(0, 0)] * (a.ndim - 1) + [(0, V7X_LANES - a.shape[-1])])


def kernel(x, p, pool_w, pool_scale, ssm_in_w, ssm_conv_w, ssm_conv_b, ssm_dt_bias, ssm_a_log, ssm_d, ssm_norm_w, ssm_out_w, mlp_w1, mlp_w2, ln_g, ln_b, ple_w, ple_gate_w, loss_target, m_pool_w, m_pool_scale, m_ssm_in_w, m_ssm_conv_w, m_ssm_conv_b, m_ssm_dt_bias, m_ssm_a_log, m_ssm_d, m_ssm_norm_w, m_ssm_out_w, m_mlp_w1, m_mlp_w2, m_ln_g, m_ln_b, m_ple_w, m_ple_gate_w, v_pool_w, v_pool_scale, v_ssm_in_w, v_ssm_conv_w, v_ssm_conv_b, v_ssm_dt_bias, v_ssm_a_log, v_ssm_d, v_ssm_norm_w, v_ssm_out_w, v_mlp_w1, v_mlp_w2, v_ln_g, v_ln_b, v_ple_w, v_ple_gate_w):
    t, d = x.shape[1:]
    h_n = ssm_dt_bias.shape[-1]
    di_s, cd_s, dp_s, d_s = ssm_norm_w.shape[-1], ssm_conv_b.shape[-1], ssm_in_w.shape[-1], ln_g.shape[-1]
    di, cd, dp = N_DEV * di_s, N_DEV * cd_s, N_DEV * dp_s
    p_dim = di // h_n
    g_n = (cd - di) // (2 * D_STATE)
    hpg = h_n // g_n
    zw = di + cd
    assert h_n <= V7X_LANES and dp == zw + h_n and zw % V7X_LANES == 0 and zw % h_n == 0
    cg = d // 4
    me = 4 * lax.axis_index("x") + 2 * lax.axis_index("y") + lax.axis_index("c")

    x0, target = x[0], loss_target[0]
    p_l = [p[0, 0].astype(BF16), p[1, 0].astype(BF16)]

    small_shapes = [(CONV_WIDTH, cd_s), (1, cd_s), (1, di_s), (2, 2, d_s), (2, 2, d_s)]
    small = _pack([ssm_conv_w[0], ssm_conv_b, ssm_norm_w, ln_g, ln_b], 8 * V7X_LANES)
    first = [w.astype(BF16) for w in (pool_w[0], mlp_w1[0])]

    def gather_start(tag, own, after):
        lands = [lax.empty((N_DEV,) + w.shape, w.dtype) for w in own]
        return _async_start(f"ag_{tag}_start", _ag_first_copies, (4 * len(own),), own, lands, after)

    def gather_finish(tag, handle, after):
        n = len(handle[2]) // 2
        own, lands = _async_wait(f"ag_{tag}_wait", _ag_first_copies, handle, n, after)
        fwd = _async_start(f"ag_{tag}_forward_start", _ag_forward_copies, (3 * n,), [], lands)
        _, lands = _async_wait(f"ag_{tag}_forward_wait", _ag_forward_copies, fwd, 0)
        return [lax.dynamic_update_slice_in_dim(g, w[None], me, 0) for g, w in zip(lands, own)]

    ag_first = gather_start("first", first + [small], ())
    zero = ag_first[3][0, 0]
    own_l0 = [(w + zero).astype(BF16) for w in (mlp_w2[0], ple_w[0], ple_gate_w[0])]
    own_ssm = [(ssm_in_w[0] + zero).astype(BF16).T, (ssm_out_w[0] + zero).astype(BF16)]
    own_mlp = [(w + zero).astype(BF16) for w in (mlp_w1[1], mlp_w2[1], ple_w[1], ple_gate_w[1])]
    ag_l0 = gather_start("l0", own_l0, (ag_first[3],))
    ag_ssm = gather_start("ssm", own_ssm, (ag_l0[3],))
    ag_mlp = gather_start("mlp1", own_mlp, (ag_ssm[3],))
    pooled = _pool_windows("pool_fwd", x0, False, after=(ag_mlp[3],))
    pool_g, w1_0, small_g = gather_finish("first", ag_first, (pooled,))
    pool_full = pool_g.transpose(1, 0, 2, 3).reshape(4, cg, cg)
    sm = small_g.reshape(N_DEV, -1)
    o = 0
    parts = []
    for shp in small_shapes:
        size = 1
        for s in shp:
            size *= s
        parts.append(sm[:, o:o + size].reshape((N_DEV,) + shp))
        o += size
    conv_w_full = parts[0].transpose(1, 0, 2).reshape(CONV_WIDTH, cd)
    conv_b_full = parts[1].transpose(1, 0, 2).reshape(1, cd)
    norm_w_full = parts[2].transpose(1, 0, 2).reshape(1, di)
    ln_g_full = parts[3].transpose(1, 2, 0, 3).reshape(2, 2, 1, d)
    ln_b_full = parts[4].transpose(1, 2, 0, 3).reshape(2, 2, 1, d)
    bias128, alog128 = _pad_lanes(ssm_dt_bias), _pad_lanes(ssm_a_log)

    u0, hraw = _pool_mm("pool_mm", pooled, pool_full, pool_scale, x0)
    wts = [dict(w1=w1_0)]

    def finish_l0(after):
        w2_0, plew_0, gate_0 = gather_finish("l0", ag_l0, (after,))
        wts[0].update(w2=w2_0, plew=plew_0, gate=gate_0)

    x1, x1_b, saved0 = _tail_fwd("l0", u0, wts[0], ln_g_full[0], ln_b_full[0], p_l[0], finish_l0)

    in_g, out_g = gather_finish("ssm", ag_ssm, (x1,))
    in_t = in_g.reshape(dp, d)
    (zx,) = _mm_dx("in_proj", x1_b, in_t, "plain", [F32], k_rows=zw)
    dt_raw = _pad_lanes(_in_proj_dt("in_proj_dt", x1_b, in_t, zw, h_n))
    xbc = _conv_fwd("conv_fwd", zx, conv_w_full, conv_b_full, di)
    dt, acs, e_in, e_out = _dt_fwd("dt_fwd", dt_raw, bias128, alog128, 0)

    dt_x, ein_x, eout_x = _expand_heads("expand_heads", [dt, e_in, e_out], h_n, p_dim)
    d_x = jnp.repeat(ssm_d, p_dim, axis=1)

    def to_col(a):
        return a[:, :h_n].reshape(t, g_n, hpg).transpose(1, 0, 2)

    def to_row(a):
        return a[:, :h_n].reshape(t, g_n, hpg).transpose(1, 2, 0)

    def from_col(a):
        return _pad_lanes(a.transpose(1, 0, 2).reshape(t, h_n))

    a_col, a_row = to_col(acs), to_row(acs)
    y_ssd, prev = _ssd_fwd("ssd_fwd", xbc, dt_x, ein_x, eout_x, a_col, a_row, d_x, di, g_n, hpg, p_dim)
    yn = _gated_rms_fwd("gated_rms_fwd", y_ssd, zx, norm_w_full)
    (u2,) = _mm_fwd("out_proj", yn, out_g, "row", [F32], _residual_epilogue, (x1,))
    w1_1, w2_1, plew_1, gate_1 = gather_finish("mlp1", ag_mlp, (u2,))
    wts.append(dict(w1=w1_1, w2=w2_1, plew=plew_1, gate=gate_1))
    x2, _, saved1 = _tail_fwd("l1", u2, wts[1], ln_g_full[1], ln_b_full[1], p_l[1], lambda after: None)

    core = lax.axis_index("c").astype(jnp.int32).reshape(1)
    chip = (2 * lax.axis_index("x") + lax.axis_index("y")).astype(jnp.int32).reshape(1)
    scattering = {}
    pending = []

    def to_chips(after):
        if not pending:
            return []
        tag, names, handle = pending.pop()
        own, halves = _async_wait(f"rs_{tag}_sibling_wait", _rs_sibling_copies, handle, len(names), after)
        sums = [_pair_sum(f"rs_{tag}_pair_sum_{n}", g, hv, core) for n, g, hv in zip(names, own, halves)]
        lands = [lax.empty((3,) + s.shape[1:], s.dtype) for s in sums]
        handle = _async_start(f"rs_{tag}_start", _rs_chip_copies, (3 * len(sums),), sums, lands)
        scattering[tag] = (names, handle)
        return [handle[3]]

    def emit(tag, grads):
        names, arrays = list(grads), list(grads.values())
        toks = to_chips(tuple(arrays))
        lands = [lax.empty((4,) + g.shape[1:], g.dtype) for g in arrays]
        handle = _async_start(f"rs_{tag}_sibling_start", _rs_sibling_copies, (4 * len(arrays),), arrays, lands,
                              tuple(toks))
        pending.append((tag, names, handle))
        return toks + [handle[3]]

    def collect(tag, after):
        names, handle = scattering.pop(tag)
        sums, thirds = _async_wait(f"rs_{tag}_wait", _rs_chip_copies, handle, len(names), after)
        return {n: (s.reshape(4, -1, s.shape[-1]), r.reshape(3, -1, r.shape[-1])) for n, s, r in zip(names, sums, thirds)}

    dx2, loss_cols = _loss_bwd("loss", x2, target)
    du2, du2_b, dg_1, db_1, toks = _tail_bwd("l1", dx2, saved1, wts[1], ln_g_full[1], p_l[1], emit, to_chips, [])
    toks = emit("ssm_out", dict(out=_mm_dw("out_proj_dw", yn, du2_b, "row", after=toks)))
    (dyn,) = _mm_dx("out_proj_dx", du2_b, out_g, "row", [F32], after=toks)
    toks = to_chips((dyn,))
    dy_ssd, dz, d_norm_w = _gated_rms_bwd("gated_rms_bwd", y_ssd, zx, norm_w_full, dyn)
    dxs, dbm, dcm, ddt_x, dacs, dd = _ssd_bwd("ssd_bwd", xbc, dt_x, ein_x, eout_x, a_col, a_row, d_x, prev, dy_ssd,
                                              di, g_n, hpg, p_dim)
    draw, d_bias, d_alog = _dt_bwd("dt_bwd", dt_raw, bias128, alog128, from_col(dacs), from_col(ddt_x), 0)
    conv_parts = [_conv_bwd(f"conv_bwd_{tag}", zx, conv_w_full, conv_b_full, dact, di, first)
                  for tag, dact, first in (("xs", dxs, 0), ("b", dbm, di), ("c", dcm, di + g_n * D_STATE))]
    d_conv_w = jnp.concatenate([c[1] for c in conv_parts], axis=1)
    d_conv_b = jnp.concatenate([c[2] for c in conv_parts], axis=1)
    dzx = jnp.concatenate([dz] + [c[0] for c in conv_parts], axis=1)
    d_dt = draw[:, :h_n]
    g_in_t = _in_proj_dw_t("in_proj_dw", dzx, d_dt, x1_b, after=toks)
    toks = emit("ssm_in", {"in": g_in_t.reshape(N_DEV, dp_s, d)})
    dx_dt = _in_proj_dt_dx("in_proj_dt_dx", d_dt, in_t, zw, du2)
    (dx1,) = _mm_fwd("in_proj_dx", dzx, in_t, "plain", [F32], _plain_add_epilogue, (dx_dt,), after=toks, k_rows=zw)
    toks = to_chips((dx1,))

    du0, _, dg_0, db_0, toks = _tail_bwd("l0", dx1, saved0, wts[0], ln_g_full[0], p_l[0], emit, to_chips, toks)
    dh, dpool, d_scale = _pool_bwd_mm("pool_bwd_mm", du0, hraw, pool_full, pool_scale)
    toks += emit("pool", dict(pool=_to_slots(_pool_dw("pool_dw", pooled, dh), 1)))
    grad_x = _pool_windows("pool_bwd", dpool, True, du0, after=tuple(toks))
    toks = to_chips((grad_x,))

    def update(tag, w, m, v, parts, layer, prev=None):
        own, recv = parts
        return _adamw(f"adamw_{tag}_{layer}", _as3d(w), _as3d(m), _as3d(v),
                      [(own, None), (recv, 0), (recv, 1), (recv, 2)], layer, prev, chip)

    q = collect("l1_ple", (grad_x, *toks))
    r_gate = update("ple_gate_w", ple_gate_w, m_ple_gate_w, v_ple_gate_w, q["gate"], 1)
    r_plew = update("ple_w", ple_w, m_ple_w, v_ple_w, q["plew"], 1)
    r_w2 = update("mlp_w2", mlp_w2, m_mlp_w2, v_mlp_w2, collect("l1_w2", (r_gate[0],))["w2"], 1)
    r_w1 = update("mlp_w1", mlp_w1, m_mlp_w1, v_mlp_w1, collect("l1_w1", (r_w2[0],))["w1"], 1)
    r_out = update("ssm_out_w", ssm_out_w, m_ssm_out_w, v_ssm_out_w, collect("ssm_out", (r_w1[0],))["out"], 0)
    in_wt, in_mt, in_vt = [jnp.swapaxes(a, 1, 2) for a in (ssm_in_w, m_ssm_in_w, v_ssm_in_w)]
    r_in = update("ssm_in_w", in_wt, in_mt, in_vt, collect("ssm_in", (r_out[0],))["in"], 0)

    d_ln_g = jnp.stack([jnp.stack(dg_0), jnp.stack(dg_1)]).reshape(2, 2, d)
    d_ln_b = jnp.stack([jnp.stack(db_0), jnp.stack(db_1)]).reshape(2, 2, d)
    partial_shapes = [(CONV_WIDTH, cd), (1, cd), (1, di), (2, 2, d), (2, 2, d), (1, d), (1, h_n), (1, h_n), (1, h_n),
                      (1, d)]
    partial = _pack([d_conv_w, d_conv_b, d_norm_w, d_ln_g, d_ln_b, d_scale, d_bias[:, :h_n], d_alog[:, :h_n],
                     dd.reshape(1, h_n), loss_cols], 8 * V7X_LANES)
    (all_partials,) = _all_gather("ag_small_grads", [partial], after=(r_in[0],))
    tot = _unpack(_sum8("sum_small_grads", all_partials), partial_shapes)
    t_conv_w, t_conv_b, t_norm_w, t_ln_g, t_ln_b, t_scale, t_bias, t_alog, t_dd, t_loss = tot
    loss = jnp.sum(t_loss)

    def mine(a, per):
        return lax.dynamic_slice_in_dim(a, me * per, per, axis=a.ndim - 1)

    small_names = ["ssm_conv_w", "ssm_conv_b", "ssm_norm_w", "ln_g", "ln_b", "pool_scale", "ssm_dt_bias", "ssm_a_log",
                   "ssm_d"]
    small_w = [ssm_conv_w, ssm_conv_b, ssm_norm_w, ln_g, ln_b, pool_scale, ssm_dt_bias, ssm_a_log, ssm_d]
    small_m = [m_ssm_conv_w, m_ssm_conv_b, m_ssm_norm_w, m_ln_g, m_ln_b, m_pool_scale, m_ssm_dt_bias, m_ssm_a_log,
               m_ssm_d]
    small_v = [v_ssm_conv_w, v_ssm_conv_b, v_ssm_norm_w, v_ln_g, v_ln_b, v_pool_scale, v_ssm_dt_bias, v_ssm_a_log,
               v_ssm_d]
    small_grads = [mine(t_conv_w, cd_s), mine(t_conv_b, cd_s), mine(t_norm_w, di_s), mine(t_ln_g, d_s),
                   mine(t_ln_b, d_s), t_scale, t_bias, t_alog, t_dd]
    shapes = [w.shape for w in small_w]
    pk = [_pack(group, 8 * V7X_LANES)[None] for group in (small_w, small_m, small_v, small_grads)]
    res = _adamw("adamw_small", pk[0], pk[1], pk[2], [(pk[3], 0)], 0)
    upd = {}
    for name, vals in zip(small_names, zip(*[_unpack(r, shapes) for r in res])):
        upd[name] = list(vals)

    q = collect("l0_ple", (res[0],))
    r_gate = update("ple_gate_w", ple_gate_w, m_ple_gate_w, v_ple_gate_w, q["gate"], 0, r_gate)
    r_plew = update("ple_w", ple_w, m_ple_w, v_ple_w, q["plew"], 0, r_plew)
    r_w2 = update("mlp_w2", mlp_w2, m_mlp_w2, v_mlp_w2, collect("l0_w2", (r_gate[0],))["w2"], 0, r_w2)
    r_w1 = update("mlp_w1", mlp_w1, m_mlp_w1, v_mlp_w1, collect("l0_w1", (r_w2[0],))["w1"], 0, r_w1)
    r_pool = update("pool_w", pool_w, m_pool_w, v_pool_w, collect("pool", (r_w1[0],))["pool"], 0)
    assert not scattering
    large = {"pool_w": (pool_w, r_pool), "ssm_in_w": (in_wt, r_in), "ssm_out_w": (ssm_out_w, r_out),
             "mlp_w1": (mlp_w1, r_w1), "mlp_w2": (mlp_w2, r_w2), "ple_w": (ple_w, r_plew),
             "ple_gate_w": (ple_gate_w, r_gate)}
    for name, (w, rs) in large.items():
        upd[name] = [r.reshape(w.shape) for r in rs]
    upd["ssm_in_w"] = [jnp.swapaxes(r, 1, 2) for r in upd["ssm_in_w"]]

    order = ["pool_w", "pool_scale", "ssm_in_w", "ssm_conv_w", "ssm_conv_b", "ssm_dt_bias", "ssm_a_log", "ssm_d",
             "ssm_norm_w", "ssm_out_w", "mlp_w1", "mlp_w2", "ln_g", "ln_b", "ple_w", "ple_gate_w"]
    out = [loss, grad_x[None]]
    for k in range(4):
        out += [upd[name][k] for name in order]
    return tuple(out)
```

```python
import jax
import jax.numpy as jnp
from jax import lax
from jax.experimental import pallas as pl
from jax.experimental.pallas import tpu as pltpu

F32 = jnp.float32
BF16 = jnp.bfloat16
SDS = jax.ShapeDtypeStruct
MESH = pl.DeviceIdType.MESH
ANY = pl.BlockSpec(memory_space=pl.ANY)

N_DEV = 8
DEPTH = 2
ALPHA = (2.0 * DEPTH) ** 0.25
LN_EPS = 1e-5
RMS_EPS = 1e-5
POOL_WINDOW_LOG2 = (1, 2, 3, 4)
D_STATE = 128
CHUNK = 128
CONV_WIDTH = 4
ADAM_LR = 0.001
ADAM_B1 = 0.9
ADAM_B2 = 0.999
ADAM_EPS = 1e-08
ADAM_WD = 0.01
ADAM_STEP = 10

V7X_LANES = 128
V7X_VMEM_LIMIT = 48 * 1024 * 1024


def _cp(*sem):
    return pltpu.CompilerParams(dimension_semantics=sem, vmem_limit_bytes=V7X_VMEM_LIMIT)


def _pick(dim, cap):
    if dim <= cap:
        return dim
    best = None
    for t in range(V7X_LANES, cap + 1, V7X_LANES):
        if dim % t == 0:
            best = t
    assert best is not None, (dim, cap)
    return best


def _row_tile(rows, cols, itemsize=4, target=1 << 20):
    t = rows
    while t % 2 == 0 and t // 2 >= 16 and (t // 2) % 16 == 0 and t * cols * itemsize > target:
        t //= 2
    return t


def _all_gather(name, shards, after=()):
    n, na = len(shards), len(after)

    def body(*refs):
        ins, outs = refs[:n], refs[n + na:2 * n + na]
        send_sems, recv_sems, local_sems = refs[2 * n + na:]
        x, y, c = lax.axis_index("x"), lax.axis_index("y"), lax.axis_index("c")
        me, sibling = (x, y, c), (x, y, 1 - c)
        chips = [(1 - x, y), (x, 1 - y), (1 - x, 1 - y)]

        def copy(a, k, block, to, src=None):
            dst = outs[a].at[4 * block[0] + 2 * block[1] + block[2]]
            return pltpu.make_async_remote_copy(
                src_ref=dst if src is None else src, dst_ref=dst, send_sem=send_sems.at[a, k],
                recv_sem=recv_sems.at[a, k], device_id=to, device_id_type=MESH)

        mine = [pltpu.make_async_copy(ins[a], outs[a].at[4 * x + 2 * y + c], local_sems.at[a]) for a in range(n)]
        for cp in mine:
            cp.start()
        first = []
        for a in range(n):
            first.append(copy(a, 0, me, sibling, src=ins[a]))
            first += [copy(a, 1 + j, me, (*chip, c), src=ins[a]) for j, chip in enumerate(chips)]
        for cp in first:
            cp.start()
        passed = []
        for j, chip in enumerate(chips):
            for a in range(n):
                copy(a, 1 + j, (*chip, c), me).wait_recv()
                fwd = copy(a, 4 + j, (*chip, c), sibling)
                fwd.start()
                passed.append(fwd)
        for a in range(n):
            copy(a, 0, sibling, me).wait_recv()
            for j, chip in enumerate(chips):
                copy(a, 4 + j, (*chip, 1 - c), me).wait_recv()
        for cp in first + passed:
            cp.wait_send()
        for cp in mine:
            cp.wait()

    return pl.pallas_call(
        body, name=name,
        out_shape=[SDS((N_DEV,) + s.shape, s.dtype) for s in shards],
        in_specs=[ANY] * (n + na), out_specs=[ANY] * n,
        scratch_shapes=[pltpu.SemaphoreType.DMA((n, 7)), pltpu.SemaphoreType.DMA((n, 7)),
                        pltpu.SemaphoreType.DMA((n,))],
    )(*shards, *after)


HBM_SPEC = pl.BlockSpec(memory_space=pltpu.HBM)
SEM_SPEC = pl.BlockSpec(memory_space=pltpu.SEMAPHORE)
EFFECT = pltpu.SideEffectType.DATAFLOW_SIDE_EFFECTING


def _ag_first_copies(ins, lands, send_sems, recv_sems):
    x, y, c = lax.axis_index("x"), lax.axis_index("y"), lax.axis_index("c")
    targets = [(x, y, 1 - c), (1 - x, y, c), (x, 1 - y, c), (1 - x, 1 - y, c)]
    return [pltpu.make_async_remote_copy(
        src_ref=ins[a], dst_ref=lands[a].at[4 * x + 2 * y + c], send_sem=send_sems.at[4 * a + k],
        recv_sem=recv_sems.at[4 * a + k], device_id=to, device_id_type=MESH)
        for a in range(len(ins)) for k, to in enumerate(targets)]


def _ag_forward_copies(ins, lands, send_sems, recv_sems):
    x, y, c = lax.axis_index("x"), lax.axis_index("y"), lax.axis_index("c")
    cps = []
    for a in range(len(lands)):
        for j, (px, py) in enumerate([(1 - x, y), (x, 1 - y), (1 - x, 1 - y)]):
            blk = lands[a].at[4 * px + 2 * py + c]
            cps.append(pltpu.make_async_remote_copy(
                src_ref=blk, dst_ref=blk, send_sem=send_sems.at[3 * a + j], recv_sem=recv_sems.at[3 * a + j],
                device_id=(x, y, 1 - c), device_id_type=MESH))
    return cps


def _rs_sibling_copies(ins, lands, send_sems, recv_sems):
    x, y, c = lax.axis_index("x"), lax.axis_index("y"), lax.axis_index("c")
    return [pltpu.make_async_remote_copy(
        src_ref=ins[a].at[2 * q + 1 - c], dst_ref=lands[a].at[q], send_sem=send_sems.at[4 * a + q],
        recv_sem=recv_sems.at[4 * a + q], device_id=(x, y, 1 - c), device_id_type=MESH)
        for a in range(len(ins)) for q in range(4)]


def _rs_chip_copies(ins, lands, send_sems, recv_sems):
    x, y, c = lax.axis_index("x"), lax.axis_index("y"), lax.axis_index("c")
    cps = []
    for a in range(len(ins)):
        for j, (px, py) in enumerate([(1 - x, y), (x, 1 - y), (1 - x, 1 - y)]):
            cps.append(pltpu.make_async_remote_copy(
                src_ref=ins[a].at[2 * px + py], dst_ref=lands[a].at[j], send_sem=send_sems.at[3 * a + j],
                recv_sem=recv_sems.at[3 * a + j], device_id=(px, py, c), device_id_type=MESH))
    return cps


def _async_start(name, build, sem_shape, ins, lands, after=()):
    arrays = [*ins, *lands]
    n_i, n_t, n_a = len(ins), len(arrays), len(after)

    def body(*refs):
        outs = refs[n_t + n_a:]
        for cp in build(refs[:n_i], refs[n_i:n_t], outs[0], outs[1]):
            cp.start()
        outs[-1][...] = jnp.zeros_like(outs[-1])

    res = pl.pallas_call(
        body, name=name,
        out_shape=(pltpu.SemaphoreType.DMA(sem_shape), pltpu.SemaphoreType.DMA(sem_shape),
                   *[pltpu.HBM(a.shape, a.dtype) for a in arrays], SDS((8, V7X_LANES), F32)),
        in_specs=[HBM_SPEC] * n_t + [ANY] * n_a,
        out_specs=(SEM_SPEC, SEM_SPEC, *[HBM_SPEC] * n_t, pl.BlockSpec(memory_space=pltpu.VMEM)),
        input_output_aliases={i: 2 + i for i in range(n_t)},
        compiler_params=pltpu.CompilerParams(has_side_effects=EFFECT),
    )(*[pltpu.with_memory_space_constraint(a, pltpu.HBM) for a in arrays], *after)
    return res[0], res[1], list(res[2:2 + n_t]), res[-1]


def _async_wait(name, build, handle, n_i, after=()):
    send_sems, recv_sems, arrays, _ = handle
    n_t, n_a = len(arrays), len(after)

    def body(*refs):
        for cp in build(refs[:n_i], refs[n_i:n_t], refs[n_t], refs[n_t + 1]):
            cp.wait_send()
            cp.wait_recv()

    res = pl.pallas_call(
        body, name=name, out_shape=tuple(pltpu.HBM(a.shape, a.dtype) for a in arrays),
        in_specs=[HBM_SPEC] * n_t + [SEM_SPEC, SEM_SPEC] + [ANY] * n_a, out_specs=tuple([HBM_SPEC] * n_t),
        input_output_aliases={i: i for i in range(n_t)},
        compiler_params=pltpu.CompilerParams(has_side_effects=EFFECT),
    )(*arrays, send_sems, recv_sems, *after)
    return list(res[:n_i]), list(res[n_i:])


def _mm_core(name, a, b, *, grid, a_spec, b_spec, dims, acc_shape, outs, out_spec, epilogue=None, extras=(),
             extra_specs=(), a_fn=None, after=(), carry=None):
    nk = grid[2]
    if carry is not None:
        after = (*after, carry)
    ne, no, na = len(extras), len(outs), len(after)

    def body(a_ref, b_ref, *rest):
        e_refs, o_refs, acc = rest[:ne], rest[ne + na:ne + na + no], rest[ne + na + no]
        k = pl.program_id(2)

        def product():
            lhs = a_ref[...] if a_fn is None else a_fn(a_ref[...])
            return lax.dot_general(lhs.astype(BF16), b_ref[...].astype(BF16), dims, preferred_element_type=F32)

        @pl.when(k == 0)
        def _():
            acc[...] = product()

        @pl.when(k > 0)
        def _():
            acc[...] += product()

        @pl.when(k == nk - 1)
        def _():
            r = acc[...]
            vals = epilogue(r, *[e[...] for e in e_refs]) if epilogue is not None else (r,)
            for o, v in zip(o_refs, vals):
                o[...] = v.astype(o.dtype)

    res = pl.pallas_call(
        body, name=name, grid=grid, out_shape=list(outs),
        in_specs=[a_spec, b_spec, *extra_specs, *[ANY] * na], out_specs=[out_spec] * no,
        scratch_shapes=[pltpu.VMEM(acc_shape, F32)],
        input_output_aliases={} if carry is None else {1 + ne + na: 0},
        compiler_params=_cp("parallel", "parallel", "arbitrary"),
    )(a, b, *extras, *after)
    return res


NN = (((1,), (0,)), ((), ()))
NT = (((1,), (1,)), ((), ()))
TN = (((0,), (0,)), ((), ()))


def _w_dims(w, kind):
    if kind == "col":
        return w.shape[1], N_DEV * w.shape[2], w.shape[1], w.shape[2]
    if kind == "row":
        return N_DEV * w.shape[1], w.shape[2], w.shape[1], w.shape[2]
    return w.shape[0], w.shape[1], w.shape[0], w.shape[1]


MM_VMEM_BUDGET = 36 * 1024 * 1024


def _row_and_k_blocks(m, tn, k_len, k_caps, a, out_dtypes, extras):
    per_out = sum(jnp.dtype(dt).itemsize for dt in out_dtypes) + sum(e.dtype.itemsize for e in extras)
    best = None
    for tm in (_pick(m, 2048), _pick(m, 1024), _pick(m, 512)):
        for tk in [_pick(k_len, cap) for cap in k_caps]:
            used = tm * tn * (4 + 2 * per_out) + 2 * (tm * tk * a.dtype.itemsize + tk * tn * 2)
            key = ((m // tm) * (k_len // tk), -tm)
            if used <= MM_VMEM_BUDGET and (best is None or key < best[0]):
                best = (key, tm, tk)
    assert best is not None, (m, tn, k_len)
    return best[1], best[2]


def _mm_fwd(name, a, w, kind, out_dtypes, epilogue=None, extras=(), a_fn=None, after=(), k_rows=None):
    if kind == "row":
        w, kind = w.reshape(-1, w.shape[-1]), "plain"
    m = a.shape[0]
    kk, n, ks, ns = _w_dims(w, kind)
    if k_rows is None:
        assert kk == a.shape[1]
    else:
        assert kind == "plain" and k_rows <= min(kk, a.shape[1])
        kk = k_rows
    tn = _pick(ns, 1024) if kind == "col" else _pick(n, 1152)
    tm, tk = _row_and_k_blocks(m, tn, kk, (1024, 512), a, out_dtypes, extras)
    if kind == "col":
        nb = ns // tn
        b_spec = pl.BlockSpec((None, tk, tn), lambda i, j, k: (j // nb, k, j % nb))
    else:
        b_spec = pl.BlockSpec((tk, tn), lambda i, j, k: (k, j))
    mn_spec = pl.BlockSpec((tm, tn), lambda i, j, k: (i, j))
    return _mm_core(
        name, a, w, grid=(m // tm, n // tn, kk // tk),
        a_spec=pl.BlockSpec((tm, tk), lambda i, j, k: (i, k)), b_spec=b_spec, dims=NN, acc_shape=(tm, tn),
        outs=[SDS((m, n), dt) for dt in out_dtypes], out_spec=mn_spec, epilogue=epilogue, extras=extras,
        extra_specs=[mn_spec] * len(extras), a_fn=a_fn, after=after)


def _mm_dx(name, dy, w, kind, out_dtypes, epilogue=None, extras=(), after=(), k_rows=None):
    if kind == "row":
        w, kind = w.reshape(-1, w.shape[-1]), "plain"
    m, n_dim = dy.shape
    kk, n, ks, ns = _w_dims(w, kind)
    assert n == n_dim
    if k_rows is not None:
        assert kind == "plain" and k_rows <= kk
        kk = k_rows
    tn = _pick(kk, 1024)
    if kind == "col":
        tm, tk = _row_and_k_blocks(m, tn, ns, (1024, 512), dy, out_dtypes, extras)
        kb = ns // tk
        b_spec = pl.BlockSpec((None, tn, tk), lambda i, j, k: (k // kb, j, k % kb))
    else:
        tm, tk = _row_and_k_blocks(m, tn, n, (1152, 512), dy, out_dtypes, extras)
        b_spec = pl.BlockSpec((tn, tk), lambda i, j, k: (j, k))
    mk_spec = pl.BlockSpec((tm, tn), lambda i, j, k: (i, j))
    return _mm_core(
        name, dy, w, grid=(m // tm, kk // tn, n // tk),
        a_spec=pl.BlockSpec((tm, tk), lambda i, j, k: (i, k)), b_spec=b_spec, dims=NT, acc_shape=(tm, tn),
        outs=[SDS((m, kk), dt) for dt in out_dtypes], out_spec=mk_spec, epilogue=epilogue, extras=extras,
        extra_specs=[mk_spec] * len(extras), after=after)


def _in_proj_dt(name, a, w_t, first, h_n):
    m, kk = a.shape
    tm, tk = _pick(m, 1024), _pick(kk, 1024)
    blk = first // h_n
    return _mm_core(
        name, a, w_t, grid=(m // tm, 1, kk // tk), a_spec=pl.BlockSpec((tm, tk), lambda i, j, k: (i, k)),
        b_spec=pl.BlockSpec((h_n, tk), lambda i, j, k: (blk, k)), dims=NT, acc_shape=(tm, h_n),
        outs=[SDS((m, h_n), F32)], out_spec=pl.BlockSpec((tm, h_n), lambda i, j, k: (i, 0)))[0]


def _in_proj_dt_dx(name, d_dt, w_t, first, res):
    m, h_n = d_dt.shape
    n = w_t.shape[1]
    tm, tn = _pick(m, 1024), _pick(n, 1024)
    blk = first // h_n
    mn_spec = pl.BlockSpec((tm, tn), lambda i, j, k: (i, j))
    return _mm_core(
        name, d_dt, w_t, grid=(m // tm, n // tn, 1), a_spec=pl.BlockSpec((tm, h_n), lambda i, j, k: (i, 0)),
        b_spec=pl.BlockSpec((h_n, tn), lambda i, j, k: (blk, j)), dims=NN, acc_shape=(tm, tn),
        outs=[SDS((m, n), F32)], out_spec=mn_spec, epilogue=_residual_epilogue, extras=(res,),
        extra_specs=[mn_spec])[0]


def _in_proj_dw_t(name, dz, d_dt, x, after=()):
    m, n = x.shape
    zw, h_n = dz.shape[1], d_dt.shape[1]
    tm, tn, tk = _pick(zw, 1024), _pick(n, 1024), _pick(m, 1024)
    out = SDS((zw + h_n, n), BF16)
    x_spec = pl.BlockSpec((tk, tn), lambda i, j, k: (k, j))
    main = _mm_core(
        name, dz, x, grid=(zw // tm, n // tn, m // tk), a_spec=pl.BlockSpec((tk, tm), lambda i, j, k: (k, i)),
        b_spec=x_spec, dims=TN, acc_shape=(tm, tn), outs=[out], out_spec=pl.BlockSpec((tm, tn), lambda i, j, k: (i, j)),
        after=after)[0]
    blk = zw // h_n
    return _mm_core(
        name + "_dt", d_dt, x, grid=(1, n // tn, m // tk), a_spec=pl.BlockSpec((tk, h_n), lambda i, j, k: (k, 0)),
        b_spec=x_spec, dims=TN, acc_shape=(h_n, tn), outs=[out],
        out_spec=pl.BlockSpec((h_n, tn), lambda i, j, k: (blk, j)), carry=main)[0]


def _mm_dw(name, a, dy, kind, a_fn=None, after=()):
    m, kk = a.shape
    n = dy.shape[1]
    tk = _pick(m, 1024)
    if kind == "col":
        ns = n // N_DEV
        tm, tn = _pick(kk, 1024), _pick(ns, 1024)
        nb = ns // tn
        out = SDS((N_DEV, kk, ns), BF16)
        out_spec = pl.BlockSpec((None, tm, tn), lambda i, j, k: (j // nb, i, j % nb))
    else:
        tm, tn = _pick(kk, 1024), _pick(n, 1152)
        out = SDS((kk, n), BF16)
        out_spec = pl.BlockSpec((tm, tn), lambda i, j, k: (i, j))
    res = _mm_core(
        name, a, dy, grid=(kk // tm, n // tn, m // tk),
        a_spec=pl.BlockSpec((tk, tm), lambda i, j, k: (k, i)), b_spec=pl.BlockSpec((tk, tn), lambda i, j, k: (k, j)),
        dims=TN, acc_shape=(tm, tn), outs=[out], out_spec=out_spec, a_fn=a_fn, after=after)[0]
    return res.reshape(N_DEV, kk // N_DEV, n) if kind == "row" else res


def _rowwise(name, fn, ins, outs, rows, tile):
    arrays, specs = [], []
    for arr, kind in ins:
        arrays.append(arr)
        if kind == "row":
            specs.append(pl.BlockSpec((tile, arr.shape[1]), lambda i: (i, 0)))
        elif kind == "vec":
            specs.append(pl.BlockSpec(arr.shape, lambda i, nd=arr.ndim: (0,) * nd))
        else:
            specs.append(kind)
    out_shapes, out_specs, kinds = [], [], []
    for cols, dt, kind in outs:
        kinds.append(kind)
        if kind == "row":
            out_shapes.append(SDS((rows, cols), dt))
            out_specs.append(pl.BlockSpec((tile, cols), lambda i: (i, 0)))
        else:
            out_shapes.append(SDS((1, cols), F32))
            out_specs.append(pl.BlockSpec((1, cols), lambda i: (0, 0)))
    ni = len(arrays)
    has_acc = "acc" in kinds

    def body(*refs):
        vals = fn(*[r[...] for r in refs[:ni]])
        i = pl.program_id(0)
        for o, v, kind in zip(refs[ni:], vals, kinds):
            if kind == "row":
                o[...] = v.astype(o.dtype)
            else:
                @pl.when(i == 0)
                def _(o=o):
                    o[...] = jnp.zeros_like(o)

                o[...] += v

    return pl.pallas_call(
        body, name=name, grid=(rows // tile,), out_shape=out_shapes, in_specs=specs, out_specs=out_specs,
        compiler_params=_cp("arbitrary" if has_acc else "parallel"),
    )(*arrays)


def _ln_fwd(name, u, g, b, after=()):
    d = u.shape[1]

    def fn(u, g, b, *unused):
        mu = jnp.mean(u, axis=1, keepdims=True)
        xc = u - mu
        var = jnp.mean(xc * xc, axis=1, keepdims=True)
        y = xc * lax.rsqrt(var + LN_EPS) * g + b
        return y, y

    ins = [(u, "row"), (g, "vec"), (b, "vec")] + [(t, "vec") for t in after]
    return _rowwise(name, fn, ins, [(d, F32, "row"), (d, BF16, "row")], u.shape[0], 256)


def _ln_bwd(name, u, dy, g):
    d = u.shape[1]

    def fn(u, dy, g):
        mu = jnp.mean(u, axis=1, keepdims=True)
        xc = u - mu
        var = jnp.mean(xc * xc, axis=1, keepdims=True)
        rstd = lax.rsqrt(var + LN_EPS)
        xhat = xc * rstd
        dxhat = dy * g
        m1 = jnp.mean(dxhat, axis=1, keepdims=True)
        m2 = jnp.mean(dxhat * xhat, axis=1, keepdims=True)
        du = rstd * (dxhat - m1 - xhat * m2)
        return du, du, jnp.sum(dy * xhat, axis=0, keepdims=True), jnp.sum(dy, axis=0, keepdims=True)

    return _rowwise(name, fn, [(u, "row"), (dy, "row"), (g, "vec")],
                    [(d, F32, "row"), (d, BF16, "row"), (d, F32, "acc"), (d, F32, "acc")], u.shape[0], 256)


def _loss_bwd(name, y, target):
    d = y.shape[1]

    def fn(y, t):
        e = y - t
        return e * (1.0 / d), jnp.sum(e * e, axis=0, keepdims=True) * (0.5 / d)

    return _rowwise(name, fn, [(y, "row"), (target, "row")], [(d, F32, "row"), (d, F32, "acc")], y.shape[0], 256)


def _ple_bwd(name, dx, e, gate):
    d = dx.shape[1]

    def fn(dx, e, gate):
        return dx * e * gate * (1.0 - gate), dx * gate

    return _rowwise(name, fn, [(dx, "row"), (e, "row"), (gate, "row")], [(d, BF16, "row"), (d, BF16, "row")],
                    dx.shape[0], 256)


def _sigmoid(v):
    return 1.0 / (1.0 + jnp.exp(-v))


def _gated_rms_fwd(name, y, zx, norm_w, after=()):
    di = y.shape[1]

    def fn(y, z, w, *unused):
        yg = y * (z * _sigmoid(z))
        r = lax.rsqrt(jnp.mean(yg * yg, axis=1, keepdims=True) + RMS_EPS)
        return (yg * r * w,)

    z_spec = pl.BlockSpec((128, di), lambda i: (i, 0))
    ins = [(y, "row"), (zx, z_spec), (norm_w, "vec")] + [(t, "vec") for t in after]
    return _rowwise(name, fn, ins, [(di, BF16, "row")], y.shape[0], 128)[0]


def _gated_rms_bwd(name, y, zx, norm_w, dout):
    di = y.shape[1]

    def fn(y, z, w, dout):
        sg = _sigmoid(z)
        sz = z * sg
        yg = y * sz
        r = lax.rsqrt(jnp.mean(yg * yg, axis=1, keepdims=True) + RMS_EPS)
        dn = dout * w
        dyg = r * (dn - yg * (r * r) * jnp.mean(dn * yg, axis=1, keepdims=True))
        dy = dyg * sz
        dz = dyg * y * (sg * (1.0 + z * (1.0 - sg)))
        return dy, dz, jnp.sum(dout * yg * r, axis=0, keepdims=True)

    z_spec = pl.BlockSpec((128, di), lambda i: (i, 0))
    return _rowwise(name, fn, [(y, "row"), (zx, z_spec), (norm_w, "vec"), (dout, "row")],
                    [(di, F32, "row"), (di, BF16, "row"), (di, F32, "acc")], y.shape[0], 128)


def _shift_down(v, j, row):
    return jnp.where(row >= j, pltpu.roll(v, j, 0), 0.0)


def _shift_up(v, j, row):
    t = v.shape[0]
    return jnp.where(row < t - j, pltpu.roll(v, t - j, 0), 0.0)


def _pool_select(parts, g):
    return jnp.where(g == 0, parts[0], jnp.where(g == 1, parts[1], jnp.where(g == 2, parts[2], parts[3])))


def _pool_windows(name, x, transpose, scale_by=None, after=()):
    t, d = x.shape
    cg = d // 4
    cw = V7X_LANES
    per = cg // cw

    def body(*refs):
        x_ref, o_ref = refs[0], refs[-1]
        g = pl.program_id(0) // per
        xv = x_ref[...]
        row = lax.broadcasted_iota(jnp.int32, (t, 1), 0)
        cnt = jnp.minimum(row + 1, jnp.left_shift(2, g)).astype(F32)
        s = xv / cnt if transpose else xv
        parts = []
        for lg in POOL_WINDOW_LOG2:
            j = 1 << (lg - 1)
            s = s + (_shift_up(s, j, row) if transpose else _shift_down(s, j, row))
            parts.append(s)
        sel = _pool_select(parts, g)
        if transpose:
            o_ref[...] = ALPHA * refs[1][...] + sel - xv
        else:
            o_ref[...] = (sel / cnt - xv).astype(o_ref.dtype)

    col = pl.BlockSpec((t, cw), lambda j: (0, j))
    ins = [x] if scale_by is None else [x, scale_by]
    return pl.pallas_call(
        body, name=name, grid=(d // cw,), out_shape=SDS((t, d), F32 if transpose else BF16),
        in_specs=[col] * len(ins) + [ANY] * len(after), out_specs=col, compiler_params=_cp("parallel"),
    )(*ins, *after)


def _pool_mm(name, pooled, w, scale, x):
    t, d = x.shape
    cg = d // 4
    tm = _pick(t, 1024)

    def body(p_ref, w_ref, s_ref, x_ref, u_ref, h_ref):
        h = jnp.dot(p_ref[...], w_ref[...], preferred_element_type=F32)
        h_ref[...] = h
        u_ref[...] = ALPHA * x_ref[...] + h * s_ref[...]

    blk = pl.BlockSpec((tm, cg), lambda g, i: (i, g))
    return pl.pallas_call(
        body, name=name, grid=(4, t // tm), out_shape=[SDS((t, d), F32), SDS((t, d), F32)],
        in_specs=[blk, pl.BlockSpec((None, cg, cg), lambda g, i: (g, 0, 0)), pl.BlockSpec((1, cg), lambda g, i: (0, g)),
                  blk],
        out_specs=[blk, blk], compiler_params=_cp("parallel", "parallel"),
    )(pooled, w, scale, x)


def _pool_bwd_mm(name, du, hraw, w, scale):
    t, d = du.shape
    cg = d // 4
    tm = _pick(t, 1024)

    def body(du_ref, h_ref, w_ref, s_ref, dh_ref, dp_ref, ds_ref):
        @pl.when(pl.program_id(1) == 0)
        def _():
            ds_ref[...] = jnp.zeros_like(ds_ref)

        duv = du_ref[...]
        ds_ref[...] += jnp.sum(duv * h_ref[...], axis=0, keepdims=True)
        dh = (duv * s_ref[...]).astype(BF16)
        dh_ref[...] = dh
        dp_ref[...] = lax.dot_general(dh, w_ref[...], NT, preferred_element_type=F32)

    blk = pl.BlockSpec((tm, cg), lambda g, i: (i, g))
    vec = pl.BlockSpec((1, cg), lambda g, i: (0, g))
    return pl.pallas_call(
        body, name=name, grid=(4, t // tm), out_shape=[SDS((t, d), BF16), SDS((t, d), F32), SDS((1, d), F32)],
        in_specs=[blk, blk, pl.BlockSpec((None, cg, cg), lambda g, i: (g, 0, 0)), vec],
        out_specs=[blk, blk, vec], compiler_params=_cp("parallel", "arbitrary"),
    )(du, hraw, w, scale)


def _pool_dw(name, pooled, dh):
    t, d = pooled.shape
    cg = d // 4
    tk = _pick(t, 512)
    nk = t // tk

    def body(p_ref, dh_ref, o_ref, acc):
        k = pl.program_id(1)

        @pl.when(k == 0)
        def _():
            acc[...] = jnp.zeros_like(acc)

        acc[...] += lax.dot_general(p_ref[...], dh_ref[...], TN, preferred_element_type=F32)

        @pl.when(k == nk - 1)
        def _():
            o_ref[...] = acc[...].astype(o_ref.dtype)

    blk = pl.BlockSpec((tk, cg), lambda g, k: (k, g))
    return pl.pallas_call(
        body, name=name, grid=(4, nk), out_shape=SDS((4, cg, cg), BF16), in_specs=[blk, blk],
        out_specs=pl.BlockSpec((None, cg, cg), lambda g, k: (g, 0, 0)), scratch_shapes=[pltpu.VMEM((cg, cg), F32)],
        compiler_params=_cp("parallel", "arbitrary"),
    )(pooled, dh)


def _conv_pre(u, w_ref, b_ref, row):
    pre = b_ref[...] + _shift_down(u, 3, row) * w_ref[0:1, :]
    pre = pre + _shift_down(u, 2, row) * w_ref[1:2, :]
    pre = pre + _shift_down(u, 1, row) * w_ref[2:3, :]
    return pre + u * w_ref[3:4, :]


def _conv_fwd(name, zx, conv_w, conv_b, di):
    t = zx.shape[0]
    cd = conv_w.shape[1]
    cw = _pick(cd, 256)
    off = di // cw

    def body(u_ref, w_ref, b_ref, o_ref):
        row = lax.broadcasted_iota(jnp.int32, (t, 1), 0)
        pre = _conv_pre(u_ref[...], w_ref, b_ref, row)
        o_ref[...] = pre * _sigmoid(pre)

    return pl.pallas_call(
        body, name=name, grid=(cd // cw,), out_shape=SDS((t, cd), F32),
        in_specs=[pl.BlockSpec((t, cw), lambda j: (0, off + j)), pl.BlockSpec((CONV_WIDTH, cw), lambda j: (0, j)),
                  pl.BlockSpec((1, cw), lambda j: (0, j))],
        out_specs=pl.BlockSpec((t, cw), lambda j: (0, j)), compiler_params=_cp("parallel"),
    )(zx, conv_w, conv_b)


def _conv_bwd(name, zx, conv_w, conv_b, dact, di, first):
    t, cd = dact.shape
    cw = _pick(cd, 256)
    off, woff = (di + first) // cw, first // cw

    def body(u_ref, w_ref, b_ref, da_ref, du_ref, dw_ref, db_ref):
        row = lax.broadcasted_iota(jnp.int32, (t, 1), 0)
        u = u_ref[...]
        pre = _conv_pre(u, w_ref, b_ref, row)
        sg = _sigmoid(pre)
        dpre = da_ref[...] * (sg * (1.0 + pre * (1.0 - sg)))
        du = dpre * w_ref[3:4, :]
        for j in (1, 2, 3):
            du = du + _shift_up(dpre, j, row) * w_ref[3 - j:4 - j, :]
            dw_ref[3 - j:4 - j, :] = jnp.sum(dpre * _shift_down(u, j, row), axis=0, keepdims=True)
        dw_ref[3:4, :] = jnp.sum(dpre * u, axis=0, keepdims=True)
        db_ref[...] = jnp.sum(dpre, axis=0, keepdims=True)
        du_ref[...] = du.astype(du_ref.dtype)

    wspec = pl.BlockSpec((CONV_WIDTH, cw), lambda j: (0, j))
    bspec = pl.BlockSpec((1, cw), lambda j: (0, j))
    ospec = pl.BlockSpec((t, cw), lambda j: (0, j))
    return pl.pallas_call(
        body, name=name, grid=(cd // cw,), out_shape=[SDS((t, cd), BF16), SDS((CONV_WIDTH, cd), F32), SDS((1, cd), F32)],
        in_specs=[pl.BlockSpec((t, cw), lambda j: (0, off + j)), pl.BlockSpec((CONV_WIDTH, cw), lambda j: (0, woff + j)),
                  pl.BlockSpec((1, cw), lambda j: (0, woff + j)), ospec],
        out_specs=[ospec, wspec, bspec], compiler_params=_cp("parallel"),
    )(zx, conv_w, conv_b, dact)


def _expand_heads(name, arrays, h_n, p):
    t = arrays[0].shape[0]
    n = len(arrays)
    w = _pick(h_n * p, 512)

    def body(*refs):
        j = pl.program_id(0)
        head = lax.broadcasted_iota(jnp.int32, (V7X_LANES, w), 0)
        lane = lax.broadcasted_iota(jnp.int32, (V7X_LANES, w), 1)
        spread = (head == j * (w // p) + lane // p).astype(BF16)
        for a_ref, o_ref in zip(refs[:n], refs[n:]):
            rest = a_ref[...]
            out = jnp.zeros((t, w), F32)
            for _ in range(3):
                piece = rest.astype(BF16)
                out = out + jnp.dot(piece, spread, preferred_element_type=F32)
                rest = rest - piece.astype(F32)
            o_ref[...] = out

    full = pl.BlockSpec((t, V7X_LANES), lambda j: (0, 0))
    return pl.pallas_call(
        body, name=name, grid=(h_n * p // w,), out_shape=[SDS((t, h_n * p), F32)] * n, in_specs=[full] * n,
        out_specs=[pl.BlockSpec((t, w), lambda j: (0, j))] * n, compiler_params=_cp("parallel"),
    )(*arrays)


def _softplus(v):
    return jnp.maximum(v, 0.0) + jnp.log(1.0 + jnp.exp(-jnp.abs(v)))


def _dt_fwd(name, zx, bias, a_log, col_block):
    t = zx.shape[0]

    def body(r_ref, b_ref, al_ref, dt_ref, acs_ref, ein_ref, eout_ref):
        row = lax.broadcasted_iota(jnp.int32, (t, 1), 0) % CHUNK
        dt = _softplus(r_ref[...] + b_ref[...])
        da = dt * (-jnp.exp(al_ref[...]))
        s, r = da, da
        j = 1
        while j < CHUNK:
            s = s + jnp.where(row >= j, pltpu.roll(s, j, 0), 0.0)
            r = r + jnp.where(row < CHUNK - j, pltpu.roll(r, t - j, 0), 0.0)
            j *= 2
        dt_ref[...] = dt
        acs_ref[...] = s
        ein_ref[...] = jnp.exp(s)
        eout_ref[...] = jnp.exp(r - da)

    vec = pl.BlockSpec((1, V7X_LANES), lambda i: (0, 0))
    full = pl.BlockSpec((t, V7X_LANES), lambda i: (0, 0))
    return pl.pallas_call(
        body, name=name, grid=(1,), out_shape=[SDS((t, V7X_LANES), F32)] * 4,
        in_specs=[pl.BlockSpec((t, V7X_LANES), lambda i: (0, col_block)), vec, vec], out_specs=[full] * 4,
        compiler_params=_cp("arbitrary"),
    )(zx, bias, a_log)


def _dt_bwd(name, zx, bias, a_log, d_acs, d_dt, col_block):
    t = zx.shape[0]

    def body(r_ref, b_ref, al_ref, da_ref, dd_ref, draw_ref, db_ref, dal_ref):
        row = lax.broadcasted_iota(jnp.int32, (t, 1), 0) % CHUNK
        pre = r_ref[...] + b_ref[...]
        dt = _softplus(pre)
        a = -jnp.exp(al_ref[...])
        s = da_ref[...]
        j = 1
        while j < CHUNK:
            s = s + jnp.where(row < CHUNK - j, pltpu.roll(s, t - j, 0), 0.0)
            j *= 2
        ddt = dd_ref[...] + s * a
        dal_ref[...] = jnp.sum(s * dt, axis=0, keepdims=True) * a
        draw = ddt * _sigmoid(pre)
        db_ref[...] = jnp.sum(draw, axis=0, keepdims=True)
        draw_ref[...] = draw.astype(draw_ref.dtype)

    vec = pl.BlockSpec((1, V7X_LANES), lambda i: (0, 0))
    full = pl.BlockSpec((t, V7X_LANES), lambda i: (0, 0))
    return pl.pallas_call(
        body, name=name, grid=(1,), out_shape=[SDS((t, V7X_LANES), BF16), SDS((1, V7X_LANES), F32), SDS((1, V7X_LANES), F32)],
        in_specs=[pl.BlockSpec((t, V7X_LANES), lambda i: (0, col_block)), vec, vec, full, full],
        out_specs=[full, vec, vec], compiler_params=_cp("arbitrary"),
    )(zx, bias, a_log, d_acs, d_dt)


def _ssd_specs(t, di, g_n, hpg, p, rev):
    nc = t // CHUNK
    w = hpg * p
    nb = di // D_STATE

    def cc(c):
        return nc - 1 - c if rev else c

    return dict(
        xs=pl.BlockSpec((CHUNK, w), lambda g, c: (cc(c), g)),
        bm=pl.BlockSpec((CHUNK, D_STATE), lambda g, c: (cc(c), nb + g)),
        cm=pl.BlockSpec((CHUNK, D_STATE), lambda g, c: (cc(c), nb + g_n + g)),
        col=pl.BlockSpec((None, CHUNK, hpg), lambda g, c: (g, cc(c), 0)),
        rowv=pl.BlockSpec((None, hpg, CHUNK), lambda g, c: (g, 0, cc(c))),
        head=pl.BlockSpec((None, 1, hpg), lambda g, c: (g, 0, 0)),
        lanes=pl.BlockSpec((1, w), lambda g, c: (0, g)),
        bc=pl.BlockSpec((CHUNK, D_STATE), lambda g, c: (cc(c), g)),
        prev=pl.BlockSpec((None, None, D_STATE, w), lambda g, c: (cc(c), g, 0, 0)),
        seg=pl.BlockSpec((w, V7X_LANES), lambda g, c: (0, 0)),
    )


def _decay_masks(cb, ac_ref, ar_ref, heads):
    li = lax.broadcasted_iota(jnp.int32, (CHUNK, CHUNK), 0)
    si = lax.broadcasted_iota(jnp.int32, (CHUNK, CHUNK), 1)
    lms = [jnp.exp(jnp.where(li >= si, ac_ref[:, hh:hh + 1] - ar_ref[hh:hh + 1, :], -jnp.inf)) for hh in heads]
    return lms, [(cb * lm).astype(BF16) for lm in lms]


def _ssd_fwd(name, xbc, dt_x, ein_x, eout_x, a_col, a_row, d_x, di, g_n, hpg, p):
    t = xbc.shape[0]
    nc = t // CHUNK
    w = hpg * p
    assert 2 * p == V7X_LANES and hpg % 2 == 0
    sp = _ssd_specs(t, di, g_n, hpg, p, False)

    def body(xs_ref, bm_ref, cm_ref, dt_ref, ein_ref, eout_ref, ac_ref, ar_ref, d_ref, y_ref, prev_ref, h_ref):
        @pl.when(pl.program_id(1) == 0)
        def _():
            h_ref[...] = jnp.zeros_like(h_ref)

        bm = bm_ref[...].astype(BF16)
        cm = cm_ref[...].astype(BF16)
        cb = lax.dot_general(cm, bm, NT, preferred_element_type=F32)
        first = lax.broadcasted_iota(jnp.int32, (1, V7X_LANES), 1) < p
        xs = xs_ref[...]
        e_in = ein_ref[...]
        xdt = xs * dt_ref[...]
        ys = []
        for pr in range(hpg // 2):
            _, ms = _decay_masks(cb, ac_ref, ar_ref, (2 * pr, 2 * pr + 1))
            xp = xdt[:, pr * V7X_LANES:(pr + 1) * V7X_LANES]
            rhs = jnp.concatenate([jnp.where(first, xp, 0.0), jnp.where(first, 0.0, xp)], axis=0).astype(BF16)
            ys.append(jnp.dot(jnp.concatenate(ms, axis=1), rhs, preferred_element_type=F32))
        h_prev = h_ref[...]
        prev_ref[...] = h_prev
        y = jnp.concatenate(ys, axis=1) + jnp.dot(cm, h_prev.astype(BF16), preferred_element_type=F32) * e_in
        y_ref[...] = y + xs * d_ref[...]
        st = lax.dot_general(bm, (xdt * eout_ref[...]).astype(BF16), TN, preferred_element_type=F32)
        h_ref[...] = e_in[CHUNK - 1:CHUNK, :] * h_prev + st

    return pl.pallas_call(
        body, name=name, grid=(g_n, nc),
        out_shape=[SDS((t, di), F32), SDS((nc, g_n, D_STATE, w), F32)],
        in_specs=[sp["xs"], sp["bm"], sp["cm"], sp["xs"], sp["xs"], sp["xs"], sp["col"], sp["rowv"], sp["lanes"]],
        out_specs=[sp["xs"], sp["prev"]], scratch_shapes=[pltpu.VMEM((D_STATE, w), F32)],
        compiler_params=_cp("parallel", "arbitrary"),
    )(xbc, xbc, xbc, dt_x, ein_x, eout_x, a_col, a_row, d_x)


def _head_sums(v, seg):
    hi = v.astype(BF16)
    lo = (v - hi.astype(F32)).astype(BF16)
    return jnp.dot(hi, seg, preferred_element_type=F32) + jnp.dot(lo, seg, preferred_element_type=F32)


def _head_totals(v, seg):
    part = v[0:8]
    for r in range(8, v.shape[0], 8):
        part = part + v[r:r + 8]
    return jnp.sum(_head_sums(part, seg), axis=0, keepdims=True)


def _ssd_bwd(name, xbc, dt_x, ein_x, eout_x, a_col, a_row, d_x, prev, dy, di, g_n, hpg, p):
    t = xbc.shape[0]
    nc = t // CHUNK
    w = hpg * p
    sp = _ssd_specs(t, di, g_n, hpg, p, True)
    seg = (lax.broadcasted_iota(jnp.int32, (w, V7X_LANES), 0) // p
           == lax.broadcasted_iota(jnp.int32, (w, V7X_LANES), 1)).astype(BF16)

    def body(xs_ref, bm_ref, cm_ref, dt_ref, ein_ref, eout_ref, ac_ref, ar_ref, d_ref, prev_ref, dy_ref,
             seg_ref, dx_ref, dbm_ref, dcm_ref, ddt_ref, dacs_ref, dd_ref, dh_ref):
        @pl.when(pl.program_id(1) == 0)
        def _():
            dh_ref[...] = jnp.zeros_like(dh_ref)
            dd_ref[...] = jnp.zeros_like(dd_ref)

        bm = bm_ref[...].astype(BF16)
        cm = cm_ref[...].astype(BF16)
        cb = lax.dot_general(cm, bm, NT, preferred_element_type=F32)
        first = lax.broadcasted_iota(jnp.int32, (1, V7X_LANES), 1) < p
        last_row = lax.broadcasted_iota(jnp.int32, (CHUNK, 1), 0) == CHUNK - 1
        seg_m = seg_ref[...]
        xs, dy, e_in, e_out, d_skip = xs_ref[...], dy_ref[...], ein_ref[...], eout_ref[...], d_ref[...]
        dt_l = dt_ref[...]
        xdt = xs * dt_l
        h_prev = prev_ref[...]
        h_prev_b = h_prev.astype(BF16)
        dh_next = dh_ref[...]
        dh_next_b = dh_next.astype(BF16)
        dy_e = (dy * e_in).astype(BF16)
        d_cm = lax.dot_general(dy_e, h_prev_b, NT, preferred_element_type=F32)
        dh_ref[...] = e_in[CHUNK - 1:CHUNK, :] * dh_next + lax.dot_general(cm, dy_e, TN, preferred_element_type=F32)
        q = jnp.dot(bm, dh_next_b, preferred_element_type=F32)
        xf = xdt * e_out
        d_bm = lax.dot_general(xf.astype(BF16), dh_next_b, NT, preferred_element_type=F32)
        d_cb = jnp.zeros((CHUNK, CHUNK), F32)
        parts, w_parts = [], []
        for pr in range(hpg // 2):
            lanes = slice(pr * V7X_LANES, (pr + 1) * V7X_LANES)
            lms, ms = _decay_masks(cb, ac_ref, ar_ref, (2 * pr, 2 * pr + 1))
            xp = xdt[:, lanes]
            xp_b = xp.astype(BF16)
            dyp = dy[:, lanes]
            halves = [jnp.where(first, dyp, 0.0).astype(BF16), jnp.where(first, 0.0, dyp).astype(BF16)]
            for lm, half in zip(lms, halves):
                d_cb = d_cb + lax.dot_general(half, xp_b, NT, preferred_element_type=F32) * lm
            dxd = lax.dot_general(jnp.concatenate(ms, axis=0), jnp.concatenate(halves, axis=0), TN,
                                  preferred_element_type=F32)
            stacked = jnp.concatenate([jnp.where(first, xp, 0.0), jnp.where(first, 0.0, xp)], axis=0).astype(BF16)
            y_diag = jnp.dot(jnp.concatenate(ms, axis=1), stacked, preferred_element_type=F32)
            parts.append(dxd)
            w_parts.append(dyp.astype(BF16).astype(F32) * y_diag - xp_b.astype(F32) * dxd)
        d_xdt = jnp.concatenate(parts, axis=1) + q * e_out
        dx_ref[...] = d_xdt * dt_l + dy * d_skip
        ch = jnp.dot(cm, h_prev_b, preferred_element_type=F32)
        qx = q * xf
        s_a = _head_sums(dy * ch * e_in - qx + jnp.concatenate(w_parts, axis=1), seg_m)[:, :hpg]
        d_last = (_head_totals(qx, seg_m)[:, :hpg]
                  + jnp.exp(ac_ref[CHUNK - 1:CHUNK, :]) * _head_totals(dh_next * h_prev, seg_m)[:, :hpg])
        ddt_ref[...] = _head_sums(d_xdt * xs, seg_m)[:, :hpg]
        dacs_ref[...] = s_a + jnp.where(last_row, d_last, 0.0)
        dd_ref[...] += _head_totals(dy * xs, seg_m)[:, :hpg]
        d_cb_b = d_cb.astype(BF16)
        dcm_ref[...] = d_cm + jnp.dot(d_cb_b, bm, preferred_element_type=F32)
        dbm_ref[...] = d_bm + lax.dot_general(d_cb_b, cm, TN, preferred_element_type=F32)

    gn = g_n * D_STATE
    return pl.pallas_call(
        body, name=name, grid=(g_n, nc),
        out_shape=[SDS((t, di), F32), SDS((t, gn), F32), SDS((t, gn), F32), SDS((g_n, t, hpg), F32),
                   SDS((g_n, t, hpg), F32), SDS((g_n, 1, hpg), F32)],
        in_specs=[sp["xs"], sp["bm"], sp["cm"], sp["xs"], sp["xs"], sp["xs"], sp["col"], sp["rowv"],
                  sp["lanes"], sp["prev"], sp["xs"], sp["seg"]],
        out_specs=[sp["xs"], sp["bc"], sp["bc"], sp["col"], sp["col"], sp["head"]],
        scratch_shapes=[pltpu.VMEM((D_STATE, w), F32)],
        compiler_params=_cp("parallel", "arbitrary"),
    )(xbc, xbc, xbc, dt_x, ein_x, eout_x, a_col, a_row, d_x, prev, dy, seg)


def _as3d(a):
    return a.reshape(a.shape[0], -1, a.shape[-1])


def _pair_sum(name, own, recv, core):
    shape = recv.shape
    cols = shape[-1]
    own3, recv3 = own.reshape(8, -1, cols), recv.reshape(4, -1, cols)
    rows = recv3.shape[1]
    tr = _row_tile(rows, cols, 2)

    def body(c_ref, a_ref, b_ref, o_ref):
        o_ref[...] = (a_ref[...].astype(F32) + b_ref[...].astype(F32)).astype(o_ref.dtype)

    blk = pl.BlockSpec((None, tr, cols), lambda q, i, c_ref: (q, i, 0))
    out = pl.pallas_call(
        body, name=name, out_shape=SDS(recv3.shape, recv.dtype),
        grid_spec=pltpu.PrefetchScalarGridSpec(
            num_scalar_prefetch=1, grid=(4, rows // tr),
            in_specs=[pl.BlockSpec((None, tr, cols), lambda q, i, c_ref: (2 * q + c_ref[0], i, 0)), blk], out_specs=blk),
        compiler_params=_cp("parallel", "parallel"),
    )(core, own3, recv3)
    return out.reshape(shape)


def _adamw(name, w, m, v, parts, layer, prev=None, sel=None):
    lyr, rows, cols = w.shape
    n = len(parts)
    by_rows = rows % 16 == 0
    tr, tc = (_row_tile(rows, cols), cols) if by_rows else (rows, _pick(cols, 256))
    np_ = 0 if prev is None else 4
    if sel is None:
        sel = jnp.zeros((1,), jnp.int32)

    def body(sel_ref, *refs):
        w_ref, m_ref, v_ref = refs[:3]
        p_refs = refs[3:3 + n]
        g_ref, d_ref, nm_ref, nv_ref = refs[3 + n + np_:]
        g = p_refs[0][...].astype(F32)
        for r in p_refs[1:]:
            g = g + r[...].astype(F32)
        nm = ADAM_B1 * m_ref[...] + (1.0 - ADAM_B1) * g
        nv = ADAM_B2 * v_ref[...] + (1.0 - ADAM_B2) * (g * g)
        m_hat = nm / (1.0 - ADAM_B1 ** ADAM_STEP)
        v_hat = nv / (1.0 - ADAM_B2 ** ADAM_STEP)
        g_ref[...] = g
        d_ref[...] = -ADAM_LR * (m_hat / (jnp.sqrt(v_hat) + ADAM_EPS) + ADAM_WD * w_ref[...])
        nm_ref[...] = nm
        nv_ref[...] = nv

    def at(lead):
        return pl.BlockSpec((None, tr, tc), lambda i, s: (lead(s), i, 0) if by_rows else (lead(s), 0, i))

    lspec = at(lambda s: layer)
    pspecs = [at(lambda s: s[0]) if q is None else at(lambda s, q=q: q) for _, q in parts]
    aliases = {} if prev is None else {4 + n + q: q for q in range(4)}
    return pl.pallas_call(
        body, name=name, out_shape=[SDS(w.shape, F32)] * 4,
        grid_spec=pltpu.PrefetchScalarGridSpec(
            num_scalar_prefetch=1, grid=(rows // tr if by_rows else cols // tc,),
            in_specs=[lspec] * 3 + pspecs + [ANY] * np_, out_specs=[lspec] * 4),
        input_output_aliases=aliases, compiler_params=_cp("parallel"),
    )(sel, w, m, v, *[arr for arr, _ in parts], *(prev or ()))


def _sum8(name, parts):
    rows = parts.shape[1]

    def body(p_ref, o_ref):
        s = p_ref[0]
        for q in range(1, N_DEV):
            s = s + p_ref[q]
        o_ref[...] = s

    return pl.pallas_call(
        body, name=name, grid=(1,), out_shape=SDS((rows, V7X_LANES), F32),
        in_specs=[pl.BlockSpec((N_DEV, rows, V7X_LANES), lambda i: (0, 0, 0))],
        out_specs=pl.BlockSpec((rows, V7X_LANES), lambda i: (0, 0)), compiler_params=_cp("arbitrary"),
    )(parts)


def _pack(vectors, align):
    flat = jnp.concatenate([v.reshape(-1) for v in vectors])
    pad = (-flat.shape[0]) % align
    if pad:
        flat = jnp.concatenate([flat, jnp.zeros((pad,), F32)])
    return flat.reshape(-1, V7X_LANES)


def _unpack(packed, shapes):
    flat = packed.reshape(-1)
    out, o = [], 0
    for s in shapes:
        size = 1
        for dim in s:
            size *= dim
        out.append(flat[o:o + size].reshape(s))
        o += size
    return out


def _residual_epilogue(acc, res):
    return (ALPHA * res + acc,)


def _plain_add_epilogue(acc, res):
    return (res + acc,)


def _gate_epilogue(acc, y, e):
    gate = _sigmoid(acc)
    xn = y + gate * e
    return xn, gate, xn


def _relu2(pre):
    r = jnp.maximum(pre, 0.0)
    return r * r


def _relu2_bwd_epilogue(acc, pre):
    return (acc * (2.0 * jnp.maximum(pre.astype(F32), 0.0)),)


def _tail_fwd(tag, u_a, wts, lng, lnb, p_l, finish_w1, finish_w2, after=()):
    y1, y1_b = _ln_fwd(f"ln1_{tag}", u_a, lng[0], lnb[0], after)
    finish_w1(y1_b)
    (pre,) = _mm_fwd(f"mlp1_{tag}", y1_b, wts["w1"], "col", [BF16])
    finish_w2(pre)
    (u_b,) = _mm_fwd(f"mlp2_{tag}", pre, wts["w2"], "row", [F32], _residual_epilogue, (y1,), a_fn=_relu2)
    y2, y2_b = _ln_fwd(f"ln2_{tag}", u_b, lng[1], lnb[1])
    (e,) = _mm_fwd(f"ple_{tag}", p_l, wts["plew"], "col", [F32])
    xn, gate, xn_b = _mm_fwd(f"gate_{tag}", y2_b, wts["gate"], "row", [F32, F32, BF16], _gate_epilogue, (y2, e))
    return xn, xn_b, (u_a, y1_b, pre, u_b, y2_b, e, gate)


def _tail_bwd(tag, dxn, saved, wts, lng, p_l, emit, advance, toks):
    u_a, y1_b, pre, u_b, y2_b, e, gate = saved
    dgpre, de = _ple_bwd(f"ple_bwd_{tag}", dxn, e, gate)
    toks = emit(f"{tag}_ple", dict(gate=_mm_dw(f"gate_dw_{tag}", y2_b, dgpre, "row", after=toks),
                                   plew=_mm_dw(f"ple_dw_{tag}", p_l, de, "col")))
    (dy2,) = _mm_dx(f"gate_dx_{tag}", dgpre, wts["gate"], "row", [F32], _plain_add_epilogue, (dxn,), after=toks)
    toks = advance((dy2,))
    du_b, du_b16, dg2, db2 = _ln_bwd(f"ln2_bwd_{tag}", u_b, dy2, lng[1])
    toks = emit(f"{tag}_w2", dict(w2=_mm_dw(f"mlp2_dw_{tag}", pre, du_b16, "row", a_fn=_relu2, after=toks)))
    (dpre,) = _mm_dx(f"mlp2_dx_{tag}", du_b16, wts["w2"], "row", [BF16], _relu2_bwd_epilogue, (pre,), after=toks)
    toks = advance((dpre,))
    toks = emit(f"{tag}_w1", dict(w1=_mm_dw(f"mlp1_dw_{tag}", y1_b, dpre, "col", after=toks)))
    (dy1,) = _mm_dx(f"mlp1_dx_{tag}", dpre, wts["w1"], "col", [F32], _residual_epilogue, (du_b,), after=toks)
    toks = advance((dy1,))
    du_a, du_a16, dg1, db1 = _ln_bwd(f"ln1_bwd_{tag}", u_a, dy1, lng[0])
    return du_a, du_a16, [dg1, dg2], [db1, db2], toks


def _to_slots(a, axis):
    shape = a.shape
    per = shape[axis] // N_DEV
    v = a.reshape(shape[:axis] + (N_DEV, per) + shape[axis + 1:])
    return jnp.moveaxis(v, axis, 0)


def _pad_lanes(a):
    return jnp.pad(a, [(0, 0)] * (a.ndim - 1) + [(0, V7X_LANES - a.shape[-1])])


def kernel(x, p, pool_w, pool_scale, ssm_in_w, ssm_conv_w, ssm_conv_b, ssm_dt_bias, ssm_a_log, ssm_d, ssm_norm_w, ssm_out_w, mlp_w1, mlp_w2, ln_g, ln_b, ple_w, ple_gate_w, loss_target, m_pool_w, m_pool_scale, m_ssm_in_w, m_ssm_conv_w, m_ssm_conv_b, m_ssm_dt_bias, m_ssm_a_log, m_ssm_d, m_ssm_norm_w, m_ssm_out_w, m_mlp_w1, m_mlp_w2, m_ln_g, m_ln_b, m_ple_w, m_ple_gate_w, v_pool_w, v_pool_scale, v_ssm_in_w, v_ssm_conv_w, v_ssm_conv_b, v_ssm_dt_bias, v_ssm_a_log, v_ssm_d, v_ssm_norm_w, v_ssm_out_w, v_mlp_w1, v_mlp_w2, v_ln_g, v_ln_b, v_ple_w, v_ple_gate_w):
    t, d = x.shape[1:]
    h_n = ssm_dt_bias.shape[-1]
    di_s, cd_s, dp_s, d_s = ssm_norm_w.shape[-1], ssm_conv_b.shape[-1], ssm_in_w.shape[-1], ln_g.shape[-1]
    di, cd, dp = N_DEV * di_s, N_DEV * cd_s, N_DEV * dp_s
    p_dim = di // h_n
    g_n = (cd - di) // (2 * D_STATE)
    hpg = h_n // g_n
    zw = di + cd
    assert h_n <= V7X_LANES and dp == zw + h_n and zw % V7X_LANES == 0 and zw % h_n == 0
    cg = d // 4
    me = 4 * lax.axis_index("x") + 2 * lax.axis_index("y") + lax.axis_index("c")

    x0, target = x[0], loss_target[0]
    p_l = [p[0, 0].astype(BF16), p[1, 0].astype(BF16)]

    small_shapes = [(CONV_WIDTH, cd_s), (1, cd_s), (1, di_s), (2, 2, d_s), (2, 2, d_s)]
    small = _pack([ssm_conv_w[0], ssm_conv_b, ssm_norm_w, ln_g, ln_b], 8 * V7X_LANES)
    first = [w.astype(BF16) for w in (pool_w[0], mlp_w1[0])]

    def gather_start(tag, own, after):
        lands = [lax.empty((N_DEV,) + w.shape, w.dtype) for w in own]
        return _async_start(f"ag_{tag}_start", _ag_first_copies, (4 * len(own),), own, lands, after)

    def gather_pass(tag, handle, after):
        n = len(handle[2]) // 2
        own, lands = _async_wait(f"ag_{tag}_wait", _ag_first_copies, handle, n, after)
        return _async_start(f"ag_{tag}_forward_start", _ag_forward_copies, (3 * n,), [], lands), own

    def gather_done(tag, passed, after=()):
        fwd, own = passed
        _, lands = _async_wait(f"ag_{tag}_forward_wait", _ag_forward_copies, fwd, 0, after)
        return [lax.dynamic_update_slice_in_dim(g, w[None], me, 0) for g, w in zip(lands, own)]

    def gather_finish(tag, handle, after):
        return gather_done(tag, gather_pass(tag, handle, after))

    ag_first = gather_start("first", first + [small], ())
    zero = ag_first[3][0, 0]
    own_l0 = [(w + zero).astype(BF16) for w in (mlp_w2[0], ple_w[0], ple_gate_w[0])]
    own_ssm = [(ssm_in_w[0] + zero).astype(BF16).T]
    own_out = [(ssm_out_w[0] + zero).astype(BF16)]
    own_mlp = [(w + zero).astype(BF16) for w in (mlp_w1[1], mlp_w2[1], ple_w[1], ple_gate_w[1])]
    ag_l0 = gather_start("l0", own_l0, (ag_first[3],))
    ag_ssm = gather_start("ssm", own_ssm, (ag_l0[3],))
    ag_out = gather_start("out", own_out, (ag_ssm[3],))
    ag_mlp = gather_start("mlp1", own_mlp, (ag_out[3],))
    pooled = _pool_windows("pool_fwd", x0, False, after=(ag_mlp[3],))
    pool_g, w1_0, small_g = gather_finish("first", ag_first, (pooled,))
    pool_full = pool_g.transpose(1, 0, 2, 3).reshape(4, cg, cg)
    sm = small_g.reshape(N_DEV, -1)
    o = 0
    parts = []
    for shp in small_shapes:
        size = 1
        for s in shp:
            size *= s
        parts.append(sm[:, o:o + size].reshape((N_DEV,) + shp))
        o += size
    conv_w_full = parts[0].transpose(1, 0, 2).reshape(CONV_WIDTH, cd)
    conv_b_full = parts[1].transpose(1, 0, 2).reshape(1, cd)
    norm_w_full = parts[2].transpose(1, 0, 2).reshape(1, di)
    ln_g_full = parts[3].transpose(1, 2, 0, 3).reshape(2, 2, 1, d)
    ln_b_full = parts[4].transpose(1, 2, 0, 3).reshape(2, 2, 1, d)
    bias128, alog128 = _pad_lanes(ssm_dt_bias), _pad_lanes(ssm_a_log)

    u0, hraw = _pool_mm("pool_mm", pooled, pool_full, pool_scale, x0)
    wts = [dict(w1=w1_0)]

    def finish_l0(after):
        w2_0, plew_0, gate_0 = gather_finish("l0", ag_l0, (after,))
        wts[0].update(w2=w2_0, plew=plew_0, gate=gate_0)

    x1, x1_b, saved0 = _tail_fwd("l0", u0, wts[0], ln_g_full[0], ln_b_full[0], p_l[0], lambda after: None, finish_l0)

    (in_g,) = gather_finish("ssm", ag_ssm, (x1,))
    in_t = in_g.reshape(dp, d)
    (zx,) = _mm_dx("in_proj", x1_b, in_t, "plain", [F32], k_rows=zw)
    dt_raw = _pad_lanes(_in_proj_dt("in_proj_dt", x1_b, in_t, zw, h_n))
    xbc = _conv_fwd("conv_fwd", zx, conv_w_full, conv_b_full, di)
    dt, acs, e_in, e_out = _dt_fwd("dt_fwd", dt_raw, bias128, alog128, 0)

    dt_x, ein_x, eout_x = _expand_heads("expand_heads", [dt, e_in, e_out], h_n, p_dim)
    d_x = jnp.repeat(ssm_d, p_dim, axis=1)

    def to_col(a):
        return a[:, :h_n].reshape(t, g_n, hpg).transpose(1, 0, 2)

    def to_row(a):
        return a[:, :h_n].reshape(t, g_n, hpg).transpose(1, 2, 0)

    def from_col(a):
        return _pad_lanes(a.transpose(1, 0, 2).reshape(t, h_n))

    a_col, a_row = to_col(acs), to_row(acs)
    y_ssd, prev = _ssd_fwd("ssd_fwd", xbc, dt_x, ein_x, eout_x, a_col, a_row, d_x, di, g_n, hpg, p_dim)
    out_passed = gather_pass("out", ag_out, (y_ssd,))
    yn = _gated_rms_fwd("gated_rms_fwd", y_ssd, zx, norm_w_full, after=(out_passed[0][3],))
    (out_g,) = gather_done("out", out_passed, (yn,))
    (u2,) = _mm_fwd("out_proj", yn, out_g, "row", [F32], _residual_epilogue, (x1,))
    mlp_passed = gather_pass("mlp1", ag_mlp, (u2,))
    wts.append({})

    def finish_l1(after):
        wts[1].update(zip(("w1", "w2", "plew", "gate"), gather_done("mlp1", mlp_passed, (after,))))

    x2, _, saved1 = _tail_fwd("l1", u2, wts[1], ln_g_full[1], ln_b_full[1], p_l[1], finish_l1, lambda after: None,
                              after=(mlp_passed[0][3],))

    core = lax.axis_index("c").astype(jnp.int32).reshape(1)
    chip = (2 * lax.axis_index("x") + lax.axis_index("y")).astype(jnp.int32).reshape(1)
    scattering = {}
    pending = []

    def to_chips(after):
        if not pending:
            return []
        tag, names, handle = pending.pop()
        own, halves = _async_wait(f"rs_{tag}_sibling_wait", _rs_sibling_copies, handle, len(names), after)
        sums = [_pair_sum(f"rs_{tag}_pair_sum_{n}", g, hv, core) for n, g, hv in zip(names, own, halves)]
        lands = [lax.empty((3,) + s.shape[1:], s.dtype) for s in sums]
        handle = _async_start(f"rs_{tag}_start", _rs_chip_copies, (3 * len(sums),), sums, lands)
        scattering[tag] = (names, handle)
        return [handle[3]]

    def emit(tag, grads):
        names, arrays = list(grads), list(grads.values())
        toks = to_chips(tuple(arrays))
        lands = [lax.empty((4,) + g.shape[1:], g.dtype) for g in arrays]
        handle = _async_start(f"rs_{tag}_sibling_start", _rs_sibling_copies, (4 * len(arrays),), arrays, lands,
                              tuple(toks))
        pending.append((tag, names, handle))
        return toks + [handle[3]]

    def collect(tag, after):
        names, handle = scattering.pop(tag)
        sums, thirds = _async_wait(f"rs_{tag}_wait", _rs_chip_copies, handle, len(names), after)
        return {n: (s.reshape(4, -1, s.shape[-1]), r.reshape(3, -1, r.shape[-1])) for n, s, r in zip(names, sums, thirds)}

    dx2, loss_cols = _loss_bwd("loss", x2, target)
    du2, du2_b, dg_1, db_1, toks = _tail_bwd("l1", dx2, saved1, wts[1], ln_g_full[1], p_l[1], emit, to_chips, [])
    toks = emit("ssm_out", dict(out=_mm_dw("out_proj_dw", yn, du2_b, "row", after=toks)))
    (dyn,) = _mm_dx("out_proj_dx", du2_b, out_g, "row", [F32], after=toks)
    toks = to_chips((dyn,))
    dy_ssd, dz, d_norm_w = _gated_rms_bwd("gated_rms_bwd", y_ssd, zx, norm_w_full, dyn)
    dxs, dbm, dcm, ddt_x, dacs, dd = _ssd_bwd("ssd_bwd", xbc, dt_x, ein_x, eout_x, a_col, a_row, d_x, prev, dy_ssd,
                                              di, g_n, hpg, p_dim)
    draw, d_bias, d_alog = _dt_bwd("dt_bwd", dt_raw, bias128, alog128, from_col(dacs), from_col(ddt_x), 0)
    conv_parts = [_conv_bwd(f"conv_bwd_{tag}", zx, conv_w_full, conv_b_full, dact, di, first)
                  for tag, dact, first in (("xs", dxs, 0), ("b", dbm, di), ("c", dcm, di + g_n * D_STATE))]
    d_conv_w = jnp.concatenate([c[1] for c in conv_parts], axis=1)
    d_conv_b = jnp.concatenate([c[2] for c in conv_parts], axis=1)
    dzx = jnp.concatenate([dz] + [c[0] for c in conv_parts], axis=1)
    d_dt = draw[:, :h_n]
    g_in_t = _in_proj_dw_t("in_proj_dw", dzx, d_dt, x1_b, after=toks)
    toks = emit("ssm_in", {"in": g_in_t.reshape(N_DEV, dp_s, d)})
    dx_dt = _in_proj_dt_dx("in_proj_dt_dx", d_dt, in_t, zw, du2)
    (dx1,) = _mm_fwd("in_proj_dx", dzx, in_t, "plain", [F32], _plain_add_epilogue, (dx_dt,), after=toks, k_rows=zw)
    toks = to_chips((dx1,))

    du0, _, dg_0, db_0, toks = _tail_bwd("l0", dx1, saved0, wts[0], ln_g_full[0], p_l[0], emit, to_chips, toks)
    dh, dpool, d_scale = _pool_bwd_mm("pool_bwd_mm", du0, hraw, pool_full, pool_scale)
    toks += emit("pool", dict(pool=_to_slots(_pool_dw("pool_dw", pooled, dh), 1)))
    grad_x = _pool_windows("pool_bwd", dpool, True, du0, after=tuple(toks))
    toks = to_chips((grad_x,))

    def update(tag, w, m, v, parts, layer, prev=None):
        own, recv = parts
        return _adamw(f"adamw_{tag}_{layer}", _as3d(w), _as3d(m), _as3d(v),
                      [(own, None), (recv, 0), (recv, 1), (recv, 2)], layer, prev, chip)

    q = collect("l1_ple", (grad_x, *toks))
    r_gate = update("ple_gate_w", ple_gate_w, m_ple_gate_w, v_ple_gate_w, q["gate"], 1)
    r_plew = update("ple_w", ple_w, m_ple_w, v_ple_w, q["plew"], 1)
    r_w2 = update("mlp_w2", mlp_w2, m_mlp_w2, v_mlp_w2, collect("l1_w2", (r_gate[0],))["w2"], 1)
    r_w1 = update("mlp_w1", mlp_w1, m_mlp_w1, v_mlp_w1, collect("l1_w1", (r_w2[0],))["w1"], 1)
    r_out = update("ssm_out_w", ssm_out_w, m_ssm_out_w, v_ssm_out_w, collect("ssm_out", (r_w1[0],))["out"], 0)
    in_wt, in_mt, in_vt = [jnp.swapaxes(a, 1, 2) for a in (ssm_in_w, m_ssm_in_w, v_ssm_in_w)]
    r_in = update("ssm_in_w", in_wt, in_mt, in_vt, collect("ssm_in", (r_out[0],))["in"], 0)

    d_ln_g = jnp.stack([jnp.stack(dg_0), jnp.stack(dg_1)]).reshape(2, 2, d)
    d_ln_b = jnp.stack([jnp.stack(db_0), jnp.stack(db_1)]).reshape(2, 2, d)
    partial_shapes = [(CONV_WIDTH, cd), (1, cd), (1, di), (2, 2, d), (2, 2, d), (1, d), (1, h_n), (1, h_n), (1, h_n),
                      (1, d)]
    partial = _pack([d_conv_w, d_conv_b, d_norm_w, d_ln_g, d_ln_b, d_scale, d_bias[:, :h_n], d_alog[:, :h_n],
                     dd.reshape(1, h_n), loss_cols], 8 * V7X_LANES)
    (all_partials,) = _all_gather("ag_small_grads", [partial], after=(r_in[0],))
    tot = _unpack(_sum8("sum_small_grads", all_partials), partial_shapes)
    t_conv_w, t_conv_b, t_norm_w, t_ln_g, t_ln_b, t_scale, t_bias, t_alog, t_dd, t_loss = tot
    loss = jnp.sum(t_loss)

    def mine(a, per):
        return lax.dynamic_slice_in_dim(a, me * per, per, axis=a.ndim - 1)

    small_names = ["ssm_conv_w", "ssm_conv_b", "ssm_norm_w", "ln_g", "ln_b", "pool_scale", "ssm_dt_bias", "ssm_a_log",
                   "ssm_d"]
    small_w = [ssm_conv_w, ssm_conv_b, ssm_norm_w, ln_g, ln_b, pool_scale, ssm_dt_bias, ssm_a_log, ssm_d]
    small_m = [m_ssm_conv_w, m_ssm_conv_b, m_ssm_norm_w, m_ln_g, m_ln_b, m_pool_scale, m_ssm_dt_bias, m_ssm_a_log,
               m_ssm_d]
    small_v = [v_ssm_conv_w, v_ssm_conv_b, v_ssm_norm_w, v_ln_g, v_ln_b, v_pool_scale, v_ssm_dt_bias, v_ssm_a_log,
               v_ssm_d]
    small_grads = [mine(t_conv_w, cd_s), mine(t_conv_b, cd_s), mine(t_norm_w, di_s), mine(t_ln_g, d_s),
                   mine(t_ln_b, d_s), t_scale, t_bias, t_alog, t_dd]
    shapes = [w.shape for w in small_w]
    pk = [_pack(group, 8 * V7X_LANES)[None] for group in (small_w, small_m, small_v, small_grads)]
    res = _adamw("adamw_small", pk[0], pk[1], pk[2], [(pk[3], 0)], 0)
    upd = {}
    for name, vals in zip(small_names, zip(*[_unpack(r, shapes) for r in res])):
        upd[name] = list(vals)

    q = collect("l0_ple", (res[0],))
    r_gate = update("ple_gate_w", ple_gate_w, m_ple_gate_w, v_ple_gate_w, q["gate"], 0, r_gate)
    r_plew = update("ple_w", ple_w, m_ple_w, v_ple_w, q["plew"], 0, r_plew)
    r_w2 = update("mlp_w2", mlp_w2, m_mlp_w2, v_mlp_w2, collect("l0_w2", (r_gate[0],))["w2"], 0, r_w2)
    r_w1 = update("mlp_w1", mlp_w1, m_mlp_w1, v_mlp_w1, collect("l0_w1", (r_w2[0],))["w1"], 0, r_w1)
    r_pool = update("pool_w", pool_w, m_pool_w, v_pool_w, collect("pool", (r_w1[0],))["pool"], 0)
    assert not scattering
    large = {"pool_w": (pool_w, r_pool), "ssm_in_w": (in_wt, r_in), "ssm_out_w": (ssm_out_w, r_out),
             "mlp_w1": (mlp_w1, r_w1), "mlp_w2": (mlp_w2, r_w2), "ple_w": (ple_w, r_plew),
             "ple_gate_w": (ple_gate_w, r_gate)}
    for name, (w, rs) in large.items():
        upd[name] = [r.reshape(w.shape) for r in rs]
    upd["ssm_in_w"] = [jnp.swapaxes(r, 1, 2) for r in upd["ssm_in_w"]]

    order = ["pool_w", "pool_scale", "ssm_in_w", "ssm_conv_w", "ssm_conv_b", "ssm_dt_bias", "ssm_a_log", "ssm_d",
             "ssm_norm_w", "ssm_out_w", "mlp_w1", "mlp_w2", "ln_g", "ln_b", "ple_w", "ple_gate_w"]
    out = [loss, grad_x[None]]
    for k in range(4):
        out += [upd[name][k] for name in order]
    return tuple(out)
```

```python
import jax
import jax.numpy as jnp
from jax import lax
from jax.experimental import pallas as pl
from jax.experimental.pallas import tpu as pltpu

F32 = jnp.float32
BF16 = jnp.bfloat16
SDS = jax.ShapeDtypeStruct
MESH = pl.DeviceIdType.MESH
ANY = pl.BlockSpec(memory_space=pl.ANY)

N_DEV = 8
DEPTH = 2
ALPHA = (2.0 * DEPTH) ** 0.25
LN_EPS = 1e-5
RMS_EPS = 1e-5
POOL_WINDOW_LOG2 = (1, 2, 3, 4)
D_STATE = 128
CHUNK = 128
SSD_STEP_CHUNKS = 2
CONV_WIDTH = 4
ADAM_LR = 0.001
ADAM_B1 = 0.9
ADAM_B2 = 0.999
ADAM_EPS = 1e-08
ADAM_WD = 0.01
ADAM_STEP = 10

V7X_LANES = 128
V7X_VMEM_LIMIT = 48 * 1024 * 1024


def _cp(*sem):
    return pltpu.CompilerParams(dimension_semantics=sem, vmem_limit_bytes=V7X_VMEM_LIMIT)


def _pick(dim, cap):
    if dim <= cap:
        return dim
    best = None
    for t in range(V7X_LANES, cap + 1, V7X_LANES):
        if dim % t == 0:
            best = t
    assert best is not None, (dim, cap)
    return best


def _row_tile(rows, cols, itemsize=4, target=1 << 20):
    t = rows
    while t % 2 == 0 and t // 2 >= 16 and (t // 2) % 16 == 0 and t * cols * itemsize > target:
        t //= 2
    return t


def _all_gather(name, shards, after=()):
    n, na = len(shards), len(after)

    def body(*refs):
        ins, outs = refs[:n], refs[n + na:2 * n + na]
        send_sems, recv_sems, local_sems = refs[2 * n + na:]
        x, y, c = lax.axis_index("x"), lax.axis_index("y"), lax.axis_index("c")
        me, sibling = (x, y, c), (x, y, 1 - c)
        chips = [(1 - x, y), (x, 1 - y), (1 - x, 1 - y)]

        def copy(a, k, block, to, src=None):
            dst = outs[a].at[4 * block[0] + 2 * block[1] + block[2]]
            return pltpu.make_async_remote_copy(
                src_ref=dst if src is None else src, dst_ref=dst, send_sem=send_sems.at[a, k],
                recv_sem=recv_sems.at[a, k], device_id=to, device_id_type=MESH)

        mine = [pltpu.make_async_copy(ins[a], outs[a].at[4 * x + 2 * y + c], local_sems.at[a]) for a in range(n)]
        for cp in mine:
            cp.start()
        first = []
        for a in range(n):
            first.append(copy(a, 0, me, sibling, src=ins[a]))
            first += [copy(a, 1 + j, me, (*chip, c), src=ins[a]) for j, chip in enumerate(chips)]
        for cp in first:
            cp.start()
        passed = []
        for j, chip in enumerate(chips):
            for a in range(n):
                copy(a, 1 + j, (*chip, c), me).wait_recv()
                fwd = copy(a, 4 + j, (*chip, c), sibling)
                fwd.start()
                passed.append(fwd)
        for a in range(n):
            copy(a, 0, sibling, me).wait_recv()
            for j, chip in enumerate(chips):
                copy(a, 4 + j, (*chip, 1 - c), me).wait_recv()
        for cp in first + passed:
            cp.wait_send()
        for cp in mine:
            cp.wait()

    return pl.pallas_call(
        body, name=name,
        out_shape=[SDS((N_DEV,) + s.shape, s.dtype) for s in shards],
        in_specs=[ANY] * (n + na), out_specs=[ANY] * n,
        scratch_shapes=[pltpu.SemaphoreType.DMA((n, 7)), pltpu.SemaphoreType.DMA((n, 7)),
                        pltpu.SemaphoreType.DMA((n,))],
    )(*shards, *after)


HBM_SPEC = pl.BlockSpec(memory_space=pltpu.HBM)
SEM_SPEC = pl.BlockSpec(memory_space=pltpu.SEMAPHORE)
EFFECT = pltpu.SideEffectType.DATAFLOW_SIDE_EFFECTING


def _ag_first_copies(ins, lands, send_sems, recv_sems):
    x, y, c = lax.axis_index("x"), lax.axis_index("y"), lax.axis_index("c")
    targets = [(x, y, 1 - c), (1 - x, y, c), (x, 1 - y, c), (1 - x, 1 - y, c)]
    return [pltpu.make_async_remote_copy(
        src_ref=ins[a], dst_ref=lands[a].at[4 * x + 2 * y + c], send_sem=send_sems.at[4 * a + k],
        recv_sem=recv_sems.at[4 * a + k], device_id=to, device_id_type=MESH)
        for a in range(len(ins)) for k, to in enumerate(targets)]


def _ag_forward_copies(ins, lands, send_sems, recv_sems):
    x, y, c = lax.axis_index("x"), lax.axis_index("y"), lax.axis_index("c")
    cps = []
    for a in range(len(lands)):
        for j, (px, py) in enumerate([(1 - x, y), (x, 1 - y), (1 - x, 1 - y)]):
            blk = lands[a].at[4 * px + 2 * py + c]
            cps.append(pltpu.make_async_remote_copy(
                src_ref=blk, dst_ref=blk, send_sem=send_sems.at[3 * a + j], recv_sem=recv_sems.at[3 * a + j],
                device_id=(x, y, 1 - c), device_id_type=MESH))
    return cps


def _rs_sibling_copies(ins, lands, send_sems, recv_sems):
    x, y, c = lax.axis_index("x"), lax.axis_index("y"), lax.axis_index("c")
    return [pltpu.make_async_remote_copy(
        src_ref=ins[a].at[2 * q + 1 - c], dst_ref=lands[a].at[q], send_sem=send_sems.at[4 * a + q],
        recv_sem=recv_sems.at[4 * a + q], device_id=(x, y, 1 - c), device_id_type=MESH)
        for a in range(len(ins)) for q in range(4)]


def _rs_chip_copies(ins, lands, send_sems, recv_sems):
    x, y, c = lax.axis_index("x"), lax.axis_index("y"), lax.axis_index("c")
    cps = []
    for a in range(len(ins)):
        for j, (px, py) in enumerate([(1 - x, y), (x, 1 - y), (1 - x, 1 - y)]):
            cps.append(pltpu.make_async_remote_copy(
                src_ref=ins[a].at[2 * px + py], dst_ref=lands[a].at[j], send_sem=send_sems.at[3 * a + j],
                recv_sem=recv_sems.at[3 * a + j], device_id=(px, py, c), device_id_type=MESH))
    return cps


def _async_start(name, build, sem_shape, ins, lands, after=()):
    arrays = [*ins, *lands]
    n_i, n_t, n_a = len(ins), len(arrays), len(after)

    def body(*refs):
        outs = refs[n_t + n_a:]
        for cp in build(refs[:n_i], refs[n_i:n_t], outs[0], outs[1]):
            cp.start()
        outs[-1][...] = jnp.zeros_like(outs[-1])

    res = pl.pallas_call(
        body, name=name,
        out_shape=(pltpu.SemaphoreType.DMA(sem_shape), pltpu.SemaphoreType.DMA(sem_shape),
                   *[pltpu.HBM(a.shape, a.dtype) for a in arrays], SDS((8, V7X_LANES), F32)),
        in_specs=[HBM_SPEC] * n_t + [ANY] * n_a,
        out_specs=(SEM_SPEC, SEM_SPEC, *[HBM_SPEC] * n_t, pl.BlockSpec(memory_space=pltpu.VMEM)),
        input_output_aliases={i: 2 + i for i in range(n_t)},
        compiler_params=pltpu.CompilerParams(has_side_effects=EFFECT),
    )(*[pltpu.with_memory_space_constraint(a, pltpu.HBM) for a in arrays], *after)
    return res[0], res[1], list(res[2:2 + n_t]), res[-1]


def _async_wait(name, build, handle, n_i, after=()):
    send_sems, recv_sems, arrays, _ = handle
    n_t, n_a = len(arrays), len(after)

    def body(*refs):
        for cp in build(refs[:n_i], refs[n_i:n_t], refs[n_t], refs[n_t + 1]):
            cp.wait_send()
            cp.wait_recv()

    res = pl.pallas_call(
        body, name=name, out_shape=tuple(pltpu.HBM(a.shape, a.dtype) for a in arrays),
        in_specs=[HBM_SPEC] * n_t + [SEM_SPEC, SEM_SPEC] + [ANY] * n_a, out_specs=tuple([HBM_SPEC] * n_t),
        input_output_aliases={i: i for i in range(n_t)},
        compiler_params=pltpu.CompilerParams(has_side_effects=EFFECT),
    )(*arrays, send_sems, recv_sems, *after)
    return list(res[:n_i]), list(res[n_i:])


def _mm_core(name, a, b, *, grid, a_spec, b_spec, dims, acc_shape, outs, out_spec, epilogue=None, extras=(),
             extra_specs=(), a_fn=None, after=(), carry=None):
    nk = grid[2]
    if carry is not None:
        after = (*after, carry)
    ne, no, na = len(extras), len(outs), len(after)

    def body(a_ref, b_ref, *rest):
        e_refs, o_refs, acc = rest[:ne], rest[ne + na:ne + na + no], rest[ne + na + no]
        k = pl.program_id(2)

        def product():
            lhs = a_ref[...] if a_fn is None else a_fn(a_ref[...])
            return lax.dot_general(lhs.astype(BF16), b_ref[...].astype(BF16), dims, preferred_element_type=F32)

        @pl.when(k == 0)
        def _():
            acc[...] = product()

        @pl.when(k > 0)
        def _():
            acc[...] += product()

        @pl.when(k == nk - 1)
        def _():
            r = acc[...]
            vals = epilogue(r, *[e[...] for e in e_refs]) if epilogue is not None else (r,)
            for o, v in zip(o_refs, vals):
                o[...] = v.astype(o.dtype)

    res = pl.pallas_call(
        body, name=name, grid=grid, out_shape=list(outs),
        in_specs=[a_spec, b_spec, *extra_specs, *[ANY] * na], out_specs=[out_spec] * no,
        scratch_shapes=[pltpu.VMEM(acc_shape, F32)],
        input_output_aliases={} if carry is None else {1 + ne + na: 0},
        compiler_params=_cp("parallel", "parallel", "arbitrary"),
    )(a, b, *extras, *after)
    return res


NN = (((1,), (0,)), ((), ()))
NT = (((1,), (1,)), ((), ()))
TN = (((0,), (0,)), ((), ()))


def _w_dims(w, kind):
    if kind == "col":
        return w.shape[1], N_DEV * w.shape[2], w.shape[1], w.shape[2]
    if kind == "row":
        return N_DEV * w.shape[1], w.shape[2], w.shape[1], w.shape[2]
    return w.shape[0], w.shape[1], w.shape[0], w.shape[1]


MM_VMEM_BUDGET = 36 * 1024 * 1024


def _row_and_k_blocks(m, tn, k_len, k_caps, a, out_dtypes, extras):
    per_out = sum(jnp.dtype(dt).itemsize for dt in out_dtypes) + sum(e.dtype.itemsize for e in extras)
    best = None
    for tm in (_pick(m, 2048), _pick(m, 1024), _pick(m, 512)):
        for tk in [_pick(k_len, cap) for cap in k_caps]:
            used = tm * tn * (4 + 2 * per_out) + 2 * (tm * tk * a.dtype.itemsize + tk * tn * 2)
            key = ((m // tm) * (k_len // tk), -tm)
            if used <= MM_VMEM_BUDGET and (best is None or key < best[0]):
                best = (key, tm, tk)
    assert best is not None, (m, tn, k_len)
    return best[1], best[2]


def _mm_fwd(name, a, w, kind, out_dtypes, epilogue=None, extras=(), a_fn=None, after=(), k_rows=None):
    if kind == "row":
        w, kind = w.reshape(-1, w.shape[-1]), "plain"
    m = a.shape[0]
    kk, n, ks, ns = _w_dims(w, kind)
    if k_rows is None:
        assert kk == a.shape[1]
    else:
        assert kind == "plain" and k_rows <= min(kk, a.shape[1])
        kk = k_rows
    tn = _pick(ns, 1024) if kind == "col" else _pick(n, 1152)
    tm, tk = _row_and_k_blocks(m, tn, kk, (1024, 512), a, out_dtypes, extras)
    if kind == "col":
        nb = ns // tn
        b_spec = pl.BlockSpec((None, tk, tn), lambda i, j, k: (j // nb, k, j % nb))
    else:
        b_spec = pl.BlockSpec((tk, tn), lambda i, j, k: (k, j))
    mn_spec = pl.BlockSpec((tm, tn), lambda i, j, k: (i, j))
    return _mm_core(
        name, a, w, grid=(m // tm, n // tn, kk // tk),
        a_spec=pl.BlockSpec((tm, tk), lambda i, j, k: (i, k)), b_spec=b_spec, dims=NN, acc_shape=(tm, tn),
        outs=[SDS((m, n), dt) for dt in out_dtypes], out_spec=mn_spec, epilogue=epilogue, extras=extras,
        extra_specs=[mn_spec] * len(extras), a_fn=a_fn, after=after)


def _mm_dx(name, dy, w, kind, out_dtypes, epilogue=None, extras=(), after=(), k_rows=None):
    if kind == "row":
        w, kind = w.reshape(-1, w.shape[-1]), "plain"
    m, n_dim = dy.shape
    kk, n, ks, ns = _w_dims(w, kind)
    assert n == n_dim
    if k_rows is not None:
        assert kind == "plain" and k_rows <= kk
        kk = k_rows
    tn = _pick(kk, 1024)
    if kind == "col":
        tm, tk = _row_and_k_blocks(m, tn, ns, (1024, 512), dy, out_dtypes, extras)
        kb = ns // tk
        b_spec = pl.BlockSpec((None, tn, tk), lambda i, j, k: (k // kb, j, k % kb))
    else:
        tm, tk = _row_and_k_blocks(m, tn, n, (1152, 512), dy, out_dtypes, extras)
        b_spec = pl.BlockSpec((tn, tk), lambda i, j, k: (j, k))
    mk_spec = pl.BlockSpec((tm, tn), lambda i, j, k: (i, j))
    return _mm_core(
        name, dy, w, grid=(m // tm, kk // tn, n // tk),
        a_spec=pl.BlockSpec((tm, tk), lambda i, j, k: (i, k)), b_spec=b_spec, dims=NT, acc_shape=(tm, tn),
        outs=[SDS((m, kk), dt) for dt in out_dtypes], out_spec=mk_spec, epilogue=epilogue, extras=extras,
        extra_specs=[mk_spec] * len(extras), after=after)


def _in_proj_dt(name, a, w_t, first, h_n):
    m, kk = a.shape
    tm, tk = _pick(m, 1024), _pick(kk, 1024)
    blk = first // h_n
    return _mm_core(
        name, a, w_t, grid=(m // tm, 1, kk // tk), a_spec=pl.BlockSpec((tm, tk), lambda i, j, k: (i, k)),
        b_spec=pl.BlockSpec((h_n, tk), lambda i, j, k: (blk, k)), dims=NT, acc_shape=(tm, h_n),
        outs=[SDS((m, h_n), F32)], out_spec=pl.BlockSpec((tm, h_n), lambda i, j, k: (i, 0)))[0]


def _in_proj_dt_dx(name, d_dt, w_t, first, res):
    m, h_n = d_dt.shape
    n = w_t.shape[1]
    tm, tn = _pick(m, 1024), _pick(n, 1024)
    blk = first // h_n
    mn_spec = pl.BlockSpec((tm, tn), lambda i, j, k: (i, j))
    return _mm_core(
        name, d_dt, w_t, grid=(m // tm, n // tn, 1), a_spec=pl.BlockSpec((tm, h_n), lambda i, j, k: (i, 0)),
        b_spec=pl.BlockSpec((h_n, tn), lambda i, j, k: (blk, j)), dims=NN, acc_shape=(tm, tn),
        outs=[SDS((m, n), F32)], out_spec=mn_spec, epilogue=_residual_epilogue, extras=(res,),
        extra_specs=[mn_spec])[0]


def _in_proj_dw_t(name, dz, d_dt, x, after=()):
    m, n = x.shape
    zw, h_n = dz.shape[1], d_dt.shape[1]
    tm, tn, tk = _pick(zw, 1024), _pick(n, 1024), _pick(m, 1024)
    out = SDS((zw + h_n, n), BF16)
    x_spec = pl.BlockSpec((tk, tn), lambda i, j, k: (k, j))
    main = _mm_core(
        name, dz, x, grid=(zw // tm, n // tn, m // tk), a_spec=pl.BlockSpec((tk, tm), lambda i, j, k: (k, i)),
        b_spec=x_spec, dims=TN, acc_shape=(tm, tn), outs=[out], out_spec=pl.BlockSpec((tm, tn), lambda i, j, k: (i, j)),
        after=after)[0]
    blk = zw // h_n
    return _mm_core(
        name + "_dt", d_dt, x, grid=(1, n // tn, m // tk), a_spec=pl.BlockSpec((tk, h_n), lambda i, j, k: (k, 0)),
        b_spec=x_spec, dims=TN, acc_shape=(h_n, tn), outs=[out],
        out_spec=pl.BlockSpec((h_n, tn), lambda i, j, k: (blk, j)), carry=main)[0]


def _mm_dw(name, a, dy, kind, a_fn=None, after=()):
    m, kk = a.shape
    n = dy.shape[1]
    tk = _pick(m, 1024)
    if kind == "col":
        ns = n // N_DEV
        tm, tn = _pick(kk, 1024), _pick(ns, 1024)
        nb = ns // tn
        out = SDS((N_DEV, kk, ns), BF16)
        out_spec = pl.BlockSpec((None, tm, tn), lambda i, j, k: (j // nb, i, j % nb))
    else:
        tm, tn = _pick(kk, 1024), _pick(n, 1152)
        out = SDS((kk, n), BF16)
        out_spec = pl.BlockSpec((tm, tn), lambda i, j, k: (i, j))
    res = _mm_core(
        name, a, dy, grid=(kk // tm, n // tn, m // tk),
        a_spec=pl.BlockSpec((tk, tm), lambda i, j, k: (k, i)), b_spec=pl.BlockSpec((tk, tn), lambda i, j, k: (k, j)),
        dims=TN, acc_shape=(tm, tn), outs=[out], out_spec=out_spec, a_fn=a_fn, after=after)[0]
    return res.reshape(N_DEV, kk // N_DEV, n) if kind == "row" else res


def _rowwise(name, fn, ins, outs, rows, tile):
    arrays, specs = [], []
    for arr, kind in ins:
        arrays.append(arr)
        if kind == "row":
            specs.append(pl.BlockSpec((tile, arr.shape[1]), lambda i: (i, 0)))
        elif kind == "vec":
            specs.append(pl.BlockSpec(arr.shape, lambda i, nd=arr.ndim: (0,) * nd))
        else:
            specs.append(kind)
    out_shapes, out_specs, kinds = [], [], []
    for cols, dt, kind in outs:
        kinds.append(kind)
        if kind == "row":
            out_shapes.append(SDS((rows, cols), dt))
            out_specs.append(pl.BlockSpec((tile, cols), lambda i: (i, 0)))
        else:
            out_shapes.append(SDS((1, cols), F32))
            out_specs.append(pl.BlockSpec((1, cols), lambda i: (0, 0)))
    ni = len(arrays)
    has_acc = "acc" in kinds

    def body(*refs):
        vals = fn(*[r[...] for r in refs[:ni]])
        i = pl.program_id(0)
        for o, v, kind in zip(refs[ni:], vals, kinds):
            if kind == "row":
                o[...] = v.astype(o.dtype)
            else:
                @pl.when(i == 0)
                def _(o=o):
                    o[...] = jnp.zeros_like(o)

                o[...] += v

    return pl.pallas_call(
        body, name=name, grid=(rows // tile,), out_shape=out_shapes, in_specs=specs, out_specs=out_specs,
        compiler_params=_cp("arbitrary" if has_acc else "parallel"),
    )(*arrays)


def _ln_fwd(name, u, g, b, after=()):
    d = u.shape[1]

    def fn(u, g, b, *unused):
        mu = jnp.mean(u, axis=1, keepdims=True)
        xc = u - mu
        var = jnp.mean(xc * xc, axis=1, keepdims=True)
        y = xc * lax.rsqrt(var + LN_EPS) * g + b
        return y, y

    ins = [(u, "row"), (g, "vec"), (b, "vec")] + [(t, "vec") for t in after]
    return _rowwise(name, fn, ins, [(d, F32, "row"), (d, BF16, "row")], u.shape[0], 256)


def _ln_bwd(name, u, dy, g):
    d = u.shape[1]

    def fn(u, dy, g):
        mu = jnp.mean(u, axis=1, keepdims=True)
        xc = u - mu
        var = jnp.mean(xc * xc, axis=1, keepdims=True)
        rstd = lax.rsqrt(var + LN_EPS)
        xhat = xc * rstd
        dxhat = dy * g
        m1 = jnp.mean(dxhat, axis=1, keepdims=True)
        m2 = jnp.mean(dxhat * xhat, axis=1, keepdims=True)
        du = rstd * (dxhat - m1 - xhat * m2)
        return du, du, jnp.sum(dy * xhat, axis=0, keepdims=True), jnp.sum(dy, axis=0, keepdims=True)

    return _rowwise(name, fn, [(u, "row"), (dy, "row"), (g, "vec")],
                    [(d, F32, "row"), (d, BF16, "row"), (d, F32, "acc"), (d, F32, "acc")], u.shape[0], 256)


def _loss_bwd(name, y, target):
    d = y.shape[1]

    def fn(y, t):
        e = y - t
        return e * (1.0 / d), jnp.sum(e * e, axis=0, keepdims=True) * (0.5 / d)

    return _rowwise(name, fn, [(y, "row"), (target, "row")], [(d, F32, "row"), (d, F32, "acc")], y.shape[0], 256)


def _ple_bwd(name, dx, e, gate):
    d = dx.shape[1]

    def fn(dx, e, gate):
        return dx * e * gate * (1.0 - gate), dx * gate

    return _rowwise(name, fn, [(dx, "row"), (e, "row"), (gate, "row")], [(d, BF16, "row"), (d, BF16, "row")],
                    dx.shape[0], 256)


def _sigmoid(v):
    return 1.0 / (1.0 + jnp.exp(-v))


def _gated_rms_fwd(name, y, zx, norm_w, after=()):
    di = y.shape[1]

    def fn(y, z, w, *unused):
        yg = y * (z * _sigmoid(z))
        r = lax.rsqrt(jnp.mean(yg * yg, axis=1, keepdims=True) + RMS_EPS)
        return (yg * r * w,)

    z_spec = pl.BlockSpec((128, di), lambda i: (i, 0))
    ins = [(y, "row"), (zx, z_spec), (norm_w, "vec")] + [(t, "vec") for t in after]
    return _rowwise(name, fn, ins, [(di, BF16, "row")], y.shape[0], 128)[0]


def _gated_rms_bwd(name, y, zx, norm_w, dout):
    di = y.shape[1]

    def fn(y, z, w, dout):
        sg = _sigmoid(z)
        sz = z * sg
        yg = y * sz
        r = lax.rsqrt(jnp.mean(yg * yg, axis=1, keepdims=True) + RMS_EPS)
        dn = dout * w
        dyg = r * (dn - yg * (r * r) * jnp.mean(dn * yg, axis=1, keepdims=True))
        dy = dyg * sz
        dz = dyg * y * (sg * (1.0 + z * (1.0 - sg)))
        return dy, dz, jnp.sum(dout * yg * r, axis=0, keepdims=True)

    z_spec = pl.BlockSpec((128, di), lambda i: (i, 0))
    return _rowwise(name, fn, [(y, "row"), (zx, z_spec), (norm_w, "vec"), (dout, "row")],
                    [(di, F32, "row"), (di, BF16, "row"), (di, F32, "acc")], y.shape[0], 128)


def _shift_down(v, j, row):
    return jnp.where(row >= j, pltpu.roll(v, j, 0), 0.0)


def _shift_up(v, j, row):
    t = v.shape[0]
    return jnp.where(row < t - j, pltpu.roll(v, t - j, 0), 0.0)


def _pool_select(parts, g):
    return jnp.where(g == 0, parts[0], jnp.where(g == 1, parts[1], jnp.where(g == 2, parts[2], parts[3])))


def _pool_windows(name, x, transpose, scale_by=None, after=()):
    t, d = x.shape
    cg = d // 4
    cw = V7X_LANES
    per = cg // cw

    def body(*refs):
        x_ref, o_ref = refs[0], refs[-1]
        g = pl.program_id(0) // per
        xv = x_ref[...]
        row = lax.broadcasted_iota(jnp.int32, (t, 1), 0)
        cnt = jnp.minimum(row + 1, jnp.left_shift(2, g)).astype(F32)
        s = xv / cnt if transpose else xv
        parts = []
        for lg in POOL_WINDOW_LOG2:
            j = 1 << (lg - 1)
            s = s + (_shift_up(s, j, row) if transpose else _shift_down(s, j, row))
            parts.append(s)
        sel = _pool_select(parts, g)
        if transpose:
            o_ref[...] = ALPHA * refs[1][...] + sel - xv
        else:
            o_ref[...] = (sel / cnt - xv).astype(o_ref.dtype)

    col = pl.BlockSpec((t, cw), lambda j: (0, j))
    ins = [x] if scale_by is None else [x, scale_by]
    return pl.pallas_call(
        body, name=name, grid=(d // cw,), out_shape=SDS((t, d), F32 if transpose else BF16),
        in_specs=[col] * len(ins) + [ANY] * len(after), out_specs=col, compiler_params=_cp("parallel"),
    )(*ins, *after)


def _pool_mm(name, pooled, w, scale, x):
    t, d = x.shape
    cg = d // 4
    tm = _pick(t, 1024)

    def body(p_ref, w_ref, s_ref, x_ref, u_ref, h_ref):
        h = jnp.dot(p_ref[...], w_ref[...], preferred_element_type=F32)
        h_ref[...] = h
        u_ref[...] = ALPHA * x_ref[...] + h * s_ref[...]

    blk = pl.BlockSpec((tm, cg), lambda g, i: (i, g))
    return pl.pallas_call(
        body, name=name, grid=(4, t // tm), out_shape=[SDS((t, d), F32), SDS((t, d), F32)],
        in_specs=[blk, pl.BlockSpec((None, cg, cg), lambda g, i: (g, 0, 0)), pl.BlockSpec((1, cg), lambda g, i: (0, g)),
                  blk],
        out_specs=[blk, blk], compiler_params=_cp("parallel", "parallel"),
    )(pooled, w, scale, x)


def _pool_bwd_mm(name, du, hraw, w, scale):
    t, d = du.shape
    cg = d // 4
    tm = _pick(t, 1024)

    def body(du_ref, h_ref, w_ref, s_ref, dh_ref, dp_ref, ds_ref):
        @pl.when(pl.program_id(1) == 0)
        def _():
            ds_ref[...] = jnp.zeros_like(ds_ref)

        duv = du_ref[...]
        ds_ref[...] += jnp.sum(duv * h_ref[...], axis=0, keepdims=True)
        dh = (duv * s_ref[...]).astype(BF16)
        dh_ref[...] = dh
        dp_ref[...] = lax.dot_general(dh, w_ref[...], NT, preferred_element_type=F32)

    blk = pl.BlockSpec((tm, cg), lambda g, i: (i, g))
    vec = pl.BlockSpec((1, cg), lambda g, i: (0, g))
    return pl.pallas_call(
        body, name=name, grid=(4, t // tm), out_shape=[SDS((t, d), BF16), SDS((t, d), F32), SDS((1, d), F32)],
        in_specs=[blk, blk, pl.BlockSpec((None, cg, cg), lambda g, i: (g, 0, 0)), vec],
        out_specs=[blk, blk, vec], compiler_params=_cp("parallel", "arbitrary"),
    )(du, hraw, w, scale)


def _pool_dw(name, pooled, dh):
    t, d = pooled.shape
    cg = d // 4
    tk = _pick(t, 512)
    nk = t // tk

    def body(p_ref, dh_ref, o_ref, acc):
        k = pl.program_id(1)

        @pl.when(k == 0)
        def _():
            acc[...] = jnp.zeros_like(acc)

        acc[...] += lax.dot_general(p_ref[...], dh_ref[...], TN, preferred_element_type=F32)

        @pl.when(k == nk - 1)
        def _():
            o_ref[...] = acc[...].astype(o_ref.dtype)

    blk = pl.BlockSpec((tk, cg), lambda g, k: (k, g))
    return pl.pallas_call(
        body, name=name, grid=(4, nk), out_shape=SDS((4, cg, cg), BF16), in_specs=[blk, blk],
        out_specs=pl.BlockSpec((None, cg, cg), lambda g, k: (g, 0, 0)), scratch_shapes=[pltpu.VMEM((cg, cg), F32)],
        compiler_params=_cp("parallel", "arbitrary"),
    )(pooled, dh)


def _conv_pre(u, w_ref, b_ref, row):
    pre = b_ref[...] + _shift_down(u, 3, row) * w_ref[0:1, :]
    pre = pre + _shift_down(u, 2, row) * w_ref[1:2, :]
    pre = pre + _shift_down(u, 1, row) * w_ref[2:3, :]
    return pre + u * w_ref[3:4, :]


def _conv_fwd(name, zx, conv_w, conv_b, di):
    t = zx.shape[0]
    cd = conv_w.shape[1]
    cw = _pick(cd, 256)
    off = di // cw

    def body(u_ref, w_ref, b_ref, o_ref):
        row = lax.broadcasted_iota(jnp.int32, (t, 1), 0)
        pre = _conv_pre(u_ref[...], w_ref, b_ref, row)
        o_ref[...] = pre * _sigmoid(pre)

    return pl.pallas_call(
        body, name=name, grid=(cd // cw,), out_shape=SDS((t, cd), F32),
        in_specs=[pl.BlockSpec((t, cw), lambda j: (0, off + j)), pl.BlockSpec((CONV_WIDTH, cw), lambda j: (0, j)),
                  pl.BlockSpec((1, cw), lambda j: (0, j))],
        out_specs=pl.BlockSpec((t, cw), lambda j: (0, j)), compiler_params=_cp("parallel"),
    )(zx, conv_w, conv_b)


def _conv_bwd(name, zx, conv_w, conv_b, dact, di, first):
    t, cd = dact.shape
    cw = _pick(cd, 256)
    off, woff = (di + first) // cw, first // cw

    def body(u_ref, w_ref, b_ref, da_ref, du_ref, dw_ref, db_ref):
        row = lax.broadcasted_iota(jnp.int32, (t, 1), 0)
        u = u_ref[...]
        pre = _conv_pre(u, w_ref, b_ref, row)
        sg = _sigmoid(pre)
        dpre = da_ref[...] * (sg * (1.0 + pre * (1.0 - sg)))
        du = dpre * w_ref[3:4, :]
        for j in (1, 2, 3):
            du = du + _shift_up(dpre, j, row) * w_ref[3 - j:4 - j, :]
            dw_ref[3 - j:4 - j, :] = jnp.sum(dpre * _shift_down(u, j, row), axis=0, keepdims=True)
        dw_ref[3:4, :] = jnp.sum(dpre * u, axis=0, keepdims=True)
        db_ref[...] = jnp.sum(dpre, axis=0, keepdims=True)
        du_ref[...] = du.astype(du_ref.dtype)

    wspec = pl.BlockSpec((CONV_WIDTH, cw), lambda j: (0, j))
    bspec = pl.BlockSpec((1, cw), lambda j: (0, j))
    ospec = pl.BlockSpec((t, cw), lambda j: (0, j))
    return pl.pallas_call(
        body, name=name, grid=(cd // cw,), out_shape=[SDS((t, cd), BF16), SDS((CONV_WIDTH, cd), F32), SDS((1, cd), F32)],
        in_specs=[pl.BlockSpec((t, cw), lambda j: (0, off + j)), pl.BlockSpec((CONV_WIDTH, cw), lambda j: (0, woff + j)),
                  pl.BlockSpec((1, cw), lambda j: (0, woff + j)), ospec],
        out_specs=[ospec, wspec, bspec], compiler_params=_cp("parallel"),
    )(zx, conv_w, conv_b, dact)


def _expand_heads(name, arrays, h_n, p):
    t = arrays[0].shape[0]
    n = len(arrays)
    w = _pick(h_n * p, 512)

    def body(*refs):
        j = pl.program_id(0)
        head = lax.broadcasted_iota(jnp.int32, (V7X_LANES, w), 0)
        lane = lax.broadcasted_iota(jnp.int32, (V7X_LANES, w), 1)
        spread = (head == j * (w // p) + lane // p).astype(BF16)
        for a_ref, o_ref in zip(refs[:n], refs[n:]):
            rest = a_ref[...]
            out = jnp.zeros((t, w), F32)
            for _ in range(3):
                piece = rest.astype(BF16)
                out = out + jnp.dot(piece, spread, preferred_element_type=F32)
                rest = rest - piece.astype(F32)
            o_ref[...] = out

    full = pl.BlockSpec((t, V7X_LANES), lambda j: (0, 0))
    return pl.pallas_call(
        body, name=name, grid=(h_n * p // w,), out_shape=[SDS((t, h_n * p), F32)] * n, in_specs=[full] * n,
        out_specs=[pl.BlockSpec((t, w), lambda j: (0, j))] * n, compiler_params=_cp("parallel"),
    )(*arrays)


def _softplus(v):
    return jnp.maximum(v, 0.0) + jnp.log(1.0 + jnp.exp(-jnp.abs(v)))


def _dt_fwd(name, zx, bias, a_log, col_block):
    t = zx.shape[0]

    def body(r_ref, b_ref, al_ref, dt_ref, acs_ref, ein_ref, eout_ref):
        row = lax.broadcasted_iota(jnp.int32, (t, 1), 0) % CHUNK
        dt = _softplus(r_ref[...] + b_ref[...])
        da = dt * (-jnp.exp(al_ref[...]))
        s, r = da, da
        j = 1
        while j < CHUNK:
            s = s + jnp.where(row >= j, pltpu.roll(s, j, 0), 0.0)
            r = r + jnp.where(row < CHUNK - j, pltpu.roll(r, t - j, 0), 0.0)
            j *= 2
        dt_ref[...] = dt
        acs_ref[...] = s
        ein_ref[...] = jnp.exp(s)
        eout_ref[...] = jnp.exp(r - da)

    vec = pl.BlockSpec((1, V7X_LANES), lambda i: (0, 0))
    full = pl.BlockSpec((t, V7X_LANES), lambda i: (0, 0))
    return pl.pallas_call(
        body, name=name, grid=(1,), out_shape=[SDS((t, V7X_LANES), F32)] * 4,
        in_specs=[pl.BlockSpec((t, V7X_LANES), lambda i: (0, col_block)), vec, vec], out_specs=[full] * 4,
        compiler_params=_cp("arbitrary"),
    )(zx, bias, a_log)


def _dt_bwd(name, zx, bias, a_log, d_acs, d_dt, col_block):
    t = zx.shape[0]

    def body(r_ref, b_ref, al_ref, da_ref, dd_ref, draw_ref, db_ref, dal_ref):
        row = lax.broadcasted_iota(jnp.int32, (t, 1), 0) % CHUNK
        pre = r_ref[...] + b_ref[...]
        dt = _softplus(pre)
        a = -jnp.exp(al_ref[...])
        s = da_ref[...]
        j = 1
        while j < CHUNK:
            s = s + jnp.where(row < CHUNK - j, pltpu.roll(s, t - j, 0), 0.0)
            j *= 2
        ddt = dd_ref[...] + s * a
        dal_ref[...] = jnp.sum(s * dt, axis=0, keepdims=True) * a
        draw = ddt * _sigmoid(pre)
        db_ref[...] = jnp.sum(draw, axis=0, keepdims=True)
        draw_ref[...] = draw.astype(draw_ref.dtype)

    vec = pl.BlockSpec((1, V7X_LANES), lambda i: (0, 0))
    full = pl.BlockSpec((t, V7X_LANES), lambda i: (0, 0))
    return pl.pallas_call(
        body, name=name, grid=(1,), out_shape=[SDS((t, V7X_LANES), BF16), SDS((1, V7X_LANES), F32), SDS((1, V7X_LANES), F32)],
        in_specs=[pl.BlockSpec((t, V7X_LANES), lambda i: (0, col_block)), vec, vec, full, full],
        out_specs=[full, vec, vec], compiler_params=_cp("arbitrary"),
    )(zx, bias, a_log, d_acs, d_dt)


def _ssd_specs(t, di, g_n, hpg, p, rev):
    rows = SSD_STEP_CHUNKS * CHUNK
    nc = t // rows
    w = hpg * p
    nb = di // D_STATE

    def cc(c):
        return nc - 1 - c if rev else c

    return dict(
        xs=pl.BlockSpec((rows, w), lambda g, c: (cc(c), g)),
        bm=pl.BlockSpec((rows, D_STATE), lambda g, c: (cc(c), nb + g)),
        cm=pl.BlockSpec((rows, D_STATE), lambda g, c: (cc(c), nb + g_n + g)),
        col=pl.BlockSpec((None, rows, hpg), lambda g, c: (g, cc(c), 0)),
        rowv=pl.BlockSpec((None, hpg, rows), lambda g, c: (g, 0, cc(c))),
        head=pl.BlockSpec((None, 1, hpg), lambda g, c: (g, 0, 0)),
        lanes=pl.BlockSpec((1, w), lambda g, c: (0, g)),
        bc=pl.BlockSpec((rows, D_STATE), lambda g, c: (cc(c), g)),
        prev=pl.BlockSpec((SSD_STEP_CHUNKS, None, D_STATE, w), lambda g, c: (cc(c), g, 0, 0)),
        seg=pl.BlockSpec((w, V7X_LANES), lambda g, c: (0, 0)),
    )


def _decay_masks(cb, ac, ar, heads):
    li = lax.broadcasted_iota(jnp.int32, (CHUNK, CHUNK), 0)
    si = lax.broadcasted_iota(jnp.int32, (CHUNK, CHUNK), 1)
    lms = [jnp.exp(jnp.where(li >= si, ac[:, hh:hh + 1] - ar[hh:hh + 1, :], -jnp.inf)) for hh in heads]
    return lms, [(cb * lm).astype(BF16) for lm in lms]


def _ssd_fwd(name, xbc, dt_x, ein_x, eout_x, a_col, a_row, d_x, di, g_n, hpg, p):
    t = xbc.shape[0]
    nc = t // CHUNK
    w = hpg * p
    assert 2 * p == V7X_LANES and hpg % 2 == 0 and nc % SSD_STEP_CHUNKS == 0
    sp = _ssd_specs(t, di, g_n, hpg, p, False)

    def body(xs_ref, bm_ref, cm_ref, dt_ref, ein_ref, eout_ref, ac_ref, ar_ref, d_ref, y_ref, prev_ref, h_ref):
        @pl.when(pl.program_id(1) == 0)
        def _():
            h_ref[...] = jnp.zeros_like(h_ref)

        first = lax.broadcasted_iota(jnp.int32, (1, V7X_LANES), 1) < p
        for sub in range(SSD_STEP_CHUNKS):
            r = slice(sub * CHUNK, (sub + 1) * CHUNK)
            bm = bm_ref[r, :].astype(BF16)
            cm = cm_ref[r, :].astype(BF16)
            cb = lax.dot_general(cm, bm, NT, preferred_element_type=F32)
            xs = xs_ref[r, :]
            e_in = ein_ref[r, :]
            xdt = xs * dt_ref[r, :]
            ac, ar = ac_ref[r, :], ar_ref[:, r]
            ys = []
            for pr in range(hpg // 2):
                _, ms = _decay_masks(cb, ac, ar, (2 * pr, 2 * pr + 1))
                xp = xdt[:, pr * V7X_LANES:(pr + 1) * V7X_LANES]
                rhs = jnp.concatenate([jnp.where(first, xp, 0.0), jnp.where(first, 0.0, xp)], axis=0).astype(BF16)
                ys.append(jnp.dot(jnp.concatenate(ms, axis=1), rhs, preferred_element_type=F32))
            h_prev = h_ref[...]
            prev_ref[sub] = h_prev
            y = jnp.concatenate(ys, axis=1) + jnp.dot(cm, h_prev.astype(BF16), preferred_element_type=F32) * e_in
            y_ref[r, :] = y + xs * d_ref[...]
            st = lax.dot_general(bm, (xdt * eout_ref[r, :]).astype(BF16), TN, preferred_element_type=F32)
            h_ref[...] = e_in[CHUNK - 1:CHUNK, :] * h_prev + st

    return pl.pallas_call(
        body, name=name, grid=(g_n, nc // SSD_STEP_CHUNKS),
        out_shape=[SDS((t, di), F32), SDS((nc, g_n, D_STATE, w), F32)],
        in_specs=[sp["xs"], sp["bm"], sp["cm"], sp["xs"], sp["xs"], sp["xs"], sp["col"], sp["rowv"], sp["lanes"]],
        out_specs=[sp["xs"], sp["prev"]], scratch_shapes=[pltpu.VMEM((D_STATE, w), F32)],
        compiler_params=_cp("parallel", "arbitrary"),
    )(xbc, xbc, xbc, dt_x, ein_x, eout_x, a_col, a_row, d_x)


def _head_sums(v, seg):
    hi = v.astype(BF16)
    lo = (v - hi.astype(F32)).astype(BF16)
    return jnp.dot(hi, seg, preferred_element_type=F32) + jnp.dot(lo, seg, preferred_element_type=F32)


def _head_totals(v, seg):
    part = v[0:8]
    for r in range(8, v.shape[0], 8):
        part = part + v[r:r + 8]
    return jnp.sum(_head_sums(part, seg), axis=0, keepdims=True)


def _ssd_bwd(name, xbc, dt_x, ein_x, eout_x, a_col, a_row, d_x, prev, dy, di, g_n, hpg, p):
    t = xbc.shape[0]
    nc = t // CHUNK
    w = hpg * p
    sp = _ssd_specs(t, di, g_n, hpg, p, True)
    seg = (lax.broadcasted_iota(jnp.int32, (w, V7X_LANES), 0) // p
           == lax.broadcasted_iota(jnp.int32, (w, V7X_LANES), 1)).astype(BF16)

    def body(xs_ref, bm_ref, cm_ref, dt_ref, ein_ref, eout_ref, ac_ref, ar_ref, d_ref, prev_ref, dy_ref,
             seg_ref, dx_ref, dbm_ref, dcm_ref, ddt_ref, dacs_ref, dd_ref, dh_ref):
        @pl.when(pl.program_id(1) == 0)
        def _():
            dh_ref[...] = jnp.zeros_like(dh_ref)
            dd_ref[...] = jnp.zeros_like(dd_ref)

        first = lax.broadcasted_iota(jnp.int32, (1, V7X_LANES), 1) < p
        last_row = lax.broadcasted_iota(jnp.int32, (CHUNK, 1), 0) == CHUNK - 1
        seg_m = seg_ref[...]
        d_skip = d_ref[...]
        for sub in reversed(range(SSD_STEP_CHUNKS)):
            r = slice(sub * CHUNK, (sub + 1) * CHUNK)
            bm = bm_ref[r, :].astype(BF16)
            cm = cm_ref[r, :].astype(BF16)
            cb = lax.dot_general(cm, bm, NT, preferred_element_type=F32)
            xs, dy, e_in, e_out, dt_l = xs_ref[r, :], dy_ref[r, :], ein_ref[r, :], eout_ref[r, :], dt_ref[r, :]
            ac, ar = ac_ref[r, :], ar_ref[:, r]
            xdt = xs * dt_l
            h_prev = prev_ref[sub]
            h_prev_b = h_prev.astype(BF16)
            dh_next = dh_ref[...]
            dh_next_b = dh_next.astype(BF16)
            dy_e = (dy * e_in).astype(BF16)
            d_cm = lax.dot_general(dy_e, h_prev_b, NT, preferred_element_type=F32)
            dh_ref[...] = (e_in[CHUNK - 1:CHUNK, :] * dh_next
                           + lax.dot_general(cm, dy_e, TN, preferred_element_type=F32))
            q = jnp.dot(bm, dh_next_b, preferred_element_type=F32)
            xf = xdt * e_out
            d_bm = lax.dot_general(xf.astype(BF16), dh_next_b, NT, preferred_element_type=F32)
            d_cb = jnp.zeros((CHUNK, CHUNK), F32)
            parts, w_parts = [], []
            for pr in range(hpg // 2):
                lanes = slice(pr * V7X_LANES, (pr + 1) * V7X_LANES)
                lms, ms = _decay_masks(cb, ac, ar, (2 * pr, 2 * pr + 1))
                xp = xdt[:, lanes]
                xp_b = xp.astype(BF16)
                dyp = dy[:, lanes]
                halves = [jnp.where(first, dyp, 0.0).astype(BF16), jnp.where(first, 0.0, dyp).astype(BF16)]
                for lm, half in zip(lms, halves):
                    d_cb = d_cb + lax.dot_general(half, xp_b, NT, preferred_element_type=F32) * lm
                dxd = lax.dot_general(jnp.concatenate(ms, axis=0), jnp.concatenate(halves, axis=0), TN,
                                      preferred_element_type=F32)
                stacked = jnp.concatenate([jnp.where(first, xp, 0.0), jnp.where(first, 0.0, xp)], axis=0).astype(BF16)
                y_diag = jnp.dot(jnp.concatenate(ms, axis=1), stacked, preferred_element_type=F32)
                parts.append(dxd)
                w_parts.append(dyp.astype(BF16).astype(F32) * y_diag - xp_b.astype(F32) * dxd)
            d_xdt = jnp.concatenate(parts, axis=1) + q * e_out
            dx_ref[r, :] = d_xdt * dt_l + dy * d_skip
            ch = jnp.dot(cm, h_prev_b, preferred_element_type=F32)
            qx = q * xf
            s_a = _head_sums(dy * ch * e_in - qx + jnp.concatenate(w_parts, axis=1), seg_m)[:, :hpg]
            d_last = (_head_totals(qx, seg_m)[:, :hpg]
                      + jnp.exp(ac[CHUNK - 1:CHUNK, :]) * _head_totals(dh_next * h_prev, seg_m)[:, :hpg])
            ddt_ref[r, :] = _head_sums(d_xdt * xs, seg_m)[:, :hpg]
            dacs_ref[r, :] = s_a + jnp.where(last_row, d_last, 0.0)
            dd_ref[...] += _head_totals(dy * xs, seg_m)[:, :hpg]
            d_cb_b = d_cb.astype(BF16)
            dcm_ref[r, :] = d_cm + jnp.dot(d_cb_b, bm, preferred_element_type=F32)
            dbm_ref[r, :] = d_bm + lax.dot_general(d_cb_b, cm, TN, preferred_element_type=F32)

    gn = g_n * D_STATE
    return pl.pallas_call(
        body, name=name, grid=(g_n, nc // SSD_STEP_CHUNKS),
        out_shape=[SDS((t, di), F32), SDS((t, gn), F32), SDS((t, gn), F32), SDS((g_n, t, hpg), F32),
                   SDS((g_n, t, hpg), F32), SDS((g_n, 1, hpg), F32)],
        in_specs=[sp["xs"], sp["bm"], sp["cm"], sp["xs"], sp["xs"], sp["xs"], sp["col"], sp["rowv"],
                  sp["lanes"], sp["prev"], sp["xs"], sp["seg"]],
        out_specs=[sp["xs"], sp["bc"], sp["bc"], sp["col"], sp["col"], sp["head"]],
        scratch_shapes=[pltpu.VMEM((D_STATE, w), F32)],
        compiler_params=_cp("parallel", "arbitrary"),
    )(xbc, xbc, xbc, dt_x, ein_x, eout_x, a_col, a_row, d_x, prev, dy, seg)


def _as3d(a):
    return a.reshape(a.shape[0], -1, a.shape[-1])


def _pair_sum(name, own, recv, core):
    shape = recv.shape
    cols = shape[-1]
    own3, recv3 = own.reshape(8, -1, cols), recv.reshape(4, -1, cols)
    rows = recv3.shape[1]
    tr = _row_tile(rows, cols, 2)

    def body(c_ref, a_ref, b_ref, o_ref):
        o_ref[...] = (a_ref[...].astype(F32) + b_ref[...].astype(F32)).astype(o_ref.dtype)

    blk = pl.BlockSpec((None, tr, cols), lambda q, i, c_ref: (q, i, 0))
    out = pl.pallas_call(
        body, name=name, out_shape=SDS(recv3.shape, recv.dtype),
        grid_spec=pltpu.PrefetchScalarGridSpec(
            num_scalar_prefetch=1, grid=(4, rows // tr),
            in_specs=[pl.BlockSpec((None, tr, cols), lambda q, i, c_ref: (2 * q + c_ref[0], i, 0)), blk], out_specs=blk),
        compiler_params=_cp("parallel", "parallel"),
    )(core, own3, recv3)
    return out.reshape(shape)


def _adamw(name, w, m, v, parts, layer, prev=None, sel=None):
    lyr, rows, cols = w.shape
    n = len(parts)
    by_rows = rows % 16 == 0
    tr, tc = (_row_tile(rows, cols), cols) if by_rows else (rows, _pick(cols, 256))
    np_ = 0 if prev is None else 4
    if sel is None:
        sel = jnp.zeros((1,), jnp.int32)

    def body(sel_ref, *refs):
        w_ref, m_ref, v_ref = refs[:3]
        p_refs = refs[3:3 + n]
        g_ref, d_ref, nm_ref, nv_ref = refs[3 + n + np_:]
        g = p_refs[0][...].astype(F32)
        for r in p_refs[1:]:
            g = g + r[...].astype(F32)
        nm = ADAM_B1 * m_ref[...] + (1.0 - ADAM_B1) * g
        nv = ADAM_B2 * v_ref[...] + (1.0 - ADAM_B2) * (g * g)
        m_hat = nm / (1.0 - ADAM_B1 ** ADAM_STEP)
        v_hat = nv / (1.0 - ADAM_B2 ** ADAM_STEP)
        g_ref[...] = g
        d_ref[...] = -ADAM_LR * (m_hat / (jnp.sqrt(v_hat) + ADAM_EPS) + ADAM_WD * w_ref[...])
        nm_ref[...] = nm
        nv_ref[...] = nv

    def at(lead):
        return pl.BlockSpec((None, tr, tc), lambda i, s: (lead(s), i, 0) if by_rows else (lead(s), 0, i))

    lspec = at(lambda s: layer)
    pspecs = [at(lambda s: s[0]) if q is None else at(lambda s, q=q: q) for _, q in parts]
    aliases = {} if prev is None else {4 + n + q: q for q in range(4)}
    return pl.pallas_call(
        body, name=name, out_shape=[SDS(w.shape, F32)] * 4,
        grid_spec=pltpu.PrefetchScalarGridSpec(
            num_scalar_prefetch=1, grid=(rows // tr if by_rows else cols // tc,),
            in_specs=[lspec] * 3 + pspecs + [ANY] * np_, out_specs=[lspec] * 4),
        input_output_aliases=aliases, compiler_params=_cp("parallel"),
    )(sel, w, m, v, *[arr for arr, _ in parts], *(prev or ()))


def _sum8(name, parts):
    rows = parts.shape[1]

    def body(p_ref, o_ref):
        s = p_ref[0]
        for q in range(1, N_DEV):
            s = s + p_ref[q]
        o_ref[...] = s

    return pl.pallas_call(
        body, name=name, grid=(1,), out_shape=SDS((rows, V7X_LANES), F32),
        in_specs=[pl.BlockSpec((N_DEV, rows, V7X_LANES), lambda i: (0, 0, 0))],
        out_specs=pl.BlockSpec((rows, V7X_LANES), lambda i: (0, 0)), compiler_params=_cp("arbitrary"),
    )(parts)


def _pack(vectors, align):
    flat = jnp.concatenate([v.reshape(-1) for v in vectors])
    pad = (-flat.shape[0]) % align
    if pad:
        flat = jnp.concatenate([flat, jnp.zeros((pad,), F32)])
    return flat.reshape(-1, V7X_LANES)


def _unpack(packed, shapes):
    flat = packed.reshape(-1)
    out, o = [], 0
    for s in shapes:
        size = 1
        for dim in s:
            size *= dim
        out.append(flat[o:o + size].reshape(s))
        o += size
    return out


def _residual_epilogue(acc, res):
    return (ALPHA * res + acc,)


def _plain_add_epilogue(acc, res):
    return (res + acc,)


def _gate_epilogue(acc, y, e):
    gate = _sigmoid(acc)
    xn = y + gate * e
    return xn, gate, xn


def _relu2(pre):
    r = jnp.maximum(pre, 0.0)
    return r * r


def _relu2_bwd_epilogue(acc, pre):
    return (acc * (2.0 * jnp.maximum(pre.astype(F32), 0.0)),)


def _tail_fwd(tag, u_a, wts, lng, lnb, p_l, finish_w1, finish_w2, after=()):
    y1, y1_b = _ln_fwd(f"ln1_{tag}", u_a, lng[0], lnb[0], after)
    finish_w1(y1_b)
    (pre,) = _mm_fwd(f"mlp1_{tag}", y1_b, wts["w1"], "col", [BF16])
    finish_w2(pre)
    (u_b,) = _mm_fwd(f"mlp2_{tag}", pre, wts["w2"], "row", [F32], _residual_epilogue, (y1,), a_fn=_relu2)
    y2, y2_b = _ln_fwd(f"ln2_{tag}", u_b, lng[1], lnb[1])
    (e,) = _mm_fwd(f"ple_{tag}", p_l, wts["plew"], "col", [F32])
    xn, gate, xn_b = _mm_fwd(f"gate_{tag}", y2_b, wts["gate"], "row", [F32, F32, BF16], _gate_epilogue, (y2, e))
    return xn, xn_b, (u_a, y1_b, pre, u_b, y2_b, e, gate)


def _tail_bwd(tag, dxn, saved, wts, lng, p_l, emit, advance, toks):
    u_a, y1_b, pre, u_b, y2_b, e, gate = saved
    dgpre, de = _ple_bwd(f"ple_bwd_{tag}", dxn, e, gate)
    toks = emit(f"{tag}_ple", dict(gate=_mm_dw(f"gate_dw_{tag}", y2_b, dgpre, "row", after=toks),
                                   plew=_mm_dw(f"ple_dw_{tag}", p_l, de, "col")))
    (dy2,) = _mm_dx(f"gate_dx_{tag}", dgpre, wts["gate"], "row", [F32], _plain_add_epilogue, (dxn,), after=toks)
    toks = advance((dy2,))
    du_b, du_b16, dg2, db2 = _ln_bwd(f"ln2_bwd_{tag}", u_b, dy2, lng[1])
    toks = emit(f"{tag}_w2", dict(w2=_mm_dw(f"mlp2_dw_{tag}", pre, du_b16, "row", a_fn=_relu2, after=toks)))
    (dpre,) = _mm_dx(f"mlp2_dx_{tag}", du_b16, wts["w2"], "row", [BF16], _relu2_bwd_epilogue, (pre,), after=toks)
    toks = advance((dpre,))
    toks = emit(f"{tag}_w1", dict(w1=_mm_dw(f"mlp1_dw_{tag}", y1_b, dpre, "col", after=toks)))
    (dy1,) = _mm_dx(f"mlp1_dx_{tag}", dpre, wts["w1"], "col", [F32], _residual_epilogue, (du_b,), after=toks)
    toks = advance((dy1,))
    du_a, du_a16, dg1, db1 = _ln_bwd(f"ln1_bwd_{tag}", u_a, dy1, lng[0])
    return du_a, du_a16, [dg1, dg2], [db1, db2], toks


def _to_slots(a, axis):
    shape = a.shape
    per = shape[axis] // N_DEV
    v = a.reshape(shape[:axis] + (N_DEV, per) + shape[axis + 1:])
    return jnp.moveaxis(v, axis, 0)


def _pad_lanes(a):
    return jnp.pad(a, [(0, 0)] * (a.ndim - 1) + [(0, V7X_LANES - a.shape[-1])])


def kernel(x, p, pool_w, pool_scale, ssm_in_w, ssm_conv_w, ssm_conv_b, ssm_dt_bias, ssm_a_log, ssm_d, ssm_norm_w, ssm_out_w, mlp_w1, mlp_w2, ln_g, ln_b, ple_w, ple_gate_w, loss_target, m_pool_w, m_pool_scale, m_ssm_in_w, m_ssm_conv_w, m_ssm_conv_b, m_ssm_dt_bias, m_ssm_a_log, m_ssm_d, m_ssm_norm_w, m_ssm_out_w, m_mlp_w1, m_mlp_w2, m_ln_g, m_ln_b, m_ple_w, m_ple_gate_w, v_pool_w, v_pool_scale, v_ssm_in_w, v_ssm_conv_w, v_ssm_conv_b, v_ssm_dt_bias, v_ssm_a_log, v_ssm_d, v_ssm_norm_w, v_ssm_out_w, v_mlp_w1, v_mlp_w2, v_ln_g, v_ln_b, v_ple_w, v_ple_gate_w):
    t, d = x.shape[1:]
    h_n = ssm_dt_bias.shape[-1]
    di_s, cd_s, dp_s, d_s = ssm_norm_w.shape[-1], ssm_conv_b.shape[-1], ssm_in_w.shape[-1], ln_g.shape[-1]
    di, cd, dp = N_DEV * di_s, N_DEV * cd_s, N_DEV * dp_s
    p_dim = di // h_n
    g_n = (cd - di) // (2 * D_STATE)
    hpg = h_n // g_n
    zw = di + cd
    assert h_n <= V7X_LANES and dp == zw + h_n and zw % V7X_LANES == 0 and zw % h_n == 0
    cg = d // 4
    me = 4 * lax.axis_index("x") + 2 * lax.axis_index("y") + lax.axis_index("c")

    x0, target = x[0], loss_target[0]
    p_l = [p[0, 0].astype(BF16), p[1, 0].astype(BF16)]

    small_shapes = [(CONV_WIDTH, cd_s), (1, cd_s), (1, di_s), (2, 2, d_s), (2, 2, d_s)]
    small = _pack([ssm_conv_w[0], ssm_conv_b, ssm_norm_w, ln_g, ln_b], 8 * V7X_LANES)
    first = [w.astype(BF16) for w in (pool_w[0], mlp_w1[0])]

    def gather_start(tag, own, after):
        lands = [lax.empty((N_DEV,) + w.shape, w.dtype) for w in own]
        return _async_start(f"ag_{tag}_start", _ag_first_copies, (4 * len(own),), own, lands, after)

    def gather_pass(tag, handle, after):
        n = len(handle[2]) // 2
        own, lands = _async_wait(f"ag_{tag}_wait", _ag_first_copies, handle, n, after)
        return _async_start(f"ag_{tag}_forward_start", _ag_forward_copies, (3 * n,), [], lands), own

    def gather_done(tag, passed, after=()):
        fwd, own = passed
        _, lands = _async_wait(f"ag_{tag}_forward_wait", _ag_forward_copies, fwd, 0, after)
        return [lax.dynamic_update_slice_in_dim(g, w[None], me, 0) for g, w in zip(lands, own)]

    def gather_finish(tag, handle, after):
        return gather_done(tag, gather_pass(tag, handle, after))

    ag_first = gather_start("first", first + [small], ())
    zero = ag_first[3][0, 0]
    own_l0 = [(w + zero).astype(BF16) for w in (mlp_w2[0], ple_w[0], ple_gate_w[0])]
    own_ssm = [(ssm_in_w[0] + zero).astype(BF16).T]
    own_out = [(ssm_out_w[0] + zero).astype(BF16)]
    own_mlp = [(w + zero).astype(BF16) for w in (mlp_w1[1], mlp_w2[1], ple_w[1], ple_gate_w[1])]
    ag_l0 = gather_start("l0", own_l0, (ag_first[3],))
    ag_ssm = gather_start("ssm", own_ssm, (ag_l0[3],))
    ag_out = gather_start("out", own_out, (ag_ssm[3],))
    ag_mlp = gather_start("mlp1", own_mlp, (ag_out[3],))
    pooled = _pool_windows("pool_fwd", x0, False, after=(ag_mlp[3],))
    pool_g, w1_0, small_g = gather_finish("first", ag_first, (pooled,))
    pool_full = pool_g.transpose(1, 0, 2, 3).reshape(4, cg, cg)
    sm = small_g.reshape(N_DEV, -1)
    o = 0
    parts = []
    for shp in small_shapes:
        size = 1
        for s in shp:
            size *= s
        parts.append(sm[:, o:o + size].reshape((N_DEV,) + shp))
        o += size
    conv_w_full = parts[0].transpose(1, 0, 2).reshape(CONV_WIDTH, cd)
    conv_b_full = parts[1].transpose(1, 0, 2).reshape(1, cd)
    norm_w_full = parts[2].transpose(1, 0, 2).reshape(1, di)
    ln_g_full = parts[3].transpose(1, 2, 0, 3).reshape(2, 2, 1, d)
    ln_b_full = parts[4].transpose(1, 2, 0, 3).reshape(2, 2, 1, d)
    bias128, alog128 = _pad_lanes(ssm_dt_bias), _pad_lanes(ssm_a_log)

    u0, hraw = _pool_mm("pool_mm", pooled, pool_full, pool_scale, x0)
    wts = [dict(w1=w1_0)]

    def finish_l0(after):
        w2_0, plew_0, gate_0 = gather_finish("l0", ag_l0, (after,))
        wts[0].update(w2=w2_0, plew=plew_0, gate=gate_0)

    x1, x1_b, saved0 = _tail_fwd("l0", u0, wts[0], ln_g_full[0], ln_b_full[0], p_l[0], lambda after: None, finish_l0)

    (in_g,) = gather_finish("ssm", ag_ssm, (x1,))
    in_t = in_g.reshape(dp, d)
    (zx,) = _mm_dx("in_proj", x1_b, in_t, "plain", [F32], k_rows=zw)
    dt_raw = _pad_lanes(_in_proj_dt("in_proj_dt", x1_b, in_t, zw, h_n))
    xbc = _conv_fwd("conv_fwd", zx, conv_w_full, conv_b_full, di)
    dt, acs, e_in, e_out = _dt_fwd("dt_fwd", dt_raw, bias128, alog128, 0)

    dt_x, ein_x, eout_x = _expand_heads("expand_heads", [dt, e_in, e_out], h_n, p_dim)
    d_x = jnp.repeat(ssm_d, p_dim, axis=1)

    def to_col(a):
        return a[:, :h_n].reshape(t, g_n, hpg).transpose(1, 0, 2)

    def to_row(a):
        return a[:, :h_n].reshape(t, g_n, hpg).transpose(1, 2, 0)

    def from_col(a):
        return _pad_lanes(a.transpose(1, 0, 2).reshape(t, h_n))

    a_col, a_row = to_col(acs), to_row(acs)
    y_ssd, prev = _ssd_fwd("ssd_fwd", xbc, dt_x, ein_x, eout_x, a_col, a_row, d_x, di, g_n, hpg, p_dim)
    out_passed = gather_pass("out", ag_out, (y_ssd,))
    yn = _gated_rms_fwd("gated_rms_fwd", y_ssd, zx, norm_w_full, after=(out_passed[0][3],))
    (out_g,) = gather_done("out", out_passed, (yn,))
    (u2,) = _mm_fwd("out_proj", yn, out_g, "row", [F32], _residual_epilogue, (x1,))
    mlp_passed = gather_pass("mlp1", ag_mlp, (u2,))
    wts.append({})

    def finish_l1(after):
        wts[1].update(zip(("w1", "w2", "plew", "gate"), gather_done("mlp1", mlp_passed, (after,))))

    x2, _, saved1 = _tail_fwd("l1", u2, wts[1], ln_g_full[1], ln_b_full[1], p_l[1], finish_l1, lambda after: None,
                              after=(mlp_passed[0][3],))

    core = lax.axis_index("c").astype(jnp.int32).reshape(1)
    chip = (2 * lax.axis_index("x") + lax.axis_index("y")).astype(jnp.int32).reshape(1)
    scattering = {}
    pending = []

    def to_chips(after):
        if not pending:
            return []
        tag, names, handle = pending.pop()
        own, halves = _async_wait(f"rs_{tag}_sibling_wait", _rs_sibling_copies, handle, len(names), after)
        sums = [_pair_sum(f"rs_{tag}_pair_sum_{n}", g, hv, core) for n, g, hv in zip(names, own, halves)]
        lands = [lax.empty((3,) + s.shape[1:], s.dtype) for s in sums]
        handle = _async_start(f"rs_{tag}_start", _rs_chip_copies, (3 * len(sums),), sums, lands)
        scattering[tag] = (names, handle)
        return [handle[3]]

    def emit(tag, grads):
        names, arrays = list(grads), list(grads.values())
        toks = to_chips(tuple(arrays))
        lands = [lax.empty((4,) + g.shape[1:], g.dtype) for g in arrays]
        handle = _async_start(f"rs_{tag}_sibling_start", _rs_sibling_copies, (4 * len(arrays),), arrays, lands,
                              tuple(toks))
        pending.append((tag, names, handle))
        return toks + [handle[3]]

    def collect(tag, after):
        names, handle = scattering.pop(tag)
        sums, thirds = _async_wait(f"rs_{tag}_wait", _rs_chip_copies, handle, len(names), after)
        return {n: (s.reshape(4, -1, s.shape[-1]), r.reshape(3, -1, r.shape[-1])) for n, s, r in zip(names, sums, thirds)}

    dx2, loss_cols = _loss_bwd("loss", x2, target)
    du2, du2_b, dg_1, db_1, toks = _tail_bwd("l1", dx2, saved1, wts[1], ln_g_full[1], p_l[1], emit, to_chips, [])
    toks = emit("ssm_out", dict(out=_mm_dw("out_proj_dw", yn, du2_b, "row", after=toks)))
    (dyn,) = _mm_dx("out_proj_dx", du2_b, out_g, "row", [F32], after=toks)
    toks = to_chips((dyn,))
    dy_ssd, dz, d_norm_w = _gated_rms_bwd("gated_rms_bwd", y_ssd, zx, norm_w_full, dyn)
    dxs, dbm, dcm, ddt_x, dacs, dd = _ssd_bwd("ssd_bwd", xbc, dt_x, ein_x, eout_x, a_col, a_row, d_x, prev, dy_ssd,
                                              di, g_n, hpg, p_dim)
    draw, d_bias, d_alog = _dt_bwd("dt_bwd", dt_raw, bias128, alog128, from_col(dacs), from_col(ddt_x), 0)
    conv_parts = [_conv_bwd(f"conv_bwd_{tag}", zx, conv_w_full, conv_b_full, dact, di, first)
                  for tag, dact, first in (("xs", dxs, 0), ("b", dbm, di), ("c", dcm, di + g_n * D_STATE))]
    d_conv_w = jnp.concatenate([c[1] for c in conv_parts], axis=1)
    d_conv_b = jnp.concatenate([c[2] for c in conv_parts], axis=1)
    dzx = jnp.concatenate([dz] + [c[0] for c in conv_parts], axis=1)
    d_dt = draw[:, :h_n]
    g_in_t = _in_proj_dw_t("in_proj_dw", dzx, d_dt, x1_b, after=toks)
    toks = emit("ssm_in", {"in": g_in_t.reshape(N_DEV, dp_s, d)})
    dx_dt = _in_proj_dt_dx("in_proj_dt_dx", d_dt, in_t, zw, du2)
    (dx1,) = _mm_fwd("in_proj_dx", dzx, in_t, "plain", [F32], _plain_add_epilogue, (dx_dt,), after=toks, k_rows=zw)
    toks = to_chips((dx1,))

    du0, _, dg_0, db_0, toks = _tail_bwd("l0", dx1, saved0, wts[0], ln_g_full[0], p_l[0], emit, to_chips, toks)
    dh, dpool, d_scale = _pool_bwd_mm("pool_bwd_mm", du0, hraw, pool_full, pool_scale)
    toks += emit("pool", dict(pool=_to_slots(_pool_dw("pool_dw", pooled, dh), 1)))
    grad_x = _pool_windows("pool_bwd", dpool, True, du0, after=tuple(toks))
    toks = to_chips((grad_x,))

    def update(tag, w, m, v, parts, layer, prev=None):
        own, recv = parts
        return _adamw(f"adamw_{tag}_{layer}", _as3d(w), _as3d(m), _as3d(v),
                      [(own, None), (recv, 0), (recv, 1), (recv, 2)], layer, prev, chip)

    q = collect("l1_ple", (grad_x, *toks))
    r_gate = update("ple_gate_w", ple_gate_w, m_ple_gate_w, v_ple_gate_w, q["gate"], 1)
    r_plew = update("ple_w", ple_w, m_ple_w, v_ple_w, q["plew"], 1)
    r_w2 = update("mlp_w2", mlp_w2, m_mlp_w2, v_mlp_w2, collect("l1_w2", (r_gate[0],))["w2"], 1)
    r_w1 = update("mlp_w1", mlp_w1, m_mlp_w1, v_mlp_w1, collect("l1_w1", (r_w2[0],))["w1"], 1)
    r_out = update("ssm_out_w", ssm_out_w, m_ssm_out_w, v_ssm_out_w, collect("ssm_out", (r_w1[0],))["out"], 0)
    in_wt, in_mt, in_vt = [jnp.swapaxes(a, 1, 2) for a in (ssm_in_w, m_ssm_in_w, v_ssm_in_w)]
    r_in = update("ssm_in_w", in_wt, in_mt, in_vt, collect("ssm_in", (r_out[0],))["in"], 0)

    d_ln_g = jnp.stack([jnp.stack(dg_0), jnp.stack(dg_1)]).reshape(2, 2, d)
    d_ln_b = jnp.stack([jnp.stack(db_0), jnp.stack(db_1)]).reshape(2, 2, d)
    partial_shapes = [(CONV_WIDTH, cd), (1, cd), (1, di), (2, 2, d), (2, 2, d), (1, d), (1, h_n), (1, h_n), (1, h_n),
                      (1, d)]
    partial = _pack([d_conv_w, d_conv_b, d_norm_w, d_ln_g, d_ln_b, d_scale, d_bias[:, :h_n], d_alog[:, :h_n],
                     dd.reshape(1, h_n), loss_cols], 8 * V7X_LANES)
    (all_partials,) = _all_gather("ag_small_grads", [partial], after=(r_in[0],))
    tot = _unpack(_sum8("sum_small_grads", all_partials), partial_shapes)
    t_conv_w, t_conv_b, t_norm_w, t_ln_g, t_ln_b, t_scale, t_bias, t_alog, t_dd, t_loss = tot
    loss = jnp.sum(t_loss)

    def mine(a, per):
        return lax.dynamic_slice_in_dim(a, me * per, per, axis=a.ndim - 1)

    small_names = ["ssm_conv_w", "ssm_conv_b", "ssm_norm_w", "ln_g", "ln_b", "pool_scale", "ssm_dt_bias", "ssm_a_log",
                   "ssm_d"]
    small_w = [ssm_conv_w, ssm_conv_b, ssm_norm_w, ln_g, ln_b, pool_scale, ssm_dt_bias, ssm_a_log, ssm_d]
    small_m = [m_ssm_conv_w, m_ssm_conv_b, m_ssm_norm_w, m_ln_g, m_ln_b, m_pool_scale, m_ssm_dt_bias, m_ssm_a_log,
               m_ssm_d]
    small_v = [v_ssm_conv_w, v_ssm_conv_b, v_ssm_norm_w, v_ln_g, v_ln_b, v_pool_scale, v_ssm_dt_bias, v_ssm_a_log,
               v_ssm_d]
    small_grads = [mine(t_conv_w, cd_s), mine(t_conv_b, cd_s), mine(t_norm_w, di_s), mine(t_ln_g, d_s),
                   mine(t_ln_b, d_s), t_scale, t_bias, t_alog, t_dd]
    shapes = [w.shape for w in small_w]
    pk = [_pack(group, 8 * V7X_LANES)[None] for group in (small_w, small_m, small_v, small_grads)]
    res = _adamw("adamw_small", pk[0], pk[1], pk[2], [(pk[3], 0)], 0)
    upd = {}
    for name, vals in zip(small_names, zip(*[_unpack(r, shapes) for r in res])):
        upd[name] = list(vals)

    q = collect("l0_ple", (res[0],))
    r_gate = update("ple_gate_w", ple_gate_w, m_ple_gate_w, v_ple_gate_w, q["gate"], 0, r_gate)
    r_plew = update("ple_w", ple_w, m_ple_w, v_ple_w, q["plew"], 0, r_plew)
    r_w2 = update("mlp_w2", mlp_w2, m_mlp_w2, v_mlp_w2, collect("l0_w2", (r_gate[0],))["w2"], 0, r_w2)
    r_w1 = update("mlp_w1", mlp_w1, m_mlp_w1, v_mlp_w1, collect("l0_w1", (r_w2[0],))["w1"], 0, r_w1)
    r_pool = update("pool_w", pool_w, m_pool_w, v_pool_w, collect("pool", (r_w1[0],))["pool"], 0)
    assert not scattering
    large = {"pool_w": (pool_w, r_pool), "ssm_in_w": (in_wt, r_in), "ssm_out_w": (ssm_out_w, r_out),
             "mlp_w1": (mlp_w1, r_w1), "mlp_w2": (mlp_w2, r_w2), "ple_w": (ple_w, r_plew),
             "ple_gate_w": (ple_gate_w, r_gate)}
    for name, (w, rs) in large.items():
        upd[name] = [r.reshape(w.shape) for r in rs]
    upd["ssm_in_w"] = [jnp.swapaxes(r, 1, 2) for r in upd["ssm_in_w"]]

    order = ["pool_w", "pool_scale", "ssm_in_w", "ssm_conv_w", "ssm_conv_b", "ssm_dt_bias", "ssm_a_log", "ssm_d",
             "ssm_norm_w", "ssm_out_w", "mlp_w1", "mlp_w2", "ln_g", "ln_b", "ple_w", "ple_gate_w"]
    out = [loss, grad_x[None]]
    for k in range(4):
        out += [upd[name][k] for name in order]
    return tuple(out)
```

```python
import jax
import jax.numpy as jnp
from jax import lax
from jax.experimental import pallas as pl
from jax.experimental.pallas import tpu as pltpu

F32 = jnp.float32
BF16 = jnp.bfloat16
SDS = jax.ShapeDtypeStruct
MESH = pl.DeviceIdType.MESH
ANY = pl.BlockSpec(memory_space=pl.ANY)

N_DEV = 8
DEPTH = 2
ALPHA = (2.0 * DEPTH) ** 0.25
LN_EPS = 1e-5
RMS_EPS = 1e-5
POOL_WINDOW_LOG2 = (1, 2, 3, 4)
D_STATE = 128
CHUNK = 128
SSD_STEP_CHUNKS = 4
CONV_WIDTH = 4
ADAM_LR = 0.001
ADAM_B1 = 0.9
ADAM_B2 = 0.999
ADAM_EPS = 1e-08
ADAM_WD = 0.01
ADAM_STEP = 10

V7X_LANES = 128
V7X_VMEM_LIMIT = 48 * 1024 * 1024


def _cp(*sem):
    return pltpu.CompilerParams(dimension_semantics=sem, vmem_limit_bytes=V7X_VMEM_LIMIT)


def _pick(dim, cap):
    if dim <= cap:
        return dim
    best = None
    for t in range(V7X_LANES, cap + 1, V7X_LANES):
        if dim % t == 0:
            best = t
    assert best is not None, (dim, cap)
    return best


def _row_tile(rows, cols, itemsize=4, target=1 << 20):
    t = rows
    while t % 2 == 0 and t // 2 >= 16 and (t // 2) % 16 == 0 and t * cols * itemsize > target:
        t //= 2
    return t


def _all_gather(name, shards, after=()):
    n, na = len(shards), len(after)

    def body(*refs):
        ins, outs = refs[:n], refs[n + na:2 * n + na]
        send_sems, recv_sems, local_sems = refs[2 * n + na:]
        x, y, c = lax.axis_index("x"), lax.axis_index("y"), lax.axis_index("c")
        me, sibling = (x, y, c), (x, y, 1 - c)
        chips = [(1 - x, y), (x, 1 - y), (1 - x, 1 - y)]

        def copy(a, k, block, to, src=None):
            dst = outs[a].at[4 * block[0] + 2 * block[1] + block[2]]
            return pltpu.make_async_remote_copy(
                src_ref=dst if src is None else src, dst_ref=dst, send_sem=send_sems.at[a, k],
                recv_sem=recv_sems.at[a, k], device_id=to, device_id_type=MESH)

        mine = [pltpu.make_async_copy(ins[a], outs[a].at[4 * x + 2 * y + c], local_sems.at[a]) for a in range(n)]
        for cp in mine:
            cp.start()
        first = []
        for a in range(n):
            first.append(copy(a, 0, me, sibling, src=ins[a]))
            first += [copy(a, 1 + j, me, (*chip, c), src=ins[a]) for j, chip in enumerate(chips)]
        for cp in first:
            cp.start()
        passed = []
        for j, chip in enumerate(chips):
            for a in range(n):
                copy(a, 1 + j, (*chip, c), me).wait_recv()
                fwd = copy(a, 4 + j, (*chip, c), sibling)
                fwd.start()
                passed.append(fwd)
        for a in range(n):
            copy(a, 0, sibling, me).wait_recv()
            for j, chip in enumerate(chips):
                copy(a, 4 + j, (*chip, 1 - c), me).wait_recv()
        for cp in first + passed:
            cp.wait_send()
        for cp in mine:
            cp.wait()

    return pl.pallas_call(
        body, name=name,
        out_shape=[SDS((N_DEV,) + s.shape, s.dtype) for s in shards],
        in_specs=[ANY] * (n + na), out_specs=[ANY] * n,
        scratch_shapes=[pltpu.SemaphoreType.DMA((n, 7)), pltpu.SemaphoreType.DMA((n, 7)),
                        pltpu.SemaphoreType.DMA((n,))],
    )(*shards, *after)


HBM_SPEC = pl.BlockSpec(memory_space=pltpu.HBM)
SEM_SPEC = pl.BlockSpec(memory_space=pltpu.SEMAPHORE)
EFFECT = pltpu.SideEffectType.DATAFLOW_SIDE_EFFECTING


def _ag_first_copies(ins, lands, send_sems, recv_sems):
    x, y, c = lax.axis_index("x"), lax.axis_index("y"), lax.axis_index("c")
    targets = [(x, y, 1 - c), (1 - x, y, c), (x, 1 - y, c), (1 - x, 1 - y, c)]
    return [pltpu.make_async_remote_copy(
        src_ref=ins[a], dst_ref=lands[a].at[4 * x + 2 * y + c], send_sem=send_sems.at[4 * a + k],
        recv_sem=recv_sems.at[4 * a + k], device_id=to, device_id_type=MESH)
        for a in range(len(ins)) for k, to in enumerate(targets)]


def _ag_forward_copies(ins, lands, send_sems, recv_sems):
    x, y, c = lax.axis_index("x"), lax.axis_index("y"), lax.axis_index("c")
    cps = []
    for a in range(len(lands)):
        for j, (px, py) in enumerate([(1 - x, y), (x, 1 - y), (1 - x, 1 - y)]):
            blk = lands[a].at[4 * px + 2 * py + c]
            cps.append(pltpu.make_async_remote_copy(
                src_ref=blk, dst_ref=blk, send_sem=send_sems.at[3 * a + j], recv_sem=recv_sems.at[3 * a + j],
                device_id=(x, y, 1 - c), device_id_type=MESH))
    return cps


def _rs_sibling_copies(ins, lands, send_sems, recv_sems):
    x, y, c = lax.axis_index("x"), lax.axis_index("y"), lax.axis_index("c")
    return [pltpu.make_async_remote_copy(
        src_ref=ins[a].at[2 * q + 1 - c], dst_ref=lands[a].at[q], send_sem=send_sems.at[4 * a + q],
        recv_sem=recv_sems.at[4 * a + q], device_id=(x, y, 1 - c), device_id_type=MESH)
        for a in range(len(ins)) for q in range(4)]


def _rs_chip_copies(ins, lands, send_sems, recv_sems):
    x, y, c = lax.axis_index("x"), lax.axis_index("y"), lax.axis_index("c")
    cps = []
    for a in range(len(ins)):
        for j, (px, py) in enumerate([(1 - x, y), (x, 1 - y), (1 - x, 1 - y)]):
            cps.append(pltpu.make_async_remote_copy(
                src_ref=ins[a].at[2 * px + py], dst_ref=lands[a].at[j], send_sem=send_sems.at[3 * a + j],
                recv_sem=recv_sems.at[3 * a + j], device_id=(px, py, c), device_id_type=MESH))
    return cps


def _async_start(name, build, sem_shape, ins, lands, after=()):
    arrays = [*ins, *lands]
    n_i, n_t, n_a = len(ins), len(arrays), len(after)

    def body(*refs):
        outs = refs[n_t + n_a:]
        for cp in build(refs[:n_i], refs[n_i:n_t], outs[0], outs[1]):
            cp.start()
        outs[-1][...] = jnp.zeros_like(outs[-1])

    res = pl.pallas_call(
        body, name=name,
        out_shape=(pltpu.SemaphoreType.DMA(sem_shape), pltpu.SemaphoreType.DMA(sem_shape),
                   *[pltpu.HBM(a.shape, a.dtype) for a in arrays], SDS((8, V7X_LANES), F32)),
        in_specs=[HBM_SPEC] * n_t + [ANY] * n_a,
        out_specs=(SEM_SPEC, SEM_SPEC, *[HBM_SPEC] * n_t, pl.BlockSpec(memory_space=pltpu.VMEM)),
        input_output_aliases={i: 2 + i for i in range(n_t)},
        compiler_params=pltpu.CompilerParams(has_side_effects=EFFECT),
    )(*[pltpu.with_memory_space_constraint(a, pltpu.HBM) for a in arrays], *after)
    return res[0], res[1], list(res[2:2 + n_t]), res[-1]


def _async_wait(name, build, handle, n_i, after=()):
    send_sems, recv_sems, arrays, _ = handle
    n_t, n_a = len(arrays), len(after)

    def body(*refs):
        for cp in build(refs[:n_i], refs[n_i:n_t], refs[n_t], refs[n_t + 1]):
            cp.wait_send()
            cp.wait_recv()

    res = pl.pallas_call(
        body, name=name, out_shape=tuple(pltpu.HBM(a.shape, a.dtype) for a in arrays),
        in_specs=[HBM_SPEC] * n_t + [SEM_SPEC, SEM_SPEC] + [ANY] * n_a, out_specs=tuple([HBM_SPEC] * n_t),
        input_output_aliases={i: i for i in range(n_t)},
        compiler_params=pltpu.CompilerParams(has_side_effects=EFFECT),
    )(*arrays, send_sems, recv_sems, *after)
    return list(res[:n_i]), list(res[n_i:])


def _mm_core(name, a, b, *, grid, a_spec, b_spec, dims, acc_shape, outs, out_spec, epilogue=None, extras=(),
             extra_specs=(), a_fn=None, after=(), carry=None):
    nk = grid[2]
    if carry is not None:
        after = (*after, carry)
    ne, no, na = len(extras), len(outs), len(after)

    def body(a_ref, b_ref, *rest):
        e_refs, o_refs, acc = rest[:ne], rest[ne + na:ne + na + no], rest[ne + na + no]
        k = pl.program_id(2)

        def product():
            lhs = a_ref[...] if a_fn is None else a_fn(a_ref[...])
            return lax.dot_general(lhs.astype(BF16), b_ref[...].astype(BF16), dims, preferred_element_type=F32)

        @pl.when(k == 0)
        def _():
            acc[...] = product()

        @pl.when(k > 0)
        def _():
            acc[...] += product()

        @pl.when(k == nk - 1)
        def _():
            r = acc[...]
            vals = epilogue(r, *[e[...] for e in e_refs]) if epilogue is not None else (r,)
            for o, v in zip(o_refs, vals):
                o[...] = v.astype(o.dtype)

    res = pl.pallas_call(
        body, name=name, grid=grid, out_shape=list(outs),
        in_specs=[a_spec, b_spec, *extra_specs, *[ANY] * na], out_specs=[out_spec] * no,
        scratch_shapes=[pltpu.VMEM(acc_shape, F32)],
        input_output_aliases={} if carry is None else {1 + ne + na: 0},
        compiler_params=_cp("parallel", "parallel", "arbitrary"),
    )(a, b, *extras, *after)
    return res


NN = (((1,), (0,)), ((), ()))
NT = (((1,), (1,)), ((), ()))
TN = (((0,), (0,)), ((), ()))


def _w_dims(w, kind):
    if kind == "col":
        return w.shape[1], N_DEV * w.shape[2], w.shape[1], w.shape[2]
    if kind == "row":
        return N_DEV * w.shape[1], w.shape[2], w.shape[1], w.shape[2]
    return w.shape[0], w.shape[1], w.shape[0], w.shape[1]


MM_VMEM_BUDGET = 40 * 1024 * 1024


def _row_and_k_blocks(m, tn, k_len, k_caps, a, out_dtypes, extras):
    per_out = sum(jnp.dtype(dt).itemsize for dt in out_dtypes) + sum(e.dtype.itemsize for e in extras)
    best = None
    for tm in (_pick(m, 2048), _pick(m, 1024), _pick(m, 512)):
        for tk in [_pick(k_len, cap) for cap in k_caps]:
            used = tm * tn * (4 + 2 * per_out) + 2 * (tm * tk * a.dtype.itemsize + tk * tn * 2)
            key = ((m // tm) * (k_len // tk), -tm)
            if used <= MM_VMEM_BUDGET and (best is None or key < best[0]):
                best = (key, tm, tk)
    assert best is not None, (m, tn, k_len)
    return best[1], best[2]


def _mm_fwd(name, a, w, kind, out_dtypes, epilogue=None, extras=(), a_fn=None, after=(), k_rows=None):
    if kind == "row":
        w, kind = w.reshape(-1, w.shape[-1]), "plain"
    m = a.shape[0]
    kk, n, ks, ns = _w_dims(w, kind)
    if k_rows is None:
        assert kk == a.shape[1]
    else:
        assert kind == "plain" and k_rows <= min(kk, a.shape[1])
        kk = k_rows
    tn = _pick(ns, 1024) if kind == "col" else _pick(n, 1152)
    tm, tk = _row_and_k_blocks(m, tn, kk, (1024, 512), a, out_dtypes, extras)
    if kind == "col":
        nb = ns // tn
        b_spec = pl.BlockSpec((None, tk, tn), lambda i, j, k: (j // nb, k, j % nb))
    else:
        b_spec = pl.BlockSpec((tk, tn), lambda i, j, k: (k, j))
    mn_spec = pl.BlockSpec((tm, tn), lambda i, j, k: (i, j))
    return _mm_core(
        name, a, w, grid=(m // tm, n // tn, kk // tk),
        a_spec=pl.BlockSpec((tm, tk), lambda i, j, k: (i, k)), b_spec=b_spec, dims=NN, acc_shape=(tm, tn),
        outs=[SDS((m, n), dt) for dt in out_dtypes], out_spec=mn_spec, epilogue=epilogue, extras=extras,
        extra_specs=[mn_spec] * len(extras), a_fn=a_fn, after=after)


def _mm_dx(name, dy, w, kind, out_dtypes, epilogue=None, extras=(), after=(), k_rows=None):
    if kind == "row":
        w, kind = w.reshape(-1, w.shape[-1]), "plain"
    m, n_dim = dy.shape
    kk, n, ks, ns = _w_dims(w, kind)
    assert n == n_dim
    if k_rows is not None:
        assert kind == "plain" and k_rows <= kk
        kk = k_rows
    tn = _pick(kk, 1024)
    if kind == "col":
        tm, tk = _row_and_k_blocks(m, tn, ns, (1024, 512), dy, out_dtypes, extras)
        kb = ns // tk
        b_spec = pl.BlockSpec((None, tn, tk), lambda i, j, k: (k // kb, j, k % kb))
    else:
        tm, tk = _row_and_k_blocks(m, tn, n, (1152, 512), dy, out_dtypes, extras)
        b_spec = pl.BlockSpec((tn, tk), lambda i, j, k: (j, k))
    mk_spec = pl.BlockSpec((tm, tn), lambda i, j, k: (i, j))
    return _mm_core(
        name, dy, w, grid=(m // tm, kk // tn, n // tk),
        a_spec=pl.BlockSpec((tm, tk), lambda i, j, k: (i, k)), b_spec=b_spec, dims=NT, acc_shape=(tm, tn),
        outs=[SDS((m, kk), dt) for dt in out_dtypes], out_spec=mk_spec, epilogue=epilogue, extras=extras,
        extra_specs=[mk_spec] * len(extras), after=after)


def _in_proj_dt(name, a, w_t, first, h_n):
    m, kk = a.shape
    tm, tk = _pick(m, 1024), _pick(kk, 1024)
    blk = first // h_n
    return _mm_core(
        name, a, w_t, grid=(m // tm, 1, kk // tk), a_spec=pl.BlockSpec((tm, tk), lambda i, j, k: (i, k)),
        b_spec=pl.BlockSpec((h_n, tk), lambda i, j, k: (blk, k)), dims=NT, acc_shape=(tm, h_n),
        outs=[SDS((m, h_n), F32)], out_spec=pl.BlockSpec((tm, h_n), lambda i, j, k: (i, 0)))[0]


def _in_proj_dt_dx(name, d_dt, w_t, first, res):
    m, h_n = d_dt.shape
    n = w_t.shape[1]
    tm, tn = _pick(m, 1024), _pick(n, 1024)
    blk = first // h_n
    mn_spec = pl.BlockSpec((tm, tn), lambda i, j, k: (i, j))
    return _mm_core(
        name, d_dt, w_t, grid=(m // tm, n // tn, 1), a_spec=pl.BlockSpec((tm, h_n), lambda i, j, k: (i, 0)),
        b_spec=pl.BlockSpec((h_n, tn), lambda i, j, k: (blk, j)), dims=NN, acc_shape=(tm, tn),
        outs=[SDS((m, n), F32)], out_spec=mn_spec, epilogue=_residual_epilogue, extras=(res,),
        extra_specs=[mn_spec])[0]


def _in_proj_dw_t(name, dz, d_dt, x, after=()):
    m, n = x.shape
    zw, h_n = dz.shape[1], d_dt.shape[1]
    tm, tn, tk = _pick(zw, 1024), _pick(n, 1024), _pick(m, 1024)
    out = SDS((zw + h_n, n), BF16)
    x_spec = pl.BlockSpec((tk, tn), lambda i, j, k: (k, j))
    main = _mm_core(
        name, dz, x, grid=(zw // tm, n // tn, m // tk), a_spec=pl.BlockSpec((tk, tm), lambda i, j, k: (k, i)),
        b_spec=x_spec, dims=TN, acc_shape=(tm, tn), outs=[out], out_spec=pl.BlockSpec((tm, tn), lambda i, j, k: (i, j)),
        after=after)[0]
    blk = zw // h_n
    return _mm_core(
        name + "_dt", d_dt, x, grid=(1, n // tn, m // tk), a_spec=pl.BlockSpec((tk, h_n), lambda i, j, k: (k, 0)),
        b_spec=x_spec, dims=TN, acc_shape=(h_n, tn), outs=[out],
        out_spec=pl.BlockSpec((h_n, tn), lambda i, j, k: (blk, j)), carry=main)[0]


def _mm_dw(name, a, dy, kind, a_fn=None, after=()):
    m, kk = a.shape
    n = dy.shape[1]
    tk = _pick(m, 1024)
    if kind == "col":
        ns = n // N_DEV
        tm, tn = _pick(kk, 1024), _pick(ns, 1024)
        nb = ns // tn
        out = SDS((N_DEV, kk, ns), BF16)
        out_spec = pl.BlockSpec((None, tm, tn), lambda i, j, k: (j // nb, i, j % nb))
    else:
        tm, tn = _pick(kk, 1024), _pick(n, 1152)
        out = SDS((kk, n), BF16)
        out_spec = pl.BlockSpec((tm, tn), lambda i, j, k: (i, j))
    res = _mm_core(
        name, a, dy, grid=(kk // tm, n // tn, m // tk),
        a_spec=pl.BlockSpec((tk, tm), lambda i, j, k: (k, i)), b_spec=pl.BlockSpec((tk, tn), lambda i, j, k: (k, j)),
        dims=TN, acc_shape=(tm, tn), outs=[out], out_spec=out_spec, a_fn=a_fn, after=after)[0]
    return res.reshape(N_DEV, kk // N_DEV, n) if kind == "row" else res


def _rowwise(name, fn, ins, outs, rows, tile):
    arrays, specs = [], []
    for arr, kind in ins:
        arrays.append(arr)
        if kind == "row":
            specs.append(pl.BlockSpec((tile, arr.shape[1]), lambda i: (i, 0)))
        elif kind == "vec":
            specs.append(pl.BlockSpec(arr.shape, lambda i, nd=arr.ndim: (0,) * nd))
        else:
            specs.append(kind)
    out_shapes, out_specs, kinds = [], [], []
    for cols, dt, kind in outs:
        kinds.append(kind)
        if kind == "row":
            out_shapes.append(SDS((rows, cols), dt))
            out_specs.append(pl.BlockSpec((tile, cols), lambda i: (i, 0)))
        else:
            out_shapes.append(SDS((1, cols), F32))
            out_specs.append(pl.BlockSpec((1, cols), lambda i: (0, 0)))
    ni = len(arrays)
    has_acc = "acc" in kinds

    def body(*refs):
        vals = fn(*[r[...] for r in refs[:ni]])
        i = pl.program_id(0)
        for o, v, kind in zip(refs[ni:], vals, kinds):
            if kind == "row":
                o[...] = v.astype(o.dtype)
            else:
                @pl.when(i == 0)
                def _(o=o):
                    o[...] = jnp.zeros_like(o)

                o[...] += v

    return pl.pallas_call(
        body, name=name, grid=(rows // tile,), out_shape=out_shapes, in_specs=specs, out_specs=out_specs,
        compiler_params=_cp("arbitrary" if has_acc else "parallel"),
    )(*arrays)


def _ln_fwd(name, u, g, b, after=()):
    d = u.shape[1]

    def fn(u, g, b, *unused):
        mu = jnp.mean(u, axis=1, keepdims=True)
        xc = u - mu
        var = jnp.mean(xc * xc, axis=1, keepdims=True)
        y = xc * lax.rsqrt(var + LN_EPS) * g + b
        return y, y

    ins = [(u, "row"), (g, "vec"), (b, "vec")] + [(t, "vec") for t in after]
    return _rowwise(name, fn, ins, [(d, F32, "row"), (d, BF16, "row")], u.shape[0], 256)


def _ln_bwd(name, u, dy, g):
    d = u.shape[1]

    def fn(u, dy, g):
        mu = jnp.mean(u, axis=1, keepdims=True)
        xc = u - mu
        var = jnp.mean(xc * xc, axis=1, keepdims=True)
        rstd = lax.rsqrt(var + LN_EPS)
        xhat = xc * rstd
        dxhat = dy * g
        m1 = jnp.mean(dxhat, axis=1, keepdims=True)
        m2 = jnp.mean(dxhat * xhat, axis=1, keepdims=True)
        du = rstd * (dxhat - m1 - xhat * m2)
        return du, du, jnp.sum(dy * xhat, axis=0, keepdims=True), jnp.sum(dy, axis=0, keepdims=True)

    return _rowwise(name, fn, [(u, "row"), (dy, "row"), (g, "vec")],
                    [(d, F32, "row"), (d, BF16, "row"), (d, F32, "acc"), (d, F32, "acc")], u.shape[0], 256)


def _loss_bwd(name, y, target):
    d = y.shape[1]

    def fn(y, t):
        e = y - t
        return e * (1.0 / d), jnp.sum(e * e, axis=0, keepdims=True) * (0.5 / d)

    return _rowwise(name, fn, [(y, "row"), (target, "row")], [(d, F32, "row"), (d, F32, "acc")], y.shape[0], 256)


def _ple_bwd(name, dx, e, gate):
    d = dx.shape[1]

    def fn(dx, e, gate):
        return dx * e * gate * (1.0 - gate), dx * gate

    return _rowwise(name, fn, [(dx, "row"), (e, "row"), (gate, "row")], [(d, BF16, "row"), (d, BF16, "row")],
                    dx.shape[0], 256)


def _sigmoid(v):
    return 1.0 / (1.0 + jnp.exp(-v))


def _gated_rms_fwd(name, y, zx, norm_w, after=()):
    di = y.shape[1]

    def fn(y, z, w, *unused):
        yg = y * (z * _sigmoid(z))
        r = lax.rsqrt(jnp.mean(yg * yg, axis=1, keepdims=True) + RMS_EPS)
        return (yg * r * w,)

    z_spec = pl.BlockSpec((128, di), lambda i: (i, 0))
    ins = [(y, "row"), (zx, z_spec), (norm_w, "vec")] + [(t, "vec") for t in after]
    return _rowwise(name, fn, ins, [(di, BF16, "row")], y.shape[0], 128)[0]


def _gated_rms_bwd(name, y, zx, norm_w, dout):
    di = y.shape[1]

    def fn(y, z, w, dout):
        sg = _sigmoid(z)
        sz = z * sg
        yg = y * sz
        r = lax.rsqrt(jnp.mean(yg * yg, axis=1, keepdims=True) + RMS_EPS)
        dn = dout * w
        dyg = r * (dn - yg * (r * r) * jnp.mean(dn * yg, axis=1, keepdims=True))
        dy = dyg * sz
        dz = dyg * y * (sg * (1.0 + z * (1.0 - sg)))
        return dy, dz, jnp.sum(dout * yg * r, axis=0, keepdims=True)

    z_spec = pl.BlockSpec((128, di), lambda i: (i, 0))
    return _rowwise(name, fn, [(y, "row"), (zx, z_spec), (norm_w, "vec"), (dout, "row")],
                    [(di, F32, "row"), (di, BF16, "row"), (di, F32, "acc")], y.shape[0], 128)


def _shift_down(v, j, row):
    return jnp.where(row >= j, pltpu.roll(v, j, 0), 0.0)


def _shift_up(v, j, row):
    t = v.shape[0]
    return jnp.where(row < t - j, pltpu.roll(v, t - j, 0), 0.0)


def _pool_select(parts, g):
    return jnp.where(g == 0, parts[0], jnp.where(g == 1, parts[1], jnp.where(g == 2, parts[2], parts[3])))


def _pool_windows(name, x, transpose, scale_by=None, after=()):
    t, d = x.shape
    cg = d // 4
    cw = V7X_LANES
    per = cg // cw

    def body(*refs):
        x_ref, o_ref = refs[0], refs[-1]
        g = pl.program_id(0) // per
        xv = x_ref[...]
        row = lax.broadcasted_iota(jnp.int32, (t, 1), 0)
        cnt = jnp.minimum(row + 1, jnp.left_shift(2, g)).astype(F32)
        s = xv / cnt if transpose else xv
        parts = []
        for lg in POOL_WINDOW_LOG2:
            j = 1 << (lg - 1)
            s = s + (_shift_up(s, j, row) if transpose else _shift_down(s, j, row))
            parts.append(s)
        sel = _pool_select(parts, g)
        if transpose:
            o_ref[...] = ALPHA * refs[1][...] + sel - xv
        else:
            o_ref[...] = (sel / cnt - xv).astype(o_ref.dtype)

    col = pl.BlockSpec((t, cw), lambda j: (0, j))
    ins = [x] if scale_by is None else [x, scale_by]
    return pl.pallas_call(
        body, name=name, grid=(d // cw,), out_shape=SDS((t, d), F32 if transpose else BF16),
        in_specs=[col] * len(ins) + [ANY] * len(after), out_specs=col, compiler_params=_cp("parallel"),
    )(*ins, *after)


def _pool_mm(name, pooled, w, scale, x):
    t, d = x.shape
    cg = d // 4
    tm = _pick(t, 1024)

    def body(p_ref, w_ref, s_ref, x_ref, u_ref, h_ref):
        h = jnp.dot(p_ref[...], w_ref[...], preferred_element_type=F32)
        h_ref[...] = h
        u_ref[...] = ALPHA * x_ref[...] + h * s_ref[...]

    blk = pl.BlockSpec((tm, cg), lambda g, i: (i, g))
    return pl.pallas_call(
        body, name=name, grid=(4, t // tm), out_shape=[SDS((t, d), F32), SDS((t, d), F32)],
        in_specs=[blk, pl.BlockSpec((None, cg, cg), lambda g, i: (g, 0, 0)), pl.BlockSpec((1, cg), lambda g, i: (0, g)),
                  blk],
        out_specs=[blk, blk], compiler_params=_cp("parallel", "parallel"),
    )(pooled, w, scale, x)


def _pool_bwd_mm(name, du, hraw, w, scale):
    t, d = du.shape
    cg = d // 4
    tm = _pick(t, 1024)

    def body(du_ref, h_ref, w_ref, s_ref, dh_ref, dp_ref, ds_ref):
        @pl.when(pl.program_id(1) == 0)
        def _():
            ds_ref[...] = jnp.zeros_like(ds_ref)

        duv = du_ref[...]
        ds_ref[...] += jnp.sum(duv * h_ref[...], axis=0, keepdims=True)
        dh = (duv * s_ref[...]).astype(BF16)
        dh_ref[...] = dh
        dp_ref[...] = lax.dot_general(dh, w_ref[...], NT, preferred_element_type=F32)

    blk = pl.BlockSpec((tm, cg), lambda g, i: (i, g))
    vec = pl.BlockSpec((1, cg), lambda g, i: (0, g))
    return pl.pallas_call(
        body, name=name, grid=(4, t // tm), out_shape=[SDS((t, d), BF16), SDS((t, d), F32), SDS((1, d), F32)],
        in_specs=[blk, blk, pl.BlockSpec((None, cg, cg), lambda g, i: (g, 0, 0)), vec],
        out_specs=[blk, blk, vec], compiler_params=_cp("parallel", "arbitrary"),
    )(du, hraw, w, scale)


def _pool_dw(name, pooled, dh):
    t, d = pooled.shape
    cg = d // 4
    tk = _pick(t, 512)
    nk = t // tk

    def body(p_ref, dh_ref, o_ref, acc):
        k = pl.program_id(1)

        @pl.when(k == 0)
        def _():
            acc[...] = jnp.zeros_like(acc)

        acc[...] += lax.dot_general(p_ref[...], dh_ref[...], TN, preferred_element_type=F32)

        @pl.when(k == nk - 1)
        def _():
            o_ref[...] = acc[...].astype(o_ref.dtype)

    blk = pl.BlockSpec((tk, cg), lambda g, k: (k, g))
    return pl.pallas_call(
        body, name=name, grid=(4, nk), out_shape=SDS((4, cg, cg), BF16), in_specs=[blk, blk],
        out_specs=pl.BlockSpec((None, cg, cg), lambda g, k: (g, 0, 0)), scratch_shapes=[pltpu.VMEM((cg, cg), F32)],
        compiler_params=_cp("parallel", "arbitrary"),
    )(pooled, dh)


def _conv_pre(u, w_ref, b_ref, row):
    pre = b_ref[...] + _shift_down(u, 3, row) * w_ref[0:1, :]
    pre = pre + _shift_down(u, 2, row) * w_ref[1:2, :]
    pre = pre + _shift_down(u, 1, row) * w_ref[2:3, :]
    return pre + u * w_ref[3:4, :]


def _conv_fwd(name, zx, conv_w, conv_b, di):
    t = zx.shape[0]
    cd = conv_w.shape[1]
    cw = _pick(cd, 256)
    off = di // cw

    def body(u_ref, w_ref, b_ref, o_ref):
        row = lax.broadcasted_iota(jnp.int32, (t, 1), 0)
        pre = _conv_pre(u_ref[...], w_ref, b_ref, row)
        o_ref[...] = pre * _sigmoid(pre)

    return pl.pallas_call(
        body, name=name, grid=(cd // cw,), out_shape=SDS((t, cd), F32),
        in_specs=[pl.BlockSpec((t, cw), lambda j: (0, off + j)), pl.BlockSpec((CONV_WIDTH, cw), lambda j: (0, j)),
                  pl.BlockSpec((1, cw), lambda j: (0, j))],
        out_specs=pl.BlockSpec((t, cw), lambda j: (0, j)), compiler_params=_cp("parallel"),
    )(zx, conv_w, conv_b)


def _conv_bwd(name, zx, conv_w, conv_b, dact, di, first):
    t, cd = dact.shape
    cw = _pick(cd, 256)
    off, woff = (di + first) // cw, first // cw

    def body(u_ref, w_ref, b_ref, da_ref, du_ref, dw_ref, db_ref):
        row = lax.broadcasted_iota(jnp.int32, (t, 1), 0)
        u = u_ref[...]
        pre = _conv_pre(u, w_ref, b_ref, row)
        sg = _sigmoid(pre)
        dpre = da_ref[...] * (sg * (1.0 + pre * (1.0 - sg)))
        du = dpre * w_ref[3:4, :]
        for j in (1, 2, 3):
            du = du + _shift_up(dpre, j, row) * w_ref[3 - j:4 - j, :]
            dw_ref[3 - j:4 - j, :] = jnp.sum(dpre * _shift_down(u, j, row), axis=0, keepdims=True)
        dw_ref[3:4, :] = jnp.sum(dpre * u, axis=0, keepdims=True)
        db_ref[...] = jnp.sum(dpre, axis=0, keepdims=True)
        du_ref[...] = du.astype(du_ref.dtype)

    wspec = pl.BlockSpec((CONV_WIDTH, cw), lambda j: (0, j))
    bspec = pl.BlockSpec((1, cw), lambda j: (0, j))
    ospec = pl.BlockSpec((t, cw), lambda j: (0, j))
    return pl.pallas_call(
        body, name=name, grid=(cd // cw,), out_shape=[SDS((t, cd), BF16), SDS((CONV_WIDTH, cd), F32), SDS((1, cd), F32)],
        in_specs=[pl.BlockSpec((t, cw), lambda j: (0, off + j)), pl.BlockSpec((CONV_WIDTH, cw), lambda j: (0, woff + j)),
                  pl.BlockSpec((1, cw), lambda j: (0, woff + j)), ospec],
        out_specs=[ospec, wspec, bspec], compiler_params=_cp("parallel"),
    )(zx, conv_w, conv_b, dact)


def _expand_heads(name, arrays, h_n, p):
    t = arrays[0].shape[0]
    n = len(arrays)
    w = _pick(h_n * p, 512)

    def body(*refs):
        j = pl.program_id(0)
        head = lax.broadcasted_iota(jnp.int32, (V7X_LANES, w), 0)
        lane = lax.broadcasted_iota(jnp.int32, (V7X_LANES, w), 1)
        spread = (head == j * (w // p) + lane // p).astype(BF16)
        for a_ref, o_ref in zip(refs[:n], refs[n:]):
            rest = a_ref[...]
            out = jnp.zeros((t, w), F32)
            for _ in range(3):
                piece = rest.astype(BF16)
                out = out + jnp.dot(piece, spread, preferred_element_type=F32)
                rest = rest - piece.astype(F32)
            o_ref[...] = out

    full = pl.BlockSpec((t, V7X_LANES), lambda j: (0, 0))
    return pl.pallas_call(
        body, name=name, grid=(h_n * p // w,), out_shape=[SDS((t, h_n * p), F32)] * n, in_specs=[full] * n,
        out_specs=[pl.BlockSpec((t, w), lambda j: (0, j))] * n, compiler_params=_cp("parallel"),
    )(*arrays)


def _softplus(v):
    return jnp.maximum(v, 0.0) + jnp.log(1.0 + jnp.exp(-jnp.abs(v)))


def _dt_fwd(name, zx, bias, a_log, col_block):
    t = zx.shape[0]

    def body(r_ref, b_ref, al_ref, dt_ref, acs_ref, ein_ref, eout_ref):
        row = lax.broadcasted_iota(jnp.int32, (t, 1), 0) % CHUNK
        dt = _softplus(r_ref[...] + b_ref[...])
        da = dt * (-jnp.exp(al_ref[...]))
        s, r = da, da
        j = 1
        while j < CHUNK:
            s = s + jnp.where(row >= j, pltpu.roll(s, j, 0), 0.0)
            r = r + jnp.where(row < CHUNK - j, pltpu.roll(r, t - j, 0), 0.0)
            j *= 2
        dt_ref[...] = dt
        acs_ref[...] = s
        ein_ref[...] = jnp.exp(s)
        eout_ref[...] = jnp.exp(r - da)

    vec = pl.BlockSpec((1, V7X_LANES), lambda i: (0, 0))
    full = pl.BlockSpec((t, V7X_LANES), lambda i: (0, 0))
    return pl.pallas_call(
        body, name=name, grid=(1,), out_shape=[SDS((t, V7X_LANES), F32)] * 4,
        in_specs=[pl.BlockSpec((t, V7X_LANES), lambda i: (0, col_block)), vec, vec], out_specs=[full] * 4,
        compiler_params=_cp("arbitrary"),
    )(zx, bias, a_log)


def _dt_bwd(name, zx, bias, a_log, d_acs, d_dt, col_block):
    t = zx.shape[0]

    def body(r_ref, b_ref, al_ref, da_ref, dd_ref, draw_ref, db_ref, dal_ref):
        row = lax.broadcasted_iota(jnp.int32, (t, 1), 0) % CHUNK
        pre = r_ref[...] + b_ref[...]
        dt = _softplus(pre)
        a = -jnp.exp(al_ref[...])
        s = da_ref[...]
        j = 1
        while j < CHUNK:
            s = s + jnp.where(row < CHUNK - j, pltpu.roll(s, t - j, 0), 0.0)
            j *= 2
        ddt = dd_ref[...] + s * a
        dal_ref[...] = jnp.sum(s * dt, axis=0, keepdims=True) * a
        draw = ddt * _sigmoid(pre)
        db_ref[...] = jnp.sum(draw, axis=0, keepdims=True)
        draw_ref[...] = draw.astype(draw_ref.dtype)

    vec = pl.BlockSpec((1, V7X_LANES), lambda i: (0, 0))
    full = pl.BlockSpec((t, V7X_LANES), lambda i: (0, 0))
    return pl.pallas_call(
        body, name=name, grid=(1,), out_shape=[SDS((t, V7X_LANES), BF16), SDS((1, V7X_LANES), F32), SDS((1, V7X_LANES), F32)],
        in_specs=[pl.BlockSpec((t, V7X_LANES), lambda i: (0, col_block)), vec, vec, full, full],
        out_specs=[full, vec, vec], compiler_params=_cp("arbitrary"),
    )(zx, bias, a_log, d_acs, d_dt)


def _ssd_specs(t, di, g_n, hpg, p, rev):
    rows = SSD_STEP_CHUNKS * CHUNK
    nc = t // rows
    w = hpg * p
    nb = di // D_STATE

    def cc(c):
        return nc - 1 - c if rev else c

    return dict(
        xs=pl.BlockSpec((rows, w), lambda g, c: (cc(c), g)),
        bm=pl.BlockSpec((rows, D_STATE), lambda g, c: (cc(c), nb + g)),
        cm=pl.BlockSpec((rows, D_STATE), lambda g, c: (cc(c), nb + g_n + g)),
        col=pl.BlockSpec((None, rows, hpg), lambda g, c: (g, cc(c), 0)),
        rowv=pl.BlockSpec((None, hpg, rows), lambda g, c: (g, 0, cc(c))),
        head=pl.BlockSpec((None, 1, hpg), lambda g, c: (g, 0, 0)),
        lanes=pl.BlockSpec((1, w), lambda g, c: (0, g)),
        bc=pl.BlockSpec((rows, D_STATE), lambda g, c: (cc(c), g)),
        prev=pl.BlockSpec((SSD_STEP_CHUNKS, None, D_STATE, w), lambda g, c: (cc(c), g, 0, 0)),
        seg=pl.BlockSpec((w, V7X_LANES), lambda g, c: (0, 0)),
    )


def _decay_masks(cb, ac, ar, heads):
    li = lax.broadcasted_iota(jnp.int32, (CHUNK, CHUNK), 0)
    si = lax.broadcasted_iota(jnp.int32, (CHUNK, CHUNK), 1)
    lms = [jnp.exp(jnp.where(li >= si, ac[:, hh:hh + 1] - ar[hh:hh + 1, :], -jnp.inf)) for hh in heads]
    return lms, [(cb * lm).astype(BF16) for lm in lms]


def _ssd_fwd(name, xbc, dt_x, ein_x, eout_x, a_col, a_row, d_x, di, g_n, hpg, p):
    t = xbc.shape[0]
    nc = t // CHUNK
    w = hpg * p
    assert 2 * p == V7X_LANES and hpg % 2 == 0 and nc % SSD_STEP_CHUNKS == 0
    sp = _ssd_specs(t, di, g_n, hpg, p, False)

    def body(xs_ref, bm_ref, cm_ref, dt_ref, ein_ref, eout_ref, ac_ref, ar_ref, d_ref, y_ref, prev_ref, h_ref):
        @pl.when(pl.program_id(1) == 0)
        def _():
            h_ref[...] = jnp.zeros_like(h_ref)

        first = lax.broadcasted_iota(jnp.int32, (1, V7X_LANES), 1) < p
        for sub in range(SSD_STEP_CHUNKS):
            r = slice(sub * CHUNK, (sub + 1) * CHUNK)
            bm = bm_ref[r, :].astype(BF16)
            cm = cm_ref[r, :].astype(BF16)
            cb = lax.dot_general(cm, bm, NT, preferred_element_type=F32)
            xs = xs_ref[r, :]
            e_in = ein_ref[r, :]
            xdt = xs * dt_ref[r, :]
            ac, ar = ac_ref[r, :], ar_ref[:, r]
            ys = []
            for pr in range(hpg // 2):
                _, ms = _decay_masks(cb, ac, ar, (2 * pr, 2 * pr + 1))
                xp = xdt[:, pr * V7X_LANES:(pr + 1) * V7X_LANES]
                rhs = jnp.concatenate([jnp.where(first, xp, 0.0), jnp.where(first, 0.0, xp)], axis=0).astype(BF16)
                ys.append(jnp.dot(jnp.concatenate(ms, axis=1), rhs, preferred_element_type=F32))
            h_prev = h_ref[...]
            prev_ref[sub] = h_prev
            y = jnp.concatenate(ys, axis=1) + jnp.dot(cm, h_prev.astype(BF16), preferred_element_type=F32) * e_in
            y_ref[r, :] = y + xs * d_ref[...]
            st = lax.dot_general(bm, (xdt * eout_ref[r, :]).astype(BF16), TN, preferred_element_type=F32)
            h_ref[...] = e_in[CHUNK - 1:CHUNK, :] * h_prev + st

    return pl.pallas_call(
        body, name=name, grid=(g_n, nc // SSD_STEP_CHUNKS),
        out_shape=[SDS((t, di), F32), SDS((nc, g_n, D_STATE, w), F32)],
        in_specs=[sp["xs"], sp["bm"], sp["cm"], sp["xs"], sp["xs"], sp["xs"], sp["col"], sp["rowv"], sp["lanes"]],
        out_specs=[sp["xs"], sp["prev"]], scratch_shapes=[pltpu.VMEM((D_STATE, w), F32)],
        compiler_params=_cp("parallel", "arbitrary"),
    )(xbc, xbc, xbc, dt_x, ein_x, eout_x, a_col, a_row, d_x)


def _head_sums(v, seg):
    hi = v.astype(BF16)
    lo = (v - hi.astype(F32)).astype(BF16)
    return jnp.dot(hi, seg, preferred_element_type=F32) + jnp.dot(lo, seg, preferred_element_type=F32)


def _head_totals(v, seg):
    part = v[0:8]
    for r in range(8, v.shape[0], 8):
        part = part + v[r:r + 8]
    return jnp.sum(_head_sums(part, seg), axis=0, keepdims=True)


def _ssd_bwd(name, xbc, dt_x, ein_x, eout_x, a_col, a_row, d_x, prev, dy, di, g_n, hpg, p):
    t = xbc.shape[0]
    nc = t // CHUNK
    w = hpg * p
    sp = _ssd_specs(t, di, g_n, hpg, p, True)
    seg = (lax.broadcasted_iota(jnp.int32, (w, V7X_LANES), 0) // p
           == lax.broadcasted_iota(jnp.int32, (w, V7X_LANES), 1)).astype(BF16)

    def body(xs_ref, bm_ref, cm_ref, dt_ref, ein_ref, eout_ref, ac_ref, ar_ref, d_ref, prev_ref, dy_ref,
             seg_ref, dx_ref, dbm_ref, dcm_ref, ddt_ref, dacs_ref, dd_ref, dh_ref):
        @pl.when(pl.program_id(1) == 0)
        def _():
            dh_ref[...] = jnp.zeros_like(dh_ref)
            dd_ref[...] = jnp.zeros_like(dd_ref)

        first = lax.broadcasted_iota(jnp.int32, (1, V7X_LANES), 1) < p
        last_row = lax.broadcasted_iota(jnp.int32, (CHUNK, 1), 0) == CHUNK - 1
        seg_m = seg_ref[...]
        d_skip = d_ref[...]
        for sub in reversed(range(SSD_STEP_CHUNKS)):
            r = slice(sub * CHUNK, (sub + 1) * CHUNK)
            bm = bm_ref[r, :].astype(BF16)
            cm = cm_ref[r, :].astype(BF16)
            cb = lax.dot_general(cm, bm, NT, preferred_element_type=F32)
            xs, dy, e_in, e_out, dt_l = xs_ref[r, :], dy_ref[r, :], ein_ref[r, :], eout_ref[r, :], dt_ref[r, :]
            ac, ar = ac_ref[r, :], ar_ref[:, r]
            xdt = xs * dt_l
            h_prev = prev_ref[sub]
            h_prev_b = h_prev.astype(BF16)
            dh_next = dh_ref[...]
            dh_next_b = dh_next.astype(BF16)
            dy_e = (dy * e_in).astype(BF16)
            d_cm = lax.dot_general(dy_e, h_prev_b, NT, preferred_element_type=F32)
            dh_ref[...] = (e_in[CHUNK - 1:CHUNK, :] * dh_next
                           + lax.dot_general(cm, dy_e, TN, preferred_element_type=F32))
            q = jnp.dot(bm, dh_next_b, preferred_element_type=F32)
            xf = xdt * e_out
            d_bm = lax.dot_general(xf.astype(BF16), dh_next_b, NT, preferred_element_type=F32)
            d_cb = jnp.zeros((CHUNK, CHUNK), F32)
            parts, w_parts = [], []
            for pr in range(hpg // 2):
                lanes = slice(pr * V7X_LANES, (pr + 1) * V7X_LANES)
                lms, ms = _decay_masks(cb, ac, ar, (2 * pr, 2 * pr + 1))
                xp = xdt[:, lanes]
                xp_b = xp.astype(BF16)
                dyp = dy[:, lanes]
                halves = [jnp.where(first, dyp, 0.0).astype(BF16), jnp.where(first, 0.0, dyp).astype(BF16)]
                for lm, half in zip(lms, halves):
                    d_cb = d_cb + lax.dot_general(half, xp_b, NT, preferred_element_type=F32) * lm
                dxd = lax.dot_general(jnp.concatenate(ms, axis=0), jnp.concatenate(halves, axis=0), TN,
                                      preferred_element_type=F32)
                stacked = jnp.concatenate([jnp.where(first, xp, 0.0), jnp.where(first, 0.0, xp)], axis=0).astype(BF16)
                y_diag = jnp.dot(jnp.concatenate(ms, axis=1), stacked, preferred_element_type=F32)
                parts.append(dxd)
                w_parts.append(dyp.astype(BF16).astype(F32) * y_diag - xp_b.astype(F32) * dxd)
            d_xdt = jnp.concatenate(parts, axis=1) + q * e_out
            dx_ref[r, :] = d_xdt * dt_l + dy * d_skip
            ch = jnp.dot(cm, h_prev_b, preferred_element_type=F32)
            qx = q * xf
            s_a = _head_sums(dy * ch * e_in - qx + jnp.concatenate(w_parts, axis=1), seg_m)[:, :hpg]
            d_last = (_head_totals(qx, seg_m)[:, :hpg]
                      + jnp.exp(ac[CHUNK - 1:CHUNK, :]) * _head_totals(dh_next * h_prev, seg_m)[:, :hpg])
            ddt_ref[r, :] = _head_sums(d_xdt * xs, seg_m)[:, :hpg]
            dacs_ref[r, :] = s_a + jnp.where(last_row, d_last, 0.0)
            dd_ref[...] += _head_totals(dy * xs, seg_m)[:, :hpg]
            d_cb_b = d_cb.astype(BF16)
            dcm_ref[r, :] = d_cm + jnp.dot(d_cb_b, bm, preferred_element_type=F32)
            dbm_ref[r, :] = d_bm + lax.dot_general(d_cb_b, cm, TN, preferred_element_type=F32)

    gn = g_n * D_STATE
    return pl.pallas_call(
        body, name=name, grid=(g_n, nc // SSD_STEP_CHUNKS),
        out_shape=[SDS((t, di), F32), SDS((t, gn), F32), SDS((t, gn), F32), SDS((g_n, t, hpg), F32),
                   SDS((g_n, t, hpg), F32), SDS((g_n, 1, hpg), F32)],
        in_specs=[sp["xs"], sp["bm"], sp["cm"], sp["xs"], sp["xs"], sp["xs"], sp["col"], sp["rowv"],
                  sp["lanes"], sp["prev"], sp["xs"], sp["seg"]],
        out_specs=[sp["xs"], sp["bc"], sp["bc"], sp["col"], sp["col"], sp["head"]],
        scratch_shapes=[pltpu.VMEM((D_STATE, w), F32)],
        compiler_params=_cp("parallel", "arbitrary"),
    )(xbc, xbc, xbc, dt_x, ein_x, eout_x, a_col, a_row, d_x, prev, dy, seg)


def _as3d(a):
    return a.reshape(a.shape[0], -1, a.shape[-1])


def _pair_sum(name, own, recv, core):
    shape = recv.shape
    cols = shape[-1]
    own3, recv3 = own.reshape(8, -1, cols), recv.reshape(4, -1, cols)
    rows = recv3.shape[1]
    tr = _row_tile(rows, cols, 2)

    def body(c_ref, a_ref, b_ref, o_ref):
        o_ref[...] = (a_ref[...].astype(F32) + b_ref[...].astype(F32)).astype(o_ref.dtype)

    blk = pl.BlockSpec((None, tr, cols), lambda q, i, c_ref: (q, i, 0))
    out = pl.pallas_call(
        body, name=name, out_shape=SDS(recv3.shape, recv.dtype),
        grid_spec=pltpu.PrefetchScalarGridSpec(
            num_scalar_prefetch=1, grid=(4, rows // tr),
            in_specs=[pl.BlockSpec((None, tr, cols), lambda q, i, c_ref: (2 * q + c_ref[0], i, 0)), blk], out_specs=blk),
        compiler_params=_cp("parallel", "parallel"),
    )(core, own3, recv3)
    return out.reshape(shape)


def _adamw(name, w, m, v, parts, layer, prev=None, sel=None):
    lyr, rows, cols = w.shape
    n = len(parts)
    by_rows = rows % 16 == 0
    tr, tc = (_row_tile(rows, cols), cols) if by_rows else (rows, _pick(cols, 256))
    np_ = 0 if prev is None else 4
    if sel is None:
        sel = jnp.zeros((1,), jnp.int32)

    def body(sel_ref, *refs):
        w_ref, m_ref, v_ref = refs[:3]
        p_refs = refs[3:3 + n]
        g_ref, d_ref, nm_ref, nv_ref = refs[3 + n + np_:]
        g = p_refs[0][...].astype(F32)
        for r in p_refs[1:]:
            g = g + r[...].astype(F32)
        nm = ADAM_B1 * m_ref[...] + (1.0 - ADAM_B1) * g
        nv = ADAM_B2 * v_ref[...] + (1.0 - ADAM_B2) * (g * g)
        m_hat = nm / (1.0 - ADAM_B1 ** ADAM_STEP)
        v_hat = nv / (1.0 - ADAM_B2 ** ADAM_STEP)
        g_ref[...] = g
        d_ref[...] = -ADAM_LR * (m_hat / (jnp.sqrt(v_hat) + ADAM_EPS) + ADAM_WD * w_ref[...])
        nm_ref[...] = nm
        nv_ref[...] = nv

    def at(lead):
        return pl.BlockSpec((None, tr, tc), lambda i, s: (lead(s), i, 0) if by_rows else (lead(s), 0, i))

    lspec = at(lambda s: layer)
    pspecs = [at(lambda s: s[0]) if q is None else at(lambda s, q=q: q) for _, q in parts]
    aliases = {} if prev is None else {4 + n + q: q for q in range(4)}
    return pl.pallas_call(
        body, name=name, out_shape=[SDS(w.shape, F32)] * 4,
        grid_spec=pltpu.PrefetchScalarGridSpec(
            num_scalar_prefetch=1, grid=(rows // tr if by_rows else cols // tc,),
            in_specs=[lspec] * 3 + pspecs + [ANY] * np_, out_specs=[lspec] * 4),
        input_output_aliases=aliases, compiler_params=_cp("parallel"),
    )(sel, w, m, v, *[arr for arr, _ in parts], *(prev or ()))


def _sum8(name, parts):
    rows = parts.shape[1]

    def body(p_ref, o_ref):
        s = p_ref[0]
        for q in range(1, N_DEV):
            s = s + p_ref[q]
        o_ref[...] = s

    return pl.pallas_call(
        body, name=name, grid=(1,), out_shape=SDS((rows, V7X_LANES), F32),
        in_specs=[pl.BlockSpec((N_DEV, rows, V7X_LANES), lambda i: (0, 0, 0))],
        out_specs=pl.BlockSpec((rows, V7X_LANES), lambda i: (0, 0)), compiler_params=_cp("arbitrary"),
    )(parts)


def _pack(vectors, align):
    flat = jnp.concatenate([v.reshape(-1) for v in vectors])
    pad = (-flat.shape[0]) % align
    if pad:
        flat = jnp.concatenate([flat, jnp.zeros((pad,), F32)])
    return flat.reshape(-1, V7X_LANES)


def _unpack(packed, shapes):
    flat = packed.reshape(-1)
    out, o = [], 0
    for s in shapes:
        size = 1
        for dim in s:
            size *= dim
        out.append(flat[o:o + size].reshape(s))
        o += size
    return out


def _residual_epilogue(acc, res):
    return (ALPHA * res + acc,)


def _plain_add_epilogue(acc, res):
    return (res + acc,)


def _gate_epilogue(acc, y, e):
    gate = _sigmoid(acc)
    xn = y + gate * e
    return xn, gate, xn


def _relu2(pre):
    r = jnp.maximum(pre, 0.0)
    return r * r


def _relu2_bwd_epilogue(acc, pre):
    return (acc * (2.0 * jnp.maximum(pre.astype(F32), 0.0)),)


def _tail_fwd(tag, u_a, wts, lng, lnb, p_l, finish_w1, finish_w2, after=()):
    y1, y1_b = _ln_fwd(f"ln1_{tag}", u_a, lng[0], lnb[0], after)
    finish_w1(y1_b)
    (pre,) = _mm_fwd(f"mlp1_{tag}", y1_b, wts["w1"], "col", [BF16])
    finish_w2(pre)
    (u_b,) = _mm_fwd(f"mlp2_{tag}", pre, wts["w2"], "row", [F32], _residual_epilogue, (y1,), a_fn=_relu2)
    y2, y2_b = _ln_fwd(f"ln2_{tag}", u_b, lng[1], lnb[1])
    (e,) = _mm_fwd(f"ple_{tag}", p_l, wts["plew"], "col", [F32])
    xn, gate, xn_b = _mm_fwd(f"gate_{tag}", y2_b, wts["gate"], "row", [F32, F32, BF16], _gate_epilogue, (y2, e))
    return xn, xn_b, (u_a, y1_b, pre, u_b, y2_b, e, gate)


def _tail_bwd(tag, dxn, saved, wts, lng, p_l, emit, advance, toks):
    u_a, y1_b, pre, u_b, y2_b, e, gate = saved
    dgpre, de = _ple_bwd(f"ple_bwd_{tag}", dxn, e, gate)
    toks = emit(f"{tag}_ple", dict(gate=_mm_dw(f"gate_dw_{tag}", y2_b, dgpre, "row", after=toks),
                                   plew=_mm_dw(f"ple_dw_{tag}", p_l, de, "col")))
    (dy2,) = _mm_dx(f"gate_dx_{tag}", dgpre, wts["gate"], "row", [F32], _plain_add_epilogue, (dxn,), after=toks)
    toks = advance((dy2,))
    du_b, du_b16, dg2, db2 = _ln_bwd(f"ln2_bwd_{tag}", u_b, dy2, lng[1])
    toks = emit(f"{tag}_w2", dict(w2=_mm_dw(f"mlp2_dw_{tag}", pre, du_b16, "row", a_fn=_relu2, after=toks)))
    (dpre,) = _mm_dx(f"mlp2_dx_{tag}", du_b16, wts["w2"], "row", [BF16], _relu2_bwd_epilogue, (pre,), after=toks)
    toks = advance((dpre,))
    toks = emit(f"{tag}_w1", dict(w1=_mm_dw(f"mlp1_dw_{tag}", y1_b, dpre, "col", after=toks)))
    (dy1,) = _mm_dx(f"mlp1_dx_{tag}", dpre, wts["w1"], "col", [F32], _residual_epilogue, (du_b,), after=toks)
    toks = advance((dy1,))
    du_a, du_a16, dg1, db1 = _ln_bwd(f"ln1_bwd_{tag}", u_a, dy1, lng[0])
    return du_a, du_a16, [dg1, dg2], [db1, db2], toks


def _to_slots(a, axis):
    shape = a.shape
    per = shape[axis] // N_DEV
    v = a.reshape(shape[:axis] + (N_DEV, per) + shape[axis + 1:])
    return jnp.moveaxis(v, axis, 0)


def _pad_lanes(a):
    return jnp.pad(a, [(0, 0)] * (a.ndim - 1) + [(0, V7X_LANES - a.shape[-1])])


def kernel(x, p, pool_w, pool_scale, ssm_in_w, ssm_conv_w, ssm_conv_b, ssm_dt_bias, ssm_a_log, ssm_d, ssm_norm_w, ssm_out_w, mlp_w1, mlp_w2, ln_g, ln_b, ple_w, ple_gate_w, loss_target, m_pool_w, m_pool_scale, m_ssm_in_w, m_ssm_conv_w, m_ssm_conv_b, m_ssm_dt_bias, m_ssm_a_log, m_ssm_d, m_ssm_norm_w, m_ssm_out_w, m_mlp_w1, m_mlp_w2, m_ln_g, m_ln_b, m_ple_w, m_ple_gate_w, v_pool_w, v_pool_scale, v_ssm_in_w, v_ssm_conv_w, v_ssm_conv_b, v_ssm_dt_bias, v_ssm_a_log, v_ssm_d, v_ssm_norm_w, v_ssm_out_w, v_mlp_w1, v_mlp_w2, v_ln_g, v_ln_b, v_ple_w, v_ple_gate_w):
    t, d = x.shape[1:]
    h_n = ssm_dt_bias.shape[-1]
    di_s, cd_s, dp_s, d_s = ssm_norm_w.shape[-1], ssm_conv_b.shape[-1], ssm_in_w.shape[-1], ln_g.shape[-1]
    di, cd, dp = N_DEV * di_s, N_DEV * cd_s, N_DEV * dp_s
    p_dim = di // h_n
    g_n = (cd - di) // (2 * D_STATE)
    hpg = h_n // g_n
    zw = di + cd
    assert h_n <= V7X_LANES and dp == zw + h_n and zw % V7X_LANES == 0 and zw % h_n == 0
    cg = d // 4
    me = 4 * lax.axis_index("x") + 2 * lax.axis_index("y") + lax.axis_index("c")

    x0, target = x[0], loss_target[0]
    p_l = [p[0, 0].astype(BF16), p[1, 0].astype(BF16)]

    small_shapes = [(CONV_WIDTH, cd_s), (1, cd_s), (1, di_s), (2, 2, d_s), (2, 2, d_s)]
    small = _pack([ssm_conv_w[0], ssm_conv_b, ssm_norm_w, ln_g, ln_b], 8 * V7X_LANES)
    first = [w.astype(BF16) for w in (pool_w[0], mlp_w1[0])]

    def gather_start(tag, own, after):
        lands = [lax.empty((N_DEV,) + w.shape, w.dtype) for w in own]
        return _async_start(f"ag_{tag}_start", _ag_first_copies, (4 * len(own),), own, lands, after)

    def gather_pass(tag, handle, after):
        n = len(handle[2]) // 2
        own, lands = _async_wait(f"ag_{tag}_wait", _ag_first_copies, handle, n, after)
        return _async_start(f"ag_{tag}_forward_start", _ag_forward_copies, (3 * n,), [], lands), own

    def gather_done(tag, passed, after=()):
        fwd, own = passed
        _, lands = _async_wait(f"ag_{tag}_forward_wait", _ag_forward_copies, fwd, 0, after)
        return [lax.dynamic_update_slice_in_dim(g, w[None], me, 0) for g, w in zip(lands, own)]

    def gather_finish(tag, handle, after):
        return gather_done(tag, gather_pass(tag, handle, after))

    ag_first = gather_start("first", first + [small], ())
    zero = ag_first[3][0, 0]
    own_l0 = [(w + zero).astype(BF16) for w in (mlp_w2[0], ple_w[0], ple_gate_w[0])]
    own_ssm = [(ssm_in_w[0] + zero).astype(BF16).T]
    own_out = [(ssm_out_w[0] + zero).astype(BF16)]
    own_mlp = [(w + zero).astype(BF16) for w in (mlp_w1[1], mlp_w2[1], ple_w[1], ple_gate_w[1])]
    ag_l0 = gather_start("l0", own_l0, (ag_first[3],))
    ag_ssm = gather_start("ssm", own_ssm, (ag_l0[3],))
    ag_out = gather_start("out", own_out, (ag_ssm[3],))
    ag_mlp = gather_start("mlp1", own_mlp, (ag_out[3],))
    pooled = _pool_windows("pool_fwd", x0, False, after=(ag_mlp[3],))
    pool_g, w1_0, small_g = gather_finish("first", ag_first, (pooled,))
    pool_full = pool_g.transpose(1, 0, 2, 3).reshape(4, cg, cg)
    sm = small_g.reshape(N_DEV, -1)
    o = 0
    parts = []
    for shp in small_shapes:
        size = 1
        for s in shp:
            size *= s
        parts.append(sm[:, o:o + size].reshape((N_DEV,) + shp))
        o += size
    conv_w_full = parts[0].transpose(1, 0, 2).reshape(CONV_WIDTH, cd)
    conv_b_full = parts[1].transpose(1, 0, 2).reshape(1, cd)
    norm_w_full = parts[2].transpose(1, 0, 2).reshape(1, di)
    ln_g_full = parts[3].transpose(1, 2, 0, 3).reshape(2, 2, 1, d)
    ln_b_full = parts[4].transpose(1, 2, 0, 3).reshape(2, 2, 1, d)
    bias128, alog128 = _pad_lanes(ssm_dt_bias), _pad_lanes(ssm_a_log)

    u0, hraw = _pool_mm("pool_mm", pooled, pool_full, pool_scale, x0)
    wts = [dict(w1=w1_0)]

    def finish_l0(after):
        w2_0, plew_0, gate_0 = gather_finish("l0", ag_l0, (after,))
        wts[0].update(w2=w2_0, plew=plew_0, gate=gate_0)

    x1, x1_b, saved0 = _tail_fwd("l0", u0, wts[0], ln_g_full[0], ln_b_full[0], p_l[0], lambda after: None, finish_l0)

    (in_g,) = gather_finish("ssm", ag_ssm, (x1,))
    in_t = in_g.reshape(dp, d)
    (zx,) = _mm_dx("in_proj", x1_b, in_t, "plain", [F32], k_rows=zw)
    dt_raw = _pad_lanes(_in_proj_dt("in_proj_dt", x1_b, in_t, zw, h_n))
    xbc = _conv_fwd("conv_fwd", zx, conv_w_full, conv_b_full, di)
    dt, acs, e_in, e_out = _dt_fwd("dt_fwd", dt_raw, bias128, alog128, 0)

    dt_x, ein_x, eout_x = _expand_heads("expand_heads", [dt, e_in, e_out], h_n, p_dim)
    d_x = jnp.repeat(ssm_d, p_dim, axis=1)

    def to_col(a):
        return a[:, :h_n].reshape(t, g_n, hpg).transpose(1, 0, 2)

    def to_row(a):
        return a[:, :h_n].reshape(t, g_n, hpg).transpose(1, 2, 0)

    def from_col(a):
        return _pad_lanes(a.transpose(1, 0, 2).reshape(t, h_n))

    a_col, a_row = to_col(acs), to_row(acs)
    y_ssd, prev = _ssd_fwd("ssd_fwd", xbc, dt_x, ein_x, eout_x, a_col, a_row, d_x, di, g_n, hpg, p_dim)
    out_passed = gather_pass("out", ag_out, (y_ssd,))
    yn = _gated_rms_fwd("gated_rms_fwd", y_ssd, zx, norm_w_full, after=(out_passed[0][3],))
    (out_g,) = gather_done("out", out_passed, (yn,))
    (u2,) = _mm_fwd("out_proj", yn, out_g, "row", [F32], _residual_epilogue, (x1,))
    mlp_passed = gather_pass("mlp1", ag_mlp, (u2,))
    wts.append({})

    def finish_l1(after):
        wts[1].update(zip(("w1", "w2", "plew", "gate"), gather_done("mlp1", mlp_passed, (after,))))

    x2, _, saved1 = _tail_fwd("l1", u2, wts[1], ln_g_full[1], ln_b_full[1], p_l[1], finish_l1, lambda after: None,
                              after=(mlp_passed[0][3],))

    core = lax.axis_index("c").astype(jnp.int32).reshape(1)
    chip = (2 * lax.axis_index("x") + lax.axis_index("y")).astype(jnp.int32).reshape(1)
    scattering = {}
    pending = []

    def to_chips(after):
        if not pending:
            return []
        tag, names, handle = pending.pop()
        own, halves = _async_wait(f"rs_{tag}_sibling_wait", _rs_sibling_copies, handle, len(names), after)
        sums = [_pair_sum(f"rs_{tag}_pair_sum_{n}", g, hv, core) for n, g, hv in zip(names, own, halves)]
        lands = [lax.empty((3,) + s.shape[1:], s.dtype) for s in sums]
        handle = _async_start(f"rs_{tag}_start", _rs_chip_copies, (3 * len(sums),), sums, lands)
        scattering[tag] = (names, handle)
        return [handle[3]]

    def emit(tag, grads):
        names, arrays = list(grads), list(grads.values())
        toks = to_chips(tuple(arrays))
        lands = [lax.empty((4,) + g.shape[1:], g.dtype) for g in arrays]
        handle = _async_start(f"rs_{tag}_sibling_start", _rs_sibling_copies, (4 * len(arrays),), arrays, lands,
                              tuple(toks))
        pending.append((tag, names, handle))
        return toks + [handle[3]]

    def collect(tag, after):
        names, handle = scattering.pop(tag)
        sums, thirds = _async_wait(f"rs_{tag}_wait", _rs_chip_copies, handle, len(names), after)
        return {n: (s.reshape(4, -1, s.shape[-1]), r.reshape(3, -1, r.shape[-1])) for n, s, r in zip(names, sums, thirds)}

    dx2, loss_cols = _loss_bwd("loss", x2, target)
    du2, du2_b, dg_1, db_1, toks = _tail_bwd("l1", dx2, saved1, wts[1], ln_g_full[1], p_l[1], emit, to_chips, [])
    toks = emit("ssm_out", dict(out=_mm_dw("out_proj_dw", yn, du2_b, "row", after=toks)))
    (dyn,) = _mm_dx("out_proj_dx", du2_b, out_g, "row", [F32], after=toks)
    toks = to_chips((dyn,))
    dy_ssd, dz, d_norm_w = _gated_rms_bwd("gated_rms_bwd", y_ssd, zx, norm_w_full, dyn)
    dxs, dbm, dcm, ddt_x, dacs, dd = _ssd_bwd("ssd_bwd", xbc, dt_x, ein_x, eout_x, a_col, a_row, d_x, prev, dy_ssd,
                                              di, g_n, hpg, p_dim)
    draw, d_bias, d_alog = _dt_bwd("dt_bwd", dt_raw, bias128, alog128, from_col(dacs), from_col(ddt_x), 0)
    conv_parts = [_conv_bwd(f"conv_bwd_{tag}", zx, conv_w_full, conv_b_full, dact, di, first)
                  for tag, dact, first in (("xs", dxs, 0), ("b", dbm, di), ("c", dcm, di + g_n * D_STATE))]
    d_conv_w = jnp.concatenate([c[1] for c in conv_parts], axis=1)
    d_conv_b = jnp.concatenate([c[2] for c in conv_parts], axis=1)
    dzx = jnp.concatenate([dz] + [c[0] for c in conv_parts], axis=1)
    d_dt = draw[:, :h_n]
    g_in_t = _in_proj_dw_t("in_proj_dw", dzx, d_dt, x1_b, after=toks)
    toks = emit("ssm_in", {"in": g_in_t.reshape(N_DEV, dp_s, d)})
    dx_dt = _in_proj_dt_dx("in_proj_dt_dx", d_dt, in_t, zw, du2)
    (dx1,) = _mm_fwd("in_proj_dx", dzx, in_t, "plain", [F32], _plain_add_epilogue, (dx_dt,), after=toks, k_rows=zw)
    toks = to_chips((dx1,))

    du0, _, dg_0, db_0, toks = _tail_bwd("l0", dx1, saved0, wts[0], ln_g_full[0], p_l[0], emit, to_chips, toks)
    dh, dpool, d_scale = _pool_bwd_mm("pool_bwd_mm", du0, hraw, pool_full, pool_scale)
    toks += emit("pool", dict(pool=_to_slots(_pool_dw("pool_dw", pooled, dh), 1)))
    grad_x = _pool_windows("pool_bwd", dpool, True, du0, after=tuple(toks))
    toks = to_chips((grad_x,))

    def update(tag, w, m, v, parts, layer, prev=None):
        own, recv = parts
        return _adamw(f"adamw_{tag}_{layer}", _as3d(w), _as3d(m), _as3d(v),
                      [(own, None), (recv, 0), (recv, 1), (recv, 2)], layer, prev, chip)

    q = collect("l1_ple", (grad_x, *toks))
    r_gate = update("ple_gate_w", ple_gate_w, m_ple_gate_w, v_ple_gate_w, q["gate"], 1)
    r_plew = update("ple_w", ple_w, m_ple_w, v_ple_w, q["plew"], 1)
    r_w2 = update("mlp_w2", mlp_w2, m_mlp_w2, v_mlp_w2, collect("l1_w2", (r_gate[0],))["w2"], 1)
    r_w1 = update("mlp_w1", mlp_w1, m_mlp_w1, v_mlp_w1, collect("l1_w1", (r_w2[0],))["w1"], 1)
    r_out = update("ssm_out_w", ssm_out_w, m_ssm_out_w, v_ssm_out_w, collect("ssm_out", (r_w1[0],))["out"], 0)
    in_wt, in_mt, in_vt = [jnp.swapaxes(a, 1, 2) for a in (ssm_in_w, m_ssm_in_w, v_ssm_in_w)]
    r_in = update("ssm_in_w", in_wt, in_mt, in_vt, collect("ssm_in", (r_out[0],))["in"], 0)

    d_ln_g = jnp.stack([jnp.stack(dg_0), jnp.stack(dg_1)]).reshape(2, 2, d)
    d_ln_b = jnp.stack([jnp.stack(db_0), jnp.stack(db_1)]).reshape(2, 2, d)
    partial_shapes = [(CONV_WIDTH, cd), (1, cd), (1, di), (2, 2, d), (2, 2, d), (1, d), (1, h_n), (1, h_n), (1, h_n),
                      (1, d)]
    partial = _pack([d_conv_w, d_conv_b, d_norm_w, d_ln_g, d_ln_b, d_scale, d_bias[:, :h_n], d_alog[:, :h_n],
                     dd.reshape(1, h_n), loss_cols], 8 * V7X_LANES)
    (all_partials,) = _all_gather("ag_small_grads", [partial], after=(r_in[0],))
    tot = _unpack(_sum8("sum_small_grads", all_partials), partial_shapes)
    t_conv_w, t_conv_b, t_norm_w, t_ln_g, t_ln_b, t_scale, t_bias, t_alog, t_dd, t_loss = tot
    loss = jnp.sum(t_loss)

    def mine(a, per):
        return lax.dynamic_slice_in_dim(a, me * per, per, axis=a.ndim - 1)

    small_names = ["ssm_conv_w", "ssm_conv_b", "ssm_norm_w", "ln_g", "ln_b", "pool_scale", "ssm_dt_bias", "ssm_a_log",
                   "ssm_d"]
    small_w = [ssm_conv_w, ssm_conv_b, ssm_norm_w, ln_g, ln_b, pool_scale, ssm_dt_bias, ssm_a_log, ssm_d]
    small_m = [m_ssm_conv_w, m_ssm_conv_b, m_ssm_norm_w, m_ln_g, m_ln_b, m_pool_scale, m_ssm_dt_bias, m_ssm_a_log,
               m_ssm_d]
    small_v = [v_ssm_conv_w, v_ssm_conv_b, v_ssm_norm_w, v_ln_g, v_ln_b, v_pool_scale, v_ssm_dt_bias, v_ssm_a_log,
               v_ssm_d]
    small_grads = [mine(t_conv_w, cd_s), mine(t_conv_b, cd_s), mine(t_norm_w, di_s), mine(t_ln_g, d_s),
                   mine(t_ln_b, d_s), t_scale, t_bias, t_alog, t_dd]
    shapes = [w.shape for w in small_w]
    pk = [_pack(group, 8 * V7X_LANES)[None] for group in (small_w, small_m, small_v, small_grads)]
    res = _adamw("adamw_small", pk[0], pk[1], pk[2], [(pk[3], 0)], 0)
    upd = {}
    for name, vals in zip(small_names, zip(*[_unpack(r, shapes) for r in res])):
        upd[name] = list(vals)

    q = collect("l0_ple", (res[0],))
    r_gate = update("ple_gate_w", ple_gate_w, m_ple_gate_w, v_ple_gate_w, q["gate"], 0, r_gate)
    r_plew = update("ple_w", ple_w, m_ple_w, v_ple_w, q["plew"], 0, r_plew)
    r_w2 = update("mlp_w2", mlp_w2, m_mlp_w2, v_mlp_w2, collect("l0_w2", (r_gate[0],))["w2"], 0, r_w2)
    r_w1 = update("mlp_w1", mlp_w1, m_mlp_w1, v_mlp_w1, collect("l0_w1", (r_w2[0],))["w1"], 0, r_w1)
    r_pool = update("pool_w", pool_w, m_pool_w, v_pool_w, collect("pool", (r_w1[0],))["pool"], 0)
    assert not scattering
    large = {"pool_w": (pool_w, r_pool), "ssm_in_w": (in_wt, r_in), "ssm_out_w": (ssm_out_w, r_out),
             "mlp_w1": (mlp_w1, r_w1), "mlp_w2": (mlp_w2, r_w2), "ple_w": (ple_w, r_plew),
             "ple_gate_w": (ple_gate_w, r_gate)}
    for name, (w, rs) in large.items():
        upd[name] = [r.reshape(w.shape) for r in rs]
    upd["ssm_in_w"] = [jnp.swapaxes(r, 1, 2) for r in upd["ssm_in_w"]]

    order = ["pool_w", "pool_scale", "ssm_in_w", "ssm_conv_w", "ssm_conv_b", "ssm_dt_bias", "ssm_a_log", "ssm_d",
             "ssm_norm_w", "ssm_out_w", "mlp_w1", "mlp_w2", "ln_g", "ln_b", "ple_w", "ple_gate_w"]
    out = [loss, grad_x[None]]
    for k in range(4):
        out += [upd[name][k] for name in order]
    return tuple(out)
```

```python
import jax
import jax.numpy as jnp
from jax import lax
from jax.experimental import pallas as pl
from jax.experimental.pallas import tpu as pltpu

F32 = jnp.float32
BF16 = jnp.bfloat16
SDS = jax.ShapeDtypeStruct
MESH = pl.DeviceIdType.MESH
ANY = pl.BlockSpec(memory_space=pl.ANY)

N_DEV = 8
DEPTH = 2
ALPHA = (2.0 * DEPTH) ** 0.25
LN_EPS = 1e-5
RMS_EPS = 1e-5
POOL_WINDOW_LOG2 = (1, 2, 3, 4)
D_STATE = 128
CHUNK = 128
SSD_STEP_CHUNKS = 4
CONV_WIDTH = 4
ADAM_LR = 0.001
ADAM_B1 = 0.9
ADAM_B2 = 0.999
ADAM_EPS = 1e-08
ADAM_WD = 0.01
ADAM_STEP = 10

V7X_LANES = 128
V7X_VMEM_LIMIT = 48 * 1024 * 1024


def _cp(*sem):
    return pltpu.CompilerParams(dimension_semantics=sem, vmem_limit_bytes=V7X_VMEM_LIMIT)


def _pick(dim, cap):
    if dim <= cap:
        return dim
    best = None
    for t in range(V7X_LANES, cap + 1, V7X_LANES):
        if dim % t == 0:
            best = t
    assert best is not None, (dim, cap)
    return best


def _row_tile(rows, cols, itemsize=4, target=1 << 20):
    t = rows
    while t % 2 == 0 and t // 2 >= 16 and (t // 2) % 16 == 0 and t * cols * itemsize > target:
        t //= 2
    return t


def _all_gather(name, shards, after=()):
    n, na = len(shards), len(after)

    def body(*refs):
        ins, outs = refs[:n], refs[n + na:2 * n + na]
        send_sems, recv_sems, local_sems = refs[2 * n + na:]
        x, y, c = lax.axis_index("x"), lax.axis_index("y"), lax.axis_index("c")
        me, sibling = (x, y, c), (x, y, 1 - c)
        chips = [(1 - x, y), (x, 1 - y), (1 - x, 1 - y)]

        def copy(a, k, block, to, src=None):
            dst = outs[a].at[4 * block[0] + 2 * block[1] + block[2]]
            return pltpu.make_async_remote_copy(
                src_ref=dst if src is None else src, dst_ref=dst, send_sem=send_sems.at[a, k],
                recv_sem=recv_sems.at[a, k], device_id=to, device_id_type=MESH)

        mine = [pltpu.make_async_copy(ins[a], outs[a].at[4 * x + 2 * y + c], local_sems.at[a]) for a in range(n)]
        for cp in mine:
            cp.start()
        first = []
        for a in range(n):
            first.append(copy(a, 0, me, sibling, src=ins[a]))
            first += [copy(a, 1 + j, me, (*chip, c), src=ins[a]) for j, chip in enumerate(chips)]
        for cp in first:
            cp.start()
        passed = []
        for j, chip in enumerate(chips):
            for a in range(n):
                copy(a, 1 + j, (*chip, c), me).wait_recv()
                fwd = copy(a, 4 + j, (*chip, c), sibling)
                fwd.start()
                passed.append(fwd)
        for a in range(n):
            copy(a, 0, sibling, me).wait_recv()
            for j, chip in enumerate(chips):
                copy(a, 4 + j, (*chip, 1 - c), me).wait_recv()
        for cp in first + passed:
            cp.wait_send()
        for cp in mine:
            cp.wait()

    return pl.pallas_call(
        body, name=name,
        out_shape=[SDS((N_DEV,) + s.shape, s.dtype) for s in shards],
        in_specs=[ANY] * (n + na), out_specs=[ANY] * n,
        scratch_shapes=[pltpu.SemaphoreType.DMA((n, 7)), pltpu.SemaphoreType.DMA((n, 7)),
                        pltpu.SemaphoreType.DMA((n,))],
    )(*shards, *after)


HBM_SPEC = pl.BlockSpec(memory_space=pltpu.HBM)
SEM_SPEC = pl.BlockSpec(memory_space=pltpu.SEMAPHORE)
EFFECT = pltpu.SideEffectType.DATAFLOW_SIDE_EFFECTING


def _ag_first_copies(ins, lands, send_sems, recv_sems):
    x, y, c = lax.axis_index("x"), lax.axis_index("y"), lax.axis_index("c")
    targets = [(x, y, 1 - c), (1 - x, y, c), (x, 1 - y, c), (1 - x, 1 - y, c)]
    return [pltpu.make_async_remote_copy(
        src_ref=ins[a], dst_ref=lands[a].at[4 * x + 2 * y + c], send_sem=send_sems.at[4 * a + k],
        recv_sem=recv_sems.at[4 * a + k], device_id=to, device_id_type=MESH)
        for a in range(len(ins)) for k, to in enumerate(targets)]


def _ag_forward_copies(ins, lands, send_sems, recv_sems):
    x, y, c = lax.axis_index("x"), lax.axis_index("y"), lax.axis_index("c")
    cps = []
    for a in range(len(lands)):
        for j, (px, py) in enumerate([(1 - x, y), (x, 1 - y), (1 - x, 1 - y)]):
            blk = lands[a].at[4 * px + 2 * py + c]
            cps.append(pltpu.make_async_remote_copy(
                src_ref=blk, dst_ref=blk, send_sem=send_sems.at[3 * a + j], recv_sem=recv_sems.at[3 * a + j],
                device_id=(x, y, 1 - c), device_id_type=MESH))
    return cps


def _rs_sibling_copies(ins, lands, send_sems, recv_sems):
    x, y, c = lax.axis_index("x"), lax.axis_index("y"), lax.axis_index("c")
    return [pltpu.make_async_remote_copy(
        src_ref=ins[a].at[2 * q + 1 - c], dst_ref=lands[a].at[q], send_sem=send_sems.at[4 * a + q],
        recv_sem=recv_sems.at[4 * a + q], device_id=(x, y, 1 - c), device_id_type=MESH)
        for a in range(len(ins)) for q in range(4)]


def _rs_chip_copies(ins, lands, send_sems, recv_sems):
    x, y, c = lax.axis_index("x"), lax.axis_index("y"), lax.axis_index("c")
    cps = []
    for a in range(len(ins)):
        for j, (px, py) in enumerate([(1 - x, y), (x, 1 - y), (1 - x, 1 - y)]):
            cps.append(pltpu.make_async_remote_copy(
                src_ref=ins[a].at[2 * px + py], dst_ref=lands[a].at[j], send_sem=send_sems.at[3 * a + j],
                recv_sem=recv_sems.at[3 * a + j], device_id=(px, py, c), device_id_type=MESH))
    return cps


def _async_start(name, build, sem_shape, ins, lands, after=()):
    arrays = [*ins, *lands]
    n_i, n_t, n_a = len(ins), len(arrays), len(after)

    def body(*refs):
        outs = refs[n_t + n_a:]
        for cp in build(refs[:n_i], refs[n_i:n_t], outs[0], outs[1]):
            cp.start()
        outs[-1][...] = jnp.zeros_like(outs[-1])

    res = pl.pallas_call(
        body, name=name,
        out_shape=(pltpu.SemaphoreType.DMA(sem_shape), pltpu.SemaphoreType.DMA(sem_shape),
                   *[pltpu.HBM(a.shape, a.dtype) for a in arrays], SDS((8, V7X_LANES), F32)),
        in_specs=[HBM_SPEC] * n_t + [ANY] * n_a,
        out_specs=(SEM_SPEC, SEM_SPEC, *[HBM_SPEC] * n_t, pl.BlockSpec(memory_space=pltpu.VMEM)),
        input_output_aliases={i: 2 + i for i in range(n_t)},
        compiler_params=pltpu.CompilerParams(has_side_effects=EFFECT),
    )(*[pltpu.with_memory_space_constraint(a, pltpu.HBM) for a in arrays], *after)
    return res[0], res[1], list(res[2:2 + n_t]), res[-1]


def _async_wait(name, build, handle, n_i, after=()):
    send_sems, recv_sems, arrays, _ = handle
    n_t, n_a = len(arrays), len(after)

    def body(*refs):
        for cp in build(refs[:n_i], refs[n_i:n_t], refs[n_t], refs[n_t + 1]):
            cp.wait_send()
            cp.wait_recv()

    res = pl.pallas_call(
        body, name=name, out_shape=tuple(pltpu.HBM(a.shape, a.dtype) for a in arrays),
        in_specs=[HBM_SPEC] * n_t + [SEM_SPEC, SEM_SPEC] + [ANY] * n_a, out_specs=tuple([HBM_SPEC] * n_t),
        input_output_aliases={i: i for i in range(n_t)},
        compiler_params=pltpu.CompilerParams(has_side_effects=EFFECT),
    )(*arrays, send_sems, recv_sems, *after)
    return list(res[:n_i]), list(res[n_i:])


def _mm_core(name, a, b, *, grid, a_spec, b_spec, dims, acc_shape, outs, out_spec, epilogue=None, extras=(),
             extra_specs=(), a_fn=None, after=(), carry=None):
    nk = grid[2]
    if carry is not None:
        after = (*after, carry)
    ne, no, na = len(extras), len(outs), len(after)

    def body(a_ref, b_ref, *rest):
        e_refs, o_refs, acc = rest[:ne], rest[ne + na:ne + na + no], rest[ne + na + no]
        k = pl.program_id(2)

        def product():
            lhs = a_ref[...] if a_fn is None else a_fn(a_ref[...])
            return lax.dot_general(lhs.astype(BF16), b_ref[...].astype(BF16), dims, preferred_element_type=F32)

        @pl.when(k == 0)
        def _():
            acc[...] = product()

        @pl.when(k > 0)
        def _():
            acc[...] += product()

        @pl.when(k == nk - 1)
        def _():
            r = acc[...]
            vals = epilogue(r, *[e[...] for e in e_refs]) if epilogue is not None else (r,)
            for o, v in zip(o_refs, vals):
                o[...] = v.astype(o.dtype)

    res = pl.pallas_call(
        body, name=name, grid=grid, out_shape=list(outs),
        in_specs=[a_spec, b_spec, *extra_specs, *[ANY] * na], out_specs=[out_spec] * no,
        scratch_shapes=[pltpu.VMEM(acc_shape, F32)],
        input_output_aliases={} if carry is None else {1 + ne + na: 0},
        compiler_params=_cp("parallel", "parallel", "arbitrary"),
    )(a, b, *extras, *after)
    return res


NN = (((1,), (0,)), ((), ()))
NT = (((1,), (1,)), ((), ()))
TN = (((0,), (0,)), ((), ()))


def _w_dims(w, kind):
    if kind == "col":
        return w.shape[1], N_DEV * w.shape[2], w.shape[1], w.shape[2]
    if kind == "row":
        return N_DEV * w.shape[1], w.shape[2], w.shape[1], w.shape[2]
    return w.shape[0], w.shape[1], w.shape[0], w.shape[1]


MM_VMEM_BUDGET = 40 * 1024 * 1024


def _blocks(m, n_len, n_caps, k_len, k_caps, a, out_dtypes, extras):
    per_out = sum(jnp.dtype(dt).itemsize for dt in out_dtypes) + sum(e.dtype.itemsize for e in extras)
    best = None
    for tm in (_pick(m, 2048), _pick(m, 1024), _pick(m, 512)):
        for tn in [_pick(n_len, cap) for cap in n_caps]:
            for tk in [_pick(k_len, cap) for cap in k_caps]:
                used = tm * tn * (4 + 2 * per_out) + 2 * (tm * tk * a.dtype.itemsize + tk * tn * 2)
                key = ((m // tm) * (n_len // tn) * (k_len // tk), -tm, -tn)
                if used <= MM_VMEM_BUDGET and (best is None or key < best[0]):
                    best = (key, tm, tn, tk)
    assert best is not None, (m, n_len, k_len)
    return best[1:]


def _mm_fwd(name, a, w, kind, out_dtypes, epilogue=None, extras=(), a_fn=None, after=(), k_rows=None):
    if kind == "row":
        w, kind = w.reshape(-1, w.shape[-1]), "plain"
    m = a.shape[0]
    kk, n, ks, ns = _w_dims(w, kind)
    if k_rows is None:
        assert kk == a.shape[1]
    else:
        assert kind == "plain" and k_rows <= min(kk, a.shape[1])
        kk = k_rows
    if kind == "col":
        tm, tn, tk = _blocks(m, ns, (1024,), kk, (1024, 512), a, out_dtypes, extras)
    else:
        tm, tn, tk = _blocks(m, n, (2048, 1152, 1024), kk, (1024, 512), a, out_dtypes, extras)
    if kind == "col":
        nb = ns // tn
        b_spec = pl.BlockSpec((None, tk, tn), lambda i, j, k: (j // nb, k, j % nb))
    else:
        b_spec = pl.BlockSpec((tk, tn), lambda i, j, k: (k, j))
    mn_spec = pl.BlockSpec((tm, tn), lambda i, j, k: (i, j))
    return _mm_core(
        name, a, w, grid=(m // tm, n // tn, kk // tk),
        a_spec=pl.BlockSpec((tm, tk), lambda i, j, k: (i, k)), b_spec=b_spec, dims=NN, acc_shape=(tm, tn),
        outs=[SDS((m, n), dt) for dt in out_dtypes], out_spec=mn_spec, epilogue=epilogue, extras=extras,
        extra_specs=[mn_spec] * len(extras), a_fn=a_fn, after=after)


def _mm_dx(name, dy, w, kind, out_dtypes, epilogue=None, extras=(), after=(), k_rows=None):
    if kind == "row":
        w, kind = w.reshape(-1, w.shape[-1]), "plain"
    m, n_dim = dy.shape
    kk, n, ks, ns = _w_dims(w, kind)
    assert n == n_dim
    if k_rows is not None:
        assert kind == "plain" and k_rows <= kk
        kk = k_rows
    if kind == "col":
        tm, tn, tk = _blocks(m, kk, (2048, 1024), ns, (1024, 512), dy, out_dtypes, extras)
        kb = ns // tk
        b_spec = pl.BlockSpec((None, tn, tk), lambda i, j, k: (k // kb, j, k % kb))
    else:
        tm, tn, tk = _blocks(m, kk, (2048, 1024), n, (1152, 512), dy, out_dtypes, extras)
        b_spec = pl.BlockSpec((tn, tk), lambda i, j, k: (j, k))
    mk_spec = pl.BlockSpec((tm, tn), lambda i, j, k: (i, j))
    return _mm_core(
        name, dy, w, grid=(m // tm, kk // tn, n // tk),
        a_spec=pl.BlockSpec((tm, tk), lambda i, j, k: (i, k)), b_spec=b_spec, dims=NT, acc_shape=(tm, tn),
        outs=[SDS((m, kk), dt) for dt in out_dtypes], out_spec=mk_spec, epilogue=epilogue, extras=extras,
        extra_specs=[mk_spec] * len(extras), after=after)


def _in_proj_dt(name, a, w_t, first, h_n):
    m, kk = a.shape
    tm, tk = _pick(m, 1024), _pick(kk, 1024)
    blk = first // h_n
    return _mm_core(
        name, a, w_t, grid=(m // tm, 1, kk // tk), a_spec=pl.BlockSpec((tm, tk), lambda i, j, k: (i, k)),
        b_spec=pl.BlockSpec((h_n, tk), lambda i, j, k: (blk, k)), dims=NT, acc_shape=(tm, h_n),
        outs=[SDS((m, h_n), F32)], out_spec=pl.BlockSpec((tm, h_n), lambda i, j, k: (i, 0)))[0]


def _in_proj_dt_dx(name, d_dt, w_t, first, res):
    m, h_n = d_dt.shape
    n = w_t.shape[1]
    tm, tn = _pick(m, 1024), _pick(n, 1024)
    blk = first // h_n
    mn_spec = pl.BlockSpec((tm, tn), lambda i, j, k: (i, j))
    return _mm_core(
        name, d_dt, w_t, grid=(m // tm, n // tn, 1), a_spec=pl.BlockSpec((tm, h_n), lambda i, j, k: (i, 0)),
        b_spec=pl.BlockSpec((h_n, tn), lambda i, j, k: (blk, j)), dims=NN, acc_shape=(tm, tn),
        outs=[SDS((m, n), F32)], out_spec=mn_spec, epilogue=_residual_epilogue, extras=(res,),
        extra_specs=[mn_spec])[0]


def _in_proj_dw_t(name, dz, d_dt, x, after=()):
    m, n = x.shape
    zw, h_n = dz.shape[1], d_dt.shape[1]
    tm, tn, tk = _pick(zw, 1024), _pick(n, 2048), _pick(m, 1024)
    out = SDS((zw + h_n, n), BF16)
    x_spec = pl.BlockSpec((tk, tn), lambda i, j, k: (k, j))
    main = _mm_core(
        name, dz, x, grid=(zw // tm, n // tn, m // tk), a_spec=pl.BlockSpec((tk, tm), lambda i, j, k: (k, i)),
        b_spec=x_spec, dims=TN, acc_shape=(tm, tn), outs=[out], out_spec=pl.BlockSpec((tm, tn), lambda i, j, k: (i, j)),
        after=after)[0]
    blk = zw // h_n
    return _mm_core(
        name + "_dt", d_dt, x, grid=(1, n // tn, m // tk), a_spec=pl.BlockSpec((tk, h_n), lambda i, j, k: (k, 0)),
        b_spec=x_spec, dims=TN, acc_shape=(h_n, tn), outs=[out],
        out_spec=pl.BlockSpec((h_n, tn), lambda i, j, k: (blk, j)), carry=main)[0]


def _mm_dw(name, a, dy, kind, a_fn=None, after=()):
    m, kk = a.shape
    n = dy.shape[1]
    tk = _pick(m, 1024)
    if kind == "col":
        ns = n // N_DEV
        tm, tn = _pick(kk, 1024), _pick(ns, 1024)
        nb = ns // tn
        out = SDS((N_DEV, kk, ns), BF16)
        out_spec = pl.BlockSpec((None, tm, tn), lambda i, j, k: (j // nb, i, j % nb))
    else:
        tm, tn = _pick(kk, 1024), _pick(n, 2048)
        out = SDS((kk, n), BF16)
        out_spec = pl.BlockSpec((tm, tn), lambda i, j, k: (i, j))
    res = _mm_core(
        name, a, dy, grid=(kk // tm, n // tn, m // tk),
        a_spec=pl.BlockSpec((tk, tm), lambda i, j, k: (k, i)), b_spec=pl.BlockSpec((tk, tn), lambda i, j, k: (k, j)),
        dims=TN, acc_shape=(tm, tn), outs=[out], out_spec=out_spec, a_fn=a_fn, after=after)[0]
    return res.reshape(N_DEV, kk // N_DEV, n) if kind == "row" else res


def _rowwise(name, fn, ins, outs, rows, tile):
    arrays, specs = [], []
    for arr, kind in ins:
        arrays.append(arr)
        if kind == "row":
            specs.append(pl.BlockSpec((tile, arr.shape[1]), lambda i: (i, 0)))
        elif kind == "vec":
            specs.append(pl.BlockSpec(arr.shape, lambda i, nd=arr.ndim: (0,) * nd))
        else:
            specs.append(kind)
    out_shapes, out_specs, kinds = [], [], []
    for cols, dt, kind in outs:
        kinds.append(kind)
        if kind == "row":
            out_shapes.append(SDS((rows, cols), dt))
            out_specs.append(pl.BlockSpec((tile, cols), lambda i: (i, 0)))
        else:
            out_shapes.append(SDS((1, cols), F32))
            out_specs.append(pl.BlockSpec((1, cols), lambda i: (0, 0)))
    ni = len(arrays)
    has_acc = "acc" in kinds

    def body(*refs):
        vals = fn(*[r[...] for r in refs[:ni]])
        i = pl.program_id(0)
        for o, v, kind in zip(refs[ni:], vals, kinds):
            if kind == "row":
                o[...] = v.astype(o.dtype)
            else:
                @pl.when(i == 0)
                def _(o=o):
                    o[...] = jnp.zeros_like(o)

                o[...] += v

    return pl.pallas_call(
        body, name=name, grid=(rows // tile,), out_shape=out_shapes, in_specs=specs, out_specs=out_specs,
        compiler_params=_cp("arbitrary" if has_acc else "parallel"),
    )(*arrays)


def _ln_fwd(name, u, g, b, after=()):
    d = u.shape[1]

    def fn(u, g, b, *unused):
        mu = jnp.mean(u, axis=1, keepdims=True)
        xc = u - mu
        var = jnp.mean(xc * xc, axis=1, keepdims=True)
        y = xc * lax.rsqrt(var + LN_EPS) * g + b
        return y, y

    ins = [(u, "row"), (g, "vec"), (b, "vec")] + [(t, "vec") for t in after]
    return _rowwise(name, fn, ins, [(d, F32, "row"), (d, BF16, "row")], u.shape[0], 256)


def _ln_bwd(name, u, dy, g):
    d = u.shape[1]

    def fn(u, dy, g):
        mu = jnp.mean(u, axis=1, keepdims=True)
        xc = u - mu
        var = jnp.mean(xc * xc, axis=1, keepdims=True)
        rstd = lax.rsqrt(var + LN_EPS)
        xhat = xc * rstd
        dxhat = dy * g
        m1 = jnp.mean(dxhat, axis=1, keepdims=True)
        m2 = jnp.mean(dxhat * xhat, axis=1, keepdims=True)
        du = rstd * (dxhat - m1 - xhat * m2)
        return du, du, jnp.sum(dy * xhat, axis=0, keepdims=True), jnp.sum(dy, axis=0, keepdims=True)

    return _rowwise(name, fn, [(u, "row"), (dy, "row"), (g, "vec")],
                    [(d, F32, "row"), (d, BF16, "row"), (d, F32, "acc"), (d, F32, "acc")], u.shape[0], 256)


def _loss_bwd(name, y, target):
    d = y.shape[1]

    def fn(y, t):
        e = y - t
        return e * (1.0 / d), jnp.sum(e * e, axis=0, keepdims=True) * (0.5 / d)

    return _rowwise(name, fn, [(y, "row"), (target, "row")], [(d, F32, "row"), (d, F32, "acc")], y.shape[0], 256)


def _ple_bwd(name, dx, e, gate):
    d = dx.shape[1]

    def fn(dx, e, gate):
        return dx * e * gate * (1.0 - gate), dx * gate

    return _rowwise(name, fn, [(dx, "row"), (e, "row"), (gate, "row")], [(d, BF16, "row"), (d, BF16, "row")],
                    dx.shape[0], 256)


def _sigmoid(v):
    return 1.0 / (1.0 + jnp.exp(-v))


def _gated_rms_fwd(name, y, zx, norm_w, after=()):
    di = y.shape[1]

    def fn(y, z, w, *unused):
        yg = y * (z * _sigmoid(z))
        r = lax.rsqrt(jnp.mean(yg * yg, axis=1, keepdims=True) + RMS_EPS)
        return (yg * r * w,)

    z_spec = pl.BlockSpec((128, di), lambda i: (i, 0))
    ins = [(y, "row"), (zx, z_spec), (norm_w, "vec")] + [(t, "vec") for t in after]
    return _rowwise(name, fn, ins, [(di, BF16, "row")], y.shape[0], 128)[0]


def _gated_rms_bwd(name, y, zx, norm_w, dout):
    di = y.shape[1]

    def fn(y, z, w, dout):
        sg = _sigmoid(z)
        sz = z * sg
        yg = y * sz
        r = lax.rsqrt(jnp.mean(yg * yg, axis=1, keepdims=True) + RMS_EPS)
        dn = dout * w
        dyg = r * (dn - yg * (r * r) * jnp.mean(dn * yg, axis=1, keepdims=True))
        dy = dyg * sz
        dz = dyg * y * (sg * (1.0 + z * (1.0 - sg)))
        return dy, dz, jnp.sum(dout * yg * r, axis=0, keepdims=True)

    z_spec = pl.BlockSpec((128, di), lambda i: (i, 0))
    return _rowwise(name, fn, [(y, "row"), (zx, z_spec), (norm_w, "vec"), (dout, "row")],
                    [(di, F32, "row"), (di, BF16, "row"), (di, F32, "acc")], y.shape[0], 128)


def _shift_down(v, j, row):
    return jnp.where(row >= j, pltpu.roll(v, j, 0), 0.0)


def _shift_up(v, j, row):
    t = v.shape[0]
    return jnp.where(row < t - j, pltpu.roll(v, t - j, 0), 0.0)


def _pool_select(parts, g):
    return jnp.where(g == 0, parts[0], jnp.where(g == 1, parts[1], jnp.where(g == 2, parts[2], parts[3])))


def _pool_windows(name, x, transpose, scale_by=None, after=()):
    t, d = x.shape
    cg = d // 4
    cw = V7X_LANES
    per = cg // cw

    def body(*refs):
        x_ref, o_ref = refs[0], refs[-1]
        g = pl.program_id(0) // per
        xv = x_ref[...]
        row = lax.broadcasted_iota(jnp.int32, (t, 1), 0)
        cnt = jnp.minimum(row + 1, jnp.left_shift(2, g)).astype(F32)
        s = xv / cnt if transpose else xv
        parts = []
        for lg in POOL_WINDOW_LOG2:
            j = 1 << (lg - 1)
            s = s + (_shift_up(s, j, row) if transpose else _shift_down(s, j, row))
            parts.append(s)
        sel = _pool_select(parts, g)
        if transpose:
            o_ref[...] = ALPHA * refs[1][...] + sel - xv
        else:
            o_ref[...] = (sel / cnt - xv).astype(o_ref.dtype)

    col = pl.BlockSpec((t, cw), lambda j: (0, j))
    ins = [x] if scale_by is None else [x, scale_by]
    return pl.pallas_call(
        body, name=name, grid=(d // cw,), out_shape=SDS((t, d), F32 if transpose else BF16),
        in_specs=[col] * len(ins) + [ANY] * len(after), out_specs=col, compiler_params=_cp("parallel"),
    )(*ins, *after)


def _pool_mm(name, pooled, w, scale, x):
    t, d = x.shape
    cg = d // 4
    tm = _pick(t, 1024)

    def body(p_ref, w_ref, s_ref, x_ref, u_ref, h_ref):
        h = jnp.dot(p_ref[...], w_ref[...], preferred_element_type=F32)
        h_ref[...] = h
        u_ref[...] = ALPHA * x_ref[...] + h * s_ref[...]

    blk = pl.BlockSpec((tm, cg), lambda g, i: (i, g))
    return pl.pallas_call(
        body, name=name, grid=(4, t // tm), out_shape=[SDS((t, d), F32), SDS((t, d), F32)],
        in_specs=[blk, pl.BlockSpec((None, cg, cg), lambda g, i: (g, 0, 0)), pl.BlockSpec((1, cg), lambda g, i: (0, g)),
                  blk],
        out_specs=[blk, blk], compiler_params=_cp("parallel", "parallel"),
    )(pooled, w, scale, x)


def _pool_bwd_mm(name, du, hraw, w, scale):
    t, d = du.shape
    cg = d // 4
    tm = _pick(t, 1024)

    def body(du_ref, h_ref, w_ref, s_ref, dh_ref, dp_ref, ds_ref):
        @pl.when(pl.program_id(1) == 0)
        def _():
            ds_ref[...] = jnp.zeros_like(ds_ref)

        duv = du_ref[...]
        ds_ref[...] += jnp.sum(duv * h_ref[...], axis=0, keepdims=True)
        dh = (duv * s_ref[...]).astype(BF16)
        dh_ref[...] = dh
        dp_ref[...] = lax.dot_general(dh, w_ref[...], NT, preferred_element_type=F32)

    blk = pl.BlockSpec((tm, cg), lambda g, i: (i, g))
    vec = pl.BlockSpec((1, cg), lambda g, i: (0, g))
    return pl.pallas_call(
        body, name=name, grid=(4, t // tm), out_shape=[SDS((t, d), BF16), SDS((t, d), F32), SDS((1, d), F32)],
        in_specs=[blk, blk, pl.BlockSpec((None, cg, cg), lambda g, i: (g, 0, 0)), vec],
        out_specs=[blk, blk, vec], compiler_params=_cp("parallel", "arbitrary"),
    )(du, hraw, w, scale)


def _pool_dw(name, pooled, dh):
    t, d = pooled.shape
    cg = d // 4
    tk = _pick(t, 512)
    nk = t // tk

    def body(p_ref, dh_ref, o_ref, acc):
        k = pl.program_id(1)

        @pl.when(k == 0)
        def _():
            acc[...] = jnp.zeros_like(acc)

        acc[...] += lax.dot_general(p_ref[...], dh_ref[...], TN, preferred_element_type=F32)

        @pl.when(k == nk - 1)
        def _():
            o_ref[...] = acc[...].astype(o_ref.dtype)

    blk = pl.BlockSpec((tk, cg), lambda g, k: (k, g))
    return pl.pallas_call(
        body, name=name, grid=(4, nk), out_shape=SDS((4, cg, cg), BF16), in_specs=[blk, blk],
        out_specs=pl.BlockSpec((None, cg, cg), lambda g, k: (g, 0, 0)), scratch_shapes=[pltpu.VMEM((cg, cg), F32)],
        compiler_params=_cp("parallel", "arbitrary"),
    )(pooled, dh)


def _conv_pre(u, w_ref, b_ref, row):
    pre = b_ref[...] + _shift_down(u, 3, row) * w_ref[0:1, :]
    pre = pre + _shift_down(u, 2, row) * w_ref[1:2, :]
    pre = pre + _shift_down(u, 1, row) * w_ref[2:3, :]
    return pre + u * w_ref[3:4, :]


def _conv_fwd(name, zx, conv_w, conv_b, di):
    t = zx.shape[0]
    cd = conv_w.shape[1]
    cw = _pick(cd, 256)
    off = di // cw

    def body(u_ref, w_ref, b_ref, o_ref):
        row = lax.broadcasted_iota(jnp.int32, (t, 1), 0)
        pre = _conv_pre(u_ref[...], w_ref, b_ref, row)
        o_ref[...] = pre * _sigmoid(pre)

    return pl.pallas_call(
        body, name=name, grid=(cd // cw,), out_shape=SDS((t, cd), F32),
        in_specs=[pl.BlockSpec((t, cw), lambda j: (0, off + j)), pl.BlockSpec((CONV_WIDTH, cw), lambda j: (0, j)),
                  pl.BlockSpec((1, cw), lambda j: (0, j))],
        out_specs=pl.BlockSpec((t, cw), lambda j: (0, j)), compiler_params=_cp("parallel"),
    )(zx, conv_w, conv_b)


def _conv_bwd(name, zx, conv_w, conv_b, dact, di, first):
    t, cd = dact.shape
    cw = _pick(cd, 256)
    off, woff = (di + first) // cw, first // cw

    def body(u_ref, w_ref, b_ref, da_ref, du_ref, dw_ref, db_ref):
        row = lax.broadcasted_iota(jnp.int32, (t, 1), 0)
        u = u_ref[...]
        pre = _conv_pre(u, w_ref, b_ref, row)
        sg = _sigmoid(pre)
        dpre = da_ref[...] * (sg * (1.0 + pre * (1.0 - sg)))
        du = dpre * w_ref[3:4, :]
        for j in (1, 2, 3):
            du = du + _shift_up(dpre, j, row) * w_ref[3 - j:4 - j, :]
            dw_ref[3 - j:4 - j, :] = jnp.sum(dpre * _shift_down(u, j, row), axis=0, keepdims=True)
        dw_ref[3:4, :] = jnp.sum(dpre * u, axis=0, keepdims=True)
        db_ref[...] = jnp.sum(dpre, axis=0, keepdims=True)
        du_ref[...] = du.astype(du_ref.dtype)

    wspec = pl.BlockSpec((CONV_WIDTH, cw), lambda j: (0, j))
    bspec = pl.BlockSpec((1, cw), lambda j: (0, j))
    ospec = pl.BlockSpec((t, cw), lambda j: (0, j))
    return pl.pallas_call(
        body, name=name, grid=(cd // cw,), out_shape=[SDS((t, cd), BF16), SDS((CONV_WIDTH, cd), F32), SDS((1, cd), F32)],
        in_specs=[pl.BlockSpec((t, cw), lambda j: (0, off + j)), pl.BlockSpec((CONV_WIDTH, cw), lambda j: (0, woff + j)),
                  pl.BlockSpec((1, cw), lambda j: (0, woff + j)), ospec],
        out_specs=[ospec, wspec, bspec], compiler_params=_cp("parallel"),
    )(zx, conv_w, conv_b, dact)


def _expand_heads(name, arrays, h_n, p):
    t = arrays[0].shape[0]
    n = len(arrays)
    w = _pick(h_n * p, 512)

    def body(*refs):
        j = pl.program_id(0)
        head = lax.broadcasted_iota(jnp.int32, (V7X_LANES, w), 0)
        lane = lax.broadcasted_iota(jnp.int32, (V7X_LANES, w), 1)
        spread = (head == j * (w // p) + lane // p).astype(BF16)
        for a_ref, o_ref in zip(refs[:n], refs[n:]):
            rest = a_ref[...]
            out = jnp.zeros((t, w), F32)
            for _ in range(3):
                piece = rest.astype(BF16)
                out = out + jnp.dot(piece, spread, preferred_element_type=F32)
                rest = rest - piece.astype(F32)
            o_ref[...] = out

    full = pl.BlockSpec((t, V7X_LANES), lambda j: (0, 0))
    return pl.pallas_call(
        body, name=name, grid=(h_n * p // w,), out_shape=[SDS((t, h_n * p), F32)] * n, in_specs=[full] * n,
        out_specs=[pl.BlockSpec((t, w), lambda j: (0, j))] * n, compiler_params=_cp("parallel"),
    )(*arrays)


def _softplus(v):
    return jnp.maximum(v, 0.0) + jnp.log(1.0 + jnp.exp(-jnp.abs(v)))


def _dt_fwd(name, zx, bias, a_log, col_block):
    t = zx.shape[0]

    def body(r_ref, b_ref, al_ref, dt_ref, acs_ref, ein_ref, eout_ref):
        row = lax.broadcasted_iota(jnp.int32, (t, 1), 0) % CHUNK
        dt = _softplus(r_ref[...] + b_ref[...])
        da = dt * (-jnp.exp(al_ref[...]))
        s, r = da, da
        j = 1
        while j < CHUNK:
            s = s + jnp.where(row >= j, pltpu.roll(s, j, 0), 0.0)
            r = r + jnp.where(row < CHUNK - j, pltpu.roll(r, t - j, 0), 0.0)
            j *= 2
        dt_ref[...] = dt
        acs_ref[...] = s
        ein_ref[...] = jnp.exp(s)
        eout_ref[...] = jnp.exp(r - da)

    vec = pl.BlockSpec((1, V7X_LANES), lambda i: (0, 0))
    full = pl.BlockSpec((t, V7X_LANES), lambda i: (0, 0))
    return pl.pallas_call(
        body, name=name, grid=(1,), out_shape=[SDS((t, V7X_LANES), F32)] * 4,
        in_specs=[pl.BlockSpec((t, V7X_LANES), lambda i: (0, col_block)), vec, vec], out_specs=[full] * 4,
        compiler_params=_cp("arbitrary"),
    )(zx, bias, a_log)


def _dt_bwd(name, zx, bias, a_log, d_acs, d_dt, col_block):
    t = zx.shape[0]

    def body(r_ref, b_ref, al_ref, da_ref, dd_ref, draw_ref, db_ref, dal_ref):
        row = lax.broadcasted_iota(jnp.int32, (t, 1), 0) % CHUNK
        pre = r_ref[...] + b_ref[...]
        dt = _softplus(pre)
        a = -jnp.exp(al_ref[...])
        s = da_ref[...]
        j = 1
        while j < CHUNK:
            s = s + jnp.where(row < CHUNK - j, pltpu.roll(s, t - j, 0), 0.0)
            j *= 2
        ddt = dd_ref[...] + s * a
        dal_ref[...] = jnp.sum(s * dt, axis=0, keepdims=True) * a
        draw = ddt * _sigmoid(pre)
        db_ref[...] = jnp.sum(draw, axis=0, keepdims=True)
        draw_ref[...] = draw.astype(draw_ref.dtype)

    vec = pl.BlockSpec((1, V7X_LANES), lambda i: (0, 0))
    full = pl.BlockSpec((t, V7X_LANES), lambda i: (0, 0))
    return pl.pallas_call(
        body, name=name, grid=(1,), out_shape=[SDS((t, V7X_LANES), BF16), SDS((1, V7X_LANES), F32), SDS((1, V7X_LANES), F32)],
        in_specs=[pl.BlockSpec((t, V7X_LANES), lambda i: (0, col_block)), vec, vec, full, full],
        out_specs=[full, vec, vec], compiler_params=_cp("arbitrary"),
    )(zx, bias, a_log, d_acs, d_dt)


def _ssd_specs(t, di, g_n, hpg, p, rev):
    rows = SSD_STEP_CHUNKS * CHUNK
    nc = t // rows
    w = hpg * p
    nb = di // D_STATE

    def cc(c):
        return nc - 1 - c if rev else c

    return dict(
        xs=pl.BlockSpec((rows, w), lambda g, c: (cc(c), g)),
        bm=pl.BlockSpec((rows, D_STATE), lambda g, c: (cc(c), nb + g)),
        cm=pl.BlockSpec((rows, D_STATE), lambda g, c: (cc(c), nb + g_n + g)),
        col=pl.BlockSpec((None, rows, hpg), lambda g, c: (g, cc(c), 0)),
        rowv=pl.BlockSpec((None, hpg, rows), lambda g, c: (g, 0, cc(c))),
        head=pl.BlockSpec((None, 1, hpg), lambda g, c: (g, 0, 0)),
        lanes=pl.BlockSpec((1, w), lambda g, c: (0, g)),
        bc=pl.BlockSpec((rows, D_STATE), lambda g, c: (cc(c), g)),
        prev=pl.BlockSpec((SSD_STEP_CHUNKS, None, D_STATE, w), lambda g, c: (cc(c), g, 0, 0)),
        seg=pl.BlockSpec((w, V7X_LANES), lambda g, c: (0, 0)),
    )


def _decay_masks(cb, ac, ar, heads):
    li = lax.broadcasted_iota(jnp.int32, (CHUNK, CHUNK), 0)
    si = lax.broadcasted_iota(jnp.int32, (CHUNK, CHUNK), 1)
    lms = [jnp.exp(jnp.where(li >= si, ac[:, hh:hh + 1] - ar[hh:hh + 1, :], -jnp.inf)) for hh in heads]
    return lms, [(cb * lm).astype(BF16) for lm in lms]


def _ssd_fwd(name, xbc, dt_x, ein_x, eout_x, a_col, a_row, d_x, di, g_n, hpg, p):
    t = xbc.shape[0]
    nc = t // CHUNK
    w = hpg * p
    assert 2 * p == V7X_LANES and hpg % 2 == 0 and nc % SSD_STEP_CHUNKS == 0
    sp = _ssd_specs(t, di, g_n, hpg, p, False)

    def body(xs_ref, bm_ref, cm_ref, dt_ref, ein_ref, eout_ref, ac_ref, ar_ref, d_ref, y_ref, prev_ref, h_ref):
        @pl.when(pl.program_id(1) == 0)
        def _():
            h_ref[...] = jnp.zeros_like(h_ref)

        first = lax.broadcasted_iota(jnp.int32, (1, V7X_LANES), 1) < p
        for sub in range(SSD_STEP_CHUNKS):
            r = slice(sub * CHUNK, (sub + 1) * CHUNK)
            bm = bm_ref[r, :].astype(BF16)
            cm = cm_ref[r, :].astype(BF16)
            cb = lax.dot_general(cm, bm, NT, preferred_element_type=F32)
            xs = xs_ref[r, :]
            e_in = ein_ref[r, :]
            xdt = xs * dt_ref[r, :]
            ac, ar = ac_ref[r, :], ar_ref[:, r]
            ys = []
            for pr in range(hpg // 2):
                _, ms = _decay_masks(cb, ac, ar, (2 * pr, 2 * pr + 1))
                xp = xdt[:, pr * V7X_LANES:(pr + 1) * V7X_LANES]
                rhs = jnp.concatenate([jnp.where(first, xp, 0.0), jnp.where(first, 0.0, xp)], axis=0).astype(BF16)
                ys.append(jnp.dot(jnp.concatenate(ms, axis=1), rhs, preferred_element_type=F32))
            h_prev = h_ref[...]
            prev_ref[sub] = h_prev
            y = jnp.concatenate(ys, axis=1) + jnp.dot(cm, h_prev.astype(BF16), preferred_element_type=F32) * e_in
            y_ref[r, :] = y + xs * d_ref[...]
            st = lax.dot_general(bm, (xdt * eout_ref[r, :]).astype(BF16), TN, preferred_element_type=F32)
            h_ref[...] = e_in[CHUNK - 1:CHUNK, :] * h_prev + st

    return pl.pallas_call(
        body, name=name, grid=(g_n, nc // SSD_STEP_CHUNKS),
        out_shape=[SDS((t, di), F32), SDS((nc, g_n, D_STATE, w), F32)],
        in_specs=[sp["xs"], sp["bm"], sp["cm"], sp["xs"], sp["xs"], sp["xs"], sp["col"], sp["rowv"], sp["lanes"]],
        out_specs=[sp["xs"], sp["prev"]], scratch_shapes=[pltpu.VMEM((D_STATE, w), F32)],
        compiler_params=_cp("parallel", "arbitrary"),
    )(xbc, xbc, xbc, dt_x, ein_x, eout_x, a_col, a_row, d_x)


def _head_sums(v, seg):
    hi = v.astype(BF16)
    lo = (v - hi.astype(F32)).astype(BF16)
    return jnp.dot(hi, seg, preferred_element_type=F32) + jnp.dot(lo, seg, preferred_element_type=F32)


def _head_totals(v, seg):
    part = v[0:8]
    for r in range(8, v.shape[0], 8):
        part = part + v[r:r + 8]
    return jnp.sum(_head_sums(part, seg), axis=0, keepdims=True)


def _ssd_bwd(name, xbc, dt_x, ein_x, eout_x, a_col, a_row, d_x, prev, dy, di, g_n, hpg, p):
    t = xbc.shape[0]
    nc = t // CHUNK
    w = hpg * p
    sp = _ssd_specs(t, di, g_n, hpg, p, True)
    seg = (lax.broadcasted_iota(jnp.int32, (w, V7X_LANES), 0) // p
           == lax.broadcasted_iota(jnp.int32, (w, V7X_LANES), 1)).astype(BF16)

    def body(xs_ref, bm_ref, cm_ref, dt_ref, ein_ref, eout_ref, ac_ref, ar_ref, d_ref, prev_ref, dy_ref,
             seg_ref, dx_ref, dbm_ref, dcm_ref, ddt_ref, dacs_ref, dd_ref, dh_ref):
        @pl.when(pl.program_id(1) == 0)
        def _():
            dh_ref[...] = jnp.zeros_like(dh_ref)
            dd_ref[...] = jnp.zeros_like(dd_ref)

        first = lax.broadcasted_iota(jnp.int32, (1, V7X_LANES), 1) < p
        last_row = lax.broadcasted_iota(jnp.int32, (CHUNK, 1), 0) == CHUNK - 1
        seg_m = seg_ref[...]
        d_skip = d_ref[...]
        for sub in reversed(range(SSD_STEP_CHUNKS)):
            r = slice(sub * CHUNK, (sub + 1) * CHUNK)
            bm = bm_ref[r, :].astype(BF16)
            cm = cm_ref[r, :].astype(BF16)
            cb = lax.dot_general(cm, bm, NT, preferred_element_type=F32)
            xs, dy, e_in, e_out, dt_l = xs_ref[r, :], dy_ref[r, :], ein_ref[r, :], eout_ref[r, :], dt_ref[r, :]
            ac, ar = ac_ref[r, :], ar_ref[:, r]
            xdt = xs * dt_l
            h_prev = prev_ref[sub]
            h_prev_b = h_prev.astype(BF16)
            dh_next = dh_ref[...]
            dh_next_b = dh_next.astype(BF16)
            dy_e = (dy * e_in).astype(BF16)
            d_cm = lax.dot_general(dy_e, h_prev_b, NT, preferred_element_type=F32)
            dh_ref[...] = (e_in[CHUNK - 1:CHUNK, :] * dh_next
                           + lax.dot_general(cm, dy_e, TN, preferred_element_type=F32))
            q = jnp.dot(bm, dh_next_b, preferred_element_type=F32)
            xf = xdt * e_out
            d_bm = lax.dot_general(xf.astype(BF16), dh_next_b, NT, preferred_element_type=F32)
            d_cb = jnp.zeros((CHUNK, CHUNK), F32)
            parts, w_parts = [], []
            for pr in range(hpg // 2):
                lanes = slice(pr * V7X_LANES, (pr + 1) * V7X_LANES)
                lms, ms = _decay_masks(cb, ac, ar, (2 * pr, 2 * pr + 1))
                xp = xdt[:, lanes]
                xp_b = xp.astype(BF16)
                dyp = dy[:, lanes]
                halves = [jnp.where(first, dyp, 0.0).astype(BF16), jnp.where(first, 0.0, dyp).astype(BF16)]
                for lm, half in zip(lms, halves):
                    d_cb = d_cb + lax.dot_general(half, xp_b, NT, preferred_element_type=F32) * lm
                dxd = lax.dot_general(jnp.concatenate(ms, axis=0), jnp.concatenate(halves, axis=0), TN,
                                      preferred_element_type=F32)
                stacked = jnp.concatenate([jnp.where(first, xp, 0.0), jnp.where(first, 0.0, xp)], axis=0).astype(BF16)
                y_diag = jnp.dot(jnp.concatenate(ms, axis=1), stacked, preferred_element_type=F32)
                parts.append(dxd)
                w_parts.append(dyp.astype(BF16).astype(F32) * y_diag - xp_b.astype(F32) * dxd)
            d_xdt = jnp.concatenate(parts, axis=1) + q * e_out
            dx_ref[r, :] = d_xdt * dt_l + dy * d_skip
            ch = jnp.dot(cm, h_prev_b, preferred_element_type=F32)
            qx = q * xf
            s_a = _head_sums(dy * ch * e_in - qx + jnp.concatenate(w_parts, axis=1), seg_m)[:, :hpg]
            d_last = (_head_totals(qx, seg_m)[:, :hpg]
                      + jnp.exp(ac[CHUNK - 1:CHUNK, :]) * _head_totals(dh_next * h_prev, seg_m)[:, :hpg])
            ddt_ref[r, :] = _head_sums(d_xdt * xs, seg_m)[:, :hpg]
            dacs_ref[r, :] = s_a + jnp.where(last_row, d_last, 0.0)
            dd_ref[...] += _head_totals(dy * xs, seg_m)[:, :hpg]
            d_cb_b = d_cb.astype(BF16)
            dcm_ref[r, :] = d_cm + jnp.dot(d_cb_b, bm, preferred_element_type=F32)
            dbm_ref[r, :] = d_bm + lax.dot_general(d_cb_b, cm, TN, preferred_element_type=F32)

    gn = g_n * D_STATE
    return pl.pallas_call(
        body, name=name, grid=(g_n, nc // SSD_STEP_CHUNKS),
        out_shape=[SDS((t, di), F32), SDS((t, gn), F32), SDS((t, gn), F32), SDS((g_n, t, hpg), F32),
                   SDS((g_n, t, hpg), F32), SDS((g_n, 1, hpg), F32)],
        in_specs=[sp["xs"], sp["bm"], sp["cm"], sp["xs"], sp["xs"], sp["xs"], sp["col"], sp["rowv"],
                  sp["lanes"], sp["prev"], sp["xs"], sp["seg"]],
        out_specs=[sp["xs"], sp["bc"], sp["bc"], sp["col"], sp["col"], sp["head"]],
        scratch_shapes=[pltpu.VMEM((D_STATE, w), F32)],
        compiler_params=_cp("parallel", "arbitrary"),
    )(xbc, xbc, xbc, dt_x, ein_x, eout_x, a_col, a_row, d_x, prev, dy, seg)


def _as3d(a):
    return a.reshape(a.shape[0], -1, a.shape[-1])


def _pair_sum(name, own, recv, core):
    shape = recv.shape
    cols = shape[-1]
    own3, recv3 = own.reshape(8, -1, cols), recv.reshape(4, -1, cols)
    rows = recv3.shape[1]
    tr = _row_tile(rows, cols, 2)

    def body(c_ref, a_ref, b_ref, o_ref):
        o_ref[...] = (a_ref[...].astype(F32) + b_ref[...].astype(F32)).astype(o_ref.dtype)

    blk = pl.BlockSpec((None, tr, cols), lambda q, i, c_ref: (q, i, 0))
    out = pl.pallas_call(
        body, name=name, out_shape=SDS(recv3.shape, recv.dtype),
        grid_spec=pltpu.PrefetchScalarGridSpec(
            num_scalar_prefetch=1, grid=(4, rows // tr),
            in_specs=[pl.BlockSpec((None, tr, cols), lambda q, i, c_ref: (2 * q + c_ref[0], i, 0)), blk], out_specs=blk),
        compiler_params=_cp("parallel", "parallel"),
    )(core, own3, recv3)
    return out.reshape(shape)


def _adamw(name, w, m, v, parts, layer, prev=None, sel=None):
    lyr, rows, cols = w.shape
    n = len(parts)
    by_rows = rows % 16 == 0
    tr, tc = (_row_tile(rows, cols), cols) if by_rows else (rows, _pick(cols, 256))
    np_ = 0 if prev is None else 4
    if sel is None:
        sel = jnp.zeros((1,), jnp.int32)

    def body(sel_ref, *refs):
        w_ref, m_ref, v_ref = refs[:3]
        p_refs = refs[3:3 + n]
        g_ref, d_ref, nm_ref, nv_ref = refs[3 + n + np_:]
        g = p_refs[0][...].astype(F32)
        for r in p_refs[1:]:
            g = g + r[...].astype(F32)
        nm = ADAM_B1 * m_ref[...] + (1.0 - ADAM_B1) * g
        nv = ADAM_B2 * v_ref[...] + (1.0 - ADAM_B2) * (g * g)
        m_hat = nm / (1.0 - ADAM_B1 ** ADAM_STEP)
        v_hat = nv / (1.0 - ADAM_B2 ** ADAM_STEP)
        g_ref[...] = g
        d_ref[...] = -ADAM_LR * (m_hat / (jnp.sqrt(v_hat) + ADAM_EPS) + ADAM_WD * w_ref[...])
        nm_ref[...] = nm
        nv_ref[...] = nv

    def at(lead):
        return pl.BlockSpec((None, tr, tc), lambda i, s: (lead(s), i, 0) if by_rows else (lead(s), 0, i))

    lspec = at(lambda s: layer)
    pspecs = [at(lambda s: s[0]) if q is None else at(lambda s, q=q: q) for _, q in parts]
    aliases = {} if prev is None else {4 + n + q: q for q in range(4)}
    return pl.pallas_call(
        body, name=name, out_shape=[SDS(w.shape, F32)] * 4,
        grid_spec=pltpu.PrefetchScalarGridSpec(
            num_scalar_prefetch=1, grid=(rows // tr if by_rows else cols // tc,),
            in_specs=[lspec] * 3 + pspecs + [ANY] * np_, out_specs=[lspec] * 4),
        input_output_aliases=aliases, compiler_params=_cp("parallel"),
    )(sel, w, m, v, *[arr for arr, _ in parts], *(prev or ()))


def _sum8(name, parts):
    rows = parts.shape[1]

    def body(p_ref, o_ref):
        s = p_ref[0]
        for q in range(1, N_DEV):
            s = s + p_ref[q]
        o_ref[...] = s

    return pl.pallas_call(
        body, name=name, grid=(1,), out_shape=SDS((rows, V7X_LANES), F32),
        in_specs=[pl.BlockSpec((N_DEV, rows, V7X_LANES), lambda i: (0, 0, 0))],
        out_specs=pl.BlockSpec((rows, V7X_LANES), lambda i: (0, 0)), compiler_params=_cp("arbitrary"),
    )(parts)


def _pack(vectors, align):
    flat = jnp.concatenate([v.reshape(-1) for v in vectors])
    pad = (-flat.shape[0]) % align
    if pad:
        flat = jnp.concatenate([flat, jnp.zeros((pad,), F32)])
    return flat.reshape(-1, V7X_LANES)


def _unpack(packed, shapes):
    flat = packed.reshape(-1)
    out, o = [], 0
    for s in shapes:
        size = 1
        for dim in s:
            size *= dim
        out.append(flat[o:o + size].reshape(s))
        o += size
    return out


def _residual_epilogue(acc, res):
    return (ALPHA * res + acc,)


def _plain_add_epilogue(acc, res):
    return (res + acc,)


def _gate_epilogue(acc, y, e):
    gate = _sigmoid(acc)
    xn = y + gate * e
    return xn, gate, xn


def _relu2(pre):
    r = jnp.maximum(pre, 0.0)
    return r * r


def _relu2_bwd_epilogue(acc, pre):
    return (acc * (2.0 * jnp.maximum(pre.astype(F32), 0.0)),)


def _tail_fwd(tag, u_a, wts, lng, lnb, p_l, finish_w1, finish_w2, after=()):
    y1, y1_b = _ln_fwd(f"ln1_{tag}", u_a, lng[0], lnb[0], after)
    finish_w1(y1_b)
    (pre,) = _mm_fwd(f"mlp1_{tag}", y1_b, wts["w1"], "col", [BF16])
    finish_w2(pre)
    (u_b,) = _mm_fwd(f"mlp2_{tag}", pre, wts["w2"], "row", [F32], _residual_epilogue, (y1,), a_fn=_relu2)
    y2, y2_b = _ln_fwd(f"ln2_{tag}", u_b, lng[1], lnb[1])
    (e,) = _mm_fwd(f"ple_{tag}", p_l, wts["plew"], "col", [F32])
    xn, gate, xn_b = _mm_fwd(f"gate_{tag}", y2_b, wts["gate"], "row", [F32, F32, BF16], _gate_epilogue, (y2, e))
    return xn, xn_b, (u_a, y1_b, pre, u_b, y2_b, e, gate)


def _tail_bwd(tag, dxn, saved, wts, lng, p_l, emit, advance, toks):
    u_a, y1_b, pre, u_b, y2_b, e, gate = saved
    dgpre, de = _ple_bwd(f"ple_bwd_{tag}", dxn, e, gate)
    toks = emit(f"{tag}_ple", dict(gate=_mm_dw(f"gate_dw_{tag}", y2_b, dgpre, "row", after=toks),
                                   plew=_mm_dw(f"ple_dw_{tag}", p_l, de, "col")))
    (dy2,) = _mm_dx(f"gate_dx_{tag}", dgpre, wts["gate"], "row", [F32], _plain_add_epilogue, (dxn,), after=toks)
    toks = advance((dy2,))
    du_b, du_b16, dg2, db2 = _ln_bwd(f"ln2_bwd_{tag}", u_b, dy2, lng[1])
    toks = emit(f"{tag}_w2", dict(w2=_mm_dw(f"mlp2_dw_{tag}", pre, du_b16, "row", a_fn=_relu2, after=toks)))
    (dpre,) = _mm_dx(f"mlp2_dx_{tag}", du_b16, wts["w2"], "row", [BF16], _relu2_bwd_epilogue, (pre,), after=toks)
    toks = advance((dpre,))
    toks = emit(f"{tag}_w1", dict(w1=_mm_dw(f"mlp1_dw_{tag}", y1_b, dpre, "col", after=toks)))
    (dy1,) = _mm_dx(f"mlp1_dx_{tag}", dpre, wts["w1"], "col", [F32], _residual_epilogue, (du_b,), after=toks)
    toks = advance((dy1,))
    du_a, du_a16, dg1, db1 = _ln_bwd(f"ln1_bwd_{tag}", u_a, dy1, lng[0])
    return du_a, du_a16, [dg1, dg2], [db1, db2], toks


def _to_slots(a, axis):
    shape = a.shape
    per = shape[axis] // N_DEV
    v = a.reshape(shape[:axis] + (N_DEV, per) + shape[axis + 1:])
    return jnp.moveaxis(v, axis, 0)


def _pad_lanes(a):
    return jnp.pad(a, [(0, 0)] * (a.ndim - 1) + [(0, V7X_LANES - a.shape[-1])])


def kernel(x, p, pool_w, pool_scale, ssm_in_w, ssm_conv_w, ssm_conv_b, ssm_dt_bias, ssm_a_log, ssm_d, ssm_norm_w, ssm_out_w, mlp_w1, mlp_w2, ln_g, ln_b, ple_w, ple_gate_w, loss_target, m_pool_w, m_pool_scale, m_ssm_in_w, m_ssm_conv_w, m_ssm_conv_b, m_ssm_dt_bias, m_ssm_a_log, m_ssm_d, m_ssm_norm_w, m_ssm_out_w, m_mlp_w1, m_mlp_w2, m_ln_g, m_ln_b, m_ple_w, m_ple_gate_w, v_pool_w, v_pool_scale, v_ssm_in_w, v_ssm_conv_w, v_ssm_conv_b, v_ssm_dt_bias, v_ssm_a_log, v_ssm_d, v_ssm_norm_w, v_ssm_out_w, v_mlp_w1, v_mlp_w2, v_ln_g, v_ln_b, v_ple_w, v_ple_gate_w):
    t, d = x.shape[1:]
    h_n = ssm_dt_bias.shape[-1]
    di_s, cd_s, dp_s, d_s = ssm_norm_w.shape[-1], ssm_conv_b.shape[-1], ssm_in_w.shape[-1], ln_g.shape[-1]
    di, cd, dp = N_DEV * di_s, N_DEV * cd_s, N_DEV * dp_s
    p_dim = di // h_n
    g_n = (cd - di) // (2 * D_STATE)
    hpg = h_n // g_n
    zw = di + cd
    assert h_n <= V7X_LANES and dp == zw + h_n and zw % V7X_LANES == 0 and zw % h_n == 0
    cg = d // 4
    me = 4 * lax.axis_index("x") + 2 * lax.axis_index("y") + lax.axis_index("c")

    x0, target = x[0], loss_target[0]
    p_l = [p[0, 0].astype(BF16), p[1, 0].astype(BF16)]

    small_shapes = [(CONV_WIDTH, cd_s), (1, cd_s), (1, di_s), (2, 2, d_s), (2, 2, d_s)]
    small = _pack([ssm_conv_w[0], ssm_conv_b, ssm_norm_w, ln_g, ln_b], 8 * V7X_LANES)
    first = [w.astype(BF16) for w in (pool_w[0], mlp_w1[0])]

    def gather_start(tag, own, after):
        lands = [lax.empty((N_DEV,) + w.shape, w.dtype) for w in own]
        return _async_start(f"ag_{tag}_start", _ag_first_copies, (4 * len(own),), own, lands, after)

    def gather_pass(tag, handle, after):
        n = len(handle[2]) // 2
        own, lands = _async_wait(f"ag_{tag}_wait", _ag_first_copies, handle, n, after)
        return _async_start(f"ag_{tag}_forward_start", _ag_forward_copies, (3 * n,), [], lands), own

    def gather_done(tag, passed, after=()):
        fwd, own = passed
        _, lands = _async_wait(f"ag_{tag}_forward_wait", _ag_forward_copies, fwd, 0, after)
        return [lax.dynamic_update_slice_in_dim(g, w[None], me, 0) for g, w in zip(lands, own)]

    def gather_finish(tag, handle, after):
        return gather_done(tag, gather_pass(tag, handle, after))

    ag_first = gather_start("first", first + [small], ())
    zero = ag_first[3][0, 0]
    own_l0 = [(w + zero).astype(BF16) for w in (mlp_w2[0], ple_w[0], ple_gate_w[0])]
    own_ssm = [(ssm_in_w[0] + zero).astype(BF16).T]
    own_out = [(ssm_out_w[0] + zero).astype(BF16)]
    own_mlp = [(w + zero).astype(BF16) for w in (mlp_w1[1], mlp_w2[1], ple_w[1], ple_gate_w[1])]
    ag_l0 = gather_start("l0", own_l0, (ag_first[3],))
    ag_ssm = gather_start("ssm", own_ssm, (ag_l0[3],))
    ag_out = gather_start("out", own_out, (ag_ssm[3],))
    ag_mlp = gather_start("mlp1", own_mlp, (ag_out[3],))
    pooled = _pool_windows("pool_fwd", x0, False, after=(ag_mlp[3],))
    pool_g, w1_0, small_g = gather_finish("first", ag_first, (pooled,))
    pool_full = pool_g.transpose(1, 0, 2, 3).reshape(4, cg, cg)
    sm = small_g.reshape(N_DEV, -1)
    o = 0
    parts = []
    for shp in small_shapes:
        size = 1
        for s in shp:
            size *= s
        parts.append(sm[:, o:o + size].reshape((N_DEV,) + shp))
        o += size
    conv_w_full = parts[0].transpose(1, 0, 2).reshape(CONV_WIDTH, cd)
    conv_b_full = parts[1].transpose(1, 0, 2).reshape(1, cd)
    norm_w_full = parts[2].transpose(1, 0, 2).reshape(1, di)
    ln_g_full = parts[3].transpose(1, 2, 0, 3).reshape(2, 2, 1, d)
    ln_b_full = parts[4].transpose(1, 2, 0, 3).reshape(2, 2, 1, d)
    bias128, alog128 = _pad_lanes(ssm_dt_bias), _pad_lanes(ssm_a_log)

    u0, hraw = _pool_mm("pool_mm", pooled, pool_full, pool_scale, x0)
    wts = [dict(w1=w1_0)]

    def finish_l0(after):
        w2_0, plew_0, gate_0 = gather_finish("l0", ag_l0, (after,))
        wts[0].update(w2=w2_0, plew=plew_0, gate=gate_0)

    x1, x1_b, saved0 = _tail_fwd("l0", u0, wts[0], ln_g_full[0], ln_b_full[0], p_l[0], lambda after: None, finish_l0)

    (in_g,) = gather_finish("ssm", ag_ssm, (x1,))
    in_t = in_g.reshape(dp, d)
    (zx,) = _mm_dx("in_proj", x1_b, in_t, "plain", [F32], k_rows=zw)
    dt_raw = _pad_lanes(_in_proj_dt("in_proj_dt", x1_b, in_t, zw, h_n))
    xbc = _conv_fwd("conv_fwd", zx, conv_w_full, conv_b_full, di)
    dt, acs, e_in, e_out = _dt_fwd("dt_fwd", dt_raw, bias128, alog128, 0)

    dt_x, ein_x, eout_x = _expand_heads("expand_heads", [dt, e_in, e_out], h_n, p_dim)
    d_x = jnp.repeat(ssm_d, p_dim, axis=1)

    def to_col(a):
        return a[:, :h_n].reshape(t, g_n, hpg).transpose(1, 0, 2)

    def to_row(a):
        return a[:, :h_n].reshape(t, g_n, hpg).transpose(1, 2, 0)

    def from_col(a):
        return _pad_lanes(a.transpose(1, 0, 2).reshape(t, h_n))

    a_col, a_row = to_col(acs), to_row(acs)
    y_ssd, prev = _ssd_fwd("ssd_fwd", xbc, dt_x, ein_x, eout_x, a_col, a_row, d_x, di, g_n, hpg, p_dim)
    out_passed = gather_pass("out", ag_out, (y_ssd,))
    yn = _gated_rms_fwd("gated_rms_fwd", y_ssd, zx, norm_w_full, after=(out_passed[0][3],))
    (out_g,) = gather_done("out", out_passed, (yn,))
    (u2,) = _mm_fwd("out_proj", yn, out_g, "row", [F32], _residual_epilogue, (x1,))
    mlp_passed = gather_pass("mlp1", ag_mlp, (u2,))
    wts.append({})

    def finish_l1(after):
        wts[1].update(zip(("w1", "w2", "plew", "gate"), gather_done("mlp1", mlp_passed, (after,))))

    x2, _, saved1 = _tail_fwd("l1", u2, wts[1], ln_g_full[1], ln_b_full[1], p_l[1], finish_l1, lambda after: None,
                              after=(mlp_passed[0][3],))

    core = lax.axis_index("c").astype(jnp.int32).reshape(1)
    chip = (2 * lax.axis_index("x") + lax.axis_index("y")).astype(jnp.int32).reshape(1)
    scattering = {}
    pending = []

    def to_chips(after):
        if not pending:
            return []
        tag, names, handle = pending.pop()
        own, halves = _async_wait(f"rs_{tag}_sibling_wait", _rs_sibling_copies, handle, len(names), after)
        sums = [_pair_sum(f"rs_{tag}_pair_sum_{n}", g, hv, core) for n, g, hv in zip(names, own, halves)]
        lands = [lax.empty((3,) + s.shape[1:], s.dtype) for s in sums]
        handle = _async_start(f"rs_{tag}_start", _rs_chip_copies, (3 * len(sums),), sums, lands)
        scattering[tag] = (names, handle)
        return [handle[3]]

    def emit(tag, grads):
        names, arrays = list(grads), list(grads.values())
        toks = to_chips(tuple(arrays))
        lands = [lax.empty((4,) + g.shape[1:], g.dtype) for g in arrays]
        handle = _async_start(f"rs_{tag}_sibling_start", _rs_sibling_copies, (4 * len(arrays),), arrays, lands,
                              tuple(toks))
        pending.append((tag, names, handle))
        return toks + [handle[3]]

    def collect(tag, after):
        names, handle = scattering.pop(tag)
        sums, thirds = _async_wait(f"rs_{tag}_wait", _rs_chip_copies, handle, len(names), after)
        return {n: (s.reshape(4, -1, s.shape[-1]), r.reshape(3, -1, r.shape[-1])) for n, s, r in zip(names, sums, thirds)}

    dx2, loss_cols = _loss_bwd("loss", x2, target)
    du2, du2_b, dg_1, db_1, toks = _tail_bwd("l1", dx2, saved1, wts[1], ln_g_full[1], p_l[1], emit, to_chips, [])
    toks = emit("ssm_out", dict(out=_mm_dw("out_proj_dw", yn, du2_b, "row", after=toks)))
    (dyn,) = _mm_dx("out_proj_dx", du2_b, out_g, "row", [F32], after=toks)
    toks = to_chips((dyn,))
    dy_ssd, dz, d_norm_w = _gated_rms_bwd("gated_rms_bwd", y_ssd, zx, norm_w_full, dyn)
    dxs, dbm, dcm, ddt_x, dacs, dd = _ssd_bwd("ssd_bwd", xbc, dt_x, ein_x, eout_x, a_col, a_row, d_x, prev, dy_ssd,
                                              di, g_n, hpg, p_dim)
    draw, d_bias, d_alog = _dt_bwd("dt_bwd", dt_raw, bias128, alog128, from_col(dacs), from_col(ddt_x), 0)
    conv_parts = [_conv_bwd(f"conv_bwd_{tag}", zx, conv_w_full, conv_b_full, dact, di, first)
                  for tag, dact, first in (("xs", dxs, 0), ("b", dbm, di), ("c", dcm, di + g_n * D_STATE))]
    d_conv_w = jnp.concatenate([c[1] for c in conv_parts], axis=1)
    d_conv_b = jnp.concatenate([c[2] for c in conv_parts], axis=1)
    dzx = jnp.concatenate([dz] + [c[0] for c in conv_parts], axis=1)
    d_dt = draw[:, :h_n]
    g_in_t = _in_proj_dw_t("in_proj_dw", dzx, d_dt, x1_b, after=toks)
    toks = emit("ssm_in", {"in": g_in_t.reshape(N_DEV, dp_s, d)})
    dx_dt = _in_proj_dt_dx("in_proj_dt_dx", d_dt, in_t, zw, du2)
    (dx1,) = _mm_fwd("in_proj_dx", dzx, in_t, "plain", [F32], _plain_add_epilogue, (dx_dt,), after=toks, k_rows=zw)
    toks = to_chips((dx1,))

    du0, _, dg_0, db_0, toks = _tail_bwd("l0", dx1, saved0, wts[0], ln_g_full[0], p_l[0], emit, to_chips, toks)
    dh, dpool, d_scale = _pool_bwd_mm("pool_bwd_mm", du0, hraw, pool_full, pool_scale)
    toks += emit("pool", dict(pool=_to_slots(_pool_dw("pool_dw", pooled, dh), 1)))
    grad_x = _pool_windows("pool_bwd", dpool, True, du0, after=tuple(toks))
    toks = to_chips((grad_x,))

    def update(tag, w, m, v, parts, layer, prev=None):
        own, recv = parts
        return _adamw(f"adamw_{tag}_{layer}", _as3d(w), _as3d(m), _as3d(v),
                      [(own, None), (recv, 0), (recv, 1), (recv, 2)], layer, prev, chip)

    q = collect("l1_ple", (grad_x, *toks))
    r_gate = update("ple_gate_w", ple_gate_w, m_ple_gate_w, v_ple_gate_w, q["gate"], 1)
    r_plew = update("ple_w", ple_w, m_ple_w, v_ple_w, q["plew"], 1)
    r_w2 = update("mlp_w2", mlp_w2, m_mlp_w2, v_mlp_w2, collect("l1_w2", (r_gate[0],))["w2"], 1)
    r_w1 = update("mlp_w1", mlp_w1, m_mlp_w1, v_mlp_w1, collect("l1_w1", (r_w2[0],))["w1"], 1)
    r_out = update("ssm_out_w", ssm_out_w, m_ssm_out_w, v_ssm_out_w, collect("ssm_out", (r_w1[0],))["out"], 0)
    in_wt, in_mt, in_vt = [jnp.swapaxes(a, 1, 2) for a in (ssm_in_w, m_ssm_in_w, v_ssm_in_w)]
    r_in = update("ssm_in_w", in_wt, in_mt, in_vt, collect("ssm_in", (r_out[0],))["in"], 0)

    d_ln_g = jnp.stack([jnp.stack(dg_0), jnp.stack(dg_1)]).reshape(2, 2, d)
    d_ln_b = jnp.stack([jnp.stack(db_0), jnp.stack(db_1)]).reshape(2, 2, d)
    partial_shapes = [(CONV_WIDTH, cd), (1, cd), (1, di), (2, 2, d), (2, 2, d), (1, d), (1, h_n), (1, h_n), (1, h_n),
                      (1, d)]
    partial = _pack([d_conv_w, d_conv_b, d_norm_w, d_ln_g, d_ln_b, d_scale, d_bias[:, :h_n], d_alog[:, :h_n],
                     dd.reshape(1, h_n), loss_cols], 8 * V7X_LANES)
    (all_partials,) = _all_gather("ag_small_grads", [partial], after=(r_in[0],))
    tot = _unpack(_sum8("sum_small_grads", all_partials), partial_shapes)
    t_conv_w, t_conv_b, t_norm_w, t_ln_g, t_ln_b, t_scale, t_bias, t_alog, t_dd, t_loss = tot
    loss = jnp.sum(t_loss)

    def mine(a, per):
        return lax.dynamic_slice_in_dim(a, me * per, per, axis=a.ndim - 1)

    small_names = ["ssm_conv_w", "ssm_conv_b", "ssm_norm_w", "ln_g", "ln_b", "pool_scale", "ssm_dt_bias", "ssm_a_log",
                   "ssm_d"]
    small_w = [ssm_conv_w, ssm_conv_b, ssm_norm_w, ln_g, ln_b, pool_scale, ssm_dt_bias, ssm_a_log, ssm_d]
    small_m = [m_ssm_conv_w, m_ssm_conv_b, m_ssm_norm_w, m_ln_g, m_ln_b, m_pool_scale, m_ssm_dt_bias, m_ssm_a_log,
               m_ssm_d]
    small_v = [v_ssm_conv_w, v_ssm_conv_b, v_ssm_norm_w, v_ln_g, v_ln_b, v_pool_scale, v_ssm_dt_bias, v_ssm_a_log,
               v_ssm_d]
    small_grads = [mine(t_conv_w, cd_s), mine(t_conv_b, cd_s), mine(t_norm_w, di_s), mine(t_ln_g, d_s),
                   mine(t_ln_b, d_s), t_scale, t_bias, t_alog, t_dd]
    shapes = [w.shape for w in small_w]
    pk = [_pack(group, 8 * V7X_LANES)[None] for group in (small_w, small_m, small_v, small_grads)]
    res = _adamw("adamw_small", pk[0], pk[1], pk[2], [(pk[3], 0)], 0)
    upd = {}
    for name, vals in zip(small_names, zip(*[_unpack(r, shapes) for r in res])):
        upd[name] = list(vals)

    q = collect("l0_ple", (res[0],))
    r_gate = update("ple_gate_w", ple_gate_w, m_ple_gate_w, v_ple_gate_w, q["gate"], 0, r_gate)
    r_plew = update("ple_w", ple_w, m_ple_w, v_ple_w, q["plew"], 0, r_plew)
    r_w2 = update("mlp_w2", mlp_w2, m_mlp_w2, v_mlp_w2, collect("l0_w2", (r_gate[0],))["w2"], 0, r_w2)
    r_w1 = update("mlp_w1", mlp_w1, m_mlp_w1, v_mlp_w1, collect("l0_w1", (r_w2[0],))["w1"], 0, r_w1)
    r_pool = update("pool_w", pool_w, m_pool_w, v_pool_w, collect("pool", (r_w1[0],))["pool"], 0)
    assert not scattering
    large = {"pool_w": (pool_w, r_pool), "ssm_in_w": (in_wt, r_in), "ssm_out_w": (ssm_out_w, r_out),
             "mlp_w1": (mlp_w1, r_w1), "mlp_w2": (mlp_w2, r_w2), "ple_w": (ple_w, r_plew),
             "ple_gate_w": (ple_gate_w, r_gate)}
    for name, (w, rs) in large.items():
        upd[name] = [r.reshape(w.shape) for r in rs]
    upd["ssm_in_w"] = [jnp.swapaxes(r, 1, 2) for r in upd["ssm_in_w"]]

    order = ["pool_w", "pool_scale", "ssm_in_w", "ssm_conv_w", "ssm_conv_b", "ssm_dt_bias", "ssm_a_log", "ssm_d",
             "ssm_norm_w", "ssm_out_w", "mlp_w1", "mlp_w2", "ln_g", "ln_b", "ple_w", "ple_gate_w"]
    out = [loss, grad_x[None]]
    for k in range(4):
        out += [upd[name][k] for name in order]
    return tuple(out)
```

```python
import jax
import jax.numpy as jnp
from jax import lax
from jax.experimental import pallas as pl
from jax.experimental.pallas import tpu as pltpu

F32 = jnp.float32
BF16 = jnp.bfloat16
SDS = jax.ShapeDtypeStruct
MESH = pl.DeviceIdType.MESH
ANY = pl.BlockSpec(memory_space=pl.ANY)

N_DEV = 8
DEPTH = 2
ALPHA = (2.0 * DEPTH) ** 0.25
LN_EPS = 1e-5
RMS_EPS = 1e-5
POOL_WINDOW_LOG2 = (1, 2, 3, 4)
D_STATE = 128
CHUNK = 128
SSD_STEP_CHUNKS = 8
CONV_WIDTH = 4
ADAM_LR = 0.001
ADAM_B1 = 0.9
ADAM_B2 = 0.999
ADAM_EPS = 1e-08
ADAM_WD = 0.01
ADAM_STEP = 10

V7X_LANES = 128
V7X_VMEM_LIMIT = 48 * 1024 * 1024


def _cp(*sem):
    return pltpu.CompilerParams(dimension_semantics=sem, vmem_limit_bytes=V7X_VMEM_LIMIT)


def _pick(dim, cap):
    if dim <= cap:
        return dim
    best = None
    for t in range(V7X_LANES, cap + 1, V7X_LANES):
        if dim % t == 0:
            best = t
    assert best is not None, (dim, cap)
    return best


def _row_tile(rows, cols, itemsize=4, target=1 << 20):
    t = rows
    while t % 2 == 0 and t // 2 >= 16 and (t // 2) % 16 == 0 and t * cols * itemsize > target:
        t //= 2
    return t


def _all_gather(name, shards, after=()):
    n, na = len(shards), len(after)

    def body(*refs):
        ins, outs = refs[:n], refs[n + na:2 * n + na]
        send_sems, recv_sems, local_sems = refs[2 * n + na:]
        x, y, c = lax.axis_index("x"), lax.axis_index("y"), lax.axis_index("c")
        me, sibling = (x, y, c), (x, y, 1 - c)
        chips = [(1 - x, y), (x, 1 - y), (1 - x, 1 - y)]

        def copy(a, k, block, to, src=None):
            dst = outs[a].at[4 * block[0] + 2 * block[1] + block[2]]
            return pltpu.make_async_remote_copy(
                src_ref=dst if src is None else src, dst_ref=dst, send_sem=send_sems.at[a, k],
                recv_sem=recv_sems.at[a, k], device_id=to, device_id_type=MESH)

        mine = [pltpu.make_async_copy(ins[a], outs[a].at[4 * x + 2 * y + c], local_sems.at[a]) for a in range(n)]
        for cp in mine:
            cp.start()
        first = []
        for a in range(n):
            first.append(copy(a, 0, me, sibling, src=ins[a]))
            first += [copy(a, 1 + j, me, (*chip, c), src=ins[a]) for j, chip in enumerate(chips)]
        for cp in first:
            cp.start()
        passed = []
        for j, chip in enumerate(chips):
            for a in range(n):
                copy(a, 1 + j, (*chip, c), me).wait_recv()
                fwd = copy(a, 4 + j, (*chip, c), sibling)
                fwd.start()
                passed.append(fwd)
        for a in range(n):
            copy(a, 0, sibling, me).wait_recv()
            for j, chip in enumerate(chips):
                copy(a, 4 + j, (*chip, 1 - c), me).wait_recv()
        for cp in first + passed:
            cp.wait_send()
        for cp in mine:
            cp.wait()

    return pl.pallas_call(
        body, name=name,
        out_shape=[SDS((N_DEV,) + s.shape, s.dtype) for s in shards],
        in_specs=[ANY] * (n + na), out_specs=[ANY] * n,
        scratch_shapes=[pltpu.SemaphoreType.DMA((n, 7)), pltpu.SemaphoreType.DMA((n, 7)),
                        pltpu.SemaphoreType.DMA((n,))],
    )(*shards, *after)


HBM_SPEC = pl.BlockSpec(memory_space=pltpu.HBM)
SEM_SPEC = pl.BlockSpec(memory_space=pltpu.SEMAPHORE)
EFFECT = pltpu.SideEffectType.DATAFLOW_SIDE_EFFECTING


def _ag_first_copies(ins, lands, send_sems, recv_sems):
    x, y, c = lax.axis_index("x"), lax.axis_index("y"), lax.axis_index("c")
    targets = [(x, y, 1 - c), (1 - x, y, c), (x, 1 - y, c), (1 - x, 1 - y, c)]
    return [pltpu.make_async_remote_copy(
        src_ref=ins[a], dst_ref=lands[a].at[4 * x + 2 * y + c], send_sem=send_sems.at[4 * a + k],
        recv_sem=recv_sems.at[4 * a + k], device_id=to, device_id_type=MESH)
        for a in range(len(ins)) for k, to in enumerate(targets)]


def _ag_forward_copies(ins, lands, send_sems, recv_sems):
    x, y, c = lax.axis_index("x"), lax.axis_index("y"), lax.axis_index("c")
    cps = []
    for a in range(len(lands)):
        for j, (px, py) in enumerate([(1 - x, y), (x, 1 - y), (1 - x, 1 - y)]):
            blk = lands[a].at[4 * px + 2 * py + c]
            cps.append(pltpu.make_async_remote_copy(
                src_ref=blk, dst_ref=blk, send_sem=send_sems.at[3 * a + j], recv_sem=recv_sems.at[3 * a + j],
                device_id=(x, y, 1 - c), device_id_type=MESH))
    return cps


def _rs_sibling_copies(ins, lands, send_sems, recv_sems):
    x, y, c = lax.axis_index("x"), lax.axis_index("y"), lax.axis_index("c")
    return [pltpu.make_async_remote_copy(
        src_ref=ins[a].at[2 * q + 1 - c], dst_ref=lands[a].at[q], send_sem=send_sems.at[4 * a + q],
        recv_sem=recv_sems.at[4 * a + q], device_id=(x, y, 1 - c), device_id_type=MESH)
        for a in range(len(ins)) for q in range(4)]


def _rs_chip_copies(ins, lands, send_sems, recv_sems):
    x, y, c = lax.axis_index("x"), lax.axis_index("y"), lax.axis_index("c")
    cps = []
    for a in range(len(ins)):
        for j, (px, py) in enumerate([(1 - x, y), (x, 1 - y), (1 - x, 1 - y)]):
            cps.append(pltpu.make_async_remote_copy(
                src_ref=ins[a].at[2 * px + py], dst_ref=lands[a].at[j], send_sem=send_sems.at[3 * a + j],
                recv_sem=recv_sems.at[3 * a + j], device_id=(px, py, c), device_id_type=MESH))
    return cps


def _async_start(name, build, sem_shape, ins, lands, after=()):
    arrays = [*ins, *lands]
    n_i, n_t, n_a = len(ins), len(arrays), len(after)

    def body(*refs):
        outs = refs[n_t + n_a:]
        for cp in build(refs[:n_i], refs[n_i:n_t], outs[0], outs[1]):
            cp.start()
        outs[-1][...] = jnp.zeros_like(outs[-1])

    res = pl.pallas_call(
        body, name=name,
        out_shape=(pltpu.SemaphoreType.DMA(sem_shape), pltpu.SemaphoreType.DMA(sem_shape),
                   *[pltpu.HBM(a.shape, a.dtype) for a in arrays], SDS((8, V7X_LANES), F32)),
        in_specs=[HBM_SPEC] * n_t + [ANY] * n_a,
        out_specs=(SEM_SPEC, SEM_SPEC, *[HBM_SPEC] * n_t, pl.BlockSpec(memory_space=pltpu.VMEM)),
        input_output_aliases={i: 2 + i for i in range(n_t)},
        compiler_params=pltpu.CompilerParams(has_side_effects=EFFECT),
    )(*[pltpu.with_memory_space_constraint(a, pltpu.HBM) for a in arrays], *after)
    return res[0], res[1], list(res[2:2 + n_t]), res[-1]


def _async_wait(name, build, handle, n_i, after=()):
    send_sems, recv_sems, arrays, _ = handle
    n_t, n_a = len(arrays), len(after)

    def body(*refs):
        for cp in build(refs[:n_i], refs[n_i:n_t], refs[n_t], refs[n_t + 1]):
            cp.wait_send()
            cp.wait_recv()

    res = pl.pallas_call(
        body, name=name, out_shape=tuple(pltpu.HBM(a.shape, a.dtype) for a in arrays),
        in_specs=[HBM_SPEC] * n_t + [SEM_SPEC, SEM_SPEC] + [ANY] * n_a, out_specs=tuple([HBM_SPEC] * n_t),
        input_output_aliases={i: i for i in range(n_t)},
        compiler_params=pltpu.CompilerParams(has_side_effects=EFFECT),
    )(*arrays, send_sems, recv_sems, *after)
    return list(res[:n_i]), list(res[n_i:])


def _mm_core(name, a, b, *, grid, a_spec, b_spec, dims, acc_shape, outs, out_spec, epilogue=None, extras=(),
             extra_specs=(), a_fn=None, after=(), carry=None):
    nk = grid[2]
    if carry is not None:
        after = (*after, carry)
    ne, no, na = len(extras), len(outs), len(after)

    def body(a_ref, b_ref, *rest):
        e_refs, o_refs, acc = rest[:ne], rest[ne + na:ne + na + no], rest[ne + na + no]
        k = pl.program_id(2)

        def product():
            lhs = a_ref[...] if a_fn is None else a_fn(a_ref[...])
            return lax.dot_general(lhs.astype(BF16), b_ref[...].astype(BF16), dims, preferred_element_type=F32)

        @pl.when(k == 0)
        def _():
            acc[...] = product()

        @pl.when(k > 0)
        def _():
            acc[...] += product()

        @pl.when(k == nk - 1)
        def _():
            r = acc[...]
            vals = epilogue(r, *[e[...] for e in e_refs]) if epilogue is not None else (r,)
            for o, v in zip(o_refs, vals):
                o[...] = v.astype(o.dtype)

    res = pl.pallas_call(
        body, name=name, grid=grid, out_shape=list(outs),
        in_specs=[a_spec, b_spec, *extra_specs, *[ANY] * na], out_specs=[out_spec] * no,
        scratch_shapes=[pltpu.VMEM(acc_shape, F32)],
        input_output_aliases={} if carry is None else {1 + ne + na: 0},
        compiler_params=_cp("parallel", "parallel", "arbitrary"),
    )(a, b, *extras, *after)
    return res


NN = (((1,), (0,)), ((), ()))
NT = (((1,), (1,)), ((), ()))
TN = (((0,), (0,)), ((), ()))


def _w_dims(w, kind):
    if kind == "col":
        return w.shape[1], N_DEV * w.shape[2], w.shape[1], w.shape[2]
    if kind == "row":
        return N_DEV * w.shape[1], w.shape[2], w.shape[1], w.shape[2]
    return w.shape[0], w.shape[1], w.shape[0], w.shape[1]


MM_VMEM_BUDGET = 40 * 1024 * 1024


def _blocks(m, n_len, n_caps, k_len, k_caps, a, out_dtypes, extras):
    per_out = sum(jnp.dtype(dt).itemsize for dt in out_dtypes) + sum(e.dtype.itemsize for e in extras)
    best = None
    for tm in (_pick(m, 2048), _pick(m, 1024), _pick(m, 512)):
        for tn in [_pick(n_len, cap) for cap in n_caps]:
            for tk in [_pick(k_len, cap) for cap in k_caps]:
                used = tm * tn * (4 + 2 * per_out) + 2 * (tm * tk * a.dtype.itemsize + tk * tn * 2)
                key = ((m // tm) * (n_len // tn) * (k_len // tk), -tm, -tn)
                if used <= MM_VMEM_BUDGET and (best is None or key < best[0]):
                    best = (key, tm, tn, tk)
    assert best is not None, (m, n_len, k_len)
    return best[1:]


def _mm_fwd(name, a, w, kind, out_dtypes, epilogue=None, extras=(), a_fn=None, after=(), k_rows=None):
    if kind == "row":
        w, kind = w.reshape(-1, w.shape[-1]), "plain"
    m = a.shape[0]
    kk, n, ks, ns = _w_dims(w, kind)
    if k_rows is None:
        assert kk == a.shape[1]
    else:
        assert kind == "plain" and k_rows <= min(kk, a.shape[1])
        kk = k_rows
    if kind == "col":
        tm, tn, tk = _blocks(m, ns, (1024,), kk, (1024, 512), a, out_dtypes, extras)
    else:
        tm, tn, tk = _blocks(m, n, (2048, 1152, 1024), kk, (1024, 512), a, out_dtypes, extras)
    if kind == "col":
        nb = ns // tn
        b_spec = pl.BlockSpec((None, tk, tn), lambda i, j, k: (j // nb, k, j % nb))
    else:
        b_spec = pl.BlockSpec((tk, tn), lambda i, j, k: (k, j))
    mn_spec = pl.BlockSpec((tm, tn), lambda i, j, k: (i, j))
    return _mm_core(
        name, a, w, grid=(m // tm, n // tn, kk // tk),
        a_spec=pl.BlockSpec((tm, tk), lambda i, j, k: (i, k)), b_spec=b_spec, dims=NN, acc_shape=(tm, tn),
        outs=[SDS((m, n), dt) for dt in out_dtypes], out_spec=mn_spec, epilogue=epilogue, extras=extras,
        extra_specs=[mn_spec] * len(extras), a_fn=a_fn, after=after)


def _mm_dx(name, dy, w, kind, out_dtypes, epilogue=None, extras=(), after=(), k_rows=None):
    if kind == "row":
        w, kind = w.reshape(-1, w.shape[-1]), "plain"
    m, n_dim = dy.shape
    kk, n, ks, ns = _w_dims(w, kind)
    assert n == n_dim
    if k_rows is not None:
        assert kind == "plain" and k_rows <= kk
        kk = k_rows
    if kind == "col":
        tm, tn, tk = _blocks(m, kk, (2048, 1024), ns, (1024, 512), dy, out_dtypes, extras)
        kb = ns // tk
        b_spec = pl.BlockSpec((None, tn, tk), lambda i, j, k: (k // kb, j, k % kb))
    else:
        tm, tn, tk = _blocks(m, kk, (2048, 1024), n, (1152, 512), dy, out_dtypes, extras)
        b_spec = pl.BlockSpec((tn, tk), lambda i, j, k: (j, k))
    mk_spec = pl.BlockSpec((tm, tn), lambda i, j, k: (i, j))
    return _mm_core(
        name, dy, w, grid=(m // tm, kk // tn, n // tk),
        a_spec=pl.BlockSpec((tm, tk), lambda i, j, k: (i, k)), b_spec=b_spec, dims=NT, acc_shape=(tm, tn),
        outs=[SDS((m, kk), dt) for dt in out_dtypes], out_spec=mk_spec, epilogue=epilogue, extras=extras,
        extra_specs=[mk_spec] * len(extras), after=after)


def _in_proj_dt(name, a, w_t, first, h_n):
    m, kk = a.shape
    tm, tk = _pick(m, 1024), _pick(kk, 1024)
    blk = first // h_n
    return _mm_core(
        name, a, w_t, grid=(m // tm, 1, kk // tk), a_spec=pl.BlockSpec((tm, tk), lambda i, j, k: (i, k)),
        b_spec=pl.BlockSpec((h_n, tk), lambda i, j, k: (blk, k)), dims=NT, acc_shape=(tm, h_n),
        outs=[SDS((m, h_n), F32)], out_spec=pl.BlockSpec((tm, h_n), lambda i, j, k: (i, 0)))[0]


def _in_proj_dt_dx(name, d_dt, w_t, first, res):
    m, h_n = d_dt.shape
    n = w_t.shape[1]
    tm, tn = _pick(m, 1024), _pick(n, 1024)
    blk = first // h_n
    mn_spec = pl.BlockSpec((tm, tn), lambda i, j, k: (i, j))
    return _mm_core(
        name, d_dt, w_t, grid=(m // tm, n // tn, 1), a_spec=pl.BlockSpec((tm, h_n), lambda i, j, k: (i, 0)),
        b_spec=pl.BlockSpec((h_n, tn), lambda i, j, k: (blk, j)), dims=NN, acc_shape=(tm, tn),
        outs=[SDS((m, n), F32)], out_spec=mn_spec, epilogue=_residual_epilogue, extras=(res,),
        extra_specs=[mn_spec])[0]


def _in_proj_dw_t(name, dz, d_dt, x, after=()):
    m, n = x.shape
    zw, h_n = dz.shape[1], d_dt.shape[1]
    tm, tn, tk = _pick(zw, 1024), _pick(n, 2048), _pick(m, 1024)
    out = SDS((zw + h_n, n), BF16)
    x_spec = pl.BlockSpec((tk, tn), lambda i, j, k: (k, j))
    main = _mm_core(
        name, dz, x, grid=(zw // tm, n // tn, m // tk), a_spec=pl.BlockSpec((tk, tm), lambda i, j, k: (k, i)),
        b_spec=x_spec, dims=TN, acc_shape=(tm, tn), outs=[out], out_spec=pl.BlockSpec((tm, tn), lambda i, j, k: (i, j)),
        after=after)[0]
    blk = zw // h_n
    return _mm_core(
        name + "_dt", d_dt, x, grid=(1, n // tn, m // tk), a_spec=pl.BlockSpec((tk, h_n), lambda i, j, k: (k, 0)),
        b_spec=x_spec, dims=TN, acc_shape=(h_n, tn), outs=[out],
        out_spec=pl.BlockSpec((h_n, tn), lambda i, j, k: (blk, j)), carry=main)[0]


def _mm_dw(name, a, dy, kind, a_fn=None, after=()):
    m, kk = a.shape
    n = dy.shape[1]
    tk = _pick(m, 1024)
    if kind == "col":
        ns = n // N_DEV
        tm, tn = _pick(kk, 2048), _pick(ns, 1024)
        nb = ns // tn
        out = SDS((N_DEV, kk, ns), BF16)
        out_spec = pl.BlockSpec((None, tm, tn), lambda i, j, k: (j // nb, i, j % nb))
    else:
        tm, tn = _pick(kk, 1024), _pick(n, 2048)
        out = SDS((kk, n), BF16)
        out_spec = pl.BlockSpec((tm, tn), lambda i, j, k: (i, j))
    res = _mm_core(
        name, a, dy, grid=(kk // tm, n // tn, m // tk),
        a_spec=pl.BlockSpec((tk, tm), lambda i, j, k: (k, i)), b_spec=pl.BlockSpec((tk, tn), lambda i, j, k: (k, j)),
        dims=TN, acc_shape=(tm, tn), outs=[out], out_spec=out_spec, a_fn=a_fn, after=after)[0]
    return res.reshape(N_DEV, kk // N_DEV, n) if kind == "row" else res


def _rowwise(name, fn, ins, outs, rows, tile):
    arrays, specs = [], []
    for arr, kind in ins:
        arrays.append(arr)
        if kind == "row":
            specs.append(pl.BlockSpec((tile, arr.shape[1]), lambda i: (i, 0)))
        elif kind == "vec":
            specs.append(pl.BlockSpec(arr.shape, lambda i, nd=arr.ndim: (0,) * nd))
        else:
            specs.append(kind)
    out_shapes, out_specs, kinds = [], [], []
    for cols, dt, kind in outs:
        kinds.append(kind)
        if kind == "row":
            out_shapes.append(SDS((rows, cols), dt))
            out_specs.append(pl.BlockSpec((tile, cols), lambda i: (i, 0)))
        else:
            out_shapes.append(SDS((1, cols), F32))
            out_specs.append(pl.BlockSpec((1, cols), lambda i: (0, 0)))
    ni = len(arrays)
    has_acc = "acc" in kinds

    def body(*refs):
        vals = fn(*[r[...] for r in refs[:ni]])
        i = pl.program_id(0)
        for o, v, kind in zip(refs[ni:], vals, kinds):
            if kind == "row":
                o[...] = v.astype(o.dtype)
            else:
                @pl.when(i == 0)
                def _(o=o):
                    o[...] = jnp.zeros_like(o)

                o[...] += v

    return pl.pallas_call(
        body, name=name, grid=(rows // tile,), out_shape=out_shapes, in_specs=specs, out_specs=out_specs,
        compiler_params=_cp("arbitrary" if has_acc else "parallel"),
    )(*arrays)


def _ln_fwd(name, u, g, b, after=()):
    d = u.shape[1]

    def fn(u, g, b, *unused):
        mu = jnp.mean(u, axis=1, keepdims=True)
        xc = u - mu
        var = jnp.mean(xc * xc, axis=1, keepdims=True)
        y = xc * lax.rsqrt(var + LN_EPS) * g + b
        return y, y

    ins = [(u, "row"), (g, "vec"), (b, "vec")] + [(t, "vec") for t in after]
    return _rowwise(name, fn, ins, [(d, F32, "row"), (d, BF16, "row")], u.shape[0], 256)


def _ln_bwd(name, u, dy, g):
    d = u.shape[1]

    def fn(u, dy, g):
        mu = jnp.mean(u, axis=1, keepdims=True)
        xc = u - mu
        var = jnp.mean(xc * xc, axis=1, keepdims=True)
        rstd = lax.rsqrt(var + LN_EPS)
        xhat = xc * rstd
        dxhat = dy * g
        m1 = jnp.mean(dxhat, axis=1, keepdims=True)
        m2 = jnp.mean(dxhat * xhat, axis=1, keepdims=True)
        du = rstd * (dxhat - m1 - xhat * m2)
        return du, du, jnp.sum(dy * xhat, axis=0, keepdims=True), jnp.sum(dy, axis=0, keepdims=True)

    return _rowwise(name, fn, [(u, "row"), (dy, "row"), (g, "vec")],
                    [(d, F32, "row"), (d, BF16, "row"), (d, F32, "acc"), (d, F32, "acc")], u.shape[0], 256)


def _loss_bwd(name, y, target):
    d = y.shape[1]

    def fn(y, t):
        e = y - t
        return e * (1.0 / d), jnp.sum(e * e, axis=0, keepdims=True) * (0.5 / d)

    return _rowwise(name, fn, [(y, "row"), (target, "row")], [(d, F32, "row"), (d, F32, "acc")], y.shape[0], 256)


def _ple_bwd(name, dx, e, gate):
    d = dx.shape[1]

    def fn(dx, e, gate):
        return dx * e * gate * (1.0 - gate), dx * gate

    return _rowwise(name, fn, [(dx, "row"), (e, "row"), (gate, "row")], [(d, BF16, "row"), (d, BF16, "row")],
                    dx.shape[0], 256)


def _sigmoid(v):
    return 1.0 / (1.0 + jnp.exp(-v))


def _gated_rms_fwd(name, y, zx, norm_w, after=()):
    di = y.shape[1]

    def fn(y, z, w, *unused):
        yg = y * (z * _sigmoid(z))
        r = lax.rsqrt(jnp.mean(yg * yg, axis=1, keepdims=True) + RMS_EPS)
        return (yg * r * w,)

    z_spec = pl.BlockSpec((128, di), lambda i: (i, 0))
    ins = [(y, "row"), (zx, z_spec), (norm_w, "vec")] + [(t, "vec") for t in after]
    return _rowwise(name, fn, ins, [(di, BF16, "row")], y.shape[0], 128)[0]


def _gated_rms_bwd(name, y, zx, norm_w, dout):
    di = y.shape[1]

    def fn(y, z, w, dout):
        sg = _sigmoid(z)
        sz = z * sg
        yg = y * sz
        r = lax.rsqrt(jnp.mean(yg * yg, axis=1, keepdims=True) + RMS_EPS)
        dn = dout * w
        dyg = r * (dn - yg * (r * r) * jnp.mean(dn * yg, axis=1, keepdims=True))
        dy = dyg * sz
        dz = dyg * y * (sg * (1.0 + z * (1.0 - sg)))
        return dy, dz, jnp.sum(dout * yg * r, axis=0, keepdims=True)

    z_spec = pl.BlockSpec((128, di), lambda i: (i, 0))
    return _rowwise(name, fn, [(y, "row"), (zx, z_spec), (norm_w, "vec"), (dout, "row")],
                    [(di, F32, "row"), (di, BF16, "row"), (di, F32, "acc")], y.shape[0], 128)


def _shift_down(v, j, row):
    return jnp.where(row >= j, pltpu.roll(v, j, 0), 0.0)


def _shift_up(v, j, row):
    t = v.shape[0]
    return jnp.where(row < t - j, pltpu.roll(v, t - j, 0), 0.0)


def _pool_select(parts, g):
    return jnp.where(g == 0, parts[0], jnp.where(g == 1, parts[1], jnp.where(g == 2, parts[2], parts[3])))


def _pool_windows(name, x, transpose, scale_by=None, after=()):
    t, d = x.shape
    cg = d // 4
    cw = V7X_LANES
    per = cg // cw

    def body(*refs):
        x_ref, o_ref = refs[0], refs[-1]
        g = pl.program_id(0) // per
        xv = x_ref[...]
        row = lax.broadcasted_iota(jnp.int32, (t, 1), 0)
        cnt = jnp.minimum(row + 1, jnp.left_shift(2, g)).astype(F32)
        s = xv / cnt if transpose else xv
        parts = []
        for lg in POOL_WINDOW_LOG2:
            j = 1 << (lg - 1)
            s = s + (_shift_up(s, j, row) if transpose else _shift_down(s, j, row))
            parts.append(s)
        sel = _pool_select(parts, g)
        if transpose:
            o_ref[...] = ALPHA * refs[1][...] + sel - xv
        else:
            o_ref[...] = (sel / cnt - xv).astype(o_ref.dtype)

    col = pl.BlockSpec((t, cw), lambda j: (0, j))
    ins = [x] if scale_by is None else [x, scale_by]
    return pl.pallas_call(
        body, name=name, grid=(d // cw,), out_shape=SDS((t, d), F32 if transpose else BF16),
        in_specs=[col] * len(ins) + [ANY] * len(after), out_specs=col, compiler_params=_cp("parallel"),
    )(*ins, *after)


def _pool_mm(name, pooled, w, scale, x):
    t, d = x.shape
    cg = d // 4
    tm = _pick(t, 1024)

    def body(p_ref, w_ref, s_ref, x_ref, u_ref, h_ref):
        h = jnp.dot(p_ref[...], w_ref[...], preferred_element_type=F32)
        h_ref[...] = h
        u_ref[...] = ALPHA * x_ref[...] + h * s_ref[...]

    blk = pl.BlockSpec((tm, cg), lambda g, i: (i, g))
    return pl.pallas_call(
        body, name=name, grid=(4, t // tm), out_shape=[SDS((t, d), F32), SDS((t, d), F32)],
        in_specs=[blk, pl.BlockSpec((None, cg, cg), lambda g, i: (g, 0, 0)), pl.BlockSpec((1, cg), lambda g, i: (0, g)),
                  blk],
        out_specs=[blk, blk], compiler_params=_cp("parallel", "parallel"),
    )(pooled, w, scale, x)


def _pool_bwd_mm(name, du, hraw, w, scale):
    t, d = du.shape
    cg = d // 4
    tm = _pick(t, 1024)

    def body(du_ref, h_ref, w_ref, s_ref, dh_ref, dp_ref, ds_ref):
        @pl.when(pl.program_id(1) == 0)
        def _():
            ds_ref[...] = jnp.zeros_like(ds_ref)

        duv = du_ref[...]
        ds_ref[...] += jnp.sum(duv * h_ref[...], axis=0, keepdims=True)
        dh = (duv * s_ref[...]).astype(BF16)
        dh_ref[...] = dh
        dp_ref[...] = lax.dot_general(dh, w_ref[...], NT, preferred_element_type=F32)

    blk = pl.BlockSpec((tm, cg), lambda g, i: (i, g))
    vec = pl.BlockSpec((1, cg), lambda g, i: (0, g))
    return pl.pallas_call(
        body, name=name, grid=(4, t // tm), out_shape=[SDS((t, d), BF16), SDS((t, d), F32), SDS((1, d), F32)],
        in_specs=[blk, blk, pl.BlockSpec((None, cg, cg), lambda g, i: (g, 0, 0)), vec],
        out_specs=[blk, blk, vec], compiler_params=_cp("parallel", "arbitrary"),
    )(du, hraw, w, scale)


def _pool_dw(name, pooled, dh):
    t, d = pooled.shape
    cg = d // 4
    tk = _pick(t, 512)
    nk = t // tk

    def body(p_ref, dh_ref, o_ref, acc):
        k = pl.program_id(1)

        @pl.when(k == 0)
        def _():
            acc[...] = jnp.zeros_like(acc)

        acc[...] += lax.dot_general(p_ref[...], dh_ref[...], TN, preferred_element_type=F32)

        @pl.when(k == nk - 1)
        def _():
            o_ref[...] = acc[...].astype(o_ref.dtype)

    blk = pl.BlockSpec((tk, cg), lambda g, k: (k, g))
    return pl.pallas_call(
        body, name=name, grid=(4, nk), out_shape=SDS((4, cg, cg), BF16), in_specs=[blk, blk],
        out_specs=pl.BlockSpec((None, cg, cg), lambda g, k: (g, 0, 0)), scratch_shapes=[pltpu.VMEM((cg, cg), F32)],
        compiler_params=_cp("parallel", "arbitrary"),
    )(pooled, dh)


def _conv_pre(u, w_ref, b_ref, row):
    pre = b_ref[...] + _shift_down(u, 3, row) * w_ref[0:1, :]
    pre = pre + _shift_down(u, 2, row) * w_ref[1:2, :]
    pre = pre + _shift_down(u, 1, row) * w_ref[2:3, :]
    return pre + u * w_ref[3:4, :]


def _conv_fwd(name, zx, conv_w, conv_b, di):
    t = zx.shape[0]
    cd = conv_w.shape[1]
    cw = _pick(cd, 256)
    off = di // cw

    def body(u_ref, w_ref, b_ref, o_ref):
        row = lax.broadcasted_iota(jnp.int32, (t, 1), 0)
        pre = _conv_pre(u_ref[...], w_ref, b_ref, row)
        o_ref[...] = pre * _sigmoid(pre)

    return pl.pallas_call(
        body, name=name, grid=(cd // cw,), out_shape=SDS((t, cd), F32),
        in_specs=[pl.BlockSpec((t, cw), lambda j: (0, off + j)), pl.BlockSpec((CONV_WIDTH, cw), lambda j: (0, j)),
                  pl.BlockSpec((1, cw), lambda j: (0, j))],
        out_specs=pl.BlockSpec((t, cw), lambda j: (0, j)), compiler_params=_cp("parallel"),
    )(zx, conv_w, conv_b)


def _conv_bwd(name, zx, conv_w, conv_b, dact, di, first):
    t, cd = dact.shape
    cw = _pick(cd, 256)
    off, woff = (di + first) // cw, first // cw

    def body(u_ref, w_ref, b_ref, da_ref, du_ref, dw_ref, db_ref):
        row = lax.broadcasted_iota(jnp.int32, (t, 1), 0)
        u = u_ref[...]
        pre = _conv_pre(u, w_ref, b_ref, row)
        sg = _sigmoid(pre)
        dpre = da_ref[...] * (sg * (1.0 + pre * (1.0 - sg)))
        du = dpre * w_ref[3:4, :]
        for j in (1, 2, 3):
            du = du + _shift_up(dpre, j, row) * w_ref[3 - j:4 - j, :]
            dw_ref[3 - j:4 - j, :] = jnp.sum(dpre * _shift_down(u, j, row), axis=0, keepdims=True)
        dw_ref[3:4, :] = jnp.sum(dpre * u, axis=0, keepdims=True)
        db_ref[...] = jnp.sum(dpre, axis=0, keepdims=True)
        du_ref[...] = du.astype(du_ref.dtype)

    wspec = pl.BlockSpec((CONV_WIDTH, cw), lambda j: (0, j))
    bspec = pl.BlockSpec((1, cw), lambda j: (0, j))
    ospec = pl.BlockSpec((t, cw), lambda j: (0, j))
    return pl.pallas_call(
        body, name=name, grid=(cd // cw,), out_shape=[SDS((t, cd), BF16), SDS((CONV_WIDTH, cd), F32), SDS((1, cd), F32)],
        in_specs=[pl.BlockSpec((t, cw), lambda j: (0, off + j)), pl.BlockSpec((CONV_WIDTH, cw), lambda j: (0, woff + j)),
                  pl.BlockSpec((1, cw), lambda j: (0, woff + j)), ospec],
        out_specs=[ospec, wspec, bspec], compiler_params=_cp("parallel"),
    )(zx, conv_w, conv_b, dact)


def _expand_heads(name, arrays, h_n, p):
    t = arrays[0].shape[0]
    n = len(arrays)
    w = _pick(h_n * p, 512)

    def body(*refs):
        j = pl.program_id(0)
        head = lax.broadcasted_iota(jnp.int32, (V7X_LANES, w), 0)
        lane = lax.broadcasted_iota(jnp.int32, (V7X_LANES, w), 1)
        spread = (head == j * (w // p) + lane // p).astype(BF16)
        for a_ref, o_ref in zip(refs[:n], refs[n:]):
            rest = a_ref[...]
            out = jnp.zeros((t, w), F32)
            for _ in range(3):
                piece = rest.astype(BF16)
                out = out + jnp.dot(piece, spread, preferred_element_type=F32)
                rest = rest - piece.astype(F32)
            o_ref[...] = out

    full = pl.BlockSpec((t, V7X_LANES), lambda j: (0, 0))
    return pl.pallas_call(
        body, name=name, grid=(h_n * p // w,), out_shape=[SDS((t, h_n * p), F32)] * n, in_specs=[full] * n,
        out_specs=[pl.BlockSpec((t, w), lambda j: (0, j))] * n, compiler_params=_cp("parallel"),
    )(*arrays)


def _softplus(v):
    return jnp.maximum(v, 0.0) + jnp.log(1.0 + jnp.exp(-jnp.abs(v)))


def _dt_fwd(name, zx, bias, a_log, col_block):
    t = zx.shape[0]

    def body(r_ref, b_ref, al_ref, dt_ref, acs_ref, ein_ref, eout_ref):
        row = lax.broadcasted_iota(jnp.int32, (t, 1), 0) % CHUNK
        dt = _softplus(r_ref[...] + b_ref[...])
        da = dt * (-jnp.exp(al_ref[...]))
        s, r = da, da
        j = 1
        while j < CHUNK:
            s = s + jnp.where(row >= j, pltpu.roll(s, j, 0), 0.0)
            r = r + jnp.where(row < CHUNK - j, pltpu.roll(r, t - j, 0), 0.0)
            j *= 2
        dt_ref[...] = dt
        acs_ref[...] = s
        ein_ref[...] = jnp.exp(s)
        eout_ref[...] = jnp.exp(r - da)

    vec = pl.BlockSpec((1, V7X_LANES), lambda i: (0, 0))
    full = pl.BlockSpec((t, V7X_LANES), lambda i: (0, 0))
    return pl.pallas_call(
        body, name=name, grid=(1,), out_shape=[SDS((t, V7X_LANES), F32)] * 4,
        in_specs=[pl.BlockSpec((t, V7X_LANES), lambda i: (0, col_block)), vec, vec], out_specs=[full] * 4,
        compiler_params=_cp("arbitrary"),
    )(zx, bias, a_log)


def _dt_bwd(name, zx, bias, a_log, d_acs, d_dt, col_block):
    t = zx.shape[0]

    def body(r_ref, b_ref, al_ref, da_ref, dd_ref, draw_ref, db_ref, dal_ref):
        row = lax.broadcasted_iota(jnp.int32, (t, 1), 0) % CHUNK
        pre = r_ref[...] + b_ref[...]
        dt = _softplus(pre)
        a = -jnp.exp(al_ref[...])
        s = da_ref[...]
        j = 1
        while j < CHUNK:
            s = s + jnp.where(row < CHUNK - j, pltpu.roll(s, t - j, 0), 0.0)
            j *= 2
        ddt = dd_ref[...] + s * a
        dal_ref[...] = jnp.sum(s * dt, axis=0, keepdims=True) * a
        draw = ddt * _sigmoid(pre)
        db_ref[...] = jnp.sum(draw, axis=0, keepdims=True)
        draw_ref[...] = draw.astype(draw_ref.dtype)

    vec = pl.BlockSpec((1, V7X_LANES), lambda i: (0, 0))
    full = pl.BlockSpec((t, V7X_LANES), lambda i: (0, 0))
    return pl.pallas_call(
        body, name=name, grid=(1,), out_shape=[SDS((t, V7X_LANES), BF16), SDS((1, V7X_LANES), F32), SDS((1, V7X_LANES), F32)],
        in_specs=[pl.BlockSpec((t, V7X_LANES), lambda i: (0, col_block)), vec, vec, full, full],
        out_specs=[full, vec, vec], compiler_params=_cp("arbitrary"),
    )(zx, bias, a_log, d_acs, d_dt)


def _ssd_specs(t, di, g_n, hpg, p, rev):
    rows = SSD_STEP_CHUNKS * CHUNK
    nc = t // rows
    w = hpg * p
    nb = di // D_STATE

    def cc(c):
        return nc - 1 - c if rev else c

    return dict(
        xs=pl.BlockSpec((rows, w), lambda g, c: (cc(c), g)),
        bm=pl.BlockSpec((rows, D_STATE), lambda g, c: (cc(c), nb + g)),
        cm=pl.BlockSpec((rows, D_STATE), lambda g, c: (cc(c), nb + g_n + g)),
        col=pl.BlockSpec((None, rows, hpg), lambda g, c: (g, cc(c), 0)),
        rowv=pl.BlockSpec((None, hpg, rows), lambda g, c: (g, 0, cc(c))),
        head=pl.BlockSpec((None, 1, hpg), lambda g, c: (g, 0, 0)),
        lanes=pl.BlockSpec((1, w), lambda g, c: (0, g)),
        bc=pl.BlockSpec((rows, D_STATE), lambda g, c: (cc(c), g)),
        prev=pl.BlockSpec((SSD_STEP_CHUNKS, None, D_STATE, w), lambda g, c: (cc(c), g, 0, 0)),
        seg=pl.BlockSpec((w, V7X_LANES), lambda g, c: (0, 0)),
    )


def _decay_masks(cb, ac, ar, heads):
    li = lax.broadcasted_iota(jnp.int32, (CHUNK, CHUNK), 0)
    si = lax.broadcasted_iota(jnp.int32, (CHUNK, CHUNK), 1)
    lms = [jnp.exp(jnp.where(li >= si, ac[:, hh:hh + 1] - ar[hh:hh + 1, :], -jnp.inf)) for hh in heads]
    return lms, [(cb * lm).astype(BF16) for lm in lms]


def _ssd_fwd(name, xbc, dt_x, ein_x, eout_x, a_col, a_row, d_x, di, g_n, hpg, p):
    t = xbc.shape[0]
    nc = t // CHUNK
    w = hpg * p
    assert 2 * p == V7X_LANES and hpg % 2 == 0 and nc % SSD_STEP_CHUNKS == 0
    sp = _ssd_specs(t, di, g_n, hpg, p, False)

    def body(xs_ref, bm_ref, cm_ref, dt_ref, ein_ref, eout_ref, ac_ref, ar_ref, d_ref, y_ref, prev_ref, h_ref):
        @pl.when(pl.program_id(1) == 0)
        def _():
            h_ref[...] = jnp.zeros_like(h_ref)

        first = lax.broadcasted_iota(jnp.int32, (1, V7X_LANES), 1) < p
        for sub in range(SSD_STEP_CHUNKS):
            r = slice(sub * CHUNK, (sub + 1) * CHUNK)
            bm = bm_ref[r, :].astype(BF16)
            cm = cm_ref[r, :].astype(BF16)
            cb = lax.dot_general(cm, bm, NT, preferred_element_type=F32)
            xs = xs_ref[r, :]
            e_in = ein_ref[r, :]
            xdt = xs * dt_ref[r, :]
            ac, ar = ac_ref[r, :], ar_ref[:, r]
            ys = []
            for pr in range(hpg // 2):
                _, ms = _decay_masks(cb, ac, ar, (2 * pr, 2 * pr + 1))
                xp = xdt[:, pr * V7X_LANES:(pr + 1) * V7X_LANES]
                rhs = jnp.concatenate([jnp.where(first, xp, 0.0), jnp.where(first, 0.0, xp)], axis=0).astype(BF16)
                ys.append(jnp.dot(jnp.concatenate(ms, axis=1), rhs, preferred_element_type=F32))
            h_prev = h_ref[...]
            prev_ref[sub] = h_prev
            y = jnp.concatenate(ys, axis=1) + jnp.dot(cm, h_prev.astype(BF16), preferred_element_type=F32) * e_in
            y_ref[r, :] = y + xs * d_ref[...]
            st = lax.dot_general(bm, (xdt * eout_ref[r, :]).astype(BF16), TN, preferred_element_type=F32)
            h_ref[...] = e_in[CHUNK - 1:CHUNK, :] * h_prev + st

    return pl.pallas_call(
        body, name=name, grid=(g_n, nc // SSD_STEP_CHUNKS),
        out_shape=[SDS((t, di), F32), SDS((nc, g_n, D_STATE, w), F32)],
        in_specs=[sp["xs"], sp["bm"], sp["cm"], sp["xs"], sp["xs"], sp["xs"], sp["col"], sp["rowv"], sp["lanes"]],
        out_specs=[sp["xs"], sp["prev"]], scratch_shapes=[pltpu.VMEM((D_STATE, w), F32)],
        compiler_params=_cp("parallel", "arbitrary"),
    )(xbc, xbc, xbc, dt_x, ein_x, eout_x, a_col, a_row, d_x)


def _head_sums(v, seg):
    hi = v.astype(BF16)
    lo = (v - hi.astype(F32)).astype(BF16)
    return jnp.dot(hi, seg, preferred_element_type=F32) + jnp.dot(lo, seg, preferred_element_type=F32)


def _head_totals(v, seg):
    part = v[0:8]
    for r in range(8, v.shape[0], 8):
        part = part + v[r:r + 8]
    return jnp.sum(_head_sums(part, seg), axis=0, keepdims=True)


def _ssd_bwd(name, xbc, dt_x, ein_x, eout_x, a_col, a_row, d_x, prev, dy, di, g_n, hpg, p):
    t = xbc.shape[0]
    nc = t // CHUNK
    w = hpg * p
    sp = _ssd_specs(t, di, g_n, hpg, p, True)
    seg = (lax.broadcasted_iota(jnp.int32, (w, V7X_LANES), 0) // p
           == lax.broadcasted_iota(jnp.int32, (w, V7X_LANES), 1)).astype(BF16)

    def body(xs_ref, bm_ref, cm_ref, dt_ref, ein_ref, eout_ref, ac_ref, ar_ref, d_ref, prev_ref, dy_ref,
             seg_ref, dx_ref, dbm_ref, dcm_ref, ddt_ref, dacs_ref, dd_ref, dh_ref):
        @pl.when(pl.program_id(1) == 0)
        def _():
            dh_ref[...] = jnp.zeros_like(dh_ref)
            dd_ref[...] = jnp.zeros_like(dd_ref)

        first = lax.broadcasted_iota(jnp.int32, (1, V7X_LANES), 1) < p
        last_row = lax.broadcasted_iota(jnp.int32, (CHUNK, 1), 0) == CHUNK - 1
        seg_m = seg_ref[...]
        d_skip = d_ref[...]
        for sub in reversed(range(SSD_STEP_CHUNKS)):
            r = slice(sub * CHUNK, (sub + 1) * CHUNK)
            bm = bm_ref[r, :].astype(BF16)
            cm = cm_ref[r, :].astype(BF16)
            cb = lax.dot_general(cm, bm, NT, preferred_element_type=F32)
            xs, dy, e_in, e_out, dt_l = xs_ref[r, :], dy_ref[r, :], ein_ref[r, :], eout_ref[r, :], dt_ref[r, :]
            ac, ar = ac_ref[r, :], ar_ref[:, r]
            xdt = xs * dt_l
            h_prev = prev_ref[sub]
            h_prev_b = h_prev.astype(BF16)
            dh_next = dh_ref[...]
            dh_next_b = dh_next.astype(BF16)
            dy_e = (dy * e_in).astype(BF16)
            d_cm = lax.dot_general(dy_e, h_prev_b, NT, preferred_element_type=F32)
            dh_ref[...] = (e_in[CHUNK - 1:CHUNK, :] * dh_next
                           + lax.dot_general(cm, dy_e, TN, preferred_element_type=F32))
            q = jnp.dot(bm, dh_next_b, preferred_element_type=F32)
            xf = xdt * e_out
            d_bm = lax.dot_general(xf.astype(BF16), dh_next_b, NT, preferred_element_type=F32)
            d_cb = jnp.zeros((CHUNK, CHUNK), F32)
            parts, w_parts = [], []
            for pr in range(hpg // 2):
                lanes = slice(pr * V7X_LANES, (pr + 1) * V7X_LANES)
                lms, ms = _decay_masks(cb, ac, ar, (2 * pr, 2 * pr + 1))
                xp = xdt[:, lanes]
                xp_b = xp.astype(BF16)
                dyp = dy[:, lanes]
                halves = [jnp.where(first, dyp, 0.0).astype(BF16), jnp.where(first, 0.0, dyp).astype(BF16)]
                for lm, half in zip(lms, halves):
                    d_cb = d_cb + lax.dot_general(half, xp_b, NT, preferred_element_type=F32) * lm
                dxd = lax.dot_general(jnp.concatenate(ms, axis=0), jnp.concatenate(halves, axis=0), TN,
                                      preferred_element_type=F32)
                stacked = jnp.concatenate([jnp.where(first, xp, 0.0), jnp.where(first, 0.0, xp)], axis=0).astype(BF16)
                y_diag = jnp.dot(jnp.concatenate(ms, axis=1), stacked, preferred_element_type=F32)
                parts.append(dxd)
                w_parts.append(dyp.astype(BF16).astype(F32) * y_diag - xp_b.astype(F32) * dxd)
            d_xdt = jnp.concatenate(parts, axis=1) + q * e_out
            dx_ref[r, :] = d_xdt * dt_l + dy * d_skip
            ch = jnp.dot(cm, h_prev_b, preferred_element_type=F32)
            qx = q * xf
            s_a = _head_sums(dy * ch * e_in - qx + jnp.concatenate(w_parts, axis=1), seg_m)[:, :hpg]
            d_last = (_head_totals(qx, seg_m)[:, :hpg]
                      + jnp.exp(ac[CHUNK - 1:CHUNK, :]) * _head_totals(dh_next * h_prev, seg_m)[:, :hpg])
            ddt_ref[r, :] = _head_sums(d_xdt * xs, seg_m)[:, :hpg]
            dacs_ref[r, :] = s_a + jnp.where(last_row, d_last, 0.0)
            dd_ref[...] += _head_totals(dy * xs, seg_m)[:, :hpg]
            d_cb_b = d_cb.astype(BF16)
            dcm_ref[r, :] = d_cm + jnp.dot(d_cb_b, bm, preferred_element_type=F32)
            dbm_ref[r, :] = d_bm + lax.dot_general(d_cb_b, cm, TN, preferred_element_type=F32)

    gn = g_n * D_STATE
    return pl.pallas_call(
        body, name=name, grid=(g_n, nc // SSD_STEP_CHUNKS),
        out_shape=[SDS((t, di), F32), SDS((t, gn), F32), SDS((t, gn), F32), SDS((g_n, t, hpg), F32),
                   SDS((g_n, t, hpg), F32), SDS((g_n, 1, hpg), F32)],
        in_specs=[sp["xs"], sp["bm"], sp["cm"], sp["xs"], sp["xs"], sp["xs"], sp["col"], sp["rowv"],
                  sp["lanes"], sp["prev"], sp["xs"], sp["seg"]],
        out_specs=[sp["xs"], sp["bc"], sp["bc"], sp["col"], sp["col"], sp["head"]],
        scratch_shapes=[pltpu.VMEM((D_STATE, w), F32)],
        compiler_params=_cp("parallel", "arbitrary"),
    )(xbc, xbc, xbc, dt_x, ein_x, eout_x, a_col, a_row, d_x, prev, dy, seg)


def _as3d(a):
    return a.reshape(a.shape[0], -1, a.shape[-1])


def _pair_sum(name, own, recv, core):
    shape = recv.shape
    cols = shape[-1]
    own3, recv3 = own.reshape(8, -1, cols), recv.reshape(4, -1, cols)
    rows = recv3.shape[1]
    tr = _row_tile(rows, cols, 2)

    def body(c_ref, a_ref, b_ref, o_ref):
        o_ref[...] = (a_ref[...].astype(F32) + b_ref[...].astype(F32)).astype(o_ref.dtype)

    blk = pl.BlockSpec((None, tr, cols), lambda q, i, c_ref: (q, i, 0))
    out = pl.pallas_call(
        body, name=name, out_shape=SDS(recv3.shape, recv.dtype),
        grid_spec=pltpu.PrefetchScalarGridSpec(
            num_scalar_prefetch=1, grid=(4, rows // tr),
            in_specs=[pl.BlockSpec((None, tr, cols), lambda q, i, c_ref: (2 * q + c_ref[0], i, 0)), blk], out_specs=blk),
        compiler_params=_cp("parallel", "parallel"),
    )(core, own3, recv3)
    return out.reshape(shape)


def _adamw(name, w, m, v, parts, layer, prev=None, sel=None):
    lyr, rows, cols = w.shape
    n = len(parts)
    by_rows = rows % 16 == 0
    tr, tc = (_row_tile(rows, cols), cols) if by_rows else (rows, _pick(cols, 256))
    np_ = 0 if prev is None else 4
    if sel is None:
        sel = jnp.zeros((1,), jnp.int32)

    def body(sel_ref, *refs):
        w_ref, m_ref, v_ref = refs[:3]
        p_refs = refs[3:3 + n]
        g_ref, d_ref, nm_ref, nv_ref = refs[3 + n + np_:]
        g = p_refs[0][...].astype(F32)
        for r in p_refs[1:]:
            g = g + r[...].astype(F32)
        nm = ADAM_B1 * m_ref[...] + (1.0 - ADAM_B1) * g
        nv = ADAM_B2 * v_ref[...] + (1.0 - ADAM_B2) * (g * g)
        m_hat = nm / (1.0 - ADAM_B1 ** ADAM_STEP)
        v_hat = nv / (1.0 - ADAM_B2 ** ADAM_STEP)
        g_ref[...] = g
        d_ref[...] = -ADAM_LR * (m_hat / (jnp.sqrt(v_hat) + ADAM_EPS) + ADAM_WD * w_ref[...])
        nm_ref[...] = nm
        nv_ref[...] = nv

    def at(lead):
        return pl.BlockSpec((None, tr, tc), lambda i, s: (lead(s), i, 0) if by_rows else (lead(s), 0, i))

    lspec = at(lambda s: layer)
    pspecs = [at(lambda s: s[0]) if q is None else at(lambda s, q=q: q) for _, q in parts]
    aliases = {} if prev is None else {4 + n + q: q for q in range(4)}
    return pl.pallas_call(
        body, name=name, out_shape=[SDS(w.shape, F32)] * 4,
        grid_spec=pltpu.PrefetchScalarGridSpec(
            num_scalar_prefetch=1, grid=(rows // tr if by_rows else cols // tc,),
            in_specs=[lspec] * 3 + pspecs + [ANY] * np_, out_specs=[lspec] * 4),
        input_output_aliases=aliases, compiler_params=_cp("parallel"),
    )(sel, w, m, v, *[arr for arr, _ in parts], *(prev or ()))


def _sum8(name, parts):
    rows = parts.shape[1]

    def body(p_ref, o_ref):
        s = p_ref[0]
        for q in range(1, N_DEV):
            s = s + p_ref[q]
        o_ref[...] = s

    return pl.pallas_call(
        body, name=name, grid=(1,), out_shape=SDS((rows, V7X_LANES), F32),
        in_specs=[pl.BlockSpec((N_DEV, rows, V7X_LANES), lambda i: (0, 0, 0))],
        out_specs=pl.BlockSpec((rows, V7X_LANES), lambda i: (0, 0)), compiler_params=_cp("arbitrary"),
    )(parts)


def _pack(vectors, align):
    flat = jnp.concatenate([v.reshape(-1) for v in vectors])
    pad = (-flat.shape[0]) % align
    if pad:
        flat = jnp.concatenate([flat, jnp.zeros((pad,), F32)])
    return flat.reshape(-1, V7X_LANES)


def _unpack(packed, shapes):
    flat = packed.reshape(-1)
    out, o = [], 0
    for s in shapes:
        size = 1
        for dim in s:
            size *= dim
        out.append(flat[o:o + size].reshape(s))
        o += size
    return out


def _residual_epilogue(acc, res):
    return (ALPHA * res + acc,)


def _plain_add_epilogue(acc, res):
    return (res + acc,)


def _gate_epilogue(acc, y, e):
    gate = _sigmoid(acc)
    xn = y + gate * e
    return xn, gate, xn


def _relu2(pre):
    r = jnp.maximum(pre, 0.0)
    return r * r


def _relu2_bwd_epilogue(acc, pre):
    return (acc * (2.0 * jnp.maximum(pre.astype(F32), 0.0)),)


def _tail_fwd(tag, u_a, wts, lng, lnb, p_l, finish_w1, finish_w2, after=()):
    y1, y1_b = _ln_fwd(f"ln1_{tag}", u_a, lng[0], lnb[0], after)
    finish_w1(y1_b)
    (pre,) = _mm_fwd(f"mlp1_{tag}", y1_b, wts["w1"], "col", [BF16])
    finish_w2(pre)
    (u_b,) = _mm_fwd(f"mlp2_{tag}", pre, wts["w2"], "row", [F32], _residual_epilogue, (y1,), a_fn=_relu2)
    y2, y2_b = _ln_fwd(f"ln2_{tag}", u_b, lng[1], lnb[1])
    (e,) = _mm_fwd(f"ple_{tag}", p_l, wts["plew"], "col", [F32])
    xn, gate, xn_b = _mm_fwd(f"gate_{tag}", y2_b, wts["gate"], "row", [F32, F32, BF16], _gate_epilogue, (y2, e))
    return xn, xn_b, (u_a, y1_b, pre, u_b, y2_b, e, gate)


def _tail_bwd(tag, dxn, saved, wts, lng, p_l, emit, advance, toks):
    u_a, y1_b, pre, u_b, y2_b, e, gate = saved
    dgpre, de = _ple_bwd(f"ple_bwd_{tag}", dxn, e, gate)
    toks = emit(f"{tag}_ple", dict(gate=_mm_dw(f"gate_dw_{tag}", y2_b, dgpre, "row", after=toks),
                                   plew=_mm_dw(f"ple_dw_{tag}", p_l, de, "col")))
    (dy2,) = _mm_dx(f"gate_dx_{tag}", dgpre, wts["gate"], "row", [F32], _plain_add_epilogue, (dxn,), after=toks)
    toks = advance((dy2,))
    du_b, du_b16, dg2, db2 = _ln_bwd(f"ln2_bwd_{tag}", u_b, dy2, lng[1])
    toks = emit(f"{tag}_w2", dict(w2=_mm_dw(f"mlp2_dw_{tag}", pre, du_b16, "row", a_fn=_relu2, after=toks)))
    (dpre,) = _mm_dx(f"mlp2_dx_{tag}", du_b16, wts["w2"], "row", [BF16], _relu2_bwd_epilogue, (pre,), after=toks)
    toks = advance((dpre,))
    toks = emit(f"{tag}_w1", dict(w1=_mm_dw(f"mlp1_dw_{tag}", y1_b, dpre, "col", after=toks)))
    (dy1,) = _mm_dx(f"mlp1_dx_{tag}", dpre, wts["w1"], "col", [F32], _residual_epilogue, (du_b,), after=toks)
    toks = advance((dy1,))
    du_a, du_a16, dg1, db1 = _ln_bwd(f"ln1_bwd_{tag}", u_a, dy1, lng[0])
    return du_a, du_a16, [dg1, dg2], [db1, db2], toks


def _to_slots(a, axis):
    shape = a.shape
    per = shape[axis] // N_DEV
    v = a.reshape(shape[:axis] + (N_DEV, per) + shape[axis + 1:])
    return jnp.moveaxis(v, axis, 0)


def _pad_lanes(a):
    return jnp.pad(a, [(0, 0)] * (a.ndim - 1) + [(0, V7X_LANES - a.shape[-1])])


def kernel(x, p, pool_w, pool_scale, ssm_in_w, ssm_conv_w, ssm_conv_b, ssm_dt_bias, ssm_a_log, ssm_d, ssm_norm_w, ssm_out_w, mlp_w1, mlp_w2, ln_g, ln_b, ple_w, ple_gate_w, loss_target, m_pool_w, m_pool_scale, m_ssm_in_w, m_ssm_conv_w, m_ssm_conv_b, m_ssm_dt_bias, m_ssm_a_log, m_ssm_d, m_ssm_norm_w, m_ssm_out_w, m_mlp_w1, m_mlp_w2, m_ln_g, m_ln_b, m_ple_w, m_ple_gate_w, v_pool_w, v_pool_scale, v_ssm_in_w, v_ssm_conv_w, v_ssm_conv_b, v_ssm_dt_bias, v_ssm_a_log, v_ssm_d, v_ssm_norm_w, v_ssm_out_w, v_mlp_w1, v_mlp_w2, v_ln_g, v_ln_b, v_ple_w, v_ple_gate_w):
    t, d = x.shape[1:]
    h_n = ssm_dt_bias.shape[-1]
    di_s, cd_s, dp_s, d_s = ssm_norm_w.shape[-1], ssm_conv_b.shape[-1], ssm_in_w.shape[-1], ln_g.shape[-1]
    di, cd, dp = N_DEV * di_s, N_DEV * cd_s, N_DEV * dp_s
    p_dim = di // h_n
    g_n = (cd - di) // (2 * D_STATE)
    hpg = h_n // g_n
    zw = di + cd
    assert h_n <= V7X_LANES and dp == zw + h_n and zw % V7X_LANES == 0 and zw % h_n == 0
    cg = d // 4
    me = 4 * lax.axis_index("x") + 2 * lax.axis_index("y") + lax.axis_index("c")

    x0, target = x[0], loss_target[0]
    p_l = [p[0, 0].astype(BF16), p[1, 0].astype(BF16)]

    small_shapes = [(CONV_WIDTH, cd_s), (1, cd_s), (1, di_s), (2, 2, d_s), (2, 2, d_s)]
    small = _pack([ssm_conv_w[0], ssm_conv_b, ssm_norm_w, ln_g, ln_b], 8 * V7X_LANES)
    first = [w.astype(BF16) for w in (pool_w[0], mlp_w1[0])]

    def gather_start(tag, own, after):
        lands = [lax.empty((N_DEV,) + w.shape, w.dtype) for w in own]
        return _async_start(f"ag_{tag}_start", _ag_first_copies, (4 * len(own),), own, lands, after)

    def gather_pass(tag, handle, after):
        n = len(handle[2]) // 2
        own, lands = _async_wait(f"ag_{tag}_wait", _ag_first_copies, handle, n, after)
        return _async_start(f"ag_{tag}_forward_start", _ag_forward_copies, (3 * n,), [], lands), own

    def gather_done(tag, passed, after=()):
        fwd, own = passed
        _, lands = _async_wait(f"ag_{tag}_forward_wait", _ag_forward_copies, fwd, 0, after)
        return [lax.dynamic_update_slice_in_dim(g, w[None], me, 0) for g, w in zip(lands, own)]

    def gather_finish(tag, handle, after):
        return gather_done(tag, gather_pass(tag, handle, after))

    ag_first = gather_start("first", first + [small], ())
    zero = ag_first[3][0, 0]
    own_l0 = [(w + zero).astype(BF16) for w in (mlp_w2[0], ple_w[0], ple_gate_w[0])]
    own_ssm = [(ssm_in_w[0] + zero).astype(BF16).T]
    own_out = [(ssm_out_w[0] + zero).astype(BF16)]
    own_mlp = [(w + zero).astype(BF16) for w in (mlp_w1[1], mlp_w2[1], ple_w[1], ple_gate_w[1])]
    ag_l0 = gather_start("l0", own_l0, (ag_first[3],))
    ag_ssm = gather_start("ssm", own_ssm, (ag_l0[3],))
    ag_out = gather_start("out", own_out, (ag_ssm[3],))
    ag_mlp = gather_start("mlp1", own_mlp, (ag_out[3],))
    pooled = _pool_windows("pool_fwd", x0, False, after=(ag_mlp[3],))
    pool_g, w1_0, small_g = gather_finish("first", ag_first, (pooled,))
    pool_full = pool_g.transpose(1, 0, 2, 3).reshape(4, cg, cg)
    sm = small_g.reshape(N_DEV, -1)
    o = 0
    parts = []
    for shp in small_shapes:
        size = 1
        for s in shp:
            size *= s
        parts.append(sm[:, o:o + size].reshape((N_DEV,) + shp))
        o += size
    conv_w_full = parts[0].transpose(1, 0, 2).reshape(CONV_WIDTH, cd)
    conv_b_full = parts[1].transpose(1, 0, 2).reshape(1, cd)
    norm_w_full = parts[2].transpose(1, 0, 2).reshape(1, di)
    ln_g_full = parts[3].transpose(1, 2, 0, 3).reshape(2, 2, 1, d)
    ln_b_full = parts[4].transpose(1, 2, 0, 3).reshape(2, 2, 1, d)
    bias128, alog128 = _pad_lanes(ssm_dt_bias), _pad_lanes(ssm_a_log)

    u0, hraw = _pool_mm("pool_mm", pooled, pool_full, pool_scale, x0)
    wts = [dict(w1=w1_0)]

    def finish_l0(after):
        w2_0, plew_0, gate_0 = gather_finish("l0", ag_l0, (after,))
        wts[0].update(w2=w2_0, plew=plew_0, gate=gate_0)

    x1, x1_b, saved0 = _tail_fwd("l0", u0, wts[0], ln_g_full[0], ln_b_full[0], p_l[0], lambda after: None, finish_l0)

    (in_g,) = gather_finish("ssm", ag_ssm, (x1,))
    in_t = in_g.reshape(dp, d)
    (zx,) = _mm_dx("in_proj", x1_b, in_t, "plain", [F32], k_rows=zw)
    dt_raw = _pad_lanes(_in_proj_dt("in_proj_dt", x1_b, in_t, zw, h_n))
    xbc = _conv_fwd("conv_fwd", zx, conv_w_full, conv_b_full, di)
    dt, acs, e_in, e_out = _dt_fwd("dt_fwd", dt_raw, bias128, alog128, 0)

    dt_x, ein_x, eout_x = _expand_heads("expand_heads", [dt, e_in, e_out], h_n, p_dim)
    d_x = jnp.repeat(ssm_d, p_dim, axis=1)

    def to_col(a):
        return a[:, :h_n].reshape(t, g_n, hpg).transpose(1, 0, 2)

    def to_row(a):
        return a[:, :h_n].reshape(t, g_n, hpg).transpose(1, 2, 0)

    def from_col(a):
        return _pad_lanes(a.transpose(1, 0, 2).reshape(t, h_n))

    a_col, a_row = to_col(acs), to_row(acs)
    y_ssd, prev = _ssd_fwd("ssd_fwd", xbc, dt_x, ein_x, eout_x, a_col, a_row, d_x, di, g_n, hpg, p_dim)
    out_passed = gather_pass("out", ag_out, (y_ssd,))
    yn = _gated_rms_fwd("gated_rms_fwd", y_ssd, zx, norm_w_full, after=(out_passed[0][3],))
    (out_g,) = gather_done("out", out_passed, (yn,))
    (u2,) = _mm_fwd("out_proj", yn, out_g, "row", [F32], _residual_epilogue, (x1,))
    mlp_passed = gather_pass("mlp1", ag_mlp, (u2,))
    wts.append({})

    def finish_l1(after):
        wts[1].update(zip(("w1", "w2", "plew", "gate"), gather_done("mlp1", mlp_passed, (after,))))

    x2, _, saved1 = _tail_fwd("l1", u2, wts[1], ln_g_full[1], ln_b_full[1], p_l[1], finish_l1, lambda after: None,
                              after=(mlp_passed[0][3],))

    core = lax.axis_index("c").astype(jnp.int32).reshape(1)
    chip = (2 * lax.axis_index("x") + lax.axis_index("y")).astype(jnp.int32).reshape(1)
    scattering = {}
    pending = []

    def to_chips(after):
        if not pending:
            return []
        tag, names, handle = pending.pop()
        own, halves = _async_wait(f"rs_{tag}_sibling_wait", _rs_sibling_copies, handle, len(names), after)
        sums = [_pair_sum(f"rs_{tag}_pair_sum_{n}", g, hv, core) for n, g, hv in zip(names, own, halves)]
        lands = [lax.empty((3,) + s.shape[1:], s.dtype) for s in sums]
        handle = _async_start(f"rs_{tag}_start", _rs_chip_copies, (3 * len(sums),), sums, lands)
        scattering[tag] = (names, handle)
        return [handle[3]]

    def emit(tag, grads):
        names, arrays = list(grads), list(grads.values())
        toks = to_chips(tuple(arrays))
        lands = [lax.empty((4,) + g.shape[1:], g.dtype) for g in arrays]
        handle = _async_start(f"rs_{tag}_sibling_start", _rs_sibling_copies, (4 * len(arrays),), arrays, lands,
                              tuple(toks))
        pending.append((tag, names, handle))
        return toks + [handle[3]]

    def collect(tag, after):
        names, handle = scattering.pop(tag)
        sums, thirds = _async_wait(f"rs_{tag}_wait", _rs_chip_copies, handle, len(names), after)
        return {n: (s.reshape(4, -1, s.shape[-1]), r.reshape(3, -1, r.shape[-1])) for n, s, r in zip(names, sums, thirds)}

    dx2, loss_cols = _loss_bwd("loss", x2, target)
    du2, du2_b, dg_1, db_1, toks = _tail_bwd("l1", dx2, saved1, wts[1], ln_g_full[1], p_l[1], emit, to_chips, [])
    toks = emit("ssm_out", dict(out=_mm_dw("out_proj_dw", yn, du2_b, "row", after=toks)))
    (dyn,) = _mm_dx("out_proj_dx", du2_b, out_g, "row", [F32], after=toks)
    toks = to_chips((dyn,))
    dy_ssd, dz, d_norm_w = _gated_rms_bwd("gated_rms_bwd", y_ssd, zx, norm_w_full, dyn)
    dxs, dbm, dcm, ddt_x, dacs, dd = _ssd_bwd("ssd_bwd", xbc, dt_x, ein_x, eout_x, a_col, a_row, d_x, prev, dy_ssd,
                                              di, g_n, hpg, p_dim)
    draw, d_bias, d_alog = _dt_bwd("dt_bwd", dt_raw, bias128, alog128, from_col(dacs), from_col(ddt_x), 0)
    conv_parts = [_conv_bwd(f"conv_bwd_{tag}", zx, conv_w_full, conv_b_full, dact, di, first)
                  for tag, dact, first in (("xs", dxs, 0), ("b", dbm, di), ("c", dcm, di + g_n * D_STATE))]
    d_conv_w = jnp.concatenate([c[1] for c in conv_parts], axis=1)
    d_conv_b = jnp.concatenate([c[2] for c in conv_parts], axis=1)
    dzx = jnp.concatenate([dz] + [c[0] for c in conv_parts], axis=1)
    d_dt = draw[:, :h_n]
    g_in_t = _in_proj_dw_t("in_proj_dw", dzx, d_dt, x1_b, after=toks)
    toks = emit("ssm_in", {"in": g_in_t.reshape(N_DEV, dp_s, d)})
    dx_dt = _in_proj_dt_dx("in_proj_dt_dx", d_dt, in_t, zw, du2)
    (dx1,) = _mm_fwd("in_proj_dx", dzx, in_t, "plain", [F32], _plain_add_epilogue, (dx_dt,), after=toks, k_rows=zw)
    toks = to_chips((dx1,))

    du0, _, dg_0, db_0, toks = _tail_bwd("l0", dx1, saved0, wts[0], ln_g_full[0], p_l[0], emit, to_chips, toks)
    dh, dpool, d_scale = _pool_bwd_mm("pool_bwd_mm", du0, hraw, pool_full, pool_scale)
    toks += emit("pool", dict(pool=_to_slots(_pool_dw("pool_dw", pooled, dh), 1)))
    grad_x = _pool_windows("pool_bwd", dpool, True, du0, after=tuple(toks))
    toks = to_chips((grad_x,))

    def update(tag, w, m, v, parts, layer, prev=None):
        own, recv = parts
        return _adamw(f"adamw_{tag}_{layer}", _as3d(w), _as3d(m), _as3d(v),
                      [(own, None), (recv, 0), (recv, 1), (recv, 2)], layer, prev, chip)

    q = collect("l1_ple", (grad_x, *toks))
    r_gate = update("ple_gate_w", ple_gate_w, m_ple_gate_w, v_ple_gate_w, q["gate"], 1)
    r_plew = update("ple_w", ple_w, m_ple_w, v_ple_w, q["plew"], 1)
    r_w2 = update("mlp_w2", mlp_w2, m_mlp_w2, v_mlp_w2, collect("l1_w2", (r_gate[0],))["w2"], 1)
    r_w1 = update("mlp_w1", mlp_w1, m_mlp_w1, v_mlp_w1, collect("l1_w1", (r_w2[0],))["w1"], 1)
    r_out = update("ssm_out_w", ssm_out_w, m_ssm_out_w, v_ssm_out_w, collect("ssm_out", (r_w1[0],))["out"], 0)
    in_wt, in_mt, in_vt = [jnp.swapaxes(a, 1, 2) for a in (ssm_in_w, m_ssm_in_w, v_ssm_in_w)]
    r_in = update("ssm_in_w", in_wt, in_mt, in_vt, collect("ssm_in", (r_out[0],))["in"], 0)

    d_ln_g = jnp.stack([jnp.stack(dg_0), jnp.stack(dg_1)]).reshape(2, 2, d)
    d_ln_b = jnp.stack([jnp.stack(db_0), jnp.stack(db_1)]).reshape(2, 2, d)
    partial_shapes = [(CONV_WIDTH, cd), (1, cd), (1, di), (2, 2, d), (2, 2, d), (1, d), (1, h_n), (1, h_n), (1, h_n),
                      (1, d)]
    partial = _pack([d_conv_w, d_conv_b, d_norm_w, d_ln_g, d_ln_b, d_scale, d_bias[:, :h_n], d_alog[:, :h_n],
                     dd.reshape(1, h_n), loss_cols], 8 * V7X_LANES)
    (all_partials,) = _all_gather("ag_small_grads", [partial], after=(r_in[0],))
    tot = _unpack(_sum8("sum_small_grads", all_partials), partial_shapes)
    t_conv_w, t_conv_b, t_norm_w, t_ln_g, t_ln_b, t_scale, t_bias, t_alog, t_dd, t_loss = tot
    loss = jnp.sum(t_loss)

    def mine(a, per):
        return lax.dynamic_slice_in_dim(a, me * per, per, axis=a.ndim - 1)

    small_names = ["ssm_conv_w", "ssm_conv_b", "ssm_norm_w", "ln_g", "ln_b", "pool_scale", "ssm_dt_bias", "ssm_a_log",
                   "ssm_d"]
    small_w = [ssm_conv_w, ssm_conv_b, ssm_norm_w, ln_g, ln_b, pool_scale, ssm_dt_bias, ssm_a_log, ssm_d]
    small_m = [m_ssm_conv_w, m_ssm_conv_b, m_ssm_norm_w, m_ln_g, m_ln_b, m_pool_scale, m_ssm_dt_bias, m_ssm_a_log,
               m_ssm_d]
    small_v = [v_ssm_conv_w, v_ssm_conv_b, v_ssm_norm_w, v_ln_g, v_ln_b, v_pool_scale, v_ssm_dt_bias, v_ssm_a_log,
               v_ssm_d]
    small_grads = [mine(t_conv_w, cd_s), mine(t_conv_b, cd_s), mine(t_norm_w, di_s), mine(t_ln_g, d_s),
                   mine(t_ln_b, d_s), t_scale, t_bias, t_alog, t_dd]
    shapes = [w.shape for w in small_w]
    pk = [_pack(group, 8 * V7X_LANES)[None] for group in (small_w, small_m, small_v, small_grads)]
    res = _adamw("adamw_small", pk[0], pk[1], pk[2], [(pk[3], 0)], 0)
    upd = {}
    for name, vals in zip(small_names, zip(*[_unpack(r, shapes) for r in res])):
        upd[name] = list(vals)

    q = collect("l0_ple", (res[0],))
    r_gate = update("ple_gate_w", ple_gate_w, m_ple_gate_w, v_ple_gate_w, q["gate"], 0, r_gate)
    r_plew = update("ple_w", ple_w, m_ple_w, v_ple_w, q["plew"], 0, r_plew)
    r_w2 = update("mlp_w2", mlp_w2, m_mlp_w2, v_mlp_w2, collect("l0_w2", (r_gate[0],))["w2"], 0, r_w2)
    r_w1 = update("mlp_w1", mlp_w1, m_mlp_w1, v_mlp_w1, collect("l0_w1", (r_w2[0],))["w1"], 0, r_w1)
    r_pool = update("pool_w", pool_w, m_pool_w, v_pool_w, collect("pool", (r_w1[0],))["pool"], 0)
    assert not scattering
    large = {"pool_w": (pool_w, r_pool), "ssm_in_w": (in_wt, r_in), "ssm_out_w": (ssm_out_w, r_out),
             "mlp_w1": (mlp_w1, r_w1), "mlp_w2": (mlp_w2, r_w2), "ple_w": (ple_w, r_plew),
             "ple_gate_w": (ple_gate_w, r_gate)}
    for name, (w, rs) in large.items():
        upd[name] = [r.reshape(w.shape) for r in rs]
    upd["ssm_in_w"] = [jnp.swapaxes(r, 1, 2) for r in upd["ssm_in_w"]]

    order = ["pool_w", "pool_scale", "ssm_in_w", "ssm_conv_w", "ssm_conv_b", "ssm_dt_bias", "ssm_a_log", "ssm_d",
             "ssm_norm_w", "ssm_out_w", "mlp_w1", "mlp_w2", "ln_g", "ln_b", "ple_w", "ple_gate_w"]
    out = [loss, grad_x[None]]
    for k in range(4):
        out += [upd[name][k] for name in order]
    return tuple(out)
```

```python
import jax
import jax.numpy as jnp
from jax import lax
from jax.experimental import pallas as pl
from jax.experimental.pallas import tpu as pltpu

F32 = jnp.float32
BF16 = jnp.bfloat16
SDS = jax.ShapeDtypeStruct
MESH = pl.DeviceIdType.MESH
ANY = pl.BlockSpec(memory_space=pl.ANY)

N_DEV = 8
DEPTH = 2
ALPHA = (2.0 * DEPTH) ** 0.25
LN_EPS = 1e-5
RMS_EPS = 1e-5
POOL_WINDOW_LOG2 = (1, 2, 3, 4)
D_STATE = 128
CHUNK = 128
SSD_STEP_CHUNKS = 8
CONV_WIDTH = 4
ADAM_LR = 0.001
ADAM_B1 = 0.9
ADAM_B2 = 0.999
ADAM_EPS = 1e-08
ADAM_WD = 0.01
ADAM_STEP = 10

V7X_LANES = 128
V7X_VMEM_LIMIT = 48 * 1024 * 1024


def _cp(*sem):
    return pltpu.CompilerParams(dimension_semantics=sem, vmem_limit_bytes=V7X_VMEM_LIMIT)


def _pick(dim, cap):
    if dim <= cap:
        return dim
    best = None
    for t in range(V7X_LANES, cap + 1, V7X_LANES):
        if dim % t == 0:
            best = t
    assert best is not None, (dim, cap)
    return best


def _row_tile(rows, cols, itemsize=4, target=1 << 20):
    t = rows
    while t % 2 == 0 and t // 2 >= 16 and (t // 2) % 16 == 0 and t * cols * itemsize > target:
        t //= 2
    return t


def _all_gather(name, shards, after=()):
    n, na = len(shards), len(after)

    def body(*refs):
        ins, outs = refs[:n], refs[n + na:2 * n + na]
        send_sems, recv_sems, local_sems = refs[2 * n + na:]
        x, y, c = lax.axis_index("x"), lax.axis_index("y"), lax.axis_index("c")
        me, sibling = (x, y, c), (x, y, 1 - c)
        chips = [(1 - x, y), (x, 1 - y), (1 - x, 1 - y)]

        def copy(a, k, block, to, src=None):
            dst = outs[a].at[4 * block[0] + 2 * block[1] + block[2]]
            return pltpu.make_async_remote_copy(
                src_ref=dst if src is None else src, dst_ref=dst, send_sem=send_sems.at[a, k],
                recv_sem=recv_sems.at[a, k], device_id=to, device_id_type=MESH)

        mine = [pltpu.make_async_copy(ins[a], outs[a].at[4 * x + 2 * y + c], local_sems.at[a]) for a in range(n)]
        for cp in mine:
            cp.start()
        first = []
        for a in range(n):
            first.append(copy(a, 0, me, sibling, src=ins[a]))
            first += [copy(a, 1 + j, me, (*chip, c), src=ins[a]) for j, chip in enumerate(chips)]
        for cp in first:
            cp.start()
        passed = []
        for j, chip in enumerate(chips):
            for a in range(n):
                copy(a, 1 + j, (*chip, c), me).wait_recv()
                fwd = copy(a, 4 + j, (*chip, c), sibling)
                fwd.start()
                passed.append(fwd)
        for a in range(n):
            copy(a, 0, sibling, me).wait_recv()
            for j, chip in enumerate(chips):
                copy(a, 4 + j, (*chip, 1 - c), me).wait_recv()
        for cp in first + passed:
            cp.wait_send()
        for cp in mine:
            cp.wait()

    return pl.pallas_call(
        body, name=name,
        out_shape=[SDS((N_DEV,) + s.shape, s.dtype) for s in shards],
        in_specs=[ANY] * (n + na), out_specs=[ANY] * n,
        scratch_shapes=[pltpu.SemaphoreType.DMA((n, 7)), pltpu.SemaphoreType.DMA((n, 7)),
                        pltpu.SemaphoreType.DMA((n,))],
    )(*shards, *after)


HBM_SPEC = pl.BlockSpec(memory_space=pltpu.HBM)
SEM_SPEC = pl.BlockSpec(memory_space=pltpu.SEMAPHORE)
EFFECT = pltpu.SideEffectType.DATAFLOW_SIDE_EFFECTING


def _ag_first_copies(ins, lands, send_sems, recv_sems):
    x, y, c = lax.axis_index("x"), lax.axis_index("y"), lax.axis_index("c")
    targets = [(x, y, 1 - c), (1 - x, y, c), (x, 1 - y, c), (1 - x, 1 - y, c)]
    return [pltpu.make_async_remote_copy(
        src_ref=ins[a], dst_ref=lands[a].at[4 * x + 2 * y + c], send_sem=send_sems.at[4 * a + k],
        recv_sem=recv_sems.at[4 * a + k], device_id=to, device_id_type=MESH)
        for a in range(len(ins)) for k, to in enumerate(targets)]


def _ag_forward_copies(ins, lands, send_sems, recv_sems):
    x, y, c = lax.axis_index("x"), lax.axis_index("y"), lax.axis_index("c")
    cps = []
    for a in range(len(lands)):
        for j, (px, py) in enumerate([(1 - x, y), (x, 1 - y), (1 - x, 1 - y)]):
            blk = lands[a].at[4 * px + 2 * py + c]
            cps.append(pltpu.make_async_remote_copy(
                src_ref=blk, dst_ref=blk, send_sem=send_sems.at[3 * a + j], recv_sem=recv_sems.at[3 * a + j],
                device_id=(x, y, 1 - c), device_id_type=MESH))
    return cps


def _rs_sibling_copies(ins, lands, send_sems, recv_sems):
    x, y, c = lax.axis_index("x"), lax.axis_index("y"), lax.axis_index("c")
    return [pltpu.make_async_remote_copy(
        src_ref=ins[a].at[2 * q + 1 - c], dst_ref=lands[a].at[q], send_sem=send_sems.at[4 * a + q],
        recv_sem=recv_sems.at[4 * a + q], device_id=(x, y, 1 - c), device_id_type=MESH)
        for a in range(len(ins)) for q in range(4)]


def _rs_chip_copies(ins, lands, send_sems, recv_sems):
    x, y, c = lax.axis_index("x"), lax.axis_index("y"), lax.axis_index("c")
    cps = []
    for a in range(len(ins)):
        for j, (px, py) in enumerate([(1 - x, y), (x, 1 - y), (1 - x, 1 - y)]):
            cps.append(pltpu.make_async_remote_copy(
                src_ref=ins[a].at[2 * px + py], dst_ref=lands[a].at[j], send_sem=send_sems.at[3 * a + j],
                recv_sem=recv_sems.at[3 * a + j], device_id=(px, py, c), device_id_type=MESH))
    return cps


def _async_start(name, build, sem_shape, ins, lands, after=()):
    arrays = [*ins, *lands]
    n_i, n_t, n_a = len(ins), len(arrays), len(after)

    def body(*refs):
        outs = refs[n_t + n_a:]
        for cp in build(refs[:n_i], refs[n_i:n_t], outs[0], outs[1]):
            cp.start()
        outs[-1][...] = jnp.zeros_like(outs[-1])

    res = pl.pallas_call(
        body, name=name,
        out_shape=(pltpu.SemaphoreType.DMA(sem_shape), pltpu.SemaphoreType.DMA(sem_shape),
                   *[pltpu.HBM(a.shape, a.dtype) for a in arrays], SDS((8, V7X_LANES), F32)),
        in_specs=[HBM_SPEC] * n_t + [ANY] * n_a,
        out_specs=(SEM_SPEC, SEM_SPEC, *[HBM_SPEC] * n_t, pl.BlockSpec(memory_space=pltpu.VMEM)),
        input_output_aliases={i: 2 + i for i in range(n_t)},
        compiler_params=pltpu.CompilerParams(has_side_effects=EFFECT),
    )(*[pltpu.with_memory_space_constraint(a, pltpu.HBM) for a in arrays], *after)
    return res[0], res[1], list(res[2:2 + n_t]), res[-1]


def _async_wait(name, build, handle, n_i, after=()):
    send_sems, recv_sems, arrays, _ = handle
    n_t, n_a = len(arrays), len(after)

    def body(*refs):
        for cp in build(refs[:n_i], refs[n_i:n_t], refs[n_t], refs[n_t + 1]):
            cp.wait_send()
            cp.wait_recv()

    res = pl.pallas_call(
        body, name=name, out_shape=tuple(pltpu.HBM(a.shape, a.dtype) for a in arrays),
        in_specs=[HBM_SPEC] * n_t + [SEM_SPEC, SEM_SPEC] + [ANY] * n_a, out_specs=tuple([HBM_SPEC] * n_t),
        input_output_aliases={i: i for i in range(n_t)},
        compiler_params=pltpu.CompilerParams(has_side_effects=EFFECT),
    )(*arrays, send_sems, recv_sems, *after)
    return list(res[:n_i]), list(res[n_i:])


def _mm_core(name, a, b, *, grid, a_spec, b_spec, dims, acc_shape, outs, out_spec, epilogue=None, extras=(),
             extra_specs=(), a_fn=None, after=(), carry=None):
    nk = grid[2]
    if carry is not None:
        after = (*after, carry)
    ne, no, na = len(extras), len(outs), len(after)

    def body(a_ref, b_ref, *rest):
        e_refs, o_refs, acc = rest[:ne], rest[ne + na:ne + na + no], rest[ne + na + no]
        k = pl.program_id(2)

        def product():
            lhs = a_ref[...] if a_fn is None else a_fn(a_ref[...])
            return lax.dot_general(lhs.astype(BF16), b_ref[...].astype(BF16), dims, preferred_element_type=F32)

        @pl.when(k == 0)
        def _():
            acc[...] = product()

        @pl.when(k > 0)
        def _():
            acc[...] += product()

        @pl.when(k == nk - 1)
        def _():
            r = acc[...]
            vals = epilogue(r, *[e[...] for e in e_refs]) if epilogue is not None else (r,)
            for o, v in zip(o_refs, vals):
                o[...] = v.astype(o.dtype)

    extra_specs = [s.replace(pipeline_mode=pl.Buffered(1)) for s in extra_specs]
    res = pl.pallas_call(
        body, name=name, grid=grid, out_shape=list(outs),
        in_specs=[a_spec, b_spec, *extra_specs, *[ANY] * na], out_specs=[out_spec] * no,
        scratch_shapes=[pltpu.VMEM(acc_shape, F32)],
        input_output_aliases={} if carry is None else {1 + ne + na: 0},
        compiler_params=_cp("parallel", "parallel", "arbitrary"),
    )(a, b, *extras, *after)
    return res


NN = (((1,), (0,)), ((), ()))
NT = (((1,), (1,)), ((), ()))
TN = (((0,), (0,)), ((), ()))


def _w_dims(w, kind):
    if kind == "col":
        return w.shape[1], N_DEV * w.shape[2], w.shape[1], w.shape[2]
    if kind == "row":
        return N_DEV * w.shape[1], w.shape[2], w.shape[1], w.shape[2]
    return w.shape[0], w.shape[1], w.shape[0], w.shape[1]


MM_VMEM_BUDGET = 40 * 1024 * 1024


def _blocks(m, n_len, n_caps, k_len, k_caps, a, out_dtypes, extras):
    per_out = sum(jnp.dtype(dt).itemsize for dt in out_dtypes)
    per_extra = sum(e.dtype.itemsize for e in extras)
    best = None
    for tm in (_pick(m, 2048), _pick(m, 1024), _pick(m, 512)):
        for tn in [_pick(n_len, cap) for cap in n_caps]:
            for tk in [_pick(k_len, cap) for cap in k_caps]:
                used = tm * tn * (4 + 2 * per_out + per_extra) + 2 * (tm * tk * a.dtype.itemsize + tk * tn * 2)
                key = ((m // tm) * (n_len // tn) * (k_len // tk), -tm, -tn)
                if used <= MM_VMEM_BUDGET and (best is None or key < best[0]):
                    best = (key, tm, tn, tk)
    assert best is not None, (m, n_len, k_len)
    return best[1:]


def _mm_fwd(name, a, w, kind, out_dtypes, epilogue=None, extras=(), a_fn=None, after=(), k_rows=None):
    if kind == "row":
        w, kind = w.reshape(-1, w.shape[-1]), "plain"
    m = a.shape[0]
    kk, n, ks, ns = _w_dims(w, kind)
    if k_rows is None:
        assert kk == a.shape[1]
    else:
        assert kind == "plain" and k_rows <= min(kk, a.shape[1])
        kk = k_rows
    if kind == "col":
        tm, tn, tk = _blocks(m, ns, (1024,), kk, (1024, 512), a, out_dtypes, extras)
    else:
        tm, tn, tk = _blocks(m, n, (2048, 1152, 1024), kk, (1024, 512), a, out_dtypes, extras)
    if kind == "col":
        nb = ns // tn
        b_spec = pl.BlockSpec((None, tk, tn), lambda i, j, k: (j // nb, k, j % nb))
    else:
        b_spec = pl.BlockSpec((tk, tn), lambda i, j, k: (k, j))
    mn_spec = pl.BlockSpec((tm, tn), lambda i, j, k: (i, j))
    return _mm_core(
        name, a, w, grid=(m // tm, n // tn, kk // tk),
        a_spec=pl.BlockSpec((tm, tk), lambda i, j, k: (i, k)), b_spec=b_spec, dims=NN, acc_shape=(tm, tn),
        outs=[SDS((m, n), dt) for dt in out_dtypes], out_spec=mn_spec, epilogue=epilogue, extras=extras,
        extra_specs=[mn_spec] * len(extras), a_fn=a_fn, after=after)


def _mm_dx(name, dy, w, kind, out_dtypes, epilogue=None, extras=(), after=(), k_rows=None):
    if kind == "row":
        w, kind = w.reshape(-1, w.shape[-1]), "plain"
    m, n_dim = dy.shape
    kk, n, ks, ns = _w_dims(w, kind)
    assert n == n_dim
    if k_rows is not None:
        assert kind == "plain" and k_rows <= kk
        kk = k_rows
    if kind == "col":
        tm, tn, tk = _blocks(m, kk, (2048, 1024), ns, (1024, 512), dy, out_dtypes, extras)
        kb = ns // tk
        b_spec = pl.BlockSpec((None, tn, tk), lambda i, j, k: (k // kb, j, k % kb))
    else:
        tm, tn, tk = _blocks(m, kk, (2048, 1024), n, (1152, 512), dy, out_dtypes, extras)
        b_spec = pl.BlockSpec((tn, tk), lambda i, j, k: (j, k))
    mk_spec = pl.BlockSpec((tm, tn), lambda i, j, k: (i, j))
    return _mm_core(
        name, dy, w, grid=(m // tm, kk // tn, n // tk),
        a_spec=pl.BlockSpec((tm, tk), lambda i, j, k: (i, k)), b_spec=b_spec, dims=NT, acc_shape=(tm, tn),
        outs=[SDS((m, kk), dt) for dt in out_dtypes], out_spec=mk_spec, epilogue=epilogue, extras=extras,
        extra_specs=[mk_spec] * len(extras), after=after)


def _in_proj_dt(name, a, w_t, first, h_n):
    m, kk = a.shape
    tm, tk = _pick(m, 1024), _pick(kk, 1024)
    blk = first // h_n
    return _mm_core(
        name, a, w_t, grid=(m // tm, 1, kk // tk), a_spec=pl.BlockSpec((tm, tk), lambda i, j, k: (i, k)),
        b_spec=pl.BlockSpec((h_n, tk), lambda i, j, k: (blk, k)), dims=NT, acc_shape=(tm, h_n),
        outs=[SDS((m, h_n), F32)], out_spec=pl.BlockSpec((tm, h_n), lambda i, j, k: (i, 0)))[0]


def _in_proj_dt_dx(name, d_dt, w_t, first, res):
    m, h_n = d_dt.shape
    n = w_t.shape[1]
    tm, tn = _pick(m, 1024), _pick(n, 1024)
    blk = first // h_n
    mn_spec = pl.BlockSpec((tm, tn), lambda i, j, k: (i, j))
    return _mm_core(
        name, d_dt, w_t, grid=(m // tm, n // tn, 1), a_spec=pl.BlockSpec((tm, h_n), lambda i, j, k: (i, 0)),
        b_spec=pl.BlockSpec((h_n, tn), lambda i, j, k: (blk, j)), dims=NN, acc_shape=(tm, tn),
        outs=[SDS((m, n), F32)], out_spec=mn_spec, epilogue=_residual_epilogue, extras=(res,),
        extra_specs=[mn_spec])[0]


def _in_proj_dw_t(name, dz, d_dt, x, after=()):
    m, n = x.shape
    zw, h_n = dz.shape[1], d_dt.shape[1]
    tm, tn, tk = _pick(zw, 1024), _pick(n, 2048), _pick(m, 1024)
    out = SDS((zw + h_n, n), BF16)
    x_spec = pl.BlockSpec((tk, tn), lambda i, j, k: (k, j))
    main = _mm_core(
        name, dz, x, grid=(zw // tm, n // tn, m // tk), a_spec=pl.BlockSpec((tk, tm), lambda i, j, k: (k, i)),
        b_spec=x_spec, dims=TN, acc_shape=(tm, tn), outs=[out], out_spec=pl.BlockSpec((tm, tn), lambda i, j, k: (i, j)),
        after=after)[0]
    blk = zw // h_n
    return _mm_core(
        name + "_dt", d_dt, x, grid=(1, n // tn, m // tk), a_spec=pl.BlockSpec((tk, h_n), lambda i, j, k: (k, 0)),
        b_spec=x_spec, dims=TN, acc_shape=(h_n, tn), outs=[out],
        out_spec=pl.BlockSpec((h_n, tn), lambda i, j, k: (blk, j)), carry=main)[0]


def _mm_dw(name, a, dy, kind, a_fn=None, after=()):
    m, kk = a.shape
    n = dy.shape[1]
    tk = _pick(m, 1024)
    if kind == "col":
        ns = n // N_DEV
        tm, tn = _pick(kk, 2048), _pick(ns, 1024)
        nb = ns // tn
        out = SDS((N_DEV, kk, ns), BF16)
        out_spec = pl.BlockSpec((None, tm, tn), lambda i, j, k: (j // nb, i, j % nb))
    else:
        tm, tn = _pick(kk, 1024), _pick(n, 2048)
        out = SDS((kk, n), BF16)
        out_spec = pl.BlockSpec((tm, tn), lambda i, j, k: (i, j))
    res = _mm_core(
        name, a, dy, grid=(kk // tm, n // tn, m // tk),
        a_spec=pl.BlockSpec((tk, tm), lambda i, j, k: (k, i)), b_spec=pl.BlockSpec((tk, tn), lambda i, j, k: (k, j)),
        dims=TN, acc_shape=(tm, tn), outs=[out], out_spec=out_spec, a_fn=a_fn, after=after)[0]
    return res.reshape(N_DEV, kk // N_DEV, n) if kind == "row" else res


def _rowwise(name, fn, ins, outs, rows, tile):
    arrays, specs = [], []
    for arr, kind in ins:
        arrays.append(arr)
        if kind == "row":
            specs.append(pl.BlockSpec((tile, arr.shape[1]), lambda i: (i, 0)))
        elif kind == "vec":
            specs.append(pl.BlockSpec(arr.shape, lambda i, nd=arr.ndim: (0,) * nd))
        else:
            specs.append(kind)
    out_shapes, out_specs, kinds = [], [], []
    for cols, dt, kind in outs:
        kinds.append(kind)
        if kind == "row":
            out_shapes.append(SDS((rows, cols), dt))
            out_specs.append(pl.BlockSpec((tile, cols), lambda i: (i, 0)))
        else:
            out_shapes.append(SDS((1, cols), F32))
            out_specs.append(pl.BlockSpec((1, cols), lambda i: (0, 0)))
    ni = len(arrays)
    has_acc = "acc" in kinds

    def body(*refs):
        vals = fn(*[r[...] for r in refs[:ni]])
        i = pl.program_id(0)
        for o, v, kind in zip(refs[ni:], vals, kinds):
            if kind == "row":
                o[...] = v.astype(o.dtype)
            else:
                @pl.when(i == 0)
                def _(o=o):
                    o[...] = jnp.zeros_like(o)

                o[...] += v

    return pl.pallas_call(
        body, name=name, grid=(rows // tile,), out_shape=out_shapes, in_specs=specs, out_specs=out_specs,
        compiler_params=_cp("arbitrary" if has_acc else "parallel"),
    )(*arrays)


def _ln_fwd(name, u, g, b, after=()):
    d = u.shape[1]

    def fn(u, g, b, *unused):
        mu = jnp.mean(u, axis=1, keepdims=True)
        xc = u - mu
        var = jnp.mean(xc * xc, axis=1, keepdims=True)
        y = xc * lax.rsqrt(var + LN_EPS) * g + b
        return y, y

    ins = [(u, "row"), (g, "vec"), (b, "vec")] + [(t, "vec") for t in after]
    return _rowwise(name, fn, ins, [(d, F32, "row"), (d, BF16, "row")], u.shape[0], 256)


def _ln_bwd(name, u, dy, g):
    d = u.shape[1]

    def fn(u, dy, g):
        mu = jnp.mean(u, axis=1, keepdims=True)
        xc = u - mu
        var = jnp.mean(xc * xc, axis=1, keepdims=True)
        rstd = lax.rsqrt(var + LN_EPS)
        xhat = xc * rstd
        dxhat = dy * g
        m1 = jnp.mean(dxhat, axis=1, keepdims=True)
        m2 = jnp.mean(dxhat * xhat, axis=1, keepdims=True)
        du = rstd * (dxhat - m1 - xhat * m2)
        return du, du, jnp.sum(dy * xhat, axis=0, keepdims=True), jnp.sum(dy, axis=0, keepdims=True)

    return _rowwise(name, fn, [(u, "row"), (dy, "row"), (g, "vec")],
                    [(d, F32, "row"), (d, BF16, "row"), (d, F32, "acc"), (d, F32, "acc")], u.shape[0], 256)


def _loss_bwd(name, y, target):
    d = y.shape[1]

    def fn(y, t):
        e = y - t
        return e * (1.0 / d), jnp.sum(e * e, axis=0, keepdims=True) * (0.5 / d)

    return _rowwise(name, fn, [(y, "row"), (target, "row")], [(d, F32, "row"), (d, F32, "acc")], y.shape[0], 256)


def _ple_bwd(name, dx, e, gate):
    d = dx.shape[1]

    def fn(dx, e, gate):
        return dx * e * gate * (1.0 - gate), dx * gate

    return _rowwise(name, fn, [(dx, "row"), (e, "row"), (gate, "row")], [(d, BF16, "row"), (d, BF16, "row")],
                    dx.shape[0], 256)


def _sigmoid(v):
    return 1.0 / (1.0 + jnp.exp(-v))


def _gated_rms_fwd(name, y, zx, norm_w, after=()):
    di = y.shape[1]

    def fn(y, z, w, *unused):
        yg = y * (z * _sigmoid(z))
        r = lax.rsqrt(jnp.mean(yg * yg, axis=1, keepdims=True) + RMS_EPS)
        return (yg * r * w,)

    z_spec = pl.BlockSpec((128, di), lambda i: (i, 0))
    ins = [(y, "row"), (zx, z_spec), (norm_w, "vec")] + [(t, "vec") for t in after]
    return _rowwise(name, fn, ins, [(di, BF16, "row")], y.shape[0], 128)[0]


def _gated_rms_bwd(name, y, zx, norm_w, dout):
    di = y.shape[1]

    def fn(y, z, w, dout):
        sg = _sigmoid(z)
        sz = z * sg
        yg = y * sz
        r = lax.rsqrt(jnp.mean(yg * yg, axis=1, keepdims=True) + RMS_EPS)
        dn = dout * w
        dyg = r * (dn - yg * (r * r) * jnp.mean(dn * yg, axis=1, keepdims=True))
        dy = dyg * sz
        dz = dyg * y * (sg * (1.0 + z * (1.0 - sg)))
        return dy, dz, jnp.sum(dout * yg * r, axis=0, keepdims=True)

    z_spec = pl.BlockSpec((128, di), lambda i: (i, 0))
    return _rowwise(name, fn, [(y, "row"), (zx, z_spec), (norm_w, "vec"), (dout, "row")],
                    [(di, F32, "row"), (di, BF16, "row"), (di, F32, "acc")], y.shape[0], 128)


def _shift_down(v, j, row):
    return jnp.where(row >= j, pltpu.roll(v, j, 0), 0.0)


def _shift_up(v, j, row):
    t = v.shape[0]
    return jnp.where(row < t - j, pltpu.roll(v, t - j, 0), 0.0)


def _pool_select(parts, g):
    return jnp.where(g == 0, parts[0], jnp.where(g == 1, parts[1], jnp.where(g == 2, parts[2], parts[3])))


def _pool_windows(name, x, transpose, scale_by=None, after=()):
    t, d = x.shape
    cg = d // 4
    cw = V7X_LANES
    per = cg // cw

    def body(*refs):
        x_ref, o_ref = refs[0], refs[-1]
        g = pl.program_id(0) // per
        xv = x_ref[...]
        row = lax.broadcasted_iota(jnp.int32, (t, 1), 0)
        cnt = jnp.minimum(row + 1, jnp.left_shift(2, g)).astype(F32)
        s = xv / cnt if transpose else xv
        parts = []
        for lg in POOL_WINDOW_LOG2:
            j = 1 << (lg - 1)
            s = s + (_shift_up(s, j, row) if transpose else _shift_down(s, j, row))
            parts.append(s)
        sel = _pool_select(parts, g)
        if transpose:
            o_ref[...] = ALPHA * refs[1][...] + sel - xv
        else:
            o_ref[...] = (sel / cnt - xv).astype(o_ref.dtype)

    col = pl.BlockSpec((t, cw), lambda j: (0, j))
    ins = [x] if scale_by is None else [x, scale_by]
    return pl.pallas_call(
        body, name=name, grid=(d // cw,), out_shape=SDS((t, d), F32 if transpose else BF16),
        in_specs=[col] * len(ins) + [ANY] * len(after), out_specs=col, compiler_params=_cp("parallel"),
    )(*ins, *after)


def _pool_mm(name, pooled, w, scale, x):
    t, d = x.shape
    cg = d // 4
    tm = _pick(t, 1024)

    def body(p_ref, w_ref, s_ref, x_ref, u_ref, h_ref):
        h = jnp.dot(p_ref[...], w_ref[...], preferred_element_type=F32)
        h_ref[...] = h
        u_ref[...] = ALPHA * x_ref[...] + h * s_ref[...]

    blk = pl.BlockSpec((tm, cg), lambda g, i: (i, g))
    return pl.pallas_call(
        body, name=name, grid=(4, t // tm), out_shape=[SDS((t, d), F32), SDS((t, d), F32)],
        in_specs=[blk, pl.BlockSpec((None, cg, cg), lambda g, i: (g, 0, 0)), pl.BlockSpec((1, cg), lambda g, i: (0, g)),
                  blk],
        out_specs=[blk, blk], compiler_params=_cp("parallel", "parallel"),
    )(pooled, w, scale, x)


def _pool_bwd_mm(name, du, hraw, w, scale):
    t, d = du.shape
    cg = d // 4
    tm = _pick(t, 1024)

    def body(du_ref, h_ref, w_ref, s_ref, dh_ref, dp_ref, ds_ref):
        @pl.when(pl.program_id(1) == 0)
        def _():
            ds_ref[...] = jnp.zeros_like(ds_ref)

        duv = du_ref[...]
        ds_ref[...] += jnp.sum(duv * h_ref[...], axis=0, keepdims=True)
        dh = (duv * s_ref[...]).astype(BF16)
        dh_ref[...] = dh
        dp_ref[...] = lax.dot_general(dh, w_ref[...], NT, preferred_element_type=F32)

    blk = pl.BlockSpec((tm, cg), lambda g, i: (i, g))
    vec = pl.BlockSpec((1, cg), lambda g, i: (0, g))
    return pl.pallas_call(
        body, name=name, grid=(4, t // tm), out_shape=[SDS((t, d), BF16), SDS((t, d), F32), SDS((1, d), F32)],
        in_specs=[blk, blk, pl.BlockSpec((None, cg, cg), lambda g, i: (g, 0, 0)), vec],
        out_specs=[blk, blk, vec], compiler_params=_cp("parallel", "arbitrary"),
    )(du, hraw, w, scale)


def _pool_dw(name, pooled, dh):
    t, d = pooled.shape
    cg = d // 4
    tk = _pick(t, 512)
    nk = t // tk

    def body(p_ref, dh_ref, o_ref, acc):
        k = pl.program_id(1)

        @pl.when(k == 0)
        def _():
            acc[...] = jnp.zeros_like(acc)

        acc[...] += lax.dot_general(p_ref[...], dh_ref[...], TN, preferred_element_type=F32)

        @pl.when(k == nk - 1)
        def _():
            o_ref[...] = acc[...].astype(o_ref.dtype)

    blk = pl.BlockSpec((tk, cg), lambda g, k: (k, g))
    return pl.pallas_call(
        body, name=name, grid=(4, nk), out_shape=SDS((4, cg, cg), BF16), in_specs=[blk, blk],
        out_specs=pl.BlockSpec((None, cg, cg), lambda g, k: (g, 0, 0)), scratch_shapes=[pltpu.VMEM((cg, cg), F32)],
        compiler_params=_cp("parallel", "arbitrary"),
    )(pooled, dh)


def _conv_pre(u, w_ref, b_ref, row):
    pre = b_ref[...] + _shift_down(u, 3, row) * w_ref[0:1, :]
    pre = pre + _shift_down(u, 2, row) * w_ref[1:2, :]
    pre = pre + _shift_down(u, 1, row) * w_ref[2:3, :]
    return pre + u * w_ref[3:4, :]


def _conv_fwd(name, zx, conv_w, conv_b, di):
    t = zx.shape[0]
    cd = conv_w.shape[1]
    cw = _pick(cd, 256)
    off = di // cw

    def body(u_ref, w_ref, b_ref, o_ref):
        row = lax.broadcasted_iota(jnp.int32, (t, 1), 0)
        pre = _conv_pre(u_ref[...], w_ref, b_ref, row)
        o_ref[...] = pre * _sigmoid(pre)

    return pl.pallas_call(
        body, name=name, grid=(cd // cw,), out_shape=SDS((t, cd), F32),
        in_specs=[pl.BlockSpec((t, cw), lambda j: (0, off + j)), pl.BlockSpec((CONV_WIDTH, cw), lambda j: (0, j)),
                  pl.BlockSpec((1, cw), lambda j: (0, j))],
        out_specs=pl.BlockSpec((t, cw), lambda j: (0, j)), compiler_params=_cp("parallel"),
    )(zx, conv_w, conv_b)


def _conv_bwd(name, zx, conv_w, conv_b, dact, di, first):
    t, cd = dact.shape
    cw = _pick(cd, 256)
    off, woff = (di + first) // cw, first // cw

    def body(u_ref, w_ref, b_ref, da_ref, du_ref, dw_ref, db_ref):
        row = lax.broadcasted_iota(jnp.int32, (t, 1), 0)
        u = u_ref[...]
        pre = _conv_pre(u, w_ref, b_ref, row)
        sg = _sigmoid(pre)
        dpre = da_ref[...] * (sg * (1.0 + pre * (1.0 - sg)))
        du = dpre * w_ref[3:4, :]
        for j in (1, 2, 3):
            du = du + _shift_up(dpre, j, row) * w_ref[3 - j:4 - j, :]
            dw_ref[3 - j:4 - j, :] = jnp.sum(dpre * _shift_down(u, j, row), axis=0, keepdims=True)
        dw_ref[3:4, :] = jnp.sum(dpre * u, axis=0, keepdims=True)
        db_ref[...] = jnp.sum(dpre, axis=0, keepdims=True)
        du_ref[...] = du.astype(du_ref.dtype)

    wspec = pl.BlockSpec((CONV_WIDTH, cw), lambda j: (0, j))
    bspec = pl.BlockSpec((1, cw), lambda j: (0, j))
    ospec = pl.BlockSpec((t, cw), lambda j: (0, j))
    return pl.pallas_call(
        body, name=name, grid=(cd // cw,), out_shape=[SDS((t, cd), BF16), SDS((CONV_WIDTH, cd), F32), SDS((1, cd), F32)],
        in_specs=[pl.BlockSpec((t, cw), lambda j: (0, off + j)), pl.BlockSpec((CONV_WIDTH, cw), lambda j: (0, woff + j)),
                  pl.BlockSpec((1, cw), lambda j: (0, woff + j)), ospec],
        out_specs=[ospec, wspec, bspec], compiler_params=_cp("parallel"),
    )(zx, conv_w, conv_b, dact)


def _expand_heads(name, arrays, h_n, p):
    t = arrays[0].shape[0]
    n = len(arrays)
    w = _pick(h_n * p, 512)

    def body(*refs):
        j = pl.program_id(0)
        head = lax.broadcasted_iota(jnp.int32, (V7X_LANES, w), 0)
        lane = lax.broadcasted_iota(jnp.int32, (V7X_LANES, w), 1)
        spread = (head == j * (w // p) + lane // p).astype(BF16)
        for a_ref, o_ref in zip(refs[:n], refs[n:]):
            rest = a_ref[...]
            out = jnp.zeros((t, w), F32)
            for _ in range(3):
                piece = rest.astype(BF16)
                out = out + jnp.dot(piece, spread, preferred_element_type=F32)
                rest = rest - piece.astype(F32)
            o_ref[...] = out

    full = pl.BlockSpec((t, V7X_LANES), lambda j: (0, 0))
    return pl.pallas_call(
        body, name=name, grid=(h_n * p // w,), out_shape=[SDS((t, h_n * p), F32)] * n, in_specs=[full] * n,
        out_specs=[pl.BlockSpec((t, w), lambda j: (0, j))] * n, compiler_params=_cp("parallel"),
    )(*arrays)


def _softplus(v):
    return jnp.maximum(v, 0.0) + jnp.log(1.0 + jnp.exp(-jnp.abs(v)))


def _dt_fwd(name, zx, bias, a_log, col_block):
    t = zx.shape[0]

    def body(r_ref, b_ref, al_ref, dt_ref, acs_ref, ein_ref, eout_ref):
        row = lax.broadcasted_iota(jnp.int32, (t, 1), 0) % CHUNK
        dt = _softplus(r_ref[...] + b_ref[...])
        da = dt * (-jnp.exp(al_ref[...]))
        s, r = da, da
        j = 1
        while j < CHUNK:
            s = s + jnp.where(row >= j, pltpu.roll(s, j, 0), 0.0)
            r = r + jnp.where(row < CHUNK - j, pltpu.roll(r, t - j, 0), 0.0)
            j *= 2
        dt_ref[...] = dt
        acs_ref[...] = s
        ein_ref[...] = jnp.exp(s)
        eout_ref[...] = jnp.exp(r - da)

    vec = pl.BlockSpec((1, V7X_LANES), lambda i: (0, 0))
    full = pl.BlockSpec((t, V7X_LANES), lambda i: (0, 0))
    return pl.pallas_call(
        body, name=name, grid=(1,), out_shape=[SDS((t, V7X_LANES), F32)] * 4,
        in_specs=[pl.BlockSpec((t, V7X_LANES), lambda i: (0, col_block)), vec, vec], out_specs=[full] * 4,
        compiler_params=_cp("arbitrary"),
    )(zx, bias, a_log)


def _dt_bwd(name, zx, bias, a_log, d_acs, d_dt, col_block):
    t = zx.shape[0]

    def body(r_ref, b_ref, al_ref, da_ref, dd_ref, draw_ref, db_ref, dal_ref):
        row = lax.broadcasted_iota(jnp.int32, (t, 1), 0) % CHUNK
        pre = r_ref[...] + b_ref[...]
        dt = _softplus(pre)
        a = -jnp.exp(al_ref[...])
        s = da_ref[...]
        j = 1
        while j < CHUNK:
            s = s + jnp.where(row < CHUNK - j, pltpu.roll(s, t - j, 0), 0.0)
            j *= 2
        ddt = dd_ref[...] + s * a
        dal_ref[...] = jnp.sum(s * dt, axis=0, keepdims=True) * a
        draw = ddt * _sigmoid(pre)
        db_ref[...] = jnp.sum(draw, axis=0, keepdims=True)
        draw_ref[...] = draw.astype(draw_ref.dtype)

    vec = pl.BlockSpec((1, V7X_LANES), lambda i: (0, 0))
    full = pl.BlockSpec((t, V7X_LANES), lambda i: (0, 0))
    return pl.pallas_call(
        body, name=name, grid=(1,), out_shape=[SDS((t, V7X_LANES), BF16), SDS((1, V7X_LANES), F32), SDS((1, V7X_LANES), F32)],
        in_specs=[pl.BlockSpec((t, V7X_LANES), lambda i: (0, col_block)), vec, vec, full, full],
        out_specs=[full, vec, vec], compiler_params=_cp("arbitrary"),
    )(zx, bias, a_log, d_acs, d_dt)


def _ssd_specs(t, di, g_n, hpg, p, rev):
    rows = SSD_STEP_CHUNKS * CHUNK
    nc = t // rows
    w = hpg * p
    nb = di // D_STATE

    def cc(c):
        return nc - 1 - c if rev else c

    return dict(
        xs=pl.BlockSpec((rows, w), lambda g, c: (cc(c), g)),
        bm=pl.BlockSpec((rows, D_STATE), lambda g, c: (cc(c), nb + g)),
        cm=pl.BlockSpec((rows, D_STATE), lambda g, c: (cc(c), nb + g_n + g)),
        col=pl.BlockSpec((None, rows, hpg), lambda g, c: (g, cc(c), 0)),
        rowv=pl.BlockSpec((None, hpg, rows), lambda g, c: (g, 0, cc(c))),
        head=pl.BlockSpec((None, 1, hpg), lambda g, c: (g, 0, 0)),
        lanes=pl.BlockSpec((1, w), lambda g, c: (0, g)),
        bc=pl.BlockSpec((rows, D_STATE), lambda g, c: (cc(c), g)),
        prev=pl.BlockSpec((SSD_STEP_CHUNKS, None, D_STATE, w), lambda g, c: (cc(c), g, 0, 0)),
        seg=pl.BlockSpec((w, V7X_LANES), lambda g, c: (0, 0)),
    )


def _decay_masks(cb, ac, ar, heads):
    li = lax.broadcasted_iota(jnp.int32, (CHUNK, CHUNK), 0)
    si = lax.broadcasted_iota(jnp.int32, (CHUNK, CHUNK), 1)
    lms = [jnp.exp(jnp.where(li >= si, ac[:, hh:hh + 1] - ar[hh:hh + 1, :], -jnp.inf)) for hh in heads]
    return lms, [(cb * lm).astype(BF16) for lm in lms]


def _ssd_fwd(name, xbc, dt_x, ein_x, eout_x, a_col, a_row, d_x, di, g_n, hpg, p):
    t = xbc.shape[0]
    nc = t // CHUNK
    w = hpg * p
    assert 2 * p == V7X_LANES and hpg % 2 == 0 and nc % SSD_STEP_CHUNKS == 0
    sp = _ssd_specs(t, di, g_n, hpg, p, False)

    def body(xs_ref, bm_ref, cm_ref, dt_ref, ein_ref, eout_ref, ac_ref, ar_ref, d_ref, y_ref, prev_ref, h_ref):
        @pl.when(pl.program_id(1) == 0)
        def _():
            h_ref[...] = jnp.zeros_like(h_ref)

        first = lax.broadcasted_iota(jnp.int32, (1, V7X_LANES), 1) < p
        for sub in range(SSD_STEP_CHUNKS):
            r = slice(sub * CHUNK, (sub + 1) * CHUNK)
            bm = bm_ref[r, :].astype(BF16)
            cm = cm_ref[r, :].astype(BF16)
            cb = lax.dot_general(cm, bm, NT, preferred_element_type=F32)
            xs = xs_ref[r, :]
            e_in = ein_ref[r, :]
            xdt = xs * dt_ref[r, :]
            ac, ar = ac_ref[r, :], ar_ref[:, r]
            ys = []
            for pr in range(hpg // 2):
                _, ms = _decay_masks(cb, ac, ar, (2 * pr, 2 * pr + 1))
                xp = xdt[:, pr * V7X_LANES:(pr + 1) * V7X_LANES]
                rhs = jnp.concatenate([jnp.where(first, xp, 0.0), jnp.where(first, 0.0, xp)], axis=0).astype(BF16)
                ys.append(jnp.dot(jnp.concatenate(ms, axis=1), rhs, preferred_element_type=F32))
            h_prev = h_ref[...]
            prev_ref[sub] = h_prev
            y = jnp.concatenate(ys, axis=1) + jnp.dot(cm, h_prev.astype(BF16), preferred_element_type=F32) * e_in
            y_ref[r, :] = y + xs * d_ref[...]
            st = lax.dot_general(bm, (xdt * eout_ref[r, :]).astype(BF16), TN, preferred_element_type=F32)
            h_ref[...] = e_in[CHUNK - 1:CHUNK, :] * h_prev + st

    return pl.pallas_call(
        body, name=name, grid=(g_n, nc // SSD_STEP_CHUNKS),
        out_shape=[SDS((t, di), F32), SDS((nc, g_n, D_STATE, w), F32)],
        in_specs=[sp["xs"], sp["bm"], sp["cm"], sp["xs"], sp["xs"], sp["xs"], sp["col"], sp["rowv"], sp["lanes"]],
        out_specs=[sp["xs"], sp["prev"]], scratch_shapes=[pltpu.VMEM((D_STATE, w), F32)],
        compiler_params=_cp("parallel", "arbitrary"),
    )(xbc, xbc, xbc, dt_x, ein_x, eout_x, a_col, a_row, d_x)


def _head_sums(v, seg):
    hi = v.astype(BF16)
    lo = (v - hi.astype(F32)).astype(BF16)
    return jnp.dot(hi, seg, preferred_element_type=F32) + jnp.dot(lo, seg, preferred_element_type=F32)


def _head_totals(v, seg):
    part = v[0:8]
    for r in range(8, v.shape[0], 8):
        part = part + v[r:r + 8]
    return jnp.sum(_head_sums(part, seg), axis=0, keepdims=True)


def _ssd_bwd(name, xbc, dt_x, ein_x, eout_x, a_col, a_row, d_x, prev, dy, di, g_n, hpg, p):
    t = xbc.shape[0]
    nc = t // CHUNK
    w = hpg * p
    sp = _ssd_specs(t, di, g_n, hpg, p, True)
    seg = (lax.broadcasted_iota(jnp.int32, (w, V7X_LANES), 0) // p
           == lax.broadcasted_iota(jnp.int32, (w, V7X_LANES), 1)).astype(BF16)

    def body(xs_ref, bm_ref, cm_ref, dt_ref, ein_ref, eout_ref, ac_ref, ar_ref, d_ref, prev_ref, dy_ref,
             seg_ref, dx_ref, dbm_ref, dcm_ref, ddt_ref, dacs_ref, dd_ref, dh_ref):
        @pl.when(pl.program_id(1) == 0)
        def _():
            dh_ref[...] = jnp.zeros_like(dh_ref)
            dd_ref[...] = jnp.zeros_like(dd_ref)

        first = lax.broadcasted_iota(jnp.int32, (1, V7X_LANES), 1) < p
        last_row = lax.broadcasted_iota(jnp.int32, (CHUNK, 1), 0) == CHUNK - 1
        seg_m = seg_ref[...]
        d_skip = d_ref[...]
        for sub in reversed(range(SSD_STEP_CHUNKS)):
            r = slice(sub * CHUNK, (sub + 1) * CHUNK)
            bm = bm_ref[r, :].astype(BF16)
            cm = cm_ref[r, :].astype(BF16)
            cb = lax.dot_general(cm, bm, NT, preferred_element_type=F32)
            xs, dy, e_in, e_out, dt_l = xs_ref[r, :], dy_ref[r, :], ein_ref[r, :], eout_ref[r, :], dt_ref[r, :]
            ac, ar = ac_ref[r, :], ar_ref[:, r]
            xdt = xs * dt_l
            h_prev = prev_ref[sub]
            h_prev_b = h_prev.astype(BF16)
            dh_next = dh_ref[...]
            dh_next_b = dh_next.astype(BF16)
            dy_e = (dy * e_in).astype(BF16)
            d_cm = lax.dot_general(dy_e, h_prev_b, NT, preferred_element_type=F32)
            dh_ref[...] = (e_in[CHUNK - 1:CHUNK, :] * dh_next
                           + lax.dot_general(cm, dy_e, TN, preferred_element_type=F32))
            q = jnp.dot(bm, dh_next_b, preferred_element_type=F32)
            xf = xdt * e_out
            d_bm = lax.dot_general(xf.astype(BF16), dh_next_b, NT, preferred_element_type=F32)
            d_cb = jnp.zeros((CHUNK, CHUNK), F32)
            parts, w_parts = [], []
            for pr in range(hpg // 2):
                lanes = slice(pr * V7X_LANES, (pr + 1) * V7X_LANES)
                lms, ms = _decay_masks(cb, ac, ar, (2 * pr, 2 * pr + 1))
                xp = xdt[:, lanes]
                xp_b = xp.astype(BF16)
                dyp = dy[:, lanes]
                halves = [jnp.where(first, dyp, 0.0).astype(BF16), jnp.where(first, 0.0, dyp).astype(BF16)]
                for lm, half in zip(lms, halves):
                    d_cb = d_cb + lax.dot_general(half, xp_b, NT, preferred_element_type=F32) * lm
                dxd = lax.dot_general(jnp.concatenate(ms, axis=0), jnp.concatenate(halves, axis=0), TN,
                                      preferred_element_type=F32)
                stacked = jnp.concatenate([jnp.where(first, xp, 0.0), jnp.where(first, 0.0, xp)], axis=0).astype(BF16)
                y_diag = jnp.dot(jnp.concatenate(ms, axis=1), stacked, preferred_element_type=F32)
                parts.append(dxd)
                w_parts.append(dyp.astype(BF16).astype(F32) * y_diag - xp_b.astype(F32) * dxd)
            d_xdt = jnp.concatenate(parts, axis=1) + q * e_out
            dx_ref[r, :] = d_xdt * dt_l + dy * d_skip
            ch = jnp.dot(cm, h_prev_b, preferred_element_type=F32)
            qx = q * xf
            s_a = _head_sums(dy * ch * e_in - qx + jnp.concatenate(w_parts, axis=1), seg_m)[:, :hpg]
            d_last = (_head_totals(qx, seg_m)[:, :hpg]
                      + jnp.exp(ac[CHUNK - 1:CHUNK, :]) * _head_totals(dh_next * h_prev, seg_m)[:, :hpg])
            ddt_ref[r, :] = _head_sums(d_xdt * xs, seg_m)[:, :hpg]
            dacs_ref[r, :] = s_a + jnp.where(last_row, d_last, 0.0)
            dd_ref[...] += _head_totals(dy * xs, seg_m)[:, :hpg]
            d_cb_b = d_cb.astype(BF16)
            dcm_ref[r, :] = d_cm + jnp.dot(d_cb_b, bm, preferred_element_type=F32)
            dbm_ref[r, :] = d_bm + lax.dot_general(d_cb_b, cm, TN, preferred_element_type=F32)

    gn = g_n * D_STATE
    return pl.pallas_call(
        body, name=name, grid=(g_n, nc // SSD_STEP_CHUNKS),
        out_shape=[SDS((t, di), F32), SDS((t, gn), F32), SDS((t, gn), F32), SDS((g_n, t, hpg), F32),
                   SDS((g_n, t, hpg), F32), SDS((g_n, 1, hpg), F32)],
        in_specs=[sp["xs"], sp["bm"], sp["cm"], sp["xs"], sp["xs"], sp["xs"], sp["col"], sp["rowv"],
                  sp["lanes"], sp["prev"], sp["xs"], sp["seg"]],
        out_specs=[sp["xs"], sp["bc"], sp["bc"], sp["col"], sp["col"], sp["head"]],
        scratch_shapes=[pltpu.VMEM((D_STATE, w), F32)],
        compiler_params=_cp("parallel", "arbitrary"),
    )(xbc, xbc, xbc, dt_x, ein_x, eout_x, a_col, a_row, d_x, prev, dy, seg)


def _as3d(a):
    return a.reshape(a.shape[0], -1, a.shape[-1])


def _pair_sum(name, own, recv, core):
    shape = recv.shape
    cols = shape[-1]
    own3, recv3 = own.reshape(8, -1, cols), recv.reshape(4, -1, cols)
    rows = recv3.shape[1]
    tr = _row_tile(rows, cols, 2)

    def body(c_ref, a_ref, b_ref, o_ref):
        o_ref[...] = (a_ref[...].astype(F32) + b_ref[...].astype(F32)).astype(o_ref.dtype)

    blk = pl.BlockSpec((None, tr, cols), lambda q, i, c_ref: (q, i, 0))
    out = pl.pallas_call(
        body, name=name, out_shape=SDS(recv3.shape, recv.dtype),
        grid_spec=pltpu.PrefetchScalarGridSpec(
            num_scalar_prefetch=1, grid=(4, rows // tr),
            in_specs=[pl.BlockSpec((None, tr, cols), lambda q, i, c_ref: (2 * q + c_ref[0], i, 0)), blk], out_specs=blk),
        compiler_params=_cp("parallel", "parallel"),
    )(core, own3, recv3)
    return out.reshape(shape)


def _adamw(name, w, m, v, parts, layer, prev=None, sel=None):
    lyr, rows, cols = w.shape
    n = len(parts)
    by_rows = rows % 16 == 0
    tr, tc = (_row_tile(rows, cols), cols) if by_rows else (rows, _pick(cols, 256))
    np_ = 0 if prev is None else 4
    if sel is None:
        sel = jnp.zeros((1,), jnp.int32)

    def body(sel_ref, *refs):
        w_ref, m_ref, v_ref = refs[:3]
        p_refs = refs[3:3 + n]
        g_ref, d_ref, nm_ref, nv_ref = refs[3 + n + np_:]
        g = p_refs[0][...].astype(F32)
        for r in p_refs[1:]:
            g = g + r[...].astype(F32)
        nm = ADAM_B1 * m_ref[...] + (1.0 - ADAM_B1) * g
        nv = ADAM_B2 * v_ref[...] + (1.0 - ADAM_B2) * (g * g)
        m_hat = nm / (1.0 - ADAM_B1 ** ADAM_STEP)
        v_hat = nv / (1.0 - ADAM_B2 ** ADAM_STEP)
        g_ref[...] = g
        d_ref[...] = -ADAM_LR * (m_hat / (jnp.sqrt(v_hat) + ADAM_EPS) + ADAM_WD * w_ref[...])
        nm_ref[...] = nm
        nv_ref[...] = nv

    def at(lead):
        return pl.BlockSpec((None, tr, tc), lambda i, s: (lead(s), i, 0) if by_rows else (lead(s), 0, i))

    lspec = at(lambda s: layer)
    pspecs = [at(lambda s: s[0]) if q is None else at(lambda s, q=q: q) for _, q in parts]
    aliases = {} if prev is None else {4 + n + q: q for q in range(4)}
    return pl.pallas_call(
        body, name=name, out_shape=[SDS(w.shape, F32)] * 4,
        grid_spec=pltpu.PrefetchScalarGridSpec(
            num_scalar_prefetch=1, grid=(rows // tr if by_rows else cols // tc,),
            in_specs=[lspec] * 3 + pspecs + [ANY] * np_, out_specs=[lspec] * 4),
        input_output_aliases=aliases, compiler_params=_cp("parallel"),
    )(sel, w, m, v, *[arr for arr, _ in parts], *(prev or ()))


def _sum8(name, parts):
    rows = parts.shape[1]

    def body(p_ref, o_ref):
        s = p_ref[0]
        for q in range(1, N_DEV):
            s = s + p_ref[q]
        o_ref[...] = s

    return pl.pallas_call(
        body, name=name, grid=(1,), out_shape=SDS((rows, V7X_LANES), F32),
        in_specs=[pl.BlockSpec((N_DEV, rows, V7X_LANES), lambda i: (0, 0, 0))],
        out_specs=pl.BlockSpec((rows, V7X_LANES), lambda i: (0, 0)), compiler_params=_cp("arbitrary"),
    )(parts)


def _pack(vectors, align):
    flat = jnp.concatenate([v.reshape(-1) for v in vectors])
    pad = (-flat.shape[0]) % align
    if pad:
        flat = jnp.concatenate([flat, jnp.zeros((pad,), F32)])
    return flat.reshape(-1, V7X_LANES)


def _unpack(packed, shapes):
    flat = packed.reshape(-1)
    out, o = [], 0
    for s in shapes:
        size = 1
        for dim in s:
            size *= dim
        out.append(flat[o:o + size].reshape(s))
        o += size
    return out


def _residual_epilogue(acc, res):
    return (ALPHA * res + acc,)


def _plain_add_epilogue(acc, res):
    return (res + acc,)


def _gate_epilogue(acc, y, e):
    gate = _sigmoid(acc)
    xn = y + gate * e
    return xn, gate, xn


def _relu2(pre):
    r = jnp.maximum(pre, 0.0)
    return r * r


def _relu2_bwd_epilogue(acc, pre):
    return (acc * (2.0 * jnp.maximum(pre.astype(F32), 0.0)),)


def _tail_fwd(tag, u_a, wts, lng, lnb, p_l, finish_w1, finish_w2, after=()):
    y1, y1_b = _ln_fwd(f"ln1_{tag}", u_a, lng[0], lnb[0], after)
    finish_w1(y1_b)
    (pre,) = _mm_fwd(f"mlp1_{tag}", y1_b, wts["w1"], "col", [BF16])
    finish_w2(pre)
    (u_b,) = _mm_fwd(f"mlp2_{tag}", pre, wts["w2"], "row", [F32], _residual_epilogue, (y1,), a_fn=_relu2)
    y2, y2_b = _ln_fwd(f"ln2_{tag}", u_b, lng[1], lnb[1])
    (e,) = _mm_fwd(f"ple_{tag}", p_l, wts["plew"], "col", [F32])
    xn, gate, xn_b = _mm_fwd(f"gate_{tag}", y2_b, wts["gate"], "row", [F32, F32, BF16], _gate_epilogue, (y2, e))
    return xn, xn_b, (u_a, y1_b, pre, u_b, y2_b, e, gate)


def _tail_bwd(tag, dxn, saved, wts, lng, p_l, emit, advance, toks):
    u_a, y1_b, pre, u_b, y2_b, e, gate = saved
    dgpre, de = _ple_bwd(f"ple_bwd_{tag}", dxn, e, gate)
    toks = emit(f"{tag}_ple", dict(gate=_mm_dw(f"gate_dw_{tag}", y2_b, dgpre, "row", after=toks),
                                   plew=_mm_dw(f"ple_dw_{tag}", p_l, de, "col")))
    (dy2,) = _mm_dx(f"gate_dx_{tag}", dgpre, wts["gate"], "row", [F32], _plain_add_epilogue, (dxn,), after=toks)
    toks = advance((dy2,))
    du_b, du_b16, dg2, db2 = _ln_bwd(f"ln2_bwd_{tag}", u_b, dy2, lng[1])
    toks = emit(f"{tag}_w2", dict(w2=_mm_dw(f"mlp2_dw_{tag}", pre, du_b16, "row", a_fn=_relu2, after=toks)))
    (dpre,) = _mm_dx(f"mlp2_dx_{tag}", du_b16, wts["w2"], "row", [BF16], _relu2_bwd_epilogue, (pre,), after=toks)
    toks = advance((dpre,))
    toks = emit(f"{tag}_w1", dict(w1=_mm_dw(f"mlp1_dw_{tag}", y1_b, dpre, "col", after=toks)))
    (dy1,) = _mm_dx(f"mlp1_dx_{tag}", dpre, wts["w1"], "col", [F32], _residual_epilogue, (du_b,), after=toks)
    toks = advance((dy1,))
    du_a, du_a16, dg1, db1 = _ln_bwd(f"ln1_bwd_{tag}", u_a, dy1, lng[0])
    return du_a, du_a16, [dg1, dg2], [db1, db2], toks


def _to_slots(a, axis):
    shape = a.shape
    per = shape[axis] // N_DEV
    v = a.reshape(shape[:axis] + (N_DEV, per) + shape[axis + 1:])
    return jnp.moveaxis(v, axis, 0)


def _pad_lanes(a):
    return jnp.pad(a, [(0, 0)] * (a.ndim - 1) + [(0, V7X_LANES - a.shape[-1])])


def kernel(x, p, pool_w, pool_scale, ssm_in_w, ssm_conv_w, ssm_conv_b, ssm_dt_bias, ssm_a_log, ssm_d, ssm_norm_w, ssm_out_w, mlp_w1, mlp_w2, ln_g, ln_b, ple_w, ple_gate_w, loss_target, m_pool_w, m_pool_scale, m_ssm_in_w, m_ssm_conv_w, m_ssm_conv_b, m_ssm_dt_bias, m_ssm_a_log, m_ssm_d, m_ssm_norm_w, m_ssm_out_w, m_mlp_w1, m_mlp_w2, m_ln_g, m_ln_b, m_ple_w, m_ple_gate_w, v_pool_w, v_pool_scale, v_ssm_in_w, v_ssm_conv_w, v_ssm_conv_b, v_ssm_dt_bias, v_ssm_a_log, v_ssm_d, v_ssm_norm_w, v_ssm_out_w, v_mlp_w1, v_mlp_w2, v_ln_g, v_ln_b, v_ple_w, v_ple_gate_w):
    t, d = x.shape[1:]
    h_n = ssm_dt_bias.shape[-1]
    di_s, cd_s, dp_s, d_s = ssm_norm_w.shape[-1], ssm_conv_b.shape[-1], ssm_in_w.shape[-1], ln_g.shape[-1]
    di, cd, dp = N_DEV * di_s, N_DEV * cd_s, N_DEV * dp_s
    p_dim = di // h_n
    g_n = (cd - di) // (2 * D_STATE)
    hpg = h_n // g_n
    zw = di + cd
    assert h_n <= V7X_LANES and dp == zw + h_n and zw % V7X_LANES == 0 and zw % h_n == 0
    cg = d // 4
    me = 4 * lax.axis_index("x") + 2 * lax.axis_index("y") + lax.axis_index("c")

    x0, target = x[0], loss_target[0]
    p_l = [p[0, 0].astype(BF16), p[1, 0].astype(BF16)]

    small_shapes = [(CONV_WIDTH, cd_s), (1, cd_s), (1, di_s), (2, 2, d_s), (2, 2, d_s)]
    small = _pack([ssm_conv_w[0], ssm_conv_b, ssm_norm_w, ln_g, ln_b], 8 * V7X_LANES)
    first = [w.astype(BF16) for w in (pool_w[0], mlp_w1[0])]

    def gather_start(tag, own, after):
        lands = [lax.empty((N_DEV,) + w.shape, w.dtype) for w in own]
        return _async_start(f"ag_{tag}_start", _ag_first_copies, (4 * len(own),), own, lands, after)

    def gather_pass(tag, handle, after):
        n = len(handle[2]) // 2
        own, lands = _async_wait(f"ag_{tag}_wait", _ag_first_copies, handle, n, after)
        return _async_start(f"ag_{tag}_forward_start", _ag_forward_copies, (3 * n,), [], lands), own

    def gather_done(tag, passed, after=()):
        fwd, own = passed
        _, lands = _async_wait(f"ag_{tag}_forward_wait", _ag_forward_copies, fwd, 0, after)
        return [lax.dynamic_update_slice_in_dim(g, w[None], me, 0) for g, w in zip(lands, own)]

    def gather_finish(tag, handle, after):
        return gather_done(tag, gather_pass(tag, handle, after))

    ag_first = gather_start("first", first + [small], ())
    zero = ag_first[3][0, 0]
    own_l0 = [(w + zero).astype(BF16) for w in (mlp_w2[0], ple_w[0], ple_gate_w[0])]
    own_ssm = [(ssm_in_w[0] + zero).astype(BF16).T]
    own_out = [(ssm_out_w[0] + zero).astype(BF16)]
    own_mlp = [(w + zero).astype(BF16) for w in (mlp_w1[1], mlp_w2[1], ple_w[1], ple_gate_w[1])]
    ag_l0 = gather_start("l0", own_l0, (ag_first[3],))
    ag_ssm = gather_start("ssm", own_ssm, (ag_l0[3],))
    ag_out = gather_start("out", own_out, (ag_ssm[3],))
    ag_mlp = gather_start("mlp1", own_mlp, (ag_out[3],))
    pooled = _pool_windows("pool_fwd", x0, False, after=(ag_mlp[3],))
    pool_g, w1_0, small_g = gather_finish("first", ag_first, (pooled,))
    pool_full = pool_g.transpose(1, 0, 2, 3).reshape(4, cg, cg)
    sm = small_g.reshape(N_DEV, -1)
    o = 0
    parts = []
    for shp in small_shapes:
        size = 1
        for s in shp:
            size *= s
        parts.append(sm[:, o:o + size].reshape((N_DEV,) + shp))
        o += size
    conv_w_full = parts[0].transpose(1, 0, 2).reshape(CONV_WIDTH, cd)
    conv_b_full = parts[1].transpose(1, 0, 2).reshape(1, cd)
    norm_w_full = parts[2].transpose(1, 0, 2).reshape(1, di)
    ln_g_full = parts[3].transpose(1, 2, 0, 3).reshape(2, 2, 1, d)
    ln_b_full = parts[4].transpose(1, 2, 0, 3).reshape(2, 2, 1, d)
    bias128, alog128 = _pad_lanes(ssm_dt_bias), _pad_lanes(ssm_a_log)

    u0, hraw = _pool_mm("pool_mm", pooled, pool_full, pool_scale, x0)
    wts = [dict(w1=w1_0)]

    def finish_l0(after):
        w2_0, plew_0, gate_0 = gather_finish("l0", ag_l0, (after,))
        wts[0].update(w2=w2_0, plew=plew_0, gate=gate_0)

    x1, x1_b, saved0 = _tail_fwd("l0", u0, wts[0], ln_g_full[0], ln_b_full[0], p_l[0], lambda after: None, finish_l0)

    (in_g,) = gather_finish("ssm", ag_ssm, (x1,))
    in_t = in_g.reshape(dp, d)
    (zx,) = _mm_dx("in_proj", x1_b, in_t, "plain", [F32], k_rows=zw)
    dt_raw = _pad_lanes(_in_proj_dt("in_proj_dt", x1_b, in_t, zw, h_n))
    xbc = _conv_fwd("conv_fwd", zx, conv_w_full, conv_b_full, di)
    dt, acs, e_in, e_out = _dt_fwd("dt_fwd", dt_raw, bias128, alog128, 0)

    dt_x, ein_x, eout_x = _expand_heads("expand_heads", [dt, e_in, e_out], h_n, p_dim)
    d_x = jnp.repeat(ssm_d, p_dim, axis=1)

    def to_col(a):
        return a[:, :h_n].reshape(t, g_n, hpg).transpose(1, 0, 2)

    def to_row(a):
        return a[:, :h_n].reshape(t, g_n, hpg).transpose(1, 2, 0)

    def from_col(a):
        return _pad_lanes(a.transpose(1, 0, 2).reshape(t, h_n))

    a_col, a_row = to_col(acs), to_row(acs)
    y_ssd, prev = _ssd_fwd("ssd_fwd", xbc, dt_x, ein_x, eout_x, a_col, a_row, d_x, di, g_n, hpg, p_dim)
    out_passed = gather_pass("out", ag_out, (y_ssd,))
    yn = _gated_rms_fwd("gated_rms_fwd", y_ssd, zx, norm_w_full, after=(out_passed[0][3],))
    (out_g,) = gather_done("out", out_passed, (yn,))
    (u2,) = _mm_fwd("out_proj", yn, out_g, "row", [F32], _residual_epilogue, (x1,))
    mlp_passed = gather_pass("mlp1", ag_mlp, (u2,))
    wts.append({})

    def finish_l1(after):
        wts[1].update(zip(("w1", "w2", "plew", "gate"), gather_done("mlp1", mlp_passed, (after,))))

    x2, _, saved1 = _tail_fwd("l1", u2, wts[1], ln_g_full[1], ln_b_full[1], p_l[1], finish_l1, lambda after: None,
                              after=(mlp_passed[0][3],))

    core = lax.axis_index("c").astype(jnp.int32).reshape(1)
    chip = (2 * lax.axis_index("x") + lax.axis_index("y")).astype(jnp.int32).reshape(1)
    scattering = {}
    pending = []

    def to_chips(after):
        if not pending:
            return []
        tag, names, handle = pending.pop()
        own, halves = _async_wait(f"rs_{tag}_sibling_wait", _rs_sibling_copies, handle, len(names), after)
        sums = [_pair_sum(f"rs_{tag}_pair_sum_{n}", g, hv, core) for n, g, hv in zip(names, own, halves)]
        lands = [lax.empty((3,) + s.shape[1:], s.dtype) for s in sums]
        handle = _async_start(f"rs_{tag}_start", _rs_chip_copies, (3 * len(sums),), sums, lands)
        scattering[tag] = (names, handle)
        return [handle[3]]

    def emit(tag, grads):
        names, arrays = list(grads), list(grads.values())
        toks = to_chips(tuple(arrays))
        lands = [lax.empty((4,) + g.shape[1:], g.dtype) for g in arrays]
        handle = _async_start(f"rs_{tag}_sibling_start", _rs_sibling_copies, (4 * len(arrays),), arrays, lands,
                              tuple(toks))
        pending.append((tag, names, handle))
        return toks + [handle[3]]

    def collect(tag, after):
        names, handle = scattering.pop(tag)
        sums, thirds = _async_wait(f"rs_{tag}_wait", _rs_chip_copies, handle, len(names), after)
        return {n: (s.reshape(4, -1, s.shape[-1]), r.reshape(3, -1, r.shape[-1])) for n, s, r in zip(names, sums, thirds)}

    dx2, loss_cols = _loss_bwd("loss", x2, target)
    du2, du2_b, dg_1, db_1, toks = _tail_bwd("l1", dx2, saved1, wts[1], ln_g_full[1], p_l[1], emit, to_chips, [])
    toks = emit("ssm_out", dict(out=_mm_dw("out_proj_dw", yn, du2_b, "row", after=toks)))
    (dyn,) = _mm_dx("out_proj_dx", du2_b, out_g, "row", [F32], after=toks)
    toks = to_chips((dyn,))
    dy_ssd, dz, d_norm_w = _gated_rms_bwd("gated_rms_bwd", y_ssd, zx, norm_w_full, dyn)
    dxs, dbm, dcm, ddt_x, dacs, dd = _ssd_bwd("ssd_bwd", xbc, dt_x, ein_x, eout_x, a_col, a_row, d_x, prev, dy_ssd,
                                              di, g_n, hpg, p_dim)
    draw, d_bias, d_alog = _dt_bwd("dt_bwd", dt_raw, bias128, alog128, from_col(dacs), from_col(ddt_x), 0)
    conv_parts = [_conv_bwd(f"conv_bwd_{tag}", zx, conv_w_full, conv_b_full, dact, di, first)
                  for tag, dact, first in (("xs", dxs, 0), ("b", dbm, di), ("c", dcm, di + g_n * D_STATE))]
    d_conv_w = jnp.concatenate([c[1] for c in conv_parts], axis=1)
    d_conv_b = jnp.concatenate([c[2] for c in conv_parts], axis=1)
    dzx = jnp.concatenate([dz] + [c[0] for c in conv_parts], axis=1)
    d_dt = draw[:, :h_n]
    g_in_t = _in_proj_dw_t("in_proj_dw", dzx, d_dt, x1_b, after=toks)
    toks = emit("ssm_in", {"in": g_in_t.reshape(N_DEV, dp_s, d)})
    dx_dt = _in_proj_dt_dx("in_proj_dt_dx", d_dt, in_t, zw, du2)
    (dx1,) = _mm_fwd("in_proj_dx", dzx, in_t, "plain", [F32], _plain_add_epilogue, (dx_dt,), after=toks, k_rows=zw)
    toks = to_chips((dx1,))

    du0, _, dg_0, db_0, toks = _tail_bwd("l0", dx1, saved0, wts[0], ln_g_full[0], p_l[0], emit, to_chips, toks)
    dh, dpool, d_scale = _pool_bwd_mm("pool_bwd_mm", du0, hraw, pool_full, pool_scale)
    toks += emit("pool", dict(pool=_to_slots(_pool_dw("pool_dw", pooled, dh), 1)))
    grad_x = _pool_windows("pool_bwd", dpool, True, du0, after=tuple(toks))
    toks = to_chips((grad_x,))

    def update(tag, w, m, v, parts, layer, prev=None):
        own, recv = parts
        return _adamw(f"adamw_{tag}_{layer}", _as3d(w), _as3d(m), _as3d(v),
                      [(own, None), (recv, 0), (recv, 1), (recv, 2)], layer, prev, chip)

    q = collect("l1_ple", (grad_x, *toks))
    r_gate = update("ple_gate_w", ple_gate_w, m_ple_gate_w, v_ple_gate_w, q["gate"], 1)
    r_plew = update("ple_w", ple_w, m_ple_w, v_ple_w, q["plew"], 1)
    r_w2 = update("mlp_w2", mlp_w2, m_mlp_w2, v_mlp_w2, collect("l1_w2", (r_gate[0],))["w2"], 1)
    r_w1 = update("mlp_w1", mlp_w1, m_mlp_w1, v_mlp_w1, collect("l1_w1", (r_w2[0],))["w1"], 1)
    r_out = update("ssm_out_w", ssm_out_w, m_ssm_out_w, v_ssm_out_w, collect("ssm_out", (r_w1[0],))["out"], 0)
    in_wt, in_mt, in_vt = [jnp.swapaxes(a, 1, 2) for a in (ssm_in_w, m_ssm_in_w, v_ssm_in_w)]
    r_in = update("ssm_in_w", in_wt, in_mt, in_vt, collect("ssm_in", (r_out[0],))["in"], 0)

    d_ln_g = jnp.stack([jnp.stack(dg_0), jnp.stack(dg_1)]).reshape(2, 2, d)
    d_ln_b = jnp.stack([jnp.stack(db_0), jnp.stack(db_1)]).reshape(2, 2, d)
    partial_shapes = [(CONV_WIDTH, cd), (1, cd), (1, di), (2, 2, d), (2, 2, d), (1, d), (1, h_n), (1, h_n), (1, h_n),
                      (1, d)]
    partial = _pack([d_conv_w, d_conv_b, d_norm_w, d_ln_g, d_ln_b, d_scale, d_bias[:, :h_n], d_alog[:, :h_n],
                     dd.reshape(1, h_n), loss_cols], 8 * V7X_LANES)
    (all_partials,) = _all_gather("ag_small_grads", [partial], after=(r_in[0],))
    tot = _unpack(_sum8("sum_small_grads", all_partials), partial_shapes)
    t_conv_w, t_conv_b, t_norm_w, t_ln_g, t_ln_b, t_scale, t_bias, t_alog, t_dd, t_loss = tot
    loss = jnp.sum(t_loss)

    def mine(a, per):
        return lax.dynamic_slice_in_dim(a, me * per, per, axis=a.ndim - 1)

    small_names = ["ssm_conv_w", "ssm_conv_b", "ssm_norm_w", "ln_g", "ln_b", "pool_scale", "ssm_dt_bias", "ssm_a_log",
                   "ssm_d"]
    small_w = [ssm_conv_w, ssm_conv_b, ssm_norm_w, ln_g, ln_b, pool_scale, ssm_dt_bias, ssm_a_log, ssm_d]
    small_m = [m_ssm_conv_w, m_ssm_conv_b, m_ssm_norm_w, m_ln_g, m_ln_b, m_pool_scale, m_ssm_dt_bias, m_ssm_a_log,
               m_ssm_d]
    small_v = [v_ssm_conv_w, v_ssm_conv_b, v_ssm_norm_w, v_ln_g, v_ln_b, v_pool_scale, v_ssm_dt_bias, v_ssm_a_log,
               v_ssm_d]
    small_grads = [mine(t_conv_w, cd_s), mine(t_conv_b, cd_s), mine(t_norm_w, di_s), mine(t_ln_g, d_s),
                   mine(t_ln_b, d_s), t_scale, t_bias, t_alog, t_dd]
    shapes = [w.shape for w in small_w]
    pk = [_pack(group, 8 * V7X_LANES)[None] for group in (small_w, small_m, small_v, small_grads)]
    res = _adamw("adamw_small", pk[0], pk[1], pk[2], [(pk[3], 0)], 0)
    upd = {}
    for name, vals in zip(small_names, zip(*[_unpack(r, shapes) for r in res])):
        upd[name] = list(vals)

    q = collect("l0_ple", (res[0],))
    r_gate = update("ple_gate_w", ple_gate_w, m_ple_gate_w, v_ple_gate_w, q["gate"], 0, r_gate)
    r_plew = update("ple_w", ple_w, m_ple_w, v_ple_w, q["plew"], 0, r_plew)
    r_w2 = update("mlp_w2", mlp_w2, m_mlp_w2, v_mlp_w2, collect("l0_w2", (r_gate[0],))["w2"], 0, r_w2)
    r_w1 = update("mlp_w1", mlp_w1, m_mlp_w1, v_mlp_w1, collect("l0_w1", (r_w2[0],))["w1"], 0, r_w1)
    r_pool = update("pool_w", pool_w, m_pool_w, v_pool_w, collect("pool", (r_w1[0],))["pool"], 0)
    assert not scattering
    large = {"pool_w": (pool_w, r_pool), "ssm_in_w": (in_wt, r_in), "ssm_out_w": (ssm_out_w, r_out),
             "mlp_w1": (mlp_w1, r_w1), "mlp_w2": (mlp_w2, r_w2), "ple_w": (ple_w, r_plew),
             "ple_gate_w": (ple_gate_w, r_gate)}
    for name, (w, rs) in large.items():
        upd[name] = [r.reshape(w.shape) for r in rs]
    upd["ssm_in_w"] = [jnp.swapaxes(r, 1, 2) for r in upd["ssm_in_w"]]

    order = ["pool_w", "pool_scale", "ssm_in_w", "ssm_conv_w", "ssm_conv_b", "ssm_dt_bias", "ssm_a_log", "ssm_d",
             "ssm_norm_w", "ssm_out_w", "mlp_w1", "mlp_w2", "ln_g", "ln_b", "ple_w", "ple_gate_w"]
    out = [loss, grad_x[None]]
    for k in range(4):
        out += [upd[name][k] for name in order]
    return tuple(out)
```

```python
import jax
import jax.numpy as jnp
from jax import lax
from jax.experimental import pallas as pl
from jax.experimental.pallas import tpu as pltpu

F32 = jnp.float32
BF16 = jnp.bfloat16
SDS = jax.ShapeDtypeStruct
MESH = pl.DeviceIdType.MESH
ANY = pl.BlockSpec(memory_space=pl.ANY)

N_DEV = 8
DEPTH = 2
ALPHA = (2.0 * DEPTH) ** 0.25
LN_EPS = 1e-5
RMS_EPS = 1e-5
POOL_WINDOW_LOG2 = (1, 2, 3, 4)
D_STATE = 128
CHUNK = 128
SSD_STEP_CHUNKS = 8
CONV_WIDTH = 4
ADAM_LR = 0.001
ADAM_B1 = 0.9
ADAM_B2 = 0.999
ADAM_EPS = 1e-08
ADAM_WD = 0.01
ADAM_STEP = 10

V7X_LANES = 128
V7X_VMEM_LIMIT = 48 * 1024 * 1024


def _cp(*sem):
    return pltpu.CompilerParams(dimension_semantics=sem, vmem_limit_bytes=V7X_VMEM_LIMIT)


def _pick(dim, cap):
    if dim <= cap:
        return dim
    best = None
    for t in range(V7X_LANES, cap + 1, V7X_LANES):
        if dim % t == 0:
            best = t
    assert best is not None, (dim, cap)
    return best


def _row_tile(rows, cols, itemsize=4, target=1 << 20):
    t = rows
    while t % 2 == 0 and t // 2 >= 16 and (t // 2) % 16 == 0 and t * cols * itemsize > target:
        t //= 2
    return t


def _all_gather(name, shards, after=()):
    n, na = len(shards), len(after)

    def body(*refs):
        ins, outs = refs[:n], refs[n + na:2 * n + na]
        send_sems, recv_sems, local_sems = refs[2 * n + na:]
        x, y, c = lax.axis_index("x"), lax.axis_index("y"), lax.axis_index("c")
        me, sibling = (x, y, c), (x, y, 1 - c)
        chips = [(1 - x, y), (x, 1 - y), (1 - x, 1 - y)]

        def copy(a, k, block, to, src=None):
            dst = outs[a].at[4 * block[0] + 2 * block[1] + block[2]]
            return pltpu.make_async_remote_copy(
                src_ref=dst if src is None else src, dst_ref=dst, send_sem=send_sems.at[a, k],
                recv_sem=recv_sems.at[a, k], device_id=to, device_id_type=MESH)

        mine = [pltpu.make_async_copy(ins[a], outs[a].at[4 * x + 2 * y + c], local_sems.at[a]) for a in range(n)]
        for cp in mine:
            cp.start()
        first = []
        for a in range(n):
            first.append(copy(a, 0, me, sibling, src=ins[a]))
            first += [copy(a, 1 + j, me, (*chip, c), src=ins[a]) for j, chip in enumerate(chips)]
        for cp in first:
            cp.start()
        passed = []
        for j, chip in enumerate(chips):
            for a in range(n):
                copy(a, 1 + j, (*chip, c), me).wait_recv()
                fwd = copy(a, 4 + j, (*chip, c), sibling)
                fwd.start()
                passed.append(fwd)
        for a in range(n):
            copy(a, 0, sibling, me).wait_recv()
            for j, chip in enumerate(chips):
                copy(a, 4 + j, (*chip, 1 - c), me).wait_recv()
        for cp in first + passed:
            cp.wait_send()
        for cp in mine:
            cp.wait()

    return pl.pallas_call(
        body, name=name,
        out_shape=[SDS((N_DEV,) + s.shape, s.dtype) for s in shards],
        in_specs=[ANY] * (n + na), out_specs=[ANY] * n,
        scratch_shapes=[pltpu.SemaphoreType.DMA((n, 7)), pltpu.SemaphoreType.DMA((n, 7)),
                        pltpu.SemaphoreType.DMA((n,))],
    )(*shards, *after)


HBM_SPEC = pl.BlockSpec(memory_space=pltpu.HBM)
SEM_SPEC = pl.BlockSpec(memory_space=pltpu.SEMAPHORE)
EFFECT = pltpu.SideEffectType.DATAFLOW_SIDE_EFFECTING


def _ag_first_copies(ins, lands, send_sems, recv_sems):
    x, y, c = lax.axis_index("x"), lax.axis_index("y"), lax.axis_index("c")
    targets = [(x, y, 1 - c), (1 - x, y, c), (x, 1 - y, c), (1 - x, 1 - y, c)]
    return [pltpu.make_async_remote_copy(
        src_ref=ins[a], dst_ref=lands[a].at[4 * x + 2 * y + c], send_sem=send_sems.at[4 * a + k],
        recv_sem=recv_sems.at[4 * a + k], device_id=to, device_id_type=MESH)
        for a in range(len(ins)) for k, to in enumerate(targets)]


def _ag_forward_copies(ins, lands, send_sems, recv_sems):
    x, y, c = lax.axis_index("x"), lax.axis_index("y"), lax.axis_index("c")
    cps = []
    for a in range(len(lands)):
        for j, (px, py) in enumerate([(1 - x, y), (x, 1 - y), (1 - x, 1 - y)]):
            blk = lands[a].at[4 * px + 2 * py + c]
            cps.append(pltpu.make_async_remote_copy(
                src_ref=blk, dst_ref=blk, send_sem=send_sems.at[3 * a + j], recv_sem=recv_sems.at[3 * a + j],
                device_id=(x, y, 1 - c), device_id_type=MESH))
    return cps


def _rs_sibling_copies(ins, lands, send_sems, recv_sems):
    x, y, c = lax.axis_index("x"), lax.axis_index("y"), lax.axis_index("c")
    return [pltpu.make_async_remote_copy(
        src_ref=ins[a].at[2 * q + 1 - c], dst_ref=lands[a].at[q], send_sem=send_sems.at[4 * a + q],
        recv_sem=recv_sems.at[4 * a + q], device_id=(x, y, 1 - c), device_id_type=MESH)
        for a in range(len(ins)) for q in range(4)]


def _rs_chip_copies(ins, lands, send_sems, recv_sems):
    x, y, c = lax.axis_index("x"), lax.axis_index("y"), lax.axis_index("c")
    cps = []
    for a in range(len(ins)):
        for j, (px, py) in enumerate([(1 - x, y), (x, 1 - y), (1 - x, 1 - y)]):
            cps.append(pltpu.make_async_remote_copy(
                src_ref=ins[a].at[2 * px + py], dst_ref=lands[a].at[j], send_sem=send_sems.at[3 * a + j],
                recv_sem=recv_sems.at[3 * a + j], device_id=(px, py, c), device_id_type=MESH))
    return cps


def _async_start(name, build, sem_shape, ins, lands, after=()):
    arrays = [*ins, *lands]
    n_i, n_t, n_a = len(ins), len(arrays), len(after)

    def body(*refs):
        outs = refs[n_t + n_a:]
        for cp in build(refs[:n_i], refs[n_i:n_t], outs[0], outs[1]):
            cp.start()
        outs[-1][...] = jnp.zeros_like(outs[-1])

    res = pl.pallas_call(
        body, name=name,
        out_shape=(pltpu.SemaphoreType.DMA(sem_shape), pltpu.SemaphoreType.DMA(sem_shape),
                   *[pltpu.HBM(a.shape, a.dtype) for a in arrays], SDS((8, V7X_LANES), F32)),
        in_specs=[HBM_SPEC] * n_t + [ANY] * n_a,
        out_specs=(SEM_SPEC, SEM_SPEC, *[HBM_SPEC] * n_t, pl.BlockSpec(memory_space=pltpu.VMEM)),
        input_output_aliases={i: 2 + i for i in range(n_t)},
        compiler_params=pltpu.CompilerParams(has_side_effects=EFFECT),
    )(*[pltpu.with_memory_space_constraint(a, pltpu.HBM) for a in arrays], *after)
    return res[0], res[1], list(res[2:2 + n_t]), res[-1]


def _async_wait(name, build, handle, n_i, after=()):
    send_sems, recv_sems, arrays, _ = handle
    n_t, n_a = len(arrays), len(after)

    def body(*refs):
        for cp in build(refs[:n_i], refs[n_i:n_t], refs[n_t], refs[n_t + 1]):
            cp.wait_send()
            cp.wait_recv()

    res = pl.pallas_call(
        body, name=name, out_shape=tuple(pltpu.HBM(a.shape, a.dtype) for a in arrays),
        in_specs=[HBM_SPEC] * n_t + [SEM_SPEC, SEM_SPEC] + [ANY] * n_a, out_specs=tuple([HBM_SPEC] * n_t),
        input_output_aliases={i: i for i in range(n_t)},
        compiler_params=pltpu.CompilerParams(has_side_effects=EFFECT),
    )(*arrays, send_sems, recv_sems, *after)
    return list(res[:n_i]), list(res[n_i:])


def _mm_core(name, a, b, *, grid, a_spec, b_spec, dims, acc_shape, outs, out_spec, epilogue=None, extras=(),
             extra_specs=(), a_fn=None, after=(), carry=None):
    nk = grid[2]
    if carry is not None:
        after = (*after, carry)
    ne, no, na = len(extras), len(outs), len(after)

    def body(a_ref, b_ref, *rest):
        e_refs, o_refs, acc = rest[:ne], rest[ne + na:ne + na + no], rest[ne + na + no]
        k = pl.program_id(2)

        def product():
            lhs = a_ref[...] if a_fn is None else a_fn(a_ref[...])
            return lax.dot_general(lhs.astype(BF16), b_ref[...].astype(BF16), dims, preferred_element_type=F32)

        @pl.when(k == 0)
        def _():
            acc[...] = product()

        @pl.when(k > 0)
        def _():
            acc[...] += product()

        @pl.when(k == nk - 1)
        def _():
            r = acc[...]
            vals = epilogue(r, *[e[...] for e in e_refs]) if epilogue is not None else (r,)
            for o, v in zip(o_refs, vals):
                o[...] = v.astype(o.dtype)

    res = pl.pallas_call(
        body, name=name, grid=grid, out_shape=list(outs),
        in_specs=[a_spec, b_spec, *extra_specs, *[ANY] * na], out_specs=[out_spec] * no,
        scratch_shapes=[pltpu.VMEM(acc_shape, F32)],
        input_output_aliases={} if carry is None else {1 + ne + na: 0},
        compiler_params=_cp("parallel", "parallel", "arbitrary"),
    )(a, b, *extras, *after)
    return res


NN = (((1,), (0,)), ((), ()))
NT = (((1,), (1,)), ((), ()))
TN = (((0,), (0,)), ((), ()))


def _w_dims(w, kind):
    if kind == "col":
        return w.shape[1], N_DEV * w.shape[2], w.shape[1], w.shape[2]
    if kind == "row":
        return N_DEV * w.shape[1], w.shape[2], w.shape[1], w.shape[2]
    return w.shape[0], w.shape[1], w.shape[0], w.shape[1]


MM_VMEM_BUDGET = 40 * 1024 * 1024


def _blocks(m, n_len, n_caps, k_len, k_caps, a, out_dtypes, extras):
    per_out = sum(jnp.dtype(dt).itemsize for dt in out_dtypes) + sum(e.dtype.itemsize for e in extras)
    best = None
    for tm in (_pick(m, 2048), _pick(m, 1024), _pick(m, 512)):
        for tn in [_pick(n_len, cap) for cap in n_caps]:
            for tk in [_pick(k_len, cap) for cap in k_caps]:
                used = tm * tn * (4 + 2 * per_out) + 2 * (tm * tk * a.dtype.itemsize + tk * tn * 2)
                key = ((m // tm) * (n_len // tn) * (k_len // tk), -tm, -tn)
                if used <= MM_VMEM_BUDGET and (best is None or key < best[0]):
                    best = (key, tm, tn, tk)
    assert best is not None, (m, n_len, k_len)
    return best[1:]


def _mm_fwd(name, a, w, kind, out_dtypes, epilogue=None, extras=(), a_fn=None, after=(), k_rows=None):
    if kind == "row":
        w, kind = w.reshape(-1, w.shape[-1]), "plain"
    m = a.shape[0]
    kk, n, ks, ns = _w_dims(w, kind)
    if k_rows is None:
        assert kk == a.shape[1]
    else:
        assert kind == "plain" and k_rows <= min(kk, a.shape[1])
        kk = k_rows
    if kind == "col":
        tm, tn, tk = _blocks(m, ns, (1024,), kk, (1024, 512), a, out_dtypes, extras)
    else:
        tm, tn, tk = _blocks(m, n, (2048, 1152, 1024), kk, (1024, 512), a, out_dtypes, extras)
    if kind == "col":
        nb = ns // tn
        b_spec = pl.BlockSpec((None, tk, tn), lambda i, j, k: (j // nb, k, j % nb))
    else:
        b_spec = pl.BlockSpec((tk, tn), lambda i, j, k: (k, j))
    mn_spec = pl.BlockSpec((tm, tn), lambda i, j, k: (i, j))
    return _mm_core(
        name, a, w, grid=(m // tm, n // tn, kk // tk),
        a_spec=pl.BlockSpec((tm, tk), lambda i, j, k: (i, k)), b_spec=b_spec, dims=NN, acc_shape=(tm, tn),
        outs=[SDS((m, n), dt) for dt in out_dtypes], out_spec=mn_spec, epilogue=epilogue, extras=extras,
        extra_specs=[mn_spec] * len(extras), a_fn=a_fn, after=after)


def _mm_dx(name, dy, w, kind, out_dtypes, epilogue=None, extras=(), after=(), k_rows=None):
    if kind == "row":
        w, kind = w.reshape(-1, w.shape[-1]), "plain"
    m, n_dim = dy.shape
    kk, n, ks, ns = _w_dims(w, kind)
    assert n == n_dim
    if k_rows is not None:
        assert kind == "plain" and k_rows <= kk
        kk = k_rows
    if kind == "col":
        tm, tn, tk = _blocks(m, kk, (2048, 1024), ns, (1024, 512), dy, out_dtypes, extras)
        kb = ns // tk
        b_spec = pl.BlockSpec((None, tn, tk), lambda i, j, k: (k // kb, j, k % kb))
    else:
        tm, tn, tk = _blocks(m, kk, (2048, 1024), n, (1152, 512), dy, out_dtypes, extras)
        b_spec = pl.BlockSpec((tn, tk), lambda i, j, k: (j, k))
    mk_spec = pl.BlockSpec((tm, tn), lambda i, j, k: (i, j))
    return _mm_core(
        name, dy, w, grid=(m // tm, kk // tn, n // tk),
        a_spec=pl.BlockSpec((tm, tk), lambda i, j, k: (i, k)), b_spec=b_spec, dims=NT, acc_shape=(tm, tn),
        outs=[SDS((m, kk), dt) for dt in out_dtypes], out_spec=mk_spec, epilogue=epilogue, extras=extras,
        extra_specs=[mk_spec] * len(extras), after=after)


def _in_proj_dt(name, a, w_t, first, h_n):
    m, kk = a.shape
    tm, tk = _pick(m, 1024), _pick(kk, 1024)
    blk = first // h_n
    return _mm_core(
        name, a, w_t, grid=(m // tm, 1, kk // tk), a_spec=pl.BlockSpec((tm, tk), lambda i, j, k: (i, k)),
        b_spec=pl.BlockSpec((h_n, tk), lambda i, j, k: (blk, k)), dims=NT, acc_shape=(tm, h_n),
        outs=[SDS((m, h_n), F32)], out_spec=pl.BlockSpec((tm, h_n), lambda i, j, k: (i, 0)))[0]


def _in_proj_dt_dx(name, d_dt, w_t, first, res):
    m, h_n = d_dt.shape
    n = w_t.shape[1]
    tm, tn = _pick(m, 1024), _pick(n, 1024)
    blk = first // h_n
    mn_spec = pl.BlockSpec((tm, tn), lambda i, j, k: (i, j))
    return _mm_core(
        name, d_dt, w_t, grid=(m // tm, n // tn, 1), a_spec=pl.BlockSpec((tm, h_n), lambda i, j, k: (i, 0)),
        b_spec=pl.BlockSpec((h_n, tn), lambda i, j, k: (blk, j)), dims=NN, acc_shape=(tm, tn),
        outs=[SDS((m, n), F32)], out_spec=mn_spec, epilogue=_residual_epilogue, extras=(res,),
        extra_specs=[mn_spec])[0]


def _in_proj_dw_t(name, dz, d_dt, x, after=()):
    m, n = x.shape
    zw, h_n = dz.shape[1], d_dt.shape[1]
    tm, tn, tk = _pick(zw, 1024), _pick(n, 2048), _pick(m, 1024)
    out = SDS((zw + h_n, n), BF16)
    x_spec = pl.BlockSpec((tk, tn), lambda i, j, k: (k, j))
    main = _mm_core(
        name, dz, x, grid=(zw // tm, n // tn, m // tk), a_spec=pl.BlockSpec((tk, tm), lambda i, j, k: (k, i)),
        b_spec=x_spec, dims=TN, acc_shape=(tm, tn), outs=[out], out_spec=pl.BlockSpec((tm, tn), lambda i, j, k: (i, j)),
        after=after)[0]
    blk = zw // h_n
    return _mm_core(
        name + "_dt", d_dt, x, grid=(1, n // tn, m // tk), a_spec=pl.BlockSpec((tk, h_n), lambda i, j, k: (k, 0)),
        b_spec=x_spec, dims=TN, acc_shape=(h_n, tn), outs=[out],
        out_spec=pl.BlockSpec((h_n, tn), lambda i, j, k: (blk, j)), carry=main)[0]


def _mm_dw(name, a, dy, kind, a_fn=None, after=()):
    m, kk = a.shape
    n = dy.shape[1]
    tk = _pick(m, 1024)
    if kind == "col":
        ns = n // N_DEV
        tm, tn = _pick(kk, 2048), _pick(ns, 1024)
        nb = ns // tn
        out = SDS((N_DEV, kk, ns), BF16)
        out_spec = pl.BlockSpec((None, tm, tn), lambda i, j, k: (j // nb, i, j % nb))
    else:
        tm, tn = _pick(kk, 1024), _pick(n, 2048)
        out = SDS((kk, n), BF16)
        out_spec = pl.BlockSpec((tm, tn), lambda i, j, k: (i, j))
    res = _mm_core(
        name, a, dy, grid=(kk // tm, n // tn, m // tk),
        a_spec=pl.BlockSpec((tk, tm), lambda i, j, k: (k, i)), b_spec=pl.BlockSpec((tk, tn), lambda i, j, k: (k, j)),
        dims=TN, acc_shape=(tm, tn), outs=[out], out_spec=out_spec, a_fn=a_fn, after=after)[0]
    return res.reshape(N_DEV, kk // N_DEV, n) if kind == "row" else res


def _rowwise(name, fn, ins, outs, rows, tile):
    arrays, specs = [], []
    for arr, kind in ins:
        arrays.append(arr)
        if kind == "row":
            specs.append(pl.BlockSpec((tile, arr.shape[1]), lambda i: (i, 0)))
        elif kind == "vec":
            specs.append(pl.BlockSpec(arr.shape, lambda i, nd=arr.ndim: (0,) * nd))
        else:
            specs.append(kind)
    out_shapes, out_specs, kinds = [], [], []
    for cols, dt, kind in outs:
        kinds.append(kind)
        if kind == "row":
            out_shapes.append(SDS((rows, cols), dt))
            out_specs.append(pl.BlockSpec((tile, cols), lambda i: (i, 0)))
        else:
            out_shapes.append(SDS((1, cols), F32))
            out_specs.append(pl.BlockSpec((1, cols), lambda i: (0, 0)))
    ni = len(arrays)
    has_acc = "acc" in kinds

    def body(*refs):
        vals = fn(*[r[...] for r in refs[:ni]])
        i = pl.program_id(0)
        for o, v, kind in zip(refs[ni:], vals, kinds):
            if kind == "row":
                o[...] = v.astype(o.dtype)
            else:
                @pl.when(i == 0)
                def _(o=o):
                    o[...] = jnp.zeros_like(o)

                o[...] += v

    return pl.pallas_call(
        body, name=name, grid=(rows // tile,), out_shape=out_shapes, in_specs=specs, out_specs=out_specs,
        compiler_params=_cp("arbitrary" if has_acc else "parallel"),
    )(*arrays)


def _ln_fwd(name, u, g, b, after=()):
    d = u.shape[1]

    def fn(u, g, b, *unused):
        mu = jnp.mean(u, axis=1, keepdims=True)
        xc = u - mu
        var = jnp.mean(xc * xc, axis=1, keepdims=True)
        y = xc * lax.rsqrt(var + LN_EPS) * g + b
        return y, y

    ins = [(u, "row"), (g, "vec"), (b, "vec")] + [(t, "vec") for t in after]
    return _rowwise(name, fn, ins, [(d, F32, "row"), (d, BF16, "row")], u.shape[0], 256)


def _ln_bwd(name, u, dy, g):
    d = u.shape[1]

    def fn(u, dy, g):
        mu = jnp.mean(u, axis=1, keepdims=True)
        xc = u - mu
        var = jnp.mean(xc * xc, axis=1, keepdims=True)
        rstd = lax.rsqrt(var + LN_EPS)
        xhat = xc * rstd
        dxhat = dy * g
        m1 = jnp.mean(dxhat, axis=1, keepdims=True)
        m2 = jnp.mean(dxhat * xhat, axis=1, keepdims=True)
        du = rstd * (dxhat - m1 - xhat * m2)
        return du, du, jnp.sum(dy * xhat, axis=0, keepdims=True), jnp.sum(dy, axis=0, keepdims=True)

    return _rowwise(name, fn, [(u, "row"), (dy, "row"), (g, "vec")],
                    [(d, F32, "row"), (d, BF16, "row"), (d, F32, "acc"), (d, F32, "acc")], u.shape[0], 256)


def _loss_bwd(name, y, target):
    d = y.shape[1]

    def fn(y, t):
        e = y - t
        return e * (1.0 / d), jnp.sum(e * e, axis=0, keepdims=True) * (0.5 / d)

    return _rowwise(name, fn, [(y, "row"), (target, "row")], [(d, F32, "row"), (d, F32, "acc")], y.shape[0], 256)


def _ple_bwd(name, dx, e, gate):
    d = dx.shape[1]

    def fn(dx, e, gate):
        return dx * e * gate * (1.0 - gate), dx * gate

    return _rowwise(name, fn, [(dx, "row"), (e, "row"), (gate, "row")], [(d, BF16, "row"), (d, BF16, "row")],
                    dx.shape[0], 256)


def _sigmoid(v):
    return 1.0 / (1.0 + jnp.exp(-v))


def _gated_rms_fwd(name, y, zx, norm_w, after=()):
    di = y.shape[1]

    def fn(y, z, w, *unused):
        yg = y * (z * _sigmoid(z))
        r = lax.rsqrt(jnp.mean(yg * yg, axis=1, keepdims=True) + RMS_EPS)
        return (yg * r * w,)

    z_spec = pl.BlockSpec((128, di), lambda i: (i, 0))
    ins = [(y, "row"), (zx, z_spec), (norm_w, "vec")] + [(t, "vec") for t in after]
    return _rowwise(name, fn, ins, [(di, BF16, "row")], y.shape[0], 128)[0]


def _gated_rms_bwd(name, y, zx, norm_w, dout):
    di = y.shape[1]

    def fn(y, z, w, dout):
        sg = _sigmoid(z)
        sz = z * sg
        yg = y * sz
        r = lax.rsqrt(jnp.mean(yg * yg, axis=1, keepdims=True) + RMS_EPS)
        dn = dout * w
        dyg = r * (dn - yg * (r * r) * jnp.mean(dn * yg, axis=1, keepdims=True))
        dy = dyg * sz
        dz = dyg * y * (sg * (1.0 + z * (1.0 - sg)))
        return dy, dz, jnp.sum(dout * yg * r, axis=0, keepdims=True)

    z_spec = pl.BlockSpec((128, di), lambda i: (i, 0))
    return _rowwise(name, fn, [(y, "row"), (zx, z_spec), (norm_w, "vec"), (dout, "row")],
                    [(di, F32, "row"), (di, BF16, "row"), (di, F32, "acc")], y.shape[0], 128)


def _shift_down(v, j, row):
    return jnp.where(row >= j, pltpu.roll(v, j, 0), 0.0)


def _shift_up(v, j, row):
    t = v.shape[0]
    return jnp.where(row < t - j, pltpu.roll(v, t - j, 0), 0.0)


def _pool_select(parts, g):
    return jnp.where(g == 0, parts[0], jnp.where(g == 1, parts[1], jnp.where(g == 2, parts[2], parts[3])))


def _pool_windows(name, x, transpose, scale_by=None, after=()):
    t, d = x.shape
    cg = d // 4
    cw = V7X_LANES
    per = cg // cw

    def body(*refs):
        x_ref, o_ref = refs[0], refs[-1]
        g = pl.program_id(0) // per
        xv = x_ref[...]
        row = lax.broadcasted_iota(jnp.int32, (t, 1), 0)
        cnt = jnp.minimum(row + 1, jnp.left_shift(2, g)).astype(F32)
        s = xv / cnt if transpose else xv
        parts = []
        for lg in POOL_WINDOW_LOG2:
            j = 1 << (lg - 1)
            s = s + (_shift_up(s, j, row) if transpose else _shift_down(s, j, row))
            parts.append(s)
        sel = _pool_select(parts, g)
        if transpose:
            o_ref[...] = ALPHA * refs[1][...] + sel - xv
        else:
            o_ref[...] = (sel / cnt - xv).astype(o_ref.dtype)

    col = pl.BlockSpec((t, cw), lambda j: (0, j))
    ins = [x] if scale_by is None else [x, scale_by]
    return pl.pallas_call(
        body, name=name, grid=(d // cw,), out_shape=SDS((t, d), F32 if transpose else BF16),
        in_specs=[col] * len(ins) + [ANY] * len(after), out_specs=col, compiler_params=_cp("parallel"),
    )(*ins, *after)


def _pool_mm(name, pooled, w, scale, x):
    t, d = x.shape
    cg = d // 4
    tm = _pick(t, 1024)

    def body(p_ref, w_ref, s_ref, x_ref, u_ref, h_ref):
        h = jnp.dot(p_ref[...], w_ref[...], preferred_element_type=F32)
        h_ref[...] = h
        u_ref[...] = ALPHA * x_ref[...] + h * s_ref[...]

    blk = pl.BlockSpec((tm, cg), lambda g, i: (i, g))
    return pl.pallas_call(
        body, name=name, grid=(4, t // tm), out_shape=[SDS((t, d), F32), SDS((t, d), F32)],
        in_specs=[blk, pl.BlockSpec((None, cg, cg), lambda g, i: (g, 0, 0)), pl.BlockSpec((1, cg), lambda g, i: (0, g)),
                  blk],
        out_specs=[blk, blk], compiler_params=_cp("parallel", "parallel"),
    )(pooled, w, scale, x)


def _pool_bwd_mm(name, du, hraw, w, scale):
    t, d = du.shape
    cg = d // 4
    tm = _pick(t, 1024)

    def body(du_ref, h_ref, w_ref, s_ref, dh_ref, dp_ref, ds_ref):
        @pl.when(pl.program_id(1) == 0)
        def _():
            ds_ref[...] = jnp.zeros_like(ds_ref)

        duv = du_ref[...]
        ds_ref[...] += jnp.sum(duv * h_ref[...], axis=0, keepdims=True)
        dh = (duv * s_ref[...]).astype(BF16)
        dh_ref[...] = dh
        dp_ref[...] = lax.dot_general(dh, w_ref[...], NT, preferred_element_type=F32)

    blk = pl.BlockSpec((tm, cg), lambda g, i: (i, g))
    vec = pl.BlockSpec((1, cg), lambda g, i: (0, g))
    return pl.pallas_call(
        body, name=name, grid=(4, t // tm), out_shape=[SDS((t, d), BF16), SDS((t, d), F32), SDS((1, d), F32)],
        in_specs=[blk, blk, pl.BlockSpec((None, cg, cg), lambda g, i: (g, 0, 0)), vec],
        out_specs=[blk, blk, vec], compiler_params=_cp("parallel", "arbitrary"),
    )(du, hraw, w, scale)


def _pool_dw(name, pooled, dh):
    t, d = pooled.shape
    cg = d // 4
    tk = _pick(t, 512)
    nk = t // tk

    def body(p_ref, dh_ref, o_ref, acc):
        k = pl.program_id(1)

        @pl.when(k == 0)
        def _():
            acc[...] = jnp.zeros_like(acc)

        acc[...] += lax.dot_general(p_ref[...], dh_ref[...], TN, preferred_element_type=F32)

        @pl.when(k == nk - 1)
        def _():
            o_ref[...] = acc[...].astype(o_ref.dtype)

    blk = pl.BlockSpec((tk, cg), lambda g, k: (k, g))
    return pl.pallas_call(
        body, name=name, grid=(4, nk), out_shape=SDS((4, cg, cg), BF16), in_specs=[blk, blk],
        out_specs=pl.BlockSpec((None, cg, cg), lambda g, k: (g, 0, 0)), scratch_shapes=[pltpu.VMEM((cg, cg), F32)],
        compiler_params=_cp("parallel", "arbitrary"),
    )(pooled, dh)


def _conv_pre(u, w_ref, b_ref, row):
    pre = b_ref[...] + _shift_down(u, 3, row) * w_ref[0:1, :]
    pre = pre + _shift_down(u, 2, row) * w_ref[1:2, :]
    pre = pre + _shift_down(u, 1, row) * w_ref[2:3, :]
    return pre + u * w_ref[3:4, :]


def _conv_fwd(name, zx, conv_w, conv_b, di):
    t = zx.shape[0]
    cd = conv_w.shape[1]
    cw = _pick(cd, 256)
    off = di // cw

    def body(u_ref, w_ref, b_ref, o_ref):
        row = lax.broadcasted_iota(jnp.int32, (t, 1), 0)
        pre = _conv_pre(u_ref[...], w_ref, b_ref, row)
        o_ref[...] = pre * _sigmoid(pre)

    return pl.pallas_call(
        body, name=name, grid=(cd // cw,), out_shape=SDS((t, cd), F32),
        in_specs=[pl.BlockSpec((t, cw), lambda j: (0, off + j)), pl.BlockSpec((CONV_WIDTH, cw), lambda j: (0, j)),
                  pl.BlockSpec((1, cw), lambda j: (0, j))],
        out_specs=pl.BlockSpec((t, cw), lambda j: (0, j)), compiler_params=_cp("parallel"),
    )(zx, conv_w, conv_b)


def _conv_bwd(name, zx, conv_w, conv_b, dact, di, first):
    t, cd = dact.shape
    cw = _pick(cd, 256)
    off, woff = (di + first) // cw, first // cw

    def body(u_ref, w_ref, b_ref, da_ref, du_ref, dw_ref, db_ref):
        row = lax.broadcasted_iota(jnp.int32, (t, 1), 0)
        u = u_ref[...]
        pre = _conv_pre(u, w_ref, b_ref, row)
        sg = _sigmoid(pre)
        dpre = da_ref[...] * (sg * (1.0 + pre * (1.0 - sg)))
        du = dpre * w_ref[3:4, :]
        for j in (1, 2, 3):
            du = du + _shift_up(dpre, j, row) * w_ref[3 - j:4 - j, :]
            dw_ref[3 - j:4 - j, :] = jnp.sum(dpre * _shift_down(u, j, row), axis=0, keepdims=True)
        dw_ref[3:4, :] = jnp.sum(dpre * u, axis=0, keepdims=True)
        db_ref[...] = jnp.sum(dpre, axis=0, keepdims=True)
        du_ref[...] = du.astype(du_ref.dtype)

    wspec = pl.BlockSpec((CONV_WIDTH, cw), lambda j: (0, j))
    bspec = pl.BlockSpec((1, cw), lambda j: (0, j))
    ospec = pl.BlockSpec((t, cw), lambda j: (0, j))
    return pl.pallas_call(
        body, name=name, grid=(cd // cw,), out_shape=[SDS((t, cd), BF16), SDS((CONV_WIDTH, cd), F32), SDS((1, cd), F32)],
        in_specs=[pl.BlockSpec((t, cw), lambda j: (0, off + j)), pl.BlockSpec((CONV_WIDTH, cw), lambda j: (0, woff + j)),
                  pl.BlockSpec((1, cw), lambda j: (0, woff + j)), ospec],
        out_specs=[ospec, wspec, bspec], compiler_params=_cp("parallel"),
    )(zx, conv_w, conv_b, dact)


def _expand_heads(name, arrays, h_n, p):
    t = arrays[0].shape[0]
    n = len(arrays)
    w = _pick(h_n * p, 512)

    def body(*refs):
        j = pl.program_id(0)
        head = lax.broadcasted_iota(jnp.int32, (V7X_LANES, w), 0)
        lane = lax.broadcasted_iota(jnp.int32, (V7X_LANES, w), 1)
        spread = (head == j * (w // p) + lane // p).astype(BF16)
        for a_ref, o_ref in zip(refs[:n], refs[n:]):
            rest = a_ref[...]
            out = jnp.zeros((t, w), F32)
            for _ in range(3):
                piece = rest.astype(BF16)
                out = out + jnp.dot(piece, spread, preferred_element_type=F32)
                rest = rest - piece.astype(F32)
            o_ref[...] = out

    full = pl.BlockSpec((t, V7X_LANES), lambda j: (0, 0))
    return pl.pallas_call(
        body, name=name, grid=(h_n * p // w,), out_shape=[SDS((t, h_n * p), F32)] * n, in_specs=[full] * n,
        out_specs=[pl.BlockSpec((t, w), lambda j: (0, j))] * n, compiler_params=_cp("parallel"),
    )(*arrays)


def _softplus(v):
    return jnp.maximum(v, 0.0) + jnp.log(1.0 + jnp.exp(-jnp.abs(v)))


def _dt_fwd(name, zx, bias, a_log, col_block):
    t = zx.shape[0]

    def body(r_ref, b_ref, al_ref, dt_ref, acs_ref, ein_ref, eout_ref):
        row = lax.broadcasted_iota(jnp.int32, (t, 1), 0) % CHUNK
        dt = _softplus(r_ref[...] + b_ref[...])
        da = dt * (-jnp.exp(al_ref[...]))
        s, r = da, da
        j = 1
        while j < CHUNK:
            s = s + jnp.where(row >= j, pltpu.roll(s, j, 0), 0.0)
            r = r + jnp.where(row < CHUNK - j, pltpu.roll(r, t - j, 0), 0.0)
            j *= 2
        dt_ref[...] = dt
        acs_ref[...] = s
        ein_ref[...] = jnp.exp(s)
        eout_ref[...] = jnp.exp(r - da)

    vec = pl.BlockSpec((1, V7X_LANES), lambda i: (0, 0))
    full = pl.BlockSpec((t, V7X_LANES), lambda i: (0, 0))
    return pl.pallas_call(
        body, name=name, grid=(1,), out_shape=[SDS((t, V7X_LANES), F32)] * 4,
        in_specs=[pl.BlockSpec((t, V7X_LANES), lambda i: (0, col_block)), vec, vec], out_specs=[full] * 4,
        compiler_params=_cp("arbitrary"),
    )(zx, bias, a_log)


def _dt_bwd(name, zx, bias, a_log, d_acs, d_dt, col_block):
    t = zx.shape[0]

    def body(r_ref, b_ref, al_ref, da_ref, dd_ref, draw_ref, db_ref, dal_ref):
        row = lax.broadcasted_iota(jnp.int32, (t, 1), 0) % CHUNK
        pre = r_ref[...] + b_ref[...]
        dt = _softplus(pre)
        a = -jnp.exp(al_ref[...])
        s = da_ref[...]
        j = 1
        while j < CHUNK:
            s = s + jnp.where(row < CHUNK - j, pltpu.roll(s, t - j, 0), 0.0)
            j *= 2
        ddt = dd_ref[...] + s * a
        dal_ref[...] = jnp.sum(s * dt, axis=0, keepdims=True) * a
        draw = ddt * _sigmoid(pre)
        db_ref[...] = jnp.sum(draw, axis=0, keepdims=True)
        draw_ref[...] = draw.astype(draw_ref.dtype)

    vec = pl.BlockSpec((1, V7X_LANES), lambda i: (0, 0))
    full = pl.BlockSpec((t, V7X_LANES), lambda i: (0, 0))
    return pl.pallas_call(
        body, name=name, grid=(1,), out_shape=[SDS((t, V7X_LANES), BF16), SDS((1, V7X_LANES), F32), SDS((1, V7X_LANES), F32)],
        in_specs=[pl.BlockSpec((t, V7X_LANES), lambda i: (0, col_block)), vec, vec, full, full],
        out_specs=[full, vec, vec], compiler_params=_cp("arbitrary"),
    )(zx, bias, a_log, d_acs, d_dt)


def _ssd_specs(t, di, g_n, hpg, p, rev):
    rows = SSD_STEP_CHUNKS * CHUNK
    nc = t // rows
    w = hpg * p
    nb = di // D_STATE

    def cc(c):
        return nc - 1 - c if rev else c

    return dict(
        xs=pl.BlockSpec((rows, w), lambda g, c: (cc(c), g)),
        bm=pl.BlockSpec((rows, D_STATE), lambda g, c: (cc(c), nb + g)),
        cm=pl.BlockSpec((rows, D_STATE), lambda g, c: (cc(c), nb + g_n + g)),
        col=pl.BlockSpec((None, rows, hpg), lambda g, c: (g, cc(c), 0)),
        rowv=pl.BlockSpec((None, hpg, rows), lambda g, c: (g, 0, cc(c))),
        head=pl.BlockSpec((None, 1, hpg), lambda g, c: (g, 0, 0)),
        lanes=pl.BlockSpec((1, w), lambda g, c: (0, g)),
        bc=pl.BlockSpec((rows, D_STATE), lambda g, c: (cc(c), g)),
        prev=pl.BlockSpec((SSD_STEP_CHUNKS, None, D_STATE, w), lambda g, c: (cc(c), g, 0, 0)),
        seg=pl.BlockSpec((w, V7X_LANES), lambda g, c: (0, 0)),
    )


def _decay_masks(cb, ac, ar, heads):
    li = lax.broadcasted_iota(jnp.int32, (CHUNK, CHUNK), 0)
    si = lax.broadcasted_iota(jnp.int32, (CHUNK, CHUNK), 1)
    lms = [jnp.exp(jnp.where(li >= si, ac[:, hh:hh + 1] - ar[hh:hh + 1, :], -jnp.inf)) for hh in heads]
    return lms, [(cb * lm).astype(BF16) for lm in lms]


def _ssd_fwd(name, xbc, dt_x, ein_x, eout_x, a_col, a_row, d_x, di, g_n, hpg, p):
    t = xbc.shape[0]
    nc = t // CHUNK
    w = hpg * p
    assert 2 * p == V7X_LANES and hpg % 2 == 0 and nc % SSD_STEP_CHUNKS == 0
    sp = _ssd_specs(t, di, g_n, hpg, p, False)

    def body(xs_ref, bm_ref, cm_ref, dt_ref, ein_ref, eout_ref, ac_ref, ar_ref, d_ref, y_ref, prev_ref, h_ref):
        @pl.when(pl.program_id(1) == 0)
        def _():
            h_ref[...] = jnp.zeros_like(h_ref)

        first = lax.broadcasted_iota(jnp.int32, (1, V7X_LANES), 1) < p
        for sub in range(SSD_STEP_CHUNKS):
            r = slice(sub * CHUNK, (sub + 1) * CHUNK)
            bm = bm_ref[r, :].astype(BF16)
            cm = cm_ref[r, :].astype(BF16)
            cb = lax.dot_general(cm, bm, NT, preferred_element_type=F32)
            xs = xs_ref[r, :]
            e_in = ein_ref[r, :]
            xdt = xs * dt_ref[r, :]
            ac, ar = ac_ref[r, :], ar_ref[:, r]
            ys = []
            for pr in range(hpg // 2):
                _, ms = _decay_masks(cb, ac, ar, (2 * pr, 2 * pr + 1))
                xp = xdt[:, pr * V7X_LANES:(pr + 1) * V7X_LANES]
                rhs = jnp.concatenate([jnp.where(first, xp, 0.0), jnp.where(first, 0.0, xp)], axis=0).astype(BF16)
                ys.append(jnp.dot(jnp.concatenate(ms, axis=1), rhs, preferred_element_type=F32))
            h_prev = h_ref[...]
            prev_ref[sub] = h_prev
            y = jnp.concatenate(ys, axis=1) + jnp.dot(cm, h_prev.astype(BF16), preferred_element_type=F32) * e_in
            y_ref[r, :] = y + xs * d_ref[...]
            st = lax.dot_general(bm, (xdt * eout_ref[r, :]).astype(BF16), TN, preferred_element_type=F32)
            h_ref[...] = e_in[CHUNK - 1:CHUNK, :] * h_prev + st

    return pl.pallas_call(
        body, name=name, grid=(g_n, nc // SSD_STEP_CHUNKS),
        out_shape=[SDS((t, di), F32), SDS((nc, g_n, D_STATE, w), F32)],
        in_specs=[sp["xs"], sp["bm"], sp["cm"], sp["xs"], sp["xs"], sp["xs"], sp["col"], sp["rowv"], sp["lanes"]],
        out_specs=[sp["xs"], sp["prev"]], scratch_shapes=[pltpu.VMEM((D_STATE, w), F32)],
        compiler_params=_cp("parallel", "arbitrary"),
    )(xbc, xbc, xbc, dt_x, ein_x, eout_x, a_col, a_row, d_x)


def _head_sums(v, seg):
    hi = v.astype(BF16)
    lo = (v - hi.astype(F32)).astype(BF16)
    return jnp.dot(hi, seg, preferred_element_type=F32) + jnp.dot(lo, seg, preferred_element_type=F32)


def _head_totals(v, seg):
    part = v[0:8]
    for r in range(8, v.shape[0], 8):
        part = part + v[r:r + 8]
    return jnp.sum(_head_sums(part, seg), axis=0, keepdims=True)


def _ssd_bwd(name, xbc, dt_x, ein_x, eout_x, a_col, a_row, d_x, prev, dy, di, g_n, hpg, p):
    t = xbc.shape[0]
    nc = t // CHUNK
    w = hpg * p
    sp = _ssd_specs(t, di, g_n, hpg, p, True)
    seg = (lax.broadcasted_iota(jnp.int32, (w, V7X_LANES), 0) // p
           == lax.broadcasted_iota(jnp.int32, (w, V7X_LANES), 1)).astype(BF16)

    def body(xs_ref, bm_ref, cm_ref, dt_ref, ein_ref, eout_ref, ac_ref, ar_ref, d_ref, prev_ref, dy_ref,
             seg_ref, dx_ref, dbm_ref, dcm_ref, ddt_ref, dacs_ref, dd_ref, dh_ref):
        @pl.when(pl.program_id(1) == 0)
        def _():
            dh_ref[...] = jnp.zeros_like(dh_ref)
            dd_ref[...] = jnp.zeros_like(dd_ref)

        first = lax.broadcasted_iota(jnp.int32, (1, V7X_LANES), 1) < p
        last_row = lax.broadcasted_iota(jnp.int32, (CHUNK, 1), 0) == CHUNK - 1
        seg_m = seg_ref[...]
        d_skip = d_ref[...]
        for sub in reversed(range(SSD_STEP_CHUNKS)):
            r = slice(sub * CHUNK, (sub + 1) * CHUNK)
            bm = bm_ref[r, :].astype(BF16)
            cm = cm_ref[r, :].astype(BF16)
            cb = lax.dot_general(cm, bm, NT, preferred_element_type=F32)
            xs, dy, e_in, e_out, dt_l = xs_ref[r, :], dy_ref[r, :], ein_ref[r, :], eout_ref[r, :], dt_ref[r, :]
            ac, ar = ac_ref[r, :], ar_ref[:, r]
            xdt = xs * dt_l
            h_prev = prev_ref[sub]
            h_prev_b = h_prev.astype(BF16)
            dh_next = dh_ref[...]
            dh_next_b = dh_next.astype(BF16)
            dy_e = (dy * e_in).astype(BF16)
            d_cm = lax.dot_general(dy_e, h_prev_b, NT, preferred_element_type=F32)
            dh_ref[...] = (e_in[CHUNK - 1:CHUNK, :] * dh_next
                           + lax.dot_general(cm, dy_e, TN, preferred_element_type=F32))
            q = jnp.dot(bm, dh_next_b, preferred_element_type=F32)
            xf = xdt * e_out
            d_bm = lax.dot_general(xf.astype(BF16), dh_next_b, NT, preferred_element_type=F32)
            d_cb = jnp.zeros((CHUNK, CHUNK), F32)
            parts, w_parts = [], []
            for pr in range(hpg // 2):
                lanes = slice(pr * V7X_LANES, (pr + 1) * V7X_LANES)
                lms, ms = _decay_masks(cb, ac, ar, (2 * pr, 2 * pr + 1))
                xp = xdt[:, lanes]
                xp_b = xp.astype(BF16)
                dyp = dy[:, lanes]
                halves = [jnp.where(first, dyp, 0.0).astype(BF16), jnp.where(first, 0.0, dyp).astype(BF16)]
                for lm, half in zip(lms, halves):
                    d_cb = d_cb + lax.dot_general(half, xp_b, NT, preferred_element_type=F32) * lm
                dxd = lax.dot_general(jnp.concatenate(ms, axis=0), jnp.concatenate(halves, axis=0), TN,
                                      preferred_element_type=F32)
                stacked = jnp.concatenate([jnp.where(first, xp, 0.0), jnp.where(first, 0.0, xp)], axis=0).astype(BF16)
                y_diag = jnp.dot(jnp.concatenate(ms, axis=1), stacked, preferred_element_type=F32)
                parts.append(dxd)
                w_parts.append(dyp.astype(BF16).astype(F32) * y_diag - xp_b.astype(F32) * dxd)
            d_xdt = jnp.concatenate(parts, axis=1) + q * e_out
            dx_ref[r, :] = d_xdt * dt_l + dy * d_skip
            ch = jnp.dot(cm, h_prev_b, preferred_element_type=F32)
            qx = q * xf
            s_a = _head_sums(dy * ch * e_in - qx + jnp.concatenate(w_parts, axis=1), seg_m)[:, :hpg]
            d_last = (_head_totals(qx, seg_m)[:, :hpg]
                      + jnp.exp(ac[CHUNK - 1:CHUNK, :]) * _head_totals(dh_next * h_prev, seg_m)[:, :hpg])
            ddt_ref[r, :] = _head_sums(d_xdt * xs, seg_m)[:, :hpg]
            dacs_ref[r, :] = s_a + jnp.where(last_row, d_last, 0.0)
            dd_ref[...] += _head_totals(dy * xs, seg_m)[:, :hpg]
            d_cb_b = d_cb.astype(BF16)
            dcm_ref[r, :] = d_cm + jnp.dot(d_cb_b, bm, preferred_element_type=F32)
            dbm_ref[r, :] = d_bm + lax.dot_general(d_cb_b, cm, TN, preferred_element_type=F32)

    gn = g_n * D_STATE
    return pl.pallas_call(
        body, name=name, grid=(g_n, nc // SSD_STEP_CHUNKS),
        out_shape=[SDS((t, di), F32), SDS((t, gn), F32), SDS((t, gn), F32), SDS((g_n, t, hpg), F32),
                   SDS((g_n, t, hpg), F32), SDS((g_n, 1, hpg), F32)],
        in_specs=[sp["xs"], sp["bm"], sp["cm"], sp["xs"], sp["xs"], sp["xs"], sp["col"], sp["rowv"],
                  sp["lanes"], sp["prev"], sp["xs"], sp["seg"]],
        out_specs=[sp["xs"], sp["bc"], sp["bc"], sp["col"], sp["col"], sp["head"]],
        scratch_shapes=[pltpu.VMEM((D_STATE, w), F32)],
        compiler_params=_cp("parallel", "arbitrary"),
    )(xbc, xbc, xbc, dt_x, ein_x, eout_x, a_col, a_row, d_x, prev, dy, seg)


def _as3d(a):
    return a.reshape(a.shape[0], -1, a.shape[-1])


def _pair_sum(name, own, recv, core):
    shape = recv.shape
    cols = shape[-1]
    own3, recv3 = own.reshape(8, -1, cols), recv.reshape(4, -1, cols)
    rows = recv3.shape[1]
    tr = _row_tile(rows, cols, 2)

    def body(c_ref, a_ref, b_ref, o_ref):
        o_ref[...] = (a_ref[...].astype(F32) + b_ref[...].astype(F32)).astype(o_ref.dtype)

    blk = pl.BlockSpec((None, tr, cols), lambda q, i, c_ref: (q, i, 0))
    out = pl.pallas_call(
        body, name=name, out_shape=SDS(recv3.shape, recv.dtype),
        grid_spec=pltpu.PrefetchScalarGridSpec(
            num_scalar_prefetch=1, grid=(4, rows // tr),
            in_specs=[pl.BlockSpec((None, tr, cols), lambda q, i, c_ref: (2 * q + c_ref[0], i, 0)), blk], out_specs=blk),
        compiler_params=_cp("parallel", "parallel"),
    )(core, own3, recv3)
    return out.reshape(shape)


def _adamw(name, w, m, v, parts, layer, prev=None, sel=None):
    lyr, rows, cols = w.shape
    n = len(parts)
    by_rows = rows % 16 == 0
    tr, tc = (_row_tile(rows, cols), cols) if by_rows else (rows, _pick(cols, 256))
    np_ = 0 if prev is None else 4
    if sel is None:
        sel = jnp.zeros((1,), jnp.int32)

    def body(sel_ref, *refs):
        w_ref, m_ref, v_ref = refs[:3]
        p_refs = refs[3:3 + n]
        g_ref, d_ref, nm_ref, nv_ref = refs[3 + n + np_:]
        g = p_refs[0][...].astype(F32)
        for r in p_refs[1:]:
            g = g + r[...].astype(F32)
        nm = ADAM_B1 * m_ref[...] + (1.0 - ADAM_B1) * g
        nv = ADAM_B2 * v_ref[...] + (1.0 - ADAM_B2) * (g * g)
        m_hat = nm / (1.0 - ADAM_B1 ** ADAM_STEP)
        v_hat = nv / (1.0 - ADAM_B2 ** ADAM_STEP)
        g_ref[...] = g
        d_ref[...] = -ADAM_LR * (m_hat / (jnp.sqrt(v_hat) + ADAM_EPS) + ADAM_WD * w_ref[...])
        nm_ref[...] = nm
        nv_ref[...] = nv

    def at(lead):
        return pl.BlockSpec((None, tr, tc), lambda i, s: (lead(s), i, 0) if by_rows else (lead(s), 0, i))

    lspec = at(lambda s: layer)
    pspecs = [at(lambda s: s[0]) if q is None else at(lambda s, q=q: q) for _, q in parts]
    aliases = {} if prev is None else {4 + n + q: q for q in range(4)}
    return pl.pallas_call(
        body, name=name, out_shape=[SDS(w.shape, F32)] * 4,
        grid_spec=pltpu.PrefetchScalarGridSpec(
            num_scalar_prefetch=1, grid=(rows // tr if by_rows else cols // tc,),
            in_specs=[lspec] * 3 + pspecs + [ANY] * np_, out_specs=[lspec] * 4),
        input_output_aliases=aliases, compiler_params=_cp("parallel"),
    )(sel, w, m, v, *[arr for arr, _ in parts], *(prev or ()))


def _sum8(name, parts):
    rows = parts.shape[1]

    def body(p_ref, o_ref):
        s = p_ref[0]
        for q in range(1, N_DEV):
            s = s + p_ref[q]
        o_ref[...] = s

    return pl.pallas_call(
        body, name=name, grid=(1,), out_shape=SDS((rows, V7X_LANES), F32),
        in_specs=[pl.BlockSpec((N_DEV, rows, V7X_LANES), lambda i: (0, 0, 0))],
        out_specs=pl.BlockSpec((rows, V7X_LANES), lambda i: (0, 0)), compiler_params=_cp("arbitrary"),
    )(parts)


def _pack(vectors, align):
    flat = jnp.concatenate([v.reshape(-1) for v in vectors])
    pad = (-flat.shape[0]) % align
    if pad:
        flat = jnp.concatenate([flat, jnp.zeros((pad,), F32)])
    return flat.reshape(-1, V7X_LANES)


def _unpack(packed, shapes):
    flat = packed.reshape(-1)
    out, o = [], 0
    for s in shapes:
        size = 1
        for dim in s:
            size *= dim
        out.append(flat[o:o + size].reshape(s))
        o += size
    return out


def _residual_epilogue(acc, res):
    return (ALPHA * res + acc,)


def _plain_add_epilogue(acc, res):
    return (res + acc,)


def _gate_epilogue(acc, y, e):
    gate = _sigmoid(acc)
    xn = y + gate * e
    return xn, gate, xn


def _relu2(pre):
    r = jnp.maximum(pre, 0.0)
    return r * r


def _relu2_bwd_epilogue(acc, pre):
    return (acc * (2.0 * jnp.maximum(pre.astype(F32), 0.0)),)


def _tail_fwd(tag, u_a, wts, lng, lnb, p_l, finish_w1, finish_w2, after=()):
    y1, y1_b = _ln_fwd(f"ln1_{tag}", u_a, lng[0], lnb[0], after)
    finish_w1(y1_b)
    (pre,) = _mm_fwd(f"mlp1_{tag}", y1_b, wts["w1"], "col", [BF16])
    finish_w2(pre)
    (u_b,) = _mm_fwd(f"mlp2_{tag}", pre, wts["w2"], "row", [F32], _residual_epilogue, (y1,), a_fn=_relu2)
    y2, y2_b = _ln_fwd(f"ln2_{tag}", u_b, lng[1], lnb[1])
    (e,) = _mm_fwd(f"ple_{tag}", p_l, wts["plew"], "col", [F32])
    xn, gate, xn_b = _mm_fwd(f"gate_{tag}", y2_b, wts["gate"], "row", [F32, F32, BF16], _gate_epilogue, (y2, e))
    return xn, xn_b, (u_a, y1_b, pre, u_b, y2_b, e, gate)


def _tail_bwd(tag, dxn, saved, wts, lng, p_l, emit, advance, toks):
    u_a, y1_b, pre, u_b, y2_b, e, gate = saved
    dgpre, de = _ple_bwd(f"ple_bwd_{tag}", dxn, e, gate)
    toks = emit(f"{tag}_ple", dict(gate=_mm_dw(f"gate_dw_{tag}", y2_b, dgpre, "row", after=toks),
                                   plew=_mm_dw(f"ple_dw_{tag}", p_l, de, "col")))
    (dy2,) = _mm_dx(f"gate_dx_{tag}", dgpre, wts["gate"], "row", [F32], _plain_add_epilogue, (dxn,), after=toks)
    toks = advance((dy2,))
    du_b, du_b16, dg2, db2 = _ln_bwd(f"ln2_bwd_{tag}", u_b, dy2, lng[1])
    toks = emit(f"{tag}_w2", dict(w2=_mm_dw(f"mlp2_dw_{tag}", pre, du_b16, "row", a_fn=_relu2, after=toks)))
    (dpre,) = _mm_dx(f"mlp2_dx_{tag}", du_b16, wts["w2"], "row", [BF16], _relu2_bwd_epilogue, (pre,), after=toks)
    toks = advance((dpre,))
    toks = emit(f"{tag}_w1", dict(w1=_mm_dw(f"mlp1_dw_{tag}", y1_b, dpre, "col", after=toks)))
    (dy1,) = _mm_dx(f"mlp1_dx_{tag}", dpre, wts["w1"], "col", [F32], _residual_epilogue, (du_b,), after=toks)
    toks = advance((dy1,))
    du_a, du_a16, dg1, db1 = _ln_bwd(f"ln1_bwd_{tag}", u_a, dy1, lng[0])
    return du_a, du_a16, [dg1, dg2], [db1, db2], toks


def _to_slots(a, axis):
    shape = a.shape
    per = shape[axis] // N_DEV
    v = a.reshape(shape[:axis] + (N_DEV, per) + shape[axis + 1:])
    return jnp.moveaxis(v, axis, 0)


def _pad_lanes(a):
    return jnp.pad(a, [(0, 0)] * (a.ndim - 1) + [(0, V7X_LANES - a.shape[-1])])


def kernel(x, p, pool_w, pool_scale, ssm_in_w, ssm_conv_w, ssm_conv_b, ssm_dt_bias, ssm_a_log, ssm_d, ssm_norm_w, ssm_out_w, mlp_w1, mlp_w2, ln_g, ln_b, ple_w, ple_gate_w, loss_target, m_pool_w, m_pool_scale, m_ssm_in_w, m_ssm_conv_w, m_ssm_conv_b, m_ssm_dt_bias, m_ssm_a_log, m_ssm_d, m_ssm_norm_w, m_ssm_out_w, m_mlp_w1, m_mlp_w2, m_ln_g, m_ln_b, m_ple_w, m_ple_gate_w, v_pool_w, v_pool_scale, v_ssm_in_w, v_ssm_conv_w, v_ssm_conv_b, v_ssm_dt_bias, v_ssm_a_log, v_ssm_d, v_ssm_norm_w, v_ssm_out_w, v_mlp_w1, v_mlp_w2, v_ln_g, v_ln_b, v_ple_w, v_ple_gate_w):
    t, d = x.shape[1:]
    h_n = ssm_dt_bias.shape[-1]
    di_s, cd_s, dp_s, d_s = ssm_norm_w.shape[-1], ssm_conv_b.shape[-1], ssm_in_w.shape[-1], ln_g.shape[-1]
    di, cd, dp = N_DEV * di_s, N_DEV * cd_s, N_DEV * dp_s
    p_dim = di // h_n
    g_n = (cd - di) // (2 * D_STATE)
    hpg = h_n // g_n
    zw = di + cd
    assert h_n <= V7X_LANES and dp == zw + h_n and zw % V7X_LANES == 0 and zw % h_n == 0
    cg = d // 4
    me = 4 * lax.axis_index("x") + 2 * lax.axis_index("y") + lax.axis_index("c")

    x0, target = x[0], loss_target[0]
    p_l = [p[0, 0].astype(BF16), p[1, 0].astype(BF16)]

    small_shapes = [(CONV_WIDTH, cd_s), (1, cd_s), (1, di_s), (2, 2, d_s), (2, 2, d_s)]
    small = _pack([ssm_conv_w[0], ssm_conv_b, ssm_norm_w, ln_g, ln_b], 8 * V7X_LANES)
    first = [w.astype(BF16) for w in (pool_w[0], mlp_w1[0])]

    def gather_start(tag, own, after):
        lands = [lax.empty((N_DEV,) + w.shape, w.dtype) for w in own]
        return _async_start(f"ag_{tag}_start", _ag_first_copies, (4 * len(own),), own, lands, after)

    def gather_pass(tag, handle, after):
        n = len(handle[2]) // 2
        own, lands = _async_wait(f"ag_{tag}_wait", _ag_first_copies, handle, n, after)
        return _async_start(f"ag_{tag}_forward_start", _ag_forward_copies, (3 * n,), [], lands), own

    def gather_done(tag, passed, after=()):
        fwd, own = passed
        _, lands = _async_wait(f"ag_{tag}_forward_wait", _ag_forward_copies, fwd, 0, after)
        return [lax.dynamic_update_slice_in_dim(g, w[None], me, 0) for g, w in zip(lands, own)]

    def gather_finish(tag, handle, after):
        return gather_done(tag, gather_pass(tag, handle, after))

    ag_first = gather_start("first", first[:1] + [small], ())
    ag_w1 = gather_start("w1", first[1:], (ag_first[3],))
    zero = ag_w1[3][0, 0]
    own_l0 = [(w + zero).astype(BF16) for w in (mlp_w2[0], ple_w[0], ple_gate_w[0])]
    own_ssm = [(ssm_in_w[0] + zero).astype(BF16).T]
    own_out = [(ssm_out_w[0] + zero).astype(BF16)]
    own_mlp = [(w + zero).astype(BF16) for w in (mlp_w1[1], mlp_w2[1], ple_w[1], ple_gate_w[1])]
    ag_l0 = gather_start("l0", own_l0, (ag_w1[3],))
    ag_ssm = gather_start("ssm", own_ssm, (ag_l0[3],))
    ag_out = gather_start("out", own_out, (ag_ssm[3],))
    ag_mlp = gather_start("mlp1", own_mlp, (ag_out[3],))
    pooled = _pool_windows("pool_fwd", x0, False, after=(ag_mlp[3],))
    pool_g, small_g = gather_finish("first", ag_first, (pooled,))
    pool_full = pool_g.transpose(1, 0, 2, 3).reshape(4, cg, cg)
    sm = small_g.reshape(N_DEV, -1)
    o = 0
    parts = []
    for shp in small_shapes:
        size = 1
        for s in shp:
            size *= s
        parts.append(sm[:, o:o + size].reshape((N_DEV,) + shp))
        o += size
    conv_w_full = parts[0].transpose(1, 0, 2).reshape(CONV_WIDTH, cd)
    conv_b_full = parts[1].transpose(1, 0, 2).reshape(1, cd)
    norm_w_full = parts[2].transpose(1, 0, 2).reshape(1, di)
    ln_g_full = parts[3].transpose(1, 2, 0, 3).reshape(2, 2, 1, d)
    ln_b_full = parts[4].transpose(1, 2, 0, 3).reshape(2, 2, 1, d)
    bias128, alog128 = _pad_lanes(ssm_dt_bias), _pad_lanes(ssm_a_log)

    u0, hraw = _pool_mm("pool_mm", pooled, pool_full, pool_scale, x0)
    wts = [{}]

    def finish_w1_l0(after):
        wts[0].update(w1=gather_finish("w1", ag_w1, (after,))[0])

    def finish_l0(after):
        w2_0, plew_0, gate_0 = gather_finish("l0", ag_l0, (after,))
        wts[0].update(w2=w2_0, plew=plew_0, gate=gate_0)

    x1, x1_b, saved0 = _tail_fwd("l0", u0, wts[0], ln_g_full[0], ln_b_full[0], p_l[0], finish_w1_l0, finish_l0)

    (in_g,) = gather_finish("ssm", ag_ssm, (x1,))
    in_t = in_g.reshape(dp, d)
    (zx,) = _mm_dx("in_proj", x1_b, in_t, "plain", [F32], k_rows=zw)
    dt_raw = _pad_lanes(_in_proj_dt("in_proj_dt", x1_b, in_t, zw, h_n))
    xbc = _conv_fwd("conv_fwd", zx, conv_w_full, conv_b_full, di)
    dt, acs, e_in, e_out = _dt_fwd("dt_fwd", dt_raw, bias128, alog128, 0)

    dt_x, ein_x, eout_x = _expand_heads("expand_heads", [dt, e_in, e_out], h_n, p_dim)
    d_x = jnp.repeat(ssm_d, p_dim, axis=1)

    def to_col(a):
        return a[:, :h_n].reshape(t, g_n, hpg).transpose(1, 0, 2)

    def to_row(a):
        return a[:, :h_n].reshape(t, g_n, hpg).transpose(1, 2, 0)

    def from_col(a):
        return _pad_lanes(a.transpose(1, 0, 2).reshape(t, h_n))

    a_col, a_row = to_col(acs), to_row(acs)
    y_ssd, prev = _ssd_fwd("ssd_fwd", xbc, dt_x, ein_x, eout_x, a_col, a_row, d_x, di, g_n, hpg, p_dim)
    out_passed = gather_pass("out", ag_out, (y_ssd,))
    yn = _gated_rms_fwd("gated_rms_fwd", y_ssd, zx, norm_w_full, after=(out_passed[0][3],))
    (out_g,) = gather_done("out", out_passed, (yn,))
    (u2,) = _mm_fwd("out_proj", yn, out_g, "row", [F32], _residual_epilogue, (x1,))
    mlp_passed = gather_pass("mlp1", ag_mlp, (u2,))
    wts.append({})

    def finish_l1(after):
        wts[1].update(zip(("w1", "w2", "plew", "gate"), gather_done("mlp1", mlp_passed, (after,))))

    x2, _, saved1 = _tail_fwd("l1", u2, wts[1], ln_g_full[1], ln_b_full[1], p_l[1], finish_l1, lambda after: None,
                              after=(mlp_passed[0][3],))

    core = lax.axis_index("c").astype(jnp.int32).reshape(1)
    chip = (2 * lax.axis_index("x") + lax.axis_index("y")).astype(jnp.int32).reshape(1)
    scattering = {}
    pending = []

    def to_chips(after):
        if not pending:
            return []
        tag, names, handle = pending.pop()
        own, halves = _async_wait(f"rs_{tag}_sibling_wait", _rs_sibling_copies, handle, len(names), after)
        sums = [_pair_sum(f"rs_{tag}_pair_sum_{n}", g, hv, core) for n, g, hv in zip(names, own, halves)]
        lands = [lax.empty((3,) + s.shape[1:], s.dtype) for s in sums]
        handle = _async_start(f"rs_{tag}_start", _rs_chip_copies, (3 * len(sums),), sums, lands)
        scattering[tag] = (names, handle)
        return [handle[3]]

    def emit(tag, grads):
        names, arrays = list(grads), list(grads.values())
        toks = to_chips(tuple(arrays))
        lands = [lax.empty((4,) + g.shape[1:], g.dtype) for g in arrays]
        handle = _async_start(f"rs_{tag}_sibling_start", _rs_sibling_copies, (4 * len(arrays),), arrays, lands,
                              tuple(toks))
        pending.append((tag, names, handle))
        return toks + [handle[3]]

    def collect(tag, after):
        names, handle = scattering.pop(tag)
        sums, thirds = _async_wait(f"rs_{tag}_wait", _rs_chip_copies, handle, len(names), after)
        return {n: (s.reshape(4, -1, s.shape[-1]), r.reshape(3, -1, r.shape[-1])) for n, s, r in zip(names, sums, thirds)}

    dx2, loss_cols = _loss_bwd("loss", x2, target)
    du2, du2_b, dg_1, db_1, toks = _tail_bwd("l1", dx2, saved1, wts[1], ln_g_full[1], p_l[1], emit, to_chips, [])
    toks = emit("ssm_out", dict(out=_mm_dw("out_proj_dw", yn, du2_b, "row", after=toks)))
    (dyn,) = _mm_dx("out_proj_dx", du2_b, out_g, "row", [F32], after=toks)
    toks = to_chips((dyn,))
    dy_ssd, dz, d_norm_w = _gated_rms_bwd("gated_rms_bwd", y_ssd, zx, norm_w_full, dyn)
    dxs, dbm, dcm, ddt_x, dacs, dd = _ssd_bwd("ssd_bwd", xbc, dt_x, ein_x, eout_x, a_col, a_row, d_x, prev, dy_ssd,
                                              di, g_n, hpg, p_dim)
    draw, d_bias, d_alog = _dt_bwd("dt_bwd", dt_raw, bias128, alog128, from_col(dacs), from_col(ddt_x), 0)
    conv_parts = [_conv_bwd(f"conv_bwd_{tag}", zx, conv_w_full, conv_b_full, dact, di, first)
                  for tag, dact, first in (("xs", dxs, 0), ("b", dbm, di), ("c", dcm, di + g_n * D_STATE))]
    d_conv_w = jnp.concatenate([c[1] for c in conv_parts], axis=1)
    d_conv_b = jnp.concatenate([c[2] for c in conv_parts], axis=1)
    dzx = jnp.concatenate([dz] + [c[0] for c in conv_parts], axis=1)
    d_dt = draw[:, :h_n]
    g_in_t = _in_proj_dw_t("in_proj_dw", dzx, d_dt, x1_b, after=toks)
    toks = emit("ssm_in", {"in": g_in_t.reshape(N_DEV, dp_s, d)})
    dx_dt = _in_proj_dt_dx("in_proj_dt_dx", d_dt, in_t, zw, du2)
    (dx1,) = _mm_fwd("in_proj_dx", dzx, in_t, "plain", [F32], _plain_add_epilogue, (dx_dt,), after=toks, k_rows=zw)
    toks = to_chips((dx1,))

    du0, _, dg_0, db_0, toks = _tail_bwd("l0", dx1, saved0, wts[0], ln_g_full[0], p_l[0], emit, to_chips, toks)
    dh, dpool, d_scale = _pool_bwd_mm("pool_bwd_mm", du0, hraw, pool_full, pool_scale)
    toks += emit("pool", dict(pool=_to_slots(_pool_dw("pool_dw", pooled, dh), 1)))
    grad_x = _pool_windows("pool_bwd", dpool, True, du0, after=tuple(toks))
    toks = to_chips((grad_x,))

    def update(tag, w, m, v, parts, layer, prev=None):
        own, recv = parts
        return _adamw(f"adamw_{tag}_{layer}", _as3d(w), _as3d(m), _as3d(v),
                      [(own, None), (recv, 0), (recv, 1), (recv, 2)], layer, prev, chip)

    q = collect("l1_ple", (grad_x, *toks))
    r_gate = update("ple_gate_w", ple_gate_w, m_ple_gate_w, v_ple_gate_w, q["gate"], 1)
    r_plew = update("ple_w", ple_w, m_ple_w, v_ple_w, q["plew"], 1)
    r_w2 = update("mlp_w2", mlp_w2, m_mlp_w2, v_mlp_w2, collect("l1_w2", (r_gate[0],))["w2"], 1)
    r_w1 = update("mlp_w1", mlp_w1, m_mlp_w1, v_mlp_w1, collect("l1_w1", (r_w2[0],))["w1"], 1)
    r_out = update("ssm_out_w", ssm_out_w, m_ssm_out_w, v_ssm_out_w, collect("ssm_out", (r_w1[0],))["out"], 0)
    in_wt, in_mt, in_vt = [jnp.swapaxes(a, 1, 2) for a in (ssm_in_w, m_ssm_in_w, v_ssm_in_w)]
    r_in = update("ssm_in_w", in_wt, in_mt, in_vt, collect("ssm_in", (r_out[0],))["in"], 0)

    d_ln_g = jnp.stack([jnp.stack(dg_0), jnp.stack(dg_1)]).reshape(2, 2, d)
    d_ln_b = jnp.stack([jnp.stack(db_0), jnp.stack(db_1)]).reshape(2, 2, d)
    partial_shapes = [(CONV_WIDTH, cd), (1, cd), (1, di), (2, 2, d), (2, 2, d), (1, d), (1, h_n), (1, h_n), (1, h_n),
                      (1, d)]
    partial = _pack([d_conv_w, d_conv_b, d_norm_w, d_ln_g, d_ln_b, d_scale, d_bias[:, :h_n], d_alog[:, :h_n],
                     dd.reshape(1, h_n), loss_cols], 8 * V7X_LANES)
    (all_partials,) = _all_gather("ag_small_grads", [partial], after=(r_in[0],))
    tot = _unpack(_sum8("sum_small_grads", all_partials), partial_shapes)
    t_conv_w, t_conv_b, t_norm_w, t_ln_g, t_ln_b, t_scale, t_bias, t_alog, t_dd, t_loss = tot
    loss = jnp.sum(t_loss)

    def mine(a, per):
        return lax.dynamic_slice_in_dim(a, me * per, per, axis=a.ndim - 1)

    small_names = ["ssm_conv_w", "ssm_conv_b", "ssm_norm_w", "ln_g", "ln_b", "pool_scale", "ssm_dt_bias", "ssm_a_log",
                   "ssm_d"]
    small_w = [ssm_conv_w, ssm_conv_b, ssm_norm_w, ln_g, ln_b, pool_scale, ssm_dt_bias, ssm_a_log, ssm_d]
    small_m = [m_ssm_conv_w, m_ssm_conv_b, m_ssm_norm_w, m_ln_g, m_ln_b, m_pool_scale, m_ssm_dt_bias, m_ssm_a_log,
               m_ssm_d]
    small_v = [v_ssm_conv_w, v_ssm_conv_b, v_ssm_norm_w, v_ln_g, v_ln_b, v_pool_scale, v_ssm_dt_bias, v_ssm_a_log,
               v_ssm_d]
    small_grads = [mine(t_conv_w, cd_s), mine(t_conv_b, cd_s), mine(t_norm_w, di_s), mine(t_ln_g, d_s),
                   mine(t_ln_b, d_s), t_scale, t_bias, t_alog, t_dd]
    shapes = [w.shape for w in small_w]
    pk = [_pack(group, 8 * V7X_LANES)[None] for group in (small_w, small_m, small_v, small_grads)]
    res = _adamw("adamw_small", pk[0], pk[1], pk[2], [(pk[3], 0)], 0)
    upd = {}
    for name, vals in zip(small_names, zip(*[_unpack(r, shapes) for r in res])):
        upd[name] = list(vals)

    q = collect("l0_ple", (res[0],))
    r_gate = update("ple_gate_w", ple_gate_w, m_ple_gate_w, v_ple_gate_w, q["gate"], 0, r_gate)
    r_plew = update("ple_w", ple_w, m_ple_w, v_ple_w, q["plew"], 0, r_plew)
    r_w2 = update("mlp_w2", mlp_w2, m_mlp_w2, v_mlp_w2, collect("l0_w2", (r_gate[0],))["w2"], 0, r_w2)
    r_w1 = update("mlp_w1", mlp_w1, m_mlp_w1, v_mlp_w1, collect("l0_w1", (r_w2[0],))["w1"], 0, r_w1)
    r_pool = update("pool_w", pool_w, m_pool_w, v_pool_w, collect("pool", (r_w1[0],))["pool"], 0)
    assert not scattering
    large = {"pool_w": (pool_w, r_pool), "ssm_in_w": (in_wt, r_in), "ssm_out_w": (ssm_out_w, r_out),
             "mlp_w1": (mlp_w1, r_w1), "mlp_w2": (mlp_w2, r_w2), "ple_w": (ple_w, r_plew),
             "ple_gate_w": (ple_gate_w, r_gate)}
    for name, (w, rs) in large.items():
        upd[name] = [r.reshape(w.shape) for r in rs]
    upd["ssm_in_w"] = [jnp.swapaxes(r, 1, 2) for r in upd["ssm_in_w"]]

    order = ["pool_w", "pool_scale", "ssm_in_w", "ssm_conv_w", "ssm_conv_b", "ssm_dt_bias", "ssm_a_log", "ssm_d",
             "ssm_norm_w", "ssm_out_w", "mlp_w1", "mlp_w2", "ln_g", "ln_b", "ple_w", "ple_gate_w"]
    out = [loss, grad_x[None]]
    for k in range(4):
        out += [upd[name][k] for name in order]
    return tuple(out)
```
